```python
import math
import jax, jax.numpy as jnp
from jax import lax
import numpy as np

D_MODEL = 1024
BATCH = 16
SEQ = 2048
DEPTH = 2

N_A = max(1, DEPTH // 2)
N_B = DEPTH - N_A
N_HEADS = 16
HEAD_DIM = D_MODEL // N_HEADS
D_FF = 2816
CONV_WIDTH = 3
Q_BLOCK = 128
RMS_EPS = 1e-6

kernel_name = "yoco_shortconv_fox_macaron_sandwich"


def rms_norm(x, g):
    xf = x.astype(jnp.float32)
    y = xf * lax.rsqrt(jnp.mean(xf * xf, axis=-1, keepdims=True) + RMS_EPS)
    return (y * g.astype(jnp.float32)).astype(x.dtype)


def swiglu(x, w_in, w_out):
    gate, up = jnp.split(x @ w_in, 2, axis=-1)
    return (jax.nn.silu(gate) * up) @ w_out


def short_conv_mixer(x, w_in, conv_k, w_out):
    b_gate, c_gate, h = jnp.split(x @ w_in, 3, axis=-1)
    u = c_gate * h
    y = lax.conv_general_dilated(
        u, conv_k[:, None, :].astype(u.dtype),
        window_strides=(1,), padding=[(CONV_WIDTH - 1, 0)],
        dimension_numbers=("NWC", "WIO", "NWC"),
        feature_group_count=D_MODEL)
    return (b_gate * y) @ w_out


def shared_kv(x, kv_g, kv_w, forget_b):
    bsz, seq, _ = x.shape
    p = rms_norm(x, kv_g) @ kv_w
    k = p[..., :D_MODEL].reshape(bsz, seq, N_HEADS, HEAD_DIM).transpose(0, 2, 1, 3)
    v = p[..., D_MODEL:2 * D_MODEL].reshape(bsz, seq, N_HEADS, HEAD_DIM).transpose(0, 2, 1, 3)
    f_logit = (p[..., 2 * D_MODEL:] + forget_b).astype(jnp.float32)
    log_f = jax.nn.log_sigmoid(f_logit)
    c = jnp.cumsum(log_f, axis=1).transpose(0, 2, 1)
    return k, v, c


def forgetting_attention(x, k, v, c, w_qg, w_o):
    bsz, seq, _ = x.shape
    n_blk = seq // Q_BLOCK
    q, gate = jnp.split(x @ w_qg, 2, axis=-1)
    q = q.reshape(bsz, seq, N_HEADS, HEAD_DIM).transpose(0, 2, 1, 3)
    q_blocks = q.reshape(bsz, N_HEADS, n_blk, Q_BLOCK, HEAD_DIM).transpose(2, 0, 1, 3, 4)
    c_blocks = c.reshape(bsz, N_HEADS, n_blk, Q_BLOCK).transpose(2, 0, 1, 3)
    k_pos = jnp.arange(seq)
    scale = 1.0 / math.sqrt(HEAD_DIM)

    def attend_block(args):
        qb, cb, i = args
        s = jnp.einsum("bhqd,bhkd->bhqk", qb, k, preferred_element_type=jnp.float32) * scale
        s = s + cb[..., None] - c[:, :, None, :]
        q_pos = i * Q_BLOCK + jnp.arange(Q_BLOCK)
        s = jnp.where(k_pos[None, :] <= q_pos[:, None], s, -jnp.inf)
        p = jax.nn.softmax(s, axis=-1)
        return jnp.einsum("bhqk,bhkd->bhqd", p.astype(v.dtype), v)

    o = lax.map(attend_block, (q_blocks, c_blocks, jnp.arange(n_blk)))
    o = o.transpose(1, 0, 3, 2, 4).reshape(bsz, seq, D_MODEL)
    return (jax.nn.sigmoid(gate) * o) @ w_o


def _fwd_setup_inputs(seed: int = 0) -> dict:
    key = jax.random.key(seed)
    ks = jax.random.split(key, 24)
    f32 = jnp.float32

    def w(k, shape, fan_in):
        return jax.random.normal(k, shape, f32) * fan_in ** -0.5

    def gain(k, shape):
        return 1.0 + 0.05 * jax.random.normal(k, shape, f32)

    return {
        "x": jax.random.normal(ks[0], (BATCH, SEQ, D_MODEL), f32),
        "ffn1_pre_g": gain(ks[1], (DEPTH, D_MODEL)),
        "ffn1_post_g": gain(ks[2], (DEPTH, D_MODEL)),
        "ffn1_w_in": w(ks[3], (DEPTH, D_MODEL, 2 * D_FF), D_MODEL),
        "ffn1_w_out": w(ks[4], (DEPTH, D_FF, D_MODEL), D_FF),
        "mix_pre_g": gain(ks[5], (DEPTH, D_MODEL)),
        "mix_post_g": gain(ks[6], (DEPTH, D_MODEL)),
        "ffn2_pre_g": gain(ks[7], (DEPTH, D_MODEL)),
        "ffn2_post_g": gain(ks[8], (DEPTH, D_MODEL)),
        "ffn2_w_in": w(ks[9], (DEPTH, D_MODEL, 2 * D_FF), D_MODEL),
        "ffn2_w_out": w(ks[10], (DEPTH, D_FF, D_MODEL), D_FF),
        "conv_w_in": w(ks[11], (N_A, D_MODEL, 3 * D_MODEL), D_MODEL),
        "conv_k": w(ks[12], (N_A, CONV_WIDTH, D_MODEL), CONV_WIDTH),
        "conv_w_out": w(ks[13], (N_A, D_MODEL, D_MODEL), D_MODEL),
        "kv_g": gain(ks[14], (D_MODEL,)),
        "kv_w": w(ks[15], (D_MODEL, 2 * D_MODEL + N_HEADS), D_MODEL),
        "forget_b": jax.random.uniform(ks[16], (N_HEADS,), f32, 1.0, 3.0),
        "attn_w_qg": w(ks[17], (N_B, D_MODEL, 2 * D_MODEL), D_MODEL),
        "attn_w_o": w(ks[18], (N_B, D_MODEL, D_MODEL), D_MODEL),
    }


def _fwd_reference(x, ffn1_pre_g, ffn1_post_g, ffn1_w_in, ffn1_w_out, mix_pre_g, mix_post_g,
              ffn2_pre_g, ffn2_post_g, ffn2_w_in, ffn2_w_out, conv_w_in, conv_k, conv_w_out,
              kv_g, kv_w, forget_b, attn_w_qg, attn_w_o):
    k = v = c = None
    for l in range(DEPTH):
        if l == N_A:
            k, v, c = shared_kv(x, kv_g, kv_w, forget_b)
        h = swiglu(rms_norm(x, ffn1_pre_g[l]), ffn1_w_in[l], ffn1_w_out[l])
        x = x + 0.5 * rms_norm(h, ffn1_post_g[l])
        xn = rms_norm(x, mix_pre_g[l])
        if l < N_A:
            m = short_conv_mixer(xn, conv_w_in[l], conv_k[l], conv_w_out[l])
        else:
            j = l - N_A
            m = forgetting_attention(xn, k, v, c, attn_w_qg[j], attn_w_o[j])
        x = x + rms_norm(m, mix_post_g[l])
        h = swiglu(rms_norm(x, ffn2_pre_g[l]), ffn2_w_in[l], ffn2_w_out[l])
        x = x + 0.5 * rms_norm(h, ffn2_post_g[l])
    return x


import jax as _jax
import jax.numpy as _jnp

TWIN_FORMAT = 'train_step'
FWD_PARAMS = ['x', 'ffn1_pre_g', 'ffn1_post_g', 'ffn1_w_in', 'ffn1_w_out', 'mix_pre_g', 'mix_post_g', 'ffn2_pre_g', 'ffn2_post_g', 'ffn2_w_in', 'ffn2_w_out', 'conv_w_in', 'conv_k', 'conv_w_out', 'kv_g', 'kv_w', 'forget_b', 'attn_w_qg', 'attn_w_o']
TWIN_WEIGHTS = ['ffn1_pre_g', 'ffn1_post_g', 'ffn1_w_in', 'ffn1_w_out', 'mix_pre_g', 'mix_post_g', 'ffn2_pre_g', 'ffn2_post_g', 'ffn2_w_in', 'ffn2_w_out', 'conv_w_in', 'conv_k', 'conv_w_out', 'kv_g', 'kv_w', 'forget_b', 'attn_w_qg', 'attn_w_o']
TWIN_DIFF_INPUT = 'x'
TWIN_INPUTS = ['x', 'ffn1_pre_g', 'ffn1_post_g', 'ffn1_w_in', 'ffn1_w_out', 'mix_pre_g', 'mix_post_g', 'ffn2_pre_g', 'ffn2_post_g', 'ffn2_w_in', 'ffn2_w_out', 'conv_w_in', 'conv_k', 'conv_w_out', 'kv_g', 'kv_w', 'forget_b', 'attn_w_qg', 'attn_w_o', 'loss_target', 'm_ffn1_pre_g', 'm_ffn1_post_g', 'm_ffn1_w_in', 'm_ffn1_w_out', 'm_mix_pre_g', 'm_mix_post_g', 'm_ffn2_pre_g', 'm_ffn2_post_g', 'm_ffn2_w_in', 'm_ffn2_w_out', 'm_conv_w_in', 'm_conv_k', 'm_conv_w_out', 'm_kv_g', 'm_kv_w', 'm_forget_b', 'm_attn_w_qg', 'm_attn_w_o', 'v_ffn1_pre_g', 'v_ffn1_post_g', 'v_ffn1_w_in', 'v_ffn1_w_out', 'v_mix_pre_g', 'v_mix_post_g', 'v_ffn2_pre_g', 'v_ffn2_post_g', 'v_ffn2_w_in', 'v_ffn2_w_out', 'v_conv_w_in', 'v_conv_k', 'v_conv_w_out', 'v_kv_g', 'v_kv_w', 'v_forget_b', 'v_attn_w_qg', 'v_attn_w_o']
TWIN_OUTPUTS = ['loss', 'grad_x', 'grad_ffn1_pre_g', 'grad_ffn1_post_g', 'grad_ffn1_w_in', 'grad_ffn1_w_out', 'grad_mix_pre_g', 'grad_mix_post_g', 'grad_ffn2_pre_g', 'grad_ffn2_post_g', 'grad_ffn2_w_in', 'grad_ffn2_w_out', 'grad_conv_w_in', 'grad_conv_k', 'grad_conv_w_out', 'grad_kv_g', 'grad_kv_w', 'grad_forget_b', 'grad_attn_w_qg', 'grad_attn_w_o', 'delta_ffn1_pre_g', 'delta_ffn1_post_g', 'delta_ffn1_w_in', 'delta_ffn1_w_out', 'delta_mix_pre_g', 'delta_mix_post_g', 'delta_ffn2_pre_g', 'delta_ffn2_post_g', 'delta_ffn2_w_in', 'delta_ffn2_w_out', 'delta_conv_w_in', 'delta_conv_k', 'delta_conv_w_out', 'delta_kv_g', 'delta_kv_w', 'delta_forget_b', 'delta_attn_w_qg', 'delta_attn_w_o', 'new_m_ffn1_pre_g', 'new_m_ffn1_post_g', 'new_m_ffn1_w_in', 'new_m_ffn1_w_out', 'new_m_mix_pre_g', 'new_m_mix_post_g', 'new_m_ffn2_pre_g', 'new_m_ffn2_post_g', 'new_m_ffn2_w_in', 'new_m_ffn2_w_out', 'new_m_conv_w_in', 'new_m_conv_k', 'new_m_conv_w_out', 'new_m_kv_g', 'new_m_kv_w', 'new_m_forget_b', 'new_m_attn_w_qg', 'new_m_attn_w_o', 'new_v_ffn1_pre_g', 'new_v_ffn1_post_g', 'new_v_ffn1_w_in', 'new_v_ffn1_w_out', 'new_v_mix_pre_g', 'new_v_mix_post_g', 'new_v_ffn2_pre_g', 'new_v_ffn2_post_g', 'new_v_ffn2_w_in', 'new_v_ffn2_w_out', 'new_v_conv_w_in', 'new_v_conv_k', 'new_v_conv_w_out', 'new_v_kv_g', 'new_v_kv_w', 'new_v_forget_b', 'new_v_attn_w_qg', 'new_v_attn_w_o']
TWIN_LEAF_KINDS = {'loss': 'loss', 'grad_x': 'grad_x', 'grad_ffn1_pre_g': 'grad_w', 'grad_ffn1_post_g': 'grad_w', 'grad_ffn1_w_in': 'grad_w', 'grad_ffn1_w_out': 'grad_w', 'grad_mix_pre_g': 'grad_w', 'grad_mix_post_g': 'grad_w', 'grad_ffn2_pre_g': 'grad_w', 'grad_ffn2_post_g': 'grad_w', 'grad_ffn2_w_in': 'grad_w', 'grad_ffn2_w_out': 'grad_w', 'grad_conv_w_in': 'grad_w', 'grad_conv_k': 'grad_w', 'grad_conv_w_out': 'grad_w', 'grad_kv_g': 'grad_w', 'grad_kv_w': 'grad_w', 'grad_forget_b': 'grad_w', 'grad_attn_w_qg': 'grad_w', 'grad_attn_w_o': 'grad_w', 'delta_ffn1_pre_g': 'delta_w', 'delta_ffn1_post_g': 'delta_w', 'delta_ffn1_w_in': 'delta_w', 'delta_ffn1_w_out': 'delta_w', 'delta_mix_pre_g': 'delta_w', 'delta_mix_post_g': 'delta_w', 'delta_ffn2_pre_g': 'delta_w', 'delta_ffn2_post_g': 'delta_w', 'delta_ffn2_w_in': 'delta_w', 'delta_ffn2_w_out': 'delta_w', 'delta_conv_w_in': 'delta_w', 'delta_conv_k': 'delta_w', 'delta_conv_w_out': 'delta_w', 'delta_kv_g': 'delta_w', 'delta_kv_w': 'delta_w', 'delta_forget_b': 'delta_w', 'delta_attn_w_qg': 'delta_w', 'delta_attn_w_o': 'delta_w', 'new_m_ffn1_pre_g': 'new_m', 'new_m_ffn1_post_g': 'new_m', 'new_m_ffn1_w_in': 'new_m', 'new_m_ffn1_w_out': 'new_m', 'new_m_mix_pre_g': 'new_m', 'new_m_mix_post_g': 'new_m', 'new_m_ffn2_pre_g': 'new_m', 'new_m_ffn2_post_g': 'new_m', 'new_m_ffn2_w_in': 'new_m', 'new_m_ffn2_w_out': 'new_m', 'new_m_conv_w_in': 'new_m', 'new_m_conv_k': 'new_m', 'new_m_conv_w_out': 'new_m', 'new_m_kv_g': 'new_m', 'new_m_kv_w': 'new_m', 'new_m_forget_b': 'new_m', 'new_m_attn_w_qg': 'new_m', 'new_m_attn_w_o': 'new_m', 'new_v_ffn1_pre_g': 'new_v', 'new_v_ffn1_post_g': 'new_v', 'new_v_ffn1_w_in': 'new_v', 'new_v_ffn1_w_out': 'new_v', 'new_v_mix_pre_g': 'new_v', 'new_v_mix_post_g': 'new_v', 'new_v_ffn2_pre_g': 'new_v', 'new_v_ffn2_post_g': 'new_v', 'new_v_ffn2_w_in': 'new_v', 'new_v_ffn2_w_out': 'new_v', 'new_v_conv_w_in': 'new_v', 'new_v_conv_k': 'new_v', 'new_v_conv_w_out': 'new_v', 'new_v_kv_g': 'new_v', 'new_v_kv_w': 'new_v', 'new_v_forget_b': 'new_v', 'new_v_attn_w_qg': 'new_v', 'new_v_attn_w_o': 'new_v'}


def _forward(args):
    return _fwd_reference(*[args[k] for k in FWD_PARAMS])


def _output_shape():
    out = _jax.eval_shape(lambda: _forward(_fwd_setup_inputs(0)))
    return out.shape, out.dtype

N_MICROBATCH = 1
ADAM_LR = 0.001
ADAM_B1 = 0.9
ADAM_B2 = 0.999
ADAM_EPS = 1e-08
ADAM_WD = 0.01
ADAM_STEP = 10
PER_EXAMPLE_BATCH_AXIS = {'x': 0, 'loss_target': 0}
SHARED_INPUTS = []
_WEIGHT_DTYPES = {'ffn1_pre_g': _jnp.float32, 'ffn1_post_g': _jnp.float32, 'ffn1_w_in': _jnp.float32, 'ffn1_w_out': _jnp.float32, 'mix_pre_g': _jnp.float32, 'mix_post_g': _jnp.float32, 'ffn2_pre_g': _jnp.float32, 'ffn2_post_g': _jnp.float32, 'ffn2_w_in': _jnp.float32, 'ffn2_w_out': _jnp.float32, 'conv_w_in': _jnp.float32, 'conv_k': _jnp.float32, 'conv_w_out': _jnp.float32, 'kv_g': _jnp.float32, 'kv_w': _jnp.float32, 'forget_b': _jnp.float32, 'attn_w_qg': _jnp.float32, 'attn_w_o': _jnp.float32}
MOMENT_SCALE = {'ffn1_pre_g': 7.106312e-01, 'ffn1_post_g': 7.721366e+00, 'ffn1_w_in': 2.840035e-01, 'ffn1_w_out': 4.726127e-01, 'mix_pre_g': 8.240872e-01, 'mix_post_g': 3.198986e+01, 'ffn2_pre_g': 4.382752e-01, 'ffn2_post_g': 7.892493e+00, 'ffn2_w_in': 1.751395e-01, 'ffn2_w_out': 3.215936e-01, 'conv_w_in': 6.427400e-01, 'conv_k': 6.899735e-01, 'conv_w_out': 7.141220e-01, 'kv_g': 6.422178e-01, 'kv_w': 4.506594e-01, 'forget_b': 3.576714e+00, 'attn_w_qg': 2.373755e-01, 'attn_w_o': 5.774143e-01}


def _to_microbatches(a, axis):
    t = _jnp.moveaxis(a, axis, 0)
    t = t.reshape((N_MICROBATCH, t.shape[0] // N_MICROBATCH) + t.shape[1:])
    return _jnp.moveaxis(t, 1, axis + 1)


def setup_inputs(seed: int = 0) -> dict:
    inp = _fwd_setup_inputs(seed)
    key = _jax.random.fold_in(_jax.random.key(seed), 7919)
    shape, _ = _output_shape()
    out = dict(inp)
    out["loss_target"] = _jax.random.normal(_jax.random.fold_in(key, 0), shape, _jnp.float32)
    for i, name in enumerate(TWIN_WEIGHTS):
        w = inp[name].astype(_jnp.float32)
        if MOMENT_SCALE is None:
            s = _jnp.sqrt(_jnp.mean(_jnp.square(w)) + 1e-30)
        else:
            s = MOMENT_SCALE[name]
        km, kv = _jax.random.split(_jax.random.fold_in(key, i + 1))
        out[name] = w
        out["m_" + name] = s * _jax.random.normal(km, w.shape, _jnp.float32)
        out["v_" + name] = (s * s) * _jax.random.uniform(kv, w.shape, _jnp.float32, 0.5, 1.5)
    if N_MICROBATCH > 1:
        for name, axis in PER_EXAMPLE_BATCH_AXIS.items():
            out[name] = _to_microbatches(out[name], axis)
    return {'x': out['x'], 'ffn1_pre_g': out['ffn1_pre_g'], 'ffn1_post_g': out['ffn1_post_g'], 'ffn1_w_in': out['ffn1_w_in'], 'ffn1_w_out': out['ffn1_w_out'], 'mix_pre_g': out['mix_pre_g'], 'mix_post_g': out['mix_post_g'], 'ffn2_pre_g': out['ffn2_pre_g'], 'ffn2_post_g': out['ffn2_post_g'], 'ffn2_w_in': out['ffn2_w_in'], 'ffn2_w_out': out['ffn2_w_out'], 'conv_w_in': out['conv_w_in'], 'conv_k': out['conv_k'], 'conv_w_out': out['conv_w_out'], 'kv_g': out['kv_g'], 'kv_w': out['kv_w'], 'forget_b': out['forget_b'], 'attn_w_qg': out['attn_w_qg'], 'attn_w_o': out['attn_w_o'], 'loss_target': out['loss_target'], 'm_ffn1_pre_g': out['m_ffn1_pre_g'], 'm_ffn1_post_g': out['m_ffn1_post_g'], 'm_ffn1_w_in': out['m_ffn1_w_in'], 'm_ffn1_w_out': out['m_ffn1_w_out'], 'm_mix_pre_g': out['m_mix_pre_g'], 'm_mix_post_g': out['m_mix_post_g'], 'm_ffn2_pre_g': out['m_ffn2_pre_g'], 'm_ffn2_post_g': out['m_ffn2_post_g'], 'm_ffn2_w_in': out['m_ffn2_w_in'], 'm_ffn2_w_out': out['m_ffn2_w_out'], 'm_conv_w_in': out['m_conv_w_in'], 'm_conv_k': out['m_conv_k'], 'm_conv_w_out': out['m_conv_w_out'], 'm_kv_g': out['m_kv_g'], 'm_kv_w': out['m_kv_w'], 'm_forget_b': out['m_forget_b'], 'm_attn_w_qg': out['m_attn_w_qg'], 'm_attn_w_o': out['m_attn_w_o'], 'v_ffn1_pre_g': out['v_ffn1_pre_g'], 'v_ffn1_post_g': out['v_ffn1_post_g'], 'v_ffn1_w_in': out['v_ffn1_w_in'], 'v_ffn1_w_out': out['v_ffn1_w_out'], 'v_mix_pre_g': out['v_mix_pre_g'], 'v_mix_post_g': out['v_mix_post_g'], 'v_ffn2_pre_g': out['v_ffn2_pre_g'], 'v_ffn2_post_g': out['v_ffn2_post_g'], 'v_ffn2_w_in': out['v_ffn2_w_in'], 'v_ffn2_w_out': out['v_ffn2_w_out'], 'v_conv_w_in': out['v_conv_w_in'], 'v_conv_k': out['v_conv_k'], 'v_conv_w_out': out['v_conv_w_out'], 'v_kv_g': out['v_kv_g'], 'v_kv_w': out['v_kv_w'], 'v_forget_b': out['v_forget_b'], 'v_attn_w_qg': out['v_attn_w_qg'], 'v_attn_w_o': out['v_attn_w_o']}


def _loss(weights, diff, rest, loss_target):
    with _jax.named_scope("forward"):
        args = {**rest, TWIN_DIFF_INPUT: diff, **{k: w.astype(_WEIGHT_DTYPES[k]) for k, w in weights.items()}}
        y = _forward(args)
    with _jax.named_scope("loss_head"):
        err = _jnp.square(y.astype(_jnp.float32) - loss_target)
        return 0.5 * _jnp.sum(_jnp.mean(err, axis=-1)) if err.ndim else 0.5 * err


def _adamw(w, g, m, v):
    m = ADAM_B1 * m + (1.0 - ADAM_B1) * g
    v = ADAM_B2 * v + (1.0 - ADAM_B2) * _jnp.square(g)
    m_hat = m / (1.0 - ADAM_B1 ** ADAM_STEP)
    v_hat = v / (1.0 - ADAM_B2 ** ADAM_STEP)
    delta = -ADAM_LR * (m_hat / (_jnp.sqrt(v_hat) + ADAM_EPS) + ADAM_WD * w)
    return delta, m, v


def reference(x, ffn1_pre_g, ffn1_post_g, ffn1_w_in, ffn1_w_out, mix_pre_g, mix_post_g, ffn2_pre_g, ffn2_post_g, ffn2_w_in, ffn2_w_out, conv_w_in, conv_k, conv_w_out, kv_g, kv_w, forget_b, attn_w_qg, attn_w_o, loss_target, m_ffn1_pre_g, m_ffn1_post_g, m_ffn1_w_in, m_ffn1_w_out, m_mix_pre_g, m_mix_post_g, m_ffn2_pre_g, m_ffn2_post_g, m_ffn2_w_in, m_ffn2_w_out, m_conv_w_in, m_conv_k, m_conv_w_out, m_kv_g, m_kv_w, m_forget_b, m_attn_w_qg, m_attn_w_o, v_ffn1_pre_g, v_ffn1_post_g, v_ffn1_w_in, v_ffn1_w_out, v_mix_pre_g, v_mix_post_g, v_ffn2_pre_g, v_ffn2_post_g, v_ffn2_w_in, v_ffn2_w_out, v_conv_w_in, v_conv_k, v_conv_w_out, v_kv_g, v_kv_w, v_forget_b, v_attn_w_qg, v_attn_w_o):
    given = dict(x=x, ffn1_pre_g=ffn1_pre_g, ffn1_post_g=ffn1_post_g, ffn1_w_in=ffn1_w_in, ffn1_w_out=ffn1_w_out, mix_pre_g=mix_pre_g, mix_post_g=mix_post_g, ffn2_pre_g=ffn2_pre_g, ffn2_post_g=ffn2_post_g, ffn2_w_in=ffn2_w_in, ffn2_w_out=ffn2_w_out, conv_w_in=conv_w_in, conv_k=conv_k, conv_w_out=conv_w_out, kv_g=kv_g, kv_w=kv_w, forget_b=forget_b, attn_w_qg=attn_w_qg, attn_w_o=attn_w_o, loss_target=loss_target, m_ffn1_pre_g=m_ffn1_pre_g, m_ffn1_post_g=m_ffn1_post_g, m_ffn1_w_in=m_ffn1_w_in, m_ffn1_w_out=m_ffn1_w_out, m_mix_pre_g=m_mix_pre_g, m_mix_post_g=m_mix_post_g, m_ffn2_pre_g=m_ffn2_pre_g, m_ffn2_post_g=m_ffn2_post_g, m_ffn2_w_in=m_ffn2_w_in, m_ffn2_w_out=m_ffn2_w_out, m_conv_w_in=m_conv_w_in, m_conv_k=m_conv_k, m_conv_w_out=m_conv_w_out, m_kv_g=m_kv_g, m_kv_w=m_kv_w, m_forget_b=m_forget_b, m_attn_w_qg=m_attn_w_qg, m_attn_w_o=m_attn_w_o, v_ffn1_pre_g=v_ffn1_pre_g, v_ffn1_post_g=v_ffn1_post_g, v_ffn1_w_in=v_ffn1_w_in, v_ffn1_w_out=v_ffn1_w_out, v_mix_pre_g=v_mix_pre_g, v_mix_post_g=v_mix_post_g, v_ffn2_pre_g=v_ffn2_pre_g, v_ffn2_post_g=v_ffn2_post_g, v_ffn2_w_in=v_ffn2_w_in, v_ffn2_w_out=v_ffn2_w_out, v_conv_w_in=v_conv_w_in, v_conv_k=v_conv_k, v_conv_w_out=v_conv_w_out, v_kv_g=v_kv_g, v_kv_w=v_kv_w, v_forget_b=v_forget_b, v_attn_w_qg=v_attn_w_qg, v_attn_w_o=v_attn_w_o)
    weights = {n: given[n] for n in TWIN_WEIGHTS}
    shared = {n: given[n] for n in SHARED_INPUTS}
    per_example = {n: given[n] for n in ['x']}
    grad_fn = _jax.value_and_grad(_loss, argnums=(0, 1))

    def one_microbatch(ex, loss_target):
        ex = dict(ex)
        diff = ex.pop(TWIN_DIFF_INPUT)
        return grad_fn(weights, diff, {**shared, **ex}, loss_target)

    if N_MICROBATCH == 1:
        loss, (grad_w, grad_x) = one_microbatch(per_example, given["loss_target"])
    else:
        def body(carry, xs):
            loss_sum, grad_sum = carry
            l_k, (gw_k, gx_k) = one_microbatch(xs[0], xs[1])
            with _jax.named_scope("update"):
                return (loss_sum + l_k, _jax.tree.map(_jnp.add, grad_sum, gw_k)), gx_k

        init = (_jnp.zeros((), _jnp.float32), _jax.tree.map(_jnp.zeros_like, weights))
        (loss, grad_w), grad_x = _jax.lax.scan(body, init, (per_example, given["loss_target"]))
    with _jax.named_scope("update"):
        delta_w, new_m, new_v = {}, {}, {}
        for n in TWIN_WEIGHTS:
            delta_w[n], new_m[n], new_v[n] = _adamw(weights[n], grad_w[n], given["m_" + n], given["v_" + n])
    return (loss, grad_x, *[grad_w[n] for n in TWIN_WEIGHTS], *[delta_w[n] for n in TWIN_WEIGHTS],
            *[new_m[n] for n in TWIN_WEIGHTS], *[new_v[n] for n in TWIN_WEIGHTS])
```

```python
import functools
import math

import jax
import jax.numpy as jnp
from jax import lax
from jax.experimental import pallas as pl
from jax.experimental.pallas import tpu as pltpu

F32 = jnp.float32
MM_DTYPE = jnp.bfloat16
WIRE_DTYPE = jnp.bfloat16

RMS_EPS = 1e-6
ADAM_LR = 0.001
ADAM_B1 = 0.9
ADAM_B2 = 0.999
ADAM_EPS = 1e-08
ADAM_WD = 0.01
ADAM_STEP = 10

HEAD_DIM = 64
LANES = 128
N_CHIP = 4
N_DEV = 8
ROW_TILE = 256
MM_TILE = 512
ATT_BLOCK = 256
SMALL_ROWS = 24
VMEM_LIMIT = 56 * 1024 * 1024
MESH = pl.DeviceIdType.MESH
ANY = pl.BlockSpec(memory_space=pl.ANY)

NT = (((1,), (1,)), ((), ()))
TN = (((0,), (0,)), ((), ()))


def _tile(n, pref):
    if n <= pref:
        return n
    t = pref - pref % 16
    while n % t:
        t -= 16
    return t


def _params():
    return pltpu.CompilerParams(vmem_limit_bytes=VMEM_LIMIT)


def _sds(shape, dtype):
    return jax.ShapeDtypeStruct(shape, dtype)


def _rows(tm, c):
    return pl.BlockSpec((tm, c), lambda i: (i, 0))


def _whole(shape):
    return pl.BlockSpec(shape, lambda *_: (0,) * len(shape))


def _rms_fwd(x, g, tag):
    T, D = x.shape
    tm = _tile(T, ROW_TILE)

    def body(x_ref, g_ref, o_ref):
        xv = x_ref[...]
        r = lax.rsqrt(jnp.mean(xv * xv, axis=-1, keepdims=True) + RMS_EPS)
        o_ref[...] = (xv * r * g_ref[...]).astype(o_ref.dtype)

    return pl.pallas_call(
        body, name=f"rms_fwd_{tag}", grid=(T // tm,),
        in_specs=[_rows(tm, D), _whole((1, D))], out_specs=_rows(tm, D),
        out_shape=_sds((T, D), MM_DTYPE), compiler_params=_params())(x, g.reshape(1, D))


def _post_fwd(x, h, g, alpha, tag):
    T, D = x.shape
    tm = _tile(T, ROW_TILE)

    def body(x_ref, h_ref, g_ref, o_ref):
        hv = h_ref[...]
        r = lax.rsqrt(jnp.mean(hv * hv, axis=-1, keepdims=True) + RMS_EPS)
        o_ref[...] = x_ref[...] + alpha * (hv * r * g_ref[...])

    return pl.pallas_call(
        body, name=f"post_fwd_{tag}", grid=(T // tm,),
        in_specs=[_rows(tm, D), _rows(tm, D), _whole((1, D))], out_specs=_rows(tm, D),
        out_shape=_sds((T, D), F32), compiler_params=_params())(x, h, g.reshape(1, D))


def _accumulate(ref, part, first):
    @pl.when(first)
    def _():
        ref[...] = part

    @pl.when(jnp.logical_not(first))
    def _():
        ref[...] += part


def _post_bwd(dx, h, g, alpha, tag):
    T, D = dx.shape
    tm = _tile(T, ROW_TILE)

    def body(dx_ref, h_ref, g_ref, dh_ref, dg_ref):
        hv = h_ref[...]
        r = lax.rsqrt(jnp.mean(hv * hv, axis=-1, keepdims=True) + RMS_EPS)
        hh = hv * r
        dyn = alpha * dx_ref[...]
        _accumulate(dg_ref, jnp.sum(dyn * hh, axis=0, keepdims=True), pl.program_id(0) == 0)
        dhh = dyn * g_ref[...]
        dh = r * (dhh - hh * jnp.mean(dhh * hh, axis=-1, keepdims=True))
        dh_ref[...] = dh.astype(dh_ref.dtype)

    return pl.pallas_call(
        body, name=f"post_bwd_{tag}", grid=(T // tm,),
        in_specs=[_rows(tm, D), _rows(tm, D), _whole((1, D))],
        out_specs=[_rows(tm, D), _whole((1, D))],
        out_shape=[_sds((T, D), MM_DTYPE), _sds((1, D), F32)],
        compiler_params=_params())(dx, h, g.reshape(1, D))


def _pre_bwd(dres, dxn, x, g, tag):
    T, D = x.shape
    tm = _tile(T, ROW_TILE)

    def body(dres_ref, dxn_ref, x_ref, g_ref, dx_ref, dg_ref):
        xv = x_ref[...]
        r = lax.rsqrt(jnp.mean(xv * xv, axis=-1, keepdims=True) + RMS_EPS)
        xh = xv * r
        dn = dxn_ref[...]
        _accumulate(dg_ref, jnp.sum(dn * xh, axis=0, keepdims=True), pl.program_id(0) == 0)
        dxh = dn * g_ref[...]
        dx_ref[...] = dres_ref[...] + r * (dxh - xh * jnp.mean(dxh * xh, axis=-1, keepdims=True))

    return pl.pallas_call(
        body, name=f"pre_bwd_{tag}", grid=(T // tm,),
        in_specs=[_rows(tm, D), _rows(tm, D), _rows(tm, D), _whole((1, D))],
        out_specs=[_rows(tm, D), _whole((1, D))],
        out_shape=[_sds((T, D), F32), _sds((1, D), F32)],
        compiler_params=_params())(dres, dxn, x, g.reshape(1, D))


def _swiglu_fwd(hgu, tag):
    T, F2 = hgu.shape
    F = F2 // 2
    tm = _tile(T, ROW_TILE)

    def body(g_ref, u_ref, o_ref):
        g = g_ref[...].astype(F32)
        o_ref[...] = (g * jax.nn.sigmoid(g) * u_ref[...].astype(F32)).astype(o_ref.dtype)

    return pl.pallas_call(
        body, name=f"swiglu_fwd_{tag}", grid=(T // tm,),
        in_specs=[pl.BlockSpec((tm, F), lambda i: (i, 0)), pl.BlockSpec((tm, F), lambda i: (i, 1))],
        out_specs=_rows(tm, F), out_shape=_sds((T, F), MM_DTYPE), compiler_params=_params())(hgu, hgu)


def _swiglu_bwd(hgu, da, tag):
    T, F2 = hgu.shape
    F = F2 // 2
    tm = _tile(T, ROW_TILE)

    def body(h_ref, da_ref, o_ref):
        g = h_ref[:, :F].astype(F32)
        u = h_ref[:, F:].astype(F32)
        d = da_ref[...].astype(F32)
        sg = jax.nn.sigmoid(g)
        o_ref[:, :F] = (d * u * sg * (1.0 + g * (1.0 - sg))).astype(o_ref.dtype)
        o_ref[:, F:] = (d * g * sg).astype(o_ref.dtype)

    return pl.pallas_call(
        body, name=f"swiglu_bwd_{tag}", grid=(T // tm,),
        in_specs=[_rows(tm, F2), _rows(tm, F)], out_specs=_rows(tm, F2),
        out_shape=_sds((T, F2), MM_DTYPE), compiler_params=_params())(hgu, da)


def _loss_grad(y, tgt):
    T, D = y.shape
    tm = _tile(T, ROW_TILE)

    def body(y_ref, t_ref, dy_ref, l_ref):
        e = y_ref[...] - t_ref[...]
        row = jnp.mean(e * e, axis=-1, keepdims=True)
        part = jnp.broadcast_to(jnp.sum(row, axis=0, keepdims=True), (8, LANES))
        _accumulate(l_ref, part, pl.program_id(0) == 0)
        dy_ref[...] = e * (1.0 / D)

    dy, lsum = pl.pallas_call(
        body, name="loss_grad", grid=(T // tm,),
        in_specs=[_rows(tm, D), _rows(tm, D)], out_specs=[_rows(tm, D), _whole((8, LANES))],
        out_shape=[_sds((T, D), F32), _sds((8, LANES), F32)], compiler_params=_params())(y, tgt)
    return dy, 0.5 * lsum[0, 0]


def _shift_down(u, d, rows):
    return jnp.where(rows >= d, pltpu.roll(u, d, 0), 0.0)


def _shift_up(u, d, rows, S):
    return jnp.where(rows < S - d, pltpu.roll(u, S - d, 0), 0.0)


def _conv_fwd(bch, k8, Bl, S):
    T, D3 = bch.shape
    D = D3 // 3
    dc = min(D, 2 * LANES)
    nd = D // dc

    def body(b_ref, c_ref, h_ref, k_ref, z_ref):
        rows = lax.broadcasted_iota(jnp.int32, (S, 1), 0)
        u = c_ref[...].astype(F32) * h_ref[...].astype(F32)
        y = k_ref[2:3, :] * u + k_ref[1:2, :] * _shift_down(u, 1, rows) + k_ref[0:1, :] * _shift_down(u, 2, rows)
        z_ref[...] = (b_ref[...].astype(F32) * y).astype(z_ref.dtype)

    return pl.pallas_call(
        body, name="conv_fwd", grid=(Bl, nd),
        in_specs=[pl.BlockSpec((S, dc), lambda b, j: (b, j)),
                  pl.BlockSpec((S, dc), lambda b, j: (b, nd + j)),
                  pl.BlockSpec((S, dc), lambda b, j: (b, 2 * nd + j)),
                  pl.BlockSpec((8, dc), lambda b, j: (0, j))],
        out_specs=pl.BlockSpec((S, dc), lambda b, j: (b, j)),
        out_shape=_sds((T, D), MM_DTYPE), compiler_params=_params())(bch, bch, bch, k8)


def _conv_bwd(bch, dz, k8, Bl, S):
    T, D3 = bch.shape
    D = D3 // 3
    dc = min(D, 2 * LANES)
    nd = D // dc

    def body(b_ref, c_ref, h_ref, dz_ref, k_ref, db_ref, dc_ref, dh_ref, dk_ref):
        rows = lax.broadcasted_iota(jnp.int32, (S, 1), 0)
        bv = b_ref[...].astype(F32)
        cv = c_ref[...].astype(F32)
        hv = h_ref[...].astype(F32)
        dzv = dz_ref[...].astype(F32)
        u = cv * hv
        u1 = _shift_down(u, 1, rows)
        u2 = _shift_down(u, 2, rows)
        y = k_ref[2:3, :] * u + k_ref[1:2, :] * u1 + k_ref[0:1, :] * u2
        db_ref[...] = (dzv * y).astype(db_ref.dtype)
        dy = dzv * bv
        du = k_ref[2:3, :] * dy + k_ref[1:2, :] * _shift_up(dy, 1, rows, S) + k_ref[0:1, :] * _shift_up(dy, 2, rows, S)
        dc_ref[...] = (du * hv).astype(dc_ref.dtype)
        dh_ref[...] = (du * cv).astype(dh_ref.dtype)

        @pl.when(pl.program_id(1) == 0)
        def _():
            dk_ref[...] = jnp.zeros_like(dk_ref)

        dk_ref[0:1, :] += jnp.sum(dy * u2, axis=0, keepdims=True)
        dk_ref[1:2, :] += jnp.sum(dy * u1, axis=0, keepdims=True)
        dk_ref[2:3, :] += jnp.sum(dy * u, axis=0, keepdims=True)

    seq = lambda off: pl.BlockSpec((S, dc), lambda j, b: (b, off + j))
    return pl.pallas_call(
        body, name="conv_bwd", grid=(nd, Bl),
        in_specs=[seq(0), seq(nd), seq(2 * nd), seq(0), pl.BlockSpec((8, dc), lambda j, b: (0, j))],
        out_specs=[seq(0), seq(0), seq(0), pl.BlockSpec((8, dc), lambda j, b: (0, j))],
        out_shape=[_sds((T, D), MM_DTYPE)] * 3 + [_sds((8, D), F32)],
        compiler_params=_params())(bch, bch, bch, dz, k8)


def _forget_fwd(pf, fb, Bl, S):
    T = pf.shape[0]

    def body(p_ref, fb_ref, c_ref):
        rows = lax.broadcasted_iota(jnp.int32, (S, 1), 0)
        z = p_ref[...] + fb_ref[...]
        acc = jnp.minimum(z, 0.0) - jnp.log1p(jnp.exp(-jnp.abs(z)))
        d = 1
        while d < S:
            acc = acc + _shift_down(acc, d, rows)
            d *= 2
        c_ref[...] = acc

    return pl.pallas_call(
        body, name="forget_fwd", grid=(Bl,),
        in_specs=[_rows(S, LANES), _whole((1, LANES))], out_specs=_rows(S, LANES),
        out_shape=_sds((T, LANES), F32), compiler_params=_params())(pf, fb)


def _forget_bwd(dc, pf, fb, Bl, S):
    T = pf.shape[0]

    def body(dc_ref, p_ref, fb_ref, df_ref, dfb_ref):
        rows = lax.broadcasted_iota(jnp.int32, (S, 1), 0)
        acc = dc_ref[...]
        d = 1
        while d < S:
            acc = acc + _shift_up(acc, d, rows, S)
            d *= 2
        df = acc * jax.nn.sigmoid(-(p_ref[...] + fb_ref[...]))
        df_ref[...] = df.astype(df_ref.dtype)
        _accumulate(dfb_ref, jnp.sum(df, axis=0, keepdims=True), pl.program_id(0) == 0)

    return pl.pallas_call(
        body, name="forget_bwd", grid=(Bl,),
        in_specs=[_rows(S, LANES), _rows(S, LANES), _whole((1, LANES))],
        out_specs=[_rows(S, LANES), _whole((1, LANES))],
        out_shape=[_sds((T, LANES), MM_DTYPE), _sds((1, LANES), F32)],
        compiler_params=_params())(dc, pf, fb)


def _head_mask(h):
    lane = lax.broadcasted_iota(jnp.int32, (1, LANES), 1)
    return (lane >= h * HEAD_DIM) & (lane < (h + 1) * HEAD_DIM)


def _attn_fwd(qg, kv, c_col, c_row, Bl, S, D):
    T = Bl * S
    H = D // HEAD_DIM
    HP = D // LANES
    bq = min(S, ATT_BLOCK)
    nq = S // bq
    scale = 1.0 / math.sqrt(HEAD_DIM)

    def body(q_ref, k_ref, v_ref, cc_ref, cr_ref, o_ref, lse_ref):
        i = pl.program_id(2)
        q2 = q_ref[...]
        rows = i * bq + lax.broadcasted_iota(jnp.int32, (bq, 1), 0)
        cols0 = lax.broadcasted_iota(jnp.int32, (1, bq), 1)
        outs = []
        for h in range(2):
            qh = q2 * _head_mask(h).astype(q2.dtype)
            cc = cc_ref[h][:, :1]

            def step(j, carry, qh=qh, cc=cc, h=h):
                m, l, acc = carry
                off = pl.multiple_of(j * bq, bq)
                kj = k_ref[pl.ds(off, bq), :]
                vj = v_ref[pl.ds(off, bq), :]
                s = lax.dot_general(qh, kj, NT, preferred_element_type=F32) * scale
                s = s + cc - cr_ref[h, j]
                s = jnp.where(cols0 + j * bq <= rows, s, -jnp.inf)
                m_new = jnp.maximum(m, jnp.max(s, axis=1, keepdims=True))
                p = jnp.exp(s - m_new)
                a = jnp.exp(m - m_new)
                l = a * l + jnp.sum(p, axis=1, keepdims=True)
                acc = a * acc + jnp.dot(p.astype(MM_DTYPE), vj, preferred_element_type=F32)
                return m_new, l, acc

            init = (jnp.full((bq, 1), -jnp.inf, F32), jnp.zeros((bq, 1), F32), jnp.zeros((bq, LANES), F32))
            m, l, acc = lax.fori_loop(0, i + 1, step, init)
            outs.append(acc / l)
            lse_ref[h] = jnp.broadcast_to(m + jnp.log(l), (bq, LANES))
        o_ref[...] = jnp.where(_head_mask(0), outs[0], outs[1])

    return pl.pallas_call(
        body, name="attn_fwd", grid=(Bl, HP, nq),
        in_specs=[pl.BlockSpec((bq, LANES), lambda b, hp, i: (b * nq + i, hp)),
                  pl.BlockSpec((S, LANES), lambda b, hp, i: (b, hp)),
                  pl.BlockSpec((S, LANES), lambda b, hp, i: (b, HP + hp)),
                  pl.BlockSpec((None, 2, bq, LANES), lambda b, hp, i: (b, hp, i, 0)),
                  pl.BlockSpec((None, 2, nq, 1, bq), lambda b, hp, i: (b, hp, 0, 0, 0))],
        out_specs=[pl.BlockSpec((bq, LANES), lambda b, hp, i: (b * nq + i, hp)),
                   pl.BlockSpec((None, 2, bq, LANES), lambda b, hp, i: (b, hp, i, 0))],
        out_shape=[_sds((T, D), F32), _sds((Bl, H, S, LANES), F32)],
        compiler_params=_params())(qg, kv, kv, c_col, c_row)


def _attn_delta(qg, kv, do, lse, c_col, c_row, Bl, S, D):
    H = D // HEAD_DIM
    HP = D // LANES
    bq = min(S, ATT_BLOCK)
    nq = S // bq
    scale = 1.0 / math.sqrt(HEAD_DIM)

    def body(q_ref, k_ref, v_ref, do_ref, lse_ref, cc_ref, cr_ref, dl_ref):
        i = pl.program_id(2)
        q2 = q_ref[...]
        do2 = do_ref[...]
        rows = i * bq + lax.broadcasted_iota(jnp.int32, (bq, 1), 0)
        cols0 = lax.broadcasted_iota(jnp.int32, (1, bq), 1)
        for h in range(2):
            mh = _head_mask(h)
            qh = q2 * mh.astype(q2.dtype)
            cc = cc_ref[h][:, :1]
            lse_h = lse_ref[h][:, :1]

            def step(j, acc, qh=qh, cc=cc, lse_h=lse_h, mh=mh, h=h):
                off = pl.multiple_of(j * bq, bq)
                kj = k_ref[pl.ds(off, bq), :]
                vj = v_ref[pl.ds(off, bq), :]
                s = lax.dot_general(qh, kj, NT, preferred_element_type=F32) * scale
                s = s + cc - cr_ref[h, j]
                s = jnp.where(cols0 + j * bq <= rows, s, -jnp.inf)
                p = jnp.exp(s - lse_h)
                dp = lax.dot_general(do2, vj * mh.astype(vj.dtype), NT, preferred_element_type=F32)
                return acc + jnp.sum(p * dp, axis=1, keepdims=True)

            acc = lax.fori_loop(0, i + 1, step, jnp.zeros((bq, 1), F32))
            dl_ref[h] = jnp.broadcast_to(acc, (bq, LANES))

    per_head = pl.BlockSpec((None, 2, bq, LANES), lambda b, hp, i: (b, hp, i, 0))
    return pl.pallas_call(
        body, name="attn_delta", grid=(Bl, HP, nq),
        in_specs=[pl.BlockSpec((bq, LANES), lambda b, hp, i: (b * nq + i, hp)),
                  pl.BlockSpec((S, LANES), lambda b, hp, i: (b, hp)),
                  pl.BlockSpec((S, LANES), lambda b, hp, i: (b, HP + hp)),
                  pl.BlockSpec((bq, LANES), lambda b, hp, i: (b * nq + i, hp)),
                  per_head, per_head,
                  pl.BlockSpec((None, 2, nq, 1, bq), lambda b, hp, i: (b, hp, 0, 0, 0))],
        out_specs=per_head, out_shape=_sds((Bl, H, S, LANES), F32),
        compiler_params=_params())(qg, kv, kv, do, lse, c_col, c_row)


def _attn_bwd(qg, kv, delta, do, lse, c_col, c_row, Bl, S, D):
    T = Bl * S
    H = D // HEAD_DIM
    HP = D // LANES
    bq = min(S, ATT_BLOCK)
    nq = S // bq
    scale = 1.0 / math.sqrt(HEAD_DIM)

    def body(q_ref, k_ref, v_ref, dl_ref, do_ref, lse_ref, cc_ref, cr_ref,
             dq_ref, dk_ref, dv_ref, dcr_ref, dk_sc, dv_sc, dcr_sc):
        j = pl.program_id(2)

        @pl.when(j == 0)
        def _():
            dq_ref[...] = jnp.zeros_like(dq_ref)

        dk_sc[...] = jnp.zeros_like(dk_sc)
        dv_sc[...] = jnp.zeros_like(dv_sc)
        dcr_sc[...] = jnp.zeros_like(dcr_sc)
        k2 = k_ref[...]
        v2 = v_ref[...]
        cols = j * bq + lax.broadcasted_iota(jnp.int32, (1, bq), 1)

        def step(i, carry):
            off = pl.multiple_of(i * bq, bq)
            q2 = q_ref[pl.ds(off, bq), :]
            do2 = do_ref[pl.ds(off, bq), :]
            rows = off + lax.broadcasted_iota(jnp.int32, (bq, 1), 0)
            dqs = []
            for h in range(2):
                mh = _head_mask(h)
                kh = k2 * mh.astype(k2.dtype)
                vh = v2 * mh.astype(v2.dtype)
                s = lax.dot_general(q2, kh, NT, preferred_element_type=F32) * scale
                s = s + cc_ref[h, pl.ds(off, bq), :][:, :1] - cr_ref[h, j]
                s = jnp.where(cols <= rows, s, -jnp.inf)
                p = jnp.exp(s - lse_ref[h, pl.ds(off, bq), :][:, :1])
                delta = dl_ref[h, pl.ds(off, bq), :][:, :1]
                dv_sc[h] += lax.dot_general(p.astype(MM_DTYPE), do2, TN, preferred_element_type=F32)
                dp = lax.dot_general(do2, vh, NT, preferred_element_type=F32)
                ds = p * (dp - delta)
                dcr_sc[h] -= jnp.sum(ds, axis=0, keepdims=True)
                dsb = (ds * scale).astype(MM_DTYPE)
                dqs.append(jnp.dot(dsb, k2, preferred_element_type=F32))
                dk_sc[h] += lax.dot_general(dsb, q2, TN, preferred_element_type=F32)
            dq_ref[pl.ds(off, bq), :] += jnp.where(_head_mask(0), dqs[0], dqs[1])
            return carry

        lax.fori_loop(j, nq, step, 0)
        dk_ref[...] = jnp.where(_head_mask(0), dk_sc[0], dk_sc[1]).astype(dk_ref.dtype)
        dv_ref[...] = jnp.where(_head_mask(0), dv_sc[0], dv_sc[1]).astype(dv_ref.dtype)
        dcr_ref[...] = dcr_sc[...]

    seq = lambda col: pl.BlockSpec((S, LANES), lambda b, hp, j: (b, col(hp)))
    blk = lambda col: pl.BlockSpec((bq, LANES), lambda b, hp, j: (b * nq + j, col(hp)))
    per_head = pl.BlockSpec((None, 2, S, LANES), lambda b, hp, j: (b, hp, 0, 0))
    return pl.pallas_call(
        body, name="attn_bwd", grid=(Bl, HP, nq),
        in_specs=[seq(lambda hp: hp), blk(lambda hp: hp), blk(lambda hp: HP + hp),
                  per_head, seq(lambda hp: hp), per_head, per_head,
                  pl.BlockSpec((None, 2, nq, 1, bq), lambda b, hp, j: (b, hp, 0, 0, 0))],
        out_specs=[seq(lambda hp: hp), blk(lambda hp: hp), blk(lambda hp: hp),
                   pl.BlockSpec((None, 2, None, 1, bq), lambda b, hp, j: (b, hp, j, 0, 0))],
        out_shape=[_sds((T, D), F32), _sds((T, D), MM_DTYPE), _sds((T, D), MM_DTYPE),
                   _sds((Bl, H, nq, 1, bq), F32)],
        scratch_shapes=[pltpu.VMEM((2, bq, LANES), F32), pltpu.VMEM((2, bq, LANES), F32),
                        pltpu.VMEM((2, 1, bq), F32)],
        compiler_params=_params())(qg, kv, kv, delta, do, lse, c_col, c_row)


def _gate_fwd(qg, o):
    T, D = o.shape
    tm = _tile(T, ROW_TILE)

    def body(g_ref, o_ref, z_ref):
        z_ref[...] = (jax.nn.sigmoid(g_ref[...].astype(F32)) * o_ref[...]).astype(z_ref.dtype)

    return pl.pallas_call(
        body, name="gate_fwd", grid=(T // tm,),
        in_specs=[pl.BlockSpec((tm, D), lambda i: (i, 1)), _rows(tm, D)], out_specs=_rows(tm, D),
        out_shape=_sds((T, D), MM_DTYPE), compiler_params=_params())(qg, o)


def _gate_do(dz, qg):
    T, D = dz.shape
    tm = _tile(T, ROW_TILE)

    def body(dz_ref, g_ref, do_ref):
        do_ref[...] = (dz_ref[...].astype(F32) * jax.nn.sigmoid(g_ref[...].astype(F32))).astype(do_ref.dtype)

    return pl.pallas_call(
        body, name="gate_do", grid=(T // tm,),
        in_specs=[_rows(tm, D), pl.BlockSpec((tm, D), lambda i: (i, 1))], out_specs=_rows(tm, D),
        out_shape=_sds((T, D), MM_DTYPE), compiler_params=_params())(dz, qg)


def _gate_bwd(dz, qg, o, dq):
    T, D = dz.shape
    tm = _tile(T, ROW_TILE)

    def body(dz_ref, g_ref, o_ref, dq_ref, out_ref):
        g = g_ref[...].astype(F32)
        sg = jax.nn.sigmoid(g)
        out_ref[:, :D] = dq_ref[...].astype(out_ref.dtype)
        out_ref[:, D:] = (dz_ref[...].astype(F32) * o_ref[...] * sg * (1.0 - sg)).astype(out_ref.dtype)

    return pl.pallas_call(
        body, name="gate_bwd", grid=(T // tm,),
        in_specs=[_rows(tm, D), pl.BlockSpec((tm, D), lambda i: (i, 1)), _rows(tm, D), _rows(tm, D)],
        out_specs=_rows(tm, 2 * D), out_shape=_sds((T, 2 * D), MM_DTYPE),
        compiler_params=_params())(dz, qg, o, dq)


def _mm_in(a, wg, out_dtype, tag, l=None):
    T, K = a.shape
    n = wg.shape[-1]
    tm = _tile(T, MM_TILE)
    if l is None:
        w_spec = _whole((N_CHIP, K, n))
    else:
        w_spec = pl.BlockSpec((None, N_CHIP, K, n), lambda i: (l, 0, 0, 0))

    def body(a_ref, w_ref, o_ref):
        av = a_ref[...]
        for s in range(N_CHIP):
            o_ref[:, s * n:(s + 1) * n] = jnp.dot(av, w_ref[s], preferred_element_type=F32).astype(o_ref.dtype)

    return pl.pallas_call(
        body, name=f"mm_in_{tag}", grid=(T // tm,),
        in_specs=[_rows(tm, K), w_spec], out_specs=_rows(tm, N_CHIP * n),
        out_shape=_sds((T, N_CHIP * n), out_dtype), compiler_params=_params())(a, wg)


def _mm_nt_in(dy, wg, tag, l=None):
    T = dy.shape[0]
    K, n = wg.shape[-2:]
    tm = _tile(T, MM_TILE)
    if l is None:
        w_spec = _whole((N_CHIP, K, n))
    else:
        w_spec = pl.BlockSpec((None, N_CHIP, K, n), lambda i: (l, 0, 0, 0))

    def body(d_ref, w_ref, o_ref):
        acc = None
        for s in range(N_CHIP):
            part = lax.dot_general(d_ref[:, s * n:(s + 1) * n], w_ref[s], NT, preferred_element_type=F32)
            acc = part if acc is None else acc + part
        o_ref[...] = acc

    return pl.pallas_call(
        body, name=f"mm_nt_in_{tag}", grid=(T // tm,),
        in_specs=[_rows(tm, N_CHIP * n), w_spec], out_specs=_rows(tm, K),
        out_shape=_sds((T, K), F32), compiler_params=_params())(dy, wg)


def _mm_nn(a, b, out_dtype, tag):
    T, K = a.shape
    N = b.shape[1]
    tm = _tile(T, MM_TILE)

    def body(a_ref, b_ref, o_ref):
        o_ref[...] = jnp.dot(a_ref[...], b_ref[...], preferred_element_type=F32).astype(o_ref.dtype)

    return pl.pallas_call(
        body, name=f"mm_nn_{tag}", grid=(T // tm,),
        in_specs=[_rows(tm, K), _whole((K, N))], out_specs=_rows(tm, N),
        out_shape=_sds((T, N), out_dtype), compiler_params=_params())(a, b)


def _mm_nt(a, b, out_dtype, tag):
    T, C = a.shape
    N = b.shape[0]
    tm = _tile(T, MM_TILE)
    nb = N
    for cand in (1408, 1024):
        if N > cand and N % cand == 0:
            nb = cand
            break

    def body(a_ref, b_ref, o_ref):
        o_ref[...] = lax.dot_general(a_ref[...], b_ref[...], NT, preferred_element_type=F32).astype(o_ref.dtype)

    return pl.pallas_call(
        body, name=f"mm_nt_{tag}", grid=(N // nb, T // tm),
        in_specs=[pl.BlockSpec((tm, C), lambda j, i: (i, 0)), pl.BlockSpec((nb, C), lambda j, i: (j, 0))],
        out_specs=pl.BlockSpec((tm, nb), lambda j, i: (i, j)),
        out_shape=_sds((T, N), out_dtype), compiler_params=_params())(a, b)


def _mm_tn_in(a, dy, tag, l=None, prev=None):
    T, K = a.shape
    n = dy.shape[1] // N_CHIP
    tt = _tile(T, MM_TILE)

    def body(a_ref, d_ref, *rest):
        o_ref = rest[-1]
        part = lax.dot_general(a_ref[...], d_ref[...], TN, preferred_element_type=F32)
        _accumulate(o_ref, part, pl.program_id(1) == 0)

    in_specs = [pl.BlockSpec((tt, K), lambda s, t: (t, 0)), pl.BlockSpec((tt, n), lambda s, t: (t, s))]
    args = [a, dy]
    kw = {}
    if l is None:
        out_spec = pl.BlockSpec((None, K, n), lambda s, t: (s, 0, 0))
        out_shape = _sds((N_CHIP, K, n), F32)
    else:
        out_spec = pl.BlockSpec((None, None, K, n), lambda s, t: (l, s, 0, 0))
        out_shape = _sds((2, N_CHIP, K, n), F32)
        if prev is not None:
            in_specs.append(ANY)
            args.append(prev)
            kw["input_output_aliases"] = {2: 0}
    return pl.pallas_call(
        body, name=f"mm_tn_in_{tag}", grid=(N_CHIP, T // tt),
        in_specs=in_specs, out_specs=out_spec, out_shape=out_shape,
        compiler_params=_params(), **kw)(*args)


def _mm_tn_out(act, dh, tag, l=None, prev=None):
    T, R4 = act.shape
    D = dh.shape[1]
    r = R4 // N_CHIP
    g = 1 if r % LANES == 0 else 2
    tt = _tile(T, MM_TILE)

    def body(a_ref, d_ref, *rest):
        o_ref = rest[-1]
        part = lax.dot_general(a_ref[...], d_ref[...], TN, preferred_element_type=F32)
        first = pl.program_id(1) == 0
        for q in range(g):
            _accumulate(o_ref.at[q], part[q * r:(q + 1) * r], first)

    in_specs = [pl.BlockSpec((tt, g * r), lambda s, t: (t, s)), pl.BlockSpec((tt, D), lambda s, t: (t, 0))]
    args = [act, dh]
    kw = {}
    if l is None:
        out_spec = pl.BlockSpec((g, r, D), lambda s, t: (s, 0, 0))
        out_shape = _sds((N_CHIP, r, D), F32)
    else:
        out_spec = pl.BlockSpec((None, g, r, D), lambda s, t: (l, s, 0, 0))
        out_shape = _sds((2, N_CHIP, r, D), F32)
        if prev is not None:
            in_specs.append(ANY)
            args.append(prev)
            kw["input_output_aliases"] = {2: 0}
    return pl.pallas_call(
        body, name=f"mm_tn_out_{tag}", grid=(N_CHIP // g, T // tt),
        in_specs=in_specs, out_specs=out_spec, out_shape=out_shape,
        compiler_params=_params(), **kw)(*args)


def _mm_tn(a, b, tag):
    T, K = a.shape
    N = b.shape[1]
    tt = _tile(T, MM_TILE)

    def body(a_ref, b_ref, o_ref):
        part = lax.dot_general(a_ref[...], b_ref[...], TN, preferred_element_type=F32)
        _accumulate(o_ref, part, pl.program_id(0) == 0)

    return pl.pallas_call(
        body, name=f"mm_tn_{tag}", grid=(T // tt,),
        in_specs=[_rows(tt, K), _rows(tt, N)], out_specs=_whole((K, N)),
        out_shape=_sds((K, N), F32), compiler_params=_params())(a, b)


def _adamw(w, g, m, v, tag):
    R, C = w.shape
    tr = _tile(R, ROW_TILE)

    def body(w_ref, g_ref, m_ref, v_ref, d_ref, mo_ref, vo_ref):
        gv = g_ref[...]
        mn = ADAM_B1 * m_ref[...] + (1.0 - ADAM_B1) * gv
        vn = ADAM_B2 * v_ref[...] + (1.0 - ADAM_B2) * (gv * gv)
        m_hat = mn / (1.0 - ADAM_B1 ** ADAM_STEP)
        v_hat = vn / (1.0 - ADAM_B2 ** ADAM_STEP)
        d_ref[...] = -ADAM_LR * (m_hat / (jnp.sqrt(v_hat) + ADAM_EPS) + ADAM_WD * w_ref[...])
        mo_ref[...] = mn
        vo_ref[...] = vn

    return pl.pallas_call(
        body, name=f"adamw_{tag}", grid=(R // tr,),
        in_specs=[_rows(tr, C)] * 4, out_specs=[_rows(tr, C)] * 3,
        out_shape=[_sds((R, C), F32)] * 3, compiler_params=_params())(w, g, m, v)


def _sum_devices(gall):
    _, R, C = gall.shape

    def body(g_ref, o_ref):
        acc = g_ref[0]
        for d in range(1, N_DEV):
            acc = acc + g_ref[d]
        o_ref[...] = acc

    return pl.pallas_call(
        body, name="sum_devices", in_specs=[_whole((N_DEV, R, C))], out_specs=_whole((R, C)),
        out_shape=_sds((R, C), F32), grid=(1,), compiler_params=_params())(gall)


def _place():
    x, y, c = lax.axis_index("x"), lax.axis_index("y"), lax.axis_index("c")
    chips = ((1 - x, y), (x, 1 - y), (1 - x, 1 - y))
    return x, y, c, chips


def _blk(ref, chip, half, layer_major):
    return ref.at[half, chip] if layer_major else ref.at[chip, half]


def _remote(src, dst, send_sem, recv_sem, dev):
    return pltpu.make_async_remote_copy(src_ref=src, dst_ref=dst, send_sem=send_sem, recv_sem=recv_sem,
                                        device_id=dev, device_id_type=MESH)


def _allgather(halves, layer_major):
    n = len(halves)

    def body(*refs):
        ins, outs = refs[:n], refs[n:2 * n]
        s_ici, r_ici, s_d2d, r_d2d, s_loc = refs[2 * n:]
        x, y, c, chips = _place()
        me = 2 * x + y
        sib = (x, y, 1 - c)
        local, sends = [], []
        for i in range(n):
            for half in range(2):
                cp = pltpu.make_async_copy(ins[i].at[half], _blk(outs[i], me, half, layer_major[i]), s_loc.at[i, half])
                cp.start()
                local.append(cp)
            for j, (px, py) in enumerate(chips):
                cp = _remote(ins[i].at[c], _blk(outs[i], me, c, layer_major[i]), s_ici.at[i, j], r_ici.at[i, j], (px, py, c))
                cp.start()
                sends.append(cp)
        for j, (px, py) in enumerate(chips):
            src = 2 * px + py
            for i in range(n):
                blk = _blk(outs[i], src, c, layer_major[i])
                _remote(blk, blk, s_ici.at[i, j], r_ici.at[i, j], (px, py, c)).wait_recv()
                cp = _remote(blk, blk, s_d2d.at[i, j], r_d2d.at[i, j], sib)
                cp.start()
                sends.append(cp)
        for j, (px, py) in enumerate(chips):
            src = 2 * px + py
            for i in range(n):
                blk = _blk(outs[i], src, 1 - c, layer_major[i])
                _remote(blk, blk, s_d2d.at[i, j], r_d2d.at[i, j], sib).wait_recv()
        for cp in sends:
            cp.wait_send()
        for cp in local:
            cp.wait()

    out_shape = []
    for a, lm in zip(halves, layer_major):
        _, r, col = a.shape
        out_shape.append(_sds((2, N_CHIP, r, col) if lm else (N_CHIP, 2, r, col), a.dtype))
    dma = pltpu.SemaphoreType.DMA
    return pl.pallas_call(
        body, name="allgather_weights", in_specs=[ANY] * n, out_specs=[ANY] * n, out_shape=out_shape,
        scratch_shapes=[dma((n, 3)), dma((n, 3)), dma((n, 3)), dma((n, 3)), dma((n, 2))],
        )(*halves)


def _rs_pair_send(grads, layer_major):
    n = len(grads)

    def body(*refs):
        ins, outs = refs[:n], refs[n:2 * n]
        s_sem, r_sem = refs[2 * n:]
        x, y, c, _ = _place()
        sib = (x, y, 1 - c)
        sends = []
        for i in range(n):
            src = ins[i].at[1 - c] if layer_major[i] else ins[i].at[:, 1 - c]
            cp = _remote(src, outs[i], s_sem.at[i], r_sem.at[i], sib)
            cp.start()
            sends.append(cp)
        for cp in sends:
            cp.wait()

    out_shape = [_sds((N_CHIP,) + g.shape[2:], g.dtype) for g in grads]
    dma = pltpu.SemaphoreType.DMA
    return pl.pallas_call(
        body, name="rs_pair_send", in_specs=[ANY] * n, out_specs=[ANY] * n, out_shape=out_shape,
        scratch_shapes=[dma((n,)), dma((n,))],
        )(*grads)


def _rs_pair_add(g, recv, layer_major, place, tag):
    r, col = g.shape[-2:]
    tr = _tile(r, ROW_TILE)

    def body(place_ref, g_ref, r_ref, wire_ref, own_ref):
        tot = g_ref[...] + r_ref[...]
        wire_ref[...] = tot.astype(wire_ref.dtype)

        @pl.when(pl.program_id(1) == place_ref[1])
        def _():
            own_ref[...] = tot

    if layer_major:
        g_spec = pl.BlockSpec((None, None, tr, col), lambda i, s, p: (p[0], s, i, 0))
    else:
        g_spec = pl.BlockSpec((None, None, tr, col), lambda i, s, p: (s, p[0], i, 0))
    grid_spec = pltpu.PrefetchScalarGridSpec(
        num_scalar_prefetch=1, grid=(r // tr, N_CHIP),
        in_specs=[g_spec, pl.BlockSpec((None, tr, col), lambda i, s, p: (s, i, 0))],
        out_specs=[pl.BlockSpec((None, tr, col), lambda i, s, p: (s, i, 0)),
                   pl.BlockSpec((tr, col), lambda i, s, p: (i, 0))])
    return pl.pallas_call(
        body, name=f"rs_pair_add_{tag}", grid_spec=grid_spec,
        out_shape=[_sds((N_CHIP, r, col), WIRE_DTYPE), _sds((r, col), F32)],
        compiler_params=_params())(place, g, recv)


def _rs_chip_send(wires, small):
    n = len(wires)

    def body(*refs):
        ins, small_ref = refs[:n], refs[n]
        outs, gall_ref = refs[n + 1:2 * n + 1], refs[2 * n + 1]
        s_sem, r_sem, s_small, r_small, s_loc = refs[2 * n + 2:]
        x, y, c, chips = _place()
        me = 4 * x + 2 * y + c
        sends = []
        loc = pltpu.make_async_copy(small_ref, gall_ref.at[me], s_loc)
        loc.start()
        for i in range(n):
            for j, (px, py) in enumerate(chips):
                cp = _remote(ins[i].at[2 * px + py], outs[i].at[j], s_sem.at[i, j], r_sem.at[i, j], (px, py, c))
                cp.start()
                sends.append(cp)
        for k in range(1, N_DEV):
            peer = (x ^ (k >> 2), y ^ ((k >> 1) & 1), c ^ (k & 1))
            cp = _remote(small_ref, gall_ref.at[me], s_small.at[k - 1], r_small.at[k - 1], peer)
            cp.start()
            sends.append(cp)
        for cp in sends:
            cp.wait()
        loc.wait()

    out_shape = [_sds((3,) + w.shape[1:], w.dtype) for w in wires] + [_sds((N_DEV,) + small.shape, small.dtype)]
    dma = pltpu.SemaphoreType.DMA
    return pl.pallas_call(
        body, name="rs_chip_send", in_specs=[ANY] * (n + 1), out_specs=[ANY] * (n + 1), out_shape=out_shape,
        scratch_shapes=[dma((n, 3)), dma((n, 3)), dma((N_DEV - 1,)), dma((N_DEV - 1,)), dma(())],
        )(*wires, small)


def _rs_chip_add(own, recv, tag):
    r, col = own.shape
    tr = _tile(r, ROW_TILE)

    def body(o_ref, r_ref, out_ref):
        acc = o_ref[...]
        for j in range(3):
            acc = acc + r_ref[j].astype(F32)
        out_ref[...] = acc

    return pl.pallas_call(
        body, name=f"rs_chip_add_{tag}", grid=(r // tr,),
        in_specs=[_rows(tr, col), pl.BlockSpec((3, tr, col), lambda i: (0, i, 0))], out_specs=_rows(tr, col),
        out_shape=_sds((r, col), F32), compiler_params=_params())(own, recv)


def _rs_pair_share(parts):
    n = len(parts)

    def body(*refs):
        ins, outs = refs[:n], refs[n:2 * n]
        s_sem, r_sem, s_loc = refs[2 * n:]
        x, y, c, _ = _place()
        sib = (x, y, 1 - c)
        started = []
        for i in range(n):
            loc = pltpu.make_async_copy(ins[i], outs[i].at[c], s_loc.at[i])
            loc.start()
            cp = _remote(ins[i], outs[i].at[c], s_sem.at[i], r_sem.at[i], sib)
            cp.start()
            started += [loc, cp]
        for cp in started:
            cp.wait()

    out_shape = [_sds((2,) + p.shape, p.dtype) for p in parts]
    dma = pltpu.SemaphoreType.DMA
    return pl.pallas_call(
        body, name="rs_pair_share", in_specs=[ANY] * n, out_specs=[ANY] * n, out_shape=out_shape,
        scratch_shapes=[dma((n,)), dma((n,)), dma((n,))],
        )(*parts)


def _ffn_fwd(x, g_pre, g_post, w_in, w_out, l, tag):
    xn = _rms_fwd(x, g_pre[l], tag)
    hgu = _mm_in(xn, w_in, MM_DTYPE, tag, l=l)
    act = _swiglu_fwd(hgu, tag)
    h = _mm_nn(act, w_out[l].reshape(-1, w_out.shape[-1]), F32, tag)
    return _post_fwd(x, h, g_post[l], 0.5, tag), (x, xn, hgu, act, h)


def _ffn_bwd(dx, saved, g_pre, g_post, w_in, w_out, l, tag, prev):
    x, xn, hgu, act, h = saved
    dh, dg_post = _post_bwd(dx, h, g_post[l], 0.5, tag)
    dact = _mm_nt(dh, w_out[l].reshape(-1, w_out.shape[-1]), MM_DTYPE, f"{tag}_out")
    dw_out = _mm_tn_out(act, dh, tag, l=l, prev=prev[1])
    dhgu = _swiglu_bwd(hgu, dact, tag)
    dxn = _mm_nt_in(dhgu, w_in, tag, l=l)
    dw_in = _mm_tn_in(xn, dhgu, tag, l=l, prev=prev[0])
    dx_in, dg_pre = _pre_bwd(dx, dxn, x, g_pre[l], tag)
    return dx_in, dg_pre, dg_post, (dw_in, dw_out)


def kernel(x, ffn1_pre_g, ffn1_post_g, ffn1_w_in, ffn1_w_out, mix_pre_g, mix_post_g, ffn2_pre_g, ffn2_post_g, ffn2_w_in, ffn2_w_out, conv_w_in, conv_k, conv_w_out, kv_g, kv_w, forget_b, attn_w_qg, attn_w_o, loss_target, m_ffn1_pre_g, m_ffn1_post_g, m_ffn1_w_in, m_ffn1_w_out, m_mix_pre_g, m_mix_post_g, m_ffn2_pre_g, m_ffn2_post_g, m_ffn2_w_in, m_ffn2_w_out, m_conv_w_in, m_conv_k, m_conv_w_out, m_kv_g, m_kv_w, m_forget_b, m_attn_w_qg, m_attn_w_o, v_ffn1_pre_g, v_ffn1_post_g, v_ffn1_w_in, v_ffn1_w_out, v_mix_pre_g, v_mix_post_g, v_ffn2_pre_g, v_ffn2_post_g, v_ffn2_w_in, v_ffn2_w_out, v_conv_w_in, v_conv_k, v_conv_w_out, v_kv_g, v_kv_w, v_forget_b, v_attn_w_qg, v_attn_w_o):
    Bl, S, D = x.shape
    T = Bl * S
    H = forget_b.shape[0]
    assert D == H * HEAD_DIM and D % LANES == 0
    kvc = kv_w.shape[1]
    kvp = -(-kvc // LANES) * LANES
    kv_all = 2 * D + LANES
    dk_cols = conv_k.shape[2]
    chip = 2 * lax.axis_index("x") + lax.axis_index("y")
    core = lax.axis_index("c")

    def halves(a3):
        L, r, col = a3.shape
        return (a3, True) if L == 2 else (a3.reshape(2, r // 2, col), False)

    big = {"ffn1_w_in": ffn1_w_in, "ffn1_w_out": ffn1_w_out, "ffn2_w_in": ffn2_w_in, "ffn2_w_out": ffn2_w_out,
           "conv_w_in": conv_w_in, "conv_w_out": conv_w_out,
           "kv_w": jnp.pad(kv_w, ((0, 0), (0, kvp - kvc)))[None],
           "attn_w_qg": attn_w_qg, "attn_w_o": attn_w_o}
    names = list(big)
    send = [halves(big[k].astype(MM_DTYPE)) for k in names]
    send.append(halves(jnp.pad(conv_k[0], ((0, 13), (0, 0)))[None]))
    layer_major = [lm for _, lm in send]
    gathered = _allgather([a for a, _ in send], layer_major)
    W = {}
    for k, g, lm in zip(names, gathered, layer_major):
        W[k] = g if lm else g.reshape(N_CHIP, -1, g.shape[-1])
    k_taps = gathered[-1].reshape(N_CHIP, 16, dk_cols).transpose(1, 0, 2).reshape(16, D)[:8]
    kv_full = jnp.concatenate([W["kv_w"][s, :, :kvc] for s in range(N_CHIP)], axis=1)
    kv_full = jnp.pad(kv_full, ((0, 0), (0, kv_all - kv_full.shape[1])))
    w_o_conv = W["conv_w_out"].reshape(D, D)
    w_o_attn = W["attn_w_o"].reshape(D, D)
    fb = jnp.pad(forget_b, (0, LANES - H)).reshape(1, LANES)

    x0 = x.reshape(T, D)
    x1, s_f1a = _ffn_fwd(x0, ffn1_pre_g, ffn1_post_g, W["ffn1_w_in"], W["ffn1_w_out"], 0, "l0f1")
    xn_c = _rms_fwd(x1, mix_pre_g[0], "l0mix")
    bch = _mm_in(xn_c, W["conv_w_in"], MM_DTYPE, "conv")
    z_c = _conv_fwd(bch, k_taps, Bl, S)
    m_c = _mm_nn(z_c, w_o_conv, F32, "conv_out")
    x2 = _post_fwd(x1, m_c, mix_post_g[0], 1.0, "l0mix")
    x3, s_f2a = _ffn_fwd(x2, ffn2_pre_g, ffn2_post_g, W["ffn2_w_in"], W["ffn2_w_out"], 0, "l0f2")

    xn_kv = _rms_fwd(x3, kv_g, "kv")
    kvact = _mm_nn(xn_kv, kv_full[:, :2 * D], MM_DTYPE, "kv")
    pf = _mm_nn(xn_kv, kv_full[:, 2 * D:], F32, "forget")
    cum = _forget_fwd(pf, fb, Bl, S)
    bq = min(S, ATT_BLOCK)
    c3 = cum.reshape(Bl, S, LANES)[:, :, :H].transpose(0, 2, 1)
    c_col = jnp.broadcast_to(c3[..., None], (Bl, H, S, LANES))
    c_row = c3.reshape(Bl, H, S // bq, 1, bq)

    x4, s_f1b = _ffn_fwd(x3, ffn1_pre_g, ffn1_post_g, W["ffn1_w_in"], W["ffn1_w_out"], 1, "l1f1")
    xn_a = _rms_fwd(x4, mix_pre_g[1], "l1mix")
    qg = _mm_in(xn_a, W["attn_w_qg"], MM_DTYPE, "qg")
    o, lse = _attn_fwd(qg, kvact, c_col, c_row, Bl, S, D)
    z_a = _gate_fwd(qg, o)
    m_a = _mm_nn(z_a, w_o_attn, F32, "attn_out")
    x5 = _post_fwd(x4, m_a, mix_post_g[1], 1.0, "l1mix")
    x6, s_f2b = _ffn_fwd(x5, ffn2_pre_g, ffn2_post_g, W["ffn2_w_in"], W["ffn2_w_out"], 1, "l1f2")

    dy, loss_local = _loss_grad(x6, loss_target.reshape(T, D))
    loss = lax.psum(loss_local, ("x", "y", "c"))

    none2 = (None, None)
    dx5, dg_f2pre_1, dg_f2post_1, G_f2 = _ffn_bwd(dy, s_f2b, ffn2_pre_g, ffn2_post_g, W["ffn2_w_in"], W["ffn2_w_out"], 1, "l1f2", none2)
    dm_a, dg_mixpost_1 = _post_bwd(dx5, m_a, mix_post_g[1], 1.0, "l1mix")
    dz_a = _mm_nt(dm_a, w_o_attn, MM_DTYPE, "attn_out")
    G_attn_o = _mm_tn_out(z_a, dm_a, "attn_out")
    do = _gate_do(dz_a, qg)
    delta = _attn_delta(qg, kvact, do, lse, c_col, c_row, Bl, S, D)
    dq, dk, dv, dcr = _attn_bwd(qg, kvact, delta, do, lse, c_col, c_row, Bl, S, D)
    dqg = _gate_bwd(dz_a, qg, o, dq)
    dxn_a = _mm_nt_in(dqg, W["attn_w_qg"], "qg")
    G_qg = _mm_tn_in(xn_a, dqg, "qg")
    dx4, dg_mixpre_1 = _pre_bwd(dx5, dxn_a, x4, mix_pre_g[1], "l1mix")
    dx3, dg_f1pre_1, dg_f1post_1, G_f1 = _ffn_bwd(dx4, s_f1b, ffn1_pre_g, ffn1_post_g, W["ffn1_w_in"], W["ffn1_w_out"], 1, "l1f1", none2)

    dcum = jnp.pad(dcr.reshape(Bl, H, S).transpose(0, 2, 1), ((0, 0), (0, 0), (0, LANES - H))).reshape(T, LANES)
    dpf, dfb = _forget_bwd(dcum, pf, fb, Bl, S)
    dp = jnp.concatenate([dk, dv, dpf], axis=1)
    dxn_kv = _mm_nt(dp, kv_full, F32, "kv")
    G_kv_full = _mm_tn(xn_kv, dp, "kv")
    dx3, dg_kv = _pre_bwd(dx3, dxn_kv, x3, kv_g, "kv")

    dx2, dg_f2pre_0, dg_f2post_0, G_f2 = _ffn_bwd(dx3, s_f2a, ffn2_pre_g, ffn2_post_g, W["ffn2_w_in"], W["ffn2_w_out"], 0, "l0f2", G_f2)
    dm_c, dg_mixpost_0 = _post_bwd(dx2, m_c, mix_post_g[0], 1.0, "l0mix")
    dz_c = _mm_nt(dm_c, w_o_conv, MM_DTYPE, "conv_out")
    G_conv_o = _mm_tn_out(z_c, dm_c, "conv_out")
    db, dcg, dhh, dk_taps = _conv_bwd(bch, dz_c, k_taps, Bl, S)
    dbch = jnp.concatenate([db, dcg, dhh], axis=1)
    dxn_c = _mm_nt_in(dbch, W["conv_w_in"], "conv")
    G_conv_in = _mm_tn_in(xn_c, dbch, "conv")
    dx1, dg_mixpre_0 = _pre_bwd(dx2, dxn_c, x1, mix_pre_g[0], "l0mix")
    dx0, dg_f1pre_0, dg_f1post_0, G_f1 = _ffn_bwd(dx1, s_f1a, ffn1_pre_g, ffn1_post_g, W["ffn1_w_in"], W["ffn1_w_out"], 0, "l0f1", G_f1)
    grad_x = dx0.reshape(Bl, S, D)

    G_kv = jnp.stack([jnp.pad(G_kv_full[:, s * kvc:(s + 1) * kvc], ((0, 0), (0, kvp - kvc))) for s in range(N_CHIP)])
    G = {"ffn1_w_in": G_f1[0], "ffn1_w_out": G_f1[1], "ffn2_w_in": G_f2[0], "ffn2_w_out": G_f2[1],
         "conv_w_in": G_conv_in, "conv_w_out": G_conv_o, "kv_w": G_kv, "attn_w_qg": G_qg, "attn_w_o": G_attn_o}
    grads = []
    for k, lm in zip(names, layer_major):
        g = G[k]
        grads.append(g if lm else g.reshape(N_CHIP, 2, g.shape[1] // 2, g.shape[2]))
    lms = layer_major[:len(names)]
    place = jnp.stack([core, chip]).astype(jnp.int32)
    from_sibling = _rs_pair_send(grads, lms)
    wires, owns = [], []
    for k, g, r, lm in zip(names, grads, from_sibling, lms):
        w, own = _rs_pair_add(g, r, lm, place, k)
        wires.append(w)
        owns.append(own)

    def row(v):
        return jnp.pad(v.reshape(-1), (0, D - v.size)).reshape(1, D)

    small_parts = [dg_f1pre_0, dg_f1pre_1, dg_f1post_0, dg_f1post_1, dg_mixpre_0, dg_mixpre_1, dg_mixpost_0, dg_mixpost_1,
                   dg_f2pre_0, dg_f2pre_1, dg_f2post_0, dg_f2post_1, dg_kv, row(dfb[0, :H]), dk_taps[:3]]
    small = jnp.concatenate(small_parts, axis=0)
    small = jnp.pad(small, ((0, SMALL_ROWS - small.shape[0]), (0, 0)))
    outs = _rs_chip_send(wires, small)
    recvs, gall = outs[:-1], outs[-1]
    parts = [_rs_chip_add(own, r, k) for k, own, r in zip(names, owns, recvs)]
    reduced = _rs_pair_share(parts)
    gsum = _sum_devices(gall)

    given = dict(ffn1_w_in=(ffn1_w_in, m_ffn1_w_in, v_ffn1_w_in), ffn1_w_out=(ffn1_w_out, m_ffn1_w_out, v_ffn1_w_out),
                 ffn2_w_in=(ffn2_w_in, m_ffn2_w_in, v_ffn2_w_in), ffn2_w_out=(ffn2_w_out, m_ffn2_w_out, v_ffn2_w_out),
                 conv_w_in=(conv_w_in, m_conv_w_in, v_conv_w_in), conv_w_out=(conv_w_out, m_conv_w_out, v_conv_w_out),
                 kv_w=(kv_w, m_kv_w, v_kv_w), attn_w_qg=(attn_w_qg, m_attn_w_qg, v_attn_w_qg),
                 attn_w_o=(attn_w_o, m_attn_w_o, v_attn_w_o))
    res = {}
    for k, red in zip(names, reduced):
        w, m, v = given[k]
        g2 = red.reshape(-1, red.shape[-1])
        if k == "kv_w":
            g2 = g2[:, :kvc]
        flat = lambda a: a.reshape(-1, a.shape[-1])
        d, mn, vn = _adamw(flat(w), g2, flat(m), flat(v), k)
        res[k] = tuple(a.reshape(w.shape) for a in (g2, d, mn, vn))

    small_names = ["ffn1_pre_g", "ffn1_post_g", "mix_pre_g", "mix_post_g", "ffn2_pre_g", "ffn2_post_g"]
    small_given = dict(ffn1_pre_g=(ffn1_pre_g, m_ffn1_pre_g, v_ffn1_pre_g), ffn1_post_g=(ffn1_post_g, m_ffn1_post_g, v_ffn1_post_g),
                       mix_pre_g=(mix_pre_g, m_mix_pre_g, v_mix_pre_g), mix_post_g=(mix_post_g, m_mix_post_g, v_mix_post_g),
                       ffn2_pre_g=(ffn2_pre_g, m_ffn2_pre_g, v_ffn2_pre_g), ffn2_post_g=(ffn2_post_g, m_ffn2_post_g, v_ffn2_post_g))

    def pack(idx):
        rows_ = [small_given[k][idx] for k in small_names]
        rows_ += [row((kv_g, m_kv_g, v_kv_g)[idx]), row((forget_b, m_forget_b, v_forget_b)[idx])]
        rows_.append(jnp.pad((conv_k, m_conv_k, v_conv_k)[idx][0], ((0, 0), (0, D - dk_cols))))
        a = jnp.concatenate(rows_, axis=0)
        return jnp.pad(a, ((0, SMALL_ROWS - a.shape[0]), (0, 0)))

    g_taps = lax.dynamic_slice_in_dim(gsum[14:17], chip * dk_cols, dk_cols, axis=1)
    g_small = jnp.concatenate([gsum[:14], jnp.pad(g_taps, ((0, 0), (0, D - dk_cols))), gsum[17:]], axis=0)
    d_s, m_s, v_s = _adamw(pack(0), g_small, pack(1), pack(2), "small")
    for i, k in enumerate(small_names):
        res[k] = tuple(a[2 * i:2 * i + 2] for a in (g_small, d_s, m_s, v_s))
    res["kv_g"] = tuple(a[12] for a in (g_small, d_s, m_s, v_s))
    res["forget_b"] = tuple(a[13, :H] for a in (g_small, d_s, m_s, v_s))
    res["conv_k"] = tuple(a[14:17, :dk_cols][None] for a in (g_small, d_s, m_s, v_s))

    order = ["ffn1_pre_g", "ffn1_post_g", "ffn1_w_in", "ffn1_w_out", "mix_pre_g", "mix_post_g", "ffn2_pre_g", "ffn2_post_g",
             "ffn2_w_in", "ffn2_w_out", "conv_w_in", "conv_k", "conv_w_out", "kv_g", "kv_w", "forget_b", "attn_w_qg", "attn_w_o"]
    out = [loss, grad_x]
    for idx in range(4):
        out += [res[k][idx] for k in order]
    return tuple(out)
```

```python
import functools
import math

import jax
import jax.numpy as jnp
from jax import lax
from jax.experimental import pallas as pl
from jax.experimental.pallas import tpu as pltpu

F32 = jnp.float32
MM_DTYPE = jnp.bfloat16
WIRE_DTYPE = jnp.bfloat16

RMS_EPS = 1e-6
ADAM_LR = 0.001
ADAM_B1 = 0.9
ADAM_B2 = 0.999
ADAM_EPS = 1e-08
ADAM_WD = 0.01
ADAM_STEP = 10

HEAD_DIM = 64
LANES = 128
N_CHIP = 4
N_DEV = 8
ROW_TILE = 256
MM_TILE = 512
ATT_BLOCK = 256
SMALL_ROWS = 24
VMEM_LIMIT = 56 * 1024 * 1024
MESH = pl.DeviceIdType.MESH
ANY = pl.BlockSpec(memory_space=pl.ANY)

NT = (((1,), (1,)), ((), ()))
TN = (((0,), (0,)), ((), ()))


def _tile(n, pref):
    if n <= pref:
        return n
    t = pref - pref % 16
    while n % t:
        t -= 16
    return t


def _params():
    return pltpu.CompilerParams(vmem_limit_bytes=VMEM_LIMIT)


def _sds(shape, dtype):
    return jax.ShapeDtypeStruct(shape, dtype)


def _rows(tm, c):
    return pl.BlockSpec((tm, c), lambda i: (i, 0))


def _whole(shape):
    return pl.BlockSpec(shape, lambda *_: (0,) * len(shape))


def _rms_fwd(x, g, tag):
    T, D = x.shape
    tm = _tile(T, ROW_TILE)

    def body(x_ref, g_ref, o_ref):
        xv = x_ref[...]
        r = lax.rsqrt(jnp.mean(xv * xv, axis=-1, keepdims=True) + RMS_EPS)
        o_ref[...] = (xv * r * g_ref[...]).astype(o_ref.dtype)

    return pl.pallas_call(
        body, name=f"rms_fwd_{tag}", grid=(T // tm,),
        in_specs=[_rows(tm, D), _whole((1, D))], out_specs=_rows(tm, D),
        out_shape=_sds((T, D), MM_DTYPE), compiler_params=_params())(x, g.reshape(1, D))


def _post_fwd(x, h, g, alpha, tag):
    T, D = x.shape
    tm = _tile(T, ROW_TILE)

    def body(x_ref, h_ref, g_ref, o_ref):
        hv = h_ref[...]
        r = lax.rsqrt(jnp.mean(hv * hv, axis=-1, keepdims=True) + RMS_EPS)
        o_ref[...] = x_ref[...] + alpha * (hv * r * g_ref[...])

    return pl.pallas_call(
        body, name=f"post_fwd_{tag}", grid=(T // tm,),
        in_specs=[_rows(tm, D), _rows(tm, D), _whole((1, D))], out_specs=_rows(tm, D),
        out_shape=_sds((T, D), F32), compiler_params=_params())(x, h, g.reshape(1, D))


def _accumulate(ref, part, first):
    @pl.when(first)
    def _():
        ref[...] = part

    @pl.when(jnp.logical_not(first))
    def _():
        ref[...] += part


def _post_bwd(dx, h, g, alpha, tag):
    T, D = dx.shape
    tm = _tile(T, ROW_TILE)

    def body(dx_ref, h_ref, g_ref, dh_ref, dg_ref):
        hv = h_ref[...]
        r = lax.rsqrt(jnp.mean(hv * hv, axis=-1, keepdims=True) + RMS_EPS)
        hh = hv * r
        dyn = alpha * dx_ref[...]
        _accumulate(dg_ref, jnp.sum(dyn * hh, axis=0, keepdims=True), pl.program_id(0) == 0)
        dhh = dyn * g_ref[...]
        dh = r * (dhh - hh * jnp.mean(dhh * hh, axis=-1, keepdims=True))
        dh_ref[...] = dh.astype(dh_ref.dtype)

    return pl.pallas_call(
        body, name=f"post_bwd_{tag}", grid=(T // tm,),
        in_specs=[_rows(tm, D), _rows(tm, D), _whole((1, D))],
        out_specs=[_rows(tm, D), _whole((1, D))],
        out_shape=[_sds((T, D), MM_DTYPE), _sds((1, D), F32)],
        compiler_params=_params())(dx, h, g.reshape(1, D))


def _pre_bwd(dres, dxn, x, g, tag):
    T, D = x.shape
    tm = _tile(T, ROW_TILE)

    def body(dres_ref, dxn_ref, x_ref, g_ref, dx_ref, dg_ref):
        xv = x_ref[...]
        r = lax.rsqrt(jnp.mean(xv * xv, axis=-1, keepdims=True) + RMS_EPS)
        xh = xv * r
        dn = dxn_ref[...]
        _accumulate(dg_ref, jnp.sum(dn * xh, axis=0, keepdims=True), pl.program_id(0) == 0)
        dxh = dn * g_ref[...]
        dx_ref[...] = dres_ref[...] + r * (dxh - xh * jnp.mean(dxh * xh, axis=-1, keepdims=True))

    return pl.pallas_call(
        body, name=f"pre_bwd_{tag}", grid=(T // tm,),
        in_specs=[_rows(tm, D), _rows(tm, D), _rows(tm, D), _whole((1, D))],
        out_specs=[_rows(tm, D), _whole((1, D))],
        out_shape=[_sds((T, D), F32), _sds((1, D), F32)],
        compiler_params=_params())(dres, dxn, x, g.reshape(1, D))


def _swiglu_fwd(hgu, tag):
    T, F2 = hgu.shape
    F = F2 // 2
    tm = _tile(T, ROW_TILE)

    def body(g_ref, u_ref, o_ref):
        g = g_ref[...].astype(F32)
        o_ref[...] = (g * jax.nn.sigmoid(g) * u_ref[...].astype(F32)).astype(o_ref.dtype)

    return pl.pallas_call(
        body, name=f"swiglu_fwd_{tag}", grid=(T // tm,),
        in_specs=[pl.BlockSpec((tm, F), lambda i: (i, 0)), pl.BlockSpec((tm, F), lambda i: (i, 1))],
        out_specs=_rows(tm, F), out_shape=_sds((T, F), MM_DTYPE), compiler_params=_params())(hgu, hgu)


def _swiglu_bwd(hgu, da, tag):
    T, F2 = hgu.shape
    F = F2 // 2
    tm = _tile(T, ROW_TILE)

    def body(h_ref, da_ref, o_ref):
        g = h_ref[:, :F].astype(F32)
        u = h_ref[:, F:].astype(F32)
        d = da_ref[...].astype(F32)
        sg = jax.nn.sigmoid(g)
        o_ref[:, :F] = (d * u * sg * (1.0 + g * (1.0 - sg))).astype(o_ref.dtype)
        o_ref[:, F:] = (d * g * sg).astype(o_ref.dtype)

    return pl.pallas_call(
        body, name=f"swiglu_bwd_{tag}", grid=(T // tm,),
        in_specs=[_rows(tm, F2), _rows(tm, F)], out_specs=_rows(tm, F2),
        out_shape=_sds((T, F2), MM_DTYPE), compiler_params=_params())(hgu, da)


def _loss_grad(y, tgt):
    T, D = y.shape
    tm = _tile(T, ROW_TILE)

    def body(y_ref, t_ref, dy_ref, l_ref):
        e = y_ref[...] - t_ref[...]
        row = jnp.mean(e * e, axis=-1, keepdims=True)
        part = jnp.broadcast_to(jnp.sum(row, axis=0, keepdims=True), (8, LANES))
        _accumulate(l_ref, part, pl.program_id(0) == 0)
        dy_ref[...] = e * (1.0 / D)

    dy, lsum = pl.pallas_call(
        body, name="loss_grad", grid=(T // tm,),
        in_specs=[_rows(tm, D), _rows(tm, D)], out_specs=[_rows(tm, D), _whole((8, LANES))],
        out_shape=[_sds((T, D), F32), _sds((8, LANES), F32)], compiler_params=_params())(y, tgt)
    return dy, 0.5 * lsum[0, 0]


def _shift_down(u, d, rows):
    return jnp.where(rows >= d, pltpu.roll(u, d, 0), 0.0)


def _shift_up(u, d, rows, S):
    return jnp.where(rows < S - d, pltpu.roll(u, S - d, 0), 0.0)


def _conv_fwd(bch, k8, Bl, S):
    T, D3 = bch.shape
    D = D3 // 3
    dc = min(D, 2 * LANES)
    nd = D // dc

    def body(b_ref, c_ref, h_ref, k_ref, z_ref):
        rows = lax.broadcasted_iota(jnp.int32, (S, 1), 0)
        u = c_ref[...].astype(F32) * h_ref[...].astype(F32)
        y = k_ref[2:3, :] * u + k_ref[1:2, :] * _shift_down(u, 1, rows) + k_ref[0:1, :] * _shift_down(u, 2, rows)
        z_ref[...] = (b_ref[...].astype(F32) * y).astype(z_ref.dtype)

    return pl.pallas_call(
        body, name="conv_fwd", grid=(Bl, nd),
        in_specs=[pl.BlockSpec((S, dc), lambda b, j: (b, j)),
                  pl.BlockSpec((S, dc), lambda b, j: (b, nd + j)),
                  pl.BlockSpec((S, dc), lambda b, j: (b, 2 * nd + j)),
                  pl.BlockSpec((8, dc), lambda b, j: (0, j))],
        out_specs=pl.BlockSpec((S, dc), lambda b, j: (b, j)),
        out_shape=_sds((T, D), MM_DTYPE), compiler_params=_params())(bch, bch, bch, k8)


def _conv_bwd(bch, dz, k8, Bl, S):
    T, D3 = bch.shape
    D = D3 // 3
    dc = min(D, 2 * LANES)
    nd = D // dc

    def body(b_ref, c_ref, h_ref, dz_ref, k_ref, db_ref, dc_ref, dh_ref, dk_ref):
        rows = lax.broadcasted_iota(jnp.int32, (S, 1), 0)
        bv = b_ref[...].astype(F32)
        cv = c_ref[...].astype(F32)
        hv = h_ref[...].astype(F32)
        dzv = dz_ref[...].astype(F32)
        u = cv * hv
        u1 = _shift_down(u, 1, rows)
        u2 = _shift_down(u, 2, rows)
        y = k_ref[2:3, :] * u + k_ref[1:2, :] * u1 + k_ref[0:1, :] * u2
        db_ref[...] = (dzv * y).astype(db_ref.dtype)
        dy = dzv * bv
        du = k_ref[2:3, :] * dy + k_ref[1:2, :] * _shift_up(dy, 1, rows, S) + k_ref[0:1, :] * _shift_up(dy, 2, rows, S)
        dc_ref[...] = (du * hv).astype(dc_ref.dtype)
        dh_ref[...] = (du * cv).astype(dh_ref.dtype)

        @pl.when(pl.program_id(1) == 0)
        def _():
            dk_ref[...] = jnp.zeros_like(dk_ref)

        dk_ref[0:1, :] += jnp.sum(dy * u2, axis=0, keepdims=True)
        dk_ref[1:2, :] += jnp.sum(dy * u1, axis=0, keepdims=True)
        dk_ref[2:3, :] += jnp.sum(dy * u, axis=0, keepdims=True)

    seq = lambda off: pl.BlockSpec((S, dc), lambda j, b: (b, off + j))
    return pl.pallas_call(
        body, name="conv_bwd", grid=(nd, Bl),
        in_specs=[seq(0), seq(nd), seq(2 * nd), seq(0), pl.BlockSpec((8, dc), lambda j, b: (0, j))],
        out_specs=[seq(0), seq(0), seq(0), pl.BlockSpec((8, dc), lambda j, b: (0, j))],
        out_shape=[_sds((T, D), MM_DTYPE)] * 3 + [_sds((8, D), F32)],
        compiler_params=_params())(bch, bch, bch, dz, k8)


def _forget_fwd(pf, fb, Bl, S):
    T = pf.shape[0]

    def body(p_ref, fb_ref, c_ref):
        rows = lax.broadcasted_iota(jnp.int32, (S, 1), 0)
        z = p_ref[...] + fb_ref[...]
        acc = jnp.minimum(z, 0.0) - jnp.log1p(jnp.exp(-jnp.abs(z)))
        d = 1
        while d < S:
            acc = acc + _shift_down(acc, d, rows)
            d *= 2
        c_ref[...] = acc

    return pl.pallas_call(
        body, name="forget_fwd", grid=(Bl,),
        in_specs=[_rows(S, LANES), _whole((1, LANES))], out_specs=_rows(S, LANES),
        out_shape=_sds((T, LANES), F32), compiler_params=_params())(pf, fb)


def _forget_bwd(dc, pf, fb, Bl, S):
    T = pf.shape[0]

    def body(dc_ref, p_ref, fb_ref, df_ref, dfb_ref):
        rows = lax.broadcasted_iota(jnp.int32, (S, 1), 0)
        acc = dc_ref[...]
        d = 1
        while d < S:
            acc = acc + _shift_up(acc, d, rows, S)
            d *= 2
        df = acc * jax.nn.sigmoid(-(p_ref[...] + fb_ref[...]))
        df_ref[...] = df.astype(df_ref.dtype)
        _accumulate(dfb_ref, jnp.sum(df, axis=0, keepdims=True), pl.program_id(0) == 0)

    return pl.pallas_call(
        body, name="forget_bwd", grid=(Bl,),
        in_specs=[_rows(S, LANES), _rows(S, LANES), _whole((1, LANES))],
        out_specs=[_rows(S, LANES), _whole((1, LANES))],
        out_shape=[_sds((T, LANES), MM_DTYPE), _sds((1, LANES), F32)],
        compiler_params=_params())(dc, pf, fb)


def _head_mask(h):
    lane = lax.broadcasted_iota(jnp.int32, (1, LANES), 1)
    return (lane >= h * HEAD_DIM) & (lane < (h + 1) * HEAD_DIM)


def _attn_fwd(qg, kv, c_col, c_row, Bl, S, D):
    T = Bl * S
    H = D // HEAD_DIM
    HP = D // LANES
    bq = min(S, ATT_BLOCK)
    nq = S // bq
    scale = 1.0 / math.sqrt(HEAD_DIM)

    def body(q_ref, k_ref, v_ref, cc_ref, cr_ref, o_ref, lse_ref):
        i = pl.program_id(2)
        q2 = q_ref[...]
        qh = [q2 * (_head_mask(h).astype(F32) * scale).astype(q2.dtype) for h in range(2)]
        cc = [cc_ref[h][:, :1] for h in range(2)]
        diag = lax.broadcasted_iota(jnp.int32, (1, bq), 1) <= lax.broadcasted_iota(jnp.int32, (bq, 1), 0)

        def block(j, carry, on_diagonal):
            off = pl.multiple_of(j * bq, bq)
            kj = k_ref[pl.ds(off, bq), :]
            vj = v_ref[pl.ds(off, bq), :]
            new = []
            for h in range(2):
                m, l, acc = carry[h]
                s = lax.dot_general(qh[h], kj, NT, preferred_element_type=F32) + cc[h] - cr_ref[h, j]
                if on_diagonal:
                    s = jnp.where(diag, s, -jnp.inf)
                m_new = jnp.maximum(m, jnp.max(s, axis=1, keepdims=True))
                p = jnp.exp(s - m_new)
                a = jnp.exp(m - m_new)
                l = a * l + jnp.sum(p, axis=1, keepdims=True)
                acc = a * acc + jnp.dot(p.astype(MM_DTYPE), vj, preferred_element_type=F32)
                new.append((m_new, l, acc))
            return tuple(new)

        one = (jnp.full((bq, 1), -jnp.inf, F32), jnp.zeros((bq, 1), F32), jnp.zeros((bq, LANES), F32))
        carry = lax.fori_loop(0, i, lambda j, c: block(j, c, False), (one, one))
        carry = block(i, carry, True)
        outs = []
        for h in range(2):
            m, l, acc = carry[h]
            outs.append(acc / l)
            lse_ref[h] = jnp.broadcast_to(m + jnp.log(l), (bq, LANES))
        o_ref[...] = jnp.where(_head_mask(0), outs[0], outs[1])

    return pl.pallas_call(
        body, name="attn_fwd", grid=(Bl, HP, nq),
        in_specs=[pl.BlockSpec((bq, LANES), lambda b, hp, i: (b * nq + i, hp)),
                  pl.BlockSpec((S, LANES), lambda b, hp, i: (b, hp)),
                  pl.BlockSpec((S, LANES), lambda b, hp, i: (b, HP + hp)),
                  pl.BlockSpec((None, 2, bq, LANES), lambda b, hp, i: (b, hp, i, 0)),
                  pl.BlockSpec((None, 2, nq, 1, bq), lambda b, hp, i: (b, hp, 0, 0, 0))],
        out_specs=[pl.BlockSpec((bq, LANES), lambda b, hp, i: (b * nq + i, hp)),
                   pl.BlockSpec((None, 2, bq, LANES), lambda b, hp, i: (b, hp, i, 0))],
        out_shape=[_sds((T, D), F32), _sds((Bl, H, S, LANES), F32)],
        compiler_params=_params())(qg, kv, kv, c_col, c_row)


def _attn_bwd(qg, kv, do, lse, c_col, c_row, Bl, S, D):
    T = Bl * S
    H = D // HEAD_DIM
    HP = D // LANES
    bq = min(S, ATT_BLOCK)
    nq = S // bq
    scale = 1.0 / math.sqrt(HEAD_DIM)

    def body(q_ref, k_ref, v_ref, do_ref, lse_ref, cc_ref, cr_ref, dq_ref, dk_ref, dv_ref, dcr_ref, p_sc, dp_sc):
        i = pl.program_id(2)

        @pl.when(i == 0)
        def _():
            dk_ref[...] = jnp.zeros_like(dk_ref)
            dv_ref[...] = jnp.zeros_like(dv_ref)
            dcr_ref[...] = jnp.zeros_like(dcr_ref)

        q2 = q_ref[...]
        do2 = do_ref[...]
        masks = [_head_mask(h).astype(F32) for h in range(2)]
        qh = [q2 * (masks[h] * scale).astype(q2.dtype) for h in range(2)]
        doh = [do2 * masks[h].astype(do2.dtype) for h in range(2)]
        cc = [cc_ref[h][:, :1] for h in range(2)]
        lse = [lse_ref[h][:, :1] for h in range(2)]
        diag = lax.broadcasted_iota(jnp.int32, (1, bq), 1) <= lax.broadcasted_iota(jnp.int32, (bq, 1), 0)

        def sweep1(j, delta, on_diagonal):
            off = pl.multiple_of(j * bq, bq)
            kj = k_ref[pl.ds(off, bq), :]
            vj = v_ref[pl.ds(off, bq), :]
            new = []
            dv = None
            for h in range(2):
                s = lax.dot_general(qh[h], kj, NT, preferred_element_type=F32) + cc[h] - cr_ref[h, j]
                if on_diagonal:
                    s = jnp.where(diag, s, -jnp.inf)
                p = jnp.exp(s - lse[h])
                dp = lax.dot_general(doh[h], vj, NT, preferred_element_type=F32)
                p_sc[h, j] = p
                dp_sc[h, j] = dp
                part = lax.dot_general(p.astype(MM_DTYPE), doh[h], TN, preferred_element_type=F32)
                dv = part if dv is None else dv + part
                new.append(delta[h] + jnp.sum(p * dp, axis=1, keepdims=True))
            dv_ref[pl.ds(off, bq), :] += dv
            return tuple(new)

        zero = jnp.zeros((bq, 1), F32)
        delta = lax.fori_loop(0, i, lambda j, d: sweep1(j, d, False), (zero, zero))
        delta = sweep1(i, delta, True)

        def sweep2(j, dq):
            off = pl.multiple_of(j * bq, bq)
            kj = k_ref[pl.ds(off, bq), :]
            dk = None
            for h in range(2):
                ds = p_sc[h, j] * (dp_sc[h, j] - delta[h])
                dcr_ref[h, j] -= jnp.sum(ds, axis=0, keepdims=True)
                dsb = ds.astype(MM_DTYPE)
                dq = dq + jnp.dot(dsb, kj * (masks[h] * scale).astype(kj.dtype), preferred_element_type=F32)
                part = lax.dot_general(dsb, qh[h], TN, preferred_element_type=F32)
                dk = part if dk is None else dk + part
            dk_ref[pl.ds(off, bq), :] += dk
            return dq

        dq_ref[...] = lax.fori_loop(0, i + 1, sweep2, jnp.zeros((bq, LANES), F32))

    blk = lambda col: pl.BlockSpec((bq, LANES), lambda b, hp, i: (b * nq + i, col(hp)))
    seq = lambda col: pl.BlockSpec((S, LANES), lambda b, hp, i: (b, col(hp)))
    per_head = pl.BlockSpec((None, 2, bq, LANES), lambda b, hp, i: (b, hp, i, 0))
    rows = pl.BlockSpec((None, 2, nq, 1, bq), lambda b, hp, i: (b, hp, 0, 0, 0))
    return pl.pallas_call(
        body, name="attn_bwd", grid=(Bl, HP, nq),
        in_specs=[blk(lambda hp: hp), seq(lambda hp: hp), seq(lambda hp: HP + hp), blk(lambda hp: hp),
                  per_head, per_head, rows],
        out_specs=[blk(lambda hp: hp), seq(lambda hp: hp), seq(lambda hp: hp), rows],
        out_shape=[_sds((T, D), F32), _sds((T, D), F32), _sds((T, D), F32), _sds((Bl, H, nq, 1, bq), F32)],
        scratch_shapes=[pltpu.VMEM((2, nq, bq, bq), F32), pltpu.VMEM((2, nq, bq, bq), F32)],
        compiler_params=_params())(qg, kv, kv, do, lse, c_col, c_row)


def _gate_fwd(qg, o):
    T, D = o.shape
    tm = _tile(T, ROW_TILE)

    def body(g_ref, o_ref, z_ref):
        z_ref[...] = (jax.nn.sigmoid(g_ref[...].astype(F32)) * o_ref[...]).astype(z_ref.dtype)

    return pl.pallas_call(
        body, name="gate_fwd", grid=(T // tm,),
        in_specs=[pl.BlockSpec((tm, D), lambda i: (i, 1)), _rows(tm, D)], out_specs=_rows(tm, D),
        out_shape=_sds((T, D), MM_DTYPE), compiler_params=_params())(qg, o)


def _gate_do(dz, qg):
    T, D = dz.shape
    tm = _tile(T, ROW_TILE)

    def body(dz_ref, g_ref, do_ref):
        do_ref[...] = (dz_ref[...].astype(F32) * jax.nn.sigmoid(g_ref[...].astype(F32))).astype(do_ref.dtype)

    return pl.pallas_call(
        body, name="gate_do", grid=(T // tm,),
        in_specs=[_rows(tm, D), pl.BlockSpec((tm, D), lambda i: (i, 1))], out_specs=_rows(tm, D),
        out_shape=_sds((T, D), MM_DTYPE), compiler_params=_params())(dz, qg)


def _gate_bwd(dz, qg, o, dq):
    T, D = dz.shape
    tm = _tile(T, ROW_TILE)

    def body(dz_ref, g_ref, o_ref, dq_ref, out_ref):
        g = g_ref[...].astype(F32)
        sg = jax.nn.sigmoid(g)
        out_ref[:, :D] = dq_ref[...].astype(out_ref.dtype)
        out_ref[:, D:] = (dz_ref[...].astype(F32) * o_ref[...] * sg * (1.0 - sg)).astype(out_ref.dtype)

    return pl.pallas_call(
        body, name="gate_bwd", grid=(T // tm,),
        in_specs=[_rows(tm, D), pl.BlockSpec((tm, D), lambda i: (i, 1)), _rows(tm, D), _rows(tm, D)],
        out_specs=_rows(tm, 2 * D), out_shape=_sds((T, 2 * D), MM_DTYPE),
        compiler_params=_params())(dz, qg, o, dq)


def _mm_in(a, wg, out_dtype, tag, l=None):
    T, K = a.shape
    n = wg.shape[-1]
    tm = _tile(T, MM_TILE)
    if l is None:
        w_spec = _whole((N_CHIP, K, n))
    else:
        w_spec = pl.BlockSpec((None, N_CHIP, K, n), lambda i: (l, 0, 0, 0))

    def body(a_ref, w_ref, o_ref):
        av = a_ref[...]
        for s in range(N_CHIP):
            o_ref[:, s * n:(s + 1) * n] = jnp.dot(av, w_ref[s], preferred_element_type=F32).astype(o_ref.dtype)

    return pl.pallas_call(
        body, name=f"mm_in_{tag}", grid=(T // tm,),
        in_specs=[_rows(tm, K), w_spec], out_specs=_rows(tm, N_CHIP * n),
        out_shape=_sds((T, N_CHIP * n), out_dtype), compiler_params=_params())(a, wg)


def _mm_nt_in(dy, wg, tag, l=None):
    T = dy.shape[0]
    K, n = wg.shape[-2:]
    tm = _tile(T, MM_TILE)
    if l is None:
        w_spec = _whole((N_CHIP, K, n))
    else:
        w_spec = pl.BlockSpec((None, N_CHIP, K, n), lambda i: (l, 0, 0, 0))

    def body(d_ref, w_ref, o_ref):
        acc = None
        for s in range(N_CHIP):
            part = lax.dot_general(d_ref[:, s * n:(s + 1) * n], w_ref[s], NT, preferred_element_type=F32)
            acc = part if acc is None else acc + part
        o_ref[...] = acc

    return pl.pallas_call(
        body, name=f"mm_nt_in_{tag}", grid=(T // tm,),
        in_specs=[_rows(tm, N_CHIP * n), w_spec], out_specs=_rows(tm, K),
        out_shape=_sds((T, K), F32), compiler_params=_params())(dy, wg)


def _mm_nn(a, b, out_dtype, tag):
    T, K = a.shape
    N = b.shape[1]
    tm = _tile(T, MM_TILE)

    def body(a_ref, b_ref, o_ref):
        o_ref[...] = jnp.dot(a_ref[...], b_ref[...], preferred_element_type=F32).astype(o_ref.dtype)

    return pl.pallas_call(
        body, name=f"mm_nn_{tag}", grid=(T // tm,),
        in_specs=[_rows(tm, K), _whole((K, N))], out_specs=_rows(tm, N),
        out_shape=_sds((T, N), out_dtype), compiler_params=_params())(a, b)


def _mm_nt(a, b, out_dtype, tag):
    T, C = a.shape
    N = b.shape[0]
    tm = _tile(T, MM_TILE)
    nb = N
    for cand in (1408, 1024):
        if N > cand and N % cand == 0:
            nb = cand
            break

    def body(a_ref, b_ref, o_ref):
        o_ref[...] = lax.dot_general(a_ref[...], b_ref[...], NT, preferred_element_type=F32).astype(o_ref.dtype)

    return pl.pallas_call(
        body, name=f"mm_nt_{tag}", grid=(N // nb, T // tm),
        in_specs=[pl.BlockSpec((tm, C), lambda j, i: (i, 0)), pl.BlockSpec((nb, C), lambda j, i: (j, 0))],
        out_specs=pl.BlockSpec((tm, nb), lambda j, i: (i, j)),
        out_shape=_sds((T, N), out_dtype), compiler_params=_params())(a, b)


def _mm_tn_in(a, dy, tag, l=None, prev=None):
    T, K = a.shape
    n = dy.shape[1] // N_CHIP
    tt = _tile(T, MM_TILE)

    def body(a_ref, d_ref, *rest):
        o_ref = rest[-1]
        part = lax.dot_general(a_ref[...], d_ref[...], TN, preferred_element_type=F32)
        _accumulate(o_ref, part, pl.program_id(1) == 0)

    in_specs = [pl.BlockSpec((tt, K), lambda s, t: (t, 0)), pl.BlockSpec((tt, n), lambda s, t: (t, s))]
    args = [a, dy]
    kw = {}
    if l is None:
        out_spec = pl.BlockSpec((None, K, n), lambda s, t: (s, 0, 0))
        out_shape = _sds((N_CHIP, K, n), F32)
    else:
        out_spec = pl.BlockSpec((None, None, K, n), lambda s, t: (l, s, 0, 0))
        out_shape = _sds((2, N_CHIP, K, n), F32)
        if prev is not None:
            in_specs.append(ANY)
            args.append(prev)
            kw["input_output_aliases"] = {2: 0}
    return pl.pallas_call(
        body, name=f"mm_tn_in_{tag}", grid=(N_CHIP, T // tt),
        in_specs=in_specs, out_specs=out_spec, out_shape=out_shape,
        compiler_params=_params(), **kw)(*args)


def _mm_tn_out(act, dh, tag, l=None, prev=None):
    T, R4 = act.shape
    D = dh.shape[1]
    r = R4 // N_CHIP
    g = 1 if r % LANES == 0 else 2
    tt = _tile(T, MM_TILE)

    def body(a_ref, d_ref, *rest):
        o_ref = rest[-1]
        part = lax.dot_general(a_ref[...], d_ref[...], TN, preferred_element_type=F32)
        first = pl.program_id(1) == 0
        for q in range(g):
            _accumulate(o_ref.at[q], part[q * r:(q + 1) * r], first)

    in_specs = [pl.BlockSpec((tt, g * r), lambda s, t: (t, s)), pl.BlockSpec((tt, D), lambda s, t: (t, 0))]
    args = [act, dh]
    kw = {}
    if l is None:
        out_spec = pl.BlockSpec((g, r, D), lambda s, t: (s, 0, 0))
        out_shape = _sds((N_CHIP, r, D), F32)
    else:
        out_spec = pl.BlockSpec((None, g, r, D), lambda s, t: (l, s, 0, 0))
        out_shape = _sds((2, N_CHIP, r, D), F32)
        if prev is not None:
            in_specs.append(ANY)
            args.append(prev)
            kw["input_output_aliases"] = {2: 0}
    return pl.pallas_call(
        body, name=f"mm_tn_out_{tag}", grid=(N_CHIP // g, T // tt),
        in_specs=in_specs, out_specs=out_spec, out_shape=out_shape,
        compiler_params=_params(), **kw)(*args)


def _mm_tn(a, b, tag):
    T, K = a.shape
    N = b.shape[1]
    tt = _tile(T, MM_TILE)

    def body(a_ref, b_ref, o_ref):
        part = lax.dot_general(a_ref[...], b_ref[...], TN, preferred_element_type=F32)
        _accumulate(o_ref, part, pl.program_id(0) == 0)

    return pl.pallas_call(
        body, name=f"mm_tn_{tag}", grid=(T // tt,),
        in_specs=[_rows(tt, K), _rows(tt, N)], out_specs=_whole((K, N)),
        out_shape=_sds((K, N), F32), compiler_params=_params())(a, b)


def _adamw(w, g, m, v, tag):
    R, C = w.shape
    tr = _tile(R, ROW_TILE)

    def body(w_ref, g_ref, m_ref, v_ref, d_ref, mo_ref, vo_ref):
        gv = g_ref[...]
        mn = ADAM_B1 * m_ref[...] + (1.0 - ADAM_B1) * gv
        vn = ADAM_B2 * v_ref[...] + (1.0 - ADAM_B2) * (gv * gv)
        m_hat = mn / (1.0 - ADAM_B1 ** ADAM_STEP)
        v_hat = vn / (1.0 - ADAM_B2 ** ADAM_STEP)
        d_ref[...] = -ADAM_LR * (m_hat / (jnp.sqrt(v_hat) + ADAM_EPS) + ADAM_WD * w_ref[...])
        mo_ref[...] = mn
        vo_ref[...] = vn

    return pl.pallas_call(
        body, name=f"adamw_{tag}", grid=(R // tr,),
        in_specs=[_rows(tr, C)] * 4, out_specs=[_rows(tr, C)] * 3,
        out_shape=[_sds((R, C), F32)] * 3, compiler_params=_params())(w, g, m, v)


def _sum_devices(gall):
    _, R, C = gall.shape

    def body(g_ref, o_ref):
        acc = g_ref[0]
        for d in range(1, N_DEV):
            acc = acc + g_ref[d]
        o_ref[...] = acc

    return pl.pallas_call(
        body, name="sum_devices", in_specs=[_whole((N_DEV, R, C))], out_specs=_whole((R, C)),
        out_shape=_sds((R, C), F32), grid=(1,), compiler_params=_params())(gall)


def _place():
    x, y, c = lax.axis_index("x"), lax.axis_index("y"), lax.axis_index("c")
    chips = ((1 - x, y), (x, 1 - y), (1 - x, 1 - y))
    return x, y, c, chips


def _blk(ref, chip, half, layer_major):
    return ref.at[half, chip] if layer_major else ref.at[chip, half]


def _remote(src, dst, send_sem, recv_sem, dev):
    return pltpu.make_async_remote_copy(src_ref=src, dst_ref=dst, send_sem=send_sem, recv_sem=recv_sem,
                                        device_id=dev, device_id_type=MESH)


def _own_slot(halves, layer_major, dtype, place, tag):
    _, r, col = halves.shape
    tr = _tile(r, 2 * ROW_TILE)

    def body(place_ref, x_ref, o_ref):
        o_ref[...] = x_ref[...].astype(o_ref.dtype)

    if layer_major:
        out_spec = pl.BlockSpec((None, None, tr, col), lambda h, i, p: (h, p[1], i, 0))
        shape = (2, N_CHIP, r, col)
    else:
        out_spec = pl.BlockSpec((None, None, tr, col), lambda h, i, p: (p[1], h, i, 0))
        shape = (N_CHIP, 2, r, col)
    grid_spec = pltpu.PrefetchScalarGridSpec(
        num_scalar_prefetch=1, grid=(2, r // tr),
        in_specs=[pl.BlockSpec((None, tr, col), lambda h, i, p: (h, i, 0))], out_specs=out_spec)
    return pl.pallas_call(
        body, name=f"own_slot_{tag}", grid_spec=grid_spec, out_shape=_sds(shape, dtype),
        compiler_params=_params())(place, halves)


def _allgather(bufs, layer_major):
    n = len(bufs)

    def body(*refs):
        outs = refs[n:2 * n]
        s_ici, r_ici, s_d2d, r_d2d = refs[2 * n:]
        x, y, c, chips = _place()
        me = 2 * x + y
        sib = (x, y, 1 - c)
        sends = []
        for i in range(n):
            mine = _blk(outs[i], me, c, layer_major[i])
            for j, (px, py) in enumerate(chips):
                cp = _remote(mine, mine, s_ici.at[i, j], r_ici.at[i, j], (px, py, c))
                cp.start()
                sends.append(cp)
        for j, (px, py) in enumerate(chips):
            src = 2 * px + py
            for i in range(n):
                blk = _blk(outs[i], src, c, layer_major[i])
                _remote(blk, blk, s_ici.at[i, j], r_ici.at[i, j], (px, py, c)).wait_recv()
                cp = _remote(blk, blk, s_d2d.at[i, j], r_d2d.at[i, j], sib)
                cp.start()
                sends.append(cp)
        for j, (px, py) in enumerate(chips):
            src = 2 * px + py
            for i in range(n):
                blk = _blk(outs[i], src, 1 - c, layer_major[i])
                _remote(blk, blk, s_d2d.at[i, j], r_d2d.at[i, j], sib).wait_recv()
        for cp in sends:
            cp.wait_send()

    dma = pltpu.SemaphoreType.DMA
    return pl.pallas_call(
        body, name="allgather_weights", in_specs=[ANY] * n, out_specs=[ANY] * n,
        out_shape=[_sds(b.shape, b.dtype) for b in bufs],
        input_output_aliases={i: i for i in range(n)},
        scratch_shapes=[dma((n, 3)), dma((n, 3)), dma((n, 3)), dma((n, 3))],
        )(*bufs)


def _rs_pair_send(grads, layer_major):
    n = len(grads)

    def body(*refs):
        ins, outs = refs[:n], refs[n:2 * n]
        s_sem, r_sem = refs[2 * n:]
        x, y, c, _ = _place()
        sib = (x, y, 1 - c)
        sends = []
        for i in range(n):
            src = ins[i].at[1 - c] if layer_major[i] else ins[i].at[:, 1 - c]
            cp = _remote(src, outs[i], s_sem.at[i], r_sem.at[i], sib)
            cp.start()
            sends.append(cp)
        for cp in sends:
            cp.wait()

    out_shape = [_sds((N_CHIP,) + g.shape[2:], g.dtype) for g in grads]
    dma = pltpu.SemaphoreType.DMA
    return pl.pallas_call(
        body, name="rs_pair_send", in_specs=[ANY] * n, out_specs=[ANY] * n, out_shape=out_shape,
        scratch_shapes=[dma((n,)), dma((n,))],
        )(*grads)


def _rs_pair_add(g, recv, layer_major, place, tag):
    r, col = g.shape[-2:]
    tr = _tile(r, ROW_TILE)

    def body(place_ref, g_ref, r_ref, wire_ref, own_ref):
        tot = g_ref[...] + r_ref[...]
        wire_ref[...] = tot.astype(wire_ref.dtype)

        @pl.when(pl.program_id(1) == place_ref[1])
        def _():
            own_ref[...] = tot

    if layer_major:
        g_spec = pl.BlockSpec((None, None, tr, col), lambda i, s, p: (p[0], s, i, 0))
    else:
        g_spec = pl.BlockSpec((None, None, tr, col), lambda i, s, p: (s, p[0], i, 0))
    grid_spec = pltpu.PrefetchScalarGridSpec(
        num_scalar_prefetch=1, grid=(r // tr, N_CHIP),
        in_specs=[g_spec, pl.BlockSpec((None, tr, col), lambda i, s, p: (s, i, 0))],
        out_specs=[pl.BlockSpec((None, tr, col), lambda i, s, p: (s, i, 0)),
                   pl.BlockSpec((tr, col), lambda i, s, p: (i, 0))])
    return pl.pallas_call(
        body, name=f"rs_pair_add_{tag}", grid_spec=grid_spec,
        out_shape=[_sds((N_CHIP, r, col), WIRE_DTYPE), _sds((r, col), F32)],
        compiler_params=_params())(place, g, recv)


def _rs_chip_send(wires, small):
    n = len(wires)

    def body(*refs):
        ins, small_ref = refs[:n], refs[n]
        outs, gall_ref = refs[n + 1:2 * n + 1], refs[2 * n + 1]
        s_sem, r_sem, s_small, r_small, s_loc = refs[2 * n + 2:]
        x, y, c, chips = _place()
        me = 4 * x + 2 * y + c
        sends = []
        loc = pltpu.make_async_copy(small_ref, gall_ref.at[me], s_loc)
        loc.start()
        for i in range(n):
            for j, (px, py) in enumerate(chips):
                cp = _remote(ins[i].at[2 * px + py], outs[i].at[j], s_sem.at[i, j], r_sem.at[i, j], (px, py, c))
                cp.start()
                sends.append(cp)
        for k in range(1, N_DEV):
            peer = (x ^ (k >> 2), y ^ ((k >> 1) & 1), c ^ (k & 1))
            cp = _remote(small_ref, gall_ref.at[me], s_small.at[k - 1], r_small.at[k - 1], peer)
            cp.start()
            sends.append(cp)
        for cp in sends:
            cp.wait()
        loc.wait()

    out_shape = [_sds((3,) + w.shape[1:], w.dtype) for w in wires] + [_sds((N_DEV,) + small.shape, small.dtype)]
    dma = pltpu.SemaphoreType.DMA
    return pl.pallas_call(
        body, name="rs_chip_send", in_specs=[ANY] * (n + 1), out_specs=[ANY] * (n + 1), out_shape=out_shape,
        scratch_shapes=[dma((n, 3)), dma((n, 3)), dma((N_DEV - 1,)), dma((N_DEV - 1,)), dma(())],
        )(*wires, small)


def _rs_chip_add(own, recv, place, tag):
    r, col = own.shape
    tr = _tile(r, ROW_TILE)

    def body(place_ref, o_ref, r_ref, out_ref):
        acc = o_ref[...]
        for j in range(3):
            acc = acc + r_ref[j].astype(F32)
        out_ref[...] = acc

    grid_spec = pltpu.PrefetchScalarGridSpec(
        num_scalar_prefetch=1, grid=(r // tr,),
        in_specs=[pl.BlockSpec((tr, col), lambda i, p: (i, 0)), pl.BlockSpec((3, tr, col), lambda i, p: (0, i, 0))],
        out_specs=pl.BlockSpec((None, tr, col), lambda i, p: (p[0], i, 0)))
    return pl.pallas_call(
        body, name=f"rs_chip_add_{tag}", grid_spec=grid_spec, out_shape=_sds((2, r, col), F32),
        compiler_params=_params())(place, own, recv)


def _rs_pair_share(fulls):
    n = len(fulls)

    def body(*refs):
        outs = refs[n:2 * n]
        s_sem, r_sem = refs[2 * n:]
        x, y, c, _ = _place()
        sib = (x, y, 1 - c)
        started = []
        for i in range(n):
            cp = _remote(outs[i].at[c], outs[i].at[c], s_sem.at[i], r_sem.at[i], sib)
            cp.start()
            started.append(cp)
        for i, cp in enumerate(started):
            cp.wait_send()
            _remote(outs[i].at[1 - c], outs[i].at[1 - c], s_sem.at[i], r_sem.at[i], sib).wait_recv()

    dma = pltpu.SemaphoreType.DMA
    return pl.pallas_call(
        body, name="rs_pair_share", in_specs=[ANY] * n, out_specs=[ANY] * n,
        out_shape=[_sds(f.shape, f.dtype) for f in fulls],
        input_output_aliases={i: i for i in range(n)},
        scratch_shapes=[dma((n,)), dma((n,))],
        )(*fulls)


def _ffn_fwd(x, g_pre, g_post, w_in, w_out, l, tag):
    xn = _rms_fwd(x, g_pre[l], tag)
    hgu = _mm_in(xn, w_in, MM_DTYPE, tag, l=l)
    act = _swiglu_fwd(hgu, tag)
    h = _mm_nn(act, w_out[l].reshape(-1, w_out.shape[-1]), F32, tag)
    return _post_fwd(x, h, g_post[l], 0.5, tag), (x, xn, hgu, act, h)


def _ffn_bwd(dx, saved, g_pre, g_post, w_in, w_out, l, tag, prev):
    x, xn, hgu, act, h = saved
    dh, dg_post = _post_bwd(dx, h, g_post[l], 0.5, tag)
    dact = _mm_nt(dh, w_out[l].reshape(-1, w_out.shape[-1]), MM_DTYPE, f"{tag}_out")
    dw_out = _mm_tn_out(act, dh, tag, l=l, prev=prev[1])
    dhgu = _swiglu_bwd(hgu, dact, tag)
    dxn = _mm_nt_in(dhgu, w_in, tag, l=l)
    dw_in = _mm_tn_in(xn, dhgu, tag, l=l, prev=prev[0])
    dx_in, dg_pre = _pre_bwd(dx, dxn, x, g_pre[l], tag)
    return dx_in, dg_pre, dg_post, (dw_in, dw_out)


def kernel(x, ffn1_pre_g, ffn1_post_g, ffn1_w_in, ffn1_w_out, mix_pre_g, mix_post_g, ffn2_pre_g, ffn2_post_g, ffn2_w_in, ffn2_w_out, conv_w_in, conv_k, conv_w_out, kv_g, kv_w, forget_b, attn_w_qg, attn_w_o, loss_target, m_ffn1_pre_g, m_ffn1_post_g, m_ffn1_w_in, m_ffn1_w_out, m_mix_pre_g, m_mix_post_g, m_ffn2_pre_g, m_ffn2_post_g, m_ffn2_w_in, m_ffn2_w_out, m_conv_w_in, m_conv_k, m_conv_w_out, m_kv_g, m_kv_w, m_forget_b, m_attn_w_qg, m_attn_w_o, v_ffn1_pre_g, v_ffn1_post_g, v_ffn1_w_in, v_ffn1_w_out, v_mix_pre_g, v_mix_post_g, v_ffn2_pre_g, v_ffn2_post_g, v_ffn2_w_in, v_ffn2_w_out, v_conv_w_in, v_conv_k, v_conv_w_out, v_kv_g, v_kv_w, v_forget_b, v_attn_w_qg, v_attn_w_o):
    Bl, S, D = x.shape
    T = Bl * S
    H = forget_b.shape[0]
    assert D == H * HEAD_DIM and D % LANES == 0
    kvc = kv_w.shape[1]
    kvp = -(-kvc // LANES) * LANES
    kv_all = 2 * D + LANES
    dk_cols = conv_k.shape[2]
    chip = 2 * lax.axis_index("x") + lax.axis_index("y")
    core = lax.axis_index("c")

    def halves(a3):
        L, r, col = a3.shape
        return (a3, True) if L == 2 else (a3.reshape(2, r // 2, col), False)

    big = {"ffn1_w_in": ffn1_w_in, "ffn1_w_out": ffn1_w_out, "ffn2_w_in": ffn2_w_in, "ffn2_w_out": ffn2_w_out,
           "conv_w_in": conv_w_in, "conv_w_out": conv_w_out,
           "kv_w": jnp.pad(kv_w, ((0, 0), (0, kvp - kvc)))[None],
           "attn_w_qg": attn_w_qg, "attn_w_o": attn_w_o}
    names = list(big)
    place = jnp.stack([core, chip]).astype(jnp.int32)
    send = [halves(big[k]) for k in names]
    send.append(halves(jnp.pad(conv_k[0], ((0, 13), (0, 0)))[None]))
    layer_major = [lm for _, lm in send]
    wire = [MM_DTYPE] * len(names) + [F32]
    slots = [_own_slot(a, lm, dt, place, k) for (a, lm), dt, k in zip(send, wire, names + ["conv_k"])]
    gathered = _allgather(slots, layer_major)
    W = {}
    for k, g, lm in zip(names, gathered, layer_major):
        W[k] = g if lm else g.reshape(N_CHIP, -1, g.shape[-1])
    k_taps = gathered[-1].reshape(N_CHIP, 16, dk_cols).transpose(1, 0, 2).reshape(16, D)[:8]
    kv_full = jnp.concatenate([W["kv_w"][s, :, :kvc] for s in range(N_CHIP)], axis=1)
    kv_full = jnp.pad(kv_full, ((0, 0), (0, kv_all - kv_full.shape[1])))
    w_o_conv = W["conv_w_out"].reshape(D, D)
    w_o_attn = W["attn_w_o"].reshape(D, D)
    fb = jnp.pad(forget_b, (0, LANES - H)).reshape(1, LANES)

    x0 = x.reshape(T, D)
    x1, s_f1a = _ffn_fwd(x0, ffn1_pre_g, ffn1_post_g, W["ffn1_w_in"], W["ffn1_w_out"], 0, "l0f1")
    xn_c = _rms_fwd(x1, mix_pre_g[0], "l0mix")
    bch = _mm_in(xn_c, W["conv_w_in"], MM_DTYPE, "conv")
    z_c = _conv_fwd(bch, k_taps, Bl, S)
    m_c = _mm_nn(z_c, w_o_conv, F32, "conv_out")
    x2 = _post_fwd(x1, m_c, mix_post_g[0], 1.0, "l0mix")
    x3, s_f2a = _ffn_fwd(x2, ffn2_pre_g, ffn2_post_g, W["ffn2_w_in"], W["ffn2_w_out"], 0, "l0f2")

    xn_kv = _rms_fwd(x3, kv_g, "kv")
    kvact = _mm_nn(xn_kv, kv_full[:, :2 * D], MM_DTYPE, "kv")
    pf = _mm_nn(xn_kv, kv_full[:, 2 * D:], F32, "forget")
    cum = _forget_fwd(pf, fb, Bl, S)
    bq = min(S, ATT_BLOCK)
    c3 = cum.reshape(Bl, S, LANES)[:, :, :H].transpose(0, 2, 1)
    c_col = jnp.broadcast_to(c3[..., None], (Bl, H, S, LANES))
    c_row = c3.reshape(Bl, H, S // bq, 1, bq)

    x4, s_f1b = _ffn_fwd(x3, ffn1_pre_g, ffn1_post_g, W["ffn1_w_in"], W["ffn1_w_out"], 1, "l1f1")
    xn_a = _rms_fwd(x4, mix_pre_g[1], "l1mix")
    qg = _mm_in(xn_a, W["attn_w_qg"], MM_DTYPE, "qg")
    o, lse = _attn_fwd(qg, kvact, c_col, c_row, Bl, S, D)
    z_a = _gate_fwd(qg, o)
    m_a = _mm_nn(z_a, w_o_attn, F32, "attn_out")
    x5 = _post_fwd(x4, m_a, mix_post_g[1], 1.0, "l1mix")
    x6, s_f2b = _ffn_fwd(x5, ffn2_pre_g, ffn2_post_g, W["ffn2_w_in"], W["ffn2_w_out"], 1, "l1f2")

    dy, loss_local = _loss_grad(x6, loss_target.reshape(T, D))
    loss = lax.psum(loss_local, ("x", "y", "c"))

    none2 = (None, None)
    dx5, dg_f2pre_1, dg_f2post_1, G_f2 = _ffn_bwd(dy, s_f2b, ffn2_pre_g, ffn2_post_g, W["ffn2_w_in"], W["ffn2_w_out"], 1, "l1f2", none2)
    dm_a, dg_mixpost_1 = _post_bwd(dx5, m_a, mix_post_g[1], 1.0, "l1mix")
    dz_a = _mm_nt(dm_a, w_o_attn, MM_DTYPE, "attn_out")
    G_attn_o = _mm_tn_out(z_a, dm_a, "attn_out")
    do = _gate_do(dz_a, qg)
    dq, dk, dv, dcr = _attn_bwd(qg, kvact, do, lse, c_col, c_row, Bl, S, D)
    dqg = _gate_bwd(dz_a, qg, o, dq)
    dxn_a = _mm_nt_in(dqg, W["attn_w_qg"], "qg")
    G_qg = _mm_tn_in(xn_a, dqg, "qg")
    dx4, dg_mixpre_1 = _pre_bwd(dx5, dxn_a, x4, mix_pre_g[1], "l1mix")
    dx3, dg_f1pre_1, dg_f1post_1, G_f1 = _ffn_bwd(dx4, s_f1b, ffn1_pre_g, ffn1_post_g, W["ffn1_w_in"], W["ffn1_w_out"], 1, "l1f1", none2)

    dcum = jnp.pad(dcr.reshape(Bl, H, S).transpose(0, 2, 1), ((0, 0), (0, 0), (0, LANES - H))).reshape(T, LANES)
    dpf, dfb = _forget_bwd(dcum, pf, fb, Bl, S)
    dp = jnp.concatenate([dk.astype(MM_DTYPE), dv.astype(MM_DTYPE), dpf], axis=1)
    dxn_kv = _mm_nt(dp, kv_full, F32, "kv")
    G_kv_full = _mm_tn(xn_kv, dp, "kv")
    dx3, dg_kv = _pre_bwd(dx3, dxn_kv, x3, kv_g, "kv")

    dx2, dg_f2pre_0, dg_f2post_0, G_f2 = _ffn_bwd(dx3, s_f2a, ffn2_pre_g, ffn2_post_g, W["ffn2_w_in"], W["ffn2_w_out"], 0, "l0f2", G_f2)
    dm_c, dg_mixpost_0 = _post_bwd(dx2, m_c, mix_post_g[0], 1.0, "l0mix")
    dz_c = _mm_nt(dm_c, w_o_conv, MM_DTYPE, "conv_out")
    G_conv_o = _mm_tn_out(z_c, dm_c, "conv_out")
    db, dcg, dhh, dk_taps = _conv_bwd(bch, dz_c, k_taps, Bl, S)
    dbch = jnp.concatenate([db, dcg, dhh], axis=1)
    dxn_c = _mm_nt_in(dbch, W["conv_w_in"], "conv")
    G_conv_in = _mm_tn_in(xn_c, dbch, "conv")
    dx1, dg_mixpre_0 = _pre_bwd(dx2, dxn_c, x1, mix_pre_g[0], "l0mix")
    dx0, dg_f1pre_0, dg_f1post_0, G_f1 = _ffn_bwd(dx1, s_f1a, ffn1_pre_g, ffn1_post_g, W["ffn1_w_in"], W["ffn1_w_out"], 0, "l0f1", G_f1)
    grad_x = dx0.reshape(Bl, S, D)

    G_kv = jnp.stack([jnp.pad(G_kv_full[:, s * kvc:(s + 1) * kvc], ((0, 0), (0, kvp - kvc))) for s in range(N_CHIP)])
    G = {"ffn1_w_in": G_f1[0], "ffn1_w_out": G_f1[1], "ffn2_w_in": G_f2[0], "ffn2_w_out": G_f2[1],
         "conv_w_in": G_conv_in, "conv_w_out": G_conv_o, "kv_w": G_kv, "attn_w_qg": G_qg, "attn_w_o": G_attn_o}
    grads = []
    for k, lm in zip(names, layer_major):
        g = G[k]
        grads.append(g if lm else g.reshape(N_CHIP, 2, g.shape[1] // 2, g.shape[2]))
    lms = layer_major[:len(names)]
    from_sibling = _rs_pair_send(grads, lms)
    wires, owns = [], []
    for k, g, r, lm in zip(names, grads, from_sibling, lms):
        w, own = _rs_pair_add(g, r, lm, place, k)
        wires.append(w)
        owns.append(own)

    def row(v):
        return jnp.pad(v.reshape(-1), (0, D - v.size)).reshape(1, D)

    small_parts = [dg_f1pre_0, dg_f1pre_1, dg_f1post_0, dg_f1post_1, dg_mixpre_0, dg_mixpre_1, dg_mixpost_0, dg_mixpost_1,
                   dg_f2pre_0, dg_f2pre_1, dg_f2post_0, dg_f2post_1, dg_kv, row(dfb[0, :H]), dk_taps[:3]]
    small = jnp.concatenate(small_parts, axis=0)
    small = jnp.pad(small, ((0, SMALL_ROWS - small.shape[0]), (0, 0)))
    outs = _rs_chip_send(wires, small)
    recvs, gall = outs[:-1], outs[-1]
    reduced = _rs_pair_share([_rs_chip_add(own, r, place, k) for k, own, r in zip(names, owns, recvs)])
    gsum = _sum_devices(gall)

    given = dict(ffn1_w_in=(ffn1_w_in, m_ffn1_w_in, v_ffn1_w_in), ffn1_w_out=(ffn1_w_out, m_ffn1_w_out, v_ffn1_w_out),
                 ffn2_w_in=(ffn2_w_in, m_ffn2_w_in, v_ffn2_w_in), ffn2_w_out=(ffn2_w_out, m_ffn2_w_out, v_ffn2_w_out),
                 conv_w_in=(conv_w_in, m_conv_w_in, v_conv_w_in), conv_w_out=(conv_w_out, m_conv_w_out, v_conv_w_out),
                 kv_w=(kv_w, m_kv_w, v_kv_w), attn_w_qg=(attn_w_qg, m_attn_w_qg, v_attn_w_qg),
                 attn_w_o=(attn_w_o, m_attn_w_o, v_attn_w_o))
    res = {}
    for k, red in zip(names, reduced):
        w, m, v = given[k]
        g2 = red.reshape(-1, red.shape[-1])
        if k == "kv_w":
            g2 = g2[:, :kvc]
        flat = lambda a: a.reshape(-1, a.shape[-1])
        d, mn, vn = _adamw(flat(w), g2, flat(m), flat(v), k)
        res[k] = tuple(a.reshape(w.shape) for a in (g2, d, mn, vn))

    small_names = ["ffn1_pre_g", "ffn1_post_g", "mix_pre_g", "mix_post_g", "ffn2_pre_g", "ffn2_post_g"]
    small_given = dict(ffn1_pre_g=(ffn1_pre_g, m_ffn1_pre_g, v_ffn1_pre_g), ffn1_post_g=(ffn1_post_g, m_ffn1_post_g, v_ffn1_post_g),
                       mix_pre_g=(mix_pre_g, m_mix_pre_g, v_mix_pre_g), mix_post_g=(mix_post_g, m_mix_post_g, v_mix_post_g),
                       ffn2_pre_g=(ffn2_pre_g, m_ffn2_pre_g, v_ffn2_pre_g), ffn2_post_g=(ffn2_post_g, m_ffn2_post_g, v_ffn2_post_g))

    def pack(idx):
        rows_ = [small_given[k][idx] for k in small_names]
        rows_ += [row((kv_g, m_kv_g, v_kv_g)[idx]), row((forget_b, m_forget_b, v_forget_b)[idx])]
        rows_.append(jnp.pad((conv_k, m_conv_k, v_conv_k)[idx][0], ((0, 0), (0, D - dk_cols))))
        a = jnp.concatenate(rows_, axis=0)
        return jnp.pad(a, ((0, SMALL_ROWS - a.shape[0]), (0, 0)))

    g_taps = lax.dynamic_slice_in_dim(gsum[14:17], chip * dk_cols, dk_cols, axis=1)
    g_small = jnp.concatenate([gsum[:14], jnp.pad(g_taps, ((0, 0), (0, D - dk_cols))), gsum[17:]], axis=0)
    d_s, m_s, v_s = _adamw(pack(0), g_small, pack(1), pack(2), "small")
    for i, k in enumerate(small_names):
        res[k] = tuple(a[2 * i:2 * i + 2] for a in (g_small, d_s, m_s, v_s))
    res["kv_g"] = tuple(a[12] for a in (g_small, d_s, m_s, v_s))
    res["forget_b"] = tuple(a[13, :H] for a in (g_small, d_s, m_s, v_s))
    res["conv_k"] = tuple(a[14:17, :dk_cols][None] for a in (g_small, d_s, m_s, v_s))

    order = ["ffn1_pre_g", "ffn1_post_g", "ffn1_w_in", "ffn1_w_out", "mix_pre_g", "mix_post_g", "ffn2_pre_g", "ffn2_post_g",
             "ffn2_w_in", "ffn2_w_out", "conv_w_in", "conv_k", "conv_w_out", "kv_g", "kv_w", "forget_b", "attn_w_qg", "attn_w_o"]
    out = [loss, grad_x]
    for idx in range(4):
        out += [res[k][idx] for k in order]
    return tuple(out)
```

```python
import functools
import math

import jax
import jax.numpy as jnp
from jax import lax
from jax.experimental import pallas as pl
from jax.experimental.pallas import tpu as pltpu

F32 = jnp.float32
MM_DTYPE = jnp.bfloat16
WIRE_DTYPE = jnp.bfloat16

RMS_EPS = 1e-6
ADAM_LR = 0.001
ADAM_B1 = 0.9
ADAM_B2 = 0.999
ADAM_EPS = 1e-08
ADAM_WD = 0.01
ADAM_STEP = 10

HEAD_DIM = 64
LANES = 128
N_CHIP = 4
N_DEV = 8
ROW_TILE = 256
MM_TILE = 512
ATT_BLOCK = 512
SMALL_ROWS = 24
VMEM_LIMIT = 56 * 1024 * 1024
MESH = pl.DeviceIdType.MESH
ANY = pl.BlockSpec(memory_space=pl.ANY)

NT = (((1,), (1,)), ((), ()))
TN = (((0,), (0,)), ((), ()))


def _tile(n, pref):
    if n <= pref:
        return n
    t = pref - pref % 16
    while n % t:
        t -= 16
    return t


def _params():
    return pltpu.CompilerParams(vmem_limit_bytes=VMEM_LIMIT)


def _sds(shape, dtype):
    return jax.ShapeDtypeStruct(shape, dtype)


def _rows(tm, c):
    return pl.BlockSpec((tm, c), lambda i: (i, 0))


def _whole(shape):
    return pl.BlockSpec(shape, lambda *_: (0,) * len(shape))


def _rms_fwd(x, g, tag):
    T, D = x.shape
    tm = _tile(T, ROW_TILE)

    def body(x_ref, g_ref, o_ref):
        xv = x_ref[...]
        r = lax.rsqrt(jnp.mean(xv * xv, axis=-1, keepdims=True) + RMS_EPS)
        o_ref[...] = (xv * r * g_ref[...]).astype(o_ref.dtype)

    return pl.pallas_call(
        body, name=f"rms_fwd_{tag}", grid=(T // tm,),
        in_specs=[_rows(tm, D), _whole((1, D))], out_specs=_rows(tm, D),
        out_shape=_sds((T, D), MM_DTYPE), compiler_params=_params())(x, g.reshape(1, D))


def _post_fwd(x, h, g, alpha, tag):
    T, D = x.shape
    tm = _tile(T, ROW_TILE)

    def body(x_ref, h_ref, g_ref, o_ref):
        hv = h_ref[...]
        r = lax.rsqrt(jnp.mean(hv * hv, axis=-1, keepdims=True) + RMS_EPS)
        o_ref[...] = x_ref[...] + alpha * (hv * r * g_ref[...])

    return pl.pallas_call(
        body, name=f"post_fwd_{tag}", grid=(T // tm,),
        in_specs=[_rows(tm, D), _rows(tm, D), _whole((1, D))], out_specs=_rows(tm, D),
        out_shape=_sds((T, D), F32), compiler_params=_params())(x, h, g.reshape(1, D))


def _accumulate(ref, part, first):
    @pl.when(first)
    def _():
        ref[...] = part

    @pl.when(jnp.logical_not(first))
    def _():
        ref[...] += part


def _post_bwd(dx, h, g, alpha, tag):
    T, D = dx.shape
    tm = _tile(T, ROW_TILE)

    def body(dx_ref, h_ref, g_ref, dh_ref, dg_ref):
        hv = h_ref[...]
        r = lax.rsqrt(jnp.mean(hv * hv, axis=-1, keepdims=True) + RMS_EPS)
        hh = hv * r
        dyn = alpha * dx_ref[...]
        _accumulate(dg_ref, jnp.sum(dyn * hh, axis=0, keepdims=True), pl.program_id(0) == 0)
        dhh = dyn * g_ref[...]
        dh = r * (dhh - hh * jnp.mean(dhh * hh, axis=-1, keepdims=True))
        dh_ref[...] = dh.astype(dh_ref.dtype)

    return pl.pallas_call(
        body, name=f"post_bwd_{tag}", grid=(T // tm,),
        in_specs=[_rows(tm, D), _rows(tm, D), _whole((1, D))],
        out_specs=[_rows(tm, D), _whole((1, D))],
        out_shape=[_sds((T, D), MM_DTYPE), _sds((1, D), F32)],
        compiler_params=_params())(dx, h, g.reshape(1, D))


def _pre_bwd(dres, dxn, x, g, tag):
    T, D = x.shape
    tm = _tile(T, ROW_TILE)

    def body(dres_ref, dxn_ref, x_ref, g_ref, dx_ref, dg_ref):
        xv = x_ref[...]
        r = lax.rsqrt(jnp.mean(xv * xv, axis=-1, keepdims=True) + RMS_EPS)
        xh = xv * r
        dn = dxn_ref[...]
        _accumulate(dg_ref, jnp.sum(dn * xh, axis=0, keepdims=True), pl.program_id(0) == 0)
        dxh = dn * g_ref[...]
        dx_ref[...] = dres_ref[...] + r * (dxh - xh * jnp.mean(dxh * xh, axis=-1, keepdims=True))

    return pl.pallas_call(
        body, name=f"pre_bwd_{tag}", grid=(T // tm,),
        in_specs=[_rows(tm, D), _rows(tm, D), _rows(tm, D), _whole((1, D))],
        out_specs=[_rows(tm, D), _whole((1, D))],
        out_shape=[_sds((T, D), F32), _sds((1, D), F32)],
        compiler_params=_params())(dres, dxn, x, g.reshape(1, D))


def _swiglu_fwd(hgu, tag):
    T, F2 = hgu.shape
    F = F2 // 2
    tm = _tile(T, ROW_TILE)

    def body(g_ref, u_ref, o_ref):
        g = g_ref[...].astype(F32)
        o_ref[...] = (g * jax.nn.sigmoid(g) * u_ref[...].astype(F32)).astype(o_ref.dtype)

    return pl.pallas_call(
        body, name=f"swiglu_fwd_{tag}", grid=(T // tm,),
        in_specs=[pl.BlockSpec((tm, F), lambda i: (i, 0)), pl.BlockSpec((tm, F), lambda i: (i, 1))],
        out_specs=_rows(tm, F), out_shape=_sds((T, F), MM_DTYPE), compiler_params=_params())(hgu, hgu)


def _swiglu_bwd(hgu, da, tag):
    T, F2 = hgu.shape
    F = F2 // 2
    tm = _tile(T, ROW_TILE)

    def body(h_ref, da_ref, o_ref):
        g = h_ref[:, :F].astype(F32)
        u = h_ref[:, F:].astype(F32)
        d = da_ref[...].astype(F32)
        sg = jax.nn.sigmoid(g)
        o_ref[:, :F] = (d * u * sg * (1.0 + g * (1.0 - sg))).astype(o_ref.dtype)
        o_ref[:, F:] = (d * g * sg).astype(o_ref.dtype)

    return pl.pallas_call(
        body, name=f"swiglu_bwd_{tag}", grid=(T // tm,),
        in_specs=[_rows(tm, F2), _rows(tm, F)], out_specs=_rows(tm, F2),
        out_shape=_sds((T, F2), MM_DTYPE), compiler_params=_params())(hgu, da)


def _loss_grad(y, tgt):
    T, D = y.shape
    tm = _tile(T, ROW_TILE)

    def body(y_ref, t_ref, dy_ref, l_ref):
        e = y_ref[...] - t_ref[...]
        row = jnp.mean(e * e, axis=-1, keepdims=True)
        part = jnp.broadcast_to(jnp.sum(row, axis=0, keepdims=True), (8, LANES))
        _accumulate(l_ref, part, pl.program_id(0) == 0)
        dy_ref[...] = e * (1.0 / D)

    dy, lsum = pl.pallas_call(
        body, name="loss_grad", grid=(T // tm,),
        in_specs=[_rows(tm, D), _rows(tm, D)], out_specs=[_rows(tm, D), _whole((8, LANES))],
        out_shape=[_sds((T, D), F32), _sds((8, LANES), F32)], compiler_params=_params())(y, tgt)
    return dy, 0.5 * lsum[0, 0]


def _shift_down(u, d, rows):
    return jnp.where(rows >= d, pltpu.roll(u, d, 0), 0.0)


def _shift_up(u, d, rows, S):
    return jnp.where(rows < S - d, pltpu.roll(u, S - d, 0), 0.0)


def _conv_fwd(bch, k8, Bl, S):
    T, D3 = bch.shape
    D = D3 // 3
    dc = min(D, 2 * LANES)
    nd = D // dc

    def body(b_ref, c_ref, h_ref, k_ref, z_ref):
        rows = lax.broadcasted_iota(jnp.int32, (S, 1), 0)
        u = c_ref[...].astype(F32) * h_ref[...].astype(F32)
        y = k_ref[2:3, :] * u + k_ref[1:2, :] * _shift_down(u, 1, rows) + k_ref[0:1, :] * _shift_down(u, 2, rows)
        z_ref[...] = (b_ref[...].astype(F32) * y).astype(z_ref.dtype)

    return pl.pallas_call(
        body, name="conv_fwd", grid=(Bl, nd),
        in_specs=[pl.BlockSpec((S, dc), lambda b, j: (b, j)),
                  pl.BlockSpec((S, dc), lambda b, j: (b, nd + j)),
                  pl.BlockSpec((S, dc), lambda b, j: (b, 2 * nd + j)),
                  pl.BlockSpec((8, dc), lambda b, j: (0, j))],
        out_specs=pl.BlockSpec((S, dc), lambda b, j: (b, j)),
        out_shape=_sds((T, D), MM_DTYPE), compiler_params=_params())(bch, bch, bch, k8)


def _conv_bwd(bch, dz, k8, Bl, S):
    T, D3 = bch.shape
    D = D3 // 3
    dc = min(D, 2 * LANES)
    nd = D // dc

    def body(b_ref, c_ref, h_ref, dz_ref, k_ref, db_ref, dc_ref, dh_ref, dk_ref):
        rows = lax.broadcasted_iota(jnp.int32, (S, 1), 0)
        bv = b_ref[...].astype(F32)
        cv = c_ref[...].astype(F32)
        hv = h_ref[...].astype(F32)
        dzv = dz_ref[...].astype(F32)
        u = cv * hv
        u1 = _shift_down(u, 1, rows)
        u2 = _shift_down(u, 2, rows)
        y = k_ref[2:3, :] * u + k_ref[1:2, :] * u1 + k_ref[0:1, :] * u2
        db_ref[...] = (dzv * y).astype(db_ref.dtype)
        dy = dzv * bv
        du = k_ref[2:3, :] * dy + k_ref[1:2, :] * _shift_up(dy, 1, rows, S) + k_ref[0:1, :] * _shift_up(dy, 2, rows, S)
        dc_ref[...] = (du * hv).astype(dc_ref.dtype)
        dh_ref[...] = (du * cv).astype(dh_ref.dtype)

        @pl.when(pl.program_id(1) == 0)
        def _():
            dk_ref[...] = jnp.zeros_like(dk_ref)

        dk_ref[0:1, :] += jnp.sum(dy * u2, axis=0, keepdims=True)
        dk_ref[1:2, :] += jnp.sum(dy * u1, axis=0, keepdims=True)
        dk_ref[2:3, :] += jnp.sum(dy * u, axis=0, keepdims=True)

    seq = lambda off: pl.BlockSpec((S, dc), lambda j, b: (b, off + j))
    return pl.pallas_call(
        body, name="conv_bwd", grid=(nd, Bl),
        in_specs=[seq(0), seq(nd), seq(2 * nd), seq(0), pl.BlockSpec((8, dc), lambda j, b: (0, j))],
        out_specs=[seq(0), seq(0), seq(0), pl.BlockSpec((8, dc), lambda j, b: (0, j))],
        out_shape=[_sds((T, D), MM_DTYPE)] * 3 + [_sds((8, D), F32)],
        compiler_params=_params())(bch, bch, bch, dz, k8)


def _forget_fwd(pf, fb, Bl, S):
    T = pf.shape[0]

    def body(p_ref, fb_ref, c_ref):
        rows = lax.broadcasted_iota(jnp.int32, (S, 1), 0)
        z = p_ref[...] + fb_ref[...]
        acc = jnp.minimum(z, 0.0) - jnp.log1p(jnp.exp(-jnp.abs(z)))
        d = 1
        while d < S:
            acc = acc + _shift_down(acc, d, rows)
            d *= 2
        c_ref[...] = acc

    return pl.pallas_call(
        body, name="forget_fwd", grid=(Bl,),
        in_specs=[_rows(S, LANES), _whole((1, LANES))], out_specs=_rows(S, LANES),
        out_shape=_sds((T, LANES), F32), compiler_params=_params())(pf, fb)


def _forget_bwd(dc, pf, fb, Bl, S):
    T = pf.shape[0]

    def body(dc_ref, p_ref, fb_ref, df_ref, dfb_ref):
        rows = lax.broadcasted_iota(jnp.int32, (S, 1), 0)
        acc = dc_ref[...]
        d = 1
        while d < S:
            acc = acc + _shift_up(acc, d, rows, S)
            d *= 2
        df = acc * jax.nn.sigmoid(-(p_ref[...] + fb_ref[...]))
        df_ref[...] = df.astype(df_ref.dtype)
        _accumulate(dfb_ref, jnp.sum(df, axis=0, keepdims=True), pl.program_id(0) == 0)

    return pl.pallas_call(
        body, name="forget_bwd", grid=(Bl,),
        in_specs=[_rows(S, LANES), _rows(S, LANES), _whole((1, LANES))],
        out_specs=[_rows(S, LANES), _whole((1, LANES))],
        out_shape=[_sds((T, LANES), MM_DTYPE), _sds((1, LANES), F32)],
        compiler_params=_params())(dc, pf, fb)


def _head_mask(h):
    lane = lax.broadcasted_iota(jnp.int32, (1, LANES), 1)
    return (lane >= h * HEAD_DIM) & (lane < (h + 1) * HEAD_DIM)


def _attn_fwd(qg, kv, c_col, c_row, Bl, S, D):
    T = Bl * S
    H = D // HEAD_DIM
    HP = D // LANES
    bq = min(S, ATT_BLOCK)
    nq = S // bq
    scale = 1.0 / math.sqrt(HEAD_DIM)

    def body(q_ref, k_ref, v_ref, cc_ref, cr_ref, o_ref, lse_ref):
        i = pl.program_id(2)
        q2 = q_ref[...]
        qh = [q2 * (_head_mask(h).astype(F32) * scale).astype(q2.dtype) for h in range(2)]
        cc = [cc_ref[h][:, :1] for h in range(2)]
        diag = lax.broadcasted_iota(jnp.int32, (1, bq), 1) <= lax.broadcasted_iota(jnp.int32, (bq, 1), 0)

        def block(j, carry, on_diagonal):
            off = pl.multiple_of(j * bq, bq)
            kj = k_ref[pl.ds(off, bq), :]
            vj = v_ref[pl.ds(off, bq), :]
            new = []
            for h in range(2):
                m, l, acc = carry[h]
                s = lax.dot_general(qh[h], kj, NT, preferred_element_type=F32) + cc[h] - cr_ref[h, j]
                if on_diagonal:
                    s = jnp.where(diag, s, -jnp.inf)
                m_new = jnp.maximum(m, jnp.max(s, axis=1, keepdims=True))
                p = jnp.exp(s - m_new)
                a = jnp.exp(m - m_new)
                l = a * l + jnp.sum(p, axis=1, keepdims=True)
                acc = a * acc + jnp.dot(p.astype(MM_DTYPE), vj, preferred_element_type=F32)
                new.append((m_new, l, acc))
            return tuple(new)

        one = (jnp.full((bq, 1), -jnp.inf, F32), jnp.zeros((bq, 1), F32), jnp.zeros((bq, LANES), F32))
        carry = lax.fori_loop(0, i, lambda j, c: block(j, c, False), (one, one))
        carry = block(i, carry, True)
        outs = []
        for h in range(2):
            m, l, acc = carry[h]
            outs.append(acc / l)
            lse_ref[h] = jnp.broadcast_to(m + jnp.log(l), (bq, LANES))
        o_ref[...] = jnp.where(_head_mask(0), outs[0], outs[1])

    return pl.pallas_call(
        body, name="attn_fwd", grid=(Bl, HP, nq),
        in_specs=[pl.BlockSpec((bq, LANES), lambda b, hp, i: (b * nq + i, hp)),
                  pl.BlockSpec((S, LANES), lambda b, hp, i: (b, hp)),
                  pl.BlockSpec((S, LANES), lambda b, hp, i: (b, HP + hp)),
                  pl.BlockSpec((None, 2, bq, LANES), lambda b, hp, i: (b, hp, i, 0)),
                  pl.BlockSpec((None, 2, nq, 1, bq), lambda b, hp, i: (b, hp, 0, 0, 0))],
        out_specs=[pl.BlockSpec((bq, LANES), lambda b, hp, i: (b * nq + i, hp)),
                   pl.BlockSpec((None, 2, bq, LANES), lambda b, hp, i: (b, hp, i, 0))],
        out_shape=[_sds((T, D), F32), _sds((Bl, H, S, LANES), F32)],
        compiler_params=_params())(qg, kv, kv, c_col, c_row)


def _attn_bwd(qg, kv, do, lse, c_col, c_row, Bl, S, D):
    T = Bl * S
    H = D // HEAD_DIM
    HP = D // LANES
    bq = min(S, ATT_BLOCK)
    nq = S // bq
    scale = 1.0 / math.sqrt(HEAD_DIM)

    def body(q_ref, k_ref, v_ref, do_ref, lse_ref, cc_ref, cr_ref, dq_ref, dk_ref, dv_ref, dcr_ref, p_sc, dp_sc):
        i = pl.program_id(2)

        @pl.when(i == 0)
        def _():
            dk_ref[...] = jnp.zeros_like(dk_ref)
            dv_ref[...] = jnp.zeros_like(dv_ref)
            dcr_ref[...] = jnp.zeros_like(dcr_ref)

        q2 = q_ref[...]
        do2 = do_ref[...]
        masks = [_head_mask(h).astype(F32) for h in range(2)]
        qh = [q2 * (masks[h] * scale).astype(q2.dtype) for h in range(2)]
        doh = [do2 * masks[h].astype(do2.dtype) for h in range(2)]
        cc = [cc_ref[h][:, :1] for h in range(2)]
        lse = [lse_ref[h][:, :1] for h in range(2)]
        diag = lax.broadcasted_iota(jnp.int32, (1, bq), 1) <= lax.broadcasted_iota(jnp.int32, (bq, 1), 0)

        def sweep1(j, delta, on_diagonal):
            off = pl.multiple_of(j * bq, bq)
            kj = k_ref[pl.ds(off, bq), :]
            vj = v_ref[pl.ds(off, bq), :]
            new = []
            dv = None
            for h in range(2):
                s = lax.dot_general(qh[h], kj, NT, preferred_element_type=F32) + cc[h] - cr_ref[h, j]
                if on_diagonal:
                    s = jnp.where(diag, s, -jnp.inf)
                p = jnp.exp(s - lse[h])
                dp = lax.dot_general(doh[h], vj, NT, preferred_element_type=F32)
                p_sc[h, j] = p
                dp_sc[h, j] = dp
                part = lax.dot_general(p.astype(MM_DTYPE), doh[h], TN, preferred_element_type=F32)
                dv = part if dv is None else dv + part
                new.append(delta[h] + jnp.sum(p * dp, axis=1, keepdims=True))
            dv_ref[pl.ds(off, bq), :] += dv
            return tuple(new)

        zero = jnp.zeros((bq, 1), F32)
        delta = lax.fori_loop(0, i, lambda j, d: sweep1(j, d, False), (zero, zero))
        delta = sweep1(i, delta, True)

        def sweep2(j, dq):
            off = pl.multiple_of(j * bq, bq)
            kj = k_ref[pl.ds(off, bq), :]
            dk = None
            for h in range(2):
                ds = p_sc[h, j] * (dp_sc[h, j] - delta[h])
                dcr_ref[h, j] -= jnp.sum(ds, axis=0, keepdims=True)
                dsb = ds.astype(MM_DTYPE)
                dq = dq + jnp.dot(dsb, kj * (masks[h] * scale).astype(kj.dtype), preferred_element_type=F32)
                part = lax.dot_general(dsb, qh[h], TN, preferred_element_type=F32)
                dk = part if dk is None else dk + part
            dk_ref[pl.ds(off, bq), :] += dk
            return dq

        dq_ref[...] = lax.fori_loop(0, i + 1, sweep2, jnp.zeros((bq, LANES), F32))

    blk = lambda col: pl.BlockSpec((bq, LANES), lambda b, hp, i: (b * nq + i, col(hp)))
    seq = lambda col: pl.BlockSpec((S, LANES), lambda b, hp, i: (b, col(hp)))
    per_head = pl.BlockSpec((None, 2, bq, LANES), lambda b, hp, i: (b, hp, i, 0))
    rows = pl.BlockSpec((None, 2, nq, 1, bq), lambda b, hp, i: (b, hp, 0, 0, 0))
    return pl.pallas_call(
        body, name="attn_bwd", grid=(Bl, HP, nq),
        in_specs=[blk(lambda hp: hp), seq(lambda hp: hp), seq(lambda hp: HP + hp), blk(lambda hp: hp),
                  per_head, per_head, rows],
        out_specs=[blk(lambda hp: hp), seq(lambda hp: hp), seq(lambda hp: hp), rows],
        out_shape=[_sds((T, D), F32), _sds((T, D), F32), _sds((T, D), F32), _sds((Bl, H, nq, 1, bq), F32)],
        scratch_shapes=[pltpu.VMEM((2, nq, bq, bq), F32), pltpu.VMEM((2, nq, bq, bq), F32)],
        compiler_params=_params())(qg, kv, kv, do, lse, c_col, c_row)


def _gate_fwd(qg, o):
    T, D = o.shape
    tm = _tile(T, ROW_TILE)

    def body(g_ref, o_ref, z_ref):
        z_ref[...] = (jax.nn.sigmoid(g_ref[...].astype(F32)) * o_ref[...]).astype(z_ref.dtype)

    return pl.pallas_call(
        body, name="gate_fwd", grid=(T // tm,),
        in_specs=[pl.BlockSpec((tm, D), lambda i: (i, 1)), _rows(tm, D)], out_specs=_rows(tm, D),
        out_shape=_sds((T, D), MM_DTYPE), compiler_params=_params())(qg, o)


def _gate_do(dz, qg):
    T, D = dz.shape
    tm = _tile(T, ROW_TILE)

    def body(dz_ref, g_ref, do_ref):
        do_ref[...] = (dz_ref[...].astype(F32) * jax.nn.sigmoid(g_ref[...].astype(F32))).astype(do_ref.dtype)

    return pl.pallas_call(
        body, name="gate_do", grid=(T // tm,),
        in_specs=[_rows(tm, D), pl.BlockSpec((tm, D), lambda i: (i, 1))], out_specs=_rows(tm, D),
        out_shape=_sds((T, D), MM_DTYPE), compiler_params=_params())(dz, qg)


def _gate_bwd(dz, qg, o, dq):
    T, D = dz.shape
    tm = _tile(T, ROW_TILE)

    def body(dz_ref, g_ref, o_ref, dq_ref, out_ref):
        g = g_ref[...].astype(F32)
        sg = jax.nn.sigmoid(g)
        out_ref[:, :D] = dq_ref[...].astype(out_ref.dtype)
        out_ref[:, D:] = (dz_ref[...].astype(F32) * o_ref[...] * sg * (1.0 - sg)).astype(out_ref.dtype)

    return pl.pallas_call(
        body, name="gate_bwd", grid=(T // tm,),
        in_specs=[_rows(tm, D), pl.BlockSpec((tm, D), lambda i: (i, 1)), _rows(tm, D), _rows(tm, D)],
        out_specs=_rows(tm, 2 * D), out_shape=_sds((T, 2 * D), MM_DTYPE),
        compiler_params=_params())(dz, qg, o, dq)


def _mm_in(a, wg, out_dtype, tag, l=None):
    T, K = a.shape
    n = wg.shape[-1]
    tm = _tile(T, MM_TILE)
    if l is None:
        w_spec = _whole((N_CHIP, K, n))
    else:
        w_spec = pl.BlockSpec((None, N_CHIP, K, n), lambda i: (l, 0, 0, 0))

    def body(a_ref, w_ref, o_ref):
        av = a_ref[...]
        for s in range(N_CHIP):
            o_ref[:, s * n:(s + 1) * n] = jnp.dot(av, w_ref[s], preferred_element_type=F32).astype(o_ref.dtype)

    return pl.pallas_call(
        body, name=f"mm_in_{tag}", grid=(T // tm,),
        in_specs=[_rows(tm, K), w_spec], out_specs=_rows(tm, N_CHIP * n),
        out_shape=_sds((T, N_CHIP * n), out_dtype), compiler_params=_params())(a, wg)


def _mm_nt_in(dy, wg, tag, l=None):
    T = dy.shape[0]
    K, n = wg.shape[-2:]
    tm = _tile(T, MM_TILE)
    if l is None:
        w_spec = _whole((N_CHIP, K, n))
    else:
        w_spec = pl.BlockSpec((None, N_CHIP, K, n), lambda i: (l, 0, 0, 0))

    def body(d_ref, w_ref, o_ref):
        acc = None
        for s in range(N_CHIP):
            part = lax.dot_general(d_ref[:, s * n:(s + 1) * n], w_ref[s], NT, preferred_element_type=F32)
            acc = part if acc is None else acc + part
        o_ref[...] = acc

    return pl.pallas_call(
        body, name=f"mm_nt_in_{tag}", grid=(T // tm,),
        in_specs=[_rows(tm, N_CHIP * n), w_spec], out_specs=_rows(tm, K),
        out_shape=_sds((T, K), F32), compiler_params=_params())(dy, wg)


def _mm_nn(a, b, out_dtype, tag):
    T, K = a.shape
    N = b.shape[1]
    tm = _tile(T, MM_TILE)

    def body(a_ref, b_ref, o_ref):
        o_ref[...] = jnp.dot(a_ref[...], b_ref[...], preferred_element_type=F32).astype(o_ref.dtype)

    return pl.pallas_call(
        body, name=f"mm_nn_{tag}", grid=(T // tm,),
        in_specs=[_rows(tm, K), _whole((K, N))], out_specs=_rows(tm, N),
        out_shape=_sds((T, N), out_dtype), compiler_params=_params())(a, b)


def _mm_nt(a, b, out_dtype, tag):
    T, C = a.shape
    N = b.shape[0]
    tm = _tile(T, MM_TILE)
    nb = N
    for cand in (1408, 1024):
        if N > cand and N % cand == 0:
            nb = cand
            break

    def body(a_ref, b_ref, o_ref):
        o_ref[...] = lax.dot_general(a_ref[...], b_ref[...], NT, preferred_element_type=F32).astype(o_ref.dtype)

    return pl.pallas_call(
        body, name=f"mm_nt_{tag}", grid=(N // nb, T // tm),
        in_specs=[pl.BlockSpec((tm, C), lambda j, i: (i, 0)), pl.BlockSpec((nb, C), lambda j, i: (j, 0))],
        out_specs=pl.BlockSpec((tm, nb), lambda j, i: (i, j)),
        out_shape=_sds((T, N), out_dtype), compiler_params=_params())(a, b)


def _mm_tn_in(a, dy, tag, l=None, prev=None):
    T, K = a.shape
    n = dy.shape[1] // N_CHIP
    tt = _tile(T, MM_TILE)

    def body(a_ref, d_ref, *rest):
        o_ref = rest[-1]
        part = lax.dot_general(a_ref[...], d_ref[...], TN, preferred_element_type=F32)
        _accumulate(o_ref, part, pl.program_id(1) == 0)

    in_specs = [pl.BlockSpec((tt, K), lambda s, t: (t, 0)), pl.BlockSpec((tt, n), lambda s, t: (t, s))]
    args = [a, dy]
    kw = {}
    if l is None:
        out_spec = pl.BlockSpec((None, K, n), lambda s, t: (s, 0, 0))
        out_shape = _sds((N_CHIP, K, n), F32)
    else:
        out_spec = pl.BlockSpec((None, None, K, n), lambda s, t: (l, s, 0, 0))
        out_shape = _sds((2, N_CHIP, K, n), F32)
        if prev is not None:
            in_specs.append(ANY)
            args.append(prev)
            kw["input_output_aliases"] = {2: 0}
    return pl.pallas_call(
        body, name=f"mm_tn_in_{tag}", grid=(N_CHIP, T // tt),
        in_specs=in_specs, out_specs=out_spec, out_shape=out_shape,
        compiler_params=_params(), **kw)(*args)


def _mm_tn_out(act, dh, tag, l=None, prev=None):
    T, R4 = act.shape
    D = dh.shape[1]
    r = R4 // N_CHIP
    g = 1 if r % LANES == 0 else 2
    tt = _tile(T, MM_TILE)

    def body(a_ref, d_ref, *rest):
        o_ref = rest[-1]
        part = lax.dot_general(a_ref[...], d_ref[...], TN, preferred_element_type=F32)
        first = pl.program_id(1) == 0
        for q in range(g):
            _accumulate(o_ref.at[q], part[q * r:(q + 1) * r], first)

    in_specs = [pl.BlockSpec((tt, g * r), lambda s, t: (t, s)), pl.BlockSpec((tt, D), lambda s, t: (t, 0))]
    args = [act, dh]
    kw = {}
    if l is None:
        out_spec = pl.BlockSpec((g, r, D), lambda s, t: (s, 0, 0))
        out_shape = _sds((N_CHIP, r, D), F32)
    else:
        out_spec = pl.BlockSpec((None, g, r, D), lambda s, t: (l, s, 0, 0))
        out_shape = _sds((2, N_CHIP, r, D), F32)
        if prev is not None:
            in_specs.append(ANY)
            args.append(prev)
            kw["input_output_aliases"] = {2: 0}
    return pl.pallas_call(
        body, name=f"mm_tn_out_{tag}", grid=(N_CHIP // g, T // tt),
        in_specs=in_specs, out_specs=out_spec, out_shape=out_shape,
        compiler_params=_params(), **kw)(*args)


def _mm_tn(a, b, tag):
    T, K = a.shape
    N = b.shape[1]
    tt = _tile(T, MM_TILE)

    def body(a_ref, b_ref, o_ref):
        part = lax.dot_general(a_ref[...], b_ref[...], TN, preferred_element_type=F32)
        _accumulate(o_ref, part, pl.program_id(0) == 0)

    return pl.pallas_call(
        body, name=f"mm_tn_{tag}", grid=(T // tt,),
        in_specs=[_rows(tt, K), _rows(tt, N)], out_specs=_whole((K, N)),
        out_shape=_sds((K, N), F32), compiler_params=_params())(a, b)


def _adamw(w, g, m, v, tag):
    R, C = w.shape
    tr = _tile(R, ROW_TILE)

    def body(w_ref, g_ref, m_ref, v_ref, d_ref, mo_ref, vo_ref):
        gv = g_ref[...]
        mn = ADAM_B1 * m_ref[...] + (1.0 - ADAM_B1) * gv
        vn = ADAM_B2 * v_ref[...] + (1.0 - ADAM_B2) * (gv * gv)
        m_hat = mn / (1.0 - ADAM_B1 ** ADAM_STEP)
        v_hat = vn / (1.0 - ADAM_B2 ** ADAM_STEP)
        d_ref[...] = -ADAM_LR * (m_hat / (jnp.sqrt(v_hat) + ADAM_EPS) + ADAM_WD * w_ref[...])
        mo_ref[...] = mn
        vo_ref[...] = vn

    return pl.pallas_call(
        body, name=f"adamw_{tag}", grid=(R // tr,),
        in_specs=[_rows(tr, C)] * 4, out_specs=[_rows(tr, C)] * 3,
        out_shape=[_sds((R, C), F32)] * 3, compiler_params=_params())(w, g, m, v)


def _sum_devices(gall):
    _, R, C = gall.shape

    def body(g_ref, o_ref):
        acc = g_ref[0]
        for d in range(1, N_DEV):
            acc = acc + g_ref[d]
        o_ref[...] = acc

    return pl.pallas_call(
        body, name="sum_devices", in_specs=[_whole((N_DEV, R, C))], out_specs=_whole((R, C)),
        out_shape=_sds((R, C), F32), grid=(1,), compiler_params=_params())(gall)


def _place():
    x, y, c = lax.axis_index("x"), lax.axis_index("y"), lax.axis_index("c")
    chips = ((1 - x, y), (x, 1 - y), (1 - x, 1 - y))
    return x, y, c, chips


def _blk(ref, chip, half, layer_major):
    return ref.at[half, chip] if layer_major else ref.at[chip, half]


def _remote(src, dst, send_sem, recv_sem, dev):
    return pltpu.make_async_remote_copy(src_ref=src, dst_ref=dst, send_sem=send_sem, recv_sem=recv_sem,
                                        device_id=dev, device_id_type=MESH)


def _own_slot(halves, layer_major, dtype, place, tag):
    _, r, col = halves.shape
    tr = _tile(r, 2 * ROW_TILE)

    def body(place_ref, x_ref, o_ref):
        o_ref[...] = x_ref[...].astype(o_ref.dtype)

    if layer_major:
        out_spec = pl.BlockSpec((None, None, tr, col), lambda h, i, p: (h, p[1], i, 0))
        shape = (2, N_CHIP, r, col)
    else:
        out_spec = pl.BlockSpec((None, None, tr, col), lambda h, i, p: (p[1], h, i, 0))
        shape = (N_CHIP, 2, r, col)
    grid_spec = pltpu.PrefetchScalarGridSpec(
        num_scalar_prefetch=1, grid=(2, r // tr),
        in_specs=[pl.BlockSpec((None, tr, col), lambda h, i, p: (h, i, 0))], out_specs=out_spec)
    return pl.pallas_call(
        body, name=f"own_slot_{tag}", grid_spec=grid_spec, out_shape=_sds(shape, dtype),
        compiler_params=_params())(place, halves)


def _allgather(bufs, layer_major):
    n = len(bufs)

    def body(*refs):
        outs = refs[n:2 * n]
        s_ici, r_ici, s_d2d, r_d2d = refs[2 * n:]
        x, y, c, chips = _place()
        me = 2 * x + y
        sib = (x, y, 1 - c)
        sends = []
        for i in range(n):
            mine = _blk(outs[i], me, c, layer_major[i])
            for j, (px, py) in enumerate(chips):
                cp = _remote(mine, mine, s_ici.at[i, j], r_ici.at[i, j], (px, py, c))
                cp.start()
                sends.append(cp)
        for j, (px, py) in enumerate(chips):
            src = 2 * px + py
            for i in range(n):
                blk = _blk(outs[i], src, c, layer_major[i])
                _remote(blk, blk, s_ici.at[i, j], r_ici.at[i, j], (px, py, c)).wait_recv()
                cp = _remote(blk, blk, s_d2d.at[i, j], r_d2d.at[i, j], sib)
                cp.start()
                sends.append(cp)
        for j, (px, py) in enumerate(chips):
            src = 2 * px + py
            for i in range(n):
                blk = _blk(outs[i], src, 1 - c, layer_major[i])
                _remote(blk, blk, s_d2d.at[i, j], r_d2d.at[i, j], sib).wait_recv()
        for cp in sends:
            cp.wait_send()

    dma = pltpu.SemaphoreType.DMA
    return pl.pallas_call(
        body, name="allgather_weights", in_specs=[ANY] * n, out_specs=[ANY] * n,
        out_shape=[_sds(b.shape, b.dtype) for b in bufs],
        input_output_aliases={i: i for i in range(n)},
        scratch_shapes=[dma((n, 3)), dma((n, 3)), dma((n, 3)), dma((n, 3))],
        )(*bufs)


def _rs_pair_send(grads, layer_major):
    n = len(grads)

    def body(*refs):
        ins, outs = refs[:n], refs[n:2 * n]
        s_sem, r_sem = refs[2 * n:]
        x, y, c, _ = _place()
        sib = (x, y, 1 - c)
        sends = []
        for i in range(n):
            src = ins[i].at[1 - c] if layer_major[i] else ins[i].at[:, 1 - c]
            cp = _remote(src, outs[i], s_sem.at[i], r_sem.at[i], sib)
            cp.start()
            sends.append(cp)
        for cp in sends:
            cp.wait()

    out_shape = [_sds((N_CHIP,) + g.shape[2:], g.dtype) for g in grads]
    dma = pltpu.SemaphoreType.DMA
    return pl.pallas_call(
        body, name="rs_pair_send", in_specs=[ANY] * n, out_specs=[ANY] * n, out_shape=out_shape,
        scratch_shapes=[dma((n,)), dma((n,))],
        )(*grads)


def _rs_pair_add(g, recv, layer_major, place, tag):
    r, col = g.shape[-2:]
    tr = _tile(r, ROW_TILE)

    def body(place_ref, g_ref, r_ref, wire_ref, own_ref):
        tot = g_ref[...] + r_ref[...]
        wire_ref[...] = tot.astype(wire_ref.dtype)

        @pl.when(pl.program_id(1) == place_ref[1])
        def _():
            own_ref[...] = tot

    if layer_major:
        g_spec = pl.BlockSpec((None, None, tr, col), lambda i, s, p: (p[0], s, i, 0))
    else:
        g_spec = pl.BlockSpec((None, None, tr, col), lambda i, s, p: (s, p[0], i, 0))
    grid_spec = pltpu.PrefetchScalarGridSpec(
        num_scalar_prefetch=1, grid=(r // tr, N_CHIP),
        in_specs=[g_spec, pl.BlockSpec((None, tr, col), lambda i, s, p: (s, i, 0))],
        out_specs=[pl.BlockSpec((None, tr, col), lambda i, s, p: (s, i, 0)),
                   pl.BlockSpec((tr, col), lambda i, s, p: (i, 0))])
    return pl.pallas_call(
        body, name=f"rs_pair_add_{tag}", grid_spec=grid_spec,
        out_shape=[_sds((N_CHIP, r, col), WIRE_DTYPE), _sds((r, col), F32)],
        compiler_params=_params())(place, g, recv)


def _rs_chip_send(wires, small):
    n = len(wires)

    def body(*refs):
        ins, small_ref = refs[:n], refs[n]
        outs, gall_ref = refs[n + 1:2 * n + 1], refs[2 * n + 1]
        s_sem, r_sem, s_small, r_small, s_loc = refs[2 * n + 2:]
        x, y, c, chips = _place()
        me = 4 * x + 2 * y + c
        sends = []
        loc = pltpu.make_async_copy(small_ref, gall_ref.at[me], s_loc)
        loc.start()
        for i in range(n):
            for j, (px, py) in enumerate(chips):
                cp = _remote(ins[i].at[2 * px + py], outs[i].at[j], s_sem.at[i, j], r_sem.at[i, j], (px, py, c))
                cp.start()
                sends.append(cp)
        for k in range(1, N_DEV):
            peer = (x ^ (k >> 2), y ^ ((k >> 1) & 1), c ^ (k & 1))
            cp = _remote(small_ref, gall_ref.at[me], s_small.at[k - 1], r_small.at[k - 1], peer)
            cp.start()
            sends.append(cp)
        for cp in sends:
            cp.wait()
        loc.wait()

    out_shape = [_sds((3,) + w.shape[1:], w.dtype) for w in wires] + [_sds((N_DEV,) + small.shape, small.dtype)]
    dma = pltpu.SemaphoreType.DMA
    return pl.pallas_call(
        body, name="rs_chip_send", in_specs=[ANY] * (n + 1), out_specs=[ANY] * (n + 1), out_shape=out_shape,
        scratch_shapes=[dma((n, 3)), dma((n, 3)), dma((N_DEV - 1,)), dma((N_DEV - 1,)), dma(())],
        )(*wires, small)


def _rs_chip_add(own, recv, place, tag):
    r, col = own.shape
    tr = _tile(r, ROW_TILE)

    def body(place_ref, o_ref, r_ref, out_ref):
        acc = o_ref[...]
        for j in range(3):
            acc = acc + r_ref[j].astype(F32)
        out_ref[...] = acc

    grid_spec = pltpu.PrefetchScalarGridSpec(
        num_scalar_prefetch=1, grid=(r // tr,),
        in_specs=[pl.BlockSpec((tr, col), lambda i, p: (i, 0)), pl.BlockSpec((3, tr, col), lambda i, p: (0, i, 0))],
        out_specs=pl.BlockSpec((None, tr, col), lambda i, p: (p[0], i, 0)))
    return pl.pallas_call(
        body, name=f"rs_chip_add_{tag}", grid_spec=grid_spec, out_shape=_sds((2, r, col), F32),
        compiler_params=_params())(place, own, recv)


def _rs_pair_share(fulls):
    n = len(fulls)

    def body(*refs):
        outs = refs[n:2 * n]
        s_sem, r_sem = refs[2 * n:]
        x, y, c, _ = _place()
        sib = (x, y, 1 - c)
        started = []
        for i in range(n):
            cp = _remote(outs[i].at[c], outs[i].at[c], s_sem.at[i], r_sem.at[i], sib)
            cp.start()
            started.append(cp)
        for i, cp in enumerate(started):
            cp.wait_send()
            _remote(outs[i].at[1 - c], outs[i].at[1 - c], s_sem.at[i], r_sem.at[i], sib).wait_recv()

    dma = pltpu.SemaphoreType.DMA
    return pl.pallas_call(
        body, name="rs_pair_share", in_specs=[ANY] * n, out_specs=[ANY] * n,
        out_shape=[_sds(f.shape, f.dtype) for f in fulls],
        input_output_aliases={i: i for i in range(n)},
        scratch_shapes=[dma((n,)), dma((n,))],
        )(*fulls)


def _ffn_fwd(x, g_pre, g_post, w_in, w_out, l, tag):
    xn = _rms_fwd(x, g_pre[l], tag)
    hgu = _mm_in(xn, w_in, MM_DTYPE, tag, l=l)
    act = _swiglu_fwd(hgu, tag)
    h = _mm_nn(act, w_out[l].reshape(-1, w_out.shape[-1]), F32, tag)
    return _post_fwd(x, h, g_post[l], 0.5, tag), (x, xn, hgu, act, h)


def _ffn_bwd(dx, saved, g_pre, g_post, w_in, w_out, l, tag, prev):
    x, xn, hgu, act, h = saved
    dh, dg_post = _post_bwd(dx, h, g_post[l], 0.5, tag)
    dact = _mm_nt(dh, w_out[l].reshape(-1, w_out.shape[-1]), MM_DTYPE, f"{tag}_out")
    dw_out = _mm_tn_out(act, dh, tag, l=l, prev=prev[1])
    dhgu = _swiglu_bwd(hgu, dact, tag)
    dxn = _mm_nt_in(dhgu, w_in, tag, l=l)
    dw_in = _mm_tn_in(xn, dhgu, tag, l=l, prev=prev[0])
    dx_in, dg_pre = _pre_bwd(dx, dxn, x, g_pre[l], tag)
    return dx_in, dg_pre, dg_post, (dw_in, dw_out)


def kernel(x, ffn1_pre_g, ffn1_post_g, ffn1_w_in, ffn1_w_out, mix_pre_g, mix_post_g, ffn2_pre_g, ffn2_post_g, ffn2_w_in, ffn2_w_out, conv_w_in, conv_k, conv_w_out, kv_g, kv_w, forget_b, attn_w_qg, attn_w_o, loss_target, m_ffn1_pre_g, m_ffn1_post_g, m_ffn1_w_in, m_ffn1_w_out, m_mix_pre_g, m_mix_post_g, m_ffn2_pre_g, m_ffn2_post_g, m_ffn2_w_in, m_ffn2_w_out, m_conv_w_in, m_conv_k, m_conv_w_out, m_kv_g, m_kv_w, m_forget_b, m_attn_w_qg, m_attn_w_o, v_ffn1_pre_g, v_ffn1_post_g, v_ffn1_w_in, v_ffn1_w_out, v_mix_pre_g, v_mix_post_g, v_ffn2_pre_g, v_ffn2_post_g, v_ffn2_w_in, v_ffn2_w_out, v_conv_w_in, v_conv_k, v_conv_w_out, v_kv_g, v_kv_w, v_forget_b, v_attn_w_qg, v_attn_w_o):
    Bl, S, D = x.shape
    T = Bl * S
    H = forget_b.shape[0]
    assert D == H * HEAD_DIM and D % LANES == 0
    kvc = kv_w.shape[1]
    kvp = -(-kvc // LANES) * LANES
    kv_all = 2 * D + LANES
    dk_cols = conv_k.shape[2]
    chip = 2 * lax.axis_index("x") + lax.axis_index("y")
    core = lax.axis_index("c")

    def halves(a3):
        L, r, col = a3.shape
        return (a3, True) if L == 2 else (a3.reshape(2, r // 2, col), False)

    big = {"ffn1_w_in": ffn1_w_in, "ffn1_w_out": ffn1_w_out, "ffn2_w_in": ffn2_w_in, "ffn2_w_out": ffn2_w_out,
           "conv_w_in": conv_w_in, "conv_w_out": conv_w_out,
           "kv_w": jnp.pad(kv_w, ((0, 0), (0, kvp - kvc)))[None],
           "attn_w_qg": attn_w_qg, "attn_w_o": attn_w_o}
    names = list(big)
    place = jnp.stack([core, chip]).astype(jnp.int32)
    send = [halves(big[k]) for k in names]
    send.append(halves(jnp.pad(conv_k[0], ((0, 13), (0, 0)))[None]))
    layer_major = [lm for _, lm in send]
    wire = [MM_DTYPE] * len(names) + [F32]
    slots = [_own_slot(a, lm, dt, place, k) for (a, lm), dt, k in zip(send, wire, names + ["conv_k"])]
    gathered = _allgather(slots, layer_major)
    W = {}
    for k, g, lm in zip(names, gathered, layer_major):
        W[k] = g if lm else g.reshape(N_CHIP, -1, g.shape[-1])
    k_taps = gathered[-1].reshape(N_CHIP, 16, dk_cols).transpose(1, 0, 2).reshape(16, D)[:8]
    kv_full = jnp.concatenate([W["kv_w"][s, :, :kvc] for s in range(N_CHIP)], axis=1)
    kv_full = jnp.pad(kv_full, ((0, 0), (0, kv_all - kv_full.shape[1])))
    w_o_conv = W["conv_w_out"].reshape(D, D)
    w_o_attn = W["attn_w_o"].reshape(D, D)
    fb = jnp.pad(forget_b, (0, LANES - H)).reshape(1, LANES)

    x0 = x.reshape(T, D)
    x1, s_f1a = _ffn_fwd(x0, ffn1_pre_g, ffn1_post_g, W["ffn1_w_in"], W["ffn1_w_out"], 0, "l0f1")
    xn_c = _rms_fwd(x1, mix_pre_g[0], "l0mix")
    bch = _mm_in(xn_c, W["conv_w_in"], MM_DTYPE, "conv")
    z_c = _conv_fwd(bch, k_taps, Bl, S)
    m_c = _mm_nn(z_c, w_o_conv, F32, "conv_out")
    x2 = _post_fwd(x1, m_c, mix_post_g[0], 1.0, "l0mix")
    x3, s_f2a = _ffn_fwd(x2, ffn2_pre_g, ffn2_post_g, W["ffn2_w_in"], W["ffn2_w_out"], 0, "l0f2")

    xn_kv = _rms_fwd(x3, kv_g, "kv")
    kvact = _mm_nn(xn_kv, kv_full[:, :2 * D], MM_DTYPE, "kv")
    pf = _mm_nn(xn_kv, kv_full[:, 2 * D:], F32, "forget")
    cum = _forget_fwd(pf, fb, Bl, S)
    bq = min(S, ATT_BLOCK)
    c3 = cum.reshape(Bl, S, LANES)[:, :, :H].transpose(0, 2, 1)
    c_col = jnp.broadcast_to(c3[..., None], (Bl, H, S, LANES))
    c_row = c3.reshape(Bl, H, S // bq, 1, bq)

    x4, s_f1b = _ffn_fwd(x3, ffn1_pre_g, ffn1_post_g, W["ffn1_w_in"], W["ffn1_w_out"], 1, "l1f1")
    xn_a = _rms_fwd(x4, mix_pre_g[1], "l1mix")
    qg = _mm_in(xn_a, W["attn_w_qg"], MM_DTYPE, "qg")
    o, lse = _attn_fwd(qg, kvact, c_col, c_row, Bl, S, D)
    z_a = _gate_fwd(qg, o)
    m_a = _mm_nn(z_a, w_o_attn, F32, "attn_out")
    x5 = _post_fwd(x4, m_a, mix_post_g[1], 1.0, "l1mix")
    x6, s_f2b = _ffn_fwd(x5, ffn2_pre_g, ffn2_post_g, W["ffn2_w_in"], W["ffn2_w_out"], 1, "l1f2")

    dy, loss_local = _loss_grad(x6, loss_target.reshape(T, D))
    loss = lax.psum(loss_local, ("x", "y", "c"))

    none2 = (None, None)
    dx5, dg_f2pre_1, dg_f2post_1, G_f2 = _ffn_bwd(dy, s_f2b, ffn2_pre_g, ffn2_post_g, W["ffn2_w_in"], W["ffn2_w_out"], 1, "l1f2", none2)
    dm_a, dg_mixpost_1 = _post_bwd(dx5, m_a, mix_post_g[1], 1.0, "l1mix")
    dz_a = _mm_nt(dm_a, w_o_attn, MM_DTYPE, "attn_out")
    G_attn_o = _mm_tn_out(z_a, dm_a, "attn_out")
    do = _gate_do(dz_a, qg)
    dq, dk, dv, dcr = _attn_bwd(qg, kvact, do, lse, c_col, c_row, Bl, S, D)
    dqg = _gate_bwd(dz_a, qg, o, dq)
    dxn_a = _mm_nt_in(dqg, W["attn_w_qg"], "qg")
    G_qg = _mm_tn_in(xn_a, dqg, "qg")
    dx4, dg_mixpre_1 = _pre_bwd(dx5, dxn_a, x4, mix_pre_g[1], "l1mix")
    dx3, dg_f1pre_1, dg_f1post_1, G_f1 = _ffn_bwd(dx4, s_f1b, ffn1_pre_g, ffn1_post_g, W["ffn1_w_in"], W["ffn1_w_out"], 1, "l1f1", none2)

    dcum = jnp.pad(dcr.reshape(Bl, H, S).transpose(0, 2, 1), ((0, 0), (0, 0), (0, LANES - H))).reshape(T, LANES)
    dpf, dfb = _forget_bwd(dcum, pf, fb, Bl, S)
    dp = jnp.concatenate([dk.astype(MM_DTYPE), dv.astype(MM_DTYPE), dpf], axis=1)
    dxn_kv = _mm_nt(dp, kv_full, F32, "kv")
    G_kv_full = _mm_tn(xn_kv, dp, "kv")
    dx3, dg_kv = _pre_bwd(dx3, dxn_kv, x3, kv_g, "kv")

    dx2, dg_f2pre_0, dg_f2post_0, G_f2 = _ffn_bwd(dx3, s_f2a, ffn2_pre_g, ffn2_post_g, W["ffn2_w_in"], W["ffn2_w_out"], 0, "l0f2", G_f2)
    dm_c, dg_mixpost_0 = _post_bwd(dx2, m_c, mix_post_g[0], 1.0, "l0mix")
    dz_c = _mm_nt(dm_c, w_o_conv, MM_DTYPE, "conv_out")
    G_conv_o = _mm_tn_out(z_c, dm_c, "conv_out")
    db, dcg, dhh, dk_taps = _conv_bwd(bch, dz_c, k_taps, Bl, S)
    dbch = jnp.concatenate([db, dcg, dhh], axis=1)
    dxn_c = _mm_nt_in(dbch, W["conv_w_in"], "conv")
    G_conv_in = _mm_tn_in(xn_c, dbch, "conv")
    dx1, dg_mixpre_0 = _pre_bwd(dx2, dxn_c, x1, mix_pre_g[0], "l0mix")
    dx0, dg_f1pre_0, dg_f1post_0, G_f1 = _ffn_bwd(dx1, s_f1a, ffn1_pre_g, ffn1_post_g, W["ffn1_w_in"], W["ffn1_w_out"], 0, "l0f1", G_f1)
    grad_x = dx0.reshape(Bl, S, D)

    G_kv = jnp.stack([jnp.pad(G_kv_full[:, s * kvc:(s + 1) * kvc], ((0, 0), (0, kvp - kvc))) for s in range(N_CHIP)])
    G = {"ffn1_w_in": G_f1[0], "ffn1_w_out": G_f1[1], "ffn2_w_in": G_f2[0], "ffn2_w_out": G_f2[1],
         "conv_w_in": G_conv_in, "conv_w_out": G_conv_o, "kv_w": G_kv, "attn_w_qg": G_qg, "attn_w_o": G_attn_o}
    grads = []
    for k, lm in zip(names, layer_major):
        g = G[k]
        grads.append(g if lm else g.reshape(N_CHIP, 2, g.shape[1] // 2, g.shape[2]))
    lms = layer_major[:len(names)]
    from_sibling = _rs_pair_send(grads, lms)
    wires, owns = [], []
    for k, g, r, lm in zip(names, grads, from_sibling, lms):
        w, own = _rs_pair_add(g, r, lm, place, k)
        wires.append(w)
        owns.append(own)

    def row(v):
        return jnp.pad(v.reshape(-1), (0, D - v.size)).reshape(1, D)

    small_parts = [dg_f1pre_0, dg_f1pre_1, dg_f1post_0, dg_f1post_1, dg_mixpre_0, dg_mixpre_1, dg_mixpost_0, dg_mixpost_1,
                   dg_f2pre_0, dg_f2pre_1, dg_f2post_0, dg_f2post_1, dg_kv, row(dfb[0, :H]), dk_taps[:3]]
    small = jnp.concatenate(small_parts, axis=0)
    small = jnp.pad(small, ((0, SMALL_ROWS - small.shape[0]), (0, 0)))
    outs = _rs_chip_send(wires, small)
    recvs, gall = outs[:-1], outs[-1]
    reduced = _rs_pair_share([_rs_chip_add(own, r, place, k) for k, own, r in zip(names, owns, recvs)])
    gsum = _sum_devices(gall)

    given = dict(ffn1_w_in=(ffn1_w_in, m_ffn1_w_in, v_ffn1_w_in), ffn1_w_out=(ffn1_w_out, m_ffn1_w_out, v_ffn1_w_out),
                 ffn2_w_in=(ffn2_w_in, m_ffn2_w_in, v_ffn2_w_in), ffn2_w_out=(ffn2_w_out, m_ffn2_w_out, v_ffn2_w_out),
                 conv_w_in=(conv_w_in, m_conv_w_in, v_conv_w_in), conv_w_out=(conv_w_out, m_conv_w_out, v_conv_w_out),
                 kv_w=(kv_w, m_kv_w, v_kv_w), attn_w_qg=(attn_w_qg, m_attn_w_qg, v_attn_w_qg),
                 attn_w_o=(attn_w_o, m_attn_w_o, v_attn_w_o))
    res = {}
    for k, red in zip(names, reduced):
        w, m, v = given[k]
        g2 = red.reshape(-1, red.shape[-1])
        if k == "kv_w":
            g2 = g2[:, :kvc]
        flat = lambda a: a.reshape(-1, a.shape[-1])
        d, mn, vn = _adamw(flat(w), g2, flat(m), flat(v), k)
        res[k] = tuple(a.reshape(w.shape) for a in (g2, d, mn, vn))

    small_names = ["ffn1_pre_g", "ffn1_post_g", "mix_pre_g", "mix_post_g", "ffn2_pre_g", "ffn2_post_g"]
    small_given = dict(ffn1_pre_g=(ffn1_pre_g, m_ffn1_pre_g, v_ffn1_pre_g), ffn1_post_g=(ffn1_post_g, m_ffn1_post_g, v_ffn1_post_g),
                       mix_pre_g=(mix_pre_g, m_mix_pre_g, v_mix_pre_g), mix_post_g=(mix_post_g, m_mix_post_g, v_mix_post_g),
                       ffn2_pre_g=(ffn2_pre_g, m_ffn2_pre_g, v_ffn2_pre_g), ffn2_post_g=(ffn2_post_g, m_ffn2_post_g, v_ffn2_post_g))

    def pack(idx):
        rows_ = [small_given[k][idx] for k in small_names]
        rows_ += [row((kv_g, m_kv_g, v_kv_g)[idx]), row((forget_b, m_forget_b, v_forget_b)[idx])]
        rows_.append(jnp.pad((conv_k, m_conv_k, v_conv_k)[idx][0], ((0, 0), (0, D - dk_cols))))
        a = jnp.concatenate(rows_, axis=0)
        return jnp.pad(a, ((0, SMALL_ROWS - a.shape[0]), (0, 0)))

    g_taps = lax.dynamic_slice_in_dim(gsum[14:17], chip * dk_cols, dk_cols, axis=1)
    g_small = jnp.concatenate([gsum[:14], jnp.pad(g_taps, ((0, 0), (0, D - dk_cols))), gsum[17:]], axis=0)
    d_s, m_s, v_s = _adamw(pack(0), g_small, pack(1), pack(2), "small")
    for i, k in enumerate(small_names):
        res[k] = tuple(a[2 * i:2 * i + 2] for a in (g_small, d_s, m_s, v_s))
    res["kv_g"] = tuple(a[12] for a in (g_small, d_s, m_s, v_s))
    res["forget_b"] = tuple(a[13, :H] for a in (g_small, d_s, m_s, v_s))
    res["conv_k"] = tuple(a[14:17, :dk_cols][None] for a in (g_small, d_s, m_s, v_s))

    order = ["ffn1_pre_g", "ffn1_post_g", "ffn1_w_in", "ffn1_w_out", "mix_pre_g", "mix_post_g", "ffn2_pre_g", "ffn2_post_g",
             "ffn2_w_in", "ffn2_w_out", "conv_w_in", "conv_k", "conv_w_out", "kv_g", "kv_w", "forget_b", "attn_w_qg", "attn_w_o"]
    out = [loss, grad_x]
    for idx in range(4):
        out += [res[k][idx] for k in order]
    return tuple(out)
```

```python
import functools
import math

import jax
import jax.numpy as jnp
from jax import lax
from jax.experimental import pallas as pl
from jax.experimental.pallas import tpu as pltpu

F32 = jnp.float32
MM_DTYPE = jnp.bfloat16
WIRE_DTYPE = jnp.bfloat16

RMS_EPS = 1e-6
ADAM_LR = 0.001
ADAM_B1 = 0.9
ADAM_B2 = 0.999
ADAM_EPS = 1e-08
ADAM_WD = 0.01
ADAM_STEP = 10

HEAD_DIM = 64
LANES = 128
N_CHIP = 4
N_DEV = 8
ROW_TILE = 256
MM_TILE = 512
ATT_BLOCK = 512
SMALL_ROWS = 24
VMEM_LIMIT = 56 * 1024 * 1024
MESH = pl.DeviceIdType.MESH
ANY = pl.BlockSpec(memory_space=pl.ANY)

NT = (((1,), (1,)), ((), ()))
TN = (((0,), (0,)), ((), ()))


def _tile(n, pref):
    if n <= pref:
        return n
    t = pref - pref % 16
    while n % t:
        t -= 16
    return t


def _params():
    return pltpu.CompilerParams(vmem_limit_bytes=VMEM_LIMIT)


def _sds(shape, dtype):
    return jax.ShapeDtypeStruct(shape, dtype)


def _rows(tm, c):
    return pl.BlockSpec((tm, c), lambda i: (i, 0))


def _whole(shape):
    return pl.BlockSpec(shape, lambda *_: (0,) * len(shape))


def _rms_fwd(x, g, tag):
    T, D = x.shape
    tm = _tile(T, ROW_TILE)

    def body(x_ref, g_ref, o_ref):
        xv = x_ref[...]
        r = lax.rsqrt(jnp.mean(xv * xv, axis=-1, keepdims=True) + RMS_EPS)
        o_ref[...] = (xv * r * g_ref[...]).astype(o_ref.dtype)

    return pl.pallas_call(
        body, name=f"rms_fwd_{tag}", grid=(T // tm,),
        in_specs=[_rows(tm, D), _whole((1, D))], out_specs=_rows(tm, D),
        out_shape=_sds((T, D), MM_DTYPE), compiler_params=_params())(x, g.reshape(1, D))


def _post_fwd(x, h, g, alpha, tag):
    T, D = x.shape
    tm = _tile(T, ROW_TILE)

    def body(x_ref, h_ref, g_ref, o_ref):
        hv = h_ref[...]
        r = lax.rsqrt(jnp.mean(hv * hv, axis=-1, keepdims=True) + RMS_EPS)
        o_ref[...] = x_ref[...] + alpha * (hv * r * g_ref[...])

    return pl.pallas_call(
        body, name=f"post_fwd_{tag}", grid=(T // tm,),
        in_specs=[_rows(tm, D), _rows(tm, D), _whole((1, D))], out_specs=_rows(tm, D),
        out_shape=_sds((T, D), F32), compiler_params=_params())(x, h, g.reshape(1, D))


def _accumulate(ref, part, first):
    @pl.when(first)
    def _():
        ref[...] = part

    @pl.when(jnp.logical_not(first))
    def _():
        ref[...] += part


def _post_bwd(dx, h, g, alpha, tag):
    T, D = dx.shape
    tm = _tile(T, ROW_TILE)

    def body(dx_ref, h_ref, g_ref, dh_ref, dg_ref):
        hv = h_ref[...]
        r = lax.rsqrt(jnp.mean(hv * hv, axis=-1, keepdims=True) + RMS_EPS)
        hh = hv * r
        dyn = alpha * dx_ref[...]
        _accumulate(dg_ref, jnp.sum(dyn * hh, axis=0, keepdims=True), pl.program_id(0) == 0)
        dhh = dyn * g_ref[...]
        dh = r * (dhh - hh * jnp.mean(dhh * hh, axis=-1, keepdims=True))
        dh_ref[...] = dh.astype(dh_ref.dtype)

    return pl.pallas_call(
        body, name=f"post_bwd_{tag}", grid=(T // tm,),
        in_specs=[_rows(tm, D), _rows(tm, D), _whole((1, D))],
        out_specs=[_rows(tm, D), _whole((1, D))],
        out_shape=[_sds((T, D), MM_DTYPE), _sds((1, D), F32)],
        compiler_params=_params())(dx, h, g.reshape(1, D))


def _pre_bwd(dres, dxn, x, g, tag):
    T, D = x.shape
    tm = _tile(T, ROW_TILE)

    def body(dres_ref, dxn_ref, x_ref, g_ref, dx_ref, dg_ref):
        xv = x_ref[...]
        r = lax.rsqrt(jnp.mean(xv * xv, axis=-1, keepdims=True) + RMS_EPS)
        xh = xv * r
        dn = dxn_ref[...]
        _accumulate(dg_ref, jnp.sum(dn * xh, axis=0, keepdims=True), pl.program_id(0) == 0)
        dxh = dn * g_ref[...]
        dx_ref[...] = dres_ref[...] + r * (dxh - xh * jnp.mean(dxh * xh, axis=-1, keepdims=True))

    return pl.pallas_call(
        body, name=f"pre_bwd_{tag}", grid=(T // tm,),
        in_specs=[_rows(tm, D), _rows(tm, D), _rows(tm, D), _whole((1, D))],
        out_specs=[_rows(tm, D), _whole((1, D))],
        out_shape=[_sds((T, D), F32), _sds((1, D), F32)],
        compiler_params=_params())(dres, dxn, x, g.reshape(1, D))


def _swiglu_fwd(hgu, tag):
    T, F2 = hgu.shape
    F = F2 // 2
    tm = _tile(T, ROW_TILE)

    def body(g_ref, u_ref, o_ref):
        g = g_ref[...].astype(F32)
        o_ref[...] = (g * jax.nn.sigmoid(g) * u_ref[...].astype(F32)).astype(o_ref.dtype)

    return pl.pallas_call(
        body, name=f"swiglu_fwd_{tag}", grid=(T // tm,),
        in_specs=[pl.BlockSpec((tm, F), lambda i: (i, 0)), pl.BlockSpec((tm, F), lambda i: (i, 1))],
        out_specs=_rows(tm, F), out_shape=_sds((T, F), MM_DTYPE), compiler_params=_params())(hgu, hgu)


def _swiglu_bwd(hgu, da, tag):
    T, F2 = hgu.shape
    F = F2 // 2
    tm = _tile(T, ROW_TILE)

    def body(h_ref, da_ref, o_ref):
        g = h_ref[:, :F].astype(F32)
        u = h_ref[:, F:].astype(F32)
        d = da_ref[...].astype(F32)
        sg = jax.nn.sigmoid(g)
        o_ref[:, :F] = (d * u * sg * (1.0 + g * (1.0 - sg))).astype(o_ref.dtype)
        o_ref[:, F:] = (d * g * sg).astype(o_ref.dtype)

    return pl.pallas_call(
        body, name=f"swiglu_bwd_{tag}", grid=(T // tm,),
        in_specs=[_rows(tm, F2), _rows(tm, F)], out_specs=_rows(tm, F2),
        out_shape=_sds((T, F2), MM_DTYPE), compiler_params=_params())(hgu, da)


def _loss_grad(y, tgt):
    T, D = y.shape
    tm = _tile(T, ROW_TILE)

    def body(y_ref, t_ref, dy_ref, l_ref):
        e = y_ref[...] - t_ref[...]
        row = jnp.mean(e * e, axis=-1, keepdims=True)
        part = jnp.broadcast_to(jnp.sum(row, axis=0, keepdims=True), (8, LANES))
        _accumulate(l_ref, part, pl.program_id(0) == 0)
        dy_ref[...] = e * (1.0 / D)

    dy, lsum = pl.pallas_call(
        body, name="loss_grad", grid=(T // tm,),
        in_specs=[_rows(tm, D), _rows(tm, D)], out_specs=[_rows(tm, D), _whole((8, LANES))],
        out_shape=[_sds((T, D), F32), _sds((8, LANES), F32)], compiler_params=_params())(y, tgt)
    return dy, 0.5 * lsum[0, 0]


def _shift_down(u, d, rows):
    return jnp.where(rows >= d, pltpu.roll(u, d, 0), 0.0)


def _shift_up(u, d, rows, S):
    return jnp.where(rows < S - d, pltpu.roll(u, S - d, 0), 0.0)


def _conv_fwd(bch, k8, Bl, S):
    T, D3 = bch.shape
    D = D3 // 3
    dc = min(D, 2 * LANES)
    nd = D // dc

    def body(b_ref, c_ref, h_ref, k_ref, z_ref):
        rows = lax.broadcasted_iota(jnp.int32, (S, 1), 0)
        u = c_ref[...].astype(F32) * h_ref[...].astype(F32)
        y = k_ref[2:3, :] * u + k_ref[1:2, :] * _shift_down(u, 1, rows) + k_ref[0:1, :] * _shift_down(u, 2, rows)
        z_ref[...] = (b_ref[...].astype(F32) * y).astype(z_ref.dtype)

    return pl.pallas_call(
        body, name="conv_fwd", grid=(Bl, nd),
        in_specs=[pl.BlockSpec((S, dc), lambda b, j: (b, j)),
                  pl.BlockSpec((S, dc), lambda b, j: (b, nd + j)),
                  pl.BlockSpec((S, dc), lambda b, j: (b, 2 * nd + j)),
                  pl.BlockSpec((8, dc), lambda b, j: (0, j))],
        out_specs=pl.BlockSpec((S, dc), lambda b, j: (b, j)),
        out_shape=_sds((T, D), MM_DTYPE), compiler_params=_params())(bch, bch, bch, k8)


def _conv_bwd(bch, dz, k8, Bl, S):
    T, D3 = bch.shape
    D = D3 // 3
    dc = min(D, 2 * LANES)
    nd = D // dc

    def body(b_ref, c_ref, h_ref, dz_ref, k_ref, db_ref, dc_ref, dh_ref, dk_ref):
        rows = lax.broadcasted_iota(jnp.int32, (S, 1), 0)
        bv = b_ref[...].astype(F32)
        cv = c_ref[...].astype(F32)
        hv = h_ref[...].astype(F32)
        dzv = dz_ref[...].astype(F32)
        u = cv * hv
        u1 = _shift_down(u, 1, rows)
        u2 = _shift_down(u, 2, rows)
        y = k_ref[2:3, :] * u + k_ref[1:2, :] * u1 + k_ref[0:1, :] * u2
        db_ref[...] = (dzv * y).astype(db_ref.dtype)
        dy = dzv * bv
        du = k_ref[2:3, :] * dy + k_ref[1:2, :] * _shift_up(dy, 1, rows, S) + k_ref[0:1, :] * _shift_up(dy, 2, rows, S)
        dc_ref[...] = (du * hv).astype(dc_ref.dtype)
        dh_ref[...] = (du * cv).astype(dh_ref.dtype)

        @pl.when(pl.program_id(1) == 0)
        def _():
            dk_ref[...] = jnp.zeros_like(dk_ref)

        dk_ref[0:1, :] += jnp.sum(dy * u2, axis=0, keepdims=True)
        dk_ref[1:2, :] += jnp.sum(dy * u1, axis=0, keepdims=True)
        dk_ref[2:3, :] += jnp.sum(dy * u, axis=0, keepdims=True)

    seq = lambda off: pl.BlockSpec((S, dc), lambda j, b: (b, off + j))
    return pl.pallas_call(
        body, name="conv_bwd", grid=(nd, Bl),
        in_specs=[seq(0), seq(nd), seq(2 * nd), seq(0), pl.BlockSpec((8, dc), lambda j, b: (0, j))],
        out_specs=[seq(0), seq(0), seq(0), pl.BlockSpec((8, dc), lambda j, b: (0, j))],
        out_shape=[_sds((T, D), MM_DTYPE)] * 3 + [_sds((8, D), F32)],
        compiler_params=_params())(bch, bch, bch, dz, k8)


def _forget_fwd(pf, fb, Bl, S):
    T = pf.shape[0]

    def body(p_ref, fb_ref, c_ref):
        rows = lax.broadcasted_iota(jnp.int32, (S, 1), 0)
        z = p_ref[...] + fb_ref[...]
        acc = jnp.minimum(z, 0.0) - jnp.log1p(jnp.exp(-jnp.abs(z)))
        d = 1
        while d < S:
            acc = acc + _shift_down(acc, d, rows)
            d *= 2
        c_ref[...] = acc

    return pl.pallas_call(
        body, name="forget_fwd", grid=(Bl,),
        in_specs=[_rows(S, LANES), _whole((1, LANES))], out_specs=_rows(S, LANES),
        out_shape=_sds((T, LANES), F32), compiler_params=_params())(pf, fb)


def _forget_bwd(dc, pf, fb, Bl, S):
    T = pf.shape[0]

    def body(dc_ref, p_ref, fb_ref, df_ref, dfb_ref):
        rows = lax.broadcasted_iota(jnp.int32, (S, 1), 0)
        acc = dc_ref[...]
        d = 1
        while d < S:
            acc = acc + _shift_up(acc, d, rows, S)
            d *= 2
        df = acc * jax.nn.sigmoid(-(p_ref[...] + fb_ref[...]))
        df_ref[...] = df.astype(df_ref.dtype)
        _accumulate(dfb_ref, jnp.sum(df, axis=0, keepdims=True), pl.program_id(0) == 0)

    return pl.pallas_call(
        body, name="forget_bwd", grid=(Bl,),
        in_specs=[_rows(S, LANES), _rows(S, LANES), _whole((1, LANES))],
        out_specs=[_rows(S, LANES), _whole((1, LANES))],
        out_shape=[_sds((T, LANES), MM_DTYPE), _sds((1, LANES), F32)],
        compiler_params=_params())(dc, pf, fb)


def _head_mask(h):
    lane = lax.broadcasted_iota(jnp.int32, (1, LANES), 1)
    return (lane >= h * HEAD_DIM) & (lane < (h + 1) * HEAD_DIM)


def _attn_fwd(qg, kv, c_col, c_row, Bl, S, D):
    T = Bl * S
    H = D // HEAD_DIM
    HP = D // LANES
    bq = min(S, ATT_BLOCK)
    nq = S // bq
    scale = 1.0 / math.sqrt(HEAD_DIM)

    def body(q_ref, k_ref, v_ref, cc_ref, cr_ref, o_ref, lse_ref):
        i = pl.program_id(2)
        q2 = q_ref[...]
        qh = [q2 * (_head_mask(h).astype(F32) * scale).astype(q2.dtype) for h in range(2)]
        cc = [cc_ref[h][:, :1] for h in range(2)]
        diag = lax.broadcasted_iota(jnp.int32, (1, bq), 1) <= lax.broadcasted_iota(jnp.int32, (bq, 1), 0)

        def block(j, carry, on_diagonal):
            off = pl.multiple_of(j * bq, bq)
            kj = k_ref[pl.ds(off, bq), :]
            vj = v_ref[pl.ds(off, bq), :]
            new = []
            for h in range(2):
                m, l, acc = carry[h]
                s = lax.dot_general(qh[h], kj, NT, preferred_element_type=F32) + cc[h] - cr_ref[h, j]
                if on_diagonal:
                    s = jnp.where(diag, s, -jnp.inf)
                m_new = jnp.maximum(m, jnp.max(s, axis=1, keepdims=True))
                p = jnp.exp(s - m_new)
                a = jnp.exp(m - m_new)
                l = a * l + jnp.sum(p, axis=1, keepdims=True)
                acc = a * acc + jnp.dot(p.astype(MM_DTYPE), vj, preferred_element_type=F32)
                new.append((m_new, l, acc))
            return tuple(new)

        one = (jnp.full((bq, 1), -jnp.inf, F32), jnp.zeros((bq, 1), F32), jnp.zeros((bq, LANES), F32))
        carry = lax.fori_loop(0, i, lambda j, c: block(j, c, False), (one, one))
        carry = block(i, carry, True)
        outs = []
        for h in range(2):
            m, l, acc = carry[h]
            outs.append(acc / l)
            lse_ref[h] = jnp.broadcast_to(m + jnp.log(l), (bq, LANES))
        o_ref[...] = jnp.where(_head_mask(0), outs[0], outs[1])

    return pl.pallas_call(
        body, name="attn_fwd", grid=(Bl, HP, nq),
        in_specs=[pl.BlockSpec((bq, LANES), lambda b, hp, i: (b * nq + i, hp)),
                  pl.BlockSpec((S, LANES), lambda b, hp, i: (b, hp)),
                  pl.BlockSpec((S, LANES), lambda b, hp, i: (b, HP + hp)),
                  pl.BlockSpec((None, 2, bq, LANES), lambda b, hp, i: (b, hp, i, 0)),
                  pl.BlockSpec((None, 2, nq, 1, bq), lambda b, hp, i: (b, hp, 0, 0, 0))],
        out_specs=[pl.BlockSpec((bq, LANES), lambda b, hp, i: (b * nq + i, hp)),
                   pl.BlockSpec((None, 2, bq, LANES), lambda b, hp, i: (b, hp, i, 0))],
        out_shape=[_sds((T, D), F32), _sds((Bl, H, S, LANES), F32)],
        compiler_params=_params())(qg, kv, kv, c_col, c_row)


def _attn_bwd(qg, kv, do, lse, c_col, c_row, Bl, S, D):
    T = Bl * S
    H = D // HEAD_DIM
    HP = D // LANES
    bq = min(S, ATT_BLOCK)
    nq = S // bq
    scale = 1.0 / math.sqrt(HEAD_DIM)

    def body(q_ref, k_ref, v_ref, do_ref, lse_ref, cc_ref, cr_ref, dq_ref, dk_ref, dv_ref, dcr_ref, p_sc, dp_sc):
        i = pl.program_id(2)

        @pl.when(i == 0)
        def _():
            dk_ref[...] = jnp.zeros_like(dk_ref)
            dv_ref[...] = jnp.zeros_like(dv_ref)
            dcr_ref[...] = jnp.zeros_like(dcr_ref)

        q2 = q_ref[...]
        do2 = do_ref[...]
        masks = [_head_mask(h).astype(F32) for h in range(2)]
        qh = [q2 * (masks[h] * scale).astype(q2.dtype) for h in range(2)]
        doh = [do2 * masks[h].astype(do2.dtype) for h in range(2)]
        cc = [cc_ref[h][:, :1] for h in range(2)]
        lse = [lse_ref[h][:, :1] for h in range(2)]
        diag = lax.broadcasted_iota(jnp.int32, (1, bq), 1) <= lax.broadcasted_iota(jnp.int32, (bq, 1), 0)

        def sweep1(j, delta, on_diagonal):
            off = pl.multiple_of(j * bq, bq)
            kj = k_ref[pl.ds(off, bq), :]
            vj = v_ref[pl.ds(off, bq), :]
            new = []
            dv = None
            for h in range(2):
                s = lax.dot_general(qh[h], kj, NT, preferred_element_type=F32) + cc[h] - cr_ref[h, j]
                if on_diagonal:
                    s = jnp.where(diag, s, -jnp.inf)
                p = jnp.exp(s - lse[h])
                dp = lax.dot_general(doh[h], vj, NT, preferred_element_type=F32)
                p_sc[h, j] = p
                dp_sc[h, j] = dp
                part = lax.dot_general(p.astype(MM_DTYPE), doh[h], TN, preferred_element_type=F32)
                dv = part if dv is None else dv + part
                new.append(delta[h] + jnp.sum(p * dp, axis=1, keepdims=True))
            dv_ref[pl.ds(off, bq), :] += dv
            return tuple(new)

        zero = jnp.zeros((bq, 1), F32)
        delta = lax.fori_loop(0, i, lambda j, d: sweep1(j, d, False), (zero, zero))
        delta = sweep1(i, delta, True)

        def sweep2(j, dq):
            off = pl.multiple_of(j * bq, bq)
            kj = k_ref[pl.ds(off, bq), :]
            dk = None
            for h in range(2):
                ds = p_sc[h, j] * (dp_sc[h, j] - delta[h])
                dcr_ref[h, j] -= jnp.sum(ds, axis=0, keepdims=True)
                dsb = ds.astype(MM_DTYPE)
                dq = dq + jnp.dot(dsb, kj * (masks[h] * scale).astype(kj.dtype), preferred_element_type=F32)
                part = lax.dot_general(dsb, qh[h], TN, preferred_element_type=F32)
                dk = part if dk is None else dk + part
            dk_ref[pl.ds(off, bq), :] += dk
            return dq

        dq_ref[...] = lax.fori_loop(0, i + 1, sweep2, jnp.zeros((bq, LANES), F32))

    blk = lambda col: pl.BlockSpec((bq, LANES), lambda b, hp, i: (b * nq + i, col(hp)))
    seq = lambda col: pl.BlockSpec((S, LANES), lambda b, hp, i: (b, col(hp)))
    per_head = pl.BlockSpec((None, 2, bq, LANES), lambda b, hp, i: (b, hp, i, 0))
    rows = pl.BlockSpec((None, 2, nq, 1, bq), lambda b, hp, i: (b, hp, 0, 0, 0))
    return pl.pallas_call(
        body, name="attn_bwd", grid=(Bl, HP, nq),
        in_specs=[blk(lambda hp: hp), seq(lambda hp: hp), seq(lambda hp: HP + hp), blk(lambda hp: hp),
                  per_head, per_head, rows],
        out_specs=[blk(lambda hp: hp), seq(lambda hp: hp), seq(lambda hp: hp), rows],
        out_shape=[_sds((T, D), F32), _sds((T, D), F32), _sds((T, D), F32), _sds((Bl, H, nq, 1, bq), F32)],
        scratch_shapes=[pltpu.VMEM((2, nq, bq, bq), F32), pltpu.VMEM((2, nq, bq, bq), F32)],
        compiler_params=_params())(qg, kv, kv, do, lse, c_col, c_row)


def _gate_fwd(qg, o):
    T, D = o.shape
    tm = _tile(T, ROW_TILE)

    def body(g_ref, o_ref, z_ref):
        z_ref[...] = (jax.nn.sigmoid(g_ref[...].astype(F32)) * o_ref[...]).astype(z_ref.dtype)

    return pl.pallas_call(
        body, name="gate_fwd", grid=(T // tm,),
        in_specs=[pl.BlockSpec((tm, D), lambda i: (i, 1)), _rows(tm, D)], out_specs=_rows(tm, D),
        out_shape=_sds((T, D), MM_DTYPE), compiler_params=_params())(qg, o)


def _gate_do(dz, qg):
    T, D = dz.shape
    tm = _tile(T, ROW_TILE)

    def body(dz_ref, g_ref, do_ref):
        do_ref[...] = (dz_ref[...].astype(F32) * jax.nn.sigmoid(g_ref[...].astype(F32))).astype(do_ref.dtype)

    return pl.pallas_call(
        body, name="gate_do", grid=(T // tm,),
        in_specs=[_rows(tm, D), pl.BlockSpec((tm, D), lambda i: (i, 1))], out_specs=_rows(tm, D),
        out_shape=_sds((T, D), MM_DTYPE), compiler_params=_params())(dz, qg)


def _gate_bwd(dz, qg, o, dq):
    T, D = dz.shape
    tm = _tile(T, ROW_TILE)

    def body(dz_ref, g_ref, o_ref, dq_ref, out_ref):
        g = g_ref[...].astype(F32)
        sg = jax.nn.sigmoid(g)
        out_ref[:, :D] = dq_ref[...].astype(out_ref.dtype)
        out_ref[:, D:] = (dz_ref[...].astype(F32) * o_ref[...] * sg * (1.0 - sg)).astype(out_ref.dtype)

    return pl.pallas_call(
        body, name="gate_bwd", grid=(T // tm,),
        in_specs=[_rows(tm, D), pl.BlockSpec((tm, D), lambda i: (i, 1)), _rows(tm, D), _rows(tm, D)],
        out_specs=_rows(tm, 2 * D), out_shape=_sds((T, 2 * D), MM_DTYPE),
        compiler_params=_params())(dz, qg, o, dq)


def _mm_in(a, wg, out_dtype, tag, l=None):
    T, K = a.shape
    n = wg.shape[-1]
    tm = _tile(T, MM_TILE)
    if l is None:
        w_spec = _whole((N_CHIP, K, n))
    else:
        w_spec = pl.BlockSpec((None, N_CHIP, K, n), lambda i: (l, 0, 0, 0))

    def body(a_ref, w_ref, o_ref):
        av = a_ref[...]
        for s in range(N_CHIP):
            o_ref[:, s * n:(s + 1) * n] = jnp.dot(av, w_ref[s], preferred_element_type=F32).astype(o_ref.dtype)

    return pl.pallas_call(
        body, name=f"mm_in_{tag}", grid=(T // tm,),
        in_specs=[_rows(tm, K), w_spec], out_specs=_rows(tm, N_CHIP * n),
        out_shape=_sds((T, N_CHIP * n), out_dtype), compiler_params=_params())(a, wg)


def _mm_nt_in(dy, wg, tag, l=None):
    T = dy.shape[0]
    K, n = wg.shape[-2:]
    tm = _tile(T, MM_TILE)
    if l is None:
        w_spec = _whole((N_CHIP, K, n))
    else:
        w_spec = pl.BlockSpec((None, N_CHIP, K, n), lambda i: (l, 0, 0, 0))

    def body(d_ref, w_ref, o_ref):
        acc = None
        for s in range(N_CHIP):
            part = lax.dot_general(d_ref[:, s * n:(s + 1) * n], w_ref[s], NT, preferred_element_type=F32)
            acc = part if acc is None else acc + part
        o_ref[...] = acc

    return pl.pallas_call(
        body, name=f"mm_nt_in_{tag}", grid=(T // tm,),
        in_specs=[_rows(tm, N_CHIP * n), w_spec], out_specs=_rows(tm, K),
        out_shape=_sds((T, K), F32), compiler_params=_params())(dy, wg)


def _mm_nn(a, b, out_dtype, tag):
    T, K = a.shape
    N = b.shape[1]
    tm = _tile(T, MM_TILE)

    def body(a_ref, b_ref, o_ref):
        o_ref[...] = jnp.dot(a_ref[...], b_ref[...], preferred_element_type=F32).astype(o_ref.dtype)

    return pl.pallas_call(
        body, name=f"mm_nn_{tag}", grid=(T // tm,),
        in_specs=[_rows(tm, K), _whole((K, N))], out_specs=_rows(tm, N),
        out_shape=_sds((T, N), out_dtype), compiler_params=_params())(a, b)


def _mm_nt(a, b, out_dtype, tag):
    T, C = a.shape
    N = b.shape[0]
    tm = _tile(T, MM_TILE)
    nb = N
    for cand in (1408, 1024):
        if N > cand and N % cand == 0:
            nb = cand
            break

    def body(a_ref, b_ref, o_ref):
        o_ref[...] = lax.dot_general(a_ref[...], b_ref[...], NT, preferred_element_type=F32).astype(o_ref.dtype)

    return pl.pallas_call(
        body, name=f"mm_nt_{tag}", grid=(N // nb, T // tm),
        in_specs=[pl.BlockSpec((tm, C), lambda j, i: (i, 0)), pl.BlockSpec((nb, C), lambda j, i: (j, 0))],
        out_specs=pl.BlockSpec((tm, nb), lambda j, i: (i, j)),
        out_shape=_sds((T, N), out_dtype), compiler_params=_params())(a, b)


def _mm_tn_in(a, dy, tag, l=None, prev=None):
    T, K = a.shape
    n = dy.shape[1] // N_CHIP
    tt = _tile(T, MM_TILE)

    def body(a_ref, d_ref, *rest):
        o_ref = rest[-1]
        part = lax.dot_general(a_ref[...], d_ref[...], TN, preferred_element_type=F32)
        _accumulate(o_ref, part, pl.program_id(1) == 0)

    in_specs = [pl.BlockSpec((tt, K), lambda s, t: (t, 0)), pl.BlockSpec((tt, n), lambda s, t: (t, s))]
    args = [a, dy]
    kw = {}
    if l is None:
        out_spec = pl.BlockSpec((None, K, n), lambda s, t: (s, 0, 0))
        out_shape = _sds((N_CHIP, K, n), F32)
    else:
        out_spec = pl.BlockSpec((None, None, K, n), lambda s, t: (l, s, 0, 0))
        out_shape = _sds((2, N_CHIP, K, n), F32)
        if prev is not None:
            in_specs.append(ANY)
            args.append(prev)
            kw["input_output_aliases"] = {2: 0}
    return pl.pallas_call(
        body, name=f"mm_tn_in_{tag}", grid=(N_CHIP, T // tt),
        in_specs=in_specs, out_specs=out_spec, out_shape=out_shape,
        compiler_params=_params(), **kw)(*args)


def _mm_tn_out(act, dh, tag, l=None, prev=None):
    T, R4 = act.shape
    D = dh.shape[1]
    r = R4 // N_CHIP
    g = 1 if r % LANES == 0 else 2
    tt = _tile(T, MM_TILE)

    def body(a_ref, d_ref, *rest):
        o_ref = rest[-1]
        part = lax.dot_general(a_ref[...], d_ref[...], TN, preferred_element_type=F32)
        first = pl.program_id(1) == 0
        for q in range(g):
            _accumulate(o_ref.at[q], part[q * r:(q + 1) * r], first)

    in_specs = [pl.BlockSpec((tt, g * r), lambda s, t: (t, s)), pl.BlockSpec((tt, D), lambda s, t: (t, 0))]
    args = [act, dh]
    kw = {}
    if l is None:
        out_spec = pl.BlockSpec((g, r, D), lambda s, t: (s, 0, 0))
        out_shape = _sds((N_CHIP, r, D), F32)
    else:
        out_spec = pl.BlockSpec((None, g, r, D), lambda s, t: (l, s, 0, 0))
        out_shape = _sds((2, N_CHIP, r, D), F32)
        if prev is not None:
            in_specs.append(ANY)
            args.append(prev)
            kw["input_output_aliases"] = {2: 0}
    return pl.pallas_call(
        body, name=f"mm_tn_out_{tag}", grid=(N_CHIP // g, T // tt),
        in_specs=in_specs, out_specs=out_spec, out_shape=out_shape,
        compiler_params=_params(), **kw)(*args)


def _mm_tn(a, b, tag):
    T, K = a.shape
    N = b.shape[1]
    tt = _tile(T, MM_TILE)

    def body(a_ref, b_ref, o_ref):
        part = lax.dot_general(a_ref[...], b_ref[...], TN, preferred_element_type=F32)
        _accumulate(o_ref, part, pl.program_id(0) == 0)

    return pl.pallas_call(
        body, name=f"mm_tn_{tag}", grid=(T // tt,),
        in_specs=[_rows(tt, K), _rows(tt, N)], out_specs=_whole((K, N)),
        out_shape=_sds((K, N), F32), compiler_params=_params())(a, b)


def _adamw(w, g, m, v, tag):
    R, C = w.shape
    tr = _tile(R, ROW_TILE)

    def body(w_ref, g_ref, m_ref, v_ref, d_ref, mo_ref, vo_ref):
        gv = g_ref[...]
        mn = ADAM_B1 * m_ref[...] + (1.0 - ADAM_B1) * gv
        vn = ADAM_B2 * v_ref[...] + (1.0 - ADAM_B2) * (gv * gv)
        m_hat = mn / (1.0 - ADAM_B1 ** ADAM_STEP)
        v_hat = vn / (1.0 - ADAM_B2 ** ADAM_STEP)
        d_ref[...] = -ADAM_LR * (m_hat / (jnp.sqrt(v_hat) + ADAM_EPS) + ADAM_WD * w_ref[...])
        mo_ref[...] = mn
        vo_ref[...] = vn

    return pl.pallas_call(
        body, name=f"adamw_{tag}", grid=(R // tr,),
        in_specs=[_rows(tr, C)] * 4, out_specs=[_rows(tr, C)] * 3,
        out_shape=[_sds((R, C), F32)] * 3, compiler_params=_params())(w, g, m, v)


def _sum_devices(gall):
    _, R, C = gall.shape

    def body(g_ref, o_ref):
        acc = g_ref[0]
        for d in range(1, N_DEV):
            acc = acc + g_ref[d]
        o_ref[...] = acc

    return pl.pallas_call(
        body, name="sum_devices", in_specs=[_whole((N_DEV, R, C))], out_specs=_whole((R, C)),
        out_shape=_sds((R, C), F32), grid=(1,), compiler_params=_params())(gall)


HBM = pl.BlockSpec(memory_space=pltpu.HBM)
SEM = pl.BlockSpec(memory_space=pltpu.SEMAPHORE)
EFFECT = pltpu.SideEffectType.DATAFLOW_SIDE_EFFECTING


def _place():
    x, y, c = lax.axis_index("x"), lax.axis_index("y"), lax.axis_index("c")
    chips = ((1 - x, y), (x, 1 - y), (1 - x, 1 - y))
    return x, y, c, chips


def _remote(src, dst, send_sem, recv_sem, dev):
    return pltpu.make_async_remote_copy(src_ref=src, dst_ref=dst, send_sem=send_sem, recv_sem=recv_sem,
                                        device_id=dev, device_id_type=MESH)


def _in_hbm(a):
    return pltpu.with_memory_space_constraint(a, pltpu.HBM)


def _own_slot(w4, l, dtype, place, tag):
    _, _, r, col = w4.shape
    tr = _tile(r, 2 * ROW_TILE)

    def body(place_ref, x_ref, o_ref):
        o_ref[...] = x_ref[...].astype(o_ref.dtype)

    grid_spec = pltpu.PrefetchScalarGridSpec(
        num_scalar_prefetch=1, grid=(2, r // tr),
        in_specs=[pl.BlockSpec((None, None, tr, col), lambda h, i, p: (l, h, i, 0))],
        out_specs=pl.BlockSpec((None, None, tr, col), lambda h, i, p: (p[1], h, i, 0)))
    return pl.pallas_call(
        body, name=f"own_slot_{tag}", grid_spec=grid_spec, out_shape=_sds((N_CHIP, 2, r, col), dtype),
        compiler_params=_params())(place, w4)


def _allgather(bufs):
    n = len(bufs)

    def body(*refs):
        outs, token = refs[n:2 * n], refs[2 * n]
        s_ici, r_ici, s_d2d, r_d2d = refs[2 * n + 1:]
        x, y, c, chips = _place()
        me = 2 * x + y
        sib = (x, y, 1 - c)
        sends = []
        for i in range(n):
            mine = outs[i].at[me, c]
            for j, (px, py) in enumerate(chips):
                cp = _remote(mine, mine, s_ici.at[i, j], r_ici.at[i, j], (px, py, c))
                cp.start()
                sends.append(cp)
        for j, (px, py) in enumerate(chips):
            src = 2 * px + py
            for i in range(n):
                blk = outs[i].at[src, c]
                _remote(blk, blk, s_ici.at[i, j], r_ici.at[i, j], (px, py, c)).wait_recv()
                cp = _remote(blk, blk, s_d2d.at[i, j], r_d2d.at[i, j], sib)
                cp.start()
                sends.append(cp)
        for j, (px, py) in enumerate(chips):
            src = 2 * px + py
            for i in range(n):
                blk = outs[i].at[src, 1 - c]
                _remote(blk, blk, s_d2d.at[i, j], r_d2d.at[i, j], sib).wait_recv()
        for cp in sends:
            cp.wait_send()
        token[...] = jnp.zeros_like(token)

    dma = pltpu.SemaphoreType.DMA
    res = pl.pallas_call(
        body, name="allgather_weights", in_specs=[ANY] * n,
        out_specs=[ANY] * n + [pl.BlockSpec(memory_space=pltpu.VMEM)],
        out_shape=[_sds(b.shape, b.dtype) for b in bufs] + [_sds((8, LANES), F32)],
        input_output_aliases={i: i for i in range(n)},
        scratch_shapes=[dma((n, 3)), dma((n, 3)), dma((n, 3)), dma((n, 3))],
        )(*bufs)
    return res[:n], res[n]


def _gather_start(bufs, after):
    n = len(bufs)

    def body(*refs):
        ins = refs[:n]
        s_sem, r_sem, token = refs[n + 1], refs[n + 2], refs[2 * n + 3]
        x, y, c, chips = _place()
        me = 2 * x + y
        for i in range(n):
            mine = ins[i].at[me, c]
            for j, (px, py) in enumerate(chips):
                for tc in range(2):
                    _remote(mine, mine, s_sem.at[6 * i + 2 * j + tc], r_sem.at[6 * i + 2 * j + c], (px, py, tc)).start()
        token[...] = jnp.zeros_like(token)

    dma = pltpu.SemaphoreType.DMA
    res = pl.pallas_call(
        body, name="gather_start", in_specs=[HBM] * n + [ANY],
        out_specs=[SEM, SEM] + [HBM] * n + [pl.BlockSpec(memory_space=pltpu.VMEM)],
        out_shape=[dma((6 * n,)), dma((6 * n,))] + [pltpu.HBM(b.shape, b.dtype) for b in bufs] + [_sds((8, LANES), F32)],
        input_output_aliases={i: i + 2 for i in range(n)},
        compiler_params=pltpu.CompilerParams(has_side_effects=EFFECT),
        )(*[_in_hbm(b) for b in bufs], after)
    return res[0], res[1], res[2:2 + n], res[-1]


def _gather_wait(s_sem, r_sem, bufs, after):
    n = len(bufs)

    def body(*refs):
        ins = refs[:n]
        s_ref, r_ref = refs[n], refs[n + 1]
        x, y, c, chips = _place()
        me = 2 * x + y
        for i in range(n):
            mine = ins[i].at[me, c]
            for j, (px, py) in enumerate(chips):
                for tc in range(2):
                    _remote(mine, mine, s_ref.at[6 * i + 2 * j + tc], r_ref.at[6 * i + 2 * j + c], (px, py, tc)).wait_send()
            for j, (px, py) in enumerate(chips):
                for cs in range(2):
                    blk = ins[i].at[2 * px + py, cs]
                    k = 6 * i + 2 * j + cs
                    _remote(blk, blk, s_ref.at[k], r_ref.at[k], (px, py, cs)).wait_recv()

    return pl.pallas_call(
        body, name="gather_wait", in_specs=[HBM] * n + [SEM, SEM, ANY], out_specs=[HBM] * n,
        out_shape=[pltpu.HBM(b.shape, b.dtype) for b in bufs],
        input_output_aliases={i: i for i in range(n)},
        compiler_params=pltpu.CompilerParams(has_side_effects=EFFECT),
        )(*bufs, s_sem, r_sem, after)


def _rs_pair_send(grads):
    n = len(grads)

    def body(*refs):
        ins, outs = refs[:n], refs[n:2 * n]
        s_sem, r_sem = refs[2 * n:]
        x, y, c, _ = _place()
        sib = (x, y, 1 - c)
        sends = []
        for i in range(n):
            cp = _remote(ins[i].at[:, 1 - c], outs[i], s_sem.at[i], r_sem.at[i], sib)
            cp.start()
            sends.append(cp)
        for cp in sends:
            cp.wait()

    out_shape = [_sds((N_CHIP,) + g.shape[2:], g.dtype) for g in grads]
    dma = pltpu.SemaphoreType.DMA
    return pl.pallas_call(
        body, name=f"rs_pair_send_{n}", in_specs=[ANY] * n, out_specs=[ANY] * n, out_shape=out_shape,
        scratch_shapes=[dma((n,)), dma((n,))],
        )(*grads)


def _rs_pair_add(g, recv, place, tag):
    r, col = g.shape[-2:]
    tr = _tile(r, ROW_TILE)

    def body(place_ref, g_ref, r_ref, wire_ref, own_ref):
        tot = g_ref[...] + r_ref[...]
        wire_ref[...] = tot.astype(wire_ref.dtype)

        @pl.when(pl.program_id(1) == place_ref[1])
        def _():
            own_ref[...] = tot

    grid_spec = pltpu.PrefetchScalarGridSpec(
        num_scalar_prefetch=1, grid=(r // tr, N_CHIP),
        in_specs=[pl.BlockSpec((None, None, tr, col), lambda i, s, p: (s, p[0], i, 0)),
                  pl.BlockSpec((None, tr, col), lambda i, s, p: (s, i, 0))],
        out_specs=[pl.BlockSpec((None, tr, col), lambda i, s, p: (s, i, 0)),
                   pl.BlockSpec((tr, col), lambda i, s, p: (i, 0))])
    return pl.pallas_call(
        body, name=f"rs_pair_add_{tag}", grid_spec=grid_spec,
        out_shape=[_sds((N_CHIP, r, col), WIRE_DTYPE), _sds((r, col), F32)],
        compiler_params=_params())(place, g, recv)


def _rs_start(wires):
    n = len(wires)

    def body(*refs):
        ins = refs[:n]
        s_sem, r_sem, token = refs[2 * n], refs[2 * n + 1], refs[4 * n + 2]
        x, y, c, chips = _place()
        for i in range(n):
            land = refs[n + i]
            for j, (px, py) in enumerate(chips):
                _remote(ins[i].at[2 * px + py], land.at[j], s_sem.at[3 * i + j], r_sem.at[3 * i + j], (px, py, c)).start()
        token[...] = jnp.zeros_like(token)

    lands = [lax.empty((3,) + w.shape[1:], w.dtype) for w in wires]
    both = list(wires) + lands
    dma = pltpu.SemaphoreType.DMA
    res = pl.pallas_call(
        body, name="rs_start", in_specs=[HBM] * (2 * n),
        out_specs=[SEM, SEM] + [HBM] * (2 * n) + [pl.BlockSpec(memory_space=pltpu.VMEM)],
        out_shape=[dma((3 * n,)), dma((3 * n,))] + [pltpu.HBM(b.shape, b.dtype) for b in both] + [_sds((8, LANES), F32)],
        input_output_aliases={i: i + 2 for i in range(2 * n)},
        compiler_params=pltpu.CompilerParams(has_side_effects=EFFECT),
        )(*[_in_hbm(b) for b in both])
    return res[0], res[1], res[2:2 + n], res[2 + n:2 + 2 * n], res[-1]


def _rs_wait(s_sem, r_sem, wires, lands, after):
    n = len(wires)

    def body(*refs):
        ins = refs[:n]
        s_ref, r_ref = refs[2 * n], refs[2 * n + 1]
        x, y, c, chips = _place()
        for i in range(n):
            land = refs[n + i]
            for j, (px, py) in enumerate(chips):
                cp = _remote(ins[i].at[2 * px + py], land.at[j], s_ref.at[3 * i + j], r_ref.at[3 * i + j], (px, py, c))
                cp.wait_send()
                cp.wait_recv()

    both = list(wires) + list(lands)
    res = pl.pallas_call(
        body, name="rs_wait", in_specs=[HBM] * (2 * n) + [SEM, SEM, ANY], out_specs=[HBM] * (2 * n),
        out_shape=[pltpu.HBM(b.shape, b.dtype) for b in both],
        input_output_aliases={i: i for i in range(2 * n)},
        compiler_params=pltpu.CompilerParams(has_side_effects=EFFECT),
        )(*both, s_sem, r_sem, after)
    return res[n:]


def _rs_chip_send(wires, small):
    n = len(wires)

    def body(*refs):
        ins, small_ref = refs[:n], refs[n]
        outs, gall_ref = refs[n + 1:2 * n + 1], refs[2 * n + 1]
        s_sem, r_sem, s_small, r_small, s_loc = refs[2 * n + 2:]
        x, y, c, chips = _place()
        me = 4 * x + 2 * y + c
        sends = []
        loc = pltpu.make_async_copy(small_ref, gall_ref.at[me], s_loc)
        loc.start()
        for i in range(n):
            for j, (px, py) in enumerate(chips):
                cp = _remote(ins[i].at[2 * px + py], outs[i].at[j], s_sem.at[i, j], r_sem.at[i, j], (px, py, c))
                cp.start()
                sends.append(cp)
        for k in range(1, N_DEV):
            peer = (x ^ (k >> 2), y ^ ((k >> 1) & 1), c ^ (k & 1))
            cp = _remote(small_ref, gall_ref.at[me], s_small.at[k - 1], r_small.at[k - 1], peer)
            cp.start()
            sends.append(cp)
        for cp in sends:
            cp.wait()
        loc.wait()

    out_shape = [_sds((3,) + w.shape[1:], w.dtype) for w in wires] + [_sds((N_DEV,) + small.shape, small.dtype)]
    dma = pltpu.SemaphoreType.DMA
    return pl.pallas_call(
        body, name="rs_chip_send", in_specs=[ANY] * (n + 1), out_specs=[ANY] * (n + 1), out_shape=out_shape,
        scratch_shapes=[dma((n, 3)), dma((n, 3)), dma((N_DEV - 1,)), dma((N_DEV - 1,)), dma(())],
        )(*wires, small)


def _rs_chip_add(own, recv, place, l, L, prev, tag):
    r, col = own.shape
    tr = _tile(r, ROW_TILE)

    def body(place_ref, o_ref, r_ref, *rest):
        acc = o_ref[...]
        for j in range(3):
            acc = acc + r_ref[j].astype(F32)
        rest[-1][...] = acc

    in_specs = [pl.BlockSpec((tr, col), lambda i, p: (i, 0)), pl.BlockSpec((3, tr, col), lambda i, p: (0, i, 0))]
    args = [place, own, recv]
    kw = {}
    if prev is not None:
        in_specs.append(ANY)
        args.append(prev)
        kw["input_output_aliases"] = {3: 0}
    grid_spec = pltpu.PrefetchScalarGridSpec(
        num_scalar_prefetch=1, grid=(r // tr,), in_specs=in_specs,
        out_specs=pl.BlockSpec((None, None, tr, col), lambda i, p: (l, p[0], i, 0)))
    return pl.pallas_call(
        body, name=f"rs_chip_add_{tag}", grid_spec=grid_spec, out_shape=_sds((L, 2, r, col), F32),
        compiler_params=_params(), **kw)(*args)


def _rs_pair_share(fulls):
    n = len(fulls)

    def body(*refs):
        outs = refs[n:2 * n]
        s_sem, r_sem = refs[2 * n:]
        x, y, c, _ = _place()
        sib = (x, y, 1 - c)
        started = []
        for i in range(n):
            cp = _remote(outs[i].at[:, c], outs[i].at[:, c], s_sem.at[i], r_sem.at[i], sib)
            cp.start()
            started.append(cp)
        for i, cp in enumerate(started):
            cp.wait_send()
            _remote(outs[i].at[:, 1 - c], outs[i].at[:, 1 - c], s_sem.at[i], r_sem.at[i], sib).wait_recv()

    dma = pltpu.SemaphoreType.DMA
    return pl.pallas_call(
        body, name="rs_pair_share", in_specs=[ANY] * n, out_specs=[ANY] * n,
        out_shape=[_sds(f.shape, f.dtype) for f in fulls],
        input_output_aliases={i: i for i in range(n)},
        scratch_shapes=[dma((n,)), dma((n,))],
        )(*fulls)


def _ffn_fwd(x, g_pre, g_post, w_in, w_out, tag):
    xn = _rms_fwd(x, g_pre, tag)
    hgu = _mm_in(xn, w_in, MM_DTYPE, tag)
    act = _swiglu_fwd(hgu, tag)
    h = _mm_nn(act, w_out.reshape(-1, w_out.shape[-1]), F32, tag)
    return _post_fwd(x, h, g_post, 0.5, tag), (x, xn, hgu, act, h)


def _ffn_bwd(dx, saved, g_pre, g_post, w_in, w_out, tag):
    x, xn, hgu, act, h = saved
    dh, dg_post = _post_bwd(dx, h, g_post, 0.5, tag)
    dact = _mm_nt(dh, w_out.reshape(-1, w_out.shape[-1]), MM_DTYPE, f"{tag}_out")
    dw_out = _mm_tn_out(act, dh, tag)
    dhgu = _swiglu_bwd(hgu, dact, tag)
    dxn = _mm_nt_in(dhgu, w_in, tag)
    dw_in = _mm_tn_in(xn, dhgu, tag)
    dx_in, dg_pre = _pre_bwd(dx, dxn, x, g_pre, tag)
    return dx_in, dg_pre, dg_post, dw_in, dw_out


def kernel(x, ffn1_pre_g, ffn1_post_g, ffn1_w_in, ffn1_w_out, mix_pre_g, mix_post_g, ffn2_pre_g, ffn2_post_g, ffn2_w_in, ffn2_w_out, conv_w_in, conv_k, conv_w_out, kv_g, kv_w, forget_b, attn_w_qg, attn_w_o, loss_target, m_ffn1_pre_g, m_ffn1_post_g, m_ffn1_w_in, m_ffn1_w_out, m_mix_pre_g, m_mix_post_g, m_ffn2_pre_g, m_ffn2_post_g, m_ffn2_w_in, m_ffn2_w_out, m_conv_w_in, m_conv_k, m_conv_w_out, m_kv_g, m_kv_w, m_forget_b, m_attn_w_qg, m_attn_w_o, v_ffn1_pre_g, v_ffn1_post_g, v_ffn1_w_in, v_ffn1_w_out, v_mix_pre_g, v_mix_post_g, v_ffn2_pre_g, v_ffn2_post_g, v_ffn2_w_in, v_ffn2_w_out, v_conv_w_in, v_conv_k, v_conv_w_out, v_kv_g, v_kv_w, v_forget_b, v_attn_w_qg, v_attn_w_o):
    Bl, S, D = x.shape
    T = Bl * S
    H = forget_b.shape[0]
    assert D == H * HEAD_DIM and D % LANES == 0
    kvc = kv_w.shape[1]
    kvp = -(-kvc // LANES) * LANES
    kv_all = 2 * D + LANES
    dk_cols = conv_k.shape[2]
    chip = 2 * lax.axis_index("x") + lax.axis_index("y")
    core = lax.axis_index("c")

    given = dict(ffn1_w_in=(ffn1_w_in, m_ffn1_w_in, v_ffn1_w_in), ffn1_w_out=(ffn1_w_out, m_ffn1_w_out, v_ffn1_w_out),
                 ffn2_w_in=(ffn2_w_in, m_ffn2_w_in, v_ffn2_w_in), ffn2_w_out=(ffn2_w_out, m_ffn2_w_out, v_ffn2_w_out),
                 conv_w_in=(conv_w_in, m_conv_w_in, v_conv_w_in), conv_w_out=(conv_w_out, m_conv_w_out, v_conv_w_out),
                 kv_w=(kv_w, m_kv_w, v_kv_w), attn_w_qg=(attn_w_qg, m_attn_w_qg, v_attn_w_qg),
                 attn_w_o=(attn_w_o, m_attn_w_o, v_attn_w_o))
    shards = {k: w for k, (w, _, _) in given.items()}
    shards["kv_w"] = jnp.pad(kv_w, ((0, 0), (0, kvp - kvc)))[None]
    first = [("ffn1_w_in", 0), ("ffn1_w_out", 0), ("conv_w_in", 0), ("conv_w_out", 0), ("ffn2_w_in", 0), ("ffn2_w_out", 0)]
    second = [("kv_w", 0), ("ffn1_w_in", 1), ("ffn1_w_out", 1), ("attn_w_qg", 0), ("attn_w_o", 0),
              ("ffn2_w_in", 1), ("ffn2_w_out", 1)]
    place = jnp.stack([core, chip]).astype(jnp.int32)

    def slot(key, where):
        w = shards[key[0]]
        L, r, col = w.shape
        return _own_slot(w.reshape(L, 2, r // 2, col), key[1], MM_DTYPE, where, f"{key[0]}{key[1]}")

    def whole(g):
        return g.reshape(N_CHIP, -1, g.shape[-1])

    taps_slot = _own_slot(jnp.pad(conv_k[0], ((0, 13), (0, 0))).reshape(1, 2, 8, dk_cols), 0, F32, place, "conv_k")
    got, token = _allgather([slot(key, place) for key in first] + [taps_slot])
    W = {key: whole(g) for key, g in zip(first, got)}
    k_taps = got[-1].reshape(N_CHIP, 16, dk_cols).transpose(1, 0, 2).reshape(16, D)[:8]
    g_sems, g_semr, flying, token = _gather_start([slot(key, place) for key in second], token)
    w_o_conv = W["conv_w_out", 0].reshape(D, D)
    fb = jnp.pad(forget_b, (0, LANES - H)).reshape(1, LANES)

    x0 = x.reshape(T, D)
    x1, s_f1a = _ffn_fwd(x0, ffn1_pre_g[0] + token[0, :1], ffn1_post_g[0], W["ffn1_w_in", 0], W["ffn1_w_out", 0], "l0f1")
    xn_c = _rms_fwd(x1, mix_pre_g[0], "l0mix")
    bch = _mm_in(xn_c, W["conv_w_in", 0], MM_DTYPE, "conv")
    z_c = _conv_fwd(bch, k_taps, Bl, S)
    m_c = _mm_nn(z_c, w_o_conv, F32, "conv_out")
    x2 = _post_fwd(x1, m_c, mix_post_g[0], 1.0, "l0mix")
    x3, s_f2a = _ffn_fwd(x2, ffn2_pre_g[0], ffn2_post_g[0], W["ffn2_w_in", 0], W["ffn2_w_out", 0], "l0f2")

    landed = _gather_wait(g_sems, g_semr, flying, x3)
    W.update({key: whole(g) for key, g in zip(second, landed)})
    kv_full = jnp.concatenate([W["kv_w", 0][s, :, :kvc] for s in range(N_CHIP)], axis=1)
    kv_full = jnp.pad(kv_full, ((0, 0), (0, kv_all - kv_full.shape[1])))
    w_o_attn = W["attn_w_o", 0].reshape(D, D)
    xn_kv = _rms_fwd(x3, kv_g, "kv")
    kvact = _mm_nn(xn_kv, kv_full[:, :2 * D], MM_DTYPE, "kv")
    pf = _mm_nn(xn_kv, kv_full[:, 2 * D:], F32, "forget")
    cum = _forget_fwd(pf, fb, Bl, S)
    bq = min(S, ATT_BLOCK)
    c3 = cum.reshape(Bl, S, LANES)[:, :, :H].transpose(0, 2, 1)
    c_col = jnp.broadcast_to(c3[..., None], (Bl, H, S, LANES))
    c_row = c3.reshape(Bl, H, S // bq, 1, bq)

    x4, s_f1b = _ffn_fwd(x3, ffn1_pre_g[1], ffn1_post_g[1], W["ffn1_w_in", 1], W["ffn1_w_out", 1], "l1f1")
    xn_a = _rms_fwd(x4, mix_pre_g[1], "l1mix")
    qg = _mm_in(xn_a, W["attn_w_qg", 0], MM_DTYPE, "qg")
    o, lse = _attn_fwd(qg, kvact, c_col, c_row, Bl, S, D)
    z_a = _gate_fwd(qg, o)
    m_a = _mm_nn(z_a, w_o_attn, F32, "attn_out")
    x5 = _post_fwd(x4, m_a, mix_post_g[1], 1.0, "l1mix")
    x6, s_f2b = _ffn_fwd(x5, ffn2_pre_g[1], ffn2_post_g[1], W["ffn2_w_in", 1], W["ffn2_w_out", 1], "l1f2")

    dy, loss_local = _loss_grad(x6, loss_target.reshape(T, D))
    loss = lax.psum(loss_local, ("x", "y", "c"))

    G = {}
    dx5, dg_f2pre_1, dg_f2post_1, G["ffn2_w_in", 1], G["ffn2_w_out", 1] = _ffn_bwd(
        dy, s_f2b, ffn2_pre_g[1], ffn2_post_g[1], W["ffn2_w_in", 1], W["ffn2_w_out", 1], "l1f2")
    dm_a, dg_mixpost_1 = _post_bwd(dx5, m_a, mix_post_g[1], 1.0, "l1mix")
    dz_a = _mm_nt(dm_a, w_o_attn, MM_DTYPE, "attn_out")
    G["attn_w_o", 0] = _mm_tn_out(z_a, dm_a, "attn_out")
    do = _gate_do(dz_a, qg)
    dq, dk, dv, dcr = _attn_bwd(qg, kvact, do, lse, c_col, c_row, Bl, S, D)
    dqg = _gate_bwd(dz_a, qg, o, dq)
    dxn_a = _mm_nt_in(dqg, W["attn_w_qg", 0], "qg")
    G["attn_w_qg", 0] = _mm_tn_in(xn_a, dqg, "qg")
    dx4, dg_mixpre_1 = _pre_bwd(dx5, dxn_a, x4, mix_pre_g[1], "l1mix")
    dx3, dg_f1pre_1, dg_f1post_1, G["ffn1_w_in", 1], G["ffn1_w_out", 1] = _ffn_bwd(
        dx4, s_f1b, ffn1_pre_g[1], ffn1_post_g[1], W["ffn1_w_in", 1], W["ffn1_w_out", 1], "l1f1")

    dcum = jnp.pad(dcr.reshape(Bl, H, S).transpose(0, 2, 1), ((0, 0), (0, 0), (0, LANES - H))).reshape(T, LANES)
    dpf, dfb = _forget_bwd(dcum, pf, fb, Bl, S)
    dp = jnp.concatenate([dk.astype(MM_DTYPE), dv.astype(MM_DTYPE), dpf], axis=1)
    dxn_kv = _mm_nt(dp, kv_full, F32, "kv")
    G_kv_full = _mm_tn(xn_kv, dp, "kv")
    G["kv_w", 0] = jnp.stack([jnp.pad(G_kv_full[:, s * kvc:(s + 1) * kvc], ((0, 0), (0, kvp - kvc))) for s in range(N_CHIP)])
    dx3, dg_kv = _pre_bwd(dx3, dxn_kv, x3, kv_g, "kv")

    def pair_sums(keys):
        grads = [G[k].reshape(N_CHIP, 2, G[k].shape[1] // 2, G[k].shape[2]) for k in keys]
        wires, owns = [], []
        for k, g, r in zip(keys, grads, _rs_pair_send(grads)):
            w, own = _rs_pair_add(g, r, place, f"{k[0]}{k[1]}")
            wires.append(w)
            owns.append(own)
        return wires, owns

    wires_2, owns_2 = pair_sums(second)
    r_sems, r_semr, wires_2, lands_2, token = _rs_start(wires_2)

    dx2, dg_f2pre_0, dg_f2post_0, G["ffn2_w_in", 0], G["ffn2_w_out", 0] = _ffn_bwd(
        dx3, s_f2a, ffn2_pre_g[0], ffn2_post_g[0] + token[0, :1], W["ffn2_w_in", 0], W["ffn2_w_out", 0], "l0f2")
    dm_c, dg_mixpost_0 = _post_bwd(dx2, m_c, mix_post_g[0], 1.0, "l0mix")
    dz_c = _mm_nt(dm_c, w_o_conv, MM_DTYPE, "conv_out")
    G["conv_w_out", 0] = _mm_tn_out(z_c, dm_c, "conv_out")
    db, dcg, dhh, dk_taps = _conv_bwd(bch, dz_c, k_taps, Bl, S)
    dbch = jnp.concatenate([db, dcg, dhh], axis=1)
    dxn_c = _mm_nt_in(dbch, W["conv_w_in", 0], "conv")
    G["conv_w_in", 0] = _mm_tn_in(xn_c, dbch, "conv")
    dx1, dg_mixpre_0 = _pre_bwd(dx2, dxn_c, x1, mix_pre_g[0], "l0mix")
    dx0, dg_f1pre_0, dg_f1post_0, G["ffn1_w_in", 0], G["ffn1_w_out", 0] = _ffn_bwd(
        dx1, s_f1a, ffn1_pre_g[0], ffn1_post_g[0], W["ffn1_w_in", 0], W["ffn1_w_out", 0], "l0f1")
    grad_x = dx0.reshape(Bl, S, D)

    recvs_2 = _rs_wait(r_sems, r_semr, wires_2, lands_2, dx0)
    wires_1, owns_1 = pair_sums(first)

    def row(v):
        return jnp.pad(v.reshape(-1), (0, D - v.size)).reshape(1, D)

    small_parts = [dg_f1pre_0, dg_f1pre_1, dg_f1post_0, dg_f1post_1, dg_mixpre_0, dg_mixpre_1, dg_mixpost_0, dg_mixpost_1,
                   dg_f2pre_0, dg_f2pre_1, dg_f2post_0, dg_f2post_1, dg_kv, row(dfb[0, :H]), dk_taps[:3]]
    small = jnp.concatenate(small_parts, axis=0)
    small = jnp.pad(small, ((0, SMALL_ROWS - small.shape[0]), (0, 0)))
    outs = _rs_chip_send(wires_1, small)
    recvs_1, gall = outs[:-1], outs[-1]
    partial = {}
    for key, own, rcv in zip(first + second, owns_1 + owns_2, list(recvs_1) + list(recvs_2)):
        name, l = key
        partial[name] = _rs_chip_add(own, rcv, place, l, shards[name].shape[0], partial.get(name), f"{name}{l}")
    names = list(partial)
    reduced = _rs_pair_share([partial[k] for k in names])
    gsum = _sum_devices(gall)

    res = {}
    for k, red in zip(names, reduced):
        w, m, v = given[k]
        g2 = red.reshape(-1, red.shape[-1])
        if k == "kv_w":
            g2 = g2[:, :kvc]
        flat = lambda a: a.reshape(-1, a.shape[-1])
        d, mn, vn = _adamw(flat(w), g2, flat(m), flat(v), k)
        res[k] = tuple(a.reshape(w.shape) for a in (g2, d, mn, vn))

    small_names = ["ffn1_pre_g", "ffn1_post_g", "mix_pre_g", "mix_post_g", "ffn2_pre_g", "ffn2_post_g"]
    small_given = dict(ffn1_pre_g=(ffn1_pre_g, m_ffn1_pre_g, v_ffn1_pre_g), ffn1_post_g=(ffn1_post_g, m_ffn1_post_g, v_ffn1_post_g),
                       mix_pre_g=(mix_pre_g, m_mix_pre_g, v_mix_pre_g), mix_post_g=(mix_post_g, m_mix_post_g, v_mix_post_g),
                       ffn2_pre_g=(ffn2_pre_g, m_ffn2_pre_g, v_ffn2_pre_g), ffn2_post_g=(ffn2_post_g, m_ffn2_post_g, v_ffn2_post_g))

    def pack(idx):
        rows_ = [small_given[k][idx] for k in small_names]
        rows_ += [row((kv_g, m_kv_g, v_kv_g)[idx]), row((forget_b, m_forget_b, v_forget_b)[idx])]
        rows_.append(jnp.pad((conv_k, m_conv_k, v_conv_k)[idx][0], ((0, 0), (0, D - dk_cols))))
        a = jnp.concatenate(rows_, axis=0)
        return jnp.pad(a, ((0, SMALL_ROWS - a.shape[0]), (0, 0)))

    g_taps = lax.dynamic_slice_in_dim(gsum[14:17], chip * dk_cols, dk_cols, axis=1)
    g_small = jnp.concatenate([gsum[:14], jnp.pad(g_taps, ((0, 0), (0, D - dk_cols))), gsum[17:]], axis=0)
    d_s, m_s, v_s = _adamw(pack(0), g_small, pack(1), pack(2), "small")
    for i, k in enumerate(small_names):
        res[k] = tuple(a[2 * i:2 * i + 2] for a in (g_small, d_s, m_s, v_s))
    res["kv_g"] = tuple(a[12] for a in (g_small, d_s, m_s, v_s))
    res["forget_b"] = tuple(a[13, :H] for a in (g_small, d_s, m_s, v_s))
    res["conv_k"] = tuple(a[14:17, :dk_cols][None] for a in (g_small, d_s, m_s, v_s))

    order = ["ffn1_pre_g", "ffn1_post_g", "ffn1_w_in", "ffn1_w_out", "mix_pre_g", "mix_post_g", "ffn2_pre_g", "ffn2_post_g",
             "ffn2_w_in", "ffn2_w_out", "conv_w_in", "conv_k", "conv_w_out", "kv_g", "kv_w", "forget_b", "attn_w_qg", "attn_w_o"]
    out = [loss, grad_x]
    for idx in range(4):
        out += [res[k][idx] for k in order]
    return tuple(out)
```

```python
import functools
import math

import jax
import jax.numpy as jnp
from jax import lax
from jax.experimental import pallas as pl
from jax.experimental.pallas import tpu as pltpu

F32 = jnp.float32
MM_DTYPE = jnp.bfloat16
WIRE_DTYPE = jnp.bfloat16

RMS_EPS = 1e-6
ADAM_LR = 0.001
ADAM_B1 = 0.9
ADAM_B2 = 0.999
ADAM_EPS = 1e-08
ADAM_WD = 0.01
ADAM_STEP = 10

HEAD_DIM = 64
LANES = 128
N_CHIP = 4
N_DEV = 8
ROW_TILE = 256
MM_TILE = 512
ATT_BLOCK = 512
SMALL_ROWS = 24
VMEM_LIMIT = 56 * 1024 * 1024
MESH = pl.DeviceIdType.MESH
ANY = pl.BlockSpec(memory_space=pl.ANY)

NT = (((1,), (1,)), ((), ()))
TN = (((0,), (0,)), ((), ()))


def _tile(n, pref):
    if n <= pref:
        return n
    t = pref - pref % 16
    while n % t:
        t -= 16
    return t


def _params():
    return pltpu.CompilerParams(vmem_limit_bytes=VMEM_LIMIT)


def _sds(shape, dtype):
    return jax.ShapeDtypeStruct(shape, dtype)


def _rows(tm, c):
    return pl.BlockSpec((tm, c), lambda i: (i, 0))


def _whole(shape):
    return pl.BlockSpec(shape, lambda *_: (0,) * len(shape))


def _rms_fwd(x, g, tag):
    T, D = x.shape
    tm = _tile(T, ROW_TILE)

    def body(x_ref, g_ref, o_ref):
        xv = x_ref[...]
        r = lax.rsqrt(jnp.mean(xv * xv, axis=-1, keepdims=True) + RMS_EPS)
        o_ref[...] = (xv * r * g_ref[...]).astype(o_ref.dtype)

    return pl.pallas_call(
        body, name=f"rms_fwd_{tag}", grid=(T // tm,),
        in_specs=[_rows(tm, D), _whole((1, D))], out_specs=_rows(tm, D),
        out_shape=_sds((T, D), MM_DTYPE), compiler_params=_params())(x, g.reshape(1, D))


def _post_fwd(x, h, g, alpha, tag):
    T, D = x.shape
    tm = _tile(T, ROW_TILE)

    def body(x_ref, h_ref, g_ref, o_ref):
        hv = h_ref[...]
        r = lax.rsqrt(jnp.mean(hv * hv, axis=-1, keepdims=True) + RMS_EPS)
        o_ref[...] = x_ref[...] + alpha * (hv * r * g_ref[...])

    return pl.pallas_call(
        body, name=f"post_fwd_{tag}", grid=(T // tm,),
        in_specs=[_rows(tm, D), _rows(tm, D), _whole((1, D))], out_specs=_rows(tm, D),
        out_shape=_sds((T, D), F32), compiler_params=_params())(x, h, g.reshape(1, D))


def _accumulate(ref, part, first):
    @pl.when(first)
    def _():
        ref[...] = part

    @pl.when(jnp.logical_not(first))
    def _():
        ref[...] += part


def _post_bwd(dx, h, g, alpha, tag):
    T, D = dx.shape
    tm = _tile(T, ROW_TILE)

    def body(dx_ref, h_ref, g_ref, dh_ref, dg_ref):
        hv = h_ref[...]
        r = lax.rsqrt(jnp.mean(hv * hv, axis=-1, keepdims=True) + RMS_EPS)
        hh = hv * r
        dyn = alpha * dx_ref[...]
        _accumulate(dg_ref, jnp.sum(dyn * hh, axis=0, keepdims=True), pl.program_id(0) == 0)
        dhh = dyn * g_ref[...]
        dh = r * (dhh - hh * jnp.mean(dhh * hh, axis=-1, keepdims=True))
        dh_ref[...] = dh.astype(dh_ref.dtype)

    return pl.pallas_call(
        body, name=f"post_bwd_{tag}", grid=(T // tm,),
        in_specs=[_rows(tm, D), _rows(tm, D), _whole((1, D))],
        out_specs=[_rows(tm, D), _whole((1, D))],
        out_shape=[_sds((T, D), MM_DTYPE), _sds((1, D), F32)],
        compiler_params=_params())(dx, h, g.reshape(1, D))


def _pre_bwd(dres, dxn, x, g, tag):
    T, D = x.shape
    tm = _tile(T, ROW_TILE)

    def body(dres_ref, dxn_ref, x_ref, g_ref, dx_ref, dg_ref):
        xv = x_ref[...]
        r = lax.rsqrt(jnp.mean(xv * xv, axis=-1, keepdims=True) + RMS_EPS)
        xh = xv * r
        dn = dxn_ref[...]
        _accumulate(dg_ref, jnp.sum(dn * xh, axis=0, keepdims=True), pl.program_id(0) == 0)
        dxh = dn * g_ref[...]
        dx_ref[...] = dres_ref[...] + r * (dxh - xh * jnp.mean(dxh * xh, axis=-1, keepdims=True))

    return pl.pallas_call(
        body, name=f"pre_bwd_{tag}", grid=(T // tm,),
        in_specs=[_rows(tm, D), _rows(tm, D), _rows(tm, D), _whole((1, D))],
        out_specs=[_rows(tm, D), _whole((1, D))],
        out_shape=[_sds((T, D), F32), _sds((1, D), F32)],
        compiler_params=_params())(dres, dxn, x, g.reshape(1, D))


def _swiglu_fwd(hgu, tag):
    T, F2 = hgu.shape
    F = F2 // 2
    tm = _tile(T, ROW_TILE)

    def body(g_ref, u_ref, o_ref):
        g = g_ref[...].astype(F32)
        o_ref[...] = (g * jax.nn.sigmoid(g) * u_ref[...].astype(F32)).astype(o_ref.dtype)

    return pl.pallas_call(
        body, name=f"swiglu_fwd_{tag}", grid=(T // tm,),
        in_specs=[pl.BlockSpec((tm, F), lambda i: (i, 0)), pl.BlockSpec((tm, F), lambda i: (i, 1))],
        out_specs=_rows(tm, F), out_shape=_sds((T, F), MM_DTYPE), compiler_params=_params())(hgu, hgu)


def _swiglu_bwd(hgu, da, tag):
    T, F2 = hgu.shape
    F = F2 // 2
    tm = _tile(T, ROW_TILE)

    def body(h_ref, da_ref, o_ref):
        g = h_ref[:, :F].astype(F32)
        u = h_ref[:, F:].astype(F32)
        d = da_ref[...].astype(F32)
        sg = jax.nn.sigmoid(g)
        o_ref[:, :F] = (d * u * sg * (1.0 + g * (1.0 - sg))).astype(o_ref.dtype)
        o_ref[:, F:] = (d * g * sg).astype(o_ref.dtype)

    return pl.pallas_call(
        body, name=f"swiglu_bwd_{tag}", grid=(T // tm,),
        in_specs=[_rows(tm, F2), _rows(tm, F)], out_specs=_rows(tm, F2),
        out_shape=_sds((T, F2), MM_DTYPE), compiler_params=_params())(hgu, da)


def _loss_grad(y, tgt):
    T, D = y.shape
    tm = _tile(T, ROW_TILE)

    def body(y_ref, t_ref, dy_ref, l_ref):
        e = y_ref[...] - t_ref[...]
        row = jnp.mean(e * e, axis=-1, keepdims=True)
        part = jnp.broadcast_to(jnp.sum(row, axis=0, keepdims=True), (8, LANES))
        _accumulate(l_ref, part, pl.program_id(0) == 0)
        dy_ref[...] = e * (1.0 / D)

    dy, lsum = pl.pallas_call(
        body, name="loss_grad", grid=(T // tm,),
        in_specs=[_rows(tm, D), _rows(tm, D)], out_specs=[_rows(tm, D), _whole((8, LANES))],
        out_shape=[_sds((T, D), F32), _sds((8, LANES), F32)], compiler_params=_params())(y, tgt)
    return dy, 0.5 * lsum[0, 0]


def _shift_down(u, d, rows):
    return jnp.where(rows >= d, pltpu.roll(u, d, 0), 0.0)


def _shift_up(u, d, rows, S):
    return jnp.where(rows < S - d, pltpu.roll(u, S - d, 0), 0.0)


def _conv_fwd(bch, k8, Bl, S):
    T, D3 = bch.shape
    D = D3 // 3
    dc = min(D, 2 * LANES)
    nd = D // dc

    def body(b_ref, c_ref, h_ref, k_ref, z_ref):
        rows = lax.broadcasted_iota(jnp.int32, (S, 1), 0)
        u = c_ref[...].astype(F32) * h_ref[...].astype(F32)
        y = k_ref[2:3, :] * u + k_ref[1:2, :] * _shift_down(u, 1, rows) + k_ref[0:1, :] * _shift_down(u, 2, rows)
        z_ref[...] = (b_ref[...].astype(F32) * y).astype(z_ref.dtype)

    return pl.pallas_call(
        body, name="conv_fwd", grid=(Bl, nd),
        in_specs=[pl.BlockSpec((S, dc), lambda b, j: (b, j)),
                  pl.BlockSpec((S, dc), lambda b, j: (b, nd + j)),
                  pl.BlockSpec((S, dc), lambda b, j: (b, 2 * nd + j)),
                  pl.BlockSpec((8, dc), lambda b, j: (0, j))],
        out_specs=pl.BlockSpec((S, dc), lambda b, j: (b, j)),
        out_shape=_sds((T, D), MM_DTYPE), compiler_params=_params())(bch, bch, bch, k8)


def _conv_bwd(bch, dz, k8, Bl, S):
    T, D3 = bch.shape
    D = D3 // 3
    dc = min(D, 2 * LANES)
    nd = D // dc

    def body(b_ref, c_ref, h_ref, dz_ref, k_ref, db_ref, dc_ref, dh_ref, dk_ref):
        rows = lax.broadcasted_iota(jnp.int32, (S, 1), 0)
        bv = b_ref[...].astype(F32)
        cv = c_ref[...].astype(F32)
        hv = h_ref[...].astype(F32)
        dzv = dz_ref[...].astype(F32)
        u = cv * hv
        u1 = _shift_down(u, 1, rows)
        u2 = _shift_down(u, 2, rows)
        y = k_ref[2:3, :] * u + k_ref[1:2, :] * u1 + k_ref[0:1, :] * u2
        db_ref[...] = (dzv * y).astype(db_ref.dtype)
        dy = dzv * bv
        du = k_ref[2:3, :] * dy + k_ref[1:2, :] * _shift_up(dy, 1, rows, S) + k_ref[0:1, :] * _shift_up(dy, 2, rows, S)
        dc_ref[...] = (du * hv).astype(dc_ref.dtype)
        dh_ref[...] = (du * cv).astype(dh_ref.dtype)

        @pl.when(pl.program_id(1) == 0)
        def _():
            dk_ref[...] = jnp.zeros_like(dk_ref)

        dk_ref[0:1, :] += jnp.sum(dy * u2, axis=0, keepdims=True)
        dk_ref[1:2, :] += jnp.sum(dy * u1, axis=0, keepdims=True)
        dk_ref[2:3, :] += jnp.sum(dy * u, axis=0, keepdims=True)

    seq = lambda off: pl.BlockSpec((S, dc), lambda j, b: (b, off + j))
    return pl.pallas_call(
        body, name="conv_bwd", grid=(nd, Bl),
        in_specs=[seq(0), seq(nd), seq(2 * nd), seq(0), pl.BlockSpec((8, dc), lambda j, b: (0, j))],
        out_specs=[seq(0), seq(0), seq(0), pl.BlockSpec((8, dc), lambda j, b: (0, j))],
        out_shape=[_sds((T, D), MM_DTYPE)] * 3 + [_sds((8, D), F32)],
        compiler_params=_params())(bch, bch, bch, dz, k8)


def _forget_fwd(pf, fb, Bl, S):
    T = pf.shape[0]

    def body(p_ref, fb_ref, c_ref):
        rows = lax.broadcasted_iota(jnp.int32, (S, 1), 0)
        z = p_ref[...] + fb_ref[...]
        acc = jnp.minimum(z, 0.0) - jnp.log1p(jnp.exp(-jnp.abs(z)))
        d = 1
        while d < S:
            acc = acc + _shift_down(acc, d, rows)
            d *= 2
        c_ref[...] = acc

    return pl.pallas_call(
        body, name="forget_fwd", grid=(Bl,),
        in_specs=[_rows(S, LANES), _whole((1, LANES))], out_specs=_rows(S, LANES),
        out_shape=_sds((T, LANES), F32), compiler_params=_params())(pf, fb)


def _forget_bwd(dc, pf, fb, Bl, S):
    T = pf.shape[0]

    def body(dc_ref, p_ref, fb_ref, df_ref, dfb_ref):
        rows = lax.broadcasted_iota(jnp.int32, (S, 1), 0)
        acc = dc_ref[...]
        d = 1
        while d < S:
            acc = acc + _shift_up(acc, d, rows, S)
            d *= 2
        df = acc * jax.nn.sigmoid(-(p_ref[...] + fb_ref[...]))
        df_ref[...] = df.astype(df_ref.dtype)
        _accumulate(dfb_ref, jnp.sum(df, axis=0, keepdims=True), pl.program_id(0) == 0)

    return pl.pallas_call(
        body, name="forget_bwd", grid=(Bl,),
        in_specs=[_rows(S, LANES), _rows(S, LANES), _whole((1, LANES))],
        out_specs=[_rows(S, LANES), _whole((1, LANES))],
        out_shape=[_sds((T, LANES), MM_DTYPE), _sds((1, LANES), F32)],
        compiler_params=_params())(dc, pf, fb)


def _head_mask(h):
    lane = lax.broadcasted_iota(jnp.int32, (1, LANES), 1)
    return (lane >= h * HEAD_DIM) & (lane < (h + 1) * HEAD_DIM)


def _attn_fwd(qg, kv, c_col, c_row, Bl, S, D):
    T = Bl * S
    H = D // HEAD_DIM
    HP = D // LANES
    bq = min(S, ATT_BLOCK)
    nq = S // bq
    scale = 1.0 / math.sqrt(HEAD_DIM)

    def body(q_ref, k_ref, v_ref, cc_ref, cr_ref, o_ref, lse_ref):
        i = pl.program_id(2)
        q2 = q_ref[...]
        qh = [q2 * (_head_mask(h).astype(F32) * scale).astype(q2.dtype) for h in range(2)]
        cc = [cc_ref[h][:, :1] for h in range(2)]
        diag = lax.broadcasted_iota(jnp.int32, (1, bq), 1) <= lax.broadcasted_iota(jnp.int32, (bq, 1), 0)

        def block(j, carry, on_diagonal):
            off = pl.multiple_of(j * bq, bq)
            kj = k_ref[pl.ds(off, bq), :]
            vj = v_ref[pl.ds(off, bq), :]
            new = []
            for h in range(2):
                m, l, acc = carry[h]
                s = lax.dot_general(qh[h], kj, NT, preferred_element_type=F32) + cc[h] - cr_ref[h, j]
                if on_diagonal:
                    s = jnp.where(diag, s, -jnp.inf)
                m_new = jnp.maximum(m, jnp.max(s, axis=1, keepdims=True))
                p = jnp.exp(s - m_new)
                a = jnp.exp(m - m_new)
                l = a * l + jnp.sum(p, axis=1, keepdims=True)
                acc = a * acc + jnp.dot(p.astype(MM_DTYPE), vj, preferred_element_type=F32)
                new.append((m_new, l, acc))
            return tuple(new)

        one = (jnp.full((bq, 1), -jnp.inf, F32), jnp.zeros((bq, 1), F32), jnp.zeros((bq, LANES), F32))
        carry = lax.fori_loop(0, i, lambda j, c: block(j, c, False), (one, one))
        carry = block(i, carry, True)
        outs = []
        for h in range(2):
            m, l, acc = carry[h]
            outs.append(acc / l)
            lse_ref[h] = jnp.broadcast_to(m + jnp.log(l), (bq, LANES))
        o_ref[...] = jnp.where(_head_mask(0), outs[0], outs[1])

    return pl.pallas_call(
        body, name="attn_fwd", grid=(Bl, HP, nq),
        in_specs=[pl.BlockSpec((bq, LANES), lambda b, hp, i: (b * nq + i, hp)),
                  pl.BlockSpec((S, LANES), lambda b, hp, i: (b, hp)),
                  pl.BlockSpec((S, LANES), lambda b, hp, i: (b, HP + hp)),
                  pl.BlockSpec((None, 2, bq, LANES), lambda b, hp, i: (b, hp, i, 0)),
                  pl.BlockSpec((None, 2, nq, 1, bq), lambda b, hp, i: (b, hp, 0, 0, 0))],
        out_specs=[pl.BlockSpec((bq, LANES), lambda b, hp, i: (b * nq + i, hp)),
                   pl.BlockSpec((None, 2, bq, LANES), lambda b, hp, i: (b, hp, i, 0))],
        out_shape=[_sds((T, D), F32), _sds((Bl, H, S, LANES), F32)],
        compiler_params=_params())(qg, kv, kv, c_col, c_row)


def _attn_bwd(qg, kv, do, lse, c_col, c_row, Bl, S, D):
    T = Bl * S
    H = D // HEAD_DIM
    HP = D // LANES
    bq = min(S, ATT_BLOCK)
    nq = S // bq
    scale = 1.0 / math.sqrt(HEAD_DIM)

    def body(q_ref, k_ref, v_ref, do_ref, lse_ref, cc_ref, cr_ref, dq_ref, dk_ref, dv_ref, dcr_ref, p_sc, dp_sc):
        i = pl.program_id(2)

        @pl.when(i == 0)
        def _():
            dk_ref[...] = jnp.zeros_like(dk_ref)
            dv_ref[...] = jnp.zeros_like(dv_ref)
            dcr_ref[...] = jnp.zeros_like(dcr_ref)

        q2 = q_ref[...]
        do2 = do_ref[...]
        masks = [_head_mask(h).astype(F32) for h in range(2)]
        qh = [q2 * (masks[h] * scale).astype(q2.dtype) for h in range(2)]
        doh = [do2 * masks[h].astype(do2.dtype) for h in range(2)]
        cc = [cc_ref[h][:, :1] for h in range(2)]
        lse = [lse_ref[h][:, :1] for h in range(2)]
        diag = lax.broadcasted_iota(jnp.int32, (1, bq), 1) <= lax.broadcasted_iota(jnp.int32, (bq, 1), 0)

        def sweep1(j, delta, on_diagonal):
            off = pl.multiple_of(j * bq, bq)
            kj = k_ref[pl.ds(off, bq), :]
            vj = v_ref[pl.ds(off, bq), :]
            new = []
            dv = None
            for h in range(2):
                s = lax.dot_general(qh[h], kj, NT, preferred_element_type=F32) + cc[h] - cr_ref[h, j]
                if on_diagonal:
                    s = jnp.where(diag, s, -jnp.inf)
                p = jnp.exp(s - lse[h])
                dp = lax.dot_general(doh[h], vj, NT, preferred_element_type=F32)
                p_sc[h, j] = p
                dp_sc[h, j] = dp
                part = lax.dot_general(p.astype(MM_DTYPE), doh[h], TN, preferred_element_type=F32)
                dv = part if dv is None else dv + part
                new.append(delta[h] + jnp.sum(p * dp, axis=1, keepdims=True))
            dv_ref[pl.ds(off, bq), :] += dv
            return tuple(new)

        zero = jnp.zeros((bq, 1), F32)
        delta = lax.fori_loop(0, i, lambda j, d: sweep1(j, d, False), (zero, zero))
        delta = sweep1(i, delta, True)

        def sweep2(j, dq):
            off = pl.multiple_of(j * bq, bq)
            kj = k_ref[pl.ds(off, bq), :]
            dk = None
            for h in range(2):
                ds = p_sc[h, j] * (dp_sc[h, j] - delta[h])
                dcr_ref[h, j] -= jnp.sum(ds, axis=0, keepdims=True)
                dsb = ds.astype(MM_DTYPE)
                dq = dq + jnp.dot(dsb, kj * (masks[h] * scale).astype(kj.dtype), preferred_element_type=F32)
                part = lax.dot_general(dsb, qh[h], TN, preferred_element_type=F32)
                dk = part if dk is None else dk + part
            dk_ref[pl.ds(off, bq), :] += dk
            return dq

        dq_ref[...] = lax.fori_loop(0, i + 1, sweep2, jnp.zeros((bq, LANES), F32))

    blk = lambda col: pl.BlockSpec((bq, LANES), lambda b, hp, i: (b * nq + i, col(hp)))
    seq = lambda col: pl.BlockSpec((S, LANES), lambda b, hp, i: (b, col(hp)))
    per_head = pl.BlockSpec((None, 2, bq, LANES), lambda b, hp, i: (b, hp, i, 0))
    rows = pl.BlockSpec((None, 2, nq, 1, bq), lambda b, hp, i: (b, hp, 0, 0, 0))
    return pl.pallas_call(
        body, name="attn_bwd", grid=(Bl, HP, nq),
        in_specs=[blk(lambda hp: hp), seq(lambda hp: hp), seq(lambda hp: HP + hp), blk(lambda hp: hp),
                  per_head, per_head, rows],
        out_specs=[blk(lambda hp: hp), seq(lambda hp: hp), seq(lambda hp: hp), rows],
        out_shape=[_sds((T, D), F32), _sds((T, D), F32), _sds((T, D), F32), _sds((Bl, H, nq, 1, bq), F32)],
        scratch_shapes=[pltpu.VMEM((2, nq, bq, bq), F32), pltpu.VMEM((2, nq, bq, bq), F32)],
        compiler_params=_params())(qg, kv, kv, do, lse, c_col, c_row)


def _gate_fwd(qg, o):
    T, D = o.shape
    tm = _tile(T, ROW_TILE)

    def body(g_ref, o_ref, z_ref):
        z_ref[...] = (jax.nn.sigmoid(g_ref[...].astype(F32)) * o_ref[...]).astype(z_ref.dtype)

    return pl.pallas_call(
        body, name="gate_fwd", grid=(T // tm,),
        in_specs=[pl.BlockSpec((tm, D), lambda i: (i, 1)), _rows(tm, D)], out_specs=_rows(tm, D),
        out_shape=_sds((T, D), MM_DTYPE), compiler_params=_params())(qg, o)


def _gate_do(dz, qg):
    T, D = dz.shape
    tm = _tile(T, ROW_TILE)

    def body(dz_ref, g_ref, do_ref):
        do_ref[...] = (dz_ref[...].astype(F32) * jax.nn.sigmoid(g_ref[...].astype(F32))).astype(do_ref.dtype)

    return pl.pallas_call(
        body, name="gate_do", grid=(T // tm,),
        in_specs=[_rows(tm, D), pl.BlockSpec((tm, D), lambda i: (i, 1))], out_specs=_rows(tm, D),
        out_shape=_sds((T, D), MM_DTYPE), compiler_params=_params())(dz, qg)


def _gate_bwd(dz, qg, o, dq):
    T, D = dz.shape
    tm = _tile(T, ROW_TILE)

    def body(dz_ref, g_ref, o_ref, dq_ref, out_ref):
        g = g_ref[...].astype(F32)
        sg = jax.nn.sigmoid(g)
        out_ref[:, :D] = dq_ref[...].astype(out_ref.dtype)
        out_ref[:, D:] = (dz_ref[...].astype(F32) * o_ref[...] * sg * (1.0 - sg)).astype(out_ref.dtype)

    return pl.pallas_call(
        body, name="gate_bwd", grid=(T // tm,),
        in_specs=[_rows(tm, D), pl.BlockSpec((tm, D), lambda i: (i, 1)), _rows(tm, D), _rows(tm, D)],
        out_specs=_rows(tm, 2 * D), out_shape=_sds((T, 2 * D), MM_DTYPE),
        compiler_params=_params())(dz, qg, o, dq)


def _mm_in(a, wg, out_dtype, tag, l=None):
    T, K = a.shape
    n = wg.shape[-1]
    tm = _tile(T, MM_TILE)
    if l is None:
        w_spec = _whole((N_CHIP, K, n))
    else:
        w_spec = pl.BlockSpec((None, N_CHIP, K, n), lambda i: (l, 0, 0, 0))

    def body(a_ref, w_ref, o_ref):
        av = a_ref[...]
        for s in range(N_CHIP):
            o_ref[:, s * n:(s + 1) * n] = jnp.dot(av, w_ref[s], preferred_element_type=F32).astype(o_ref.dtype)

    return pl.pallas_call(
        body, name=f"mm_in_{tag}", grid=(T // tm,),
        in_specs=[_rows(tm, K), w_spec], out_specs=_rows(tm, N_CHIP * n),
        out_shape=_sds((T, N_CHIP * n), out_dtype), compiler_params=_params())(a, wg)


def _mm_nt_in(dy, wg, tag, l=None):
    T = dy.shape[0]
    K, n = wg.shape[-2:]
    tm = _tile(T, MM_TILE)
    if l is None:
        w_spec = _whole((N_CHIP, K, n))
    else:
        w_spec = pl.BlockSpec((None, N_CHIP, K, n), lambda i: (l, 0, 0, 0))

    def body(d_ref, w_ref, o_ref):
        acc = None
        for s in range(N_CHIP):
            part = lax.dot_general(d_ref[:, s * n:(s + 1) * n], w_ref[s], NT, preferred_element_type=F32)
            acc = part if acc is None else acc + part
        o_ref[...] = acc

    return pl.pallas_call(
        body, name=f"mm_nt_in_{tag}", grid=(T // tm,),
        in_specs=[_rows(tm, N_CHIP * n), w_spec], out_specs=_rows(tm, K),
        out_shape=_sds((T, K), F32), compiler_params=_params())(dy, wg)


def _mm_nn(a, b, out_dtype, tag):
    T, K = a.shape
    N = b.shape[1]
    tm = _tile(T, MM_TILE)

    def body(a_ref, b_ref, o_ref):
        o_ref[...] = jnp.dot(a_ref[...], b_ref[...], preferred_element_type=F32).astype(o_ref.dtype)

    return pl.pallas_call(
        body, name=f"mm_nn_{tag}", grid=(T // tm,),
        in_specs=[_rows(tm, K), _whole((K, N))], out_specs=_rows(tm, N),
        out_shape=_sds((T, N), out_dtype), compiler_params=_params())(a, b)


def _mm_nt(a, b, out_dtype, tag):
    T, C = a.shape
    N = b.shape[0]
    tm = _tile(T, MM_TILE)
    nb = N
    for cand in (1408, 1024):
        if N > cand and N % cand == 0:
            nb = cand
            break

    def body(a_ref, b_ref, o_ref):
        o_ref[...] = lax.dot_general(a_ref[...], b_ref[...], NT, preferred_element_type=F32).astype(o_ref.dtype)

    return pl.pallas_call(
        body, name=f"mm_nt_{tag}", grid=(N // nb, T // tm),
        in_specs=[pl.BlockSpec((tm, C), lambda j, i: (i, 0)), pl.BlockSpec((nb, C), lambda j, i: (j, 0))],
        out_specs=pl.BlockSpec((tm, nb), lambda j, i: (i, j)),
        out_shape=_sds((T, N), out_dtype), compiler_params=_params())(a, b)


def _mm_tn_in(a, dy, tag, l=None, prev=None):
    T, K = a.shape
    n = dy.shape[1] // N_CHIP
    tt = _tile(T, MM_TILE)

    def body(a_ref, d_ref, *rest):
        o_ref = rest[-1]
        part = lax.dot_general(a_ref[...], d_ref[...], TN, preferred_element_type=F32)
        _accumulate(o_ref, part, pl.program_id(1) == 0)

    in_specs = [pl.BlockSpec((tt, K), lambda s, t: (t, 0)), pl.BlockSpec((tt, n), lambda s, t: (t, s))]
    args = [a, dy]
    kw = {}
    if l is None:
        out_spec = pl.BlockSpec((None, K, n), lambda s, t: (s, 0, 0))
        out_shape = _sds((N_CHIP, K, n), F32)
    else:
        out_spec = pl.BlockSpec((None, None, K, n), lambda s, t: (l, s, 0, 0))
        out_shape = _sds((2, N_CHIP, K, n), F32)
        if prev is not None:
            in_specs.append(ANY)
            args.append(prev)
            kw["input_output_aliases"] = {2: 0}
    return pl.pallas_call(
        body, name=f"mm_tn_in_{tag}", grid=(N_CHIP, T // tt),
        in_specs=in_specs, out_specs=out_spec, out_shape=out_shape,
        compiler_params=_params(), **kw)(*args)


def _mm_tn_out(act, dh, tag, l=None, prev=None):
    T, R4 = act.shape
    D = dh.shape[1]
    r = R4 // N_CHIP
    g = 1 if r % LANES == 0 else 2
    tt = _tile(T, MM_TILE)

    def body(a_ref, d_ref, *rest):
        o_ref = rest[-1]
        part = lax.dot_general(a_ref[...], d_ref[...], TN, preferred_element_type=F32)
        first = pl.program_id(1) == 0
        for q in range(g):
            _accumulate(o_ref.at[q], part[q * r:(q + 1) * r], first)

    in_specs = [pl.BlockSpec((tt, g * r), lambda s, t: (t, s)), pl.BlockSpec((tt, D), lambda s, t: (t, 0))]
    args = [act, dh]
    kw = {}
    if l is None:
        out_spec = pl.BlockSpec((g, r, D), lambda s, t: (s, 0, 0))
        out_shape = _sds((N_CHIP, r, D), F32)
    else:
        out_spec = pl.BlockSpec((None, g, r, D), lambda s, t: (l, s, 0, 0))
        out_shape = _sds((2, N_CHIP, r, D), F32)
        if prev is not None:
            in_specs.append(ANY)
            args.append(prev)
            kw["input_output_aliases"] = {2: 0}
    return pl.pallas_call(
        body, name=f"mm_tn_out_{tag}", grid=(N_CHIP // g, T // tt),
        in_specs=in_specs, out_specs=out_spec, out_shape=out_shape,
        compiler_params=_params(), **kw)(*args)


def _mm_tn(a, b, tag):
    T, K = a.shape
    N = b.shape[1]
    tt = _tile(T, MM_TILE)

    def body(a_ref, b_ref, o_ref):
        part = lax.dot_general(a_ref[...], b_ref[...], TN, preferred_element_type=F32)
        _accumulate(o_ref, part, pl.program_id(0) == 0)

    return pl.pallas_call(
        body, name=f"mm_tn_{tag}", grid=(T // tt,),
        in_specs=[_rows(tt, K), _rows(tt, N)], out_specs=_whole((K, N)),
        out_shape=_sds((K, N), F32), compiler_params=_params())(a, b)


def _norm_mm_in(x, g, wg, tag, swiglu=False):
    T, D = x.shape
    n = wg.shape[-1]
    tm = _tile(T, ROW_TILE)
    half = N_CHIP // 2

    def body(x_ref, g_ref, w_ref, xn_ref, y_ref, *rest):
        xv = x_ref[...]
        r = lax.rsqrt(jnp.mean(xv * xv, axis=-1, keepdims=True) + RMS_EPS)
        xn = (xv * r * g_ref[...]).astype(xn_ref.dtype)
        xn_ref[...] = xn

        def product(s):
            p = jnp.dot(xn, w_ref[s], preferred_element_type=F32)
            y_ref[:, s * n:(s + 1) * n] = p.astype(y_ref.dtype)
            return p

        if swiglu:
            for q in range(half):
                gate, up = product(q), product(half + q)
                rest[0][:, q * n:(q + 1) * n] = (gate * jax.nn.sigmoid(gate) * up).astype(rest[0].dtype)
        else:
            for s in range(N_CHIP):
                product(s)

    out_specs = [_rows(tm, D), _rows(tm, N_CHIP * n)]
    out_shape = [_sds((T, D), MM_DTYPE), _sds((T, N_CHIP * n), MM_DTYPE)]
    if swiglu:
        out_specs.append(_rows(tm, half * n))
        out_shape.append(_sds((T, half * n), MM_DTYPE))
    return pl.pallas_call(
        body, name=f"norm_mm_in_{tag}", grid=(T // tm,),
        in_specs=[_rows(tm, D), _whole((1, D)), _whole((N_CHIP, D, n))], out_specs=out_specs,
        out_shape=out_shape, compiler_params=_params())(x, g.reshape(1, D), wg)


def _mm_out_post(a, b, x, g, alpha, tag):
    T, K = a.shape
    D = b.shape[1]
    tm = _tile(T, ROW_TILE)

    def body(a_ref, b_ref, x_ref, g_ref, h_ref, o_ref):
        hv = jnp.dot(a_ref[...], b_ref[...], preferred_element_type=F32)
        h_ref[...] = hv
        r = lax.rsqrt(jnp.mean(hv * hv, axis=-1, keepdims=True) + RMS_EPS)
        o_ref[...] = x_ref[...] + alpha * (hv * r * g_ref[...])

    return pl.pallas_call(
        body, name=f"mm_out_post_{tag}", grid=(T // tm,),
        in_specs=[_rows(tm, K), _whole((K, D)), _rows(tm, D), _whole((1, D))],
        out_specs=[_rows(tm, D), _rows(tm, D)], out_shape=[_sds((T, D), F32)] * 2,
        compiler_params=_params())(a, b, x, g.reshape(1, D))


def _post_bwd_mm(dx, h, g, alpha, b, tag, hgu=None):
    T, D = dx.shape
    K = b.shape[0]
    tm = _tile(T, ROW_TILE)

    def body(dx_ref, h_ref, g_ref, b_ref, *rest):
        dh_ref, dg_ref, out_ref = rest[-3:]
        hv = h_ref[...]
        r = lax.rsqrt(jnp.mean(hv * hv, axis=-1, keepdims=True) + RMS_EPS)
        hh = hv * r
        dyn = alpha * dx_ref[...]
        _accumulate(dg_ref, jnp.sum(dyn * hh, axis=0, keepdims=True), pl.program_id(0) == 0)
        dhh = dyn * g_ref[...]
        dh = (r * (dhh - hh * jnp.mean(dhh * hh, axis=-1, keepdims=True))).astype(dh_ref.dtype)
        dh_ref[...] = dh
        da = lax.dot_general(dh, b_ref[...], NT, preferred_element_type=F32)
        if hgu is None:
            out_ref[...] = da.astype(out_ref.dtype)
        else:
            gate = rest[0][:, :K].astype(F32)
            up = rest[0][:, K:].astype(F32)
            sg = jax.nn.sigmoid(gate)
            out_ref[:, :K] = (da * up * sg * (1.0 + gate * (1.0 - sg))).astype(out_ref.dtype)
            out_ref[:, K:] = (da * gate * sg).astype(out_ref.dtype)

    in_specs = [_rows(tm, D), _rows(tm, D), _whole((1, D)), _whole((K, D))]
    args = [dx, h, g.reshape(1, D), b]
    wide = K
    if hgu is not None:
        wide = 2 * K
        in_specs.append(_rows(tm, wide))
        args.append(hgu)
    return pl.pallas_call(
        body, name=f"post_bwd_mm_{tag}", grid=(T // tm,), in_specs=in_specs,
        out_specs=[_rows(tm, D), _whole((1, D)), _rows(tm, wide)],
        out_shape=[_sds((T, D), MM_DTYPE), _sds((1, D), F32), _sds((T, wide), MM_DTYPE)],
        compiler_params=_params())(*args)


def _mm_nt_pre(dy, w, dres, x, g, tag):
    T, C = dy.shape
    D = x.shape[1]
    tm = _tile(T, ROW_TILE)
    n = w.shape[-1]

    def body(dy_ref, w_ref, dres_ref, x_ref, g_ref, dx_ref, dg_ref):
        if w.ndim == 2:
            dn = lax.dot_general(dy_ref[...], w_ref[...], NT, preferred_element_type=F32)
        else:
            dn = None
            for s in range(N_CHIP):
                part = lax.dot_general(dy_ref[:, s * n:(s + 1) * n], w_ref[s], NT, preferred_element_type=F32)
                dn = part if dn is None else dn + part
        xv = x_ref[...]
        r = lax.rsqrt(jnp.mean(xv * xv, axis=-1, keepdims=True) + RMS_EPS)
        xh = xv * r
        _accumulate(dg_ref, jnp.sum(dn * xh, axis=0, keepdims=True), pl.program_id(0) == 0)
        dxh = dn * g_ref[...]
        dx_ref[...] = dres_ref[...] + r * (dxh - xh * jnp.mean(dxh * xh, axis=-1, keepdims=True))

    return pl.pallas_call(
        body, name=f"mm_nt_pre_{tag}", grid=(T // tm,),
        in_specs=[_rows(tm, C), _whole(w.shape), _rows(tm, D), _rows(tm, D), _whole((1, D))],
        out_specs=[_rows(tm, D), _whole((1, D))], out_shape=[_sds((T, D), F32), _sds((1, D), F32)],
        compiler_params=_params())(dy, w, dres, x, g.reshape(1, D))


def _adamw(w, g, m, v, tag):
    R, C = w.shape
    tr = _tile(R, ROW_TILE)

    def body(w_ref, g_ref, m_ref, v_ref, d_ref, mo_ref, vo_ref):
        gv = g_ref[...]
        mn = ADAM_B1 * m_ref[...] + (1.0 - ADAM_B1) * gv
        vn = ADAM_B2 * v_ref[...] + (1.0 - ADAM_B2) * (gv * gv)
        m_hat = mn / (1.0 - ADAM_B1 ** ADAM_STEP)
        v_hat = vn / (1.0 - ADAM_B2 ** ADAM_STEP)
        d_ref[...] = -ADAM_LR * (m_hat / (jnp.sqrt(v_hat) + ADAM_EPS) + ADAM_WD * w_ref[...])
        mo_ref[...] = mn
        vo_ref[...] = vn

    return pl.pallas_call(
        body, name=f"adamw_{tag}", grid=(R // tr,),
        in_specs=[_rows(tr, C)] * 4, out_specs=[_rows(tr, C)] * 3,
        out_shape=[_sds((R, C), F32)] * 3, compiler_params=_params())(w, g, m, v)


def _sum_devices(gall):
    _, R, C = gall.shape

    def body(g_ref, o_ref):
        acc = g_ref[0]
        for d in range(1, N_DEV):
            acc = acc + g_ref[d]
        o_ref[...] = acc

    return pl.pallas_call(
        body, name="sum_devices", in_specs=[_whole((N_DEV, R, C))], out_specs=_whole((R, C)),
        out_shape=_sds((R, C), F32), grid=(1,), compiler_params=_params())(gall)


HBM = pl.BlockSpec(memory_space=pltpu.HBM)
SEM = pl.BlockSpec(memory_space=pltpu.SEMAPHORE)
EFFECT = pltpu.SideEffectType.DATAFLOW_SIDE_EFFECTING


def _place():
    x, y, c = lax.axis_index("x"), lax.axis_index("y"), lax.axis_index("c")
    chips = ((1 - x, y), (x, 1 - y), (1 - x, 1 - y))
    return x, y, c, chips


def _remote(src, dst, send_sem, recv_sem, dev):
    return pltpu.make_async_remote_copy(src_ref=src, dst_ref=dst, send_sem=send_sem, recv_sem=recv_sem,
                                        device_id=dev, device_id_type=MESH)


def _in_hbm(a):
    return pltpu.with_memory_space_constraint(a, pltpu.HBM)


def _own_slot(w4, l, dtype, place, tag):
    _, _, r, col = w4.shape
    tr = _tile(r, 2 * ROW_TILE)

    def body(place_ref, x_ref, o_ref):
        o_ref[...] = x_ref[...].astype(o_ref.dtype)

    grid_spec = pltpu.PrefetchScalarGridSpec(
        num_scalar_prefetch=1, grid=(2, r // tr),
        in_specs=[pl.BlockSpec((None, None, tr, col), lambda h, i, p: (l, h, i, 0))],
        out_specs=pl.BlockSpec((None, None, tr, col), lambda h, i, p: (p[1], h, i, 0)))
    return pl.pallas_call(
        body, name=f"own_slot_{tag}", grid_spec=grid_spec, out_shape=_sds((N_CHIP, 2, r, col), dtype),
        compiler_params=_params())(place, w4)


def _allgather(bufs):
    n = len(bufs)

    def body(*refs):
        outs, token = refs[n:2 * n], refs[2 * n]
        s_ici, r_ici, s_d2d, r_d2d = refs[2 * n + 1:]
        x, y, c, chips = _place()
        me = 2 * x + y
        sib = (x, y, 1 - c)
        sends = []
        for i in range(n):
            mine = outs[i].at[me, c]
            for j, (px, py) in enumerate(chips):
                cp = _remote(mine, mine, s_ici.at[i, j], r_ici.at[i, j], (px, py, c))
                cp.start()
                sends.append(cp)
        for j, (px, py) in enumerate(chips):
            src = 2 * px + py
            for i in range(n):
                blk = outs[i].at[src, c]
                _remote(blk, blk, s_ici.at[i, j], r_ici.at[i, j], (px, py, c)).wait_recv()
                cp = _remote(blk, blk, s_d2d.at[i, j], r_d2d.at[i, j], sib)
                cp.start()
                sends.append(cp)
        for j, (px, py) in enumerate(chips):
            src = 2 * px + py
            for i in range(n):
                blk = outs[i].at[src, 1 - c]
                _remote(blk, blk, s_d2d.at[i, j], r_d2d.at[i, j], sib).wait_recv()
        for cp in sends:
            cp.wait_send()
        token[...] = jnp.zeros_like(token)

    dma = pltpu.SemaphoreType.DMA
    res = pl.pallas_call(
        body, name="allgather_weights", in_specs=[ANY] * n,
        out_specs=[ANY] * n + [pl.BlockSpec(memory_space=pltpu.VMEM)],
        out_shape=[_sds(b.shape, b.dtype) for b in bufs] + [_sds((8, LANES), F32)],
        input_output_aliases={i: i for i in range(n)},
        scratch_shapes=[dma((n, 3)), dma((n, 3)), dma((n, 3)), dma((n, 3))],
        )(*bufs)
    return res[:n], res[n]


def _gather_start(bufs, after):
    n = len(bufs)

    def body(*refs):
        ins = refs[:n]
        s_sem, r_sem, token = refs[n + 1], refs[n + 2], refs[2 * n + 3]
        x, y, c, chips = _place()
        me = 2 * x + y
        for i in range(n):
            mine = ins[i].at[me, c]
            for j, (px, py) in enumerate(chips):
                for tc in range(2):
                    _remote(mine, mine, s_sem.at[6 * i + 2 * j + tc], r_sem.at[6 * i + 2 * j + c], (px, py, tc)).start()
        token[...] = jnp.zeros_like(token)

    dma = pltpu.SemaphoreType.DMA
    res = pl.pallas_call(
        body, name="gather_start", in_specs=[HBM] * n + [ANY],
        out_specs=[SEM, SEM] + [HBM] * n + [pl.BlockSpec(memory_space=pltpu.VMEM)],
        out_shape=[dma((6 * n,)), dma((6 * n,))] + [pltpu.HBM(b.shape, b.dtype) for b in bufs] + [_sds((8, LANES), F32)],
        input_output_aliases={i: i + 2 for i in range(n)},
        compiler_params=pltpu.CompilerParams(has_side_effects=EFFECT),
        )(*[_in_hbm(b) for b in bufs], after)
    return res[0], res[1], res[2:2 + n], res[-1]


def _gather_wait(s_sem, r_sem, bufs, after):
    n = len(bufs)

    def body(*refs):
        ins = refs[:n]
        s_ref, r_ref = refs[n], refs[n + 1]
        x, y, c, chips = _place()
        me = 2 * x + y
        for i in range(n):
            mine = ins[i].at[me, c]
            for j, (px, py) in enumerate(chips):
                for tc in range(2):
                    _remote(mine, mine, s_ref.at[6 * i + 2 * j + tc], r_ref.at[6 * i + 2 * j + c], (px, py, tc)).wait_send()
            for j, (px, py) in enumerate(chips):
                for cs in range(2):
                    blk = ins[i].at[2 * px + py, cs]
                    k = 6 * i + 2 * j + cs
                    _remote(blk, blk, s_ref.at[k], r_ref.at[k], (px, py, cs)).wait_recv()

    return pl.pallas_call(
        body, name="gather_wait", in_specs=[HBM] * n + [SEM, SEM, ANY], out_specs=[HBM] * n,
        out_shape=[pltpu.HBM(b.shape, b.dtype) for b in bufs],
        input_output_aliases={i: i for i in range(n)},
        compiler_params=pltpu.CompilerParams(has_side_effects=EFFECT),
        )(*bufs, s_sem, r_sem, after)


def _rs_pair_send(grads):
    n = len(grads)

    def body(*refs):
        ins, outs = refs[:n], refs[n:2 * n]
        s_sem, r_sem = refs[2 * n:]
        x, y, c, _ = _place()
        sib = (x, y, 1 - c)
        sends = []
        for i in range(n):
            cp = _remote(ins[i].at[:, 1 - c], outs[i], s_sem.at[i], r_sem.at[i], sib)
            cp.start()
            sends.append(cp)
        for cp in sends:
            cp.wait()

    out_shape = [_sds((N_CHIP,) + g.shape[2:], g.dtype) for g in grads]
    dma = pltpu.SemaphoreType.DMA
    return pl.pallas_call(
        body, name=f"rs_pair_send_{n}", in_specs=[ANY] * n, out_specs=[ANY] * n, out_shape=out_shape,
        scratch_shapes=[dma((n,)), dma((n,))],
        )(*grads)


def _rs_pair_add(g, recv, place, tag):
    r, col = g.shape[-2:]
    tr = _tile(r, ROW_TILE)

    def body(place_ref, g_ref, r_ref, wire_ref, own_ref):
        tot = g_ref[...] + r_ref[...]
        wire_ref[...] = tot.astype(wire_ref.dtype)

        @pl.when(pl.program_id(1) == place_ref[1])
        def _():
            own_ref[...] = tot

    grid_spec = pltpu.PrefetchScalarGridSpec(
        num_scalar_prefetch=1, grid=(r // tr, N_CHIP),
        in_specs=[pl.BlockSpec((None, None, tr, col), lambda i, s, p: (s, p[0], i, 0)),
                  pl.BlockSpec((None, tr, col), lambda i, s, p: (s, i, 0))],
        out_specs=[pl.BlockSpec((None, tr, col), lambda i, s, p: (s, i, 0)),
                   pl.BlockSpec((tr, col), lambda i, s, p: (i, 0))])
    return pl.pallas_call(
        body, name=f"rs_pair_add_{tag}", grid_spec=grid_spec,
        out_shape=[_sds((N_CHIP, r, col), WIRE_DTYPE), _sds((r, col), F32)],
        compiler_params=_params())(place, g, recv)


def _rs_start(wires):
    n = len(wires)

    def body(*refs):
        ins = refs[:n]
        s_sem, r_sem, token = refs[2 * n], refs[2 * n + 1], refs[4 * n + 2]
        x, y, c, chips = _place()
        for i in range(n):
            land = refs[n + i]
            for j, (px, py) in enumerate(chips):
                _remote(ins[i].at[2 * px + py], land.at[j], s_sem.at[3 * i + j], r_sem.at[3 * i + j], (px, py, c)).start()
        token[...] = jnp.zeros_like(token)

    lands = [lax.empty((3,) + w.shape[1:], w.dtype) for w in wires]
    both = list(wires) + lands
    dma = pltpu.SemaphoreType.DMA
    res = pl.pallas_call(
        body, name="rs_start", in_specs=[HBM] * (2 * n),
        out_specs=[SEM, SEM] + [HBM] * (2 * n) + [pl.BlockSpec(memory_space=pltpu.VMEM)],
        out_shape=[dma((3 * n,)), dma((3 * n,))] + [pltpu.HBM(b.shape, b.dtype) for b in both] + [_sds((8, LANES), F32)],
        input_output_aliases={i: i + 2 for i in range(2 * n)},
        compiler_params=pltpu.CompilerParams(has_side_effects=EFFECT),
        )(*[_in_hbm(b) for b in both])
    return res[0], res[1], res[2:2 + n], res[2 + n:2 + 2 * n], res[-1]


def _rs_wait(s_sem, r_sem, wires, lands, after):
    n = len(wires)

    def body(*refs):
        ins = refs[:n]
        s_ref, r_ref = refs[2 * n], refs[2 * n + 1]
        x, y, c, chips = _place()
        for i in range(n):
            land = refs[n + i]
            for j, (px, py) in enumerate(chips):
                cp = _remote(ins[i].at[2 * px + py], land.at[j], s_ref.at[3 * i + j], r_ref.at[3 * i + j], (px, py, c))
                cp.wait_send()
                cp.wait_recv()

    both = list(wires) + list(lands)
    res = pl.pallas_call(
        body, name="rs_wait", in_specs=[HBM] * (2 * n) + [SEM, SEM, ANY], out_specs=[HBM] * (2 * n),
        out_shape=[pltpu.HBM(b.shape, b.dtype) for b in both],
        input_output_aliases={i: i for i in range(2 * n)},
        compiler_params=pltpu.CompilerParams(has_side_effects=EFFECT),
        )(*both, s_sem, r_sem, after)
    return res[n:]


def _rs_chip_send(wires, small):
    n = len(wires)

    def body(*refs):
        ins, small_ref = refs[:n], refs[n]
        outs, gall_ref = refs[n + 1:2 * n + 1], refs[2 * n + 1]
        s_sem, r_sem, s_small, r_small, s_loc = refs[2 * n + 2:]
        x, y, c, chips = _place()
        me = 4 * x + 2 * y + c
        sends = []
        loc = pltpu.make_async_copy(small_ref, gall_ref.at[me], s_loc)
        loc.start()
        for i in range(n):
            for j, (px, py) in enumerate(chips):
                cp = _remote(ins[i].at[2 * px + py], outs[i].at[j], s_sem.at[i, j], r_sem.at[i, j], (px, py, c))
                cp.start()
                sends.append(cp)
        for k in range(1, N_DEV):
            peer = (x ^ (k >> 2), y ^ ((k >> 1) & 1), c ^ (k & 1))
            cp = _remote(small_ref, gall_ref.at[me], s_small.at[k - 1], r_small.at[k - 1], peer)
            cp.start()
            sends.append(cp)
        for cp in sends:
            cp.wait()
        loc.wait()

    out_shape = [_sds((3,) + w.shape[1:], w.dtype) for w in wires] + [_sds((N_DEV,) + small.shape, small.dtype)]
    dma = pltpu.SemaphoreType.DMA
    return pl.pallas_call(
        body, name="rs_chip_send", in_specs=[ANY] * (n + 1), out_specs=[ANY] * (n + 1), out_shape=out_shape,
        scratch_shapes=[dma((n, 3)), dma((n, 3)), dma((N_DEV - 1,)), dma((N_DEV - 1,)), dma(())],
        )(*wires, small)


def _rs_chip_add(own, recv, place, l, L, prev, tag):
    r, col = own.shape
    tr = _tile(r, ROW_TILE)

    def body(place_ref, o_ref, r_ref, *rest):
        acc = o_ref[...]
        for j in range(3):
            acc = acc + r_ref[j].astype(F32)
        rest[-1][...] = acc

    in_specs = [pl.BlockSpec((tr, col), lambda i, p: (i, 0)), pl.BlockSpec((3, tr, col), lambda i, p: (0, i, 0))]
    args = [place, own, recv]
    kw = {}
    if prev is not None:
        in_specs.append(ANY)
        args.append(prev)
        kw["input_output_aliases"] = {3: 0}
    grid_spec = pltpu.PrefetchScalarGridSpec(
        num_scalar_prefetch=1, grid=(r // tr,), in_specs=in_specs,
        out_specs=pl.BlockSpec((None, None, tr, col), lambda i, p: (l, p[0], i, 0)))
    return pl.pallas_call(
        body, name=f"rs_chip_add_{tag}", grid_spec=grid_spec, out_shape=_sds((L, 2, r, col), F32),
        compiler_params=_params(), **kw)(*args)


def _rs_pair_share(fulls):
    n = len(fulls)

    def body(*refs):
        outs = refs[n:2 * n]
        s_sem, r_sem = refs[2 * n:]
        x, y, c, _ = _place()
        sib = (x, y, 1 - c)
        started = []
        for i in range(n):
            cp = _remote(outs[i].at[:, c], outs[i].at[:, c], s_sem.at[i], r_sem.at[i], sib)
            cp.start()
            started.append(cp)
        for i, cp in enumerate(started):
            cp.wait_send()
            _remote(outs[i].at[:, 1 - c], outs[i].at[:, 1 - c], s_sem.at[i], r_sem.at[i], sib).wait_recv()

    dma = pltpu.SemaphoreType.DMA
    return pl.pallas_call(
        body, name="rs_pair_share", in_specs=[ANY] * n, out_specs=[ANY] * n,
        out_shape=[_sds(f.shape, f.dtype) for f in fulls],
        input_output_aliases={i: i for i in range(n)},
        scratch_shapes=[dma((n,)), dma((n,))],
        )(*fulls)


def _ffn_fwd(x, g_pre, g_post, w_in, w_out, tag):
    xn, hgu, act = _norm_mm_in(x, g_pre, w_in, tag, swiglu=True)
    h, x_out = _mm_out_post(act, w_out.reshape(-1, w_out.shape[-1]), x, g_post, 0.5, tag)
    return x_out, (x, xn, hgu, act, h)


def _ffn_bwd(dx, saved, g_pre, g_post, w_in, w_out, tag):
    x, xn, hgu, act, h = saved
    dh, dg_post, dhgu = _post_bwd_mm(dx, h, g_post, 0.5, w_out.reshape(-1, w_out.shape[-1]), tag, hgu=hgu)
    dw_out = _mm_tn_out(act, dh, tag)
    dw_in = _mm_tn_in(xn, dhgu, tag)
    dx_in, dg_pre = _mm_nt_pre(dhgu, w_in, dx, x, g_pre, tag)
    return dx_in, dg_pre, dg_post, dw_in, dw_out


def kernel(x, ffn1_pre_g, ffn1_post_g, ffn1_w_in, ffn1_w_out, mix_pre_g, mix_post_g, ffn2_pre_g, ffn2_post_g, ffn2_w_in, ffn2_w_out, conv_w_in, conv_k, conv_w_out, kv_g, kv_w, forget_b, attn_w_qg, attn_w_o, loss_target, m_ffn1_pre_g, m_ffn1_post_g, m_ffn1_w_in, m_ffn1_w_out, m_mix_pre_g, m_mix_post_g, m_ffn2_pre_g, m_ffn2_post_g, m_ffn2_w_in, m_ffn2_w_out, m_conv_w_in, m_conv_k, m_conv_w_out, m_kv_g, m_kv_w, m_forget_b, m_attn_w_qg, m_attn_w_o, v_ffn1_pre_g, v_ffn1_post_g, v_ffn1_w_in, v_ffn1_w_out, v_mix_pre_g, v_mix_post_g, v_ffn2_pre_g, v_ffn2_post_g, v_ffn2_w_in, v_ffn2_w_out, v_conv_w_in, v_conv_k, v_conv_w_out, v_kv_g, v_kv_w, v_forget_b, v_attn_w_qg, v_attn_w_o):
    Bl, S, D = x.shape
    T = Bl * S
    H = forget_b.shape[0]
    assert D == H * HEAD_DIM and D % LANES == 0
    kvc = kv_w.shape[1]
    kvp = -(-kvc // LANES) * LANES
    kv_all = 2 * D + LANES
    dk_cols = conv_k.shape[2]
    chip = 2 * lax.axis_index("x") + lax.axis_index("y")
    core = lax.axis_index("c")

    given = dict(ffn1_w_in=(ffn1_w_in, m_ffn1_w_in, v_ffn1_w_in), ffn1_w_out=(ffn1_w_out, m_ffn1_w_out, v_ffn1_w_out),
                 ffn2_w_in=(ffn2_w_in, m_ffn2_w_in, v_ffn2_w_in), ffn2_w_out=(ffn2_w_out, m_ffn2_w_out, v_ffn2_w_out),
                 conv_w_in=(conv_w_in, m_conv_w_in, v_conv_w_in), conv_w_out=(conv_w_out, m_conv_w_out, v_conv_w_out),
                 kv_w=(kv_w, m_kv_w, v_kv_w), attn_w_qg=(attn_w_qg, m_attn_w_qg, v_attn_w_qg),
                 attn_w_o=(attn_w_o, m_attn_w_o, v_attn_w_o))
    shards = {k: w for k, (w, _, _) in given.items()}
    shards["kv_w"] = jnp.pad(kv_w, ((0, 0), (0, kvp - kvc)))[None]
    first = [("ffn1_w_in", 0), ("ffn1_w_out", 0), ("conv_w_in", 0), ("conv_w_out", 0), ("ffn2_w_in", 0), ("ffn2_w_out", 0)]
    second = [("kv_w", 0), ("ffn1_w_in", 1), ("ffn1_w_out", 1), ("attn_w_qg", 0), ("attn_w_o", 0),
              ("ffn2_w_in", 1), ("ffn2_w_out", 1)]
    place = jnp.stack([core, chip]).astype(jnp.int32)

    def slot(key, where):
        w = shards[key[0]]
        L, r, col = w.shape
        return _own_slot(w.reshape(L, 2, r // 2, col), key[1], MM_DTYPE, where, f"{key[0]}{key[1]}")

    def whole(g):
        return g.reshape(N_CHIP, -1, g.shape[-1])

    taps_slot = _own_slot(jnp.pad(conv_k[0], ((0, 13), (0, 0))).reshape(1, 2, 8, dk_cols), 0, F32, place, "conv_k")
    got, token = _allgather([slot(key, place) for key in first] + [taps_slot])
    W = {key: whole(g) for key, g in zip(first, got)}
    k_taps = got[-1].reshape(N_CHIP, 16, dk_cols).transpose(1, 0, 2).reshape(16, D)[:8]
    g_sems, g_semr, flying, token = _gather_start([slot(key, place) for key in second], token)
    w_o_conv = W["conv_w_out", 0].reshape(D, D)
    fb = jnp.pad(forget_b, (0, LANES - H)).reshape(1, LANES)

    x0 = x.reshape(T, D)
    x1, s_f1a = _ffn_fwd(x0, ffn1_pre_g[0] + token[0, :1], ffn1_post_g[0], W["ffn1_w_in", 0], W["ffn1_w_out", 0], "l0f1")
    xn_c, bch = _norm_mm_in(x1, mix_pre_g[0], W["conv_w_in", 0], "conv")
    z_c = _conv_fwd(bch, k_taps, Bl, S)
    m_c, x2 = _mm_out_post(z_c, w_o_conv, x1, mix_post_g[0], 1.0, "conv_out")
    x3, s_f2a = _ffn_fwd(x2, ffn2_pre_g[0], ffn2_post_g[0], W["ffn2_w_in", 0], W["ffn2_w_out", 0], "l0f2")

    landed = _gather_wait(g_sems, g_semr, flying, x3)
    W.update({key: whole(g) for key, g in zip(second, landed)})
    kv_full = jnp.concatenate([W["kv_w", 0][s, :, :kvc] for s in range(N_CHIP)], axis=1)
    kv_full = jnp.pad(kv_full, ((0, 0), (0, kv_all - kv_full.shape[1])))
    w_o_attn = W["attn_w_o", 0].reshape(D, D)
    xn_kv = _rms_fwd(x3, kv_g, "kv")
    kvact = _mm_nn(xn_kv, kv_full[:, :2 * D], MM_DTYPE, "kv")
    pf = _mm_nn(xn_kv, kv_full[:, 2 * D:], F32, "forget")
    cum = _forget_fwd(pf, fb, Bl, S)
    bq = min(S, ATT_BLOCK)
    c3 = cum.reshape(Bl, S, LANES)[:, :, :H].transpose(0, 2, 1)
    c_col = jnp.broadcast_to(c3[..., None], (Bl, H, S, LANES))
    c_row = c3.reshape(Bl, H, S // bq, 1, bq)

    x4, s_f1b = _ffn_fwd(x3, ffn1_pre_g[1], ffn1_post_g[1], W["ffn1_w_in", 1], W["ffn1_w_out", 1], "l1f1")
    xn_a, qg = _norm_mm_in(x4, mix_pre_g[1], W["attn_w_qg", 0], "qg")
    o, lse = _attn_fwd(qg, kvact, c_col, c_row, Bl, S, D)
    z_a = _gate_fwd(qg, o)
    m_a, x5 = _mm_out_post(z_a, w_o_attn, x4, mix_post_g[1], 1.0, "attn_out")
    x6, s_f2b = _ffn_fwd(x5, ffn2_pre_g[1], ffn2_post_g[1], W["ffn2_w_in", 1], W["ffn2_w_out", 1], "l1f2")

    dy, loss_local = _loss_grad(x6, loss_target.reshape(T, D))
    loss = lax.psum(loss_local, ("x", "y", "c"))

    G = {}
    dx5, dg_f2pre_1, dg_f2post_1, G["ffn2_w_in", 1], G["ffn2_w_out", 1] = _ffn_bwd(
        dy, s_f2b, ffn2_pre_g[1], ffn2_post_g[1], W["ffn2_w_in", 1], W["ffn2_w_out", 1], "l1f2")
    dm_a, dg_mixpost_1, dz_a = _post_bwd_mm(dx5, m_a, mix_post_g[1], 1.0, w_o_attn, "attn_out")
    G["attn_w_o", 0] = _mm_tn_out(z_a, dm_a, "attn_out")
    do = _gate_do(dz_a, qg)
    dq, dk, dv, dcr = _attn_bwd(qg, kvact, do, lse, c_col, c_row, Bl, S, D)
    dqg = _gate_bwd(dz_a, qg, o, dq)
    G["attn_w_qg", 0] = _mm_tn_in(xn_a, dqg, "qg")
    dx4, dg_mixpre_1 = _mm_nt_pre(dqg, W["attn_w_qg", 0], dx5, x4, mix_pre_g[1], "qg")
    dx3, dg_f1pre_1, dg_f1post_1, G["ffn1_w_in", 1], G["ffn1_w_out", 1] = _ffn_bwd(
        dx4, s_f1b, ffn1_pre_g[1], ffn1_post_g[1], W["ffn1_w_in", 1], W["ffn1_w_out", 1], "l1f1")

    dcum = jnp.pad(dcr.reshape(Bl, H, S).transpose(0, 2, 1), ((0, 0), (0, 0), (0, LANES - H))).reshape(T, LANES)
    dpf, dfb = _forget_bwd(dcum, pf, fb, Bl, S)
    dp = jnp.concatenate([dk.astype(MM_DTYPE), dv.astype(MM_DTYPE), dpf], axis=1)
    G_kv_full = _mm_tn(xn_kv, dp, "kv")
    G["kv_w", 0] = jnp.stack([jnp.pad(G_kv_full[:, s * kvc:(s + 1) * kvc], ((0, 0), (0, kvp - kvc))) for s in range(N_CHIP)])
    dx3, dg_kv = _mm_nt_pre(dp, kv_full, dx3, x3, kv_g, "kv")

    def pair_sums(keys):
        grads = [G[k].reshape(N_CHIP, 2, G[k].shape[1] // 2, G[k].shape[2]) for k in keys]
        wires, owns = [], []
        for k, g, r in zip(keys, grads, _rs_pair_send(grads)):
            w, own = _rs_pair_add(g, r, place, f"{k[0]}{k[1]}")
            wires.append(w)
            owns.append(own)
        return wires, owns

    wires_2, owns_2 = pair_sums(second)
    r_sems, r_semr, wires_2, lands_2, token = _rs_start(wires_2)

    dx2, dg_f2pre_0, dg_f2post_0, G["ffn2_w_in", 0], G["ffn2_w_out", 0] = _ffn_bwd(
        dx3, s_f2a, ffn2_pre_g[0], ffn2_post_g[0] + token[0, :1], W["ffn2_w_in", 0], W["ffn2_w_out", 0], "l0f2")
    dm_c, dg_mixpost_0, dz_c = _post_bwd_mm(dx2, m_c, mix_post_g[0], 1.0, w_o_conv, "conv_out")
    G["conv_w_out", 0] = _mm_tn_out(z_c, dm_c, "conv_out")
    db, dcg, dhh, dk_taps = _conv_bwd(bch, dz_c, k_taps, Bl, S)
    dbch = jnp.concatenate([db, dcg, dhh], axis=1)
    G["conv_w_in", 0] = _mm_tn_in(xn_c, dbch, "conv")
    dx1, dg_mixpre_0 = _mm_nt_pre(dbch, W["conv_w_in", 0], dx2, x1, mix_pre_g[0], "conv")
    dx0, dg_f1pre_0, dg_f1post_0, G["ffn1_w_in", 0], G["ffn1_w_out", 0] = _ffn_bwd(
        dx1, s_f1a, ffn1_pre_g[0], ffn1_post_g[0], W["ffn1_w_in", 0], W["ffn1_w_out", 0], "l0f1")
    grad_x = dx0.reshape(Bl, S, D)

    recvs_2 = _rs_wait(r_sems, r_semr, wires_2, lands_2, dx0)
    wires_1, owns_1 = pair_sums(first)

    def row(v):
        return jnp.pad(v.reshape(-1), (0, D - v.size)).reshape(1, D)

    small_parts = [dg_f1pre_0, dg_f1pre_1, dg_f1post_0, dg_f1post_1, dg_mixpre_0, dg_mixpre_1, dg_mixpost_0, dg_mixpost_1,
                   dg_f2pre_0, dg_f2pre_1, dg_f2post_0, dg_f2post_1, dg_kv, row(dfb[0, :H]), dk_taps[:3]]
    small = jnp.concatenate(small_parts, axis=0)
    small = jnp.pad(small, ((0, SMALL_ROWS - small.shape[0]), (0, 0)))
    outs = _rs_chip_send(wires_1, small)
    recvs_1, gall = outs[:-1], outs[-1]
    partial = {}
    for key, own, rcv in zip(first + second, owns_1 + owns_2, list(recvs_1) + list(recvs_2)):
        name, l = key
        partial[name] = _rs_chip_add(own, rcv, place, l, shards[name].shape[0], partial.get(name), f"{name}{l}")
    names = list(partial)
    reduced = _rs_pair_share([partial[k] for k in names])
    gsum = _sum_devices(gall)

    res = {}
    for k, red in zip(names, reduced):
        w, m, v = given[k]
        g2 = red.reshape(-1, red.shape[-1])
        if k == "kv_w":
            g2 = g2[:, :kvc]
        flat = lambda a: a.reshape(-1, a.shape[-1])
        d, mn, vn = _adamw(flat(w), g2, flat(m), flat(v), k)
        res[k] = tuple(a.reshape(w.shape) for a in (g2, d, mn, vn))

    small_names = ["ffn1_pre_g", "ffn1_post_g", "mix_pre_g", "mix_post_g", "ffn2_pre_g", "ffn2_post_g"]
    small_given = dict(ffn1_pre_g=(ffn1_pre_g, m_ffn1_pre_g, v_ffn1_pre_g), ffn1_post_g=(ffn1_post_g, m_ffn1_post_g, v_ffn1_post_g),
                       mix_pre_g=(mix_pre_g, m_mix_pre_g, v_mix_pre_g), mix_post_g=(mix_post_g, m_mix_post_g, v_mix_post_g),
                       ffn2_pre_g=(ffn2_pre_g, m_ffn2_pre_g, v_ffn2_pre_g), ffn2_post_g=(ffn2_post_g, m_ffn2_post_g, v_ffn2_post_g))

    def pack(idx):
        rows_ = [small_given[k][idx] for k in small_names]
        rows_ += [row((kv_g, m_kv_g, v_kv_g)[idx]), row((forget_b, m_forget_b, v_forget_b)[idx])]
        rows_.append(jnp.pad((conv_k, m_conv_k, v_conv_k)[idx][0], ((0, 0), (0, D - dk_cols))))
        a = jnp.concatenate(rows_, axis=0)
        return jnp.pad(a, ((0, SMALL_ROWS - a.shape[0]), (0, 0)))

    g_taps = lax.dynamic_slice_in_dim(gsum[14:17], chip * dk_cols, dk_cols, axis=1)
    g_small = jnp.concatenate([gsum[:14], jnp.pad(g_taps, ((0, 0), (0, D - dk_cols))), gsum[17:]], axis=0)
    d_s, m_s, v_s = _adamw(pack(0), g_small, pack(1), pack(2), "small")
    for i, k in enumerate(small_names):
        res[k] = tuple(a[2 * i:2 * i + 2] for a in (g_small, d_s, m_s, v_s))
    res["kv_g"] = tuple(a[12] for a in (g_small, d_s, m_s, v_s))
    res["forget_b"] = tuple(a[13, :H] for a in (g_small, d_s, m_s, v_s))
    res["conv_k"] = tuple(a[14:17, :dk_cols][None] for a in (g_small, d_s, m_s, v_s))

    order = ["ffn1_pre_g", "ffn1_post_g", "ffn1_w_in", "ffn1_w_out", "mix_pre_g", "mix_post_g", "ffn2_pre_g", "ffn2_post_g",
             "ffn2_w_in", "ffn2_w_out", "conv_w_in", "conv_k", "conv_w_out", "kv_g", "kv_w", "forget_b", "attn_w_qg", "attn_w_o"]
    out = [loss, grad_x]
    for idx in range(4):
        out += [res[k][idx] for k in order]
    return tuple(out)
```

```python
import functools
import math

import jax
import jax.numpy as jnp
from jax import lax
from jax.experimental import pallas as pl
from jax.experimental.pallas import tpu as pltpu

F32 = jnp.float32
MM_DTYPE = jnp.bfloat16
WIRE_DTYPE = jnp.bfloat16

RMS_EPS = 1e-6
ADAM_LR = 0.001
ADAM_B1 = 0.9
ADAM_B2 = 0.999
ADAM_EPS = 1e-08
ADAM_WD = 0.01
ADAM_STEP = 10

HEAD_DIM = 64
LANES = 128
N_CHIP = 4
N_DEV = 8
ROW_TILE = 256
MM_TILE = 512
TN_TILE = 2048
ATT_BLOCK = 512
SMALL_ROWS = 24
VMEM_LIMIT = 56 * 1024 * 1024
MESH = pl.DeviceIdType.MESH
ANY = pl.BlockSpec(memory_space=pl.ANY)

NT = (((1,), (1,)), ((), ()))
TN = (((0,), (0,)), ((), ()))


def _tile(n, pref):
    if n <= pref:
        return n
    t = pref - pref % 16
    while n % t:
        t -= 16
    return t


def _params():
    return pltpu.CompilerParams(vmem_limit_bytes=VMEM_LIMIT)


def _sds(shape, dtype):
    return jax.ShapeDtypeStruct(shape, dtype)


def _rows(tm, c):
    return pl.BlockSpec((tm, c), lambda i: (i, 0))


def _whole(shape):
    return pl.BlockSpec(shape, lambda *_: (0,) * len(shape))


def _rms_fwd(x, g, tag):
    T, D = x.shape
    tm = _tile(T, ROW_TILE)

    def body(x_ref, g_ref, o_ref):
        xv = x_ref[...]
        r = lax.rsqrt(jnp.mean(xv * xv, axis=-1, keepdims=True) + RMS_EPS)
        o_ref[...] = (xv * r * g_ref[...]).astype(o_ref.dtype)

    return pl.pallas_call(
        body, name=f"rms_fwd_{tag}", grid=(T // tm,),
        in_specs=[_rows(tm, D), _whole((1, D))], out_specs=_rows(tm, D),
        out_shape=_sds((T, D), MM_DTYPE), compiler_params=_params())(x, g.reshape(1, D))


def _post_fwd(x, h, g, alpha, tag):
    T, D = x.shape
    tm = _tile(T, ROW_TILE)

    def body(x_ref, h_ref, g_ref, o_ref):
        hv = h_ref[...]
        r = lax.rsqrt(jnp.mean(hv * hv, axis=-1, keepdims=True) + RMS_EPS)
        o_ref[...] = x_ref[...] + alpha * (hv * r * g_ref[...])

    return pl.pallas_call(
        body, name=f"post_fwd_{tag}", grid=(T // tm,),
        in_specs=[_rows(tm, D), _rows(tm, D), _whole((1, D))], out_specs=_rows(tm, D),
        out_shape=_sds((T, D), F32), compiler_params=_params())(x, h, g.reshape(1, D))


def _accumulate(ref, part, first):
    @pl.when(first)
    def _():
        ref[...] = part

    @pl.when(jnp.logical_not(first))
    def _():
        ref[...] += part


def _post_bwd(dx, h, g, alpha, tag):
    T, D = dx.shape
    tm = _tile(T, ROW_TILE)

    def body(dx_ref, h_ref, g_ref, dh_ref, dg_ref):
        hv = h_ref[...]
        r = lax.rsqrt(jnp.mean(hv * hv, axis=-1, keepdims=True) + RMS_EPS)
        hh = hv * r
        dyn = alpha * dx_ref[...]
        _accumulate(dg_ref, jnp.sum(dyn * hh, axis=0, keepdims=True), pl.program_id(0) == 0)
        dhh = dyn * g_ref[...]
        dh = r * (dhh - hh * jnp.mean(dhh * hh, axis=-1, keepdims=True))
        dh_ref[...] = dh.astype(dh_ref.dtype)

    return pl.pallas_call(
        body, name=f"post_bwd_{tag}", grid=(T // tm,),
        in_specs=[_rows(tm, D), _rows(tm, D), _whole((1, D))],
        out_specs=[_rows(tm, D), _whole((1, D))],
        out_shape=[_sds((T, D), MM_DTYPE), _sds((1, D), F32)],
        compiler_params=_params())(dx, h, g.reshape(1, D))


def _pre_bwd(dres, dxn, x, g, tag):
    T, D = x.shape
    tm = _tile(T, ROW_TILE)

    def body(dres_ref, dxn_ref, x_ref, g_ref, dx_ref, dg_ref):
        xv = x_ref[...]
        r = lax.rsqrt(jnp.mean(xv * xv, axis=-1, keepdims=True) + RMS_EPS)
        xh = xv * r
        dn = dxn_ref[...]
        _accumulate(dg_ref, jnp.sum(dn * xh, axis=0, keepdims=True), pl.program_id(0) == 0)
        dxh = dn * g_ref[...]
        dx_ref[...] = dres_ref[...] + r * (dxh - xh * jnp.mean(dxh * xh, axis=-1, keepdims=True))

    return pl.pallas_call(
        body, name=f"pre_bwd_{tag}", grid=(T // tm,),
        in_specs=[_rows(tm, D), _rows(tm, D), _rows(tm, D), _whole((1, D))],
        out_specs=[_rows(tm, D), _whole((1, D))],
        out_shape=[_sds((T, D), F32), _sds((1, D), F32)],
        compiler_params=_params())(dres, dxn, x, g.reshape(1, D))


def _swiglu_fwd(hgu, tag):
    T, F2 = hgu.shape
    F = F2 // 2
    tm = _tile(T, ROW_TILE)

    def body(g_ref, u_ref, o_ref):
        g = g_ref[...].astype(F32)
        o_ref[...] = (g * jax.nn.sigmoid(g) * u_ref[...].astype(F32)).astype(o_ref.dtype)

    return pl.pallas_call(
        body, name=f"swiglu_fwd_{tag}", grid=(T // tm,),
        in_specs=[pl.BlockSpec((tm, F), lambda i: (i, 0)), pl.BlockSpec((tm, F), lambda i: (i, 1))],
        out_specs=_rows(tm, F), out_shape=_sds((T, F), MM_DTYPE), compiler_params=_params())(hgu, hgu)


def _swiglu_bwd(hgu, da, tag):
    T, F2 = hgu.shape
    F = F2 // 2
    tm = _tile(T, ROW_TILE)

    def body(h_ref, da_ref, o_ref):
        g = h_ref[:, :F].astype(F32)
        u = h_ref[:, F:].astype(F32)
        d = da_ref[...].astype(F32)
        sg = jax.nn.sigmoid(g)
        o_ref[:, :F] = (d * u * sg * (1.0 + g * (1.0 - sg))).astype(o_ref.dtype)
        o_ref[:, F:] = (d * g * sg).astype(o_ref.dtype)

    return pl.pallas_call(
        body, name=f"swiglu_bwd_{tag}", grid=(T // tm,),
        in_specs=[_rows(tm, F2), _rows(tm, F)], out_specs=_rows(tm, F2),
        out_shape=_sds((T, F2), MM_DTYPE), compiler_params=_params())(hgu, da)


def _loss_grad(y, tgt):
    T, D = y.shape
    tm = _tile(T, ROW_TILE)

    def body(y_ref, t_ref, dy_ref, l_ref):
        e = y_ref[...] - t_ref[...]
        row = jnp.mean(e * e, axis=-1, keepdims=True)
        part = jnp.broadcast_to(jnp.sum(row, axis=0, keepdims=True), (8, LANES))
        _accumulate(l_ref, part, pl.program_id(0) == 0)
        dy_ref[...] = e * (1.0 / D)

    dy, lsum = pl.pallas_call(
        body, name="loss_grad", grid=(T // tm,),
        in_specs=[_rows(tm, D), _rows(tm, D)], out_specs=[_rows(tm, D), _whole((8, LANES))],
        out_shape=[_sds((T, D), F32), _sds((8, LANES), F32)], compiler_params=_params())(y, tgt)
    return dy, 0.5 * lsum[0, 0]


def _shift_down(u, d, rows):
    return jnp.where(rows >= d, pltpu.roll(u, d, 0), 0.0)


def _shift_up(u, d, rows, S):
    return jnp.where(rows < S - d, pltpu.roll(u, S - d, 0), 0.0)


def _conv_fwd(bch, k8, Bl, S):
    T, D3 = bch.shape
    D = D3 // 3
    dc = min(D, 2 * LANES)
    nd = D // dc

    def body(b_ref, c_ref, h_ref, k_ref, z_ref):
        rows = lax.broadcasted_iota(jnp.int32, (S, 1), 0)
        u = c_ref[...].astype(F32) * h_ref[...].astype(F32)
        y = k_ref[2:3, :] * u + k_ref[1:2, :] * _shift_down(u, 1, rows) + k_ref[0:1, :] * _shift_down(u, 2, rows)
        z_ref[...] = (b_ref[...].astype(F32) * y).astype(z_ref.dtype)

    return pl.pallas_call(
        body, name="conv_fwd", grid=(Bl, nd),
        in_specs=[pl.BlockSpec((S, dc), lambda b, j: (b, j)),
                  pl.BlockSpec((S, dc), lambda b, j: (b, nd + j)),
                  pl.BlockSpec((S, dc), lambda b, j: (b, 2 * nd + j)),
                  pl.BlockSpec((8, dc), lambda b, j: (0, j))],
        out_specs=pl.BlockSpec((S, dc), lambda b, j: (b, j)),
        out_shape=_sds((T, D), MM_DTYPE), compiler_params=_params())(bch, bch, bch, k8)


def _conv_bwd(bch, dz, k8, Bl, S):
    T, D3 = bch.shape
    D = D3 // 3
    dc = min(D, 2 * LANES)
    nd = D // dc

    def body(b_ref, c_ref, h_ref, dz_ref, k_ref, db_ref, dc_ref, dh_ref, dk_ref):
        rows = lax.broadcasted_iota(jnp.int32, (S, 1), 0)
        bv = b_ref[...].astype(F32)
        cv = c_ref[...].astype(F32)
        hv = h_ref[...].astype(F32)
        dzv = dz_ref[...].astype(F32)
        u = cv * hv
        u1 = _shift_down(u, 1, rows)
        u2 = _shift_down(u, 2, rows)
        y = k_ref[2:3, :] * u + k_ref[1:2, :] * u1 + k_ref[0:1, :] * u2
        db_ref[...] = (dzv * y).astype(db_ref.dtype)
        dy = dzv * bv
        du = k_ref[2:3, :] * dy + k_ref[1:2, :] * _shift_up(dy, 1, rows, S) + k_ref[0:1, :] * _shift_up(dy, 2, rows, S)
        dc_ref[...] = (du * hv).astype(dc_ref.dtype)
        dh_ref[...] = (du * cv).astype(dh_ref.dtype)

        @pl.when(pl.program_id(1) == 0)
        def _():
            dk_ref[...] = jnp.zeros_like(dk_ref)

        dk_ref[0:1, :] += jnp.sum(dy * u2, axis=0, keepdims=True)
        dk_ref[1:2, :] += jnp.sum(dy * u1, axis=0, keepdims=True)
        dk_ref[2:3, :] += jnp.sum(dy * u, axis=0, keepdims=True)

    seq = lambda off: pl.BlockSpec((S, dc), lambda j, b: (b, off + j))
    return pl.pallas_call(
        body, name="conv_bwd", grid=(nd, Bl),
        in_specs=[seq(0), seq(nd), seq(2 * nd), seq(0), pl.BlockSpec((8, dc), lambda j, b: (0, j))],
        out_specs=[seq(0), seq(0), seq(0), pl.BlockSpec((8, dc), lambda j, b: (0, j))],
        out_shape=[_sds((T, D), MM_DTYPE)] * 3 + [_sds((8, D), F32)],
        compiler_params=_params())(bch, bch, bch, dz, k8)


def _forget_fwd(pf, fb, Bl, S):
    T = pf.shape[0]

    def body(p_ref, fb_ref, c_ref):
        rows = lax.broadcasted_iota(jnp.int32, (S, 1), 0)
        z = p_ref[...] + fb_ref[...]
        acc = jnp.minimum(z, 0.0) - jnp.log1p(jnp.exp(-jnp.abs(z)))
        d = 1
        while d < S:
            acc = acc + _shift_down(acc, d, rows)
            d *= 2
        c_ref[...] = acc

    return pl.pallas_call(
        body, name="forget_fwd", grid=(Bl,),
        in_specs=[_rows(S, LANES), _whole((1, LANES))], out_specs=_rows(S, LANES),
        out_shape=_sds((T, LANES), F32), compiler_params=_params())(pf, fb)


def _forget_bwd(dc, pf, fb, Bl, S):
    T = pf.shape[0]

    def body(dc_ref, p_ref, fb_ref, df_ref, dfb_ref):
        rows = lax.broadcasted_iota(jnp.int32, (S, 1), 0)
        acc = dc_ref[...]
        d = 1
        while d < S:
            acc = acc + _shift_up(acc, d, rows, S)
            d *= 2
        df = acc * jax.nn.sigmoid(-(p_ref[...] + fb_ref[...]))
        df_ref[...] = df.astype(df_ref.dtype)
        _accumulate(dfb_ref, jnp.sum(df, axis=0, keepdims=True), pl.program_id(0) == 0)

    return pl.pallas_call(
        body, name="forget_bwd", grid=(Bl,),
        in_specs=[_rows(S, LANES), _rows(S, LANES), _whole((1, LANES))],
        out_specs=[_rows(S, LANES), _whole((1, LANES))],
        out_shape=[_sds((T, LANES), MM_DTYPE), _sds((1, LANES), F32)],
        compiler_params=_params())(dc, pf, fb)


def _head_mask(h):
    lane = lax.broadcasted_iota(jnp.int32, (1, LANES), 1)
    return (lane >= h * HEAD_DIM) & (lane < (h + 1) * HEAD_DIM)


def _attn_fwd(qg, kv, c_col, c_row, Bl, S, D):
    T = Bl * S
    H = D // HEAD_DIM
    HP = D // LANES
    bq = min(S, ATT_BLOCK)
    nq = S // bq
    scale = 1.0 / math.sqrt(HEAD_DIM)

    def body(q_ref, k_ref, v_ref, cc_ref, cr_ref, o_ref, lse_ref):
        i = pl.program_id(2)
        q2 = q_ref[...]
        qh = [q2 * (_head_mask(h).astype(F32) * scale).astype(q2.dtype) for h in range(2)]
        cc = [cc_ref[h][:, :1] for h in range(2)]
        diag = lax.broadcasted_iota(jnp.int32, (1, bq), 1) <= lax.broadcasted_iota(jnp.int32, (bq, 1), 0)

        def block(j, carry, on_diagonal):
            off = pl.multiple_of(j * bq, bq)
            kj = k_ref[pl.ds(off, bq), :]
            vj = v_ref[pl.ds(off, bq), :]
            new = []
            for h in range(2):
                m, l, acc = carry[h]
                s = lax.dot_general(qh[h], kj, NT, preferred_element_type=F32) + cc[h] - cr_ref[h, j]
                if on_diagonal:
                    s = jnp.where(diag, s, -jnp.inf)
                m_new = jnp.maximum(m, jnp.max(s, axis=1, keepdims=True))
                p = jnp.exp(s - m_new)
                a = jnp.exp(m - m_new)
                l = a * l + jnp.sum(p, axis=1, keepdims=True)
                acc = a * acc + jnp.dot(p.astype(MM_DTYPE), vj, preferred_element_type=F32)
                new.append((m_new, l, acc))
            return tuple(new)

        one = (jnp.full((bq, 1), -jnp.inf, F32), jnp.zeros((bq, 1), F32), jnp.zeros((bq, LANES), F32))
        carry = lax.fori_loop(0, i, lambda j, c: block(j, c, False), (one, one))
        carry = block(i, carry, True)
        outs = []
        for h in range(2):
            m, l, acc = carry[h]
            outs.append(acc / l)
            lse_ref[h] = jnp.broadcast_to(m + jnp.log(l), (bq, LANES))
        o_ref[...] = jnp.where(_head_mask(0), outs[0], outs[1])

    return pl.pallas_call(
        body, name="attn_fwd", grid=(Bl, HP, nq),
        in_specs=[pl.BlockSpec((bq, LANES), lambda b, hp, i: (b * nq + i, hp)),
                  pl.BlockSpec((S, LANES), lambda b, hp, i: (b, hp)),
                  pl.BlockSpec((S, LANES), lambda b, hp, i: (b, HP + hp)),
                  pl.BlockSpec((None, 2, bq, LANES), lambda b, hp, i: (b, hp, i, 0)),
                  pl.BlockSpec((None, 2, nq, 1, bq), lambda b, hp, i: (b, hp, 0, 0, 0))],
        out_specs=[pl.BlockSpec((bq, LANES), lambda b, hp, i: (b * nq + i, hp)),
                   pl.BlockSpec((None, 2, bq, LANES), lambda b, hp, i: (b, hp, i, 0))],
        out_shape=[_sds((T, D), F32), _sds((Bl, H, S, LANES), F32)],
        compiler_params=_params())(qg, kv, kv, c_col, c_row)


def _attn_bwd(qg, kv, do, lse, c_col, c_row, Bl, S, D):
    T = Bl * S
    H = D // HEAD_DIM
    HP = D // LANES
    bq = min(S, ATT_BLOCK)
    nq = S // bq
    scale = 1.0 / math.sqrt(HEAD_DIM)

    def body(q_ref, k_ref, v_ref, do_ref, lse_ref, cc_ref, cr_ref, dq_ref, dk_ref, dv_ref, dcr_ref, p_sc, dp_sc):
        i = pl.program_id(2)

        @pl.when(i == 0)
        def _():
            dk_ref[...] = jnp.zeros_like(dk_ref)
            dv_ref[...] = jnp.zeros_like(dv_ref)
            dcr_ref[...] = jnp.zeros_like(dcr_ref)

        q2 = q_ref[...]
        do2 = do_ref[...]
        masks = [_head_mask(h).astype(F32) for h in range(2)]
        qh = [q2 * (masks[h] * scale).astype(q2.dtype) for h in range(2)]
        doh = [do2 * masks[h].astype(do2.dtype) for h in range(2)]
        cc = [cc_ref[h][:, :1] for h in range(2)]
        lse = [lse_ref[h][:, :1] for h in range(2)]
        diag = lax.broadcasted_iota(jnp.int32, (1, bq), 1) <= lax.broadcasted_iota(jnp.int32, (bq, 1), 0)

        def sweep1(j, delta, on_diagonal):
            off = pl.multiple_of(j * bq, bq)
            kj = k_ref[pl.ds(off, bq), :]
            vj = v_ref[pl.ds(off, bq), :]
            new = []
            dv = None
            for h in range(2):
                s = lax.dot_general(qh[h], kj, NT, preferred_element_type=F32) + cc[h] - cr_ref[h, j]
                if on_diagonal:
                    s = jnp.where(diag, s, -jnp.inf)
                p = jnp.exp(s - lse[h])
                dp = lax.dot_general(doh[h], vj, NT, preferred_element_type=F32)
                p_sc[h, j] = p
                dp_sc[h, j] = dp
                part = lax.dot_general(p.astype(MM_DTYPE), doh[h], TN, preferred_element_type=F32)
                dv = part if dv is None else dv + part
                new.append(delta[h] + jnp.sum(p * dp, axis=1, keepdims=True))
            dv_ref[pl.ds(off, bq), :] += dv
            return tuple(new)

        zero = jnp.zeros((bq, 1), F32)
        delta = lax.fori_loop(0, i, lambda j, d: sweep1(j, d, False), (zero, zero))
        delta = sweep1(i, delta, True)

        def sweep2(j, dq):
            off = pl.multiple_of(j * bq, bq)
            kj = k_ref[pl.ds(off, bq), :]
            dk = None
            for h in range(2):
                ds = p_sc[h, j] * (dp_sc[h, j] - delta[h])
                dcr_ref[h, j] -= jnp.sum(ds, axis=0, keepdims=True)
                dsb = ds.astype(MM_DTYPE)
                dq = dq + jnp.dot(dsb, kj * (masks[h] * scale).astype(kj.dtype), preferred_element_type=F32)
                part = lax.dot_general(dsb, qh[h], TN, preferred_element_type=F32)
                dk = part if dk is None else dk + part
            dk_ref[pl.ds(off, bq), :] += dk
            return dq

        dq_ref[...] = lax.fori_loop(0, i + 1, sweep2, jnp.zeros((bq, LANES), F32))

    blk = lambda col: pl.BlockSpec((bq, LANES), lambda b, hp, i: (b * nq + i, col(hp)))
    seq = lambda col: pl.BlockSpec((S, LANES), lambda b, hp, i: (b, col(hp)))
    per_head = pl.BlockSpec((None, 2, bq, LANES), lambda b, hp, i: (b, hp, i, 0))
    rows = pl.BlockSpec((None, 2, nq, 1, bq), lambda b, hp, i: (b, hp, 0, 0, 0))
    return pl.pallas_call(
        body, name="attn_bwd", grid=(Bl, HP, nq),
        in_specs=[blk(lambda hp: hp), seq(lambda hp: hp), seq(lambda hp: HP + hp), blk(lambda hp: hp),
                  per_head, per_head, rows],
        out_specs=[blk(lambda hp: hp), seq(lambda hp: hp), seq(lambda hp: hp), rows],
        out_shape=[_sds((T, D), F32), _sds((T, D), F32), _sds((T, D), F32), _sds((Bl, H, nq, 1, bq), F32)],
        scratch_shapes=[pltpu.VMEM((2, nq, bq, bq), F32), pltpu.VMEM((2, nq, bq, bq), F32)],
        compiler_params=_params())(qg, kv, kv, do, lse, c_col, c_row)


def _gate_fwd(qg, o):
    T, D = o.shape
    tm = _tile(T, ROW_TILE)

    def body(g_ref, o_ref, z_ref):
        z_ref[...] = (jax.nn.sigmoid(g_ref[...].astype(F32)) * o_ref[...]).astype(z_ref.dtype)

    return pl.pallas_call(
        body, name="gate_fwd", grid=(T // tm,),
        in_specs=[pl.BlockSpec((tm, D), lambda i: (i, 1)), _rows(tm, D)], out_specs=_rows(tm, D),
        out_shape=_sds((T, D), MM_DTYPE), compiler_params=_params())(qg, o)


def _gate_do(dz, qg):
    T, D = dz.shape
    tm = _tile(T, ROW_TILE)

    def body(dz_ref, g_ref, do_ref):
        do_ref[...] = (dz_ref[...].astype(F32) * jax.nn.sigmoid(g_ref[...].astype(F32))).astype(do_ref.dtype)

    return pl.pallas_call(
        body, name="gate_do", grid=(T // tm,),
        in_specs=[_rows(tm, D), pl.BlockSpec((tm, D), lambda i: (i, 1))], out_specs=_rows(tm, D),
        out_shape=_sds((T, D), MM_DTYPE), compiler_params=_params())(dz, qg)


def _gate_bwd(dz, qg, o, dq):
    T, D = dz.shape
    tm = _tile(T, ROW_TILE)

    def body(dz_ref, g_ref, o_ref, dq_ref, out_ref):
        g = g_ref[...].astype(F32)
        sg = jax.nn.sigmoid(g)
        out_ref[:, :D] = dq_ref[...].astype(out_ref.dtype)
        out_ref[:, D:] = (dz_ref[...].astype(F32) * o_ref[...] * sg * (1.0 - sg)).astype(out_ref.dtype)

    return pl.pallas_call(
        body, name="gate_bwd", grid=(T // tm,),
        in_specs=[_rows(tm, D), pl.BlockSpec((tm, D), lambda i: (i, 1)), _rows(tm, D), _rows(tm, D)],
        out_specs=_rows(tm, 2 * D), out_shape=_sds((T, 2 * D), MM_DTYPE),
        compiler_params=_params())(dz, qg, o, dq)


def _mm_in(a, wg, out_dtype, tag, l=None):
    T, K = a.shape
    n = wg.shape[-1]
    tm = _tile(T, MM_TILE)
    if l is None:
        w_spec = _whole((N_CHIP, K, n))
    else:
        w_spec = pl.BlockSpec((None, N_CHIP, K, n), lambda i: (l, 0, 0, 0))

    def body(a_ref, w_ref, o_ref):
        av = a_ref[...]
        for s in range(N_CHIP):
            o_ref[:, s * n:(s + 1) * n] = jnp.dot(av, w_ref[s], preferred_element_type=F32).astype(o_ref.dtype)

    return pl.pallas_call(
        body, name=f"mm_in_{tag}", grid=(T // tm,),
        in_specs=[_rows(tm, K), w_spec], out_specs=_rows(tm, N_CHIP * n),
        out_shape=_sds((T, N_CHIP * n), out_dtype), compiler_params=_params())(a, wg)


def _mm_nt_in(dy, wg, tag, l=None):
    T = dy.shape[0]
    K, n = wg.shape[-2:]
    tm = _tile(T, MM_TILE)
    if l is None:
        w_spec = _whole((N_CHIP, K, n))
    else:
        w_spec = pl.BlockSpec((None, N_CHIP, K, n), lambda i: (l, 0, 0, 0))

    def body(d_ref, w_ref, o_ref):
        acc = None
        for s in range(N_CHIP):
            part = lax.dot_general(d_ref[:, s * n:(s + 1) * n], w_ref[s], NT, preferred_element_type=F32)
            acc = part if acc is None else acc + part
        o_ref[...] = acc

    return pl.pallas_call(
        body, name=f"mm_nt_in_{tag}", grid=(T // tm,),
        in_specs=[_rows(tm, N_CHIP * n), w_spec], out_specs=_rows(tm, K),
        out_shape=_sds((T, K), F32), compiler_params=_params())(dy, wg)


def _mm_nn(a, b, out_dtype, tag):
    T, K = a.shape
    N = b.shape[1]
    tm = _tile(T, MM_TILE)

    def body(a_ref, b_ref, o_ref):
        o_ref[...] = jnp.dot(a_ref[...], b_ref[...], preferred_element_type=F32).astype(o_ref.dtype)

    return pl.pallas_call(
        body, name=f"mm_nn_{tag}", grid=(T // tm,),
        in_specs=[_rows(tm, K), _whole((K, N))], out_specs=_rows(tm, N),
        out_shape=_sds((T, N), out_dtype), compiler_params=_params())(a, b)


def _mm_nt(a, b, out_dtype, tag):
    T, C = a.shape
    N = b.shape[0]
    tm = _tile(T, MM_TILE)
    nb = N
    for cand in (1408, 1024):
        if N > cand and N % cand == 0:
            nb = cand
            break

    def body(a_ref, b_ref, o_ref):
        o_ref[...] = lax.dot_general(a_ref[...], b_ref[...], NT, preferred_element_type=F32).astype(o_ref.dtype)

    return pl.pallas_call(
        body, name=f"mm_nt_{tag}", grid=(N // nb, T // tm),
        in_specs=[pl.BlockSpec((tm, C), lambda j, i: (i, 0)), pl.BlockSpec((nb, C), lambda j, i: (j, 0))],
        out_specs=pl.BlockSpec((tm, nb), lambda j, i: (i, j)),
        out_shape=_sds((T, N), out_dtype), compiler_params=_params())(a, b)


def _mm_tn_in(a, dy, tag, l=None, prev=None):
    T, K = a.shape
    n = dy.shape[1] // N_CHIP
    tt = _tile(T, TN_TILE)

    def body(a_ref, d_ref, *rest):
        o_ref = rest[-1]
        part = lax.dot_general(a_ref[...], d_ref[...], TN, preferred_element_type=F32)
        _accumulate(o_ref, part, pl.program_id(1) == 0)

    in_specs = [pl.BlockSpec((tt, K), lambda s, t: (t, 0)), pl.BlockSpec((tt, n), lambda s, t: (t, s))]
    args = [a, dy]
    kw = {}
    if l is None:
        out_spec = pl.BlockSpec((None, K, n), lambda s, t: (s, 0, 0))
        out_shape = _sds((N_CHIP, K, n), F32)
    else:
        out_spec = pl.BlockSpec((None, None, K, n), lambda s, t: (l, s, 0, 0))
        out_shape = _sds((2, N_CHIP, K, n), F32)
        if prev is not None:
            in_specs.append(ANY)
            args.append(prev)
            kw["input_output_aliases"] = {2: 0}
    return pl.pallas_call(
        body, name=f"mm_tn_in_{tag}", grid=(N_CHIP, T // tt),
        in_specs=in_specs, out_specs=out_spec, out_shape=out_shape,
        compiler_params=_params(), **kw)(*args)


def _mm_tn_out(act, dh, tag, l=None, prev=None):
    T, R4 = act.shape
    D = dh.shape[1]
    r = R4 // N_CHIP
    g = 1 if r % LANES == 0 else 2
    tt = _tile(T, TN_TILE)

    def body(a_ref, d_ref, *rest):
        o_ref = rest[-1]
        part = lax.dot_general(a_ref[...], d_ref[...], TN, preferred_element_type=F32)
        first = pl.program_id(1) == 0
        for q in range(g):
            _accumulate(o_ref.at[q], part[q * r:(q + 1) * r], first)

    in_specs = [pl.BlockSpec((tt, g * r), lambda s, t: (t, s)), pl.BlockSpec((tt, D), lambda s, t: (t, 0))]
    args = [act, dh]
    kw = {}
    if l is None:
        out_spec = pl.BlockSpec((g, r, D), lambda s, t: (s, 0, 0))
        out_shape = _sds((N_CHIP, r, D), F32)
    else:
        out_spec = pl.BlockSpec((None, g, r, D), lambda s, t: (l, s, 0, 0))
        out_shape = _sds((2, N_CHIP, r, D), F32)
        if prev is not None:
            in_specs.append(ANY)
            args.append(prev)
            kw["input_output_aliases"] = {2: 0}
    return pl.pallas_call(
        body, name=f"mm_tn_out_{tag}", grid=(N_CHIP // g, T // tt),
        in_specs=in_specs, out_specs=out_spec, out_shape=out_shape,
        compiler_params=_params(), **kw)(*args)


def _mm_tn(a, b, tag):
    T, K = a.shape
    N = b.shape[1]
    tt = _tile(T, TN_TILE)

    def body(a_ref, b_ref, o_ref):
        part = lax.dot_general(a_ref[...], b_ref[...], TN, preferred_element_type=F32)
        _accumulate(o_ref, part, pl.program_id(0) == 0)

    return pl.pallas_call(
        body, name=f"mm_tn_{tag}", grid=(T // tt,),
        in_specs=[_rows(tt, K), _rows(tt, N)], out_specs=_whole((K, N)),
        out_shape=_sds((K, N), F32), compiler_params=_params())(a, b)


def _norm_mm_in(x, g, wg, tag, swiglu=False):
    T, D = x.shape
    n = wg.shape[-1]
    tm = _tile(T, ROW_TILE)
    half = N_CHIP // 2

    def body(x_ref, g_ref, w_ref, xn_ref, y_ref, *rest):
        xv = x_ref[...]
        r = lax.rsqrt(jnp.mean(xv * xv, axis=-1, keepdims=True) + RMS_EPS)
        xn = (xv * r * g_ref[...]).astype(xn_ref.dtype)
        xn_ref[...] = xn

        def product(s):
            p = jnp.dot(xn, w_ref[s], preferred_element_type=F32)
            y_ref[:, s * n:(s + 1) * n] = p.astype(y_ref.dtype)
            return p

        if swiglu:
            for q in range(half):
                gate, up = product(q), product(half + q)
                rest[0][:, q * n:(q + 1) * n] = (gate * jax.nn.sigmoid(gate) * up).astype(rest[0].dtype)
        else:
            for s in range(N_CHIP):
                product(s)

    out_specs = [_rows(tm, D), _rows(tm, N_CHIP * n)]
    out_shape = [_sds((T, D), MM_DTYPE), _sds((T, N_CHIP * n), MM_DTYPE)]
    if swiglu:
        out_specs.append(_rows(tm, half * n))
        out_shape.append(_sds((T, half * n), MM_DTYPE))
    return pl.pallas_call(
        body, name=f"norm_mm_in_{tag}", grid=(T // tm,),
        in_specs=[_rows(tm, D), _whole((1, D)), _whole((N_CHIP, D, n))], out_specs=out_specs,
        out_shape=out_shape, compiler_params=_params())(x, g.reshape(1, D), wg)


def _mm_out_post(a, b, x, g, alpha, tag):
    T, K = a.shape
    D = b.shape[1]
    tm = _tile(T, ROW_TILE)

    def body(a_ref, b_ref, x_ref, g_ref, h_ref, o_ref):
        hv = jnp.dot(a_ref[...], b_ref[...], preferred_element_type=F32)
        h_ref[...] = hv
        r = lax.rsqrt(jnp.mean(hv * hv, axis=-1, keepdims=True) + RMS_EPS)
        o_ref[...] = x_ref[...] + alpha * (hv * r * g_ref[...])

    return pl.pallas_call(
        body, name=f"mm_out_post_{tag}", grid=(T // tm,),
        in_specs=[_rows(tm, K), _whole((K, D)), _rows(tm, D), _whole((1, D))],
        out_specs=[_rows(tm, D), _rows(tm, D)], out_shape=[_sds((T, D), F32)] * 2,
        compiler_params=_params())(a, b, x, g.reshape(1, D))


def _post_bwd_mm(dx, h, g, alpha, b, tag, hgu=None):
    T, D = dx.shape
    K = b.shape[0]
    tm = _tile(T, ROW_TILE)

    def body(dx_ref, h_ref, g_ref, b_ref, *rest):
        dh_ref, dg_ref, out_ref = rest[-3:]
        hv = h_ref[...]
        r = lax.rsqrt(jnp.mean(hv * hv, axis=-1, keepdims=True) + RMS_EPS)
        hh = hv * r
        dyn = alpha * dx_ref[...]
        _accumulate(dg_ref, jnp.sum(dyn * hh, axis=0, keepdims=True), pl.program_id(0) == 0)
        dhh = dyn * g_ref[...]
        dh = (r * (dhh - hh * jnp.mean(dhh * hh, axis=-1, keepdims=True))).astype(dh_ref.dtype)
        dh_ref[...] = dh
        da = lax.dot_general(dh, b_ref[...], NT, preferred_element_type=F32)
        if hgu is None:
            out_ref[...] = da.astype(out_ref.dtype)
        else:
            gate = rest[0][:, :K].astype(F32)
            up = rest[0][:, K:].astype(F32)
            sg = jax.nn.sigmoid(gate)
            out_ref[:, :K] = (da * up * sg * (1.0 + gate * (1.0 - sg))).astype(out_ref.dtype)
            out_ref[:, K:] = (da * gate * sg).astype(out_ref.dtype)

    in_specs = [_rows(tm, D), _rows(tm, D), _whole((1, D)), _whole((K, D))]
    args = [dx, h, g.reshape(1, D), b]
    wide = K
    if hgu is not None:
        wide = 2 * K
        in_specs.append(_rows(tm, wide))
        args.append(hgu)
    return pl.pallas_call(
        body, name=f"post_bwd_mm_{tag}", grid=(T // tm,), in_specs=in_specs,
        out_specs=[_rows(tm, D), _whole((1, D)), _rows(tm, wide)],
        out_shape=[_sds((T, D), MM_DTYPE), _sds((1, D), F32), _sds((T, wide), MM_DTYPE)],
        compiler_params=_params())(*args)


def _mm_nt_pre(dy, w, dres, x, g, tag):
    T, C = dy.shape
    D = x.shape[1]
    tm = _tile(T, ROW_TILE)
    n = w.shape[-1]

    def body(dy_ref, w_ref, dres_ref, x_ref, g_ref, dx_ref, dg_ref):
        if w.ndim == 2:
            dn = lax.dot_general(dy_ref[...], w_ref[...], NT, preferred_element_type=F32)
        else:
            dn = None
            for s in range(N_CHIP):
                part = lax.dot_general(dy_ref[:, s * n:(s + 1) * n], w_ref[s], NT, preferred_element_type=F32)
                dn = part if dn is None else dn + part
        xv = x_ref[...]
        r = lax.rsqrt(jnp.mean(xv * xv, axis=-1, keepdims=True) + RMS_EPS)
        xh = xv * r
        _accumulate(dg_ref, jnp.sum(dn * xh, axis=0, keepdims=True), pl.program_id(0) == 0)
        dxh = dn * g_ref[...]
        dx_ref[...] = dres_ref[...] + r * (dxh - xh * jnp.mean(dxh * xh, axis=-1, keepdims=True))

    return pl.pallas_call(
        body, name=f"mm_nt_pre_{tag}", grid=(T // tm,),
        in_specs=[_rows(tm, C), _whole(w.shape), _rows(tm, D), _rows(tm, D), _whole((1, D))],
        out_specs=[_rows(tm, D), _whole((1, D))], out_shape=[_sds((T, D), F32), _sds((1, D), F32)],
        compiler_params=_params())(dy, w, dres, x, g.reshape(1, D))


def _adamw(w, g, m, v, tag):
    R, C = w.shape
    tr = _tile(R, ROW_TILE)

    def body(w_ref, g_ref, m_ref, v_ref, d_ref, mo_ref, vo_ref):
        gv = g_ref[...]
        mn = ADAM_B1 * m_ref[...] + (1.0 - ADAM_B1) * gv
        vn = ADAM_B2 * v_ref[...] + (1.0 - ADAM_B2) * (gv * gv)
        m_hat = mn / (1.0 - ADAM_B1 ** ADAM_STEP)
        v_hat = vn / (1.0 - ADAM_B2 ** ADAM_STEP)
        d_ref[...] = -ADAM_LR * (m_hat / (jnp.sqrt(v_hat) + ADAM_EPS) + ADAM_WD * w_ref[...])
        mo_ref[...] = mn
        vo_ref[...] = vn

    return pl.pallas_call(
        body, name=f"adamw_{tag}", grid=(R // tr,),
        in_specs=[_rows(tr, C)] * 4, out_specs=[_rows(tr, C)] * 3,
        out_shape=[_sds((R, C), F32)] * 3, compiler_params=_params())(w, g, m, v)


def _sum_devices(gall):
    _, R, C = gall.shape

    def body(g_ref, o_ref):
        acc = g_ref[0]
        for d in range(1, N_DEV):
            acc = acc + g_ref[d]
        o_ref[...] = acc

    return pl.pallas_call(
        body, name="sum_devices", in_specs=[_whole((N_DEV, R, C))], out_specs=_whole((R, C)),
        out_shape=_sds((R, C), F32), grid=(1,), compiler_params=_params())(gall)


HBM = pl.BlockSpec(memory_space=pltpu.HBM)
SEM = pl.BlockSpec(memory_space=pltpu.SEMAPHORE)
EFFECT = pltpu.SideEffectType.DATAFLOW_SIDE_EFFECTING


def _place():
    x, y, c = lax.axis_index("x"), lax.axis_index("y"), lax.axis_index("c")
    chips = ((1 - x, y), (x, 1 - y), (1 - x, 1 - y))
    return x, y, c, chips


def _remote(src, dst, send_sem, recv_sem, dev):
    return pltpu.make_async_remote_copy(src_ref=src, dst_ref=dst, send_sem=send_sem, recv_sem=recv_sem,
                                        device_id=dev, device_id_type=MESH)


def _in_hbm(a):
    return pltpu.with_memory_space_constraint(a, pltpu.HBM)


def _own_slot(w4, l, dtype, place, tag):
    _, _, r, col = w4.shape
    tr = _tile(r, 2 * ROW_TILE)

    def body(place_ref, x_ref, o_ref):
        o_ref[...] = x_ref[...].astype(o_ref.dtype)

    grid_spec = pltpu.PrefetchScalarGridSpec(
        num_scalar_prefetch=1, grid=(2, r // tr),
        in_specs=[pl.BlockSpec((None, None, tr, col), lambda h, i, p: (l, h, i, 0))],
        out_specs=pl.BlockSpec((None, None, tr, col), lambda h, i, p: (p[1], h, i, 0)))
    return pl.pallas_call(
        body, name=f"own_slot_{tag}", grid_spec=grid_spec, out_shape=_sds((N_CHIP, 2, r, col), dtype),
        compiler_params=_params())(place, w4)


def _gather_start(bufs, after, tag):
    n = len(bufs)

    def body(*refs):
        ins = refs[:n]
        s_sem, r_sem, token = refs[n + 1], refs[n + 2], refs[2 * n + 3]
        x, y, c, chips = _place()
        me = 2 * x + y
        for i in range(n):
            mine = ins[i].at[me, c]
            for j, (px, py) in enumerate(chips):
                _remote(mine, mine, s_sem.at[3 * i + j], r_sem.at[3 * i + j], (px, py, c)).start()
        token[...] = jnp.zeros_like(token)

    dma = pltpu.SemaphoreType.DMA
    res = pl.pallas_call(
        body, name=f"gather_start_{tag}", in_specs=[HBM] * n + [ANY],
        out_specs=[SEM, SEM] + [HBM] * n + [pl.BlockSpec(memory_space=pltpu.VMEM)],
        out_shape=[dma((3 * n,)), dma((3 * n,))] + [pltpu.HBM(b.shape, b.dtype) for b in bufs] + [_sds((8, LANES), F32)],
        input_output_aliases={i: i + 2 for i in range(n)},
        compiler_params=pltpu.CompilerParams(has_side_effects=EFFECT),
        )(*[_in_hbm(b) for b in bufs], after)
    return res[0], res[1], list(res[2:2 + n]), res[-1]


def _gather_pass(s_sem, r_sem, bufs, first, after, tag):
    n = len(bufs)

    def body(*refs):
        ins = refs[:n]
        a_s, a_r, b_s, b_r = refs[n], refs[n + 1], refs[n + 3], refs[n + 4]
        x, y, c, chips = _place()
        me = 2 * x + y
        sib = (x, y, 1 - c)
        for i in range(n):
            mine = ins[i].at[me, c]
            for j, (px, py) in enumerate(chips):
                k = 3 * (first + i) + j
                _remote(mine, mine, a_s.at[k], a_r.at[k], (px, py, c)).wait_send()
        for j, (px, py) in enumerate(chips):
            for i in range(n):
                k = 3 * (first + i) + j
                blk = ins[i].at[2 * px + py, c]
                _remote(blk, blk, a_s.at[k], a_r.at[k], (px, py, c)).wait_recv()
                _remote(blk, blk, b_s.at[3 * i + j], b_r.at[3 * i + j], sib).start()

    dma = pltpu.SemaphoreType.DMA
    res = pl.pallas_call(
        body, name=f"gather_pass_{tag}", in_specs=[HBM] * n + [SEM, SEM, ANY],
        out_specs=[SEM, SEM] + [HBM] * n,
        out_shape=[dma((3 * n,)), dma((3 * n,))] + [pltpu.HBM(b.shape, b.dtype) for b in bufs],
        input_output_aliases={i: i + 2 for i in range(n)},
        compiler_params=pltpu.CompilerParams(has_side_effects=EFFECT),
        )(*bufs, s_sem, r_sem, after)
    return res[0], res[1], list(res[2:])


def _gather_land(s_sem, r_sem, bufs, tag):
    n = len(bufs)

    def body(*refs):
        ins = refs[:n]
        b_s, b_r = refs[n], refs[n + 1]
        x, y, c, chips = _place()
        sib = (x, y, 1 - c)
        for j, (px, py) in enumerate(chips):
            for i in range(n):
                sent = ins[i].at[2 * px + py, c]
                got = ins[i].at[2 * px + py, 1 - c]
                _remote(sent, sent, b_s.at[3 * i + j], b_r.at[3 * i + j], sib).wait_send()
                _remote(got, got, b_s.at[3 * i + j], b_r.at[3 * i + j], sib).wait_recv()

    return list(pl.pallas_call(
        body, name=f"gather_land_{tag}", in_specs=[HBM] * n + [SEM, SEM], out_specs=[HBM] * n,
        out_shape=[pltpu.HBM(b.shape, b.dtype) for b in bufs],
        input_output_aliases={i: i for i in range(n)},
        compiler_params=pltpu.CompilerParams(has_side_effects=EFFECT),
        )(*bufs, s_sem, r_sem))


def _rs_pair_send(grads):
    n = len(grads)

    def body(*refs):
        ins, outs = refs[:n], refs[n:2 * n]
        s_sem, r_sem = refs[2 * n:]
        x, y, c, _ = _place()
        sib = (x, y, 1 - c)
        sends = []
        for i in range(n):
            cp = _remote(ins[i].at[:, 1 - c], outs[i], s_sem.at[i], r_sem.at[i], sib)
            cp.start()
            sends.append(cp)
        for cp in sends:
            cp.wait()

    out_shape = [_sds((N_CHIP,) + g.shape[2:], g.dtype) for g in grads]
    dma = pltpu.SemaphoreType.DMA
    return pl.pallas_call(
        body, name=f"rs_pair_send_{n}", in_specs=[ANY] * n, out_specs=[ANY] * n, out_shape=out_shape,
        scratch_shapes=[dma((n,)), dma((n,))],
        )(*grads)


def _rs_pair_add(g, recv, place, tag):
    r, col = g.shape[-2:]
    tr = _tile(r, ROW_TILE)

    def body(place_ref, g_ref, r_ref, wire_ref, own_ref):
        tot = g_ref[...] + r_ref[...]
        wire_ref[...] = tot.astype(wire_ref.dtype)

        @pl.when(pl.program_id(1) == place_ref[1])
        def _():
            own_ref[...] = tot

    grid_spec = pltpu.PrefetchScalarGridSpec(
        num_scalar_prefetch=1, grid=(r // tr, N_CHIP),
        in_specs=[pl.BlockSpec((None, None, tr, col), lambda i, s, p: (s, p[0], i, 0)),
                  pl.BlockSpec((None, tr, col), lambda i, s, p: (s, i, 0))],
        out_specs=[pl.BlockSpec((None, tr, col), lambda i, s, p: (s, i, 0)),
                   pl.BlockSpec((tr, col), lambda i, s, p: (i, 0))])
    return pl.pallas_call(
        body, name=f"rs_pair_add_{tag}", grid_spec=grid_spec,
        out_shape=[_sds((N_CHIP, r, col), WIRE_DTYPE), _sds((r, col), F32)],
        compiler_params=_params())(place, g, recv)


def _rs_start(wires):
    n = len(wires)

    def body(*refs):
        ins = refs[:n]
        s_sem, r_sem, token = refs[2 * n], refs[2 * n + 1], refs[4 * n + 2]
        x, y, c, chips = _place()
        for i in range(n):
            land = refs[n + i]
            for j, (px, py) in enumerate(chips):
                _remote(ins[i].at[2 * px + py], land.at[j], s_sem.at[3 * i + j], r_sem.at[3 * i + j], (px, py, c)).start()
        token[...] = jnp.zeros_like(token)

    lands = [lax.empty((3,) + w.shape[1:], w.dtype) for w in wires]
    both = list(wires) + lands
    dma = pltpu.SemaphoreType.DMA
    res = pl.pallas_call(
        body, name="rs_start", in_specs=[HBM] * (2 * n),
        out_specs=[SEM, SEM] + [HBM] * (2 * n) + [pl.BlockSpec(memory_space=pltpu.VMEM)],
        out_shape=[dma((3 * n,)), dma((3 * n,))] + [pltpu.HBM(b.shape, b.dtype) for b in both] + [_sds((8, LANES), F32)],
        input_output_aliases={i: i + 2 for i in range(2 * n)},
        compiler_params=pltpu.CompilerParams(has_side_effects=EFFECT),
        )(*[_in_hbm(b) for b in both])
    return res[0], res[1], res[2:2 + n], res[2 + n:2 + 2 * n], res[-1]


def _rs_wait(s_sem, r_sem, wires, lands, after):
    n = len(wires)

    def body(*refs):
        ins = refs[:n]
        s_ref, r_ref = refs[2 * n], refs[2 * n + 1]
        x, y, c, chips = _place()
        for i in range(n):
            land = refs[n + i]
            for j, (px, py) in enumerate(chips):
                cp = _remote(ins[i].at[2 * px + py], land.at[j], s_ref.at[3 * i + j], r_ref.at[3 * i + j], (px, py, c))
                cp.wait_send()
                cp.wait_recv()

    both = list(wires) + list(lands)
    res = pl.pallas_call(
        body, name="rs_wait", in_specs=[HBM] * (2 * n) + [SEM, SEM, ANY], out_specs=[HBM] * (2 * n),
        out_shape=[pltpu.HBM(b.shape, b.dtype) for b in both],
        input_output_aliases={i: i for i in range(2 * n)},
        compiler_params=pltpu.CompilerParams(has_side_effects=EFFECT),
        )(*both, s_sem, r_sem, after)
    return res[n:]


def _rs_chip_send(wires, small):
    n = len(wires)

    def body(*refs):
        ins, small_ref = refs[:n], refs[n]
        outs, gall_ref = refs[n + 1:2 * n + 1], refs[2 * n + 1]
        s_sem, r_sem, s_small, r_small, s_loc = refs[2 * n + 2:]
        x, y, c, chips = _place()
        me = 4 * x + 2 * y + c
        sends = []
        loc = pltpu.make_async_copy(small_ref, gall_ref.at[me], s_loc)
        loc.start()
        for i in range(n):
            for j, (px, py) in enumerate(chips):
                cp = _remote(ins[i].at[2 * px + py], outs[i].at[j], s_sem.at[i, j], r_sem.at[i, j], (px, py, c))
                cp.start()
                sends.append(cp)
        for k in range(1, N_DEV):
            peer = (x ^ (k >> 2), y ^ ((k >> 1) & 1), c ^ (k & 1))
            cp = _remote(small_ref, gall_ref.at[me], s_small.at[k - 1], r_small.at[k - 1], peer)
            cp.start()
            sends.append(cp)
        for cp in sends:
            cp.wait()
        loc.wait()

    out_shape = [_sds((3,) + w.shape[1:], w.dtype) for w in wires] + [_sds((N_DEV,) + small.shape, small.dtype)]
    dma = pltpu.SemaphoreType.DMA
    return pl.pallas_call(
        body, name="rs_chip_send", in_specs=[ANY] * (n + 1), out_specs=[ANY] * (n + 1), out_shape=out_shape,
        scratch_shapes=[dma((n, 3)), dma((n, 3)), dma((N_DEV - 1,)), dma((N_DEV - 1,)), dma(())],
        )(*wires, small)


def _rs_chip_add(own, recv, place, l, L, prev, tag):
    r, col = own.shape
    tr = _tile(r, ROW_TILE)

    def body(place_ref, o_ref, r_ref, *rest):
        acc = o_ref[...]
        for j in range(3):
            acc = acc + r_ref[j].astype(F32)
        rest[-1][...] = acc

    in_specs = [pl.BlockSpec((tr, col), lambda i, p: (i, 0)), pl.BlockSpec((3, tr, col), lambda i, p: (0, i, 0))]
    args = [place, own, recv]
    kw = {}
    if prev is not None:
        in_specs.append(ANY)
        args.append(prev)
        kw["input_output_aliases"] = {3: 0}
    grid_spec = pltpu.PrefetchScalarGridSpec(
        num_scalar_prefetch=1, grid=(r // tr,), in_specs=in_specs,
        out_specs=pl.BlockSpec((None, None, tr, col), lambda i, p: (l, p[0], i, 0)))
    return pl.pallas_call(
        body, name=f"rs_chip_add_{tag}", grid_spec=grid_spec, out_shape=_sds((L, 2, r, col), F32),
        compiler_params=_params(), **kw)(*args)


def _rs_pair_share(fulls):
    n = len(fulls)

    def body(*refs):
        outs = refs[n:2 * n]
        s_sem, r_sem = refs[2 * n:]
        x, y, c, _ = _place()
        sib = (x, y, 1 - c)
        started = []
        for i in range(n):
            cp = _remote(outs[i].at[:, c], outs[i].at[:, c], s_sem.at[i], r_sem.at[i], sib)
            cp.start()
            started.append(cp)
        for i, cp in enumerate(started):
            cp.wait_send()
            _remote(outs[i].at[:, 1 - c], outs[i].at[:, 1 - c], s_sem.at[i], r_sem.at[i], sib).wait_recv()

    dma = pltpu.SemaphoreType.DMA
    return pl.pallas_call(
        body, name="rs_pair_share", in_specs=[ANY] * n, out_specs=[ANY] * n,
        out_shape=[_sds(f.shape, f.dtype) for f in fulls],
        input_output_aliases={i: i for i in range(n)},
        scratch_shapes=[dma((n,)), dma((n,))],
        )(*fulls)


def _ffn_fwd(x, g_pre, g_post, w_in, w_out, tag):
    xn, hgu, act = _norm_mm_in(x, g_pre, w_in, tag, swiglu=True)
    h, x_out = _mm_out_post(act, w_out.reshape(-1, w_out.shape[-1]), x, g_post, 0.5, tag)
    return x_out, (x, xn, hgu, act, h)


def _ffn_bwd(dx, saved, g_pre, g_post, w_in, w_out, tag):
    x, xn, hgu, act, h = saved
    dh, dg_post, dhgu = _post_bwd_mm(dx, h, g_post, 0.5, w_out.reshape(-1, w_out.shape[-1]), tag, hgu=hgu)
    dw_out = _mm_tn_out(act, dh, tag)
    dw_in = _mm_tn_in(xn, dhgu, tag)
    dx_in, dg_pre = _mm_nt_pre(dhgu, w_in, dx, x, g_pre, tag)
    return dx_in, dg_pre, dg_post, dw_in, dw_out


def kernel(x, ffn1_pre_g, ffn1_post_g, ffn1_w_in, ffn1_w_out, mix_pre_g, mix_post_g, ffn2_pre_g, ffn2_post_g, ffn2_w_in, ffn2_w_out, conv_w_in, conv_k, conv_w_out, kv_g, kv_w, forget_b, attn_w_qg, attn_w_o, loss_target, m_ffn1_pre_g, m_ffn1_post_g, m_ffn1_w_in, m_ffn1_w_out, m_mix_pre_g, m_mix_post_g, m_ffn2_pre_g, m_ffn2_post_g, m_ffn2_w_in, m_ffn2_w_out, m_conv_w_in, m_conv_k, m_conv_w_out, m_kv_g, m_kv_w, m_forget_b, m_attn_w_qg, m_attn_w_o, v_ffn1_pre_g, v_ffn1_post_g, v_ffn1_w_in, v_ffn1_w_out, v_mix_pre_g, v_mix_post_g, v_ffn2_pre_g, v_ffn2_post_g, v_ffn2_w_in, v_ffn2_w_out, v_conv_w_in, v_conv_k, v_conv_w_out, v_kv_g, v_kv_w, v_forget_b, v_attn_w_qg, v_attn_w_o):
    Bl, S, D = x.shape
    T = Bl * S
    H = forget_b.shape[0]
    assert D == H * HEAD_DIM and D % LANES == 0
    kvc = kv_w.shape[1]
    kvp = -(-kvc // LANES) * LANES
    kv_all = 2 * D + LANES
    dk_cols = conv_k.shape[2]
    chip = 2 * lax.axis_index("x") + lax.axis_index("y")
    core = lax.axis_index("c")

    given = dict(ffn1_w_in=(ffn1_w_in, m_ffn1_w_in, v_ffn1_w_in), ffn1_w_out=(ffn1_w_out, m_ffn1_w_out, v_ffn1_w_out),
                 ffn2_w_in=(ffn2_w_in, m_ffn2_w_in, v_ffn2_w_in), ffn2_w_out=(ffn2_w_out, m_ffn2_w_out, v_ffn2_w_out),
                 conv_w_in=(conv_w_in, m_conv_w_in, v_conv_w_in), conv_w_out=(conv_w_out, m_conv_w_out, v_conv_w_out),
                 kv_w=(kv_w, m_kv_w, v_kv_w), attn_w_qg=(attn_w_qg, m_attn_w_qg, v_attn_w_qg),
                 attn_w_o=(attn_w_o, m_attn_w_o, v_attn_w_o))
    shards = {k: w for k, (w, _, _) in given.items()}
    shards["kv_w"] = jnp.pad(kv_w, ((0, 0), (0, kvp - kvc)))[None]
    groups = [[("ffn1_w_in", 0), ("ffn1_w_out", 0)], [("conv_w_in", 0), ("conv_w_out", 0)],
              [("ffn2_w_in", 0), ("ffn2_w_out", 0)], [("kv_w", 0), ("ffn1_w_in", 1), ("ffn1_w_out", 1)],
              [("attn_w_qg", 0), ("attn_w_o", 0), ("ffn2_w_in", 1), ("ffn2_w_out", 1)]]
    first = groups[0] + groups[1] + groups[2]
    second = groups[3] + groups[4]
    place = jnp.stack([core, chip]).astype(jnp.int32)

    def slot(key, where):
        w = shards[key[0]]
        L, r, col = w.shape
        return _own_slot(w.reshape(L, 2, r // 2, col), key[1], MM_DTYPE, where, f"{key[0]}{key[1]}")

    def whole(g):
        return g.reshape(N_CHIP, -1, g.shape[-1])

    taps_slot = _own_slot(jnp.pad(conv_k[0], ((0, 13), (0, 0))).reshape(1, 2, 8, dk_cols), 0, F32, place, "conv_k")
    fb = jnp.pad(forget_b, (0, LANES - H)).reshape(1, LANES)
    s_0, r_0, fly_0, token = _gather_start([slot(key, place) for key in groups[0]] + [taps_slot], fb, "first")
    later = groups[1] + groups[2] + groups[3] + groups[4]
    s_1, r_1, fly_1, token = _gather_start([slot(key, place) for key in later], token, "rest")
    W = {}

    def arrive(g, after):
        if g == 0:
            sems, bufs, lo = (s_0, r_0), fly_0, 0
        else:
            lo = sum(len(groups[k]) for k in range(1, g))
            sems, bufs = (s_1, r_1), fly_1[lo:lo + len(groups[g])]
        got = _gather_land(*_gather_pass(*sems, bufs, lo, after, f"g{g}"), f"g{g}")
        W.update({key: whole(b) for key, b in zip(groups[g], got)})
        return got

    k_taps = arrive(0, token)[-1].reshape(N_CHIP, 16, dk_cols).transpose(1, 0, 2).reshape(16, D)[:8]

    x0 = x.reshape(T, D)
    x1, s_f1a = _ffn_fwd(x0, ffn1_pre_g[0], ffn1_post_g[0], W["ffn1_w_in", 0], W["ffn1_w_out", 0], "l0f1")
    arrive(1, x1)
    w_o_conv = W["conv_w_out", 0].reshape(D, D)
    xn_c, bch = _norm_mm_in(x1, mix_pre_g[0], W["conv_w_in", 0], "conv")
    z_c = _conv_fwd(bch, k_taps, Bl, S)
    m_c, x2 = _mm_out_post(z_c, w_o_conv, x1, mix_post_g[0], 1.0, "conv_out")
    arrive(2, x2)
    x3, s_f2a = _ffn_fwd(x2, ffn2_pre_g[0], ffn2_post_g[0], W["ffn2_w_in", 0], W["ffn2_w_out", 0], "l0f2")

    arrive(3, x3)
    kv_full = jnp.concatenate([W["kv_w", 0][s, :, :kvc] for s in range(N_CHIP)], axis=1)
    kv_full = jnp.pad(kv_full, ((0, 0), (0, kv_all - kv_full.shape[1])))
    xn_kv = _rms_fwd(x3, kv_g, "kv")
    kvact = _mm_nn(xn_kv, kv_full[:, :2 * D], MM_DTYPE, "kv")
    pf = _mm_nn(xn_kv, kv_full[:, 2 * D:], F32, "forget")
    cum = _forget_fwd(pf, fb, Bl, S)
    bq = min(S, ATT_BLOCK)
    c3 = cum.reshape(Bl, S, LANES)[:, :, :H].transpose(0, 2, 1)
    c_col = jnp.broadcast_to(c3[..., None], (Bl, H, S, LANES))
    c_row = c3.reshape(Bl, H, S // bq, 1, bq)

    x4, s_f1b = _ffn_fwd(x3, ffn1_pre_g[1], ffn1_post_g[1], W["ffn1_w_in", 1], W["ffn1_w_out", 1], "l1f1")
    arrive(4, x4)
    w_o_attn = W["attn_w_o", 0].reshape(D, D)
    xn_a, qg = _norm_mm_in(x4, mix_pre_g[1], W["attn_w_qg", 0], "qg")
    o, lse = _attn_fwd(qg, kvact, c_col, c_row, Bl, S, D)
    z_a = _gate_fwd(qg, o)
    m_a, x5 = _mm_out_post(z_a, w_o_attn, x4, mix_post_g[1], 1.0, "attn_out")
    x6, s_f2b = _ffn_fwd(x5, ffn2_pre_g[1], ffn2_post_g[1], W["ffn2_w_in", 1], W["ffn2_w_out", 1], "l1f2")

    dy, loss_local = _loss_grad(x6, loss_target.reshape(T, D))
    loss = lax.psum(loss_local, ("x", "y", "c"))

    G = {}
    dx5, dg_f2pre_1, dg_f2post_1, G["ffn2_w_in", 1], G["ffn2_w_out", 1] = _ffn_bwd(
        dy, s_f2b, ffn2_pre_g[1], ffn2_post_g[1], W["ffn2_w_in", 1], W["ffn2_w_out", 1], "l1f2")
    dm_a, dg_mixpost_1, dz_a = _post_bwd_mm(dx5, m_a, mix_post_g[1], 1.0, w_o_attn, "attn_out")
    G["attn_w_o", 0] = _mm_tn_out(z_a, dm_a, "attn_out")
    do = _gate_do(dz_a, qg)
    dq, dk, dv, dcr = _attn_bwd(qg, kvact, do, lse, c_col, c_row, Bl, S, D)
    dqg = _gate_bwd(dz_a, qg, o, dq)
    G["attn_w_qg", 0] = _mm_tn_in(xn_a, dqg, "qg")
    dx4, dg_mixpre_1 = _mm_nt_pre(dqg, W["attn_w_qg", 0], dx5, x4, mix_pre_g[1], "qg")
    dx3, dg_f1pre_1, dg_f1post_1, G["ffn1_w_in", 1], G["ffn1_w_out", 1] = _ffn_bwd(
        dx4, s_f1b, ffn1_pre_g[1], ffn1_post_g[1], W["ffn1_w_in", 1], W["ffn1_w_out", 1], "l1f1")

    dcum = jnp.pad(dcr.reshape(Bl, H, S).transpose(0, 2, 1), ((0, 0), (0, 0), (0, LANES - H))).reshape(T, LANES)
    dpf, dfb = _forget_bwd(dcum, pf, fb, Bl, S)
    dp = jnp.concatenate([dk.astype(MM_DTYPE), dv.astype(MM_DTYPE), dpf], axis=1)
    G_kv_full = _mm_tn(xn_kv, dp, "kv")
    G["kv_w", 0] = jnp.stack([jnp.pad(G_kv_full[:, s * kvc:(s + 1) * kvc], ((0, 0), (0, kvp - kvc))) for s in range(N_CHIP)])
    dx3, dg_kv = _mm_nt_pre(dp, kv_full, dx3, x3, kv_g, "kv")

    def pair_sums(keys):
        grads = [G[k].reshape(N_CHIP, 2, G[k].shape[1] // 2, G[k].shape[2]) for k in keys]
        wires, owns = [], []
        for k, g, r in zip(keys, grads, _rs_pair_send(grads)):
            w, own = _rs_pair_add(g, r, place, f"{k[0]}{k[1]}")
            wires.append(w)
            owns.append(own)
        return wires, owns

    wires_2, owns_2 = pair_sums(second)
    r_sems, r_semr, wires_2, lands_2, token = _rs_start(wires_2)

    dx2, dg_f2pre_0, dg_f2post_0, G["ffn2_w_in", 0], G["ffn2_w_out", 0] = _ffn_bwd(
        dx3, s_f2a, ffn2_pre_g[0], ffn2_post_g[0] + token[0, :1], W["ffn2_w_in", 0], W["ffn2_w_out", 0], "l0f2")
    dm_c, dg_mixpost_0, dz_c = _post_bwd_mm(dx2, m_c, mix_post_g[0], 1.0, w_o_conv, "conv_out")
    G["conv_w_out", 0] = _mm_tn_out(z_c, dm_c, "conv_out")
    db, dcg, dhh, dk_taps = _conv_bwd(bch, dz_c, k_taps, Bl, S)
    dbch = jnp.concatenate([db, dcg, dhh], axis=1)
    G["conv_w_in", 0] = _mm_tn_in(xn_c, dbch, "conv")
    dx1, dg_mixpre_0 = _mm_nt_pre(dbch, W["conv_w_in", 0], dx2, x1, mix_pre_g[0], "conv")
    dx0, dg_f1pre_0, dg_f1post_0, G["ffn1_w_in", 0], G["ffn1_w_out", 0] = _ffn_bwd(
        dx1, s_f1a, ffn1_pre_g[0], ffn1_post_g[0], W["ffn1_w_in", 0], W["ffn1_w_out", 0], "l0f1")
    grad_x = dx0.reshape(Bl, S, D)

    recvs_2 = _rs_wait(r_sems, r_semr, wires_2, lands_2, dx0)
    wires_1, owns_1 = pair_sums(first)

    def row(v):
        return jnp.pad(v.reshape(-1), (0, D - v.size)).reshape(1, D)

    small_parts = [dg_f1pre_0, dg_f1pre_1, dg_f1post_0, dg_f1post_1, dg_mixpre_0, dg_mixpre_1, dg_mixpost_0, dg_mixpost_1,
                   dg_f2pre_0, dg_f2pre_1, dg_f2post_0, dg_f2post_1, dg_kv, row(dfb[0, :H]), dk_taps[:3]]
    small = jnp.concatenate(small_parts, axis=0)
    small = jnp.pad(small, ((0, SMALL_ROWS - small.shape[0]), (0, 0)))
    outs = _rs_chip_send(wires_1, small)
    recvs_1, gall = outs[:-1], outs[-1]
    partial = {}
    for key, own, rcv in zip(first + second, owns_1 + owns_2, list(recvs_1) + list(recvs_2)):
        name, l = key
        partial[name] = _rs_chip_add(own, rcv, place, l, shards[name].shape[0], partial.get(name), f"{name}{l}")
    names = list(partial)
    reduced = _rs_pair_share([partial[k] for k in names])
    gsum = _sum_devices(gall)

    res = {}
    for k, red in zip(names, reduced):
        w, m, v = given[k]
        g2 = red.reshape(-1, red.shape[-1])
        if k == "kv_w":
            g2 = g2[:, :kvc]
        flat = lambda a: a.reshape(-1, a.shape[-1])
        d, mn, vn = _adamw(flat(w), g2, flat(m), flat(v), k)
        res[k] = tuple(a.reshape(w.shape) for a in (g2, d, mn, vn))

    small_names = ["ffn1_pre_g", "ffn1_post_g", "mix_pre_g", "mix_post_g", "ffn2_pre_g", "ffn2_post_g"]
    small_given = dict(ffn1_pre_g=(ffn1_pre_g, m_ffn1_pre_g, v_ffn1_pre_g), ffn1_post_g=(ffn1_post_g, m_ffn1_post_g, v_ffn1_post_g),
                       mix_pre_g=(mix_pre_g, m_mix_pre_g, v_mix_pre_g), mix_post_g=(mix_post_g, m_mix_post_g, v_mix_post_g),
                       ffn2_pre_g=(ffn2_pre_g, m_ffn2_pre_g, v_ffn2_pre_g), ffn2_post_g=(ffn2_post_g, m_ffn2_post_g, v_ffn2_post_g))

    def pack(idx):
        rows_ = [small_given[k][idx] for k in small_names]
        rows_ += [row((kv_g, m_kv_g, v_kv_g)[idx]), row((forget_b, m_forget_b, v_forget_b)[idx])]
        rows_.append(jnp.pad((conv_k, m_conv_k, v_conv_k)[idx][0], ((0, 0), (0, D - dk_cols))))
        a = jnp.concatenate(rows_, axis=0)
        return jnp.pad(a, ((0, SMALL_ROWS - a.shape[0]), (0, 0)))

    g_taps = lax.dynamic_slice_in_dim(gsum[14:17], chip * dk_cols, dk_cols, axis=1)
    g_small = jnp.concatenate([gsum[:14], jnp.pad(g_taps, ((0, 0), (0, D - dk_cols))), gsum[17:]], axis=0)
    d_s, m_s, v_s = _adamw(pack(0), g_small, pack(1), pack(2), "small")
    for i, k in enumerate(small_names):
        res[k] = tuple(a[2 * i:2 * i + 2] for a in (g_small, d_s, m_s, v_s))
    res["kv_g"] = tuple(a[12] for a in (g_small, d_s, m_s, v_s))
    res["forget_b"] = tuple(a[13, :H] for a in (g_small, d_s, m_s, v_s))
    res["conv_k"] = tuple(a[14:17, :dk_cols][None] for a in (g_small, d_s, m_s, v_s))

    order = ["ffn1_pre_g", "ffn1_post_g", "ffn1_w_in", "ffn1_w_out", "mix_pre_g", "mix_post_g", "ffn2_pre_g", "ffn2_post_g",
             "ffn2_w_in", "ffn2_w_out", "conv_w_in", "conv_k", "conv_w_out", "kv_g", "kv_w", "forget_b", "attn_w_qg", "attn_w_o"]
    out = [loss, grad_x]
    for idx in range(4):
        out += [res[k][idx] for k in order]
    return tuple(out)
```

```python
import functools
import math

import jax
import jax.numpy as jnp
from jax import lax
from jax.experimental import pallas as pl
from jax.experimental.pallas import tpu as pltpu

F32 = jnp.float32
MM_DTYPE = jnp.bfloat16
WIRE_DTYPE = jnp.bfloat16

RMS_EPS = 1e-6
ADAM_LR = 0.001
ADAM_B1 = 0.9
ADAM_B2 = 0.999
ADAM_EPS = 1e-08
ADAM_WD = 0.01
ADAM_STEP = 10

HEAD_DIM = 64
LANES = 128
N_CHIP = 4
N_DEV = 8
ROW_TILE = 256
MM_TILE = 512
TN_TILE = 2048
ATT_BLOCK = 512
SMALL_ROWS = 24
VMEM_LIMIT = 56 * 1024 * 1024
MESH = pl.DeviceIdType.MESH
ANY = pl.BlockSpec(memory_space=pl.ANY)

NT = (((1,), (1,)), ((), ()))
TN = (((0,), (0,)), ((), ()))


def _tile(n, pref):
    if n <= pref:
        return n
    t = pref - pref % 16
    while n % t:
        t -= 16
    return t


def _params():
    return pltpu.CompilerParams(vmem_limit_bytes=VMEM_LIMIT)


def _sds(shape, dtype):
    return jax.ShapeDtypeStruct(shape, dtype)


def _rows(tm, c):
    return pl.BlockSpec((tm, c), lambda i: (i, 0))


def _whole(shape):
    return pl.BlockSpec(shape, lambda *_: (0,) * len(shape))


def _rms_fwd(x, g, tag):
    T, D = x.shape
    tm = _tile(T, ROW_TILE)

    def body(x_ref, g_ref, o_ref):
        xv = x_ref[...]
        r = lax.rsqrt(jnp.mean(xv * xv, axis=-1, keepdims=True) + RMS_EPS)
        o_ref[...] = (xv * r * g_ref[...]).astype(o_ref.dtype)

    return pl.pallas_call(
        body, name=f"rms_fwd_{tag}", grid=(T // tm,),
        in_specs=[_rows(tm, D), _whole((1, D))], out_specs=_rows(tm, D),
        out_shape=_sds((T, D), MM_DTYPE), compiler_params=_params())(x, g.reshape(1, D))


def _post_fwd(x, h, g, alpha, tag):
    T, D = x.shape
    tm = _tile(T, ROW_TILE)

    def body(x_ref, h_ref, g_ref, o_ref):
        hv = h_ref[...]
        r = lax.rsqrt(jnp.mean(hv * hv, axis=-1, keepdims=True) + RMS_EPS)
        o_ref[...] = x_ref[...] + alpha * (hv * r * g_ref[...])

    return pl.pallas_call(
        body, name=f"post_fwd_{tag}", grid=(T // tm,),
        in_specs=[_rows(tm, D), _rows(tm, D), _whole((1, D))], out_specs=_rows(tm, D),
        out_shape=_sds((T, D), F32), compiler_params=_params())(x, h, g.reshape(1, D))


def _accumulate(ref, part, first):
    @pl.when(first)
    def _():
        ref[...] = part

    @pl.when(jnp.logical_not(first))
    def _():
        ref[...] += part


def _post_bwd(dx, h, g, alpha, tag):
    T, D = dx.shape
    tm = _tile(T, ROW_TILE)

    def body(dx_ref, h_ref, g_ref, dh_ref, dg_ref):
        hv = h_ref[...]
        r = lax.rsqrt(jnp.mean(hv * hv, axis=-1, keepdims=True) + RMS_EPS)
        hh = hv * r
        dyn = alpha * dx_ref[...]
        _accumulate(dg_ref, jnp.sum(dyn * hh, axis=0, keepdims=True), pl.program_id(0) == 0)
        dhh = dyn * g_ref[...]
        dh = r * (dhh - hh * jnp.mean(dhh * hh, axis=-1, keepdims=True))
        dh_ref[...] = dh.astype(dh_ref.dtype)

    return pl.pallas_call(
        body, name=f"post_bwd_{tag}", grid=(T // tm,),
        in_specs=[_rows(tm, D), _rows(tm, D), _whole((1, D))],
        out_specs=[_rows(tm, D), _whole((1, D))],
        out_shape=[_sds((T, D), MM_DTYPE), _sds((1, D), F32)],
        compiler_params=_params())(dx, h, g.reshape(1, D))


def _pre_bwd(dres, dxn, x, g, tag):
    T, D = x.shape
    tm = _tile(T, ROW_TILE)

    def body(dres_ref, dxn_ref, x_ref, g_ref, dx_ref, dg_ref):
        xv = x_ref[...]
        r = lax.rsqrt(jnp.mean(xv * xv, axis=-1, keepdims=True) + RMS_EPS)
        xh = xv * r
        dn = dxn_ref[...]
        _accumulate(dg_ref, jnp.sum(dn * xh, axis=0, keepdims=True), pl.program_id(0) == 0)
        dxh = dn * g_ref[...]
        dx_ref[...] = dres_ref[...] + r * (dxh - xh * jnp.mean(dxh * xh, axis=-1, keepdims=True))

    return pl.pallas_call(
        body, name=f"pre_bwd_{tag}", grid=(T // tm,),
        in_specs=[_rows(tm, D), _rows(tm, D), _rows(tm, D), _whole((1, D))],
        out_specs=[_rows(tm, D), _whole((1, D))],
        out_shape=[_sds((T, D), F32), _sds((1, D), F32)],
        compiler_params=_params())(dres, dxn, x, g.reshape(1, D))


def _swiglu_fwd(hgu, tag):
    T, F2 = hgu.shape
    F = F2 // 2
    tm = _tile(T, ROW_TILE)

    def body(g_ref, u_ref, o_ref):
        g = g_ref[...].astype(F32)
        o_ref[...] = (g * jax.nn.sigmoid(g) * u_ref[...].astype(F32)).astype(o_ref.dtype)

    return pl.pallas_call(
        body, name=f"swiglu_fwd_{tag}", grid=(T // tm,),
        in_specs=[pl.BlockSpec((tm, F), lambda i: (i, 0)), pl.BlockSpec((tm, F), lambda i: (i, 1))],
        out_specs=_rows(tm, F), out_shape=_sds((T, F), MM_DTYPE), compiler_params=_params())(hgu, hgu)


def _swiglu_bwd(hgu, da, tag):
    T, F2 = hgu.shape
    F = F2 // 2
    tm = _tile(T, ROW_TILE)

    def body(h_ref, da_ref, o_ref):
        g = h_ref[:, :F].astype(F32)
        u = h_ref[:, F:].astype(F32)
        d = da_ref[...].astype(F32)
        sg = jax.nn.sigmoid(g)
        o_ref[:, :F] = (d * u * sg * (1.0 + g * (1.0 - sg))).astype(o_ref.dtype)
        o_ref[:, F:] = (d * g * sg).astype(o_ref.dtype)

    return pl.pallas_call(
        body, name=f"swiglu_bwd_{tag}", grid=(T // tm,),
        in_specs=[_rows(tm, F2), _rows(tm, F)], out_specs=_rows(tm, F2),
        out_shape=_sds((T, F2), MM_DTYPE), compiler_params=_params())(hgu, da)


def _loss_grad(y, tgt):
    T, D = y.shape
    tm = _tile(T, ROW_TILE)

    def body(y_ref, t_ref, dy_ref, l_ref):
        e = y_ref[...] - t_ref[...]
        row = jnp.mean(e * e, axis=-1, keepdims=True)
        part = jnp.broadcast_to(jnp.sum(row, axis=0, keepdims=True), (8, LANES))
        _accumulate(l_ref, part, pl.program_id(0) == 0)
        dy_ref[...] = e * (1.0 / D)

    dy, lsum = pl.pallas_call(
        body, name="loss_grad", grid=(T // tm,),
        in_specs=[_rows(tm, D), _rows(tm, D)], out_specs=[_rows(tm, D), _whole((8, LANES))],
        out_shape=[_sds((T, D), F32), _sds((8, LANES), F32)], compiler_params=_params())(y, tgt)
    return dy, 0.5 * lsum[0, 0]


def _shift_down(u, d, rows):
    return jnp.where(rows >= d, pltpu.roll(u, d, 0), 0.0)


def _shift_up(u, d, rows, S):
    return jnp.where(rows < S - d, pltpu.roll(u, S - d, 0), 0.0)


def _conv_fwd(bch, k8, Bl, S):
    T, D3 = bch.shape
    D = D3 // 3
    dc = min(D, 2 * LANES)
    nd = D // dc

    def body(b_ref, c_ref, h_ref, k_ref, z_ref):
        rows = lax.broadcasted_iota(jnp.int32, (S, 1), 0)
        u = c_ref[...].astype(F32) * h_ref[...].astype(F32)
        y = k_ref[2:3, :] * u + k_ref[1:2, :] * _shift_down(u, 1, rows) + k_ref[0:1, :] * _shift_down(u, 2, rows)
        z_ref[...] = (b_ref[...].astype(F32) * y).astype(z_ref.dtype)

    return pl.pallas_call(
        body, name="conv_fwd", grid=(Bl, nd),
        in_specs=[pl.BlockSpec((S, dc), lambda b, j: (b, j)),
                  pl.BlockSpec((S, dc), lambda b, j: (b, nd + j)),
                  pl.BlockSpec((S, dc), lambda b, j: (b, 2 * nd + j)),
                  pl.BlockSpec((8, dc), lambda b, j: (0, j))],
        out_specs=pl.BlockSpec((S, dc), lambda b, j: (b, j)),
        out_shape=_sds((T, D), MM_DTYPE), compiler_params=_params())(bch, bch, bch, k8)


def _conv_bwd(bch, dz, k8, Bl, S):
    T, D3 = bch.shape
    D = D3 // 3
    dc = min(D, 2 * LANES)
    nd = D // dc

    def body(b_ref, c_ref, h_ref, dz_ref, k_ref, db_ref, dc_ref, dh_ref, dk_ref):
        rows = lax.broadcasted_iota(jnp.int32, (S, 1), 0)
        bv = b_ref[...].astype(F32)
        cv = c_ref[...].astype(F32)
        hv = h_ref[...].astype(F32)
        dzv = dz_ref[...].astype(F32)
        u = cv * hv
        u1 = _shift_down(u, 1, rows)
        u2 = _shift_down(u, 2, rows)
        y = k_ref[2:3, :] * u + k_ref[1:2, :] * u1 + k_ref[0:1, :] * u2
        db_ref[...] = (dzv * y).astype(db_ref.dtype)
        dy = dzv * bv
        du = k_ref[2:3, :] * dy + k_ref[1:2, :] * _shift_up(dy, 1, rows, S) + k_ref[0:1, :] * _shift_up(dy, 2, rows, S)
        dc_ref[...] = (du * hv).astype(dc_ref.dtype)
        dh_ref[...] = (du * cv).astype(dh_ref.dtype)

        @pl.when(pl.program_id(1) == 0)
        def _():
            dk_ref[...] = jnp.zeros_like(dk_ref)

        dk_ref[0:1, :] += jnp.sum(dy * u2, axis=0, keepdims=True)
        dk_ref[1:2, :] += jnp.sum(dy * u1, axis=0, keepdims=True)
        dk_ref[2:3, :] += jnp.sum(dy * u, axis=0, keepdims=True)

    seq = lambda off: pl.BlockSpec((S, dc), lambda j, b: (b, off + j))
    return pl.pallas_call(
        body, name="conv_bwd", grid=(nd, Bl),
        in_specs=[seq(0), seq(nd), seq(2 * nd), seq(0), pl.BlockSpec((8, dc), lambda j, b: (0, j))],
        out_specs=[seq(0), seq(0), seq(0), pl.BlockSpec((8, dc), lambda j, b: (0, j))],
        out_shape=[_sds((T, D), MM_DTYPE)] * 3 + [_sds((8, D), F32)],
        compiler_params=_params())(bch, bch, bch, dz, k8)


def _forget_fwd(pf, fb, Bl, S):
    T = pf.shape[0]

    def body(p_ref, fb_ref, c_ref):
        rows = lax.broadcasted_iota(jnp.int32, (S, 1), 0)
        z = p_ref[...] + fb_ref[...]
        acc = jnp.minimum(z, 0.0) - jnp.log1p(jnp.exp(-jnp.abs(z)))
        d = 1
        while d < S:
            acc = acc + _shift_down(acc, d, rows)
            d *= 2
        c_ref[...] = acc

    return pl.pallas_call(
        body, name="forget_fwd", grid=(Bl,),
        in_specs=[_rows(S, LANES), _whole((1, LANES))], out_specs=_rows(S, LANES),
        out_shape=_sds((T, LANES), F32), compiler_params=_params())(pf, fb)


def _forget_bwd(dc, pf, fb, Bl, S):
    T = pf.shape[0]

    def body(dc_ref, p_ref, fb_ref, df_ref, dfb_ref):
        rows = lax.broadcasted_iota(jnp.int32, (S, 1), 0)
        acc = dc_ref[...]
        d = 1
        while d < S:
            acc = acc + _shift_up(acc, d, rows, S)
            d *= 2
        df = acc * jax.nn.sigmoid(-(p_ref[...] + fb_ref[...]))
        df_ref[...] = df.astype(df_ref.dtype)
        _accumulate(dfb_ref, jnp.sum(df, axis=0, keepdims=True), pl.program_id(0) == 0)

    return pl.pallas_call(
        body, name="forget_bwd", grid=(Bl,),
        in_specs=[_rows(S, LANES), _rows(S, LANES), _whole((1, LANES))],
        out_specs=[_rows(S, LANES), _whole((1, LANES))],
        out_shape=[_sds((T, LANES), MM_DTYPE), _sds((1, LANES), F32)],
        compiler_params=_params())(dc, pf, fb)


def _head_mask(h):
    lane = lax.broadcasted_iota(jnp.int32, (1, LANES), 1)
    return (lane >= h * HEAD_DIM) & (lane < (h + 1) * HEAD_DIM)


def _attn_fwd(qg, kv, c_col, c_row, Bl, S, D):
    T = Bl * S
    H = D // HEAD_DIM
    HP = D // LANES
    bq = min(S, ATT_BLOCK)
    nq = S // bq
    scale = 1.0 / math.sqrt(HEAD_DIM)

    def body(q_ref, k_ref, v_ref, cc_ref, cr_ref, o_ref, lse_ref):
        i = pl.program_id(2)
        q2 = q_ref[...]
        qh = [q2 * (_head_mask(h).astype(F32) * scale).astype(q2.dtype) for h in range(2)]
        cc = [cc_ref[h][:, :1] for h in range(2)]
        diag = lax.broadcasted_iota(jnp.int32, (1, bq), 1) <= lax.broadcasted_iota(jnp.int32, (bq, 1), 0)

        def block(j, carry, on_diagonal):
            off = pl.multiple_of(j * bq, bq)
            kj = k_ref[pl.ds(off, bq), :]
            vj = v_ref[pl.ds(off, bq), :]
            new = []
            for h in range(2):
                m, l, acc = carry[h]
                s = lax.dot_general(qh[h], kj, NT, preferred_element_type=F32) + cc[h] - cr_ref[h, j]
                if on_diagonal:
                    s = jnp.where(diag, s, -jnp.inf)
                m_new = jnp.maximum(m, jnp.max(s, axis=1, keepdims=True))
                p = jnp.exp(s - m_new)
                a = jnp.exp(m - m_new)
                l = a * l + jnp.sum(p, axis=1, keepdims=True)
                acc = a * acc + jnp.dot(p.astype(MM_DTYPE), vj, preferred_element_type=F32)
                new.append((m_new, l, acc))
            return tuple(new)

        one = (jnp.full((bq, 1), -jnp.inf, F32), jnp.zeros((bq, 1), F32), jnp.zeros((bq, LANES), F32))
        carry = lax.fori_loop(0, i, lambda j, c: block(j, c, False), (one, one))
        carry = block(i, carry, True)
        outs = []
        for h in range(2):
            m, l, acc = carry[h]
            outs.append(acc / l)
            lse_ref[h] = jnp.broadcast_to(m + jnp.log(l), (bq, LANES))
        o_ref[...] = jnp.where(_head_mask(0), outs[0], outs[1])

    return pl.pallas_call(
        body, name="attn_fwd", grid=(Bl, HP, nq),
        in_specs=[pl.BlockSpec((bq, LANES), lambda b, hp, i: (b * nq + i, hp)),
                  pl.BlockSpec((S, LANES), lambda b, hp, i: (b, hp)),
                  pl.BlockSpec((S, LANES), lambda b, hp, i: (b, HP + hp)),
                  pl.BlockSpec((None, 2, bq, LANES), lambda b, hp, i: (b, hp, i, 0)),
                  pl.BlockSpec((None, 2, nq, 1, bq), lambda b, hp, i: (b, hp, 0, 0, 0))],
        out_specs=[pl.BlockSpec((bq, LANES), lambda b, hp, i: (b * nq + i, hp)),
                   pl.BlockSpec((None, 2, bq, LANES), lambda b, hp, i: (b, hp, i, 0))],
        out_shape=[_sds((T, D), F32), _sds((Bl, H, S, LANES), F32)],
        compiler_params=_params())(qg, kv, kv, c_col, c_row)


def _attn_bwd(qg, kv, do, lse, c_col, c_row, Bl, S, D):
    T = Bl * S
    H = D // HEAD_DIM
    HP = D // LANES
    bq = min(S, ATT_BLOCK)
    nq = S // bq
    scale = 1.0 / math.sqrt(HEAD_DIM)

    def body(q_ref, k_ref, v_ref, do_ref, lse_ref, cc_ref, cr_ref, dq_ref, dk_ref, dv_ref, dcr_ref, p_sc, dp_sc):
        i = pl.program_id(2)

        @pl.when(i == 0)
        def _():
            dk_ref[...] = jnp.zeros_like(dk_ref)
            dv_ref[...] = jnp.zeros_like(dv_ref)
            dcr_ref[...] = jnp.zeros_like(dcr_ref)

        q2 = q_ref[...]
        do2 = do_ref[...]
        masks = [_head_mask(h).astype(F32) for h in range(2)]
        qh = [q2 * (masks[h] * scale).astype(q2.dtype) for h in range(2)]
        doh = [do2 * masks[h].astype(do2.dtype) for h in range(2)]
        cc = [cc_ref[h][:, :1] for h in range(2)]
        lse = [lse_ref[h][:, :1] for h in range(2)]
        diag = lax.broadcasted_iota(jnp.int32, (1, bq), 1) <= lax.broadcasted_iota(jnp.int32, (bq, 1), 0)

        def sweep1(j, delta, on_diagonal):
            off = pl.multiple_of(j * bq, bq)
            kj = k_ref[pl.ds(off, bq), :]
            vj = v_ref[pl.ds(off, bq), :]
            new = []
            dv = None
            for h in range(2):
                s = lax.dot_general(qh[h], kj, NT, preferred_element_type=F32) + cc[h] - cr_ref[h, j]
                if on_diagonal:
                    s = jnp.where(diag, s, -jnp.inf)
                p = jnp.exp(s - lse[h])
                dp = lax.dot_general(doh[h], vj, NT, preferred_element_type=F32)
                p_sc[h, j] = p
                dp_sc[h, j] = dp
                part = lax.dot_general(p.astype(MM_DTYPE), doh[h], TN, preferred_element_type=F32)
                dv = part if dv is None else dv + part
                new.append(delta[h] + jnp.sum(p * dp, axis=1, keepdims=True))
            dv_ref[pl.ds(off, bq), :] += dv
            return tuple(new)

        zero = jnp.zeros((bq, 1), F32)
        delta = lax.fori_loop(0, i, lambda j, d: sweep1(j, d, False), (zero, zero))
        delta = sweep1(i, delta, True)

        def sweep2(j, dq):
            off = pl.multiple_of(j * bq, bq)
            kj = k_ref[pl.ds(off, bq), :]
            dk = None
            for h in range(2):
                ds = p_sc[h, j] * (dp_sc[h, j] - delta[h])
                dcr_ref[h, j] -= jnp.sum(ds, axis=0, keepdims=True)
                dsb = ds.astype(MM_DTYPE)
                dq = dq + jnp.dot(dsb, kj * (masks[h] * scale).astype(kj.dtype), preferred_element_type=F32)
                part = lax.dot_general(dsb, qh[h], TN, preferred_element_type=F32)
                dk = part if dk is None else dk + part
            dk_ref[pl.ds(off, bq), :] += dk
            return dq

        dq_ref[...] = lax.fori_loop(0, i + 1, sweep2, jnp.zeros((bq, LANES), F32))

    blk = lambda col: pl.BlockSpec((bq, LANES), lambda b, hp, i: (b * nq + i, col(hp)))
    seq = lambda col: pl.BlockSpec((S, LANES), lambda b, hp, i: (b, col(hp)))
    per_head = pl.BlockSpec((None, 2, bq, LANES), lambda b, hp, i: (b, hp, i, 0))
    rows = pl.BlockSpec((None, 2, nq, 1, bq), lambda b, hp, i: (b, hp, 0, 0, 0))
    return pl.pallas_call(
        body, name="attn_bwd", grid=(Bl, HP, nq),
        in_specs=[blk(lambda hp: hp), seq(lambda hp: hp), seq(lambda hp: HP + hp), blk(lambda hp: hp),
                  per_head, per_head, rows],
        out_specs=[blk(lambda hp: hp), seq(lambda hp: hp), seq(lambda hp: hp), rows],
        out_shape=[_sds((T, D), F32), _sds((T, D), F32), _sds((T, D), F32), _sds((Bl, H, nq, 1, bq), F32)],
        scratch_shapes=[pltpu.VMEM((2, nq, bq, bq), F32), pltpu.VMEM((2, nq, bq, bq), F32)],
        compiler_params=_params())(qg, kv, kv, do, lse, c_col, c_row)


def _gate_fwd(qg, o):
    T, D = o.shape
    tm = _tile(T, ROW_TILE)

    def body(g_ref, o_ref, z_ref):
        z_ref[...] = (jax.nn.sigmoid(g_ref[...].astype(F32)) * o_ref[...]).astype(z_ref.dtype)

    return pl.pallas_call(
        body, name="gate_fwd", grid=(T // tm,),
        in_specs=[pl.BlockSpec((tm, D), lambda i: (i, 1)), _rows(tm, D)], out_specs=_rows(tm, D),
        out_shape=_sds((T, D), MM_DTYPE), compiler_params=_params())(qg, o)


def _gate_do(dz, qg):
    T, D = dz.shape
    tm = _tile(T, ROW_TILE)

    def body(dz_ref, g_ref, do_ref):
        do_ref[...] = (dz_ref[...].astype(F32) * jax.nn.sigmoid(g_ref[...].astype(F32))).astype(do_ref.dtype)

    return pl.pallas_call(
        body, name="gate_do", grid=(T // tm,),
        in_specs=[_rows(tm, D), pl.BlockSpec((tm, D), lambda i: (i, 1))], out_specs=_rows(tm, D),
        out_shape=_sds((T, D), MM_DTYPE), compiler_params=_params())(dz, qg)


def _gate_bwd(dz, qg, o, dq):
    T, D = dz.shape
    tm = _tile(T, ROW_TILE)

    def body(dz_ref, g_ref, o_ref, dq_ref, out_ref):
        g = g_ref[...].astype(F32)
        sg = jax.nn.sigmoid(g)
        out_ref[:, :D] = dq_ref[...].astype(out_ref.dtype)
        out_ref[:, D:] = (dz_ref[...].astype(F32) * o_ref[...] * sg * (1.0 - sg)).astype(out_ref.dtype)

    return pl.pallas_call(
        body, name="gate_bwd", grid=(T // tm,),
        in_specs=[_rows(tm, D), pl.BlockSpec((tm, D), lambda i: (i, 1)), _rows(tm, D), _rows(tm, D)],
        out_specs=_rows(tm, 2 * D), out_shape=_sds((T, 2 * D), MM_DTYPE),
        compiler_params=_params())(dz, qg, o, dq)


def _mm_in(a, wg, out_dtype, tag, l=None):
    T, K = a.shape
    n = wg.shape[-1]
    tm = _tile(T, MM_TILE)
    if l is None:
        w_spec = _whole((N_CHIP, K, n))
    else:
        w_spec = pl.BlockSpec((None, N_CHIP, K, n), lambda i: (l, 0, 0, 0))

    def body(a_ref, w_ref, o_ref):
        av = a_ref[...]
        for s in range(N_CHIP):
            o_ref[:, s * n:(s + 1) * n] = jnp.dot(av, w_ref[s], preferred_element_type=F32).astype(o_ref.dtype)

    return pl.pallas_call(
        body, name=f"mm_in_{tag}", grid=(T // tm,),
        in_specs=[_rows(tm, K), w_spec], out_specs=_rows(tm, N_CHIP * n),
        out_shape=_sds((T, N_CHIP * n), out_dtype), compiler_params=_params())(a, wg)


def _mm_nt_in(dy, wg, tag, l=None):
    T = dy.shape[0]
    K, n = wg.shape[-2:]
    tm = _tile(T, MM_TILE)
    if l is None:
        w_spec = _whole((N_CHIP, K, n))
    else:
        w_spec = pl.BlockSpec((None, N_CHIP, K, n), lambda i: (l, 0, 0, 0))

    def body(d_ref, w_ref, o_ref):
        acc = None
        for s in range(N_CHIP):
            part = lax.dot_general(d_ref[:, s * n:(s + 1) * n], w_ref[s], NT, preferred_element_type=F32)
            acc = part if acc is None else acc + part
        o_ref[...] = acc

    return pl.pallas_call(
        body, name=f"mm_nt_in_{tag}", grid=(T // tm,),
        in_specs=[_rows(tm, N_CHIP * n), w_spec], out_specs=_rows(tm, K),
        out_shape=_sds((T, K), F32), compiler_params=_params())(dy, wg)


def _mm_nn(a, b, out_dtype, tag):
    T, K = a.shape
    N = b.shape[1]
    tm = _tile(T, MM_TILE)

    def body(a_ref, b_ref, o_ref):
        o_ref[...] = jnp.dot(a_ref[...], b_ref[...], preferred_element_type=F32).astype(o_ref.dtype)

    return pl.pallas_call(
        body, name=f"mm_nn_{tag}", grid=(T // tm,),
        in_specs=[_rows(tm, K), _whole((K, N))], out_specs=_rows(tm, N),
        out_shape=_sds((T, N), out_dtype), compiler_params=_params())(a, b)


def _mm_nt(a, b, out_dtype, tag):
    T, C = a.shape
    N = b.shape[0]
    tm = _tile(T, MM_TILE)
    nb = N
    for cand in (1408, 1024):
        if N > cand and N % cand == 0:
            nb = cand
            break

    def body(a_ref, b_ref, o_ref):
        o_ref[...] = lax.dot_general(a_ref[...], b_ref[...], NT, preferred_element_type=F32).astype(o_ref.dtype)

    return pl.pallas_call(
        body, name=f"mm_nt_{tag}", grid=(N // nb, T // tm),
        in_specs=[pl.BlockSpec((tm, C), lambda j, i: (i, 0)), pl.BlockSpec((nb, C), lambda j, i: (j, 0))],
        out_specs=pl.BlockSpec((tm, nb), lambda j, i: (i, j)),
        out_shape=_sds((T, N), out_dtype), compiler_params=_params())(a, b)


def _mm_tn_in(a, dy, tag, l=None, prev=None):
    T, K = a.shape
    n = dy.shape[1] // N_CHIP
    tt = _tile(T, TN_TILE)

    def body(a_ref, d_ref, *rest):
        o_ref = rest[-1]
        part = lax.dot_general(a_ref[...], d_ref[...], TN, preferred_element_type=F32)
        _accumulate(o_ref, part, pl.program_id(1) == 0)

    in_specs = [pl.BlockSpec((tt, K), lambda s, t: (t, 0)), pl.BlockSpec((tt, n), lambda s, t: (t, s))]
    args = [a, dy]
    kw = {}
    if l is None:
        out_spec = pl.BlockSpec((None, K, n), lambda s, t: (s, 0, 0))
        out_shape = _sds((N_CHIP, K, n), F32)
    else:
        out_spec = pl.BlockSpec((None, None, K, n), lambda s, t: (l, s, 0, 0))
        out_shape = _sds((2, N_CHIP, K, n), F32)
        if prev is not None:
            in_specs.append(ANY)
            args.append(prev)
            kw["input_output_aliases"] = {2: 0}
    return pl.pallas_call(
        body, name=f"mm_tn_in_{tag}", grid=(N_CHIP, T // tt),
        in_specs=in_specs, out_specs=out_spec, out_shape=out_shape,
        compiler_params=_params(), **kw)(*args)


def _mm_tn_out(act, dh, tag, l=None, prev=None):
    T, R4 = act.shape
    D = dh.shape[1]
    r = R4 // N_CHIP
    g = 1 if r % LANES == 0 else 2
    tt = _tile(T, TN_TILE)

    def body(a_ref, d_ref, *rest):
        o_ref = rest[-1]
        part = lax.dot_general(a_ref[...], d_ref[...], TN, preferred_element_type=F32)
        first = pl.program_id(1) == 0
        for q in range(g):
            _accumulate(o_ref.at[q], part[q * r:(q + 1) * r], first)

    in_specs = [pl.BlockSpec((tt, g * r), lambda s, t: (t, s)), pl.BlockSpec((tt, D), lambda s, t: (t, 0))]
    args = [act, dh]
    kw = {}
    if l is None:
        out_spec = pl.BlockSpec((g, r, D), lambda s, t: (s, 0, 0))
        out_shape = _sds((N_CHIP, r, D), F32)
    else:
        out_spec = pl.BlockSpec((None, g, r, D), lambda s, t: (l, s, 0, 0))
        out_shape = _sds((2, N_CHIP, r, D), F32)
        if prev is not None:
            in_specs.append(ANY)
            args.append(prev)
            kw["input_output_aliases"] = {2: 0}
    return pl.pallas_call(
        body, name=f"mm_tn_out_{tag}", grid=(N_CHIP // g, T // tt),
        in_specs=in_specs, out_specs=out_spec, out_shape=out_shape,
        compiler_params=_params(), **kw)(*args)


def _mm_tn(a, b, tag):
    T, K = a.shape
    N = b.shape[1]
    tt = _tile(T, TN_TILE)

    def body(a_ref, b_ref, o_ref):
        part = lax.dot_general(a_ref[...], b_ref[...], TN, preferred_element_type=F32)
        _accumulate(o_ref, part, pl.program_id(0) == 0)

    return pl.pallas_call(
        body, name=f"mm_tn_{tag}", grid=(T // tt,),
        in_specs=[_rows(tt, K), _rows(tt, N)], out_specs=_whole((K, N)),
        out_shape=_sds((K, N), F32), compiler_params=_params())(a, b)


def _norm_mm_in(x, g, wg, tag, swiglu=False):
    T, D = x.shape
    n = wg.shape[-1]
    tm = _tile(T, ROW_TILE)
    half = N_CHIP // 2

    def body(x_ref, g_ref, w_ref, xn_ref, y_ref, *rest):
        xv = x_ref[...]
        r = lax.rsqrt(jnp.mean(xv * xv, axis=-1, keepdims=True) + RMS_EPS)
        xn = (xv * r * g_ref[...]).astype(xn_ref.dtype)
        xn_ref[...] = xn

        def product(s):
            p = jnp.dot(xn, w_ref[s], preferred_element_type=F32)
            y_ref[:, s * n:(s + 1) * n] = p.astype(y_ref.dtype)
            return p

        if swiglu:
            for q in range(half):
                gate, up = product(q), product(half + q)
                rest[0][:, q * n:(q + 1) * n] = (gate * jax.nn.sigmoid(gate) * up).astype(rest[0].dtype)
        else:
            for s in range(N_CHIP):
                product(s)

    out_specs = [_rows(tm, D), _rows(tm, N_CHIP * n)]
    out_shape = [_sds((T, D), MM_DTYPE), _sds((T, N_CHIP * n), MM_DTYPE)]
    if swiglu:
        out_specs.append(_rows(tm, half * n))
        out_shape.append(_sds((T, half * n), MM_DTYPE))
    return pl.pallas_call(
        body, name=f"norm_mm_in_{tag}", grid=(T // tm,),
        in_specs=[_rows(tm, D), _whole((1, D)), _whole((N_CHIP, D, n))], out_specs=out_specs,
        out_shape=out_shape, compiler_params=_params())(x, g.reshape(1, D), wg)


def _mm_out_post(a, b, x, g, alpha, tag):
    T, K = a.shape
    D = b.shape[1]
    tm = _tile(T, ROW_TILE)

    def body(a_ref, b_ref, x_ref, g_ref, h_ref, o_ref):
        hv = jnp.dot(a_ref[...], b_ref[...], preferred_element_type=F32)
        h_ref[...] = hv
        r = lax.rsqrt(jnp.mean(hv * hv, axis=-1, keepdims=True) + RMS_EPS)
        o_ref[...] = x_ref[...] + alpha * (hv * r * g_ref[...])

    return pl.pallas_call(
        body, name=f"mm_out_post_{tag}", grid=(T // tm,),
        in_specs=[_rows(tm, K), _whole((K, D)), _rows(tm, D), _whole((1, D))],
        out_specs=[_rows(tm, D), _rows(tm, D)], out_shape=[_sds((T, D), F32)] * 2,
        compiler_params=_params())(a, b, x, g.reshape(1, D))


def _post_bwd_mm(dx, h, g, alpha, b, tag, hgu=None):
    T, D = dx.shape
    K = b.shape[0]
    tm = _tile(T, ROW_TILE)

    def body(dx_ref, h_ref, g_ref, b_ref, *rest):
        dh_ref, dg_ref, out_ref = rest[-3:]
        hv = h_ref[...]
        r = lax.rsqrt(jnp.mean(hv * hv, axis=-1, keepdims=True) + RMS_EPS)
        hh = hv * r
        dyn = alpha * dx_ref[...]
        _accumulate(dg_ref, jnp.sum(dyn * hh, axis=0, keepdims=True), pl.program_id(0) == 0)
        dhh = dyn * g_ref[...]
        dh = (r * (dhh - hh * jnp.mean(dhh * hh, axis=-1, keepdims=True))).astype(dh_ref.dtype)
        dh_ref[...] = dh
        da = lax.dot_general(dh, b_ref[...], NT, preferred_element_type=F32)
        if hgu is None:
            out_ref[...] = da.astype(out_ref.dtype)
        else:
            gate = rest[0][:, :K].astype(F32)
            up = rest[0][:, K:].astype(F32)
            sg = jax.nn.sigmoid(gate)
            out_ref[:, :K] = (da * up * sg * (1.0 + gate * (1.0 - sg))).astype(out_ref.dtype)
            out_ref[:, K:] = (da * gate * sg).astype(out_ref.dtype)

    in_specs = [_rows(tm, D), _rows(tm, D), _whole((1, D)), _whole((K, D))]
    args = [dx, h, g.reshape(1, D), b]
    wide = K
    if hgu is not None:
        wide = 2 * K
        in_specs.append(_rows(tm, wide))
        args.append(hgu)
    return pl.pallas_call(
        body, name=f"post_bwd_mm_{tag}", grid=(T // tm,), in_specs=in_specs,
        out_specs=[_rows(tm, D), _whole((1, D)), _rows(tm, wide)],
        out_shape=[_sds((T, D), MM_DTYPE), _sds((1, D), F32), _sds((T, wide), MM_DTYPE)],
        compiler_params=_params())(*args)


def _mm_nt_pre(dy, w, dres, x, g, tag):
    T, C = dy.shape
    D = x.shape[1]
    tm = _tile(T, ROW_TILE)
    n = w.shape[-1]

    def body(dy_ref, w_ref, dres_ref, x_ref, g_ref, dx_ref, dg_ref):
        if w.ndim == 2:
            dn = lax.dot_general(dy_ref[...], w_ref[...], NT, preferred_element_type=F32)
        else:
            dn = None
            for s in range(N_CHIP):
                part = lax.dot_general(dy_ref[:, s * n:(s + 1) * n], w_ref[s], NT, preferred_element_type=F32)
                dn = part if dn is None else dn + part
        xv = x_ref[...]
        r = lax.rsqrt(jnp.mean(xv * xv, axis=-1, keepdims=True) + RMS_EPS)
        xh = xv * r
        _accumulate(dg_ref, jnp.sum(dn * xh, axis=0, keepdims=True), pl.program_id(0) == 0)
        dxh = dn * g_ref[...]
        dx_ref[...] = dres_ref[...] + r * (dxh - xh * jnp.mean(dxh * xh, axis=-1, keepdims=True))

    return pl.pallas_call(
        body, name=f"mm_nt_pre_{tag}", grid=(T // tm,),
        in_specs=[_rows(tm, C), _whole(w.shape), _rows(tm, D), _rows(tm, D), _whole((1, D))],
        out_specs=[_rows(tm, D), _whole((1, D))], out_shape=[_sds((T, D), F32), _sds((1, D), F32)],
        compiler_params=_params())(dy, w, dres, x, g.reshape(1, D))


def _adamw(w, g, m, v, tag):
    R, C = w.shape
    tr = _tile(R, ROW_TILE)

    def body(w_ref, g_ref, m_ref, v_ref, d_ref, mo_ref, vo_ref):
        gv = g_ref[...]
        mn = ADAM_B1 * m_ref[...] + (1.0 - ADAM_B1) * gv
        vn = ADAM_B2 * v_ref[...] + (1.0 - ADAM_B2) * (gv * gv)
        m_hat = mn / (1.0 - ADAM_B1 ** ADAM_STEP)
        v_hat = vn / (1.0 - ADAM_B2 ** ADAM_STEP)
        d_ref[...] = -ADAM_LR * (m_hat / (jnp.sqrt(v_hat) + ADAM_EPS) + ADAM_WD * w_ref[...])
        mo_ref[...] = mn
        vo_ref[...] = vn

    return pl.pallas_call(
        body, name=f"adamw_{tag}", grid=(R // tr,),
        in_specs=[_rows(tr, C)] * 4, out_specs=[_rows(tr, C)] * 3,
        out_shape=[_sds((R, C), F32)] * 3, compiler_params=_params())(w, g, m, v)


def _sum_devices(gall):
    _, R, C = gall.shape

    def body(g_ref, o_ref):
        acc = g_ref[0]
        for d in range(1, N_DEV):
            acc = acc + g_ref[d]
        o_ref[...] = acc

    return pl.pallas_call(
        body, name="sum_devices", in_specs=[_whole((N_DEV, R, C))], out_specs=_whole((R, C)),
        out_shape=_sds((R, C), F32), grid=(1,), compiler_params=_params())(gall)


HBM = pl.BlockSpec(memory_space=pltpu.HBM)
SEM = pl.BlockSpec(memory_space=pltpu.SEMAPHORE)
EFFECT = pltpu.SideEffectType.DATAFLOW_SIDE_EFFECTING


def _place():
    x, y, c = lax.axis_index("x"), lax.axis_index("y"), lax.axis_index("c")
    chips = ((1 - x, y), (x, 1 - y), (1 - x, 1 - y))
    return x, y, c, chips


def _remote(src, dst, send_sem, recv_sem, dev):
    return pltpu.make_async_remote_copy(src_ref=src, dst_ref=dst, send_sem=send_sem, recv_sem=recv_sem,
                                        device_id=dev, device_id_type=MESH)


def _in_hbm(a):
    return pltpu.with_memory_space_constraint(a, pltpu.HBM)


def _own_slot(w4, l, dtype, place, tag):
    _, _, r, col = w4.shape
    tr = _tile(r, 2 * ROW_TILE)

    def body(place_ref, x_ref, o_ref):
        o_ref[...] = x_ref[...].astype(o_ref.dtype)

    grid_spec = pltpu.PrefetchScalarGridSpec(
        num_scalar_prefetch=1, grid=(2, r // tr),
        in_specs=[pl.BlockSpec((None, None, tr, col), lambda h, i, p: (l, h, i, 0))],
        out_specs=pl.BlockSpec((None, None, tr, col), lambda h, i, p: (p[1], h, i, 0)))
    return pl.pallas_call(
        body, name=f"own_slot_{tag}", grid_spec=grid_spec, out_shape=_sds((N_CHIP, 2, r, col), dtype),
        compiler_params=_params())(place, w4)


def _gather_start(bufs, after, tag):
    n = len(bufs)

    def body(*refs):
        ins = refs[:n]
        s_sem, r_sem, token = refs[n + 1], refs[n + 2], refs[2 * n + 3]
        x, y, c, chips = _place()
        me = 2 * x + y
        for i in range(n):
            mine = ins[i].at[me, c]
            for j, (px, py) in enumerate(chips):
                _remote(mine, mine, s_sem.at[3 * i + j], r_sem.at[3 * i + j], (px, py, c)).start()
        token[...] = jnp.zeros_like(token)

    dma = pltpu.SemaphoreType.DMA
    res = pl.pallas_call(
        body, name=f"gather_start_{tag}", in_specs=[HBM] * n + [ANY],
        out_specs=[SEM, SEM] + [HBM] * n + [pl.BlockSpec(memory_space=pltpu.VMEM)],
        out_shape=[dma((3 * n,)), dma((3 * n,))] + [pltpu.HBM(b.shape, b.dtype) for b in bufs] + [_sds((8, LANES), F32)],
        input_output_aliases={i: i + 2 for i in range(n)},
        compiler_params=pltpu.CompilerParams(has_side_effects=EFFECT),
        )(*[_in_hbm(b) for b in bufs], after)
    return res[0], res[1], list(res[2:2 + n]), res[-1]


def _gather_pass(s_sem, r_sem, bufs, first, after, tag):
    n = len(bufs)

    def body(*refs):
        ins = refs[:n]
        a_s, a_r, b_s, b_r = refs[n], refs[n + 1], refs[n + 3], refs[n + 4]
        x, y, c, chips = _place()
        me = 2 * x + y
        sib = (x, y, 1 - c)
        for i in range(n):
            mine = ins[i].at[me, c]
            for j, (px, py) in enumerate(chips):
                k = 3 * (first + i) + j
                _remote(mine, mine, a_s.at[k], a_r.at[k], (px, py, c)).wait_send()
        for j, (px, py) in enumerate(chips):
            for i in range(n):
                k = 3 * (first + i) + j
                blk = ins[i].at[2 * px + py, c]
                _remote(blk, blk, a_s.at[k], a_r.at[k], (px, py, c)).wait_recv()
                _remote(blk, blk, b_s.at[3 * i + j], b_r.at[3 * i + j], sib).start()

    dma = pltpu.SemaphoreType.DMA
    res = pl.pallas_call(
        body, name=f"gather_pass_{tag}", in_specs=[HBM] * n + [SEM, SEM, ANY],
        out_specs=[SEM, SEM] + [HBM] * n,
        out_shape=[dma((3 * n,)), dma((3 * n,))] + [pltpu.HBM(b.shape, b.dtype) for b in bufs],
        input_output_aliases={i: i + 2 for i in range(n)},
        compiler_params=pltpu.CompilerParams(has_side_effects=EFFECT),
        )(*bufs, s_sem, r_sem, after)
    return res[0], res[1], list(res[2:])


def _gather_land(s_sem, r_sem, bufs, tag):
    n = len(bufs)

    def body(*refs):
        ins = refs[:n]
        b_s, b_r = refs[n], refs[n + 1]
        x, y, c, chips = _place()
        sib = (x, y, 1 - c)
        for j, (px, py) in enumerate(chips):
            for i in range(n):
                sent = ins[i].at[2 * px + py, c]
                got = ins[i].at[2 * px + py, 1 - c]
                _remote(sent, sent, b_s.at[3 * i + j], b_r.at[3 * i + j], sib).wait_send()
                _remote(got, got, b_s.at[3 * i + j], b_r.at[3 * i + j], sib).wait_recv()

    return list(pl.pallas_call(
        body, name=f"gather_land_{tag}", in_specs=[HBM] * n + [SEM, SEM], out_specs=[HBM] * n,
        out_shape=[pltpu.HBM(b.shape, b.dtype) for b in bufs],
        input_output_aliases={i: i for i in range(n)},
        compiler_params=pltpu.CompilerParams(has_side_effects=EFFECT),
        )(*bufs, s_sem, r_sem))


def _rs_pair_send(grads):
    n = len(grads)

    def body(*refs):
        ins, outs = refs[:n], refs[n:2 * n]
        s_sem, r_sem = refs[2 * n:]
        x, y, c, _ = _place()
        sib = (x, y, 1 - c)
        sends = []
        for i in range(n):
            cp = _remote(ins[i].at[:, 1 - c], outs[i], s_sem.at[i], r_sem.at[i], sib)
            cp.start()
            sends.append(cp)
        for cp in sends:
            cp.wait()

    out_shape = [_sds((N_CHIP,) + g.shape[2:], g.dtype) for g in grads]
    dma = pltpu.SemaphoreType.DMA
    return pl.pallas_call(
        body, name=f"rs_pair_send_{n}", in_specs=[ANY] * n, out_specs=[ANY] * n, out_shape=out_shape,
        scratch_shapes=[dma((n,)), dma((n,))],
        )(*grads)


def _rs_pair_add(g, recv, place, tag):
    r, col = g.shape[-2:]
    tr = _tile(r, ROW_TILE)

    def body(place_ref, g_ref, r_ref, wire_ref, own_ref):
        tot = g_ref[...] + r_ref[...]
        wire_ref[...] = tot.astype(wire_ref.dtype)

        @pl.when(pl.program_id(1) == place_ref[1])
        def _():
            own_ref[...] = tot

    grid_spec = pltpu.PrefetchScalarGridSpec(
        num_scalar_prefetch=1, grid=(r // tr, N_CHIP),
        in_specs=[pl.BlockSpec((None, None, tr, col), lambda i, s, p: (s, p[0], i, 0)),
                  pl.BlockSpec((None, tr, col), lambda i, s, p: (s, i, 0))],
        out_specs=[pl.BlockSpec((None, tr, col), lambda i, s, p: (s, i, 0)),
                   pl.BlockSpec((tr, col), lambda i, s, p: (i, 0))])
    return pl.pallas_call(
        body, name=f"rs_pair_add_{tag}", grid_spec=grid_spec,
        out_shape=[_sds((N_CHIP, r, col), WIRE_DTYPE), _sds((r, col), F32)],
        compiler_params=_params())(place, g, recv)


def _rs_start(wires, tag):
    n = len(wires)

    def body(*refs):
        ins = refs[:n]
        s_sem, r_sem, token = refs[2 * n], refs[2 * n + 1], refs[4 * n + 2]
        x, y, c, chips = _place()
        for i in range(n):
            land = refs[n + i]
            for j, (px, py) in enumerate(chips):
                _remote(ins[i].at[2 * px + py], land.at[j], s_sem.at[3 * i + j], r_sem.at[3 * i + j], (px, py, c)).start()
        token[...] = jnp.zeros_like(token)

    lands = [lax.empty((3,) + w.shape[1:], w.dtype) for w in wires]
    both = list(wires) + lands
    dma = pltpu.SemaphoreType.DMA
    res = pl.pallas_call(
        body, name=f"rs_start_{tag}", in_specs=[HBM] * (2 * n),
        out_specs=[SEM, SEM] + [HBM] * (2 * n) + [pl.BlockSpec(memory_space=pltpu.VMEM)],
        out_shape=[dma((3 * n,)), dma((3 * n,))] + [pltpu.HBM(b.shape, b.dtype) for b in both] + [_sds((8, LANES), F32)],
        input_output_aliases={i: i + 2 for i in range(2 * n)},
        compiler_params=pltpu.CompilerParams(has_side_effects=EFFECT),
        )(*[_in_hbm(b) for b in both])
    return res[0], res[1], res[2:2 + n], res[2 + n:2 + 2 * n], res[-1]


def _rs_wait(s_sem, r_sem, wires, lands, after, tag):
    n = len(wires)

    def body(*refs):
        ins = refs[:n]
        s_ref, r_ref = refs[2 * n], refs[2 * n + 1]
        x, y, c, chips = _place()
        for i in range(n):
            land = refs[n + i]
            for j, (px, py) in enumerate(chips):
                cp = _remote(ins[i].at[2 * px + py], land.at[j], s_ref.at[3 * i + j], r_ref.at[3 * i + j], (px, py, c))
                cp.wait_send()
                cp.wait_recv()

    both = list(wires) + list(lands)
    res = pl.pallas_call(
        body, name=f"rs_wait_{tag}", in_specs=[HBM] * (2 * n) + [SEM, SEM, ANY], out_specs=[HBM] * (2 * n),
        out_shape=[pltpu.HBM(b.shape, b.dtype) for b in both],
        input_output_aliases={i: i for i in range(2 * n)},
        compiler_params=pltpu.CompilerParams(has_side_effects=EFFECT),
        )(*both, s_sem, r_sem, after)
    return res[n:]


def _rs_chip_send(wires, small):
    n = len(wires)

    def body(*refs):
        ins, small_ref = refs[:n], refs[n]
        outs, gall_ref = refs[n + 1:2 * n + 1], refs[2 * n + 1]
        s_sem, r_sem, s_small, r_small, s_loc = refs[2 * n + 2:]
        x, y, c, chips = _place()
        me = 4 * x + 2 * y + c
        sends = []
        loc = pltpu.make_async_copy(small_ref, gall_ref.at[me], s_loc)
        loc.start()
        for i in range(n):
            for j, (px, py) in enumerate(chips):
                cp = _remote(ins[i].at[2 * px + py], outs[i].at[j], s_sem.at[i, j], r_sem.at[i, j], (px, py, c))
                cp.start()
                sends.append(cp)
        for k in range(1, N_DEV):
            peer = (x ^ (k >> 2), y ^ ((k >> 1) & 1), c ^ (k & 1))
            cp = _remote(small_ref, gall_ref.at[me], s_small.at[k - 1], r_small.at[k - 1], peer)
            cp.start()
            sends.append(cp)
        for cp in sends:
            cp.wait()
        loc.wait()

    out_shape = [_sds((3,) + w.shape[1:], w.dtype) for w in wires] + [_sds((N_DEV,) + small.shape, small.dtype)]
    dma = pltpu.SemaphoreType.DMA
    return pl.pallas_call(
        body, name="rs_chip_send", in_specs=[ANY] * (n + 1), out_specs=[ANY] * (n + 1), out_shape=out_shape,
        scratch_shapes=[dma((n, 3)), dma((n, 3)), dma((N_DEV - 1,)), dma((N_DEV - 1,)), dma(())],
        )(*wires, small)


def _rs_chip_add(own, recv, place, l, L, prev, tag):
    r, col = own.shape
    tr = _tile(r, ROW_TILE)

    def body(place_ref, o_ref, r_ref, *rest):
        acc = o_ref[...]
        for j in range(3):
            acc = acc + r_ref[j].astype(F32)
        rest[-1][...] = acc

    in_specs = [pl.BlockSpec((tr, col), lambda i, p: (i, 0)), pl.BlockSpec((3, tr, col), lambda i, p: (0, i, 0))]
    args = [place, own, recv]
    kw = {}
    if prev is not None:
        in_specs.append(ANY)
        args.append(prev)
        kw["input_output_aliases"] = {3: 0}
    grid_spec = pltpu.PrefetchScalarGridSpec(
        num_scalar_prefetch=1, grid=(r // tr,), in_specs=in_specs,
        out_specs=pl.BlockSpec((None, None, tr, col), lambda i, p: (l, p[0], i, 0)))
    return pl.pallas_call(
        body, name=f"rs_chip_add_{tag}", grid_spec=grid_spec, out_shape=_sds((L, 2, r, col), F32),
        compiler_params=_params(), **kw)(*args)


def _rs_pair_share(fulls):
    n = len(fulls)

    def body(*refs):
        outs = refs[n:2 * n]
        s_sem, r_sem = refs[2 * n:]
        x, y, c, _ = _place()
        sib = (x, y, 1 - c)
        started = []
        for i in range(n):
            cp = _remote(outs[i].at[:, c], outs[i].at[:, c], s_sem.at[i], r_sem.at[i], sib)
            cp.start()
            started.append(cp)
        for i, cp in enumerate(started):
            cp.wait_send()
            _remote(outs[i].at[:, 1 - c], outs[i].at[:, 1 - c], s_sem.at[i], r_sem.at[i], sib).wait_recv()

    dma = pltpu.SemaphoreType.DMA
    return pl.pallas_call(
        body, name="rs_pair_share", in_specs=[ANY] * n, out_specs=[ANY] * n,
        out_shape=[_sds(f.shape, f.dtype) for f in fulls],
        input_output_aliases={i: i for i in range(n)},
        scratch_shapes=[dma((n,)), dma((n,))],
        )(*fulls)


def _ffn_fwd(x, g_pre, g_post, w_in, w_out, tag):
    xn, hgu, act = _norm_mm_in(x, g_pre, w_in, tag, swiglu=True)
    h, x_out = _mm_out_post(act, w_out.reshape(-1, w_out.shape[-1]), x, g_post, 0.5, tag)
    return x_out, (x, xn, hgu, act, h)


def _ffn_bwd(dx, saved, g_pre, g_post, w_in, w_out, tag):
    x, xn, hgu, act, h = saved
    dh, dg_post, dhgu = _post_bwd_mm(dx, h, g_post, 0.5, w_out.reshape(-1, w_out.shape[-1]), tag, hgu=hgu)
    dw_out = _mm_tn_out(act, dh, tag)
    dw_in = _mm_tn_in(xn, dhgu, tag)
    dx_in, dg_pre = _mm_nt_pre(dhgu, w_in, dx, x, g_pre, tag)
    return dx_in, dg_pre, dg_post, dw_in, dw_out


def kernel(x, ffn1_pre_g, ffn1_post_g, ffn1_w_in, ffn1_w_out, mix_pre_g, mix_post_g, ffn2_pre_g, ffn2_post_g, ffn2_w_in, ffn2_w_out, conv_w_in, conv_k, conv_w_out, kv_g, kv_w, forget_b, attn_w_qg, attn_w_o, loss_target, m_ffn1_pre_g, m_ffn1_post_g, m_ffn1_w_in, m_ffn1_w_out, m_mix_pre_g, m_mix_post_g, m_ffn2_pre_g, m_ffn2_post_g, m_ffn2_w_in, m_ffn2_w_out, m_conv_w_in, m_conv_k, m_conv_w_out, m_kv_g, m_kv_w, m_forget_b, m_attn_w_qg, m_attn_w_o, v_ffn1_pre_g, v_ffn1_post_g, v_ffn1_w_in, v_ffn1_w_out, v_mix_pre_g, v_mix_post_g, v_ffn2_pre_g, v_ffn2_post_g, v_ffn2_w_in, v_ffn2_w_out, v_conv_w_in, v_conv_k, v_conv_w_out, v_kv_g, v_kv_w, v_forget_b, v_attn_w_qg, v_attn_w_o):
    Bl, S, D = x.shape
    T = Bl * S
    H = forget_b.shape[0]
    assert D == H * HEAD_DIM and D % LANES == 0
    kvc = kv_w.shape[1]
    kvp = -(-kvc // LANES) * LANES
    kv_all = 2 * D + LANES
    dk_cols = conv_k.shape[2]
    chip = 2 * lax.axis_index("x") + lax.axis_index("y")
    core = lax.axis_index("c")

    given = dict(ffn1_w_in=(ffn1_w_in, m_ffn1_w_in, v_ffn1_w_in), ffn1_w_out=(ffn1_w_out, m_ffn1_w_out, v_ffn1_w_out),
                 ffn2_w_in=(ffn2_w_in, m_ffn2_w_in, v_ffn2_w_in), ffn2_w_out=(ffn2_w_out, m_ffn2_w_out, v_ffn2_w_out),
                 conv_w_in=(conv_w_in, m_conv_w_in, v_conv_w_in), conv_w_out=(conv_w_out, m_conv_w_out, v_conv_w_out),
                 kv_w=(kv_w, m_kv_w, v_kv_w), attn_w_qg=(attn_w_qg, m_attn_w_qg, v_attn_w_qg),
                 attn_w_o=(attn_w_o, m_attn_w_o, v_attn_w_o))
    shards = {k: w for k, (w, _, _) in given.items()}
    shards["kv_w"] = jnp.pad(kv_w, ((0, 0), (0, kvp - kvc)))[None]
    groups = [[("ffn1_w_in", 0), ("ffn1_w_out", 0)], [("conv_w_in", 0), ("conv_w_out", 0)],
              [("ffn2_w_in", 0), ("ffn2_w_out", 0)], [("kv_w", 0), ("ffn1_w_in", 1), ("ffn1_w_out", 1)],
              [("attn_w_qg", 0), ("attn_w_o", 0), ("ffn2_w_in", 1), ("ffn2_w_out", 1)]]
    first = groups[0] + groups[1] + groups[2]
    second = groups[3] + groups[4]
    place = jnp.stack([core, chip]).astype(jnp.int32)

    def slot(key, where):
        w = shards[key[0]]
        L, r, col = w.shape
        return _own_slot(w.reshape(L, 2, r // 2, col), key[1], MM_DTYPE, where, f"{key[0]}{key[1]}")

    def whole(g):
        return g.reshape(N_CHIP, -1, g.shape[-1])

    taps_slot = _own_slot(jnp.pad(conv_k[0], ((0, 13), (0, 0))).reshape(1, 2, 8, dk_cols), 0, F32, place, "conv_k")
    fb = jnp.pad(forget_b, (0, LANES - H)).reshape(1, LANES)
    s_0, r_0, fly_0, token = _gather_start([slot(key, place) for key in groups[0]] + [taps_slot], fb, "first")
    later = groups[1] + groups[2] + groups[3] + groups[4]
    s_1, r_1, fly_1, token = _gather_start([slot(key, place) for key in later], token, "rest")
    W = {}

    def arrive(g, after):
        if g == 0:
            sems, bufs, lo = (s_0, r_0), fly_0, 0
        else:
            lo = sum(len(groups[k]) for k in range(1, g))
            sems, bufs = (s_1, r_1), fly_1[lo:lo + len(groups[g])]
        got = _gather_land(*_gather_pass(*sems, bufs, lo, after, f"g{g}"), f"g{g}")
        W.update({key: whole(b) for key, b in zip(groups[g], got)})
        return got

    k_taps = arrive(0, token)[-1].reshape(N_CHIP, 16, dk_cols).transpose(1, 0, 2).reshape(16, D)[:8]

    x0 = x.reshape(T, D)
    x1, s_f1a = _ffn_fwd(x0, ffn1_pre_g[0], ffn1_post_g[0], W["ffn1_w_in", 0], W["ffn1_w_out", 0], "l0f1")
    arrive(1, x1)
    w_o_conv = W["conv_w_out", 0].reshape(D, D)
    xn_c, bch = _norm_mm_in(x1, mix_pre_g[0], W["conv_w_in", 0], "conv")
    z_c = _conv_fwd(bch, k_taps, Bl, S)
    m_c, x2 = _mm_out_post(z_c, w_o_conv, x1, mix_post_g[0], 1.0, "conv_out")
    arrive(2, x2)
    x3, s_f2a = _ffn_fwd(x2, ffn2_pre_g[0], ffn2_post_g[0], W["ffn2_w_in", 0], W["ffn2_w_out", 0], "l0f2")

    arrive(3, x3)
    kv_full = jnp.concatenate([W["kv_w", 0][s, :, :kvc] for s in range(N_CHIP)], axis=1)
    kv_full = jnp.pad(kv_full, ((0, 0), (0, kv_all - kv_full.shape[1])))
    xn_kv = _rms_fwd(x3, kv_g, "kv")
    kvact = _mm_nn(xn_kv, kv_full[:, :2 * D], MM_DTYPE, "kv")
    pf = _mm_nn(xn_kv, kv_full[:, 2 * D:], F32, "forget")
    cum = _forget_fwd(pf, fb, Bl, S)
    bq = min(S, ATT_BLOCK)
    c3 = cum.reshape(Bl, S, LANES)[:, :, :H].transpose(0, 2, 1)
    c_col = jnp.broadcast_to(c3[..., None], (Bl, H, S, LANES))
    c_row = c3.reshape(Bl, H, S // bq, 1, bq)

    x4, s_f1b = _ffn_fwd(x3, ffn1_pre_g[1], ffn1_post_g[1], W["ffn1_w_in", 1], W["ffn1_w_out", 1], "l1f1")
    arrive(4, x4)
    w_o_attn = W["attn_w_o", 0].reshape(D, D)
    xn_a, qg = _norm_mm_in(x4, mix_pre_g[1], W["attn_w_qg", 0], "qg")
    o, lse = _attn_fwd(qg, kvact, c_col, c_row, Bl, S, D)
    z_a = _gate_fwd(qg, o)
    m_a, x5 = _mm_out_post(z_a, w_o_attn, x4, mix_post_g[1], 1.0, "attn_out")
    x6, s_f2b = _ffn_fwd(x5, ffn2_pre_g[1], ffn2_post_g[1], W["ffn2_w_in", 1], W["ffn2_w_out", 1], "l1f2")

    dy, loss_local = _loss_grad(x6, loss_target.reshape(T, D))
    loss = lax.psum(loss_local, ("x", "y", "c"))

    G = {}
    dx5, dg_f2pre_1, dg_f2post_1, G["ffn2_w_in", 1], G["ffn2_w_out", 1] = _ffn_bwd(
        dy, s_f2b, ffn2_pre_g[1], ffn2_post_g[1], W["ffn2_w_in", 1], W["ffn2_w_out", 1], "l1f2")
    dm_a, dg_mixpost_1, dz_a = _post_bwd_mm(dx5, m_a, mix_post_g[1], 1.0, w_o_attn, "attn_out")
    G["attn_w_o", 0] = _mm_tn_out(z_a, dm_a, "attn_out")
    do = _gate_do(dz_a, qg)
    dq, dk, dv, dcr = _attn_bwd(qg, kvact, do, lse, c_col, c_row, Bl, S, D)
    dqg = _gate_bwd(dz_a, qg, o, dq)
    G["attn_w_qg", 0] = _mm_tn_in(xn_a, dqg, "qg")
    dx4, dg_mixpre_1 = _mm_nt_pre(dqg, W["attn_w_qg", 0], dx5, x4, mix_pre_g[1], "qg")
    dx3, dg_f1pre_1, dg_f1post_1, G["ffn1_w_in", 1], G["ffn1_w_out", 1] = _ffn_bwd(
        dx4, s_f1b, ffn1_pre_g[1], ffn1_post_g[1], W["ffn1_w_in", 1], W["ffn1_w_out", 1], "l1f1")

    dcum = jnp.pad(dcr.reshape(Bl, H, S).transpose(0, 2, 1), ((0, 0), (0, 0), (0, LANES - H))).reshape(T, LANES)
    dpf, dfb = _forget_bwd(dcum, pf, fb, Bl, S)
    dp = jnp.concatenate([dk.astype(MM_DTYPE), dv.astype(MM_DTYPE), dpf], axis=1)
    G_kv_full = _mm_tn(xn_kv, dp, "kv")
    G["kv_w", 0] = jnp.stack([jnp.pad(G_kv_full[:, s * kvc:(s + 1) * kvc], ((0, 0), (0, kvp - kvc))) for s in range(N_CHIP)])
    dx3, dg_kv = _mm_nt_pre(dp, kv_full, dx3, x3, kv_g, "kv")

    def pair_sums(keys):
        grads = [G[k].reshape(N_CHIP, 2, G[k].shape[1] // 2, G[k].shape[2]) for k in keys]
        wires, owns = [], []
        for k, g, r in zip(keys, grads, _rs_pair_send(grads)):
            w, own = _rs_pair_add(g, r, place, f"{k[0]}{k[1]}")
            wires.append(w)
            owns.append(own)
        return wires, owns

    late = groups[2] + groups[1]
    last = groups[0]
    wires_2, owns_2 = pair_sums(second)
    r_sems, r_semr, wires_2, lands_2, token = _rs_start(wires_2, "second")

    dx2, dg_f2pre_0, dg_f2post_0, G["ffn2_w_in", 0], G["ffn2_w_out", 0] = _ffn_bwd(
        dx3, s_f2a, ffn2_pre_g[0], ffn2_post_g[0] + token[0, :1], W["ffn2_w_in", 0], W["ffn2_w_out", 0], "l0f2")
    dm_c, dg_mixpost_0, dz_c = _post_bwd_mm(dx2, m_c, mix_post_g[0], 1.0, w_o_conv, "conv_out")
    G["conv_w_out", 0] = _mm_tn_out(z_c, dm_c, "conv_out")
    db, dcg, dhh, dk_taps = _conv_bwd(bch, dz_c, k_taps, Bl, S)
    dbch = jnp.concatenate([db, dcg, dhh], axis=1)
    G["conv_w_in", 0] = _mm_tn_in(xn_c, dbch, "conv")
    dx1, dg_mixpre_0 = _mm_nt_pre(dbch, W["conv_w_in", 0], dx2, x1, mix_pre_g[0], "conv")
    wires_l, owns_l = pair_sums(late)
    l_sems, l_semr, wires_l, lands_l, token = _rs_start(wires_l, "late")
    dx0, dg_f1pre_0, dg_f1post_0, G["ffn1_w_in", 0], G["ffn1_w_out", 0] = _ffn_bwd(
        dx1, s_f1a, ffn1_pre_g[0], ffn1_post_g[0] + token[0, :1], W["ffn1_w_in", 0], W["ffn1_w_out", 0], "l0f1")
    grad_x = dx0.reshape(Bl, S, D)

    recvs_2 = _rs_wait(r_sems, r_semr, wires_2, lands_2, dx0, "second")
    recvs_l = _rs_wait(l_sems, l_semr, wires_l, lands_l, dx0, "late")
    wires_1, owns_1 = pair_sums(last)

    def row(v):
        return jnp.pad(v.reshape(-1), (0, D - v.size)).reshape(1, D)

    small_parts = [dg_f1pre_0, dg_f1pre_1, dg_f1post_0, dg_f1post_1, dg_mixpre_0, dg_mixpre_1, dg_mixpost_0, dg_mixpost_1,
                   dg_f2pre_0, dg_f2pre_1, dg_f2post_0, dg_f2post_1, dg_kv, row(dfb[0, :H]), dk_taps[:3]]
    small = jnp.concatenate(small_parts, axis=0)
    small = jnp.pad(small, ((0, SMALL_ROWS - small.shape[0]), (0, 0)))
    outs = _rs_chip_send(wires_1, small)
    recvs_1, gall = outs[:-1], outs[-1]
    partial = {}
    for key, own, rcv in zip(last + late + second, owns_1 + owns_l + owns_2,
                             list(recvs_1) + list(recvs_l) + list(recvs_2)):
        name, l = key
        partial[name] = _rs_chip_add(own, rcv, place, l, shards[name].shape[0], partial.get(name), f"{name}{l}")
    names = list(partial)
    reduced = _rs_pair_share([partial[k] for k in names])
    gsum = _sum_devices(gall)

    res = {}
    for k, red in zip(names, reduced):
        w, m, v = given[k]
        g2 = red.reshape(-1, red.shape[-1])
        if k == "kv_w":
            g2 = g2[:, :kvc]
        flat = lambda a: a.reshape(-1, a.shape[-1])
        d, mn, vn = _adamw(flat(w), g2, flat(m), flat(v), k)
        res[k] = tuple(a.reshape(w.shape) for a in (g2, d, mn, vn))

    small_names = ["ffn1_pre_g", "ffn1_post_g", "mix_pre_g", "mix_post_g", "ffn2_pre_g", "ffn2_post_g"]
    small_given = dict(ffn1_pre_g=(ffn1_pre_g, m_ffn1_pre_g, v_ffn1_pre_g), ffn1_post_g=(ffn1_post_g, m_ffn1_post_g, v_ffn1_post_g),
                       mix_pre_g=(mix_pre_g, m_mix_pre_g, v_mix_pre_g), mix_post_g=(mix_post_g, m_mix_post_g, v_mix_post_g),
                       ffn2_pre_g=(ffn2_pre_g, m_ffn2_pre_g, v_ffn2_pre_g), ffn2_post_g=(ffn2_post_g, m_ffn2_post_g, v_ffn2_post_g))

    def pack(idx):
        rows_ = [small_given[k][idx] for k in small_names]
        rows_ += [row((kv_g, m_kv_g, v_kv_g)[idx]), row((forget_b, m_forget_b, v_forget_b)[idx])]
        rows_.append(jnp.pad((conv_k, m_conv_k, v_conv_k)[idx][0], ((0, 0), (0, D - dk_cols))))
        a = jnp.concatenate(rows_, axis=0)
        return jnp.pad(a, ((0, SMALL_ROWS - a.shape[0]), (0, 0)))

    g_taps = lax.dynamic_slice_in_dim(gsum[14:17], chip * dk_cols, dk_cols, axis=1)
    g_small = jnp.concatenate([gsum[:14], jnp.pad(g_taps, ((0, 0), (0, D - dk_cols))), gsum[17:]], axis=0)
    d_s, m_s, v_s = _adamw(pack(0), g_small, pack(1), pack(2), "small")
    for i, k in enumerate(small_names):
        res[k] = tuple(a[2 * i:2 * i + 2] for a in (g_small, d_s, m_s, v_s))
    res["kv_g"] = tuple(a[12] for a in (g_small, d_s, m_s, v_s))
    res["forget_b"] = tuple(a[13, :H] for a in (g_small, d_s, m_s, v_s))
    res["conv_k"] = tuple(a[14:17, :dk_cols][None] for a in (g_small, d_s, m_s, v_s))

    order = ["ffn1_pre_g", "ffn1_post_g", "ffn1_w_in", "ffn1_w_out", "mix_pre_g", "mix_post_g", "ffn2_pre_g", "ffn2_post_g",
             "ffn2_w_in", "ffn2_w_out", "conv_w_in", "conv_k", "conv_w_out", "kv_g", "kv_w", "forget_b", "attn_w_qg", "attn_w_o"]
    out = [loss, grad_x]
    for idx in range(4):
        out += [res[k][idx] for k in order]
    return tuple(out)
```

```python
import functools
import math

import jax
import jax.numpy as jnp
from jax import lax
from jax.experimental import pallas as pl
from jax.experimental.pallas import tpu as pltpu

F32 = jnp.float32
MM_DTYPE = jnp.bfloat16
WIRE_DTYPE = jnp.bfloat16

RMS_EPS = 1e-6
ADAM_LR = 0.001
ADAM_B1 = 0.9
ADAM_B2 = 0.999
ADAM_EPS = 1e-08
ADAM_WD = 0.01
ADAM_STEP = 10

HEAD_DIM = 64
LANES = 128
N_CHIP = 4
N_DEV = 8
ROW_TILE = 256
MM_TILE = 512
FUSED_TILE = 512
TN_TILE = 2048
ATT_BLOCK = 512
SMALL_ROWS = 24
VMEM_LIMIT = 56 * 1024 * 1024
MESH = pl.DeviceIdType.MESH
ANY = pl.BlockSpec(memory_space=pl.ANY)

NT = (((1,), (1,)), ((), ()))
TN = (((0,), (0,)), ((), ()))


def _tile(n, pref):
    if n <= pref:
        return n
    t = pref - pref % 16
    while n % t:
        t -= 16
    return t


def _params():
    return pltpu.CompilerParams(vmem_limit_bytes=VMEM_LIMIT)


def _sds(shape, dtype):
    return jax.ShapeDtypeStruct(shape, dtype)


def _rows(tm, c):
    return pl.BlockSpec((tm, c), lambda i: (i, 0))


def _whole(shape):
    return pl.BlockSpec(shape, lambda *_: (0,) * len(shape))


def _resident(shape):
    return pl.BlockSpec(shape, lambda *_: (0,) * len(shape), pipeline_mode=pl.Buffered(1))


def _rms_fwd(x, g, tag):
    T, D = x.shape
    tm = _tile(T, ROW_TILE)

    def body(x_ref, g_ref, o_ref):
        xv = x_ref[...]
        r = lax.rsqrt(jnp.mean(xv * xv, axis=-1, keepdims=True) + RMS_EPS)
        o_ref[...] = (xv * r * g_ref[...]).astype(o_ref.dtype)

    return pl.pallas_call(
        body, name=f"rms_fwd_{tag}", grid=(T // tm,),
        in_specs=[_rows(tm, D), _whole((1, D))], out_specs=_rows(tm, D),
        out_shape=_sds((T, D), MM_DTYPE), compiler_params=_params())(x, g.reshape(1, D))


def _post_fwd(x, h, g, alpha, tag):
    T, D = x.shape
    tm = _tile(T, ROW_TILE)

    def body(x_ref, h_ref, g_ref, o_ref):
        hv = h_ref[...]
        r = lax.rsqrt(jnp.mean(hv * hv, axis=-1, keepdims=True) + RMS_EPS)
        o_ref[...] = x_ref[...] + alpha * (hv * r * g_ref[...])

    return pl.pallas_call(
        body, name=f"post_fwd_{tag}", grid=(T // tm,),
        in_specs=[_rows(tm, D), _rows(tm, D), _whole((1, D))], out_specs=_rows(tm, D),
        out_shape=_sds((T, D), F32), compiler_params=_params())(x, h, g.reshape(1, D))


def _accumulate(ref, part, first):
    @pl.when(first)
    def _():
        ref[...] = part

    @pl.when(jnp.logical_not(first))
    def _():
        ref[...] += part


def _post_bwd(dx, h, g, alpha, tag):
    T, D = dx.shape
    tm = _tile(T, ROW_TILE)

    def body(dx_ref, h_ref, g_ref, dh_ref, dg_ref):
        hv = h_ref[...]
        r = lax.rsqrt(jnp.mean(hv * hv, axis=-1, keepdims=True) + RMS_EPS)
        hh = hv * r
        dyn = alpha * dx_ref[...]
        _accumulate(dg_ref, jnp.sum(dyn * hh, axis=0, keepdims=True), pl.program_id(0) == 0)
        dhh = dyn * g_ref[...]
        dh = r * (dhh - hh * jnp.mean(dhh * hh, axis=-1, keepdims=True))
        dh_ref[...] = dh.astype(dh_ref.dtype)

    return pl.pallas_call(
        body, name=f"post_bwd_{tag}", grid=(T // tm,),
        in_specs=[_rows(tm, D), _rows(tm, D), _whole((1, D))],
        out_specs=[_rows(tm, D), _whole((1, D))],
        out_shape=[_sds((T, D), MM_DTYPE), _sds((1, D), F32)],
        compiler_params=_params())(dx, h, g.reshape(1, D))


def _pre_bwd(dres, dxn, x, g, tag):
    T, D = x.shape
    tm = _tile(T, ROW_TILE)

    def body(dres_ref, dxn_ref, x_ref, g_ref, dx_ref, dg_ref):
        xv = x_ref[...]
        r = lax.rsqrt(jnp.mean(xv * xv, axis=-1, keepdims=True) + RMS_EPS)
        xh = xv * r
        dn = dxn_ref[...]
        _accumulate(dg_ref, jnp.sum(dn * xh, axis=0, keepdims=True), pl.program_id(0) == 0)
        dxh = dn * g_ref[...]
        dx_ref[...] = dres_ref[...] + r * (dxh - xh * jnp.mean(dxh * xh, axis=-1, keepdims=True))

    return pl.pallas_call(
        body, name=f"pre_bwd_{tag}", grid=(T // tm,),
        in_specs=[_rows(tm, D), _rows(tm, D), _rows(tm, D), _whole((1, D))],
        out_specs=[_rows(tm, D), _whole((1, D))],
        out_shape=[_sds((T, D), F32), _sds((1, D), F32)],
        compiler_params=_params())(dres, dxn, x, g.reshape(1, D))


def _swiglu_fwd(hgu, tag):
    T, F2 = hgu.shape
    F = F2 // 2
    tm = _tile(T, ROW_TILE)

    def body(g_ref, u_ref, o_ref):
        g = g_ref[...].astype(F32)
        o_ref[...] = (g * jax.nn.sigmoid(g) * u_ref[...].astype(F32)).astype(o_ref.dtype)

    return pl.pallas_call(
        body, name=f"swiglu_fwd_{tag}", grid=(T // tm,),
        in_specs=[pl.BlockSpec((tm, F), lambda i: (i, 0)), pl.BlockSpec((tm, F), lambda i: (i, 1))],
        out_specs=_rows(tm, F), out_shape=_sds((T, F), MM_DTYPE), compiler_params=_params())(hgu, hgu)


def _swiglu_bwd(hgu, da, tag):
    T, F2 = hgu.shape
    F = F2 // 2
    tm = _tile(T, ROW_TILE)

    def body(h_ref, da_ref, o_ref):
        g = h_ref[:, :F].astype(F32)
        u = h_ref[:, F:].astype(F32)
        d = da_ref[...].astype(F32)
        sg = jax.nn.sigmoid(g)
        o_ref[:, :F] = (d * u * sg * (1.0 + g * (1.0 - sg))).astype(o_ref.dtype)
        o_ref[:, F:] = (d * g * sg).astype(o_ref.dtype)

    return pl.pallas_call(
        body, name=f"swiglu_bwd_{tag}", grid=(T // tm,),
        in_specs=[_rows(tm, F2), _rows(tm, F)], out_specs=_rows(tm, F2),
        out_shape=_sds((T, F2), MM_DTYPE), compiler_params=_params())(hgu, da)


def _loss_grad(y, tgt):
    T, D = y.shape
    tm = _tile(T, ROW_TILE)

    def body(y_ref, t_ref, dy_ref, l_ref):
        e = y_ref[...] - t_ref[...]
        row = jnp.mean(e * e, axis=-1, keepdims=True)
        part = jnp.broadcast_to(jnp.sum(row, axis=0, keepdims=True), (8, LANES))
        _accumulate(l_ref, part, pl.program_id(0) == 0)
        dy_ref[...] = e * (1.0 / D)

    dy, lsum = pl.pallas_call(
        body, name="loss_grad", grid=(T // tm,),
        in_specs=[_rows(tm, D), _rows(tm, D)], out_specs=[_rows(tm, D), _whole((8, LANES))],
        out_shape=[_sds((T, D), F32), _sds((8, LANES), F32)], compiler_params=_params())(y, tgt)
    return dy, 0.5 * lsum[0, 0]


def _shift_down(u, d, rows):
    return jnp.where(rows >= d, pltpu.roll(u, d, 0), 0.0)


def _shift_up(u, d, rows, S):
    return jnp.where(rows < S - d, pltpu.roll(u, S - d, 0), 0.0)


def _conv_fwd(bch, k8, Bl, S):
    T, D3 = bch.shape
    D = D3 // 3
    dc = min(D, 2 * LANES)
    nd = D // dc

    def body(b_ref, c_ref, h_ref, k_ref, z_ref):
        rows = lax.broadcasted_iota(jnp.int32, (S, 1), 0)
        u = c_ref[...].astype(F32) * h_ref[...].astype(F32)
        y = k_ref[2:3, :] * u + k_ref[1:2, :] * _shift_down(u, 1, rows) + k_ref[0:1, :] * _shift_down(u, 2, rows)
        z_ref[...] = (b_ref[...].astype(F32) * y).astype(z_ref.dtype)

    return pl.pallas_call(
        body, name="conv_fwd", grid=(Bl, nd),
        in_specs=[pl.BlockSpec((S, dc), lambda b, j: (b, j)),
                  pl.BlockSpec((S, dc), lambda b, j: (b, nd + j)),
                  pl.BlockSpec((S, dc), lambda b, j: (b, 2 * nd + j)),
                  pl.BlockSpec((8, dc), lambda b, j: (0, j))],
        out_specs=pl.BlockSpec((S, dc), lambda b, j: (b, j)),
        out_shape=_sds((T, D), MM_DTYPE), compiler_params=_params())(bch, bch, bch, k8)


def _conv_bwd(bch, dz, k8, Bl, S):
    T, D3 = bch.shape
    D = D3 // 3
    dc = min(D, 2 * LANES)
    nd = D // dc

    def body(b_ref, c_ref, h_ref, dz_ref, k_ref, db_ref, dc_ref, dh_ref, dk_ref):
        rows = lax.broadcasted_iota(jnp.int32, (S, 1), 0)
        bv = b_ref[...].astype(F32)
        cv = c_ref[...].astype(F32)
        hv = h_ref[...].astype(F32)
        dzv = dz_ref[...].astype(F32)
        u = cv * hv
        u1 = _shift_down(u, 1, rows)
        u2 = _shift_down(u, 2, rows)
        y = k_ref[2:3, :] * u + k_ref[1:2, :] * u1 + k_ref[0:1, :] * u2
        db_ref[...] = (dzv * y).astype(db_ref.dtype)
        dy = dzv * bv
        du = k_ref[2:3, :] * dy + k_ref[1:2, :] * _shift_up(dy, 1, rows, S) + k_ref[0:1, :] * _shift_up(dy, 2, rows, S)
        dc_ref[...] = (du * hv).astype(dc_ref.dtype)
        dh_ref[...] = (du * cv).astype(dh_ref.dtype)

        @pl.when(pl.program_id(1) == 0)
        def _():
            dk_ref[...] = jnp.zeros_like(dk_ref)

        dk_ref[0:1, :] += jnp.sum(dy * u2, axis=0, keepdims=True)
        dk_ref[1:2, :] += jnp.sum(dy * u1, axis=0, keepdims=True)
        dk_ref[2:3, :] += jnp.sum(dy * u, axis=0, keepdims=True)

    seq = lambda off: pl.BlockSpec((S, dc), lambda j, b: (b, off + j))
    return pl.pallas_call(
        body, name="conv_bwd", grid=(nd, Bl),
        in_specs=[seq(0), seq(nd), seq(2 * nd), seq(0), pl.BlockSpec((8, dc), lambda j, b: (0, j))],
        out_specs=[seq(0), seq(0), seq(0), pl.BlockSpec((8, dc), lambda j, b: (0, j))],
        out_shape=[_sds((T, D), MM_DTYPE)] * 3 + [_sds((8, D), F32)],
        compiler_params=_params())(bch, bch, bch, dz, k8)


def _forget_fwd(pf, fb, Bl, S):
    T = pf.shape[0]

    def body(p_ref, fb_ref, c_ref):
        rows = lax.broadcasted_iota(jnp.int32, (S, 1), 0)
        z = p_ref[...] + fb_ref[...]
        acc = jnp.minimum(z, 0.0) - jnp.log1p(jnp.exp(-jnp.abs(z)))
        d = 1
        while d < S:
            acc = acc + _shift_down(acc, d, rows)
            d *= 2
        c_ref[...] = acc

    return pl.pallas_call(
        body, name="forget_fwd", grid=(Bl,),
        in_specs=[_rows(S, LANES), _whole((1, LANES))], out_specs=_rows(S, LANES),
        out_shape=_sds((T, LANES), F32), compiler_params=_params())(pf, fb)


def _forget_bwd(dc, pf, fb, Bl, S):
    T = pf.shape[0]

    def body(dc_ref, p_ref, fb_ref, df_ref, dfb_ref):
        rows = lax.broadcasted_iota(jnp.int32, (S, 1), 0)
        acc = dc_ref[...]
        d = 1
        while d < S:
            acc = acc + _shift_up(acc, d, rows, S)
            d *= 2
        df = acc * jax.nn.sigmoid(-(p_ref[...] + fb_ref[...]))
        df_ref[...] = df.astype(df_ref.dtype)
        _accumulate(dfb_ref, jnp.sum(df, axis=0, keepdims=True), pl.program_id(0) == 0)

    return pl.pallas_call(
        body, name="forget_bwd", grid=(Bl,),
        in_specs=[_rows(S, LANES), _rows(S, LANES), _whole((1, LANES))],
        out_specs=[_rows(S, LANES), _whole((1, LANES))],
        out_shape=[_sds((T, LANES), MM_DTYPE), _sds((1, LANES), F32)],
        compiler_params=_params())(dc, pf, fb)


def _head_mask(h):
    lane = lax.broadcasted_iota(jnp.int32, (1, LANES), 1)
    return (lane >= h * HEAD_DIM) & (lane < (h + 1) * HEAD_DIM)


def _attn_fwd(qg, kv, c_col, c_row, Bl, S, D):
    T = Bl * S
    H = D // HEAD_DIM
    HP = D // LANES
    bq = min(S, ATT_BLOCK)
    nq = S // bq
    scale = 1.0 / math.sqrt(HEAD_DIM)

    def body(q_ref, k_ref, v_ref, cc_ref, cr_ref, o_ref, lse_ref):
        i = pl.program_id(2)
        q2 = q_ref[...]
        qh = [q2 * (_head_mask(h).astype(F32) * scale).astype(q2.dtype) for h in range(2)]
        cc = [cc_ref[h][:, :1] for h in range(2)]
        diag = lax.broadcasted_iota(jnp.int32, (1, bq), 1) <= lax.broadcasted_iota(jnp.int32, (bq, 1), 0)

        def block(j, carry, on_diagonal):
            off = pl.multiple_of(j * bq, bq)
            kj = k_ref[pl.ds(off, bq), :]
            vj = v_ref[pl.ds(off, bq), :]
            new = []
            for h in range(2):
                m, l, acc = carry[h]
                s = lax.dot_general(qh[h], kj, NT, preferred_element_type=F32) + cc[h] - cr_ref[h, j]
                if on_diagonal:
                    s = jnp.where(diag, s, -jnp.inf)
                m_new = jnp.maximum(m, jnp.max(s, axis=1, keepdims=True))
                p = jnp.exp(s - m_new)
                a = jnp.exp(m - m_new)
                l = a * l + jnp.sum(p, axis=1, keepdims=True)
                acc = a * acc + jnp.dot(p.astype(MM_DTYPE), vj, preferred_element_type=F32)
                new.append((m_new, l, acc))
            return tuple(new)

        one = (jnp.full((bq, 1), -jnp.inf, F32), jnp.zeros((bq, 1), F32), jnp.zeros((bq, LANES), F32))
        carry = lax.fori_loop(0, i, lambda j, c: block(j, c, False), (one, one))
        carry = block(i, carry, True)
        outs = []
        for h in range(2):
            m, l, acc = carry[h]
            outs.append(acc / l)
            lse_ref[h] = jnp.broadcast_to(m + jnp.log(l), (bq, LANES))
        o_ref[...] = jnp.where(_head_mask(0), outs[0], outs[1])

    return pl.pallas_call(
        body, name="attn_fwd", grid=(Bl, HP, nq),
        in_specs=[pl.BlockSpec((bq, LANES), lambda b, hp, i: (b * nq + i, hp)),
                  pl.BlockSpec((S, LANES), lambda b, hp, i: (b, hp)),
                  pl.BlockSpec((S, LANES), lambda b, hp, i: (b, HP + hp)),
                  pl.BlockSpec((None, 2, bq, LANES), lambda b, hp, i: (b, hp, i, 0)),
                  pl.BlockSpec((None, 2, nq, 1, bq), lambda b, hp, i: (b, hp, 0, 0, 0))],
        out_specs=[pl.BlockSpec((bq, LANES), lambda b, hp, i: (b * nq + i, hp)),
                   pl.BlockSpec((None, 2, bq, LANES), lambda b, hp, i: (b, hp, i, 0))],
        out_shape=[_sds((T, D), F32), _sds((Bl, H, S, LANES), F32)],
        compiler_params=_params())(qg, kv, kv, c_col, c_row)


def _attn_bwd(qg, kv, do, lse, c_col, c_row, Bl, S, D):
    T = Bl * S
    H = D // HEAD_DIM
    HP = D // LANES
    bq = min(S, ATT_BLOCK)
    nq = S // bq
    scale = 1.0 / math.sqrt(HEAD_DIM)

    def body(q_ref, k_ref, v_ref, do_ref, lse_ref, cc_ref, cr_ref, dq_ref, dk_ref, dv_ref, dcr_ref, p_sc, dp_sc):
        i = pl.program_id(2)

        @pl.when(i == 0)
        def _():
            dk_ref[...] = jnp.zeros_like(dk_ref)
            dv_ref[...] = jnp.zeros_like(dv_ref)
            dcr_ref[...] = jnp.zeros_like(dcr_ref)

        q2 = q_ref[...]
        do2 = do_ref[...]
        masks = [_head_mask(h).astype(F32) for h in range(2)]
        qh = [q2 * (masks[h] * scale).astype(q2.dtype) for h in range(2)]
        doh = [do2 * masks[h].astype(do2.dtype) for h in range(2)]
        cc = [cc_ref[h][:, :1] for h in range(2)]
        lse = [lse_ref[h][:, :1] for h in range(2)]
        diag = lax.broadcasted_iota(jnp.int32, (1, bq), 1) <= lax.broadcasted_iota(jnp.int32, (bq, 1), 0)

        def sweep1(j, delta, on_diagonal):
            off = pl.multiple_of(j * bq, bq)
            kj = k_ref[pl.ds(off, bq), :]
            vj = v_ref[pl.ds(off, bq), :]
            new = []
            dv = None
            for h in range(2):
                s = lax.dot_general(qh[h], kj, NT, preferred_element_type=F32) + cc[h] - cr_ref[h, j]
                if on_diagonal:
                    s = jnp.where(diag, s, -jnp.inf)
                p = jnp.exp(s - lse[h])
                dp = lax.dot_general(doh[h], vj, NT, preferred_element_type=F32)
                p_sc[h, j] = p
                dp_sc[h, j] = dp
                part = lax.dot_general(p.astype(MM_DTYPE), doh[h], TN, preferred_element_type=F32)
                dv = part if dv is None else dv + part
                new.append(delta[h] + jnp.sum(p * dp, axis=1, keepdims=True))
            dv_ref[pl.ds(off, bq), :] += dv
            return tuple(new)

        zero = jnp.zeros((bq, 1), F32)
        delta = lax.fori_loop(0, i, lambda j, d: sweep1(j, d, False), (zero, zero))
        delta = sweep1(i, delta, True)

        def sweep2(j, dq):
            off = pl.multiple_of(j * bq, bq)
            kj = k_ref[pl.ds(off, bq), :]
            dk = None
            for h in range(2):
                ds = p_sc[h, j] * (dp_sc[h, j] - delta[h])
                dcr_ref[h, j] -= jnp.sum(ds, axis=0, keepdims=True)
                dsb = ds.astype(MM_DTYPE)
                dq = dq + jnp.dot(dsb, kj * (masks[h] * scale).astype(kj.dtype), preferred_element_type=F32)
                part = lax.dot_general(dsb, qh[h], TN, preferred_element_type=F32)
                dk = part if dk is None else dk + part
            dk_ref[pl.ds(off, bq), :] += dk
            return dq

        dq_ref[...] = lax.fori_loop(0, i + 1, sweep2, jnp.zeros((bq, LANES), F32))

    blk = lambda col: pl.BlockSpec((bq, LANES), lambda b, hp, i: (b * nq + i, col(hp)))
    seq = lambda col: pl.BlockSpec((S, LANES), lambda b, hp, i: (b, col(hp)))
    per_head = pl.BlockSpec((None, 2, bq, LANES), lambda b, hp, i: (b, hp, i, 0))
    rows = pl.BlockSpec((None, 2, nq, 1, bq), lambda b, hp, i: (b, hp, 0, 0, 0))
    return pl.pallas_call(
        body, name="attn_bwd", grid=(Bl, HP, nq),
        in_specs=[blk(lambda hp: hp), seq(lambda hp: hp), seq(lambda hp: HP + hp), blk(lambda hp: hp),
                  per_head, per_head, rows],
        out_specs=[blk(lambda hp: hp), seq(lambda hp: hp), seq(lambda hp: hp), rows],
        out_shape=[_sds((T, D), F32), _sds((T, D), F32), _sds((T, D), F32), _sds((Bl, H, nq, 1, bq), F32)],
        scratch_shapes=[pltpu.VMEM((2, nq, bq, bq), F32), pltpu.VMEM((2, nq, bq, bq), F32)],
        compiler_params=_params())(qg, kv, kv, do, lse, c_col, c_row)


def _gate_fwd(qg, o):
    T, D = o.shape
    tm = _tile(T, ROW_TILE)

    def body(g_ref, o_ref, z_ref):
        z_ref[...] = (jax.nn.sigmoid(g_ref[...].astype(F32)) * o_ref[...]).astype(z_ref.dtype)

    return pl.pallas_call(
        body, name="gate_fwd", grid=(T // tm,),
        in_specs=[pl.BlockSpec((tm, D), lambda i: (i, 1)), _rows(tm, D)], out_specs=_rows(tm, D),
        out_shape=_sds((T, D), MM_DTYPE), compiler_params=_params())(qg, o)


def _gate_do(dz, qg):
    T, D = dz.shape
    tm = _tile(T, ROW_TILE)

    def body(dz_ref, g_ref, do_ref):
        do_ref[...] = (dz_ref[...].astype(F32) * jax.nn.sigmoid(g_ref[...].astype(F32))).astype(do_ref.dtype)

    return pl.pallas_call(
        body, name="gate_do", grid=(T // tm,),
        in_specs=[_rows(tm, D), pl.BlockSpec((tm, D), lambda i: (i, 1))], out_specs=_rows(tm, D),
        out_shape=_sds((T, D), MM_DTYPE), compiler_params=_params())(dz, qg)


def _gate_bwd(dz, qg, o, dq):
    T, D = dz.shape
    tm = _tile(T, ROW_TILE)

    def body(dz_ref, g_ref, o_ref, dq_ref, out_ref):
        g = g_ref[...].astype(F32)
        sg = jax.nn.sigmoid(g)
        out_ref[:, :D] = dq_ref[...].astype(out_ref.dtype)
        out_ref[:, D:] = (dz_ref[...].astype(F32) * o_ref[...] * sg * (1.0 - sg)).astype(out_ref.dtype)

    return pl.pallas_call(
        body, name="gate_bwd", grid=(T // tm,),
        in_specs=[_rows(tm, D), pl.BlockSpec((tm, D), lambda i: (i, 1)), _rows(tm, D), _rows(tm, D)],
        out_specs=_rows(tm, 2 * D), out_shape=_sds((T, 2 * D), MM_DTYPE),
        compiler_params=_params())(dz, qg, o, dq)


def _mm_in(a, wg, out_dtype, tag, l=None):
    T, K = a.shape
    n = wg.shape[-1]
    tm = _tile(T, MM_TILE)
    if l is None:
        w_spec = _whole((N_CHIP, K, n))
    else:
        w_spec = pl.BlockSpec((None, N_CHIP, K, n), lambda i: (l, 0, 0, 0))

    def body(a_ref, w_ref, o_ref):
        av = a_ref[...]
        for s in range(N_CHIP):
            o_ref[:, s * n:(s + 1) * n] = jnp.dot(av, w_ref[s], preferred_element_type=F32).astype(o_ref.dtype)

    return pl.pallas_call(
        body, name=f"mm_in_{tag}", grid=(T // tm,),
        in_specs=[_rows(tm, K), w_spec], out_specs=_rows(tm, N_CHIP * n),
        out_shape=_sds((T, N_CHIP * n), out_dtype), compiler_params=_params())(a, wg)


def _mm_nt_in(dy, wg, tag, l=None):
    T = dy.shape[0]
    K, n = wg.shape[-2:]
    tm = _tile(T, MM_TILE)
    if l is None:
        w_spec = _whole((N_CHIP, K, n))
    else:
        w_spec = pl.BlockSpec((None, N_CHIP, K, n), lambda i: (l, 0, 0, 0))

    def body(d_ref, w_ref, o_ref):
        acc = None
        for s in range(N_CHIP):
            part = lax.dot_general(d_ref[:, s * n:(s + 1) * n], w_ref[s], NT, preferred_element_type=F32)
            acc = part if acc is None else acc + part
        o_ref[...] = acc

    return pl.pallas_call(
        body, name=f"mm_nt_in_{tag}", grid=(T // tm,),
        in_specs=[_rows(tm, N_CHIP * n), w_spec], out_specs=_rows(tm, K),
        out_shape=_sds((T, K), F32), compiler_params=_params())(dy, wg)


def _mm_nn(a, b, out_dtype, tag):
    T, K = a.shape
    N = b.shape[1]
    tm = _tile(T, MM_TILE)

    def body(a_ref, b_ref, o_ref):
        o_ref[...] = jnp.dot(a_ref[...], b_ref[...], preferred_element_type=F32).astype(o_ref.dtype)

    return pl.pallas_call(
        body, name=f"mm_nn_{tag}", grid=(T // tm,),
        in_specs=[_rows(tm, K), _whole((K, N))], out_specs=_rows(tm, N),
        out_shape=_sds((T, N), out_dtype), compiler_params=_params())(a, b)


def _mm_nt(a, b, out_dtype, tag):
    T, C = a.shape
    N = b.shape[0]
    tm = _tile(T, MM_TILE)
    nb = N
    for cand in (1408, 1024):
        if N > cand and N % cand == 0:
            nb = cand
            break

    def body(a_ref, b_ref, o_ref):
        o_ref[...] = lax.dot_general(a_ref[...], b_ref[...], NT, preferred_element_type=F32).astype(o_ref.dtype)

    return pl.pallas_call(
        body, name=f"mm_nt_{tag}", grid=(N // nb, T // tm),
        in_specs=[pl.BlockSpec((tm, C), lambda j, i: (i, 0)), pl.BlockSpec((nb, C), lambda j, i: (j, 0))],
        out_specs=pl.BlockSpec((tm, nb), lambda j, i: (i, j)),
        out_shape=_sds((T, N), out_dtype), compiler_params=_params())(a, b)


def _mm_tn_in(a, dy, tag, l=None, prev=None):
    T, K = a.shape
    n = dy.shape[1] // N_CHIP
    tt = _tile(T, TN_TILE)

    def body(a_ref, d_ref, *rest):
        o_ref = rest[-1]
        part = lax.dot_general(a_ref[...], d_ref[...], TN, preferred_element_type=F32)
        _accumulate(o_ref, part, pl.program_id(1) == 0)

    in_specs = [pl.BlockSpec((tt, K), lambda s, t: (t, 0)), pl.BlockSpec((tt, n), lambda s, t: (t, s))]
    args = [a, dy]
    kw = {}
    if l is None:
        out_spec = pl.BlockSpec((None, K, n), lambda s, t: (s, 0, 0))
        out_shape = _sds((N_CHIP, K, n), F32)
    else:
        out_spec = pl.BlockSpec((None, None, K, n), lambda s, t: (l, s, 0, 0))
        out_shape = _sds((2, N_CHIP, K, n), F32)
        if prev is not None:
            in_specs.append(ANY)
            args.append(prev)
            kw["input_output_aliases"] = {2: 0}
    return pl.pallas_call(
        body, name=f"mm_tn_in_{tag}", grid=(N_CHIP, T // tt),
        in_specs=in_specs, out_specs=out_spec, out_shape=out_shape,
        compiler_params=_params(), **kw)(*args)


def _mm_tn_out(act, dh, tag, l=None, prev=None):
    T, R4 = act.shape
    D = dh.shape[1]
    r = R4 // N_CHIP
    g = 1 if r % LANES == 0 else 2
    tt = _tile(T, TN_TILE)

    def body(a_ref, d_ref, *rest):
        o_ref = rest[-1]
        part = lax.dot_general(a_ref[...], d_ref[...], TN, preferred_element_type=F32)
        first = pl.program_id(1) == 0
        for q in range(g):
            _accumulate(o_ref.at[q], part[q * r:(q + 1) * r], first)

    in_specs = [pl.BlockSpec((tt, g * r), lambda s, t: (t, s)), pl.BlockSpec((tt, D), lambda s, t: (t, 0))]
    args = [act, dh]
    kw = {}
    if l is None:
        out_spec = pl.BlockSpec((g, r, D), lambda s, t: (s, 0, 0))
        out_shape = _sds((N_CHIP, r, D), F32)
    else:
        out_spec = pl.BlockSpec((None, g, r, D), lambda s, t: (l, s, 0, 0))
        out_shape = _sds((2, N_CHIP, r, D), F32)
        if prev is not None:
            in_specs.append(ANY)
            args.append(prev)
            kw["input_output_aliases"] = {2: 0}
    return pl.pallas_call(
        body, name=f"mm_tn_out_{tag}", grid=(N_CHIP // g, T // tt),
        in_specs=in_specs, out_specs=out_spec, out_shape=out_shape,
        compiler_params=_params(), **kw)(*args)


def _mm_tn(a, b, tag):
    T, K = a.shape
    N = b.shape[1]
    tt = _tile(T, TN_TILE)

    def body(a_ref, b_ref, o_ref):
        part = lax.dot_general(a_ref[...], b_ref[...], TN, preferred_element_type=F32)
        _accumulate(o_ref, part, pl.program_id(0) == 0)

    return pl.pallas_call(
        body, name=f"mm_tn_{tag}", grid=(T // tt,),
        in_specs=[_rows(tt, K), _rows(tt, N)], out_specs=_whole((K, N)),
        out_shape=_sds((K, N), F32), compiler_params=_params())(a, b)


def _norm_mm_in(x, g, wg, tag, swiglu=False):
    T, D = x.shape
    n = wg.shape[-1]
    tm = _tile(T, FUSED_TILE)
    half = N_CHIP // 2

    def body(x_ref, g_ref, w_ref, xn_ref, y_ref, *rest):
        xv = x_ref[...]
        r = lax.rsqrt(jnp.mean(xv * xv, axis=-1, keepdims=True) + RMS_EPS)
        xn = (xv * r * g_ref[...]).astype(xn_ref.dtype)
        xn_ref[...] = xn

        def product(s):
            p = jnp.dot(xn, w_ref[s], preferred_element_type=F32)
            y_ref[:, s * n:(s + 1) * n] = p.astype(y_ref.dtype)
            return p

        if swiglu:
            for q in range(half):
                gate, up = product(q), product(half + q)
                rest[0][:, q * n:(q + 1) * n] = (gate * jax.nn.sigmoid(gate) * up).astype(rest[0].dtype)
        else:
            for s in range(N_CHIP):
                product(s)

    out_specs = [_rows(tm, D), _rows(tm, N_CHIP * n)]
    out_shape = [_sds((T, D), MM_DTYPE), _sds((T, N_CHIP * n), MM_DTYPE)]
    if swiglu:
        out_specs.append(_rows(tm, half * n))
        out_shape.append(_sds((T, half * n), MM_DTYPE))
    return pl.pallas_call(
        body, name=f"norm_mm_in_{tag}", grid=(T // tm,),
        in_specs=[_rows(tm, D), _whole((1, D)), _resident((N_CHIP, D, n))], out_specs=out_specs,
        out_shape=out_shape, compiler_params=_params())(x, g.reshape(1, D), wg)


def _mm_out_post(a, b, x, g, alpha, tag):
    T, K = a.shape
    D = b.shape[1]
    tm = _tile(T, FUSED_TILE)

    def body(a_ref, b_ref, x_ref, g_ref, h_ref, o_ref):
        hv = jnp.dot(a_ref[...], b_ref[...], preferred_element_type=F32)
        h_ref[...] = hv
        r = lax.rsqrt(jnp.mean(hv * hv, axis=-1, keepdims=True) + RMS_EPS)
        o_ref[...] = x_ref[...] + alpha * (hv * r * g_ref[...])

    return pl.pallas_call(
        body, name=f"mm_out_post_{tag}", grid=(T // tm,),
        in_specs=[_rows(tm, K), _resident((K, D)), _rows(tm, D), _whole((1, D))],
        out_specs=[_rows(tm, D), _rows(tm, D)], out_shape=[_sds((T, D), F32)] * 2,
        compiler_params=_params())(a, b, x, g.reshape(1, D))


def _post_bwd_mm(dx, h, g, alpha, b, tag, hgu=None):
    T, D = dx.shape
    K = b.shape[0]
    tm = _tile(T, FUSED_TILE)

    def body(dx_ref, h_ref, g_ref, b_ref, *rest):
        dh_ref, dg_ref, out_ref = rest[-3:]
        hv = h_ref[...]
        r = lax.rsqrt(jnp.mean(hv * hv, axis=-1, keepdims=True) + RMS_EPS)
        hh = hv * r
        dyn = alpha * dx_ref[...]
        _accumulate(dg_ref, jnp.sum(dyn * hh, axis=0, keepdims=True), pl.program_id(0) == 0)
        dhh = dyn * g_ref[...]
        dh = (r * (dhh - hh * jnp.mean(dhh * hh, axis=-1, keepdims=True))).astype(dh_ref.dtype)
        dh_ref[...] = dh
        da = lax.dot_general(dh, b_ref[...], NT, preferred_element_type=F32)
        if hgu is None:
            out_ref[...] = da.astype(out_ref.dtype)
        else:
            gate = rest[0][:, :K].astype(F32)
            up = rest[0][:, K:].astype(F32)
            sg = jax.nn.sigmoid(gate)
            out_ref[:, :K] = (da * up * sg * (1.0 + gate * (1.0 - sg))).astype(out_ref.dtype)
            out_ref[:, K:] = (da * gate * sg).astype(out_ref.dtype)

    in_specs = [_rows(tm, D), _rows(tm, D), _whole((1, D)), _resident((K, D))]
    args = [dx, h, g.reshape(1, D), b]
    wide = K
    if hgu is not None:
        wide = 2 * K
        in_specs.append(_rows(tm, wide))
        args.append(hgu)
    return pl.pallas_call(
        body, name=f"post_bwd_mm_{tag}", grid=(T // tm,), in_specs=in_specs,
        out_specs=[_rows(tm, D), _whole((1, D)), _rows(tm, wide)],
        out_shape=[_sds((T, D), MM_DTYPE), _sds((1, D), F32), _sds((T, wide), MM_DTYPE)],
        compiler_params=_params())(*args)


def _mm_nt_pre(dy, w, dres, x, g, tag):
    T, C = dy.shape
    D = x.shape[1]
    tm = _tile(T, FUSED_TILE)
    n = w.shape[-1]

    def body(dy_ref, w_ref, dres_ref, x_ref, g_ref, dx_ref, dg_ref):
        if w.ndim == 2:
            dn = lax.dot_general(dy_ref[...], w_ref[...], NT, preferred_element_type=F32)
        else:
            dn = None
            for s in range(N_CHIP):
                part = lax.dot_general(dy_ref[:, s * n:(s + 1) * n], w_ref[s], NT, preferred_element_type=F32)
                dn = part if dn is None else dn + part
        xv = x_ref[...]
        r = lax.rsqrt(jnp.mean(xv * xv, axis=-1, keepdims=True) + RMS_EPS)
        xh = xv * r
        _accumulate(dg_ref, jnp.sum(dn * xh, axis=0, keepdims=True), pl.program_id(0) == 0)
        dxh = dn * g_ref[...]
        dx_ref[...] = dres_ref[...] + r * (dxh - xh * jnp.mean(dxh * xh, axis=-1, keepdims=True))

    return pl.pallas_call(
        body, name=f"mm_nt_pre_{tag}", grid=(T // tm,),
        in_specs=[_rows(tm, C), _resident(w.shape), _rows(tm, D), _rows(tm, D), _whole((1, D))],
        out_specs=[_rows(tm, D), _whole((1, D))], out_shape=[_sds((T, D), F32), _sds((1, D), F32)],
        compiler_params=_params())(dy, w, dres, x, g.reshape(1, D))


def _adamw(w, g, m, v, tag):
    R, C = w.shape
    tr = _tile(R, ROW_TILE)

    def body(w_ref, g_ref, m_ref, v_ref, d_ref, mo_ref, vo_ref):
        gv = g_ref[...]
        mn = ADAM_B1 * m_ref[...] + (1.0 - ADAM_B1) * gv
        vn = ADAM_B2 * v_ref[...] + (1.0 - ADAM_B2) * (gv * gv)
        m_hat = mn / (1.0 - ADAM_B1 ** ADAM_STEP)
        v_hat = vn / (1.0 - ADAM_B2 ** ADAM_STEP)
        d_ref[...] = -ADAM_LR * (m_hat / (jnp.sqrt(v_hat) + ADAM_EPS) + ADAM_WD * w_ref[...])
        mo_ref[...] = mn
        vo_ref[...] = vn

    return pl.pallas_call(
        body, name=f"adamw_{tag}", grid=(R // tr,),
        in_specs=[_rows(tr, C)] * 4, out_specs=[_rows(tr, C)] * 3,
        out_shape=[_sds((R, C), F32)] * 3, compiler_params=_params())(w, g, m, v)


def _sum_devices(gall):
    _, R, C = gall.shape

    def body(g_ref, o_ref):
        acc = g_ref[0]
        for d in range(1, N_DEV):
            acc = acc + g_ref[d]
        o_ref[...] = acc

    return pl.pallas_call(
        body, name="sum_devices", in_specs=[_whole((N_DEV, R, C))], out_specs=_whole((R, C)),
        out_shape=_sds((R, C), F32), grid=(1,), compiler_params=_params())(gall)


HBM = pl.BlockSpec(memory_space=pltpu.HBM)
SEM = pl.BlockSpec(memory_space=pltpu.SEMAPHORE)
EFFECT = pltpu.SideEffectType.DATAFLOW_SIDE_EFFECTING


def _place():
    x, y, c = lax.axis_index("x"), lax.axis_index("y"), lax.axis_index("c")
    chips = ((1 - x, y), (x, 1 - y), (1 - x, 1 - y))
    return x, y, c, chips


def _remote(src, dst, send_sem, recv_sem, dev):
    return pltpu.make_async_remote_copy(src_ref=src, dst_ref=dst, send_sem=send_sem, recv_sem=recv_sem,
                                        device_id=dev, device_id_type=MESH)


def _in_hbm(a):
    return pltpu.with_memory_space_constraint(a, pltpu.HBM)


def _own_slot(w4, l, dtype, place, tag):
    _, _, r, col = w4.shape
    tr = _tile(r, 2 * ROW_TILE)

    def body(place_ref, x_ref, o_ref):
        o_ref[...] = x_ref[...].astype(o_ref.dtype)

    grid_spec = pltpu.PrefetchScalarGridSpec(
        num_scalar_prefetch=1, grid=(2, r // tr),
        in_specs=[pl.BlockSpec((None, None, tr, col), lambda h, i, p: (l, h, i, 0))],
        out_specs=pl.BlockSpec((None, None, tr, col), lambda h, i, p: (p[1], h, i, 0)))
    return pl.pallas_call(
        body, name=f"own_slot_{tag}", grid_spec=grid_spec, out_shape=_sds((N_CHIP, 2, r, col), dtype),
        compiler_params=_params())(place, w4)


def _gather_start(bufs, after, tag):
    n = len(bufs)

    def body(*refs):
        ins = refs[:n]
        s_sem, r_sem, token = refs[n + 1], refs[n + 2], refs[2 * n + 3]
        x, y, c, chips = _place()
        me = 2 * x + y
        for i in range(n):
            mine = ins[i].at[me, c]
            for j, (px, py) in enumerate(chips):
                _remote(mine, mine, s_sem.at[3 * i + j], r_sem.at[3 * i + j], (px, py, c)).start()
        token[...] = jnp.zeros_like(token)

    dma = pltpu.SemaphoreType.DMA
    res = pl.pallas_call(
        body, name=f"gather_start_{tag}", in_specs=[HBM] * n + [ANY],
        out_specs=[SEM, SEM] + [HBM] * n + [pl.BlockSpec(memory_space=pltpu.VMEM)],
        out_shape=[dma((3 * n,)), dma((3 * n,))] + [pltpu.HBM(b.shape, b.dtype) for b in bufs] + [_sds((8, LANES), F32)],
        input_output_aliases={i: i + 2 for i in range(n)},
        compiler_params=pltpu.CompilerParams(has_side_effects=EFFECT),
        )(*[_in_hbm(b) for b in bufs], after)
    return res[0], res[1], list(res[2:2 + n]), res[-1]


def _gather_pass(s_sem, r_sem, bufs, first, after, tag):
    n = len(bufs)

    def body(*refs):
        ins = refs[:n]
        a_s, a_r, b_s, b_r = refs[n], refs[n + 1], refs[n + 3], refs[n + 4]
        x, y, c, chips = _place()
        me = 2 * x + y
        sib = (x, y, 1 - c)
        for i in range(n):
            mine = ins[i].at[me, c]
            for j, (px, py) in enumerate(chips):
                k = 3 * (first + i) + j
                _remote(mine, mine, a_s.at[k], a_r.at[k], (px, py, c)).wait_send()
        for j, (px, py) in enumerate(chips):
            for i in range(n):
                k = 3 * (first + i) + j
                blk = ins[i].at[2 * px + py, c]
                _remote(blk, blk, a_s.at[k], a_r.at[k], (px, py, c)).wait_recv()
                _remote(blk, blk, b_s.at[3 * i + j], b_r.at[3 * i + j], sib).start()

    dma = pltpu.SemaphoreType.DMA
    res = pl.pallas_call(
        body, name=f"gather_pass_{tag}", in_specs=[HBM] * n + [SEM, SEM, ANY],
        out_specs=[SEM, SEM] + [HBM] * n,
        out_shape=[dma((3 * n,)), dma((3 * n,))] + [pltpu.HBM(b.shape, b.dtype) for b in bufs],
        input_output_aliases={i: i + 2 for i in range(n)},
        compiler_params=pltpu.CompilerParams(has_side_effects=EFFECT),
        )(*bufs, s_sem, r_sem, after)
    return res[0], res[1], list(res[2:])


def _gather_land(s_sem, r_sem, bufs, tag):
    n = len(bufs)

    def body(*refs):
        ins = refs[:n]
        b_s, b_r = refs[n], refs[n + 1]
        x, y, c, chips = _place()
        sib = (x, y, 1 - c)
        for j, (px, py) in enumerate(chips):
            for i in range(n):
                sent = ins[i].at[2 * px + py, c]
                got = ins[i].at[2 * px + py, 1 - c]
                _remote(sent, sent, b_s.at[3 * i + j], b_r.at[3 * i + j], sib).wait_send()
                _remote(got, got, b_s.at[3 * i + j], b_r.at[3 * i + j], sib).wait_recv()

    return list(pl.pallas_call(
        body, name=f"gather_land_{tag}", in_specs=[HBM] * n + [SEM, SEM], out_specs=[HBM] * n,
        out_shape=[pltpu.HBM(b.shape, b.dtype) for b in bufs],
        input_output_aliases={i: i for i in range(n)},
        compiler_params=pltpu.CompilerParams(has_side_effects=EFFECT),
        )(*bufs, s_sem, r_sem))


def _rs_pair_send(grads):
    n = len(grads)

    def body(*refs):
        ins, outs = refs[:n], refs[n:2 * n]
        s_sem, r_sem = refs[2 * n:]
        x, y, c, _ = _place()
        sib = (x, y, 1 - c)
        sends = []
        for i in range(n):
            cp = _remote(ins[i].at[:, 1 - c], outs[i], s_sem.at[i], r_sem.at[i], sib)
            cp.start()
            sends.append(cp)
        for cp in sends:
            cp.wait()

    out_shape = [_sds((N_CHIP,) + g.shape[2:], g.dtype) for g in grads]
    dma = pltpu.SemaphoreType.DMA
    return pl.pallas_call(
        body, name=f"rs_pair_send_{n}", in_specs=[ANY] * n, out_specs=[ANY] * n, out_shape=out_shape,
        scratch_shapes=[dma((n,)), dma((n,))],
        )(*grads)


def _rs_pair_add(g, recv, place, tag):
    r, col = g.shape[-2:]
    tr = _tile(r, ROW_TILE)

    def body(place_ref, g_ref, r_ref, wire_ref, own_ref):
        tot = g_ref[...] + r_ref[...]
        wire_ref[...] = tot.astype(wire_ref.dtype)

        @pl.when(pl.program_id(1) == place_ref[1])
        def _():
            own_ref[...] = tot

    grid_spec = pltpu.PrefetchScalarGridSpec(
        num_scalar_prefetch=1, grid=(r // tr, N_CHIP),
        in_specs=[pl.BlockSpec((None, None, tr, col), lambda i, s, p: (s, p[0], i, 0)),
                  pl.BlockSpec((None, tr, col), lambda i, s, p: (s, i, 0))],
        out_specs=[pl.BlockSpec((None, tr, col), lambda i, s, p: (s, i, 0)),
                   pl.BlockSpec((tr, col), lambda i, s, p: (i, 0))])
    return pl.pallas_call(
        body, name=f"rs_pair_add_{tag}", grid_spec=grid_spec,
        out_shape=[_sds((N_CHIP, r, col), WIRE_DTYPE), _sds((r, col), F32)],
        compiler_params=_params())(place, g, recv)


def _rs_start(wires, tag):
    n = len(wires)

    def body(*refs):
        ins = refs[:n]
        s_sem, r_sem, token = refs[2 * n], refs[2 * n + 1], refs[4 * n + 2]
        x, y, c, chips = _place()
        for i in range(n):
            land = refs[n + i]
            for j, (px, py) in enumerate(chips):
                _remote(ins[i].at[2 * px + py], land.at[j], s_sem.at[3 * i + j], r_sem.at[3 * i + j], (px, py, c)).start()
        token[...] = jnp.zeros_like(token)

    lands = [lax.empty((3,) + w.shape[1:], w.dtype) for w in wires]
    both = list(wires) + lands
    dma = pltpu.SemaphoreType.DMA
    res = pl.pallas_call(
        body, name=f"rs_start_{tag}", in_specs=[HBM] * (2 * n),
        out_specs=[SEM, SEM] + [HBM] * (2 * n) + [pl.BlockSpec(memory_space=pltpu.VMEM)],
        out_shape=[dma((3 * n,)), dma((3 * n,))] + [pltpu.HBM(b.shape, b.dtype) for b in both] + [_sds((8, LANES), F32)],
        input_output_aliases={i: i + 2 for i in range(2 * n)},
        compiler_params=pltpu.CompilerParams(has_side_effects=EFFECT),
        )(*[_in_hbm(b) for b in both])
    return res[0], res[1], res[2:2 + n], res[2 + n:2 + 2 * n], res[-1]


def _rs_wait(s_sem, r_sem, wires, lands, after, tag):
    n = len(wires)

    def body(*refs):
        ins = refs[:n]
        s_ref, r_ref = refs[2 * n], refs[2 * n + 1]
        x, y, c, chips = _place()
        for i in range(n):
            land = refs[n + i]
            for j, (px, py) in enumerate(chips):
                cp = _remote(ins[i].at[2 * px + py], land.at[j], s_ref.at[3 * i + j], r_ref.at[3 * i + j], (px, py, c))
                cp.wait_send()
                cp.wait_recv()

    both = list(wires) + list(lands)
    res = pl.pallas_call(
        body, name=f"rs_wait_{tag}", in_specs=[HBM] * (2 * n) + [SEM, SEM, ANY], out_specs=[HBM] * (2 * n),
        out_shape=[pltpu.HBM(b.shape, b.dtype) for b in both],
        input_output_aliases={i: i for i in range(2 * n)},
        compiler_params=pltpu.CompilerParams(has_side_effects=EFFECT),
        )(*both, s_sem, r_sem, after)
    return res[n:]


def _rs_chip_send(wires, small):
    n = len(wires)

    def body(*refs):
        ins, small_ref = refs[:n], refs[n]
        outs, gall_ref = refs[n + 1:2 * n + 1], refs[2 * n + 1]
        s_sem, r_sem, s_small, r_small, s_loc = refs[2 * n + 2:]
        x, y, c, chips = _place()
        me = 4 * x + 2 * y + c
        sends = []
        loc = pltpu.make_async_copy(small_ref, gall_ref.at[me], s_loc)
        loc.start()
        for i in range(n):
            for j, (px, py) in enumerate(chips):
                cp = _remote(ins[i].at[2 * px + py], outs[i].at[j], s_sem.at[i, j], r_sem.at[i, j], (px, py, c))
                cp.start()
                sends.append(cp)
        for k in range(1, N_DEV):
            peer = (x ^ (k >> 2), y ^ ((k >> 1) & 1), c ^ (k & 1))
            cp = _remote(small_ref, gall_ref.at[me], s_small.at[k - 1], r_small.at[k - 1], peer)
            cp.start()
            sends.append(cp)
        for cp in sends:
            cp.wait()
        loc.wait()

    out_shape = [_sds((3,) + w.shape[1:], w.dtype) for w in wires] + [_sds((N_DEV,) + small.shape, small.dtype)]
    dma = pltpu.SemaphoreType.DMA
    return pl.pallas_call(
        body, name="rs_chip_send", in_specs=[ANY] * (n + 1), out_specs=[ANY] * (n + 1), out_shape=out_shape,
        scratch_shapes=[dma((n, 3)), dma((n, 3)), dma((N_DEV - 1,)), dma((N_DEV - 1,)), dma(())],
        )(*wires, small)


def _rs_chip_add(own, recv, place, l, L, prev, tag):
    r, col = own.shape
    tr = _tile(r, ROW_TILE)

    def body(place_ref, o_ref, r_ref, *rest):
        acc = o_ref[...]
        for j in range(3):
            acc = acc + r_ref[j].astype(F32)
        rest[-1][...] = acc

    in_specs = [pl.BlockSpec((tr, col), lambda i, p: (i, 0)), pl.BlockSpec((3, tr, col), lambda i, p: (0, i, 0))]
    args = [place, own, recv]
    kw = {}
    if prev is not None:
        in_specs.append(ANY)
        args.append(prev)
        kw["input_output_aliases"] = {3: 0}
    grid_spec = pltpu.PrefetchScalarGridSpec(
        num_scalar_prefetch=1, grid=(r // tr,), in_specs=in_specs,
        out_specs=pl.BlockSpec((None, None, tr, col), lambda i, p: (l, p[0], i, 0)))
    return pl.pallas_call(
        body, name=f"rs_chip_add_{tag}", grid_spec=grid_spec, out_shape=_sds((L, 2, r, col), F32),
        compiler_params=_params(), **kw)(*args)


def _rs_pair_share(fulls):
    n = len(fulls)

    def body(*refs):
        outs = refs[n:2 * n]
        s_sem, r_sem = refs[2 * n:]
        x, y, c, _ = _place()
        sib = (x, y, 1 - c)
        started = []
        for i in range(n):
            cp = _remote(outs[i].at[:, c], outs[i].at[:, c], s_sem.at[i], r_sem.at[i], sib)
            cp.start()
            started.append(cp)
        for i, cp in enumerate(started):
            cp.wait_send()
            _remote(outs[i].at[:, 1 - c], outs[i].at[:, 1 - c], s_sem.at[i], r_sem.at[i], sib).wait_recv()

    dma = pltpu.SemaphoreType.DMA
    return pl.pallas_call(
        body, name="rs_pair_share", in_specs=[ANY] * n, out_specs=[ANY] * n,
        out_shape=[_sds(f.shape, f.dtype) for f in fulls],
        input_output_aliases={i: i for i in range(n)},
        scratch_shapes=[dma((n,)), dma((n,))],
        )(*fulls)


def _ffn_fwd(x, g_pre, g_post, w_in, w_out, tag):
    xn, hgu, act = _norm_mm_in(x, g_pre, w_in, tag, swiglu=True)
    h, x_out = _mm_out_post(act, w_out.reshape(-1, w_out.shape[-1]), x, g_post, 0.5, tag)
    return x_out, (x, xn, hgu, act, h)


def _ffn_bwd(dx, saved, g_pre, g_post, w_in, w_out, tag):
    x, xn, hgu, act, h = saved
    dh, dg_post, dhgu = _post_bwd_mm(dx, h, g_post, 0.5, w_out.reshape(-1, w_out.shape[-1]), tag, hgu=hgu)
    dw_out = _mm_tn_out(act, dh, tag)
    dw_in = _mm_tn_in(xn, dhgu, tag)
    dx_in, dg_pre = _mm_nt_pre(dhgu, w_in, dx, x, g_pre, tag)
    return dx_in, dg_pre, dg_post, dw_in, dw_out


def kernel(x, ffn1_pre_g, ffn1_post_g, ffn1_w_in, ffn1_w_out, mix_pre_g, mix_post_g, ffn2_pre_g, ffn2_post_g, ffn2_w_in, ffn2_w_out, conv_w_in, conv_k, conv_w_out, kv_g, kv_w, forget_b, attn_w_qg, attn_w_o, loss_target, m_ffn1_pre_g, m_ffn1_post_g, m_ffn1_w_in, m_ffn1_w_out, m_mix_pre_g, m_mix_post_g, m_ffn2_pre_g, m_ffn2_post_g, m_ffn2_w_in, m_ffn2_w_out, m_conv_w_in, m_conv_k, m_conv_w_out, m_kv_g, m_kv_w, m_forget_b, m_attn_w_qg, m_attn_w_o, v_ffn1_pre_g, v_ffn1_post_g, v_ffn1_w_in, v_ffn1_w_out, v_mix_pre_g, v_mix_post_g, v_ffn2_pre_g, v_ffn2_post_g, v_ffn2_w_in, v_ffn2_w_out, v_conv_w_in, v_conv_k, v_conv_w_out, v_kv_g, v_kv_w, v_forget_b, v_attn_w_qg, v_attn_w_o):
    Bl, S, D = x.shape
    T = Bl * S
    H = forget_b.shape[0]
    assert D == H * HEAD_DIM and D % LANES == 0
    kvc = kv_w.shape[1]
    kvp = -(-kvc // LANES) * LANES
    kv_all = 2 * D + LANES
    dk_cols = conv_k.shape[2]
    chip = 2 * lax.axis_index("x") + lax.axis_index("y")
    core = lax.axis_index("c")

    given = dict(ffn1_w_in=(ffn1_w_in, m_ffn1_w_in, v_ffn1_w_in), ffn1_w_out=(ffn1_w_out, m_ffn1_w_out, v_ffn1_w_out),
                 ffn2_w_in=(ffn2_w_in, m_ffn2_w_in, v_ffn2_w_in), ffn2_w_out=(ffn2_w_out, m_ffn2_w_out, v_ffn2_w_out),
                 conv_w_in=(conv_w_in, m_conv_w_in, v_conv_w_in), conv_w_out=(conv_w_out, m_conv_w_out, v_conv_w_out),
                 kv_w=(kv_w, m_kv_w, v_kv_w), attn_w_qg=(attn_w_qg, m_attn_w_qg, v_attn_w_qg),
                 attn_w_o=(attn_w_o, m_attn_w_o, v_attn_w_o))
    shards = {k: w for k, (w, _, _) in given.items()}
    shards["kv_w"] = jnp.pad(kv_w, ((0, 0), (0, kvp - kvc)))[None]
    groups = [[("ffn1_w_in", 0), ("ffn1_w_out", 0)], [("conv_w_in", 0), ("conv_w_out", 0)],
              [("ffn2_w_in", 0), ("ffn2_w_out", 0)], [("kv_w", 0), ("ffn1_w_in", 1), ("ffn1_w_out", 1)],
              [("attn_w_qg", 0), ("attn_w_o", 0), ("ffn2_w_in", 1), ("ffn2_w_out", 1)]]
    first = groups[0] + groups[1] + groups[2]
    second = groups[3] + groups[4]
    place = jnp.stack([core, chip]).astype(jnp.int32)

    def slot(key, where):
        w = shards[key[0]]
        L, r, col = w.shape
        return _own_slot(w.reshape(L, 2, r // 2, col), key[1], MM_DTYPE, where, f"{key[0]}{key[1]}")

    def whole(g):
        return g.reshape(N_CHIP, -1, g.shape[-1])

    taps_slot = _own_slot(jnp.pad(conv_k[0], ((0, 13), (0, 0))).reshape(1, 2, 8, dk_cols), 0, F32, place, "conv_k")
    fb = jnp.pad(forget_b, (0, LANES - H)).reshape(1, LANES)
    s_0, r_0, fly_0, token = _gather_start([slot(key, place) for key in groups[0]] + [taps_slot], fb, "first")
    later = groups[1] + groups[2] + groups[3] + groups[4]
    s_1, r_1, fly_1, token = _gather_start([slot(key, place) for key in later], token, "rest")
    W = {}

    def arrive(g, after):
        if g == 0:
            sems, bufs, lo = (s_0, r_0), fly_0, 0
        else:
            lo = sum(len(groups[k]) for k in range(1, g))
            sems, bufs = (s_1, r_1), fly_1[lo:lo + len(groups[g])]
        got = _gather_land(*_gather_pass(*sems, bufs, lo, after, f"g{g}"), f"g{g}")
        W.update({key: whole(b) for key, b in zip(groups[g], got)})
        return got

    k_taps = arrive(0, token)[-1].reshape(N_CHIP, 16, dk_cols).transpose(1, 0, 2).reshape(16, D)[:8]

    x0 = x.reshape(T, D)
    x1, s_f1a = _ffn_fwd(x0, ffn1_pre_g[0], ffn1_post_g[0], W["ffn1_w_in", 0], W["ffn1_w_out", 0], "l0f1")
    arrive(1, x1)
    w_o_conv = W["conv_w_out", 0].reshape(D, D)
    xn_c, bch = _norm_mm_in(x1, mix_pre_g[0], W["conv_w_in", 0], "conv")
    z_c = _conv_fwd(bch, k_taps, Bl, S)
    m_c, x2 = _mm_out_post(z_c, w_o_conv, x1, mix_post_g[0], 1.0, "conv_out")
    arrive(2, x2)
    x3, s_f2a = _ffn_fwd(x2, ffn2_pre_g[0], ffn2_post_g[0], W["ffn2_w_in", 0], W["ffn2_w_out", 0], "l0f2")

    arrive(3, x3)
    kv_full = jnp.concatenate([W["kv_w", 0][s, :, :kvc] for s in range(N_CHIP)], axis=1)
    kv_full = jnp.pad(kv_full, ((0, 0), (0, kv_all - kv_full.shape[1])))
    xn_kv = _rms_fwd(x3, kv_g, "kv")
    kvact = _mm_nn(xn_kv, kv_full[:, :2 * D], MM_DTYPE, "kv")
    pf = _mm_nn(xn_kv, kv_full[:, 2 * D:], F32, "forget")
    cum = _forget_fwd(pf, fb, Bl, S)
    bq = min(S, ATT_BLOCK)
    c3 = cum.reshape(Bl, S, LANES)[:, :, :H].transpose(0, 2, 1)
    c_col = jnp.broadcast_to(c3[..., None], (Bl, H, S, LANES))
    c_row = c3.reshape(Bl, H, S // bq, 1, bq)

    x4, s_f1b = _ffn_fwd(x3, ffn1_pre_g[1], ffn1_post_g[1], W["ffn1_w_in", 1], W["ffn1_w_out", 1], "l1f1")
    arrive(4, x4)
    w_o_attn = W["attn_w_o", 0].reshape(D, D)
    xn_a, qg = _norm_mm_in(x4, mix_pre_g[1], W["attn_w_qg", 0], "qg")
    o, lse = _attn_fwd(qg, kvact, c_col, c_row, Bl, S, D)
    z_a = _gate_fwd(qg, o)
    m_a, x5 = _mm_out_post(z_a, w_o_attn, x4, mix_post_g[1], 1.0, "attn_out")
    x6, s_f2b = _ffn_fwd(x5, ffn2_pre_g[1], ffn2_post_g[1], W["ffn2_w_in", 1], W["ffn2_w_out", 1], "l1f2")

    dy, loss_local = _loss_grad(x6, loss_target.reshape(T, D))
    loss = lax.psum(loss_local, ("x", "y", "c"))

    G = {}
    dx5, dg_f2pre_1, dg_f2post_1, G["ffn2_w_in", 1], G["ffn2_w_out", 1] = _ffn_bwd(
        dy, s_f2b, ffn2_pre_g[1], ffn2_post_g[1], W["ffn2_w_in", 1], W["ffn2_w_out", 1], "l1f2")
    dm_a, dg_mixpost_1, dz_a = _post_bwd_mm(dx5, m_a, mix_post_g[1], 1.0, w_o_attn, "attn_out")
    G["attn_w_o", 0] = _mm_tn_out(z_a, dm_a, "attn_out")
    do = _gate_do(dz_a, qg)
    dq, dk, dv, dcr = _attn_bwd(qg, kvact, do, lse, c_col, c_row, Bl, S, D)
    dqg = _gate_bwd(dz_a, qg, o, dq)
    G["attn_w_qg", 0] = _mm_tn_in(xn_a, dqg, "qg")
    dx4, dg_mixpre_1 = _mm_nt_pre(dqg, W["attn_w_qg", 0], dx5, x4, mix_pre_g[1], "qg")
    dx3, dg_f1pre_1, dg_f1post_1, G["ffn1_w_in", 1], G["ffn1_w_out", 1] = _ffn_bwd(
        dx4, s_f1b, ffn1_pre_g[1], ffn1_post_g[1], W["ffn1_w_in", 1], W["ffn1_w_out", 1], "l1f1")

    dcum = jnp.pad(dcr.reshape(Bl, H, S).transpose(0, 2, 1), ((0, 0), (0, 0), (0, LANES - H))).reshape(T, LANES)
    dpf, dfb = _forget_bwd(dcum, pf, fb, Bl, S)
    dp = jnp.concatenate([dk.astype(MM_DTYPE), dv.astype(MM_DTYPE), dpf], axis=1)
    G_kv_full = _mm_tn(xn_kv, dp, "kv")
    G["kv_w", 0] = jnp.stack([jnp.pad(G_kv_full[:, s * kvc:(s + 1) * kvc], ((0, 0), (0, kvp - kvc))) for s in range(N_CHIP)])
    dx3, dg_kv = _mm_nt_pre(dp, kv_full, dx3, x3, kv_g, "kv")

    def pair_sums(keys):
        grads = [G[k].reshape(N_CHIP, 2, G[k].shape[1] // 2, G[k].shape[2]) for k in keys]
        wires, owns = [], []
        for k, g, r in zip(keys, grads, _rs_pair_send(grads)):
            w, own = _rs_pair_add(g, r, place, f"{k[0]}{k[1]}")
            wires.append(w)
            owns.append(own)
        return wires, owns

    late = groups[2] + groups[1]
    last = groups[0]
    wires_2, owns_2 = pair_sums(second)
    r_sems, r_semr, wires_2, lands_2, token = _rs_start(wires_2, "second")

    dx2, dg_f2pre_0, dg_f2post_0, G["ffn2_w_in", 0], G["ffn2_w_out", 0] = _ffn_bwd(
        dx3, s_f2a, ffn2_pre_g[0], ffn2_post_g[0] + token[0, :1], W["ffn2_w_in", 0], W["ffn2_w_out", 0], "l0f2")
    dm_c, dg_mixpost_0, dz_c = _post_bwd_mm(dx2, m_c, mix_post_g[0], 1.0, w_o_conv, "conv_out")
    G["conv_w_out", 0] = _mm_tn_out(z_c, dm_c, "conv_out")
    db, dcg, dhh, dk_taps = _conv_bwd(bch, dz_c, k_taps, Bl, S)
    dbch = jnp.concatenate([db, dcg, dhh], axis=1)
    G["conv_w_in", 0] = _mm_tn_in(xn_c, dbch, "conv")
    dx1, dg_mixpre_0 = _mm_nt_pre(dbch, W["conv_w_in", 0], dx2, x1, mix_pre_g[0], "conv")
    wires_l, owns_l = pair_sums(late)
    l_sems, l_semr, wires_l, lands_l, token = _rs_start(wires_l, "late")
    dx0, dg_f1pre_0, dg_f1post_0, G["ffn1_w_in", 0], G["ffn1_w_out", 0] = _ffn_bwd(
        dx1, s_f1a, ffn1_pre_g[0], ffn1_post_g[0] + token[0, :1], W["ffn1_w_in", 0], W["ffn1_w_out", 0], "l0f1")
    grad_x = dx0.reshape(Bl, S, D)

    recvs_2 = _rs_wait(r_sems, r_semr, wires_2, lands_2, dx0, "second")
    recvs_l = _rs_wait(l_sems, l_semr, wires_l, lands_l, dx0, "late")
    wires_1, owns_1 = pair_sums(last)

    def row(v):
        return jnp.pad(v.reshape(-1), (0, D - v.size)).reshape(1, D)

    small_parts = [dg_f1pre_0, dg_f1pre_1, dg_f1post_0, dg_f1post_1, dg_mixpre_0, dg_mixpre_1, dg_mixpost_0, dg_mixpost_1,
                   dg_f2pre_0, dg_f2pre_1, dg_f2post_0, dg_f2post_1, dg_kv, row(dfb[0, :H]), dk_taps[:3]]
    small = jnp.concatenate(small_parts, axis=0)
    small = jnp.pad(small, ((0, SMALL_ROWS - small.shape[0]), (0, 0)))
    outs = _rs_chip_send(wires_1, small)
    recvs_1, gall = outs[:-1], outs[-1]
    partial = {}
    for key, own, rcv in zip(last + late + second, owns_1 + owns_l + owns_2,
                             list(recvs_1) + list(recvs_l) + list(recvs_2)):
        name, l = key
        partial[name] = _rs_chip_add(own, rcv, place, l, shards[name].shape[0], partial.get(name), f"{name}{l}")
    names = list(partial)
    reduced = _rs_pair_share([partial[k] for k in names])
    gsum = _sum_devices(gall)

    res = {}
    for k, red in zip(names, reduced):
        w, m, v = given[k]
        g2 = red.reshape(-1, red.shape[-1])
        if k == "kv_w":
            g2 = g2[:, :kvc]
        flat = lambda a: a.reshape(-1, a.shape[-1])
        d, mn, vn = _adamw(flat(w), g2, flat(m), flat(v), k)
        res[k] = tuple(a.reshape(w.shape) for a in (g2, d, mn, vn))

    small_names = ["ffn1_pre_g", "ffn1_post_g", "mix_pre_g", "mix_post_g", "ffn2_pre_g", "ffn2_post_g"]
    small_given = dict(ffn1_pre_g=(ffn1_pre_g, m_ffn1_pre_g, v_ffn1_pre_g), ffn1_post_g=(ffn1_post_g, m_ffn1_post_g, v_ffn1_post_g),
                       mix_pre_g=(mix_pre_g, m_mix_pre_g, v_mix_pre_g), mix_post_g=(mix_post_g, m_mix_post_g, v_mix_post_g),
                       ffn2_pre_g=(ffn2_pre_g, m_ffn2_pre_g, v_ffn2_pre_g), ffn2_post_g=(ffn2_post_g, m_ffn2_post_g, v_ffn2_post_g))

    def pack(idx):
        rows_ = [small_given[k][idx] for k in small_names]
        rows_ += [row((kv_g, m_kv_g, v_kv_g)[idx]), row((forget_b, m_forget_b, v_forget_b)[idx])]
        rows_.append(jnp.pad((conv_k, m_conv_k, v_conv_k)[idx][0], ((0, 0), (0, D - dk_cols))))
        a = jnp.concatenate(rows_, axis=0)
        return jnp.pad(a, ((0, SMALL_ROWS - a.shape[0]), (0, 0)))

    g_taps = lax.dynamic_slice_in_dim(gsum[14:17], chip * dk_cols, dk_cols, axis=1)
    g_small = jnp.concatenate([gsum[:14], jnp.pad(g_taps, ((0, 0), (0, D - dk_cols))), gsum[17:]], axis=0)
    d_s, m_s, v_s = _adamw(pack(0), g_small, pack(1), pack(2), "small")
    for i, k in enumerate(small_names):
        res[k] = tuple(a[2 * i:2 * i + 2] for a in (g_small, d_s, m_s, v_s))
    res["kv_g"] = tuple(a[12] for a in (g_small, d_s, m_s, v_s))
    res["forget_b"] = tuple(a[13, :H] for a in (g_small, d_s, m_s, v_s))
    res["conv_k"] = tuple(a[14:17, :dk_cols][None] for a in (g_small, d_s, m_s, v_s))

    order = ["ffn1_pre_g", "ffn1_post_g", "ffn1_w_in", "ffn1_w_out", "mix_pre_g", "mix_post_g", "ffn2_pre_g", "ffn2_post_g",
             "ffn2_w_in", "ffn2_w_out", "conv_w_in", "conv_k", "conv_w_out", "kv_g", "kv_w", "forget_b", "attn_w_qg", "attn_w_o"]
    out = [loss, grad_x]
    for idx in range(4):
        out += [res[k][idx] for k in order]
    return tuple(out)
```

```python
import functools
import math

import jax
import jax.numpy as jnp
from jax import lax
from jax.experimental import pallas as pl
from jax.experimental.pallas import tpu as pltpu

F32 = jnp.float32
MM_DTYPE = jnp.bfloat16
WIRE_DTYPE = jnp.bfloat16

RMS_EPS = 1e-6
ADAM_LR = 0.001
ADAM_B1 = 0.9
ADAM_B2 = 0.999
ADAM_EPS = 1e-08
ADAM_WD = 0.01
ADAM_STEP = 10

HEAD_DIM = 64
LANES = 128
N_CHIP = 4
N_DEV = 8
ROW_TILE = 256
MM_TILE = 512
FUSED_TILE = 512
TN_TILE = 2048
ATT_BLOCK = 512
SMALL_ROWS = 24
VMEM_LIMIT = 56 * 1024 * 1024
MESH = pl.DeviceIdType.MESH
ANY = pl.BlockSpec(memory_space=pl.ANY)

NT = (((1,), (1,)), ((), ()))
TN = (((0,), (0,)), ((), ()))


def _tile(n, pref):
    if n <= pref:
        return n
    t = pref - pref % 16
    while n % t:
        t -= 16
    return t


def _params():
    return pltpu.CompilerParams(vmem_limit_bytes=VMEM_LIMIT)


def _sds(shape, dtype):
    return jax.ShapeDtypeStruct(shape, dtype)


def _rows(tm, c):
    return pl.BlockSpec((tm, c), lambda i: (i, 0))


def _whole(shape):
    return pl.BlockSpec(shape, lambda *_: (0,) * len(shape))


def _resident(shape):
    return pl.BlockSpec(shape, lambda *_: (0,) * len(shape), pipeline_mode=pl.Buffered(1))


def _rms_fwd(x, g, tag):
    T, D = x.shape
    tm = _tile(T, ROW_TILE)

    def body(x_ref, g_ref, o_ref):
        xv = x_ref[...]
        r = lax.rsqrt(jnp.mean(xv * xv, axis=-1, keepdims=True) + RMS_EPS)
        o_ref[...] = (xv * r * g_ref[...]).astype(o_ref.dtype)

    return pl.pallas_call(
        body, name=f"rms_fwd_{tag}", grid=(T // tm,),
        in_specs=[_rows(tm, D), _whole((1, D))], out_specs=_rows(tm, D),
        out_shape=_sds((T, D), MM_DTYPE), compiler_params=_params())(x, g.reshape(1, D))


def _post_fwd(x, h, g, alpha, tag):
    T, D = x.shape
    tm = _tile(T, ROW_TILE)

    def body(x_ref, h_ref, g_ref, o_ref):
        hv = h_ref[...]
        r = lax.rsqrt(jnp.mean(hv * hv, axis=-1, keepdims=True) + RMS_EPS)
        o_ref[...] = x_ref[...] + alpha * (hv * r * g_ref[...])

    return pl.pallas_call(
        body, name=f"post_fwd_{tag}", grid=(T // tm,),
        in_specs=[_rows(tm, D), _rows(tm, D), _whole((1, D))], out_specs=_rows(tm, D),
        out_shape=_sds((T, D), F32), compiler_params=_params())(x, h, g.reshape(1, D))


def _accumulate(ref, part, first):
    @pl.when(first)
    def _():
        ref[...] = part

    @pl.when(jnp.logical_not(first))
    def _():
        ref[...] += part


def _post_bwd(dx, h, g, alpha, tag):
    T, D = dx.shape
    tm = _tile(T, ROW_TILE)

    def body(dx_ref, h_ref, g_ref, dh_ref, dg_ref):
        hv = h_ref[...]
        r = lax.rsqrt(jnp.mean(hv * hv, axis=-1, keepdims=True) + RMS_EPS)
        hh = hv * r
        dyn = alpha * dx_ref[...]
        _accumulate(dg_ref, jnp.sum(dyn * hh, axis=0, keepdims=True), pl.program_id(0) == 0)
        dhh = dyn * g_ref[...]
        dh = r * (dhh - hh * jnp.mean(dhh * hh, axis=-1, keepdims=True))
        dh_ref[...] = dh.astype(dh_ref.dtype)

    return pl.pallas_call(
        body, name=f"post_bwd_{tag}", grid=(T // tm,),
        in_specs=[_rows(tm, D), _rows(tm, D), _whole((1, D))],
        out_specs=[_rows(tm, D), _whole((1, D))],
        out_shape=[_sds((T, D), MM_DTYPE), _sds((1, D), F32)],
        compiler_params=_params())(dx, h, g.reshape(1, D))


def _pre_bwd(dres, dxn, x, g, tag):
    T, D = x.shape
    tm = _tile(T, ROW_TILE)

    def body(dres_ref, dxn_ref, x_ref, g_ref, dx_ref, dg_ref):
        xv = x_ref[...]
        r = lax.rsqrt(jnp.mean(xv * xv, axis=-1, keepdims=True) + RMS_EPS)
        xh = xv * r
        dn = dxn_ref[...]
        _accumulate(dg_ref, jnp.sum(dn * xh, axis=0, keepdims=True), pl.program_id(0) == 0)
        dxh = dn * g_ref[...]
        dx_ref[...] = dres_ref[...] + r * (dxh - xh * jnp.mean(dxh * xh, axis=-1, keepdims=True))

    return pl.pallas_call(
        body, name=f"pre_bwd_{tag}", grid=(T // tm,),
        in_specs=[_rows(tm, D), _rows(tm, D), _rows(tm, D), _whole((1, D))],
        out_specs=[_rows(tm, D), _whole((1, D))],
        out_shape=[_sds((T, D), F32), _sds((1, D), F32)],
        compiler_params=_params())(dres, dxn, x, g.reshape(1, D))


def _swiglu_fwd(hgu, tag):
    T, F2 = hgu.shape
    F = F2 // 2
    tm = _tile(T, ROW_TILE)

    def body(g_ref, u_ref, o_ref):
        g = g_ref[...].astype(F32)
        o_ref[...] = (g * jax.nn.sigmoid(g) * u_ref[...].astype(F32)).astype(o_ref.dtype)

    return pl.pallas_call(
        body, name=f"swiglu_fwd_{tag}", grid=(T // tm,),
        in_specs=[pl.BlockSpec((tm, F), lambda i: (i, 0)), pl.BlockSpec((tm, F), lambda i: (i, 1))],
        out_specs=_rows(tm, F), out_shape=_sds((T, F), MM_DTYPE), compiler_params=_params())(hgu, hgu)


def _swiglu_bwd(hgu, da, tag):
    T, F2 = hgu.shape
    F = F2 // 2
    tm = _tile(T, ROW_TILE)

    def body(h_ref, da_ref, o_ref):
        g = h_ref[:, :F].astype(F32)
        u = h_ref[:, F:].astype(F32)
        d = da_ref[...].astype(F32)
        sg = jax.nn.sigmoid(g)
        o_ref[:, :F] = (d * u * sg * (1.0 + g * (1.0 - sg))).astype(o_ref.dtype)
        o_ref[:, F:] = (d * g * sg).astype(o_ref.dtype)

    return pl.pallas_call(
        body, name=f"swiglu_bwd_{tag}", grid=(T // tm,),
        in_specs=[_rows(tm, F2), _rows(tm, F)], out_specs=_rows(tm, F2),
        out_shape=_sds((T, F2), MM_DTYPE), compiler_params=_params())(hgu, da)


def _loss_grad(y, tgt):
    T, D = y.shape
    tm = _tile(T, ROW_TILE)

    def body(y_ref, t_ref, dy_ref, l_ref):
        e = y_ref[...] - t_ref[...]
        row = jnp.mean(e * e, axis=-1, keepdims=True)
        part = jnp.broadcast_to(jnp.sum(row, axis=0, keepdims=True), (8, LANES))
        _accumulate(l_ref, part, pl.program_id(0) == 0)
        dy_ref[...] = e * (1.0 / D)

    dy, lsum = pl.pallas_call(
        body, name="loss_grad", grid=(T // tm,),
        in_specs=[_rows(tm, D), _rows(tm, D)], out_specs=[_rows(tm, D), _whole((8, LANES))],
        out_shape=[_sds((T, D), F32), _sds((8, LANES), F32)], compiler_params=_params())(y, tgt)
    return dy, 0.5 * lsum[0, 0]


def _shift_down(u, d, rows):
    return jnp.where(rows >= d, pltpu.roll(u, d, 0), 0.0)


def _shift_up(u, d, rows, S):
    return jnp.where(rows < S - d, pltpu.roll(u, S - d, 0), 0.0)


def _conv_fwd(bch, k8, Bl, S):
    T, D3 = bch.shape
    D = D3 // 3
    dc = min(D, 2 * LANES)
    nd = D // dc

    def body(b_ref, c_ref, h_ref, k_ref, z_ref):
        rows = lax.broadcasted_iota(jnp.int32, (S, 1), 0)
        u = c_ref[...].astype(F32) * h_ref[...].astype(F32)
        y = k_ref[2:3, :] * u + k_ref[1:2, :] * _shift_down(u, 1, rows) + k_ref[0:1, :] * _shift_down(u, 2, rows)
        z_ref[...] = (b_ref[...].astype(F32) * y).astype(z_ref.dtype)

    return pl.pallas_call(
        body, name="conv_fwd", grid=(Bl, nd),
        in_specs=[pl.BlockSpec((S, dc), lambda b, j: (b, j)),
                  pl.BlockSpec((S, dc), lambda b, j: (b, nd + j)),
                  pl.BlockSpec((S, dc), lambda b, j: (b, 2 * nd + j)),
                  pl.BlockSpec((8, dc), lambda b, j: (0, j))],
        out_specs=pl.BlockSpec((S, dc), lambda b, j: (b, j)),
        out_shape=_sds((T, D), MM_DTYPE), compiler_params=_params())(bch, bch, bch, k8)


def _conv_bwd(bch, dz, k8, Bl, S):
    T, D3 = bch.shape
    D = D3 // 3
    dc = min(D, 2 * LANES)
    nd = D // dc

    def body(b_ref, c_ref, h_ref, dz_ref, k_ref, db_ref, dc_ref, dh_ref, dk_ref):
        rows = lax.broadcasted_iota(jnp.int32, (S, 1), 0)
        bv = b_ref[...].astype(F32)
        cv = c_ref[...].astype(F32)
        hv = h_ref[...].astype(F32)
        dzv = dz_ref[...].astype(F32)
        u = cv * hv
        u1 = _shift_down(u, 1, rows)
        u2 = _shift_down(u, 2, rows)
        y = k_ref[2:3, :] * u + k_ref[1:2, :] * u1 + k_ref[0:1, :] * u2
        db_ref[...] = (dzv * y).astype(db_ref.dtype)
        dy = dzv * bv
        du = k_ref[2:3, :] * dy + k_ref[1:2, :] * _shift_up(dy, 1, rows, S) + k_ref[0:1, :] * _shift_up(dy, 2, rows, S)
        dc_ref[...] = (du * hv).astype(dc_ref.dtype)
        dh_ref[...] = (du * cv).astype(dh_ref.dtype)

        @pl.when(pl.program_id(1) == 0)
        def _():
            dk_ref[...] = jnp.zeros_like(dk_ref)

        dk_ref[0:1, :] += jnp.sum(dy * u2, axis=0, keepdims=True)
        dk_ref[1:2, :] += jnp.sum(dy * u1, axis=0, keepdims=True)
        dk_ref[2:3, :] += jnp.sum(dy * u, axis=0, keepdims=True)

    seq = lambda off: pl.BlockSpec((S, dc), lambda j, b: (b, off + j))
    return pl.pallas_call(
        body, name="conv_bwd", grid=(nd, Bl),
        in_specs=[seq(0), seq(nd), seq(2 * nd), seq(0), pl.BlockSpec((8, dc), lambda j, b: (0, j))],
        out_specs=[seq(0), seq(0), seq(0), pl.BlockSpec((8, dc), lambda j, b: (0, j))],
        out_shape=[_sds((T, D), MM_DTYPE)] * 3 + [_sds((8, D), F32)],
        compiler_params=_params())(bch, bch, bch, dz, k8)


def _forget_fwd(pf, fb, Bl, S):
    T = pf.shape[0]

    def body(p_ref, fb_ref, c_ref):
        rows = lax.broadcasted_iota(jnp.int32, (S, 1), 0)
        z = p_ref[...] + fb_ref[...]
        acc = jnp.minimum(z, 0.0) - jnp.log1p(jnp.exp(-jnp.abs(z)))
        d = 1
        while d < S:
            acc = acc + _shift_down(acc, d, rows)
            d *= 2
        c_ref[...] = acc

    return pl.pallas_call(
        body, name="forget_fwd", grid=(Bl,),
        in_specs=[_rows(S, LANES), _whole((1, LANES))], out_specs=_rows(S, LANES),
        out_shape=_sds((T, LANES), F32), compiler_params=_params())(pf, fb)


def _forget_bwd(dc, pf, fb, Bl, S):
    T = pf.shape[0]

    def body(dc_ref, p_ref, fb_ref, df_ref, dfb_ref):
        rows = lax.broadcasted_iota(jnp.int32, (S, 1), 0)
        acc = dc_ref[...]
        d = 1
        while d < S:
            acc = acc + _shift_up(acc, d, rows, S)
            d *= 2
        df = acc * jax.nn.sigmoid(-(p_ref[...] + fb_ref[...]))
        df_ref[...] = df.astype(df_ref.dtype)
        _accumulate(dfb_ref, jnp.sum(df, axis=0, keepdims=True), pl.program_id(0) == 0)

    return pl.pallas_call(
        body, name="forget_bwd", grid=(Bl,),
        in_specs=[_rows(S, LANES), _rows(S, LANES), _whole((1, LANES))],
        out_specs=[_rows(S, LANES), _whole((1, LANES))],
        out_shape=[_sds((T, LANES), MM_DTYPE), _sds((1, LANES), F32)],
        compiler_params=_params())(dc, pf, fb)


def _head_mask(h):
    lane = lax.broadcasted_iota(jnp.int32, (1, LANES), 1)
    return (lane >= h * HEAD_DIM) & (lane < (h + 1) * HEAD_DIM)


def _attn_fwd(qg, kv, c_col, c_row, Bl, S, D):
    T = Bl * S
    H = D // HEAD_DIM
    HP = D // LANES
    bq = min(S, ATT_BLOCK)
    nq = S // bq
    scale = 1.0 / math.sqrt(HEAD_DIM)

    def body(q_ref, k_ref, v_ref, cc_ref, cr_ref, o_ref, lse_ref):
        i = pl.program_id(2)
        q2 = q_ref[...]
        qh = [q2 * (_head_mask(h).astype(F32) * scale).astype(q2.dtype) for h in range(2)]
        cc = [cc_ref[h][:, :1] for h in range(2)]
        diag = lax.broadcasted_iota(jnp.int32, (1, bq), 1) <= lax.broadcasted_iota(jnp.int32, (bq, 1), 0)

        def block(j, carry, on_diagonal):
            off = pl.multiple_of(j * bq, bq)
            kj = k_ref[pl.ds(off, bq), :]
            vj = v_ref[pl.ds(off, bq), :]
            new = []
            for h in range(2):
                m, l, acc = carry[h]
                s = lax.dot_general(qh[h], kj, NT, preferred_element_type=F32) + cc[h] - cr_ref[h, j]
                if on_diagonal:
                    s = jnp.where(diag, s, -jnp.inf)
                m_new = jnp.maximum(m, jnp.max(s, axis=1, keepdims=True))
                p = jnp.exp(s - m_new)
                a = jnp.exp(m - m_new)
                l = a * l + jnp.sum(p, axis=1, keepdims=True)
                acc = a * acc + jnp.dot(p.astype(MM_DTYPE), vj, preferred_element_type=F32)
                new.append((m_new, l, acc))
            return tuple(new)

        one = (jnp.full((bq, 1), -jnp.inf, F32), jnp.zeros((bq, 1), F32), jnp.zeros((bq, LANES), F32))
        carry = lax.fori_loop(0, i, lambda j, c: block(j, c, False), (one, one))
        carry = block(i, carry, True)
        outs = []
        for h in range(2):
            m, l, acc = carry[h]
            outs.append(acc / l)
            lse_ref[h] = jnp.broadcast_to(m + jnp.log(l), (bq, LANES))
        o_ref[...] = jnp.where(_head_mask(0), outs[0], outs[1])

    return pl.pallas_call(
        body, name="attn_fwd", grid=(Bl, HP, nq),
        in_specs=[pl.BlockSpec((bq, LANES), lambda b, hp, i: (b * nq + i, hp)),
                  pl.BlockSpec((S, LANES), lambda b, hp, i: (b, hp)),
                  pl.BlockSpec((S, LANES), lambda b, hp, i: (b, HP + hp)),
                  pl.BlockSpec((None, 2, bq, LANES), lambda b, hp, i: (b, hp, i, 0)),
                  pl.BlockSpec((None, 2, nq, 1, bq), lambda b, hp, i: (b, hp, 0, 0, 0))],
        out_specs=[pl.BlockSpec((bq, LANES), lambda b, hp, i: (b * nq + i, hp)),
                   pl.BlockSpec((None, 2, bq, LANES), lambda b, hp, i: (b, hp, i, 0))],
        out_shape=[_sds((T, D), F32), _sds((Bl, H, S, LANES), F32)],
        compiler_params=_params())(qg, kv, kv, c_col, c_row)


def _attn_bwd(qg, kv, do, lse, c_col, c_row, Bl, S, D):
    T = Bl * S
    H = D // HEAD_DIM
    HP = D // LANES
    bq = min(S, ATT_BLOCK)
    nq = S // bq
    scale = 1.0 / math.sqrt(HEAD_DIM)

    def body(q_ref, k_ref, v_ref, do_ref, lse_ref, cc_ref, cr_ref, dq_ref, dk_ref, dv_ref, dcr_ref, p_sc, dp_sc):
        i = pl.program_id(2)

        @pl.when(i == 0)
        def _():
            dk_ref[...] = jnp.zeros_like(dk_ref)
            dv_ref[...] = jnp.zeros_like(dv_ref)
            dcr_ref[...] = jnp.zeros_like(dcr_ref)

        q2 = q_ref[...]
        do2 = do_ref[...]
        masks = [_head_mask(h).astype(F32) for h in range(2)]
        qh = [q2 * (masks[h] * scale).astype(q2.dtype) for h in range(2)]
        doh = [do2 * masks[h].astype(do2.dtype) for h in range(2)]
        cc = [cc_ref[h][:, :1] for h in range(2)]
        lse = [lse_ref[h][:, :1] for h in range(2)]
        diag = lax.broadcasted_iota(jnp.int32, (1, bq), 1) <= lax.broadcasted_iota(jnp.int32, (bq, 1), 0)

        def sweep1(j, delta, on_diagonal):
            off = pl.multiple_of(j * bq, bq)
            kj = k_ref[pl.ds(off, bq), :]
            vj = v_ref[pl.ds(off, bq), :]
            new = []
            dv = None
            for h in range(2):
                s = lax.dot_general(qh[h], kj, NT, preferred_element_type=F32) + cc[h] - cr_ref[h, j]
                if on_diagonal:
                    s = jnp.where(diag, s, -jnp.inf)
                p = jnp.exp(s - lse[h])
                dp = lax.dot_general(doh[h], vj, NT, preferred_element_type=F32)
                p_sc[h, j] = p
                dp_sc[h, j] = dp
                part = lax.dot_general(p.astype(MM_DTYPE), doh[h], TN, preferred_element_type=F32)
                dv = part if dv is None else dv + part
                new.append(delta[h] + jnp.sum(p * dp, axis=1, keepdims=True))
            dv_ref[pl.ds(off, bq), :] += dv
            return tuple(new)

        zero = jnp.zeros((bq, 1), F32)
        delta = lax.fori_loop(0, i, lambda j, d: sweep1(j, d, False), (zero, zero))
        delta = sweep1(i, delta, True)

        def sweep2(j, dq):
            off = pl.multiple_of(j * bq, bq)
            kj = k_ref[pl.ds(off, bq), :]
            dk = None
            for h in range(2):
                ds = p_sc[h, j] * (dp_sc[h, j] - delta[h])
                dcr_ref[h, j] -= jnp.sum(ds, axis=0, keepdims=True)
                dsb = ds.astype(MM_DTYPE)
                dq = dq + jnp.dot(dsb, kj * (masks[h] * scale).astype(kj.dtype), preferred_element_type=F32)
                part = lax.dot_general(dsb, qh[h], TN, preferred_element_type=F32)
                dk = part if dk is None else dk + part
            dk_ref[pl.ds(off, bq), :] += dk
            return dq

        dq_ref[...] = lax.fori_loop(0, i + 1, sweep2, jnp.zeros((bq, LANES), F32))

    blk = lambda col: pl.BlockSpec((bq, LANES), lambda b, hp, i: (b * nq + i, col(hp)))
    seq = lambda col: pl.BlockSpec((S, LANES), lambda b, hp, i: (b, col(hp)))
    per_head = pl.BlockSpec((None, 2, bq, LANES), lambda b, hp, i: (b, hp, i, 0))
    rows = pl.BlockSpec((None, 2, nq, 1, bq), lambda b, hp, i: (b, hp, 0, 0, 0))
    return pl.pallas_call(
        body, name="attn_bwd", grid=(Bl, HP, nq),
        in_specs=[blk(lambda hp: hp), seq(lambda hp: hp), seq(lambda hp: HP + hp), blk(lambda hp: hp),
                  per_head, per_head, rows],
        out_specs=[blk(lambda hp: hp), seq(lambda hp: hp), seq(lambda hp: hp), rows],
        out_shape=[_sds((T, D), F32), _sds((T, D), F32), _sds((T, D), F32), _sds((Bl, H, nq, 1, bq), F32)],
        scratch_shapes=[pltpu.VMEM((2, nq, bq, bq), F32), pltpu.VMEM((2, nq, bq, bq), F32)],
        compiler_params=_params())(qg, kv, kv, do, lse, c_col, c_row)


def _gate_fwd(qg, o):
    T, D = o.shape
    tm = _tile(T, ROW_TILE)

    def body(g_ref, o_ref, z_ref):
        z_ref[...] = (jax.nn.sigmoid(g_ref[...].astype(F32)) * o_ref[...]).astype(z_ref.dtype)

    return pl.pallas_call(
        body, name="gate_fwd", grid=(T // tm,),
        in_specs=[pl.BlockSpec((tm, D), lambda i: (i, 1)), _rows(tm, D)], out_specs=_rows(tm, D),
        out_shape=_sds((T, D), MM_DTYPE), compiler_params=_params())(qg, o)


def _gate_do(dz, qg):
    T, D = dz.shape
    tm = _tile(T, ROW_TILE)

    def body(dz_ref, g_ref, do_ref):
        do_ref[...] = (dz_ref[...].astype(F32) * jax.nn.sigmoid(g_ref[...].astype(F32))).astype(do_ref.dtype)

    return pl.pallas_call(
        body, name="gate_do", grid=(T // tm,),
        in_specs=[_rows(tm, D), pl.BlockSpec((tm, D), lambda i: (i, 1))], out_specs=_rows(tm, D),
        out_shape=_sds((T, D), MM_DTYPE), compiler_params=_params())(dz, qg)


def _gate_bwd(dz, qg, o, dq):
    T, D = dz.shape
    tm = _tile(T, ROW_TILE)

    def body(dz_ref, g_ref, o_ref, dq_ref, out_ref):
        g = g_ref[...].astype(F32)
        sg = jax.nn.sigmoid(g)
        out_ref[:, :D] = dq_ref[...].astype(out_ref.dtype)
        out_ref[:, D:] = (dz_ref[...].astype(F32) * o_ref[...] * sg * (1.0 - sg)).astype(out_ref.dtype)

    return pl.pallas_call(
        body, name="gate_bwd", grid=(T // tm,),
        in_specs=[_rows(tm, D), pl.BlockSpec((tm, D), lambda i: (i, 1)), _rows(tm, D), _rows(tm, D)],
        out_specs=_rows(tm, 2 * D), out_shape=_sds((T, 2 * D), MM_DTYPE),
        compiler_params=_params())(dz, qg, o, dq)


def _mm_in(a, wg, out_dtype, tag, l=None):
    T, K = a.shape
    n = wg.shape[-1]
    tm = _tile(T, MM_TILE)
    if l is None:
        w_spec = _whole((N_CHIP, K, n))
    else:
        w_spec = pl.BlockSpec((None, N_CHIP, K, n), lambda i: (l, 0, 0, 0))

    def body(a_ref, w_ref, o_ref):
        av = a_ref[...]
        for s in range(N_CHIP):
            o_ref[:, s * n:(s + 1) * n] = jnp.dot(av, w_ref[s], preferred_element_type=F32).astype(o_ref.dtype)

    return pl.pallas_call(
        body, name=f"mm_in_{tag}", grid=(T // tm,),
        in_specs=[_rows(tm, K), w_spec], out_specs=_rows(tm, N_CHIP * n),
        out_shape=_sds((T, N_CHIP * n), out_dtype), compiler_params=_params())(a, wg)


def _mm_nt_in(dy, wg, tag, l=None):
    T = dy.shape[0]
    K, n = wg.shape[-2:]
    tm = _tile(T, MM_TILE)
    if l is None:
        w_spec = _whole((N_CHIP, K, n))
    else:
        w_spec = pl.BlockSpec((None, N_CHIP, K, n), lambda i: (l, 0, 0, 0))

    def body(d_ref, w_ref, o_ref):
        acc = None
        for s in range(N_CHIP):
            part = lax.dot_general(d_ref[:, s * n:(s + 1) * n], w_ref[s], NT, preferred_element_type=F32)
            acc = part if acc is None else acc + part
        o_ref[...] = acc

    return pl.pallas_call(
        body, name=f"mm_nt_in_{tag}", grid=(T // tm,),
        in_specs=[_rows(tm, N_CHIP * n), w_spec], out_specs=_rows(tm, K),
        out_shape=_sds((T, K), F32), compiler_params=_params())(dy, wg)


def _mm_nn(a, b, out_dtype, tag):
    T, K = a.shape
    N = b.shape[1]
    tm = _tile(T, MM_TILE)

    def body(a_ref, b_ref, o_ref):
        o_ref[...] = jnp.dot(a_ref[...], b_ref[...], preferred_element_type=F32).astype(o_ref.dtype)

    return pl.pallas_call(
        body, name=f"mm_nn_{tag}", grid=(T // tm,),
        in_specs=[_rows(tm, K), _whole((K, N))], out_specs=_rows(tm, N),
        out_shape=_sds((T, N), out_dtype), compiler_params=_params())(a, b)


def _mm_nt(a, b, out_dtype, tag):
    T, C = a.shape
    N = b.shape[0]
    tm = _tile(T, MM_TILE)
    nb = N
    for cand in (1408, 1024):
        if N > cand and N % cand == 0:
            nb = cand
            break

    def body(a_ref, b_ref, o_ref):
        o_ref[...] = lax.dot_general(a_ref[...], b_ref[...], NT, preferred_element_type=F32).astype(o_ref.dtype)

    return pl.pallas_call(
        body, name=f"mm_nt_{tag}", grid=(N // nb, T // tm),
        in_specs=[pl.BlockSpec((tm, C), lambda j, i: (i, 0)), pl.BlockSpec((nb, C), lambda j, i: (j, 0))],
        out_specs=pl.BlockSpec((tm, nb), lambda j, i: (i, j)),
        out_shape=_sds((T, N), out_dtype), compiler_params=_params())(a, b)


def _mm_tn_in(a, dy, tag, l=None, prev=None):
    T, K = a.shape
    n = dy.shape[1] // N_CHIP
    tt = _tile(T, TN_TILE)

    def body(a_ref, d_ref, *rest):
        o_ref = rest[-1]
        part = lax.dot_general(a_ref[...], d_ref[...], TN, preferred_element_type=F32)
        _accumulate(o_ref, part, pl.program_id(1) == 0)

    in_specs = [pl.BlockSpec((tt, K), lambda s, t: (t, 0)), pl.BlockSpec((tt, n), lambda s, t: (t, s))]
    args = [a, dy]
    kw = {}
    if l is None:
        out_spec = pl.BlockSpec((None, K, n), lambda s, t: (s, 0, 0))
        out_shape = _sds((N_CHIP, K, n), F32)
    else:
        out_spec = pl.BlockSpec((None, None, K, n), lambda s, t: (l, s, 0, 0))
        out_shape = _sds((2, N_CHIP, K, n), F32)
        if prev is not None:
            in_specs.append(ANY)
            args.append(prev)
            kw["input_output_aliases"] = {2: 0}
    return pl.pallas_call(
        body, name=f"mm_tn_in_{tag}", grid=(N_CHIP, T // tt),
        in_specs=in_specs, out_specs=out_spec, out_shape=out_shape,
        compiler_params=_params(), **kw)(*args)


def _mm_tn_out(act, dh, tag, l=None, prev=None):
    T, R4 = act.shape
    D = dh.shape[1]
    r = R4 // N_CHIP
    g = 1 if r % LANES == 0 else 2
    tt = _tile(T, TN_TILE)

    def body(a_ref, d_ref, *rest):
        o_ref = rest[-1]
        part = lax.dot_general(a_ref[...], d_ref[...], TN, preferred_element_type=F32)
        first = pl.program_id(1) == 0
        for q in range(g):
            _accumulate(o_ref.at[q], part[q * r:(q + 1) * r], first)

    in_specs = [pl.BlockSpec((tt, g * r), lambda s, t: (t, s)), pl.BlockSpec((tt, D), lambda s, t: (t, 0))]
    args = [act, dh]
    kw = {}
    if l is None:
        out_spec = pl.BlockSpec((g, r, D), lambda s, t: (s, 0, 0))
        out_shape = _sds((N_CHIP, r, D), F32)
    else:
        out_spec = pl.BlockSpec((None, g, r, D), lambda s, t: (l, s, 0, 0))
        out_shape = _sds((2, N_CHIP, r, D), F32)
        if prev is not None:
            in_specs.append(ANY)
            args.append(prev)
            kw["input_output_aliases"] = {2: 0}
    return pl.pallas_call(
        body, name=f"mm_tn_out_{tag}", grid=(N_CHIP // g, T // tt),
        in_specs=in_specs, out_specs=out_spec, out_shape=out_shape,
        compiler_params=_params(), **kw)(*args)


def _mm_tn(a, b, tag):
    T, K = a.shape
    N = b.shape[1]
    tt = _tile(T, TN_TILE)

    def body(a_ref, b_ref, o_ref):
        part = lax.dot_general(a_ref[...], b_ref[...], TN, preferred_element_type=F32)
        _accumulate(o_ref, part, pl.program_id(0) == 0)

    return pl.pallas_call(
        body, name=f"mm_tn_{tag}", grid=(T // tt,),
        in_specs=[_rows(tt, K), _rows(tt, N)], out_specs=_whole((K, N)),
        out_shape=_sds((K, N), F32), compiler_params=_params())(a, b)


def _norm_mm_in(x, g, wg, tag, swiglu=False):
    T, D = x.shape
    n = wg.shape[-1]
    tm = _tile(T, FUSED_TILE)
    half = N_CHIP // 2

    def body(x_ref, g_ref, w_ref, xn_ref, y_ref, *rest):
        xv = x_ref[...]
        r = lax.rsqrt(jnp.mean(xv * xv, axis=-1, keepdims=True) + RMS_EPS)
        xn = (xv * r * g_ref[...]).astype(xn_ref.dtype)
        xn_ref[...] = xn

        def product(s):
            p = jnp.dot(xn, w_ref[s], preferred_element_type=F32)
            y_ref[:, s * n:(s + 1) * n] = p.astype(y_ref.dtype)
            return p

        if swiglu:
            for q in range(half):
                gate, up = product(q), product(half + q)
                rest[0][:, q * n:(q + 1) * n] = (gate * jax.nn.sigmoid(gate) * up).astype(rest[0].dtype)
        else:
            for s in range(N_CHIP):
                product(s)

    out_specs = [_rows(tm, D), _rows(tm, N_CHIP * n)]
    out_shape = [_sds((T, D), MM_DTYPE), _sds((T, N_CHIP * n), MM_DTYPE)]
    if swiglu:
        out_specs.append(_rows(tm, half * n))
        out_shape.append(_sds((T, half * n), MM_DTYPE))
    return pl.pallas_call(
        body, name=f"norm_mm_in_{tag}", grid=(T // tm,),
        in_specs=[_rows(tm, D), _whole((1, D)), _resident((N_CHIP, D, n))], out_specs=out_specs,
        out_shape=out_shape, compiler_params=_params())(x, g.reshape(1, D), wg)


def _mm_out_post(a, b, x, g, alpha, tag):
    T, K = a.shape
    D = b.shape[1]
    tm = _tile(T, FUSED_TILE)

    def body(a_ref, b_ref, x_ref, g_ref, h_ref, o_ref):
        hv = jnp.dot(a_ref[...], b_ref[...], preferred_element_type=F32)
        h_ref[...] = hv
        r = lax.rsqrt(jnp.mean(hv * hv, axis=-1, keepdims=True) + RMS_EPS)
        o_ref[...] = x_ref[...] + alpha * (hv * r * g_ref[...])

    return pl.pallas_call(
        body, name=f"mm_out_post_{tag}", grid=(T // tm,),
        in_specs=[_rows(tm, K), _resident((K, D)), _rows(tm, D), _whole((1, D))],
        out_specs=[_rows(tm, D), _rows(tm, D)], out_shape=[_sds((T, D), F32)] * 2,
        compiler_params=_params())(a, b, x, g.reshape(1, D))


def _post_bwd_mm(dx, h, g, alpha, b, tag, hgu=None):
    T, D = dx.shape
    K = b.shape[0]
    tm = _tile(T, FUSED_TILE)

    def body(dx_ref, h_ref, g_ref, b_ref, *rest):
        dh_ref, dg_ref, out_ref = rest[-3:]
        hv = h_ref[...]
        r = lax.rsqrt(jnp.mean(hv * hv, axis=-1, keepdims=True) + RMS_EPS)
        hh = hv * r
        dyn = alpha * dx_ref[...]
        _accumulate(dg_ref, jnp.sum(dyn * hh, axis=0, keepdims=True), pl.program_id(0) == 0)
        dhh = dyn * g_ref[...]
        dh = (r * (dhh - hh * jnp.mean(dhh * hh, axis=-1, keepdims=True))).astype(dh_ref.dtype)
        dh_ref[...] = dh
        da = lax.dot_general(dh, b_ref[...], NT, preferred_element_type=F32)
        if hgu is None:
            out_ref[...] = da.astype(out_ref.dtype)
        else:
            gate = rest[0][:, :K].astype(F32)
            up = rest[0][:, K:].astype(F32)
            sg = jax.nn.sigmoid(gate)
            out_ref[:, :K] = (da * up * sg * (1.0 + gate * (1.0 - sg))).astype(out_ref.dtype)
            out_ref[:, K:] = (da * gate * sg).astype(out_ref.dtype)

    in_specs = [_rows(tm, D), _rows(tm, D), _whole((1, D)), _resident((K, D))]
    args = [dx, h, g.reshape(1, D), b]
    wide = K
    if hgu is not None:
        wide = 2 * K
        in_specs.append(_rows(tm, wide))
        args.append(hgu)
    return pl.pallas_call(
        body, name=f"post_bwd_mm_{tag}", grid=(T // tm,), in_specs=in_specs,
        out_specs=[_rows(tm, D), _whole((1, D)), _rows(tm, wide)],
        out_shape=[_sds((T, D), MM_DTYPE), _sds((1, D), F32), _sds((T, wide), MM_DTYPE)],
        compiler_params=_params())(*args)


def _mm_nt_pre(dy, w, dres, x, g, tag):
    T, C = dy.shape
    D = x.shape[1]
    tm = _tile(T, FUSED_TILE)
    n = w.shape[-1]

    def body(dy_ref, w_ref, dres_ref, x_ref, g_ref, dx_ref, dg_ref):
        if w.ndim == 2:
            dn = lax.dot_general(dy_ref[...], w_ref[...], NT, preferred_element_type=F32)
        else:
            dn = None
            for s in range(N_CHIP):
                part = lax.dot_general(dy_ref[:, s * n:(s + 1) * n], w_ref[s], NT, preferred_element_type=F32)
                dn = part if dn is None else dn + part
        xv = x_ref[...]
        r = lax.rsqrt(jnp.mean(xv * xv, axis=-1, keepdims=True) + RMS_EPS)
        xh = xv * r
        _accumulate(dg_ref, jnp.sum(dn * xh, axis=0, keepdims=True), pl.program_id(0) == 0)
        dxh = dn * g_ref[...]
        dx_ref[...] = dres_ref[...] + r * (dxh - xh * jnp.mean(dxh * xh, axis=-1, keepdims=True))

    return pl.pallas_call(
        body, name=f"mm_nt_pre_{tag}", grid=(T // tm,),
        in_specs=[_rows(tm, C), _resident(w.shape), _rows(tm, D), _rows(tm, D), _whole((1, D))],
        out_specs=[_rows(tm, D), _whole((1, D))], out_shape=[_sds((T, D), F32), _sds((1, D), F32)],
        compiler_params=_params())(dy, w, dres, x, g.reshape(1, D))


def _adamw(w, g, m, v, tag, after=None):
    R, C = w.shape
    tr = _tile(R, ROW_TILE)
    extra = [] if after is None else [after]

    def body(w_ref, g_ref, m_ref, v_ref, *rest):
        d_ref, mo_ref, vo_ref = rest[-3:]
        gv = g_ref[...]
        mn = ADAM_B1 * m_ref[...] + (1.0 - ADAM_B1) * gv
        vn = ADAM_B2 * v_ref[...] + (1.0 - ADAM_B2) * (gv * gv)
        m_hat = mn / (1.0 - ADAM_B1 ** ADAM_STEP)
        v_hat = vn / (1.0 - ADAM_B2 ** ADAM_STEP)
        d_ref[...] = -ADAM_LR * (m_hat / (jnp.sqrt(v_hat) + ADAM_EPS) + ADAM_WD * w_ref[...])
        mo_ref[...] = mn
        vo_ref[...] = vn

    return pl.pallas_call(
        body, name=f"adamw_{tag}", grid=(R // tr,),
        in_specs=[_rows(tr, C)] * 4 + [ANY] * len(extra), out_specs=[_rows(tr, C)] * 3,
        out_shape=[_sds((R, C), F32)] * 3, compiler_params=_params())(w, g, m, v, *extra)


def _sum_devices(gall, own, place):
    _, R, C = gall.shape

    def body(place_ref, g_ref, s_ref, o_ref):
        me = 2 * place_ref[1] + place_ref[0]
        acc = None
        for d in range(N_DEV):
            term = jnp.where(me == d, s_ref[...], g_ref[d])
            acc = term if acc is None else acc + term
        o_ref[...] = acc

    grid_spec = pltpu.PrefetchScalarGridSpec(
        num_scalar_prefetch=1, grid=(1,),
        in_specs=[pl.BlockSpec((N_DEV, R, C), lambda i, p: (0, 0, 0)), pl.BlockSpec((R, C), lambda i, p: (0, 0))],
        out_specs=pl.BlockSpec((R, C), lambda i, p: (0, 0)))
    return pl.pallas_call(
        body, name="sum_devices", grid_spec=grid_spec, out_shape=_sds((R, C), F32),
        compiler_params=_params())(place, gall, own)


HBM = pl.BlockSpec(memory_space=pltpu.HBM)
SEM = pl.BlockSpec(memory_space=pltpu.SEMAPHORE)
EFFECT = pltpu.SideEffectType.DATAFLOW_SIDE_EFFECTING


def _place():
    x, y, c = lax.axis_index("x"), lax.axis_index("y"), lax.axis_index("c")
    chips = ((1 - x, y), (x, 1 - y), (1 - x, 1 - y))
    return x, y, c, chips


def _remote(src, dst, send_sem, recv_sem, dev):
    return pltpu.make_async_remote_copy(src_ref=src, dst_ref=dst, send_sem=send_sem, recv_sem=recv_sem,
                                        device_id=dev, device_id_type=MESH)


def _in_hbm(a):
    return pltpu.with_memory_space_constraint(a, pltpu.HBM)


def _own_slot(w4, l, dtype, place, tag):
    _, _, r, col = w4.shape
    tr = _tile(r, 2 * ROW_TILE)

    def body(place_ref, x_ref, o_ref):
        o_ref[...] = x_ref[...].astype(o_ref.dtype)

    grid_spec = pltpu.PrefetchScalarGridSpec(
        num_scalar_prefetch=1, grid=(2, r // tr),
        in_specs=[pl.BlockSpec((None, None, tr, col), lambda h, i, p: (l, h, i, 0))],
        out_specs=pl.BlockSpec((None, None, tr, col), lambda h, i, p: (p[1], h, i, 0)))
    return pl.pallas_call(
        body, name=f"own_slot_{tag}", grid_spec=grid_spec, out_shape=_sds((N_CHIP, 2, r, col), dtype),
        compiler_params=_params())(place, w4)


def _gather_start(bufs, after, tag):
    n = len(bufs)

    def body(*refs):
        ins = refs[:n]
        s_sem, r_sem, token = refs[n + 1], refs[n + 2], refs[2 * n + 3]
        x, y, c, chips = _place()
        me = 2 * x + y
        for i in range(n):
            mine = ins[i].at[me, c]
            for j, (px, py) in enumerate(chips):
                _remote(mine, mine, s_sem.at[3 * i + j], r_sem.at[3 * i + j], (px, py, c)).start()
        token[...] = jnp.zeros_like(token)

    dma = pltpu.SemaphoreType.DMA
    res = pl.pallas_call(
        body, name=f"gather_start_{tag}", in_specs=[HBM] * n + [ANY],
        out_specs=[SEM, SEM] + [HBM] * n + [pl.BlockSpec(memory_space=pltpu.VMEM)],
        out_shape=[dma((3 * n,)), dma((3 * n,))] + [pltpu.HBM(b.shape, b.dtype) for b in bufs] + [_sds((8, LANES), F32)],
        input_output_aliases={i: i + 2 for i in range(n)},
        compiler_params=pltpu.CompilerParams(has_side_effects=EFFECT),
        )(*[_in_hbm(b) for b in bufs], after)
    return res[0], res[1], list(res[2:2 + n]), res[-1]


def _gather_pass(s_sem, r_sem, bufs, first, after, tag):
    n = len(bufs)

    def body(*refs):
        ins = refs[:n]
        a_s, a_r, b_s, b_r = refs[n], refs[n + 1], refs[n + 3], refs[n + 4]
        x, y, c, chips = _place()
        me = 2 * x + y
        sib = (x, y, 1 - c)
        for i in range(n):
            mine = ins[i].at[me, c]
            for j, (px, py) in enumerate(chips):
                k = 3 * (first + i) + j
                _remote(mine, mine, a_s.at[k], a_r.at[k], (px, py, c)).wait_send()
        for j, (px, py) in enumerate(chips):
            for i in range(n):
                k = 3 * (first + i) + j
                blk = ins[i].at[2 * px + py, c]
                _remote(blk, blk, a_s.at[k], a_r.at[k], (px, py, c)).wait_recv()
                _remote(blk, blk, b_s.at[3 * i + j], b_r.at[3 * i + j], sib).start()

    dma = pltpu.SemaphoreType.DMA
    res = pl.pallas_call(
        body, name=f"gather_pass_{tag}", in_specs=[HBM] * n + [SEM, SEM, ANY],
        out_specs=[SEM, SEM] + [HBM] * n,
        out_shape=[dma((3 * n,)), dma((3 * n,))] + [pltpu.HBM(b.shape, b.dtype) for b in bufs],
        input_output_aliases={i: i + 2 for i in range(n)},
        compiler_params=pltpu.CompilerParams(has_side_effects=EFFECT),
        )(*bufs, s_sem, r_sem, after)
    return res[0], res[1], list(res[2:])


def _gather_land(s_sem, r_sem, bufs, tag):
    n = len(bufs)

    def body(*refs):
        ins = refs[:n]
        b_s, b_r = refs[n], refs[n + 1]
        x, y, c, chips = _place()
        sib = (x, y, 1 - c)
        for j, (px, py) in enumerate(chips):
            for i in range(n):
                sent = ins[i].at[2 * px + py, c]
                got = ins[i].at[2 * px + py, 1 - c]
                _remote(sent, sent, b_s.at[3 * i + j], b_r.at[3 * i + j], sib).wait_send()
                _remote(got, got, b_s.at[3 * i + j], b_r.at[3 * i + j], sib).wait_recv()

    return list(pl.pallas_call(
        body, name=f"gather_land_{tag}", in_specs=[HBM] * n + [SEM, SEM], out_specs=[HBM] * n,
        out_shape=[pltpu.HBM(b.shape, b.dtype) for b in bufs],
        input_output_aliases={i: i for i in range(n)},
        compiler_params=pltpu.CompilerParams(has_side_effects=EFFECT),
        )(*bufs, s_sem, r_sem))


def _rs_pair_send(grads):
    n = len(grads)

    def body(*refs):
        ins, outs = refs[:n], refs[n:2 * n]
        s_sem, r_sem = refs[2 * n:]
        x, y, c, _ = _place()
        sib = (x, y, 1 - c)
        sends = []
        for i in range(n):
            cp = _remote(ins[i].at[:, 1 - c], outs[i], s_sem.at[i], r_sem.at[i], sib)
            cp.start()
            sends.append(cp)
        for cp in sends:
            cp.wait()

    out_shape = [_sds((N_CHIP,) + g.shape[2:], g.dtype) for g in grads]
    dma = pltpu.SemaphoreType.DMA
    return pl.pallas_call(
        body, name=f"rs_pair_send_{n}", in_specs=[ANY] * n, out_specs=[ANY] * n, out_shape=out_shape,
        scratch_shapes=[dma((n,)), dma((n,))],
        )(*grads)


def _rs_pair_add(g, recv, place, tag):
    r, col = g.shape[-2:]
    tr = _tile(r, ROW_TILE)

    def body(place_ref, g_ref, r_ref, wire_ref, own_ref):
        tot = g_ref[...] + r_ref[...]
        wire_ref[...] = tot.astype(wire_ref.dtype)

        @pl.when(pl.program_id(1) == place_ref[1])
        def _():
            own_ref[...] = tot

    grid_spec = pltpu.PrefetchScalarGridSpec(
        num_scalar_prefetch=1, grid=(r // tr, N_CHIP),
        in_specs=[pl.BlockSpec((None, None, tr, col), lambda i, s, p: (s, p[0], i, 0)),
                  pl.BlockSpec((None, tr, col), lambda i, s, p: (s, i, 0))],
        out_specs=[pl.BlockSpec((None, tr, col), lambda i, s, p: (s, i, 0)),
                   pl.BlockSpec((tr, col), lambda i, s, p: (i, 0))])
    return pl.pallas_call(
        body, name=f"rs_pair_add_{tag}", grid_spec=grid_spec,
        out_shape=[_sds((N_CHIP, r, col), WIRE_DTYPE), _sds((r, col), F32)],
        compiler_params=_params())(place, g, recv)


def _pair_plan(srcs, lands):
    x, y, c, _ = _place()
    return [(s.at[:, 1 - c], l, (x, y, 1 - c)) for s, l in zip(srcs, lands)]


def _chip_plan(srcs, lands):
    x, y, c, chips = _place()
    plan = []
    for s, l in zip(srcs, lands):
        if len(s.shape) == 2:
            me = 4 * x + 2 * y + c
            plan += [(s, l.at[me], (x ^ (k >> 2), y ^ ((k >> 1) & 1), c ^ (k & 1))) for k in range(1, N_DEV)]
        else:
            plan += [(s.at[2 * px + py], l.at[j], (px, py, c)) for j, (px, py) in enumerate(chips)]
    return plan


def _exchange_start(srcs, lands, plan, count, tag):
    n = len(srcs)
    both = list(srcs) + list(lands)

    def body(*refs):
        s_sem, r_sem, token = refs[2 * n], refs[2 * n + 1], refs[4 * n + 2]
        for k, (src, dst, dev) in enumerate(plan(refs[:n], refs[n:2 * n])):
            _remote(src, dst, s_sem.at[k], r_sem.at[k], dev).start()
        token[...] = jnp.zeros_like(token)

    dma = pltpu.SemaphoreType.DMA
    res = pl.pallas_call(
        body, name=f"exchange_start_{tag}", in_specs=[HBM] * (2 * n),
        out_specs=[SEM, SEM] + [HBM] * (2 * n) + [pl.BlockSpec(memory_space=pltpu.VMEM)],
        out_shape=[dma((count,)), dma((count,))] + [pltpu.HBM(b.shape, b.dtype) for b in both] + [_sds((8, LANES), F32)],
        input_output_aliases={i: i + 2 for i in range(2 * n)},
        compiler_params=pltpu.CompilerParams(has_side_effects=EFFECT),
        )(*[_in_hbm(b) for b in both])
    return res[0], res[1], list(res[2:2 + n]), list(res[2 + n:2 + 2 * n]), res[-1]


def _exchange_wait(s_sem, r_sem, srcs, lands, plan, after, tag):
    n = len(srcs)

    def body(*refs):
        s_ref, r_ref = refs[2 * n], refs[2 * n + 1]
        for k, (src, dst, dev) in enumerate(plan(refs[:n], refs[n:2 * n])):
            cp = _remote(src, dst, s_ref.at[k], r_ref.at[k], dev)
            cp.wait_send()
            cp.wait_recv()

    both = list(srcs) + list(lands)
    res = pl.pallas_call(
        body, name=f"exchange_wait_{tag}", in_specs=[HBM] * (2 * n) + [SEM, SEM, ANY], out_specs=[HBM] * (2 * n),
        out_shape=[pltpu.HBM(b.shape, b.dtype) for b in both],
        input_output_aliases={i: i for i in range(2 * n)},
        compiler_params=pltpu.CompilerParams(has_side_effects=EFFECT),
        )(*both, s_sem, r_sem, after)
    return list(res[:n]), list(res[n:])


def _rs_chip_add(own, recv, place, l, L, prev, tag):
    r, col = own.shape
    tr = _tile(r, ROW_TILE)

    def body(place_ref, o_ref, r_ref, *rest):
        acc = o_ref[...]
        for j in range(3):
            acc = acc + r_ref[j].astype(F32)
        rest[-1][...] = acc

    in_specs = [pl.BlockSpec((tr, col), lambda i, p: (i, 0)), pl.BlockSpec((3, tr, col), lambda i, p: (0, i, 0))]
    args = [place, own, recv]
    kw = {}
    if prev is not None:
        in_specs.append(ANY)
        args.append(prev)
        kw["input_output_aliases"] = {3: 0}
    grid_spec = pltpu.PrefetchScalarGridSpec(
        num_scalar_prefetch=1, grid=(r // tr,), in_specs=in_specs,
        out_specs=pl.BlockSpec((None, None, tr, col), lambda i, p: (l, p[0], i, 0)))
    return pl.pallas_call(
        body, name=f"rs_chip_add_{tag}", grid_spec=grid_spec, out_shape=_sds((L, 2, r, col), F32),
        compiler_params=_params(), **kw)(*args)


def _rs_pair_share(fulls, tag):
    n = len(fulls)

    def body(*refs):
        outs = refs[n:2 * n]
        s_sem, r_sem = refs[2 * n:]
        x, y, c, _ = _place()
        sib = (x, y, 1 - c)
        started = []
        for i in range(n):
            cp = _remote(outs[i].at[:, c], outs[i].at[:, c], s_sem.at[i], r_sem.at[i], sib)
            cp.start()
            started.append(cp)
        for i, cp in enumerate(started):
            cp.wait_send()
            _remote(outs[i].at[:, 1 - c], outs[i].at[:, 1 - c], s_sem.at[i], r_sem.at[i], sib).wait_recv()

    dma = pltpu.SemaphoreType.DMA
    return pl.pallas_call(
        body, name=f"rs_pair_share_{tag}", in_specs=[ANY] * n, out_specs=[ANY] * n,
        out_shape=[_sds(f.shape, f.dtype) for f in fulls],
        input_output_aliases={i: i for i in range(n)},
        scratch_shapes=[dma((n,)), dma((n,))],
        )(*fulls)


def _ffn_fwd(x, g_pre, g_post, w_in, w_out, tag):
    xn, hgu, act = _norm_mm_in(x, g_pre, w_in, tag, swiglu=True)
    h, x_out = _mm_out_post(act, w_out.reshape(-1, w_out.shape[-1]), x, g_post, 0.5, tag)
    return x_out, (x, xn, hgu, act, h)


def _ffn_bwd(dx, saved, g_pre, g_post, w_in, w_out, tag):
    x, xn, hgu, act, h = saved
    dh, dg_post, dhgu = _post_bwd_mm(dx, h, g_post, 0.5, w_out.reshape(-1, w_out.shape[-1]), tag, hgu=hgu)
    dw_out = _mm_tn_out(act, dh, tag)
    dw_in = _mm_tn_in(xn, dhgu, tag)
    dx_in, dg_pre = _mm_nt_pre(dhgu, w_in, dx, x, g_pre, tag)
    return dx_in, dg_pre, dg_post, dw_in, dw_out


def kernel(x, ffn1_pre_g, ffn1_post_g, ffn1_w_in, ffn1_w_out, mix_pre_g, mix_post_g, ffn2_pre_g, ffn2_post_g, ffn2_w_in, ffn2_w_out, conv_w_in, conv_k, conv_w_out, kv_g, kv_w, forget_b, attn_w_qg, attn_w_o, loss_target, m_ffn1_pre_g, m_ffn1_post_g, m_ffn1_w_in, m_ffn1_w_out, m_mix_pre_g, m_mix_post_g, m_ffn2_pre_g, m_ffn2_post_g, m_ffn2_w_in, m_ffn2_w_out, m_conv_w_in, m_conv_k, m_conv_w_out, m_kv_g, m_kv_w, m_forget_b, m_attn_w_qg, m_attn_w_o, v_ffn1_pre_g, v_ffn1_post_g, v_ffn1_w_in, v_ffn1_w_out, v_mix_pre_g, v_mix_post_g, v_ffn2_pre_g, v_ffn2_post_g, v_ffn2_w_in, v_ffn2_w_out, v_conv_w_in, v_conv_k, v_conv_w_out, v_kv_g, v_kv_w, v_forget_b, v_attn_w_qg, v_attn_w_o):
    Bl, S, D = x.shape
    T = Bl * S
    H = forget_b.shape[0]
    assert D == H * HEAD_DIM and D % LANES == 0
    kvc = kv_w.shape[1]
    kvp = -(-kvc // LANES) * LANES
    kv_all = 2 * D + LANES
    dk_cols = conv_k.shape[2]
    chip = 2 * lax.axis_index("x") + lax.axis_index("y")
    core = lax.axis_index("c")

    given = dict(ffn1_w_in=(ffn1_w_in, m_ffn1_w_in, v_ffn1_w_in), ffn1_w_out=(ffn1_w_out, m_ffn1_w_out, v_ffn1_w_out),
                 ffn2_w_in=(ffn2_w_in, m_ffn2_w_in, v_ffn2_w_in), ffn2_w_out=(ffn2_w_out, m_ffn2_w_out, v_ffn2_w_out),
                 conv_w_in=(conv_w_in, m_conv_w_in, v_conv_w_in), conv_w_out=(conv_w_out, m_conv_w_out, v_conv_w_out),
                 kv_w=(kv_w, m_kv_w, v_kv_w), attn_w_qg=(attn_w_qg, m_attn_w_qg, v_attn_w_qg),
                 attn_w_o=(attn_w_o, m_attn_w_o, v_attn_w_o))
    shards = {k: w for k, (w, _, _) in given.items()}
    shards["kv_w"] = jnp.pad(kv_w, ((0, 0), (0, kvp - kvc)))[None]
    groups = [[("ffn1_w_in", 0), ("ffn1_w_out", 0)], [("conv_w_in", 0), ("conv_w_out", 0)],
              [("ffn2_w_in", 0), ("ffn2_w_out", 0)], [("kv_w", 0), ("ffn1_w_in", 1), ("ffn1_w_out", 1)],
              [("attn_w_qg", 0), ("attn_w_o", 0), ("ffn2_w_in", 1), ("ffn2_w_out", 1)]]
    first = groups[0] + groups[1] + groups[2]
    second = groups[3] + groups[4]
    place = jnp.stack([core, chip]).astype(jnp.int32)

    def slot(key, where):
        w = shards[key[0]]
        L, r, col = w.shape
        return _own_slot(w.reshape(L, 2, r // 2, col), key[1], MM_DTYPE, where, f"{key[0]}{key[1]}")

    def whole(g):
        return g.reshape(N_CHIP, -1, g.shape[-1])

    taps_slot = _own_slot(jnp.pad(conv_k[0], ((0, 13), (0, 0))).reshape(1, 2, 8, dk_cols), 0, F32, place, "conv_k")
    fb = jnp.pad(forget_b, (0, LANES - H)).reshape(1, LANES)
    s_0, r_0, fly_0, token = _gather_start([slot(key, place) for key in groups[0]] + [taps_slot], fb, "first")
    later = groups[1] + groups[2] + groups[3] + groups[4]
    s_1, r_1, fly_1, token = _gather_start([slot(key, place) for key in later], token, "rest")
    W = {}

    def arrive(g, after):
        if g == 0:
            sems, bufs, lo = (s_0, r_0), fly_0, 0
        else:
            lo = sum(len(groups[k]) for k in range(1, g))
            sems, bufs = (s_1, r_1), fly_1[lo:lo + len(groups[g])]
        got = _gather_land(*_gather_pass(*sems, bufs, lo, after, f"g{g}"), f"g{g}")
        W.update({key: whole(b) for key, b in zip(groups[g], got)})
        return got

    k_taps = arrive(0, token)[-1].reshape(N_CHIP, 16, dk_cols).transpose(1, 0, 2).reshape(16, D)[:8]

    x0 = x.reshape(T, D)
    x1, s_f1a = _ffn_fwd(x0, ffn1_pre_g[0], ffn1_post_g[0], W["ffn1_w_in", 0], W["ffn1_w_out", 0], "l0f1")
    arrive(1, x1)
    w_o_conv = W["conv_w_out", 0].reshape(D, D)
    xn_c, bch = _norm_mm_in(x1, mix_pre_g[0], W["conv_w_in", 0], "conv")
    z_c = _conv_fwd(bch, k_taps, Bl, S)
    m_c, x2 = _mm_out_post(z_c, w_o_conv, x1, mix_post_g[0], 1.0, "conv_out")
    arrive(2, x2)
    x3, s_f2a = _ffn_fwd(x2, ffn2_pre_g[0], ffn2_post_g[0], W["ffn2_w_in", 0], W["ffn2_w_out", 0], "l0f2")

    arrive(3, x3)
    kv_full = jnp.concatenate([W["kv_w", 0][s, :, :kvc] for s in range(N_CHIP)], axis=1)
    kv_full = jnp.pad(kv_full, ((0, 0), (0, kv_all - kv_full.shape[1])))
    xn_kv = _rms_fwd(x3, kv_g, "kv")
    kvact = _mm_nn(xn_kv, kv_full[:, :2 * D], MM_DTYPE, "kv")
    pf = _mm_nn(xn_kv, kv_full[:, 2 * D:], F32, "forget")
    cum = _forget_fwd(pf, fb, Bl, S)
    bq = min(S, ATT_BLOCK)
    c3 = cum.reshape(Bl, S, LANES)[:, :, :H].transpose(0, 2, 1)
    c_col = jnp.broadcast_to(c3[..., None], (Bl, H, S, LANES))
    c_row = c3.reshape(Bl, H, S // bq, 1, bq)

    x4, s_f1b = _ffn_fwd(x3, ffn1_pre_g[1], ffn1_post_g[1], W["ffn1_w_in", 1], W["ffn1_w_out", 1], "l1f1")
    arrive(4, x4)
    w_o_attn = W["attn_w_o", 0].reshape(D, D)
    xn_a, qg = _norm_mm_in(x4, mix_pre_g[1], W["attn_w_qg", 0], "qg")
    o, lse = _attn_fwd(qg, kvact, c_col, c_row, Bl, S, D)
    z_a = _gate_fwd(qg, o)
    m_a, x5 = _mm_out_post(z_a, w_o_attn, x4, mix_post_g[1], 1.0, "attn_out")
    x6, s_f2b = _ffn_fwd(x5, ffn2_pre_g[1], ffn2_post_g[1], W["ffn2_w_in", 1], W["ffn2_w_out", 1], "l1f2")

    dy, loss_local = _loss_grad(x6, loss_target.reshape(T, D))
    loss = lax.psum(loss_local, ("x", "y", "c"))

    G = {}
    dx5, dg_f2pre_1, dg_f2post_1, G["ffn2_w_in", 1], G["ffn2_w_out", 1] = _ffn_bwd(
        dy, s_f2b, ffn2_pre_g[1], ffn2_post_g[1], W["ffn2_w_in", 1], W["ffn2_w_out", 1], "l1f2")
    dm_a, dg_mixpost_1, dz_a = _post_bwd_mm(dx5, m_a, mix_post_g[1], 1.0, w_o_attn, "attn_out")
    G["attn_w_o", 0] = _mm_tn_out(z_a, dm_a, "attn_out")
    do = _gate_do(dz_a, qg)
    dq, dk, dv, dcr = _attn_bwd(qg, kvact, do, lse, c_col, c_row, Bl, S, D)
    dqg = _gate_bwd(dz_a, qg, o, dq)
    G["attn_w_qg", 0] = _mm_tn_in(xn_a, dqg, "qg")
    dx4, dg_mixpre_1 = _mm_nt_pre(dqg, W["attn_w_qg", 0], dx5, x4, mix_pre_g[1], "qg")
    dx3, dg_f1pre_1, dg_f1post_1, G["ffn1_w_in", 1], G["ffn1_w_out", 1] = _ffn_bwd(
        dx4, s_f1b, ffn1_pre_g[1], ffn1_post_g[1], W["ffn1_w_in", 1], W["ffn1_w_out", 1], "l1f1")

    dcum = jnp.pad(dcr.reshape(Bl, H, S).transpose(0, 2, 1), ((0, 0), (0, 0), (0, LANES - H))).reshape(T, LANES)
    dpf, dfb = _forget_bwd(dcum, pf, fb, Bl, S)
    dp = jnp.concatenate([dk.astype(MM_DTYPE), dv.astype(MM_DTYPE), dpf], axis=1)
    G_kv_full = _mm_tn(xn_kv, dp, "kv")
    G["kv_w", 0] = jnp.stack([jnp.pad(G_kv_full[:, s * kvc:(s + 1) * kvc], ((0, 0), (0, kvp - kvc))) for s in range(N_CHIP)])
    dx3, dg_kv = _mm_nt_pre(dp, kv_full, dx3, x3, kv_g, "kv")

    def halves_of(keys):
        return [G[k].reshape(N_CHIP, 2, G[k].shape[1] // 2, G[k].shape[2]) for k in keys]

    def pair_adds(keys, grads, recvs):
        wires, owns = [], []
        for k, g, r in zip(keys, grads, recvs):
            w, own = _rs_pair_add(g, r, place, f"{k[0]}{k[1]}")
            wires.append(w)
            owns.append(own)
        return wires, owns

    def chip_start(wires, tag, extra=()):
        lands = [lax.empty((3,) + w.shape[1:], w.dtype) for w in wires]
        lands += [jnp.zeros((N_DEV,) + e.shape, e.dtype) for e in extra]
        return _exchange_start(list(wires) + list(extra), lands, _chip_plan, 3 * len(wires) + (N_DEV - 1) * len(extra), tag)

    late = groups[2] + groups[1]
    last = groups[0]
    grads_2 = halves_of(second)
    p_sems, p_semr, grads_2, sib_2, token = _exchange_start(
        grads_2, [lax.empty((N_CHIP,) + g.shape[2:], g.dtype) for g in grads_2], _pair_plan, len(grads_2), "pair_second")

    dx2, dg_f2pre_0, dg_f2post_0, G["ffn2_w_in", 0], G["ffn2_w_out", 0] = _ffn_bwd(
        dx3, s_f2a, ffn2_pre_g[0], ffn2_post_g[0] + token[0, :1], W["ffn2_w_in", 0], W["ffn2_w_out", 0], "l0f2")
    grads_2, sib_2 = _exchange_wait(p_sems, p_semr, grads_2, sib_2, _pair_plan, dx2, "pair_second")
    wires_2, owns_2 = pair_adds(second, grads_2, sib_2)
    c_2 = chip_start(wires_2, "chip_second")
    dm_c, dg_mixpost_0, dz_c = _post_bwd_mm(dx2, m_c, mix_post_g[0] + c_2[4][0, :1], 1.0, w_o_conv, "conv_out")
    G["conv_w_out", 0] = _mm_tn_out(z_c, dm_c, "conv_out")
    db, dcg, dhh, dk_taps = _conv_bwd(bch, dz_c, k_taps, Bl, S)
    dbch = jnp.concatenate([db, dcg, dhh], axis=1)
    G["conv_w_in", 0] = _mm_tn_in(xn_c, dbch, "conv")
    dx1, dg_mixpre_0 = _mm_nt_pre(dbch, W["conv_w_in", 0], dx2, x1, mix_pre_g[0], "conv")
    grads_l = halves_of(late)
    wires_l, owns_l = pair_adds(late, grads_l, _rs_pair_send(grads_l))
    c_l = chip_start(wires_l, "chip_late")
    dx0, dg_f1pre_0, dg_f1post_0, G["ffn1_w_in", 0], G["ffn1_w_out", 0] = _ffn_bwd(
        dx1, s_f1a, ffn1_pre_g[0], ffn1_post_g[0] + c_l[4][0, :1], W["ffn1_w_in", 0], W["ffn1_w_out", 0], "l0f1")
    grad_x = dx0.reshape(Bl, S, D)

    _, recvs_2 = _exchange_wait(*c_2[:4], _chip_plan, dx0, "chip_second")
    _, recvs_l = _exchange_wait(*c_l[:4], _chip_plan, dx0, "chip_late")
    grads_1 = halves_of(last)
    wires_1, owns_1 = pair_adds(last, grads_1, _rs_pair_send(grads_1))

    def row(v):
        return jnp.pad(v.reshape(-1), (0, D - v.size)).reshape(1, D)

    small_parts = [dg_f1pre_0, dg_f1pre_1, dg_f1post_0, dg_f1post_1, dg_mixpre_0, dg_mixpre_1, dg_mixpost_0, dg_mixpost_1,
                   dg_f2pre_0, dg_f2pre_1, dg_f2post_0, dg_f2post_1, dg_kv, row(dfb[0, :H]), dk_taps[:3]]
    small = jnp.concatenate(small_parts, axis=0)
    small = jnp.pad(small, ((0, SMALL_ROWS - small.shape[0]), (0, 0)))
    c_1 = chip_start(wires_1, "chip_last", extra=[small])
    partial = {}

    def chip_adds(keys, owns, recvs):
        for (name, l), own, rcv in zip(keys, owns, recvs):
            partial[name] = _rs_chip_add(own, rcv, place, l, shards[name].shape[0], partial.get(name), f"{name}{l}")

    res = {}

    def adamw(names, reduced, after):
        for k, red in zip(names, reduced):
            w, m, v = given[k]
            g2 = red.reshape(-1, red.shape[-1])
            if k == "kv_w":
                g2 = g2[:, :kvc]
            flat = lambda a: a.reshape(-1, a.shape[-1])
            d, mn, vn = _adamw(flat(w), g2, flat(m), flat(v), k, after=after)
            res[k] = tuple(a.reshape(w.shape) for a in (g2, d, mn, vn))
        return d

    chip_adds(late + second, owns_l + owns_2, recvs_l + recvs_2)
    early = [k for k in partial if (k, 0) not in last]
    done = adamw(early, _rs_pair_share([partial[k] for k in early], "early"), c_1[4])
    _, recvs_1 = _exchange_wait(*c_1[:4], _chip_plan, done, "chip_last")
    chip_adds(last, owns_1, recvs_1[:-1])
    rest = [k for k, _ in last]
    adamw(rest, _rs_pair_share([partial[k] for k in rest], "last"), None)
    gsum = _sum_devices(recvs_1[-1], small, place)

    small_names = ["ffn1_pre_g", "ffn1_post_g", "mix_pre_g", "mix_post_g", "ffn2_pre_g", "ffn2_post_g"]
    small_given = dict(ffn1_pre_g=(ffn1_pre_g, m_ffn1_pre_g, v_ffn1_pre_g), ffn1_post_g=(ffn1_post_g, m_ffn1_post_g, v_ffn1_post_g),
                       mix_pre_g=(mix_pre_g, m_mix_pre_g, v_mix_pre_g), mix_post_g=(mix_post_g, m_mix_post_g, v_mix_post_g),
                       ffn2_pre_g=(ffn2_pre_g, m_ffn2_pre_g, v_ffn2_pre_g), ffn2_post_g=(ffn2_post_g, m_ffn2_post_g, v_ffn2_post_g))

    def pack(idx):
        rows_ = [small_given[k][idx] for k in small_names]
        rows_ += [row((kv_g, m_kv_g, v_kv_g)[idx]), row((forget_b, m_forget_b, v_forget_b)[idx])]
        rows_.append(jnp.pad((conv_k, m_conv_k, v_conv_k)[idx][0], ((0, 0), (0, D - dk_cols))))
        a = jnp.concatenate(rows_, axis=0)
        return jnp.pad(a, ((0, SMALL_ROWS - a.shape[0]), (0, 0)))

    g_taps = lax.dynamic_slice_in_dim(gsum[14:17], chip * dk_cols, dk_cols, axis=1)
    g_small = jnp.concatenate([gsum[:14], jnp.pad(g_taps, ((0, 0), (0, D - dk_cols))), gsum[17:]], axis=0)
    d_s, m_s, v_s = _adamw(pack(0), g_small, pack(1), pack(2), "small")
    for i, k in enumerate(small_names):
        res[k] = tuple(a[2 * i:2 * i + 2] for a in (g_small, d_s, m_s, v_s))
    res["kv_g"] = tuple(a[12] for a in (g_small, d_s, m_s, v_s))
    res["forget_b"] = tuple(a[13, :H] for a in (g_small, d_s, m_s, v_s))
    res["conv_k"] = tuple(a[14:17, :dk_cols][None] for a in (g_small, d_s, m_s, v_s))

    order = ["ffn1_pre_g", "ffn1_post_g", "ffn1_w_in", "ffn1_w_out", "mix_pre_g", "mix_post_g", "ffn2_pre_g", "ffn2_post_g",
             "ffn2_w_in", "ffn2_w_out", "conv_w_in", "conv_k", "conv_w_out", "kv_g", "kv_w", "forget_b", "attn_w_qg", "attn_w_o"]
    out = [loss, grad_x]
    for idx in range(4):
        out += [res[k][idx] for k in order]
    return tuple(out)
```

```python
import functools
import math

import jax
import jax.numpy as jnp
from jax import lax
from jax.experimental import pallas as pl
from jax.experimental.pallas import tpu as pltpu

F32 = jnp.float32
MM_DTYPE = jnp.bfloat16
WIRE_DTYPE = jnp.bfloat16

RMS_EPS = 1e-6
ADAM_LR = 0.001
ADAM_B1 = 0.9
ADAM_B2 = 0.999
ADAM_EPS = 1e-08
ADAM_WD = 0.01
ADAM_STEP = 10

HEAD_DIM = 64
LANES = 128
N_CHIP = 4
N_DEV = 8
ROW_TILE = 256
MM_TILE = 512
FUSED_TILE = 512
TN_TILE = 2048
ATT_BLOCK = 512
SMALL_ROWS = 24
VMEM_LIMIT = 56 * 1024 * 1024
MESH = pl.DeviceIdType.MESH
ANY = pl.BlockSpec(memory_space=pl.ANY)

NT = (((1,), (1,)), ((), ()))
TN = (((0,), (0,)), ((), ()))


def _tile(n, pref):
    if n <= pref:
        return n
    t = pref - pref % 16
    while n % t:
        t -= 16
    return t


def _params():
    return pltpu.CompilerParams(vmem_limit_bytes=VMEM_LIMIT)


def _sds(shape, dtype):
    return jax.ShapeDtypeStruct(shape, dtype)


def _rows(tm, c):
    return pl.BlockSpec((tm, c), lambda i: (i, 0))


def _whole(shape):
    return pl.BlockSpec(shape, lambda *_: (0,) * len(shape))


def _resident(shape):
    return pl.BlockSpec(shape, lambda *_: (0,) * len(shape), pipeline_mode=pl.Buffered(1))


def _rms_fwd(x, g, tag):
    T, D = x.shape
    tm = _tile(T, ROW_TILE)

    def body(x_ref, g_ref, o_ref):
        xv = x_ref[...]
        r = lax.rsqrt(jnp.mean(xv * xv, axis=-1, keepdims=True) + RMS_EPS)
        o_ref[...] = (xv * r * g_ref[...]).astype(o_ref.dtype)

    return pl.pallas_call(
        body, name=f"rms_fwd_{tag}", grid=(T // tm,),
        in_specs=[_rows(tm, D), _whole((1, D))], out_specs=_rows(tm, D),
        out_shape=_sds((T, D), MM_DTYPE), compiler_params=_params())(x, g.reshape(1, D))


def _post_fwd(x, h, g, alpha, tag):
    T, D = x.shape
    tm = _tile(T, ROW_TILE)

    def body(x_ref, h_ref, g_ref, o_ref):
        hv = h_ref[...]
        r = lax.rsqrt(jnp.mean(hv * hv, axis=-1, keepdims=True) + RMS_EPS)
        o_ref[...] = x_ref[...] + alpha * (hv * r * g_ref[...])

    return pl.pallas_call(
        body, name=f"post_fwd_{tag}", grid=(T // tm,),
        in_specs=[_rows(tm, D), _rows(tm, D), _whole((1, D))], out_specs=_rows(tm, D),
        out_shape=_sds((T, D), F32), compiler_params=_params())(x, h, g.reshape(1, D))


def _accumulate(ref, part, first):
    @pl.when(first)
    def _():
        ref[...] = part

    @pl.when(jnp.logical_not(first))
    def _():
        ref[...] += part


def _post_bwd(dx, h, g, alpha, tag):
    T, D = dx.shape
    tm = _tile(T, ROW_TILE)

    def body(dx_ref, h_ref, g_ref, dh_ref, dg_ref):
        hv = h_ref[...]
        r = lax.rsqrt(jnp.mean(hv * hv, axis=-1, keepdims=True) + RMS_EPS)
        hh = hv * r
        dyn = alpha * dx_ref[...]
        _accumulate(dg_ref, jnp.sum(dyn * hh, axis=0, keepdims=True), pl.program_id(0) == 0)
        dhh = dyn * g_ref[...]
        dh = r * (dhh - hh * jnp.mean(dhh * hh, axis=-1, keepdims=True))
        dh_ref[...] = dh.astype(dh_ref.dtype)

    return pl.pallas_call(
        body, name=f"post_bwd_{tag}", grid=(T // tm,),
        in_specs=[_rows(tm, D), _rows(tm, D), _whole((1, D))],
        out_specs=[_rows(tm, D), _whole((1, D))],
        out_shape=[_sds((T, D), MM_DTYPE), _sds((1, D), F32)],
        compiler_params=_params())(dx, h, g.reshape(1, D))


def _pre_bwd(dres, dxn, x, g, tag):
    T, D = x.shape
    tm = _tile(T, ROW_TILE)

    def body(dres_ref, dxn_ref, x_ref, g_ref, dx_ref, dg_ref):
        xv = x_ref[...]
        r = lax.rsqrt(jnp.mean(xv * xv, axis=-1, keepdims=True) + RMS_EPS)
        xh = xv * r
        dn = dxn_ref[...]
        _accumulate(dg_ref, jnp.sum(dn * xh, axis=0, keepdims=True), pl.program_id(0) == 0)
        dxh = dn * g_ref[...]
        dx_ref[...] = dres_ref[...] + r * (dxh - xh * jnp.mean(dxh * xh, axis=-1, keepdims=True))

    return pl.pallas_call(
        body, name=f"pre_bwd_{tag}", grid=(T // tm,),
        in_specs=[_rows(tm, D), _rows(tm, D), _rows(tm, D), _whole((1, D))],
        out_specs=[_rows(tm, D), _whole((1, D))],
        out_shape=[_sds((T, D), F32), _sds((1, D), F32)],
        compiler_params=_params())(dres, dxn, x, g.reshape(1, D))


def _swiglu_fwd(hgu, tag):
    T, F2 = hgu.shape
    F = F2 // 2
    tm = _tile(T, ROW_TILE)

    def body(g_ref, u_ref, o_ref):
        g = g_ref[...].astype(F32)
        o_ref[...] = (g * jax.nn.sigmoid(g) * u_ref[...].astype(F32)).astype(o_ref.dtype)

    return pl.pallas_call(
        body, name=f"swiglu_fwd_{tag}", grid=(T // tm,),
        in_specs=[pl.BlockSpec((tm, F), lambda i: (i, 0)), pl.BlockSpec((tm, F), lambda i: (i, 1))],
        out_specs=_rows(tm, F), out_shape=_sds((T, F), MM_DTYPE), compiler_params=_params())(hgu, hgu)


def _swiglu_bwd(hgu, da, tag):
    T, F2 = hgu.shape
    F = F2 // 2
    tm = _tile(T, ROW_TILE)

    def body(h_ref, da_ref, o_ref):
        g = h_ref[:, :F].astype(F32)
        u = h_ref[:, F:].astype(F32)
        d = da_ref[...].astype(F32)
        sg = jax.nn.sigmoid(g)
        o_ref[:, :F] = (d * u * sg * (1.0 + g * (1.0 - sg))).astype(o_ref.dtype)
        o_ref[:, F:] = (d * g * sg).astype(o_ref.dtype)

    return pl.pallas_call(
        body, name=f"swiglu_bwd_{tag}", grid=(T // tm,),
        in_specs=[_rows(tm, F2), _rows(tm, F)], out_specs=_rows(tm, F2),
        out_shape=_sds((T, F2), MM_DTYPE), compiler_params=_params())(hgu, da)


def _loss_grad(y, tgt):
    T, D = y.shape
    tm = _tile(T, ROW_TILE)

    def body(y_ref, t_ref, dy_ref, l_ref):
        e = y_ref[...] - t_ref[...]
        row = jnp.mean(e * e, axis=-1, keepdims=True)
        part = jnp.broadcast_to(jnp.sum(row, axis=0, keepdims=True), (8, LANES))
        _accumulate(l_ref, part, pl.program_id(0) == 0)
        dy_ref[...] = e * (1.0 / D)

    dy, lsum = pl.pallas_call(
        body, name="loss_grad", grid=(T // tm,),
        in_specs=[_rows(tm, D), _rows(tm, D)], out_specs=[_rows(tm, D), _whole((8, LANES))],
        out_shape=[_sds((T, D), F32), _sds((8, LANES), F32)], compiler_params=_params())(y, tgt)
    return dy, 0.5 * lsum[0, 0]


def _shift_down(u, d, rows):
    return jnp.where(rows >= d, pltpu.roll(u, d, 0), 0.0)


def _shift_up(u, d, rows, S):
    return jnp.where(rows < S - d, pltpu.roll(u, S - d, 0), 0.0)


def _conv_fwd(bch, k8, Bl, S):
    T, D3 = bch.shape
    D = D3 // 3
    dc = min(D, 2 * LANES)
    nd = D // dc

    def body(b_ref, c_ref, h_ref, k_ref, z_ref):
        rows = lax.broadcasted_iota(jnp.int32, (S, 1), 0)
        u = c_ref[...].astype(F32) * h_ref[...].astype(F32)
        y = k_ref[2:3, :] * u + k_ref[1:2, :] * _shift_down(u, 1, rows) + k_ref[0:1, :] * _shift_down(u, 2, rows)
        z_ref[...] = (b_ref[...].astype(F32) * y).astype(z_ref.dtype)

    return pl.pallas_call(
        body, name="conv_fwd", grid=(Bl, nd),
        in_specs=[pl.BlockSpec((S, dc), lambda b, j: (b, j)),
                  pl.BlockSpec((S, dc), lambda b, j: (b, nd + j)),
                  pl.BlockSpec((S, dc), lambda b, j: (b, 2 * nd + j)),
                  pl.BlockSpec((8, dc), lambda b, j: (0, j))],
        out_specs=pl.BlockSpec((S, dc), lambda b, j: (b, j)),
        out_shape=_sds((T, D), MM_DTYPE), compiler_params=_params())(bch, bch, bch, k8)


def _conv_bwd(bch, dz, k8, Bl, S):
    T, D3 = bch.shape
    D = D3 // 3
    dc = min(D, 2 * LANES)
    nd = D // dc

    def body(b_ref, c_ref, h_ref, dz_ref, k_ref, db_ref, dc_ref, dh_ref, dk_ref):
        rows = lax.broadcasted_iota(jnp.int32, (S, 1), 0)
        bv = b_ref[...].astype(F32)
        cv = c_ref[...].astype(F32)
        hv = h_ref[...].astype(F32)
        dzv = dz_ref[...].astype(F32)
        u = cv * hv
        u1 = _shift_down(u, 1, rows)
        u2 = _shift_down(u, 2, rows)
        y = k_ref[2:3, :] * u + k_ref[1:2, :] * u1 + k_ref[0:1, :] * u2
        db_ref[...] = (dzv * y).astype(db_ref.dtype)
        dy = dzv * bv
        du = k_ref[2:3, :] * dy + k_ref[1:2, :] * _shift_up(dy, 1, rows, S) + k_ref[0:1, :] * _shift_up(dy, 2, rows, S)
        dc_ref[...] = (du * hv).astype(dc_ref.dtype)
        dh_ref[...] = (du * cv).astype(dh_ref.dtype)

        @pl.when(pl.program_id(1) == 0)
        def _():
            dk_ref[...] = jnp.zeros_like(dk_ref)

        dk_ref[0:1, :] += jnp.sum(dy * u2, axis=0, keepdims=True)
        dk_ref[1:2, :] += jnp.sum(dy * u1, axis=0, keepdims=True)
        dk_ref[2:3, :] += jnp.sum(dy * u, axis=0, keepdims=True)

    seq = lambda off: pl.BlockSpec((S, dc), lambda j, b: (b, off + j))
    return pl.pallas_call(
        body, name="conv_bwd", grid=(nd, Bl),
        in_specs=[seq(0), seq(nd), seq(2 * nd), seq(0), pl.BlockSpec((8, dc), lambda j, b: (0, j))],
        out_specs=[seq(0), seq(0), seq(0), pl.BlockSpec((8, dc), lambda j, b: (0, j))],
        out_shape=[_sds((T, D), MM_DTYPE)] * 3 + [_sds((8, D), F32)],
        compiler_params=_params())(bch, bch, bch, dz, k8)


def _forget_fwd(pf, fb, Bl, S):
    T = pf.shape[0]

    def body(p_ref, fb_ref, c_ref):
        rows = lax.broadcasted_iota(jnp.int32, (S, 1), 0)
        z = p_ref[...] + fb_ref[...]
        acc = jnp.minimum(z, 0.0) - jnp.log1p(jnp.exp(-jnp.abs(z)))
        d = 1
        while d < S:
            acc = acc + _shift_down(acc, d, rows)
            d *= 2
        c_ref[...] = acc

    return pl.pallas_call(
        body, name="forget_fwd", grid=(Bl,),
        in_specs=[_rows(S, LANES), _whole((1, LANES))], out_specs=_rows(S, LANES),
        out_shape=_sds((T, LANES), F32), compiler_params=_params())(pf, fb)


def _forget_bwd(dc, pf, fb, Bl, S):
    T = pf.shape[0]

    def body(dc_ref, p_ref, fb_ref, df_ref, dfb_ref):
        rows = lax.broadcasted_iota(jnp.int32, (S, 1), 0)
        acc = dc_ref[...]
        d = 1
        while d < S:
            acc = acc + _shift_up(acc, d, rows, S)
            d *= 2
        df = acc * jax.nn.sigmoid(-(p_ref[...] + fb_ref[...]))
        df_ref[...] = df.astype(df_ref.dtype)
        _accumulate(dfb_ref, jnp.sum(df, axis=0, keepdims=True), pl.program_id(0) == 0)

    return pl.pallas_call(
        body, name="forget_bwd", grid=(Bl,),
        in_specs=[_rows(S, LANES), _rows(S, LANES), _whole((1, LANES))],
        out_specs=[_rows(S, LANES), _whole((1, LANES))],
        out_shape=[_sds((T, LANES), MM_DTYPE), _sds((1, LANES), F32)],
        compiler_params=_params())(dc, pf, fb)


def _head_mask(h):
    lane = lax.broadcasted_iota(jnp.int32, (1, LANES), 1)
    return (lane >= h * HEAD_DIM) & (lane < (h + 1) * HEAD_DIM)


def _attn_fwd(qg, kv, c_col, c_row, Bl, S, D):
    T = Bl * S
    H = D // HEAD_DIM
    HP = D // LANES
    bq = min(S, ATT_BLOCK)
    nq = S // bq
    scale = 1.0 / math.sqrt(HEAD_DIM)

    def body(q_ref, k_ref, v_ref, cc_ref, cr_ref, o_ref, lse_ref):
        i = pl.program_id(2)
        q2 = q_ref[...]
        qh = [q2 * (_head_mask(h).astype(F32) * scale).astype(q2.dtype) for h in range(2)]
        cc = [cc_ref[h][:, :1] for h in range(2)]
        diag = lax.broadcasted_iota(jnp.int32, (1, bq), 1) <= lax.broadcasted_iota(jnp.int32, (bq, 1), 0)

        def block(j, carry, on_diagonal):
            off = pl.multiple_of(j * bq, bq)
            kj = k_ref[pl.ds(off, bq), :]
            vj = v_ref[pl.ds(off, bq), :]
            new = []
            for h in range(2):
                m, l, acc = carry[h]
                s = lax.dot_general(qh[h], kj, NT, preferred_element_type=F32) + cc[h] - cr_ref[h, j]
                if on_diagonal:
                    s = jnp.where(diag, s, -jnp.inf)
                m_new = jnp.maximum(m, jnp.max(s, axis=1, keepdims=True))
                p = jnp.exp(s - m_new)
                a = jnp.exp(m - m_new)
                l = a * l + jnp.sum(p, axis=1, keepdims=True)
                acc = a * acc + jnp.dot(p.astype(MM_DTYPE), vj, preferred_element_type=F32)
                new.append((m_new, l, acc))
            return tuple(new)

        one = (jnp.full((bq, 1), -jnp.inf, F32), jnp.zeros((bq, 1), F32), jnp.zeros((bq, LANES), F32))
        carry = lax.fori_loop(0, i, lambda j, c: block(j, c, False), (one, one))
        carry = block(i, carry, True)
        outs = []
        for h in range(2):
            m, l, acc = carry[h]
            outs.append(acc / l)
            lse_ref[h] = jnp.broadcast_to(m + jnp.log(l), (bq, LANES))
        o_ref[...] = jnp.where(_head_mask(0), outs[0], outs[1])

    return pl.pallas_call(
        body, name="attn_fwd", grid=(Bl, HP, nq),
        in_specs=[pl.BlockSpec((bq, LANES), lambda b, hp, i: (b * nq + i, hp)),
                  pl.BlockSpec((S, LANES), lambda b, hp, i: (b, hp)),
                  pl.BlockSpec((S, LANES), lambda b, hp, i: (b, HP + hp)),
                  pl.BlockSpec((None, 2, bq, LANES), lambda b, hp, i: (b, hp, i, 0)),
                  pl.BlockSpec((None, 2, nq, 1, bq), lambda b, hp, i: (b, hp, 0, 0, 0))],
        out_specs=[pl.BlockSpec((bq, LANES), lambda b, hp, i: (b * nq + i, hp)),
                   pl.BlockSpec((None, 2, bq, LANES), lambda b, hp, i: (b, hp, i, 0))],
        out_shape=[_sds((T, D), F32), _sds((Bl, H, S, LANES), F32)],
        compiler_params=_params())(qg, kv, kv, c_col, c_row)


def _attn_bwd(qg, kv, do, lse, c_col, c_row, Bl, S, D):
    T = Bl * S
    H = D // HEAD_DIM
    HP = D // LANES
    bq = min(S, ATT_BLOCK)
    nq = S // bq
    scale = 1.0 / math.sqrt(HEAD_DIM)

    def body(q_ref, k_ref, v_ref, do_ref, lse_ref, cc_ref, cr_ref, dq_ref, dk_ref, dv_ref, dcr_ref, p_sc, dp_sc):
        i = pl.program_id(2)

        @pl.when(i == 0)
        def _():
            dk_ref[...] = jnp.zeros_like(dk_ref)
            dv_ref[...] = jnp.zeros_like(dv_ref)
            dcr_ref[...] = jnp.zeros_like(dcr_ref)

        q2 = q_ref[...]
        do2 = do_ref[...]
        masks = [_head_mask(h).astype(F32) for h in range(2)]
        qh = [q2 * (masks[h] * scale).astype(q2.dtype) for h in range(2)]
        doh = [do2 * masks[h].astype(do2.dtype) for h in range(2)]
        cc = [cc_ref[h][:, :1] for h in range(2)]
        lse = [lse_ref[h][:, :1] for h in range(2)]
        diag = lax.broadcasted_iota(jnp.int32, (1, bq), 1) <= lax.broadcasted_iota(jnp.int32, (bq, 1), 0)

        def sweep1(j, delta, on_diagonal):
            off = pl.multiple_of(j * bq, bq)
            kj = k_ref[pl.ds(off, bq), :]
            vj = v_ref[pl.ds(off, bq), :]
            new = []
            dv = None
            for h in range(2):
                s = lax.dot_general(qh[h], kj, NT, preferred_element_type=F32) + cc[h] - cr_ref[h, j]
                if on_diagonal:
                    s = jnp.where(diag, s, -jnp.inf)
                p = jnp.exp(s - lse[h])
                dp = lax.dot_general(doh[h], vj, NT, preferred_element_type=F32)
                p_sc[h, j] = p
                dp_sc[h, j] = dp
                part = lax.dot_general(p.astype(MM_DTYPE), doh[h], TN, preferred_element_type=F32)
                dv = part if dv is None else dv + part
                new.append(delta[h] + jnp.sum(p * dp, axis=1, keepdims=True))
            dv_ref[pl.ds(off, bq), :] += dv
            return tuple(new)

        zero = jnp.zeros((bq, 1), F32)
        delta = lax.fori_loop(0, i, lambda j, d: sweep1(j, d, False), (zero, zero))
        delta = sweep1(i, delta, True)

        def sweep2(j, dq):
            off = pl.multiple_of(j * bq, bq)
            kj = k_ref[pl.ds(off, bq), :]
            dk = None
            for h in range(2):
                ds = p_sc[h, j] * (dp_sc[h, j] - delta[h])
                dcr_ref[h, j] -= jnp.sum(ds, axis=0, keepdims=True)
                dsb = ds.astype(MM_DTYPE)
                dq = dq + jnp.dot(dsb, kj * (masks[h] * scale).astype(kj.dtype), preferred_element_type=F32)
                part = lax.dot_general(dsb, qh[h], TN, preferred_element_type=F32)
                dk = part if dk is None else dk + part
            dk_ref[pl.ds(off, bq), :] += dk
            return dq

        dq_ref[...] = lax.fori_loop(0, i + 1, sweep2, jnp.zeros((bq, LANES), F32))

    blk = lambda col: pl.BlockSpec((bq, LANES), lambda b, hp, i: (b * nq + i, col(hp)))
    seq = lambda col: pl.BlockSpec((S, LANES), lambda b, hp, i: (b, col(hp)))
    per_head = pl.BlockSpec((None, 2, bq, LANES), lambda b, hp, i: (b, hp, i, 0))
    rows = pl.BlockSpec((None, 2, nq, 1, bq), lambda b, hp, i: (b, hp, 0, 0, 0))
    return pl.pallas_call(
        body, name="attn_bwd", grid=(Bl, HP, nq),
        in_specs=[blk(lambda hp: hp), seq(lambda hp: hp), seq(lambda hp: HP + hp), blk(lambda hp: hp),
                  per_head, per_head, rows],
        out_specs=[blk(lambda hp: hp), seq(lambda hp: hp), seq(lambda hp: hp), rows],
        out_shape=[_sds((T, D), F32), _sds((T, D), F32), _sds((T, D), F32), _sds((Bl, H, nq, 1, bq), F32)],
        scratch_shapes=[pltpu.VMEM((2, nq, bq, bq), F32), pltpu.VMEM((2, nq, bq, bq), F32)],
        compiler_params=_params())(qg, kv, kv, do, lse, c_col, c_row)


def _gate_fwd(qg, o):
    T, D = o.shape
    tm = _tile(T, ROW_TILE)

    def body(g_ref, o_ref, z_ref):
        z_ref[...] = (jax.nn.sigmoid(g_ref[...].astype(F32)) * o_ref[...]).astype(z_ref.dtype)

    return pl.pallas_call(
        body, name="gate_fwd", grid=(T // tm,),
        in_specs=[pl.BlockSpec((tm, D), lambda i: (i, 1)), _rows(tm, D)], out_specs=_rows(tm, D),
        out_shape=_sds((T, D), MM_DTYPE), compiler_params=_params())(qg, o)


def _gate_do(dz, qg):
    T, D = dz.shape
    tm = _tile(T, ROW_TILE)

    def body(dz_ref, g_ref, do_ref):
        do_ref[...] = (dz_ref[...].astype(F32) * jax.nn.sigmoid(g_ref[...].astype(F32))).astype(do_ref.dtype)

    return pl.pallas_call(
        body, name="gate_do", grid=(T // tm,),
        in_specs=[_rows(tm, D), pl.BlockSpec((tm, D), lambda i: (i, 1))], out_specs=_rows(tm, D),
        out_shape=_sds((T, D), MM_DTYPE), compiler_params=_params())(dz, qg)


def _gate_bwd(dz, qg, o, dq):
    T, D = dz.shape
    tm = _tile(T, ROW_TILE)

    def body(dz_ref, g_ref, o_ref, dq_ref, out_ref):
        g = g_ref[...].astype(F32)
        sg = jax.nn.sigmoid(g)
        out_ref[:, :D] = dq_ref[...].astype(out_ref.dtype)
        out_ref[:, D:] = (dz_ref[...].astype(F32) * o_ref[...] * sg * (1.0 - sg)).astype(out_ref.dtype)

    return pl.pallas_call(
        body, name="gate_bwd", grid=(T // tm,),
        in_specs=[_rows(tm, D), pl.BlockSpec((tm, D), lambda i: (i, 1)), _rows(tm, D), _rows(tm, D)],
        out_specs=_rows(tm, 2 * D), out_shape=_sds((T, 2 * D), MM_DTYPE),
        compiler_params=_params())(dz, qg, o, dq)


def _mm_in(a, wg, out_dtype, tag, l=None):
    T, K = a.shape
    n = wg.shape[-1]
    tm = _tile(T, MM_TILE)
    if l is None:
        w_spec = _whole((N_CHIP, K, n))
    else:
        w_spec = pl.BlockSpec((None, N_CHIP, K, n), lambda i: (l, 0, 0, 0))

    def body(a_ref, w_ref, o_ref):
        av = a_ref[...]
        for s in range(N_CHIP):
            o_ref[:, s * n:(s + 1) * n] = jnp.dot(av, w_ref[s], preferred_element_type=F32).astype(o_ref.dtype)

    return pl.pallas_call(
        body, name=f"mm_in_{tag}", grid=(T // tm,),
        in_specs=[_rows(tm, K), w_spec], out_specs=_rows(tm, N_CHIP * n),
        out_shape=_sds((T, N_CHIP * n), out_dtype), compiler_params=_params())(a, wg)


def _mm_nt_in(dy, wg, tag, l=None):
    T = dy.shape[0]
    K, n = wg.shape[-2:]
    tm = _tile(T, MM_TILE)
    if l is None:
        w_spec = _whole((N_CHIP, K, n))
    else:
        w_spec = pl.BlockSpec((None, N_CHIP, K, n), lambda i: (l, 0, 0, 0))

    def body(d_ref, w_ref, o_ref):
        acc = None
        for s in range(N_CHIP):
            part = lax.dot_general(d_ref[:, s * n:(s + 1) * n], w_ref[s], NT, preferred_element_type=F32)
            acc = part if acc is None else acc + part
        o_ref[...] = acc

    return pl.pallas_call(
        body, name=f"mm_nt_in_{tag}", grid=(T // tm,),
        in_specs=[_rows(tm, N_CHIP * n), w_spec], out_specs=_rows(tm, K),
        out_shape=_sds((T, K), F32), compiler_params=_params())(dy, wg)


def _mm_nn(a, b, out_dtype, tag):
    T, K = a.shape
    N = b.shape[1]
    tm = _tile(T, MM_TILE)

    def body(a_ref, b_ref, o_ref):
        o_ref[...] = jnp.dot(a_ref[...], b_ref[...], preferred_element_type=F32).astype(o_ref.dtype)

    return pl.pallas_call(
        body, name=f"mm_nn_{tag}", grid=(T // tm,),
        in_specs=[_rows(tm, K), _whole((K, N))], out_specs=_rows(tm, N),
        out_shape=_sds((T, N), out_dtype), compiler_params=_params())(a, b)


def _mm_nt(a, b, out_dtype, tag):
    T, C = a.shape
    N = b.shape[0]
    tm = _tile(T, MM_TILE)
    nb = N
    for cand in (1408, 1024):
        if N > cand and N % cand == 0:
            nb = cand
            break

    def body(a_ref, b_ref, o_ref):
        o_ref[...] = lax.dot_general(a_ref[...], b_ref[...], NT, preferred_element_type=F32).astype(o_ref.dtype)

    return pl.pallas_call(
        body, name=f"mm_nt_{tag}", grid=(N // nb, T // tm),
        in_specs=[pl.BlockSpec((tm, C), lambda j, i: (i, 0)), pl.BlockSpec((nb, C), lambda j, i: (j, 0))],
        out_specs=pl.BlockSpec((tm, nb), lambda j, i: (i, j)),
        out_shape=_sds((T, N), out_dtype), compiler_params=_params())(a, b)


def _mm_tn_in(a, dy, tag, l=None, prev=None):
    T, K = a.shape
    n = dy.shape[1] // N_CHIP
    tt = _tile(T, TN_TILE)

    def body(a_ref, d_ref, *rest):
        o_ref = rest[-1]
        part = lax.dot_general(a_ref[...], d_ref[...], TN, preferred_element_type=F32)
        _accumulate(o_ref, part, pl.program_id(1) == 0)

    in_specs = [pl.BlockSpec((tt, K), lambda s, t: (t, 0)), pl.BlockSpec((tt, n), lambda s, t: (t, s))]
    args = [a, dy]
    kw = {}
    if l is None:
        out_spec = pl.BlockSpec((None, K, n), lambda s, t: (s, 0, 0))
        out_shape = _sds((N_CHIP, K, n), F32)
    else:
        out_spec = pl.BlockSpec((None, None, K, n), lambda s, t: (l, s, 0, 0))
        out_shape = _sds((2, N_CHIP, K, n), F32)
        if prev is not None:
            in_specs.append(ANY)
            args.append(prev)
            kw["input_output_aliases"] = {2: 0}
    return pl.pallas_call(
        body, name=f"mm_tn_in_{tag}", grid=(N_CHIP, T // tt),
        in_specs=in_specs, out_specs=out_spec, out_shape=out_shape,
        compiler_params=_params(), **kw)(*args)


def _mm_tn_out(act, dh, tag, l=None, prev=None):
    T, R4 = act.shape
    D = dh.shape[1]
    r = R4 // N_CHIP
    g = 1 if r % LANES == 0 else 2
    tt = _tile(T, TN_TILE)

    def body(a_ref, d_ref, *rest):
        o_ref = rest[-1]
        part = lax.dot_general(a_ref[...], d_ref[...], TN, preferred_element_type=F32)
        first = pl.program_id(1) == 0
        for q in range(g):
            _accumulate(o_ref.at[q], part[q * r:(q + 1) * r], first)

    in_specs = [pl.BlockSpec((tt, g * r), lambda s, t: (t, s)), pl.BlockSpec((tt, D), lambda s, t: (t, 0))]
    args = [act, dh]
    kw = {}
    if l is None:
        out_spec = pl.BlockSpec((g, r, D), lambda s, t: (s, 0, 0))
        out_shape = _sds((N_CHIP, r, D), F32)
    else:
        out_spec = pl.BlockSpec((None, g, r, D), lambda s, t: (l, s, 0, 0))
        out_shape = _sds((2, N_CHIP, r, D), F32)
        if prev is not None:
            in_specs.append(ANY)
            args.append(prev)
            kw["input_output_aliases"] = {2: 0}
    return pl.pallas_call(
        body, name=f"mm_tn_out_{tag}", grid=(N_CHIP // g, T // tt),
        in_specs=in_specs, out_specs=out_spec, out_shape=out_shape,
        compiler_params=_params(), **kw)(*args)


def _mm_tn(a, b, tag):
    T, K = a.shape
    N = b.shape[1]
    tt = _tile(T, TN_TILE)

    def body(a_ref, b_ref, o_ref):
        part = lax.dot_general(a_ref[...], b_ref[...], TN, preferred_element_type=F32)
        _accumulate(o_ref, part, pl.program_id(0) == 0)

    return pl.pallas_call(
        body, name=f"mm_tn_{tag}", grid=(T // tt,),
        in_specs=[_rows(tt, K), _rows(tt, N)], out_specs=_whole((K, N)),
        out_shape=_sds((K, N), F32), compiler_params=_params())(a, b)


def _norm_mm_in(x, g, wg, tag, swiglu=False):
    T, D = x.shape
    n = wg.shape[-1]
    tm = _tile(T, FUSED_TILE)
    half = N_CHIP // 2

    def body(x_ref, g_ref, w_ref, xn_ref, y_ref, *rest):
        xv = x_ref[...]
        r = lax.rsqrt(jnp.mean(xv * xv, axis=-1, keepdims=True) + RMS_EPS)
        xn = (xv * r * g_ref[...]).astype(xn_ref.dtype)
        xn_ref[...] = xn

        def product(s):
            p = jnp.dot(xn, w_ref[s], preferred_element_type=F32)
            y_ref[:, s * n:(s + 1) * n] = p.astype(y_ref.dtype)
            return p

        if swiglu:
            for q in range(half):
                gate, up = product(q), product(half + q)
                rest[0][:, q * n:(q + 1) * n] = (gate * jax.nn.sigmoid(gate) * up).astype(rest[0].dtype)
        else:
            for s in range(N_CHIP):
                product(s)

    out_specs = [_rows(tm, D), _rows(tm, N_CHIP * n)]
    out_shape = [_sds((T, D), MM_DTYPE), _sds((T, N_CHIP * n), MM_DTYPE)]
    if swiglu:
        out_specs.append(_rows(tm, half * n))
        out_shape.append(_sds((T, half * n), MM_DTYPE))
    return pl.pallas_call(
        body, name=f"norm_mm_in_{tag}", grid=(T // tm,),
        in_specs=[_rows(tm, D), _whole((1, D)), _resident((N_CHIP, D, n))], out_specs=out_specs,
        out_shape=out_shape, compiler_params=_params())(x, g.reshape(1, D), wg)


def _mm_out_post(a, b, x, g, alpha, tag):
    T, K = a.shape
    D = b.shape[1]
    tm = _tile(T, FUSED_TILE)

    def body(a_ref, b_ref, x_ref, g_ref, h_ref, o_ref):
        hv = jnp.dot(a_ref[...], b_ref[...], preferred_element_type=F32)
        h_ref[...] = hv
        r = lax.rsqrt(jnp.mean(hv * hv, axis=-1, keepdims=True) + RMS_EPS)
        o_ref[...] = x_ref[...] + alpha * (hv * r * g_ref[...])

    return pl.pallas_call(
        body, name=f"mm_out_post_{tag}", grid=(T // tm,),
        in_specs=[_rows(tm, K), _resident((K, D)), _rows(tm, D), _whole((1, D))],
        out_specs=[_rows(tm, D), _rows(tm, D)], out_shape=[_sds((T, D), F32)] * 2,
        compiler_params=_params())(a, b, x, g.reshape(1, D))


def _post_bwd_mm(dx, h, g, alpha, b, tag, hgu=None):
    T, D = dx.shape
    K = b.shape[0]
    tm = _tile(T, FUSED_TILE)

    def body(dx_ref, h_ref, g_ref, b_ref, *rest):
        dh_ref, dg_ref, out_ref = rest[-3:]
        hv = h_ref[...]
        r = lax.rsqrt(jnp.mean(hv * hv, axis=-1, keepdims=True) + RMS_EPS)
        hh = hv * r
        dyn = alpha * dx_ref[...]
        _accumulate(dg_ref, jnp.sum(dyn * hh, axis=0, keepdims=True), pl.program_id(0) == 0)
        dhh = dyn * g_ref[...]
        dh = (r * (dhh - hh * jnp.mean(dhh * hh, axis=-1, keepdims=True))).astype(dh_ref.dtype)
        dh_ref[...] = dh
        da = lax.dot_general(dh, b_ref[...], NT, preferred_element_type=F32)
        if hgu is None:
            out_ref[...] = da.astype(out_ref.dtype)
        else:
            gate = rest[0][:, :K].astype(F32)
            up = rest[0][:, K:].astype(F32)
            sg = jax.nn.sigmoid(gate)
            out_ref[:, :K] = (da * up * sg * (1.0 + gate * (1.0 - sg))).astype(out_ref.dtype)
            out_ref[:, K:] = (da * gate * sg).astype(out_ref.dtype)

    in_specs = [_rows(tm, D), _rows(tm, D), _whole((1, D)), _resident((K, D))]
    args = [dx, h, g.reshape(1, D), b]
    wide = K
    if hgu is not None:
        wide = 2 * K
        in_specs.append(_rows(tm, wide))
        args.append(hgu)
    return pl.pallas_call(
        body, name=f"post_bwd_mm_{tag}", grid=(T // tm,), in_specs=in_specs,
        out_specs=[_rows(tm, D), _whole((1, D)), _rows(tm, wide)],
        out_shape=[_sds((T, D), MM_DTYPE), _sds((1, D), F32), _sds((T, wide), MM_DTYPE)],
        compiler_params=_params())(*args)


def _mm_nt_pre(dy, w, dres, x, g, tag):
    T, C = dy.shape
    D = x.shape[1]
    tm = _tile(T, FUSED_TILE)
    n = w.shape[-1]

    def body(dy_ref, w_ref, dres_ref, x_ref, g_ref, dx_ref, dg_ref):
        if w.ndim == 2:
            dn = lax.dot_general(dy_ref[...], w_ref[...], NT, preferred_element_type=F32)
        else:
            dn = None
            for s in range(N_CHIP):
                part = lax.dot_general(dy_ref[:, s * n:(s + 1) * n], w_ref[s], NT, preferred_element_type=F32)
                dn = part if dn is None else dn + part
        xv = x_ref[...]
        r = lax.rsqrt(jnp.mean(xv * xv, axis=-1, keepdims=True) + RMS_EPS)
        xh = xv * r
        _accumulate(dg_ref, jnp.sum(dn * xh, axis=0, keepdims=True), pl.program_id(0) == 0)
        dxh = dn * g_ref[...]
        dx_ref[...] = dres_ref[...] + r * (dxh - xh * jnp.mean(dxh * xh, axis=-1, keepdims=True))

    return pl.pallas_call(
        body, name=f"mm_nt_pre_{tag}", grid=(T // tm,),
        in_specs=[_rows(tm, C), _resident(w.shape), _rows(tm, D), _rows(tm, D), _whole((1, D))],
        out_specs=[_rows(tm, D), _whole((1, D))], out_shape=[_sds((T, D), F32), _sds((1, D), F32)],
        compiler_params=_params())(dy, w, dres, x, g.reshape(1, D))


def _adamw(w, g, m, v, tag, after=None):
    R, C = w.shape
    tr = _tile(R, ROW_TILE)
    extra = [] if after is None else [after]

    def body(w_ref, g_ref, m_ref, v_ref, *rest):
        go_ref, d_ref, mo_ref, vo_ref = rest[-4:]
        gv = g_ref[...]
        go_ref[...] = gv
        mn = ADAM_B1 * m_ref[...] + (1.0 - ADAM_B1) * gv
        vn = ADAM_B2 * v_ref[...] + (1.0 - ADAM_B2) * (gv * gv)
        m_hat = mn / (1.0 - ADAM_B1 ** ADAM_STEP)
        v_hat = vn / (1.0 - ADAM_B2 ** ADAM_STEP)
        d_ref[...] = -ADAM_LR * (m_hat / (jnp.sqrt(v_hat) + ADAM_EPS) + ADAM_WD * w_ref[...])
        mo_ref[...] = mn
        vo_ref[...] = vn

    return pl.pallas_call(
        body, name=f"adamw_{tag}", grid=(R // tr,),
        in_specs=[_rows(tr, C)] * 4 + [ANY] * len(extra), out_specs=[_rows(tr, C)] * 4,
        out_shape=[_sds((R, C), F32)] * 4, compiler_params=_params())(w, g, m, v, *extra)


def _sum_devices(gall, own, place):
    _, R, C = gall.shape

    def body(place_ref, g_ref, s_ref, o_ref):
        me = 2 * place_ref[1] + place_ref[0]
        acc = None
        for d in range(N_DEV):
            term = jnp.where(me == d, s_ref[...], g_ref[d])
            acc = term if acc is None else acc + term
        o_ref[...] = acc

    grid_spec = pltpu.PrefetchScalarGridSpec(
        num_scalar_prefetch=1, grid=(1,),
        in_specs=[pl.BlockSpec((N_DEV, R, C), lambda i, p: (0, 0, 0)), pl.BlockSpec((R, C), lambda i, p: (0, 0))],
        out_specs=pl.BlockSpec((R, C), lambda i, p: (0, 0)))
    return pl.pallas_call(
        body, name="sum_devices", grid_spec=grid_spec, out_shape=_sds((R, C), F32),
        compiler_params=_params())(place, gall, own)


HBM = pl.BlockSpec(memory_space=pltpu.HBM)
SEM = pl.BlockSpec(memory_space=pltpu.SEMAPHORE)
EFFECT = pltpu.SideEffectType.DATAFLOW_SIDE_EFFECTING


def _place():
    x, y, c = lax.axis_index("x"), lax.axis_index("y"), lax.axis_index("c")
    chips = ((1 - x, y), (x, 1 - y), (1 - x, 1 - y))
    return x, y, c, chips


def _remote(src, dst, send_sem, recv_sem, dev):
    return pltpu.make_async_remote_copy(src_ref=src, dst_ref=dst, send_sem=send_sem, recv_sem=recv_sem,
                                        device_id=dev, device_id_type=MESH)


def _in_hbm(a):
    return pltpu.with_memory_space_constraint(a, pltpu.HBM)


def _own_slot(w4, l, dtype, place, tag):
    _, _, r, col = w4.shape
    tr = _tile(r, 2 * ROW_TILE)

    def body(place_ref, x_ref, o_ref):
        o_ref[...] = x_ref[...].astype(o_ref.dtype)

    grid_spec = pltpu.PrefetchScalarGridSpec(
        num_scalar_prefetch=1, grid=(2, r // tr),
        in_specs=[pl.BlockSpec((None, None, tr, col), lambda h, i, p: (l, h, i, 0))],
        out_specs=pl.BlockSpec((None, None, tr, col), lambda h, i, p: (p[1], h, i, 0)))
    return pl.pallas_call(
        body, name=f"own_slot_{tag}", grid_spec=grid_spec, out_shape=_sds((N_CHIP, 2, r, col), dtype),
        compiler_params=_params())(place, w4)


def _gather_start(bufs, after, tag):
    n = len(bufs)

    def body(*refs):
        ins = refs[:n]
        s_sem, r_sem, token = refs[n + 1], refs[n + 2], refs[2 * n + 3]
        x, y, c, chips = _place()
        me = 2 * x + y
        for i in range(n):
            mine = ins[i].at[me, c]
            for j, (px, py) in enumerate(chips):
                _remote(mine, mine, s_sem.at[3 * i + j], r_sem.at[3 * i + j], (px, py, c)).start()
        token[...] = jnp.zeros_like(token)

    dma = pltpu.SemaphoreType.DMA
    res = pl.pallas_call(
        body, name=f"gather_start_{tag}", in_specs=[HBM] * n + [ANY],
        out_specs=[SEM, SEM] + [HBM] * n + [pl.BlockSpec(memory_space=pltpu.VMEM)],
        out_shape=[dma((3 * n,)), dma((3 * n,))] + [pltpu.HBM(b.shape, b.dtype) for b in bufs] + [_sds((8, LANES), F32)],
        input_output_aliases={i: i + 2 for i in range(n)},
        compiler_params=pltpu.CompilerParams(has_side_effects=EFFECT),
        )(*[_in_hbm(b) for b in bufs], after)
    return res[0], res[1], list(res[2:2 + n]), res[-1]


def _gather_pass(s_sem, r_sem, bufs, first, after, tag):
    n = len(bufs)

    def body(*refs):
        ins = refs[:n]
        a_s, a_r, b_s, b_r = refs[n], refs[n + 1], refs[n + 3], refs[n + 4]
        x, y, c, chips = _place()
        me = 2 * x + y
        sib = (x, y, 1 - c)
        for i in range(n):
            mine = ins[i].at[me, c]
            for j, (px, py) in enumerate(chips):
                k = 3 * (first + i) + j
                _remote(mine, mine, a_s.at[k], a_r.at[k], (px, py, c)).wait_send()
        for j, (px, py) in enumerate(chips):
            for i in range(n):
                k = 3 * (first + i) + j
                blk = ins[i].at[2 * px + py, c]
                _remote(blk, blk, a_s.at[k], a_r.at[k], (px, py, c)).wait_recv()
                _remote(blk, blk, b_s.at[3 * i + j], b_r.at[3 * i + j], sib).start()

    dma = pltpu.SemaphoreType.DMA
    res = pl.pallas_call(
        body, name=f"gather_pass_{tag}", in_specs=[HBM] * n + [SEM, SEM, ANY],
        out_specs=[SEM, SEM] + [HBM] * n,
        out_shape=[dma((3 * n,)), dma((3 * n,))] + [pltpu.HBM(b.shape, b.dtype) for b in bufs],
        input_output_aliases={i: i + 2 for i in range(n)},
        compiler_params=pltpu.CompilerParams(has_side_effects=EFFECT),
        )(*bufs, s_sem, r_sem, after)
    return res[0], res[1], list(res[2:])


def _gather_land(s_sem, r_sem, bufs, tag):
    n = len(bufs)

    def body(*refs):
        ins = refs[:n]
        b_s, b_r = refs[n], refs[n + 1]
        x, y, c, chips = _place()
        sib = (x, y, 1 - c)
        for j, (px, py) in enumerate(chips):
            for i in range(n):
                sent = ins[i].at[2 * px + py, c]
                got = ins[i].at[2 * px + py, 1 - c]
                _remote(sent, sent, b_s.at[3 * i + j], b_r.at[3 * i + j], sib).wait_send()
                _remote(got, got, b_s.at[3 * i + j], b_r.at[3 * i + j], sib).wait_recv()

    return list(pl.pallas_call(
        body, name=f"gather_land_{tag}", in_specs=[HBM] * n + [SEM, SEM], out_specs=[HBM] * n,
        out_shape=[pltpu.HBM(b.shape, b.dtype) for b in bufs],
        input_output_aliases={i: i for i in range(n)},
        compiler_params=pltpu.CompilerParams(has_side_effects=EFFECT),
        )(*bufs, s_sem, r_sem))


def _rs_pair_send(grads):
    n = len(grads)

    def body(*refs):
        ins, outs = refs[:n], refs[n:2 * n]
        s_sem, r_sem = refs[2 * n:]
        x, y, c, _ = _place()
        sib = (x, y, 1 - c)
        sends = []
        for i in range(n):
            cp = _remote(ins[i].at[:, 1 - c], outs[i], s_sem.at[i], r_sem.at[i], sib)
            cp.start()
            sends.append(cp)
        for cp in sends:
            cp.wait()

    out_shape = [_sds((N_CHIP,) + g.shape[2:], g.dtype) for g in grads]
    dma = pltpu.SemaphoreType.DMA
    return pl.pallas_call(
        body, name=f"rs_pair_send_{n}", in_specs=[ANY] * n, out_specs=[ANY] * n, out_shape=out_shape,
        scratch_shapes=[dma((n,)), dma((n,))],
        )(*grads)


def _rs_pair_add(g, recv, place, tag):
    r, col = g.shape[-2:]
    tr = _tile(r, ROW_TILE)

    def body(place_ref, g_ref, r_ref, wire_ref, own_ref):
        tot = g_ref[...] + r_ref[...]
        wire_ref[...] = tot.astype(wire_ref.dtype)

        @pl.when(pl.program_id(1) == place_ref[1])
        def _():
            own_ref[...] = tot

    grid_spec = pltpu.PrefetchScalarGridSpec(
        num_scalar_prefetch=1, grid=(r // tr, N_CHIP),
        in_specs=[pl.BlockSpec((None, None, tr, col), lambda i, s, p: (s, p[0], i, 0)),
                  pl.BlockSpec((None, tr, col), lambda i, s, p: (s, i, 0))],
        out_specs=[pl.BlockSpec((None, tr, col), lambda i, s, p: (s, i, 0)),
                   pl.BlockSpec((tr, col), lambda i, s, p: (i, 0))])
    return pl.pallas_call(
        body, name=f"rs_pair_add_{tag}", grid_spec=grid_spec,
        out_shape=[_sds((N_CHIP, r, col), WIRE_DTYPE), _sds((r, col), F32)],
        compiler_params=_params())(place, g, recv)


def _pair_plan(srcs, lands):
    x, y, c, _ = _place()
    return [(s.at[:, 1 - c], l, (x, y, 1 - c)) for s, l in zip(srcs, lands)]


def _chip_plan(srcs, lands):
    x, y, c, chips = _place()
    plan = []
    for s, l in zip(srcs, lands):
        if len(s.shape) == 2:
            me = 4 * x + 2 * y + c
            plan += [(s, l.at[me], (x ^ (k >> 2), y ^ ((k >> 1) & 1), c ^ (k & 1))) for k in range(1, N_DEV)]
        else:
            plan += [(s.at[2 * px + py], l.at[j], (px, py, c)) for j, (px, py) in enumerate(chips)]
    return plan


def _exchange_start(srcs, lands, plan, count, tag):
    n = len(srcs)
    both = list(srcs) + list(lands)

    def body(*refs):
        s_sem, r_sem, token = refs[2 * n], refs[2 * n + 1], refs[4 * n + 2]
        for k, (src, dst, dev) in enumerate(plan(refs[:n], refs[n:2 * n])):
            _remote(src, dst, s_sem.at[k], r_sem.at[k], dev).start()
        token[...] = jnp.zeros_like(token)

    dma = pltpu.SemaphoreType.DMA
    res = pl.pallas_call(
        body, name=f"exchange_start_{tag}", in_specs=[HBM] * (2 * n),
        out_specs=[SEM, SEM] + [HBM] * (2 * n) + [pl.BlockSpec(memory_space=pltpu.VMEM)],
        out_shape=[dma((count,)), dma((count,))] + [pltpu.HBM(b.shape, b.dtype) for b in both] + [_sds((8, LANES), F32)],
        input_output_aliases={i: i + 2 for i in range(2 * n)},
        compiler_params=pltpu.CompilerParams(has_side_effects=EFFECT),
        )(*[_in_hbm(b) for b in both])
    return res[0], res[1], list(res[2:2 + n]), list(res[2 + n:2 + 2 * n]), res[-1]


def _exchange_wait(s_sem, r_sem, srcs, lands, plan, after, tag):
    n = len(srcs)

    def body(*refs):
        s_ref, r_ref = refs[2 * n], refs[2 * n + 1]
        for k, (src, dst, dev) in enumerate(plan(refs[:n], refs[n:2 * n])):
            cp = _remote(src, dst, s_ref.at[k], r_ref.at[k], dev)
            cp.wait_send()
            cp.wait_recv()

    both = list(srcs) + list(lands)
    res = pl.pallas_call(
        body, name=f"exchange_wait_{tag}", in_specs=[HBM] * (2 * n) + [SEM, SEM, ANY], out_specs=[HBM] * (2 * n),
        out_shape=[pltpu.HBM(b.shape, b.dtype) for b in both],
        input_output_aliases={i: i for i in range(2 * n)},
        compiler_params=pltpu.CompilerParams(has_side_effects=EFFECT),
        )(*both, s_sem, r_sem, after)
    return list(res[:n]), list(res[n:])


def _rs_chip_add(own, recv, place, l, L, prev, tag):
    r, col = own.shape
    tr = _tile(r, ROW_TILE)

    def body(place_ref, o_ref, r_ref, *rest):
        acc = o_ref[...]
        for j in range(3):
            acc = acc + r_ref[j].astype(F32)
        rest[-1][...] = acc

    in_specs = [pl.BlockSpec((tr, col), lambda i, p: (i, 0)), pl.BlockSpec((3, tr, col), lambda i, p: (0, i, 0))]
    args = [place, own, recv]
    kw = {}
    if prev is not None:
        in_specs.append(ANY)
        args.append(prev)
        kw["input_output_aliases"] = {3: 0}
    grid_spec = pltpu.PrefetchScalarGridSpec(
        num_scalar_prefetch=1, grid=(r // tr,), in_specs=in_specs,
        out_specs=pl.BlockSpec((None, None, tr, col), lambda i, p: (l, p[0], i, 0)))
    return pl.pallas_call(
        body, name=f"rs_chip_add_{tag}", grid_spec=grid_spec, out_shape=_sds((L, 2, r, col), F32),
        compiler_params=_params(), **kw)(*args)


def _rs_pair_share(fulls, tag):
    n = len(fulls)

    def body(*refs):
        outs = refs[n:2 * n]
        s_sem, r_sem = refs[2 * n:]
        x, y, c, _ = _place()
        sib = (x, y, 1 - c)
        started = []
        for i in range(n):
            cp = _remote(outs[i].at[:, c], outs[i].at[:, c], s_sem.at[i], r_sem.at[i], sib)
            cp.start()
            started.append(cp)
        for i, cp in enumerate(started):
            cp.wait_send()
            _remote(outs[i].at[:, 1 - c], outs[i].at[:, 1 - c], s_sem.at[i], r_sem.at[i], sib).wait_recv()

    dma = pltpu.SemaphoreType.DMA
    return pl.pallas_call(
        body, name=f"rs_pair_share_{tag}", in_specs=[ANY] * n, out_specs=[ANY] * n,
        out_shape=[_sds(f.shape, f.dtype) for f in fulls],
        input_output_aliases={i: i for i in range(n)},
        scratch_shapes=[dma((n,)), dma((n,))],
        )(*fulls)


def _ffn_fwd(x, g_pre, g_post, w_in, w_out, tag):
    xn, hgu, act = _norm_mm_in(x, g_pre, w_in, tag, swiglu=True)
    if callable(w_out):
        w_out = w_out(act)
    h, x_out = _mm_out_post(act, w_out.reshape(-1, w_out.shape[-1]), x, g_post, 0.5, tag)
    return x_out, (x, xn, hgu, act, h)


def _ffn_bwd(dx, saved, g_pre, g_post, w_in, w_out, tag, between=None):
    x, xn, hgu, act, h = saved
    dh, dg_post, dhgu = _post_bwd_mm(dx, h, g_post, 0.5, w_out.reshape(-1, w_out.shape[-1]), tag, hgu=hgu)
    if between is not None:
        g_pre = g_pre + between(dhgu)[0, :1]
    dw_out = _mm_tn_out(act, dh, tag)
    dw_in = _mm_tn_in(xn, dhgu, tag)
    dx_in, dg_pre = _mm_nt_pre(dhgu, w_in, dx, x, g_pre, tag)
    return dx_in, dg_pre, dg_post, dw_in, dw_out


def kernel(x, ffn1_pre_g, ffn1_post_g, ffn1_w_in, ffn1_w_out, mix_pre_g, mix_post_g, ffn2_pre_g, ffn2_post_g, ffn2_w_in, ffn2_w_out, conv_w_in, conv_k, conv_w_out, kv_g, kv_w, forget_b, attn_w_qg, attn_w_o, loss_target, m_ffn1_pre_g, m_ffn1_post_g, m_ffn1_w_in, m_ffn1_w_out, m_mix_pre_g, m_mix_post_g, m_ffn2_pre_g, m_ffn2_post_g, m_ffn2_w_in, m_ffn2_w_out, m_conv_w_in, m_conv_k, m_conv_w_out, m_kv_g, m_kv_w, m_forget_b, m_attn_w_qg, m_attn_w_o, v_ffn1_pre_g, v_ffn1_post_g, v_ffn1_w_in, v_ffn1_w_out, v_mix_pre_g, v_mix_post_g, v_ffn2_pre_g, v_ffn2_post_g, v_ffn2_w_in, v_ffn2_w_out, v_conv_w_in, v_conv_k, v_conv_w_out, v_kv_g, v_kv_w, v_forget_b, v_attn_w_qg, v_attn_w_o):
    Bl, S, D = x.shape
    T = Bl * S
    H = forget_b.shape[0]
    assert D == H * HEAD_DIM and D % LANES == 0
    kvc = kv_w.shape[1]
    kvp = -(-kvc // LANES) * LANES
    kv_all = 2 * D + LANES
    dk_cols = conv_k.shape[2]
    chip = 2 * lax.axis_index("x") + lax.axis_index("y")
    core = lax.axis_index("c")

    given = dict(ffn1_w_in=(ffn1_w_in, m_ffn1_w_in, v_ffn1_w_in), ffn1_w_out=(ffn1_w_out, m_ffn1_w_out, v_ffn1_w_out),
                 ffn2_w_in=(ffn2_w_in, m_ffn2_w_in, v_ffn2_w_in), ffn2_w_out=(ffn2_w_out, m_ffn2_w_out, v_ffn2_w_out),
                 conv_w_in=(conv_w_in, m_conv_w_in, v_conv_w_in), conv_w_out=(conv_w_out, m_conv_w_out, v_conv_w_out),
                 kv_w=(kv_w, m_kv_w, v_kv_w), attn_w_qg=(attn_w_qg, m_attn_w_qg, v_attn_w_qg),
                 attn_w_o=(attn_w_o, m_attn_w_o, v_attn_w_o))
    shards = {k: w for k, (w, _, _) in given.items()}
    shards["kv_w"] = jnp.pad(kv_w, ((0, 0), (0, kvp - kvc)))[None]
    groups = [[("ffn1_w_in", 0), ("ffn1_w_out", 0)], [("conv_w_in", 0), ("conv_w_out", 0)],
              [("ffn2_w_in", 0), ("ffn2_w_out", 0)], [("kv_w", 0), ("ffn1_w_in", 1), ("ffn1_w_out", 1)],
              [("attn_w_qg", 0), ("attn_w_o", 0), ("ffn2_w_in", 1), ("ffn2_w_out", 1)]]
    first = groups[0] + groups[1] + groups[2]
    second = groups[3] + groups[4]
    place = jnp.stack([core, chip]).astype(jnp.int32)

    def slot(key, where):
        w = shards[key[0]]
        L, r, col = w.shape
        return _own_slot(w.reshape(L, 2, r // 2, col), key[1], MM_DTYPE, where, f"{key[0]}{key[1]}")

    def whole(g):
        return g.reshape(N_CHIP, -1, g.shape[-1])

    taps_slot = _own_slot(jnp.pad(conv_k[0], ((0, 13), (0, 0))).reshape(1, 2, 8, dk_cols), 0, F32, place, "conv_k")
    fb = jnp.pad(forget_b, (0, LANES - H)).reshape(1, LANES)
    w_in0, w_out0 = groups[0]
    s_0, r_0, fly_0, token = _gather_start([slot(w_in0, place), taps_slot, slot(w_out0, place)], fb, "first")
    later = groups[1] + groups[2] + groups[3] + groups[4]
    s_1, r_1, fly_1, token = _gather_start([slot(key, place) for key in later], token, "rest")
    W = {}

    def land(sems, bufs, lo, after, tag):
        return _gather_land(*_gather_pass(*sems, bufs, lo, after, tag), tag)

    def arrive(g, after):
        lo = sum(len(groups[k]) for k in range(1, g))
        got = land((s_1, r_1), fly_1[lo:lo + len(groups[g])], lo, after, f"g{g}")
        W.update({key: whole(b) for key, b in zip(groups[g], got)})

    w_first, taps = land((s_0, r_0), fly_0[:2], 0, token, "g0")
    k_taps = taps.reshape(N_CHIP, 16, dk_cols).transpose(1, 0, 2).reshape(16, D)[:8]

    x0 = x.reshape(T, D)
    W[w_in0] = whole(w_first)

    def first_w_out(act):
        W[w_out0] = whole(land((s_0, r_0), fly_0[2:], 2, act, "g0_out")[0])
        return W[w_out0]

    x1, s_f1a = _ffn_fwd(x0, ffn1_pre_g[0], ffn1_post_g[0], W[w_in0], first_w_out, "l0f1")
    arrive(1, x1)
    w_o_conv = W["conv_w_out", 0].reshape(D, D)
    xn_c, bch = _norm_mm_in(x1, mix_pre_g[0], W["conv_w_in", 0], "conv")
    z_c = _conv_fwd(bch, k_taps, Bl, S)
    m_c, x2 = _mm_out_post(z_c, w_o_conv, x1, mix_post_g[0], 1.0, "conv_out")
    arrive(2, x2)
    x3, s_f2a = _ffn_fwd(x2, ffn2_pre_g[0], ffn2_post_g[0], W["ffn2_w_in", 0], W["ffn2_w_out", 0], "l0f2")

    arrive(3, x3)
    kv_full = jnp.concatenate([W["kv_w", 0][s, :, :kvc] for s in range(N_CHIP)], axis=1)
    kv_full = jnp.pad(kv_full, ((0, 0), (0, kv_all - kv_full.shape[1])))
    xn_kv = _rms_fwd(x3, kv_g, "kv")
    kvact = _mm_nn(xn_kv, kv_full[:, :2 * D], MM_DTYPE, "kv")
    pf = _mm_nn(xn_kv, kv_full[:, 2 * D:], F32, "forget")
    cum = _forget_fwd(pf, fb, Bl, S)
    bq = min(S, ATT_BLOCK)
    c3 = cum.reshape(Bl, S, LANES)[:, :, :H].transpose(0, 2, 1)
    c_col = jnp.broadcast_to(c3[..., None], (Bl, H, S, LANES))
    c_row = c3.reshape(Bl, H, S // bq, 1, bq)

    x4, s_f1b = _ffn_fwd(x3, ffn1_pre_g[1], ffn1_post_g[1], W["ffn1_w_in", 1], W["ffn1_w_out", 1], "l1f1")
    arrive(4, x4)
    w_o_attn = W["attn_w_o", 0].reshape(D, D)
    xn_a, qg = _norm_mm_in(x4, mix_pre_g[1], W["attn_w_qg", 0], "qg")
    o, lse = _attn_fwd(qg, kvact, c_col, c_row, Bl, S, D)
    z_a = _gate_fwd(qg, o)
    m_a, x5 = _mm_out_post(z_a, w_o_attn, x4, mix_post_g[1], 1.0, "attn_out")
    x6, s_f2b = _ffn_fwd(x5, ffn2_pre_g[1], ffn2_post_g[1], W["ffn2_w_in", 1], W["ffn2_w_out", 1], "l1f2")

    dy, loss_local = _loss_grad(x6, loss_target.reshape(T, D))
    loss = lax.psum(loss_local, ("x", "y", "c"))

    G = {}
    dx5, dg_f2pre_1, dg_f2post_1, G["ffn2_w_in", 1], G["ffn2_w_out", 1] = _ffn_bwd(
        dy, s_f2b, ffn2_pre_g[1], ffn2_post_g[1], W["ffn2_w_in", 1], W["ffn2_w_out", 1], "l1f2")
    dm_a, dg_mixpost_1, dz_a = _post_bwd_mm(dx5, m_a, mix_post_g[1], 1.0, w_o_attn, "attn_out")
    G["attn_w_o", 0] = _mm_tn_out(z_a, dm_a, "attn_out")
    do = _gate_do(dz_a, qg)
    dq, dk, dv, dcr = _attn_bwd(qg, kvact, do, lse, c_col, c_row, Bl, S, D)
    dqg = _gate_bwd(dz_a, qg, o, dq)
    G["attn_w_qg", 0] = _mm_tn_in(xn_a, dqg, "qg")
    dx4, dg_mixpre_1 = _mm_nt_pre(dqg, W["attn_w_qg", 0], dx5, x4, mix_pre_g[1], "qg")
    dx3, dg_f1pre_1, dg_f1post_1, G["ffn1_w_in", 1], G["ffn1_w_out", 1] = _ffn_bwd(
        dx4, s_f1b, ffn1_pre_g[1], ffn1_post_g[1], W["ffn1_w_in", 1], W["ffn1_w_out", 1], "l1f1")

    dcum = jnp.pad(dcr.reshape(Bl, H, S).transpose(0, 2, 1), ((0, 0), (0, 0), (0, LANES - H))).reshape(T, LANES)
    dpf, dfb = _forget_bwd(dcum, pf, fb, Bl, S)
    dp = jnp.concatenate([dk.astype(MM_DTYPE), dv.astype(MM_DTYPE), dpf], axis=1)
    G_kv_full = _mm_tn(xn_kv, dp, "kv")
    G["kv_w", 0] = jnp.stack([jnp.pad(G_kv_full[:, s * kvc:(s + 1) * kvc], ((0, 0), (0, kvp - kvc))) for s in range(N_CHIP)])
    dx3, dg_kv = _mm_nt_pre(dp, kv_full, dx3, x3, kv_g, "kv")

    def halves_of(keys):
        return [G[k].reshape(N_CHIP, 2, G[k].shape[1] // 2, G[k].shape[2]) for k in keys]

    def pair_adds(keys, grads, recvs):
        wires, owns = [], []
        for k, g, r in zip(keys, grads, recvs):
            w, own = _rs_pair_add(g, r, place, f"{k[0]}{k[1]}")
            wires.append(w)
            owns.append(own)
        return wires, owns

    def chip_start(wires, tag, extra=()):
        lands = [lax.empty((3,) + w.shape[1:], w.dtype) for w in wires]
        lands += [jnp.zeros((N_DEV,) + e.shape, e.dtype) for e in extra]
        return _exchange_start(list(wires) + list(extra), lands, _chip_plan, 3 * len(wires) + (N_DEV - 1) * len(extra), tag)

    late = groups[2] + groups[1]
    last = groups[0]
    grads_2 = halves_of(second)
    p_sems, p_semr, grads_2, sib_2, token = _exchange_start(
        grads_2, [lax.empty((N_CHIP,) + g.shape[2:], g.dtype) for g in grads_2], _pair_plan, len(grads_2), "pair_second")

    dx2, dg_f2pre_0, dg_f2post_0, G["ffn2_w_in", 0], G["ffn2_w_out", 0] = _ffn_bwd(
        dx3, s_f2a, ffn2_pre_g[0], ffn2_post_g[0] + token[0, :1], W["ffn2_w_in", 0], W["ffn2_w_out", 0], "l0f2")
    grads_2, sib_2 = _exchange_wait(p_sems, p_semr, grads_2, sib_2, _pair_plan, dx2, "pair_second")
    wires_2, owns_2 = pair_adds(second, grads_2, sib_2)
    c_2 = chip_start(wires_2, "chip_second")
    dm_c, dg_mixpost_0, dz_c = _post_bwd_mm(dx2, m_c, mix_post_g[0] + c_2[4][0, :1], 1.0, w_o_conv, "conv_out")
    G["conv_w_out", 0] = _mm_tn_out(z_c, dm_c, "conv_out")
    db, dcg, dhh, dk_taps = _conv_bwd(bch, dz_c, k_taps, Bl, S)
    dbch = jnp.concatenate([db, dcg, dhh], axis=1)
    G["conv_w_in", 0] = _mm_tn_in(xn_c, dbch, "conv")
    dx1, dg_mixpre_0 = _mm_nt_pre(dbch, W["conv_w_in", 0], dx2, x1, mix_pre_g[0], "conv")
    def pair_start(keys, tag):
        grads = halves_of(keys)
        lands = [lax.empty((N_CHIP,) + g.shape[2:], g.dtype) for g in grads]
        return _exchange_start(grads, lands, _pair_plan, len(grads), tag)

    p_l = pair_start(late, "pair_late")
    late_done = {}

    def late_leg(dhgu):
        grads_l, sib_l = _exchange_wait(*p_l[:4], _pair_plan, dhgu, "pair_late")
        late_done["wires"], late_done["owns"] = pair_adds(late, grads_l, sib_l)
        late_done["chip"] = chip_start(late_done["wires"], "chip_late")
        return late_done["chip"][4]

    dx0, dg_f1pre_0, dg_f1post_0, G["ffn1_w_in", 0], G["ffn1_w_out", 0] = _ffn_bwd(
        dx1, s_f1a, ffn1_pre_g[0], ffn1_post_g[0] + p_l[4][0, :1], W["ffn1_w_in", 0], W["ffn1_w_out", 0], "l0f1",
        between=late_leg)
    grad_x = dx0.reshape(Bl, S, D)
    owns_l, c_l = late_done["owns"], late_done["chip"]

    p_1 = pair_start(last, "pair_last")
    _, recvs_2 = _exchange_wait(*c_2[:4], _chip_plan, p_1[4], "chip_second")
    _, recvs_l = _exchange_wait(*c_l[:4], _chip_plan, p_1[4], "chip_late")
    partial = {}

    def chip_adds(keys, owns, recvs):
        for (name, l), own, rcv in zip(keys, owns, recvs):
            partial[name] = _rs_chip_add(own, rcv, place, l, shards[name].shape[0], partial.get(name), f"{name}{l}")
        return partial[name]

    added = chip_adds(late + second, owns_l + owns_2, recvs_l + recvs_2)
    grads_1, sib_1 = _exchange_wait(*p_1[:4], _pair_plan, added, "pair_last")
    wires_1, owns_1 = pair_adds(last, grads_1, sib_1)

    def row(v):
        return jnp.pad(v.reshape(-1), (0, D - v.size)).reshape(1, D)

    small_parts = [dg_f1pre_0, dg_f1pre_1, dg_f1post_0, dg_f1post_1, dg_mixpre_0, dg_mixpre_1, dg_mixpost_0, dg_mixpost_1,
                   dg_f2pre_0, dg_f2pre_1, dg_f2post_0, dg_f2post_1, dg_kv, row(dfb[0, :H]), dk_taps[:3]]
    small = jnp.concatenate(small_parts, axis=0)
    small = jnp.pad(small, ((0, SMALL_ROWS - small.shape[0]), (0, 0)))
    c_1 = chip_start(wires_1, "chip_last", extra=[small])
    res = {}

    def adamw(names, reduced, after):
        for k, red in zip(names, reduced):
            w, m, v = given[k]
            g2 = red.reshape(-1, red.shape[-1])
            if k == "kv_w":
                g2 = g2[:, :kvc]
            flat = lambda a: a.reshape(-1, a.shape[-1])
            go, d, mn, vn = _adamw(flat(w), g2, flat(m), flat(v), k, after=after)
            res[k] = tuple(a.reshape(w.shape) for a in (go, d, mn, vn))
        return d

    early = [k for k in partial if (k, 0) not in last]
    done = adamw(early, _rs_pair_share([partial[k] for k in early], "early"), c_1[4])
    _, recvs_1 = _exchange_wait(*c_1[:4], _chip_plan, done, "chip_last")
    chip_adds(last, owns_1, recvs_1[:-1])
    rest = [k for k, _ in last]
    adamw(rest, _rs_pair_share([partial[k] for k in rest], "last"), None)
    gsum = _sum_devices(recvs_1[-1], small, place)

    small_names = ["ffn1_pre_g", "ffn1_post_g", "mix_pre_g", "mix_post_g", "ffn2_pre_g", "ffn2_post_g"]
    small_given = dict(ffn1_pre_g=(ffn1_pre_g, m_ffn1_pre_g, v_ffn1_pre_g), ffn1_post_g=(ffn1_post_g, m_ffn1_post_g, v_ffn1_post_g),
                       mix_pre_g=(mix_pre_g, m_mix_pre_g, v_mix_pre_g), mix_post_g=(mix_post_g, m_mix_post_g, v_mix_post_g),
                       ffn2_pre_g=(ffn2_pre_g, m_ffn2_pre_g, v_ffn2_pre_g), ffn2_post_g=(ffn2_post_g, m_ffn2_post_g, v_ffn2_post_g))

    def pack(idx):
        rows_ = [small_given[k][idx] for k in small_names]
        rows_ += [row((kv_g, m_kv_g, v_kv_g)[idx]), row((forget_b, m_forget_b, v_forget_b)[idx])]
        rows_.append(jnp.pad((conv_k, m_conv_k, v_conv_k)[idx][0], ((0, 0), (0, D - dk_cols))))
        a = jnp.concatenate(rows_, axis=0)
        return jnp.pad(a, ((0, SMALL_ROWS - a.shape[0]), (0, 0)))

    g_taps = lax.dynamic_slice_in_dim(gsum[14:17], chip * dk_cols, dk_cols, axis=1)
    g_small = jnp.concatenate([gsum[:14], jnp.pad(g_taps, ((0, 0), (0, D - dk_cols))), gsum[17:]], axis=0)
    g_small, d_s, m_s, v_s = _adamw(pack(0), g_small, pack(1), pack(2), "small")
    for i, k in enumerate(small_names):
        res[k] = tuple(a[2 * i:2 * i + 2] for a in (g_small, d_s, m_s, v_s))
    res["kv_g"] = tuple(a[12] for a in (g_small, d_s, m_s, v_s))
    res["forget_b"] = tuple(a[13, :H] for a in (g_small, d_s, m_s, v_s))
    res["conv_k"] = tuple(a[14:17, :dk_cols][None] for a in (g_small, d_s, m_s, v_s))

    order = ["ffn1_pre_g", "ffn1_post_g", "ffn1_w_in", "ffn1_w_out", "mix_pre_g", "mix_post_g", "ffn2_pre_g", "ffn2_post_g",
             "ffn2_w_in", "ffn2_w_out", "conv_w_in", "conv_k", "conv_w_out", "kv_g", "kv_w", "forget_b", "attn_w_qg", "attn_w_o"]
    out = [loss, grad_x]
    for idx in range(4):
        out += [res[k][idx] for k in order]
    return tuple(out)
```

```python
import functools
import math

import jax
import jax.numpy as jnp
from jax import lax
from jax.experimental import pallas as pl
from jax.experimental.pallas import tpu as pltpu

F32 = jnp.float32
MM_DTYPE = jnp.bfloat16
WIRE_DTYPE = jnp.bfloat16

RMS_EPS = 1e-6
ADAM_LR = 0.001
ADAM_B1 = 0.9
ADAM_B2 = 0.999
ADAM_EPS = 1e-08
ADAM_WD = 0.01
ADAM_STEP = 10

HEAD_DIM = 64
LANES = 128
N_CHIP = 4
N_DEV = 8
ROW_TILE = 256
MM_TILE = 512
FUSED_TILE = 512
TN_TILE = 2048
ATT_BLOCK = 512
SMALL_ROWS = 24
VMEM_LIMIT = 56 * 1024 * 1024
MESH = pl.DeviceIdType.MESH
ANY = pl.BlockSpec(memory_space=pl.ANY)

NT = (((1,), (1,)), ((), ()))
TN = (((0,), (0,)), ((), ()))


def _tile(n, pref):
    if n <= pref:
        return n
    t = pref - pref % 16
    while n % t:
        t -= 16
    return t


def _params():
    return pltpu.CompilerParams(vmem_limit_bytes=VMEM_LIMIT)


def _sds(shape, dtype):
    return jax.ShapeDtypeStruct(shape, dtype)


def _rows(tm, c):
    return pl.BlockSpec((tm, c), lambda i: (i, 0))


def _whole(shape):
    return pl.BlockSpec(shape, lambda *_: (0,) * len(shape))


def _resident(shape):
    return pl.BlockSpec(shape, lambda *_: (0,) * len(shape), pipeline_mode=pl.Buffered(1))


def _rms_fwd(x, g, tag):
    T, D = x.shape
    tm = _tile(T, ROW_TILE)

    def body(x_ref, g_ref, o_ref):
        xv = x_ref[...]
        r = lax.rsqrt(jnp.mean(xv * xv, axis=-1, keepdims=True) + RMS_EPS)
        o_ref[...] = (xv * r * g_ref[...]).astype(o_ref.dtype)

    return pl.pallas_call(
        body, name=f"rms_fwd_{tag}", grid=(T // tm,),
        in_specs=[_rows(tm, D), _whole((1, D))], out_specs=_rows(tm, D),
        out_shape=_sds((T, D), MM_DTYPE), compiler_params=_params())(x, g.reshape(1, D))


def _post_fwd(x, h, g, alpha, tag):
    T, D = x.shape
    tm = _tile(T, ROW_TILE)

    def body(x_ref, h_ref, g_ref, o_ref):
        hv = h_ref[...]
        r = lax.rsqrt(jnp.mean(hv * hv, axis=-1, keepdims=True) + RMS_EPS)
        o_ref[...] = x_ref[...] + alpha * (hv * r * g_ref[...])

    return pl.pallas_call(
        body, name=f"post_fwd_{tag}", grid=(T // tm,),
        in_specs=[_rows(tm, D), _rows(tm, D), _whole((1, D))], out_specs=_rows(tm, D),
        out_shape=_sds((T, D), F32), compiler_params=_params())(x, h, g.reshape(1, D))


def _accumulate(ref, part, first):
    @pl.when(first)
    def _():
        ref[...] = part

    @pl.when(jnp.logical_not(first))
    def _():
        ref[...] += part


def _post_bwd(dx, h, g, alpha, tag):
    T, D = dx.shape
    tm = _tile(T, ROW_TILE)

    def body(dx_ref, h_ref, g_ref, dh_ref, dg_ref):
        hv = h_ref[...]
        r = lax.rsqrt(jnp.mean(hv * hv, axis=-1, keepdims=True) + RMS_EPS)
        hh = hv * r
        dyn = alpha * dx_ref[...]
        _accumulate(dg_ref, jnp.sum(dyn * hh, axis=0, keepdims=True), pl.program_id(0) == 0)
        dhh = dyn * g_ref[...]
        dh = r * (dhh - hh * jnp.mean(dhh * hh, axis=-1, keepdims=True))
        dh_ref[...] = dh.astype(dh_ref.dtype)

    return pl.pallas_call(
        body, name=f"post_bwd_{tag}", grid=(T // tm,),
        in_specs=[_rows(tm, D), _rows(tm, D), _whole((1, D))],
        out_specs=[_rows(tm, D), _whole((1, D))],
        out_shape=[_sds((T, D), MM_DTYPE), _sds((1, D), F32)],
        compiler_params=_params())(dx, h, g.reshape(1, D))


def _pre_bwd(dres, dxn, x, g, tag):
    T, D = x.shape
    tm = _tile(T, ROW_TILE)

    def body(dres_ref, dxn_ref, x_ref, g_ref, dx_ref, dg_ref):
        xv = x_ref[...]
        r = lax.rsqrt(jnp.mean(xv * xv, axis=-1, keepdims=True) + RMS_EPS)
        xh = xv * r
        dn = dxn_ref[...]
        _accumulate(dg_ref, jnp.sum(dn * xh, axis=0, keepdims=True), pl.program_id(0) == 0)
        dxh = dn * g_ref[...]
        dx_ref[...] = dres_ref[...] + r * (dxh - xh * jnp.mean(dxh * xh, axis=-1, keepdims=True))

    return pl.pallas_call(
        body, name=f"pre_bwd_{tag}", grid=(T // tm,),
        in_specs=[_rows(tm, D), _rows(tm, D), _rows(tm, D), _whole((1, D))],
        out_specs=[_rows(tm, D), _whole((1, D))],
        out_shape=[_sds((T, D), F32), _sds((1, D), F32)],
        compiler_params=_params())(dres, dxn, x, g.reshape(1, D))


def _swiglu_fwd(hgu, tag):
    T, F2 = hgu.shape
    F = F2 // 2
    tm = _tile(T, ROW_TILE)

    def body(g_ref, u_ref, o_ref):
        g = g_ref[...].astype(F32)
        o_ref[...] = (g * jax.nn.sigmoid(g) * u_ref[...].astype(F32)).astype(o_ref.dtype)

    return pl.pallas_call(
        body, name=f"swiglu_fwd_{tag}", grid=(T // tm,),
        in_specs=[pl.BlockSpec((tm, F), lambda i: (i, 0)), pl.BlockSpec((tm, F), lambda i: (i, 1))],
        out_specs=_rows(tm, F), out_shape=_sds((T, F), MM_DTYPE), compiler_params=_params())(hgu, hgu)


def _swiglu_bwd(hgu, da, tag):
    T, F2 = hgu.shape
    F = F2 // 2
    tm = _tile(T, ROW_TILE)

    def body(h_ref, da_ref, o_ref):
        g = h_ref[:, :F].astype(F32)
        u = h_ref[:, F:].astype(F32)
        d = da_ref[...].astype(F32)
        sg = jax.nn.sigmoid(g)
        o_ref[:, :F] = (d * u * sg * (1.0 + g * (1.0 - sg))).astype(o_ref.dtype)
        o_ref[:, F:] = (d * g * sg).astype(o_ref.dtype)

    return pl.pallas_call(
        body, name=f"swiglu_bwd_{tag}", grid=(T // tm,),
        in_specs=[_rows(tm, F2), _rows(tm, F)], out_specs=_rows(tm, F2),
        out_shape=_sds((T, F2), MM_DTYPE), compiler_params=_params())(hgu, da)


def _loss_grad(y, tgt):
    T, D = y.shape
    tm = _tile(T, ROW_TILE)

    def body(y_ref, t_ref, dy_ref, l_ref):
        e = y_ref[...] - t_ref[...]
        row = jnp.mean(e * e, axis=-1, keepdims=True)
        part = jnp.broadcast_to(jnp.sum(row, axis=0, keepdims=True), (8, LANES))
        _accumulate(l_ref, part, pl.program_id(0) == 0)
        dy_ref[...] = e * (1.0 / D)

    dy, lsum = pl.pallas_call(
        body, name="loss_grad", grid=(T // tm,),
        in_specs=[_rows(tm, D), _rows(tm, D)], out_specs=[_rows(tm, D), _whole((8, LANES))],
        out_shape=[_sds((T, D), F32), _sds((8, LANES), F32)], compiler_params=_params())(y, tgt)
    return dy, 0.5 * lsum[0, 0]


def _shift_down(u, d, rows):
    return jnp.where(rows >= d, pltpu.roll(u, d, 0), 0.0)


def _shift_up(u, d, rows, S):
    return jnp.where(rows < S - d, pltpu.roll(u, S - d, 0), 0.0)


def _conv_fwd(bch, k8, Bl, S):
    T, D3 = bch.shape
    D = D3 // 3
    dc = min(D, 2 * LANES)
    nd = D // dc

    def body(b_ref, c_ref, h_ref, k_ref, z_ref):
        rows = lax.broadcasted_iota(jnp.int32, (S, 1), 0)
        u = c_ref[...].astype(F32) * h_ref[...].astype(F32)
        y = k_ref[2:3, :] * u + k_ref[1:2, :] * _shift_down(u, 1, rows) + k_ref[0:1, :] * _shift_down(u, 2, rows)
        z_ref[...] = (b_ref[...].astype(F32) * y).astype(z_ref.dtype)

    return pl.pallas_call(
        body, name="conv_fwd", grid=(Bl, nd),
        in_specs=[pl.BlockSpec((S, dc), lambda b, j: (b, j)),
                  pl.BlockSpec((S, dc), lambda b, j: (b, nd + j)),
                  pl.BlockSpec((S, dc), lambda b, j: (b, 2 * nd + j)),
                  pl.BlockSpec((8, dc), lambda b, j: (0, j))],
        out_specs=pl.BlockSpec((S, dc), lambda b, j: (b, j)),
        out_shape=_sds((T, D), MM_DTYPE), compiler_params=_params())(bch, bch, bch, k8)


def _conv_bwd(bch, dz, k8, Bl, S):
    T, D3 = bch.shape
    D = D3 // 3
    dc = min(D, 2 * LANES)
    nd = D // dc

    def body(b_ref, c_ref, h_ref, dz_ref, k_ref, db_ref, dc_ref, dh_ref, dk_ref):
        rows = lax.broadcasted_iota(jnp.int32, (S, 1), 0)
        bv = b_ref[...].astype(F32)
        cv = c_ref[...].astype(F32)
        hv = h_ref[...].astype(F32)
        dzv = dz_ref[...].astype(F32)
        u = cv * hv
        u1 = _shift_down(u, 1, rows)
        u2 = _shift_down(u, 2, rows)
        y = k_ref[2:3, :] * u + k_ref[1:2, :] * u1 + k_ref[0:1, :] * u2
        db_ref[...] = (dzv * y).astype(db_ref.dtype)
        dy = dzv * bv
        du = k_ref[2:3, :] * dy + k_ref[1:2, :] * _shift_up(dy, 1, rows, S) + k_ref[0:1, :] * _shift_up(dy, 2, rows, S)
        dc_ref[...] = (du * hv).astype(dc_ref.dtype)
        dh_ref[...] = (du * cv).astype(dh_ref.dtype)

        @pl.when(pl.program_id(1) == 0)
        def _():
            dk_ref[...] = jnp.zeros_like(dk_ref)

        dk_ref[0:1, :] += jnp.sum(dy * u2, axis=0, keepdims=True)
        dk_ref[1:2, :] += jnp.sum(dy * u1, axis=0, keepdims=True)
        dk_ref[2:3, :] += jnp.sum(dy * u, axis=0, keepdims=True)

    seq = lambda off: pl.BlockSpec((S, dc), lambda j, b: (b, off + j))
    return pl.pallas_call(
        body, name="conv_bwd", grid=(nd, Bl),
        in_specs=[seq(0), seq(nd), seq(2 * nd), seq(0), pl.BlockSpec((8, dc), lambda j, b: (0, j))],
        out_specs=[seq(0), seq(0), seq(0), pl.BlockSpec((8, dc), lambda j, b: (0, j))],
        out_shape=[_sds((T, D), MM_DTYPE)] * 3 + [_sds((8, D), F32)],
        compiler_params=_params())(bch, bch, bch, dz, k8)


def _forget_fwd(pf, fb, Bl, S):
    T = pf.shape[0]

    def body(p_ref, fb_ref, c_ref):
        rows = lax.broadcasted_iota(jnp.int32, (S, 1), 0)
        z = p_ref[...] + fb_ref[...]
        acc = jnp.minimum(z, 0.0) - jnp.log1p(jnp.exp(-jnp.abs(z)))
        d = 1
        while d < S:
            acc = acc + _shift_down(acc, d, rows)
            d *= 2
        c_ref[...] = acc

    return pl.pallas_call(
        body, name="forget_fwd", grid=(Bl,),
        in_specs=[_rows(S, LANES), _whole((1, LANES))], out_specs=_rows(S, LANES),
        out_shape=_sds((T, LANES), F32), compiler_params=_params())(pf, fb)


def _forget_bwd(dc, pf, fb, Bl, S):
    T = pf.shape[0]

    def body(dc_ref, p_ref, fb_ref, df_ref, dfb_ref):
        rows = lax.broadcasted_iota(jnp.int32, (S, 1), 0)
        acc = dc_ref[...]
        d = 1
        while d < S:
            acc = acc + _shift_up(acc, d, rows, S)
            d *= 2
        df = acc * jax.nn.sigmoid(-(p_ref[...] + fb_ref[...]))
        df_ref[...] = df.astype(df_ref.dtype)
        _accumulate(dfb_ref, jnp.sum(df, axis=0, keepdims=True), pl.program_id(0) == 0)

    return pl.pallas_call(
        body, name="forget_bwd", grid=(Bl,),
        in_specs=[_rows(S, LANES), _rows(S, LANES), _whole((1, LANES))],
        out_specs=[_rows(S, LANES), _whole((1, LANES))],
        out_shape=[_sds((T, LANES), MM_DTYPE), _sds((1, LANES), F32)],
        compiler_params=_params())(dc, pf, fb)


def _head_mask(h):
    lane = lax.broadcasted_iota(jnp.int32, (1, LANES), 1)
    return (lane >= h * HEAD_DIM) & (lane < (h + 1) * HEAD_DIM)


def _attn_fwd(qg, kv, c_col, c_row, Bl, S, D):
    T = Bl * S
    H = D // HEAD_DIM
    HP = D // LANES
    bq = min(S, ATT_BLOCK)
    nq = S // bq
    scale = 1.0 / math.sqrt(HEAD_DIM)

    def body(q_ref, g_ref, k_ref, v_ref, cc_ref, cr_ref, o_ref, lse_ref, z_ref):
        i = pl.program_id(2)
        q2 = q_ref[...]
        qh = [q2 * (_head_mask(h).astype(F32) * scale).astype(q2.dtype) for h in range(2)]
        cc = [cc_ref[h][:, :1] for h in range(2)]
        diag = lax.broadcasted_iota(jnp.int32, (1, bq), 1) <= lax.broadcasted_iota(jnp.int32, (bq, 1), 0)

        def block(j, carry, on_diagonal):
            off = pl.multiple_of(j * bq, bq)
            kj = k_ref[pl.ds(off, bq), :]
            vj = v_ref[pl.ds(off, bq), :]
            new = []
            for h in range(2):
                m, l, acc = carry[h]
                s = lax.dot_general(qh[h], kj, NT, preferred_element_type=F32) + cc[h] - cr_ref[h, j]
                if on_diagonal:
                    s = jnp.where(diag, s, -jnp.inf)
                m_new = jnp.maximum(m, jnp.max(s, axis=1, keepdims=True))
                p = jnp.exp(s - m_new)
                a = jnp.exp(m - m_new)
                l = a * l + jnp.sum(p, axis=1, keepdims=True)
                acc = a * acc + jnp.dot(p.astype(MM_DTYPE), vj, preferred_element_type=F32)
                new.append((m_new, l, acc))
            return tuple(new)

        one = (jnp.full((bq, 1), -jnp.inf, F32), jnp.zeros((bq, 1), F32), jnp.zeros((bq, LANES), F32))
        carry = lax.fori_loop(0, i, lambda j, c: block(j, c, False), (one, one))
        carry = block(i, carry, True)
        outs = []
        for h in range(2):
            m, l, acc = carry[h]
            outs.append(acc / l)
            lse_ref[h] = jnp.broadcast_to(m + jnp.log(l), (bq, LANES))
        o2 = jnp.where(_head_mask(0), outs[0], outs[1])
        o_ref[...] = o2
        z_ref[...] = (jax.nn.sigmoid(g_ref[...].astype(F32)) * o2).astype(z_ref.dtype)

    return pl.pallas_call(
        body, name="attn_fwd", grid=(Bl, HP, nq),
        in_specs=[pl.BlockSpec((bq, LANES), lambda b, hp, i: (b * nq + i, hp)),
                  pl.BlockSpec((bq, LANES), lambda b, hp, i: (b * nq + i, HP + hp)),
                  pl.BlockSpec((S, LANES), lambda b, hp, i: (b, hp)),
                  pl.BlockSpec((S, LANES), lambda b, hp, i: (b, HP + hp)),
                  pl.BlockSpec((None, 2, bq, LANES), lambda b, hp, i: (b, hp, i, 0)),
                  pl.BlockSpec((None, 2, nq, 1, bq), lambda b, hp, i: (b, hp, 0, 0, 0))],
        out_specs=[pl.BlockSpec((bq, LANES), lambda b, hp, i: (b * nq + i, hp)),
                   pl.BlockSpec((None, 2, bq, LANES), lambda b, hp, i: (b, hp, i, 0)),
                   pl.BlockSpec((bq, LANES), lambda b, hp, i: (b * nq + i, hp))],
        out_shape=[_sds((T, D), F32), _sds((Bl, H, S, LANES), F32), _sds((T, D), MM_DTYPE)],
        compiler_params=_params())(qg, qg, kv, kv, c_col, c_row)


def _attn_bwd(qg, kv, dz, lse, c_col, c_row, Bl, S, D):
    T = Bl * S
    H = D // HEAD_DIM
    HP = D // LANES
    bq = min(S, ATT_BLOCK)
    nq = S // bq
    scale = 1.0 / math.sqrt(HEAD_DIM)

    def body(q_ref, g_ref, k_ref, v_ref, dz_ref, lse_ref, cc_ref, cr_ref, dq_ref, dk_ref, dv_ref, dcr_ref, p_sc, dp_sc):
        i = pl.program_id(2)

        @pl.when(i == 0)
        def _():
            dk_ref[...] = jnp.zeros_like(dk_ref)
            dv_ref[...] = jnp.zeros_like(dv_ref)
            dcr_ref[...] = jnp.zeros_like(dcr_ref)

        q2 = q_ref[...]
        do2 = (dz_ref[...].astype(F32) * jax.nn.sigmoid(g_ref[...].astype(F32))).astype(MM_DTYPE)
        masks = [_head_mask(h).astype(F32) for h in range(2)]
        qh = [q2 * (masks[h] * scale).astype(q2.dtype) for h in range(2)]
        doh = [do2 * masks[h].astype(do2.dtype) for h in range(2)]
        cc = [cc_ref[h][:, :1] for h in range(2)]
        lse = [lse_ref[h][:, :1] for h in range(2)]
        diag = lax.broadcasted_iota(jnp.int32, (1, bq), 1) <= lax.broadcasted_iota(jnp.int32, (bq, 1), 0)

        def sweep1(j, delta, on_diagonal):
            off = pl.multiple_of(j * bq, bq)
            kj = k_ref[pl.ds(off, bq), :]
            vj = v_ref[pl.ds(off, bq), :]
            new = []
            dv = None
            for h in range(2):
                s = lax.dot_general(qh[h], kj, NT, preferred_element_type=F32) + cc[h] - cr_ref[h, j]
                if on_diagonal:
                    s = jnp.where(diag, s, -jnp.inf)
                p = jnp.exp(s - lse[h])
                dp = lax.dot_general(doh[h], vj, NT, preferred_element_type=F32)
                p_sc[h, j] = p
                dp_sc[h, j] = dp
                part = lax.dot_general(p.astype(MM_DTYPE), doh[h], TN, preferred_element_type=F32)
                dv = part if dv is None else dv + part
                new.append(delta[h] + jnp.sum(p * dp, axis=1, keepdims=True))
            dv_ref[pl.ds(off, bq), :] += dv
            return tuple(new)

        zero = jnp.zeros((bq, 1), F32)
        delta = lax.fori_loop(0, i, lambda j, d: sweep1(j, d, False), (zero, zero))
        delta = sweep1(i, delta, True)

        def sweep2(j, dq):
            off = pl.multiple_of(j * bq, bq)
            kj = k_ref[pl.ds(off, bq), :]
            dk = None
            for h in range(2):
                ds = p_sc[h, j] * (dp_sc[h, j] - delta[h])
                dcr_ref[h, j] -= jnp.sum(ds, axis=0, keepdims=True)
                dsb = ds.astype(MM_DTYPE)
                dq = dq + jnp.dot(dsb, kj * (masks[h] * scale).astype(kj.dtype), preferred_element_type=F32)
                part = lax.dot_general(dsb, qh[h], TN, preferred_element_type=F32)
                dk = part if dk is None else dk + part
            dk_ref[pl.ds(off, bq), :] += dk
            return dq

        dq_ref[...] = lax.fori_loop(0, i + 1, sweep2, jnp.zeros((bq, LANES), F32))

    blk = lambda col: pl.BlockSpec((bq, LANES), lambda b, hp, i: (b * nq + i, col(hp)))
    seq = lambda col: pl.BlockSpec((S, LANES), lambda b, hp, i: (b, col(hp)))
    per_head = pl.BlockSpec((None, 2, bq, LANES), lambda b, hp, i: (b, hp, i, 0))
    rows = pl.BlockSpec((None, 2, nq, 1, bq), lambda b, hp, i: (b, hp, 0, 0, 0))
    return pl.pallas_call(
        body, name="attn_bwd", grid=(Bl, HP, nq),
        in_specs=[blk(lambda hp: hp), blk(lambda hp: HP + hp), seq(lambda hp: hp), seq(lambda hp: HP + hp),
                  blk(lambda hp: hp), per_head, per_head, rows],
        out_specs=[blk(lambda hp: hp), seq(lambda hp: hp), seq(lambda hp: hp), rows],
        out_shape=[_sds((T, D), F32), _sds((T, D), F32), _sds((T, D), F32), _sds((Bl, H, nq, 1, bq), F32)],
        scratch_shapes=[pltpu.VMEM((2, nq, bq, bq), F32), pltpu.VMEM((2, nq, bq, bq), F32)],
        compiler_params=_params())(qg, qg, kv, kv, dz, lse, c_col, c_row)


def _gate_fwd(qg, o):
    T, D = o.shape
    tm = _tile(T, ROW_TILE)

    def body(g_ref, o_ref, z_ref):
        z_ref[...] = (jax.nn.sigmoid(g_ref[...].astype(F32)) * o_ref[...]).astype(z_ref.dtype)

    return pl.pallas_call(
        body, name="gate_fwd", grid=(T // tm,),
        in_specs=[pl.BlockSpec((tm, D), lambda i: (i, 1)), _rows(tm, D)], out_specs=_rows(tm, D),
        out_shape=_sds((T, D), MM_DTYPE), compiler_params=_params())(qg, o)


def _gate_do(dz, qg):
    T, D = dz.shape
    tm = _tile(T, ROW_TILE)

    def body(dz_ref, g_ref, do_ref):
        do_ref[...] = (dz_ref[...].astype(F32) * jax.nn.sigmoid(g_ref[...].astype(F32))).astype(do_ref.dtype)

    return pl.pallas_call(
        body, name="gate_do", grid=(T // tm,),
        in_specs=[_rows(tm, D), pl.BlockSpec((tm, D), lambda i: (i, 1))], out_specs=_rows(tm, D),
        out_shape=_sds((T, D), MM_DTYPE), compiler_params=_params())(dz, qg)


def _gate_bwd(dz, qg, o, dq):
    T, D = dz.shape
    tm = _tile(T, ROW_TILE)

    def body(dz_ref, g_ref, o_ref, dq_ref, out_ref):
        g = g_ref[...].astype(F32)
        sg = jax.nn.sigmoid(g)
        out_ref[:, :D] = dq_ref[...].astype(out_ref.dtype)
        out_ref[:, D:] = (dz_ref[...].astype(F32) * o_ref[...] * sg * (1.0 - sg)).astype(out_ref.dtype)

    return pl.pallas_call(
        body, name="gate_bwd", grid=(T // tm,),
        in_specs=[_rows(tm, D), pl.BlockSpec((tm, D), lambda i: (i, 1)), _rows(tm, D), _rows(tm, D)],
        out_specs=_rows(tm, 2 * D), out_shape=_sds((T, 2 * D), MM_DTYPE),
        compiler_params=_params())(dz, qg, o, dq)


def _mm_in(a, wg, out_dtype, tag, l=None):
    T, K = a.shape
    n = wg.shape[-1]
    tm = _tile(T, MM_TILE)
    if l is None:
        w_spec = _whole((N_CHIP, K, n))
    else:
        w_spec = pl.BlockSpec((None, N_CHIP, K, n), lambda i: (l, 0, 0, 0))

    def body(a_ref, w_ref, o_ref):
        av = a_ref[...]
        for s in range(N_CHIP):
            o_ref[:, s * n:(s + 1) * n] = jnp.dot(av, w_ref[s], preferred_element_type=F32).astype(o_ref.dtype)

    return pl.pallas_call(
        body, name=f"mm_in_{tag}", grid=(T // tm,),
        in_specs=[_rows(tm, K), w_spec], out_specs=_rows(tm, N_CHIP * n),
        out_shape=_sds((T, N_CHIP * n), out_dtype), compiler_params=_params())(a, wg)


def _mm_nt_in(dy, wg, tag, l=None):
    T = dy.shape[0]
    K, n = wg.shape[-2:]
    tm = _tile(T, MM_TILE)
    if l is None:
        w_spec = _whole((N_CHIP, K, n))
    else:
        w_spec = pl.BlockSpec((None, N_CHIP, K, n), lambda i: (l, 0, 0, 0))

    def body(d_ref, w_ref, o_ref):
        acc = None
        for s in range(N_CHIP):
            part = lax.dot_general(d_ref[:, s * n:(s + 1) * n], w_ref[s], NT, preferred_element_type=F32)
            acc = part if acc is None else acc + part
        o_ref[...] = acc

    return pl.pallas_call(
        body, name=f"mm_nt_in_{tag}", grid=(T // tm,),
        in_specs=[_rows(tm, N_CHIP * n), w_spec], out_specs=_rows(tm, K),
        out_shape=_sds((T, K), F32), compiler_params=_params())(dy, wg)


def _mm_nn(a, b, out_dtype, tag):
    T, K = a.shape
    N = b.shape[1]
    tm = _tile(T, MM_TILE)

    def body(a_ref, b_ref, o_ref):
        o_ref[...] = jnp.dot(a_ref[...], b_ref[...], preferred_element_type=F32).astype(o_ref.dtype)

    return pl.pallas_call(
        body, name=f"mm_nn_{tag}", grid=(T // tm,),
        in_specs=[_rows(tm, K), _whole((K, N))], out_specs=_rows(tm, N),
        out_shape=_sds((T, N), out_dtype), compiler_params=_params())(a, b)


def _mm_nt(a, b, out_dtype, tag):
    T, C = a.shape
    N = b.shape[0]
    tm = _tile(T, MM_TILE)
    nb = N
    for cand in (1408, 1024):
        if N > cand and N % cand == 0:
            nb = cand
            break

    def body(a_ref, b_ref, o_ref):
        o_ref[...] = lax.dot_general(a_ref[...], b_ref[...], NT, preferred_element_type=F32).astype(o_ref.dtype)

    return pl.pallas_call(
        body, name=f"mm_nt_{tag}", grid=(N // nb, T // tm),
        in_specs=[pl.BlockSpec((tm, C), lambda j, i: (i, 0)), pl.BlockSpec((nb, C), lambda j, i: (j, 0))],
        out_specs=pl.BlockSpec((tm, nb), lambda j, i: (i, j)),
        out_shape=_sds((T, N), out_dtype), compiler_params=_params())(a, b)


def _mm_tn_in(a, dy, tag, l=None, prev=None):
    T, K = a.shape
    n = dy.shape[1] // N_CHIP
    tt = _tile(T, TN_TILE)

    def body(a_ref, d_ref, *rest):
        o_ref = rest[-1]
        part = lax.dot_general(a_ref[...], d_ref[...], TN, preferred_element_type=F32)
        _accumulate(o_ref, part, pl.program_id(1) == 0)

    in_specs = [pl.BlockSpec((tt, K), lambda s, t: (t, 0)), pl.BlockSpec((tt, n), lambda s, t: (t, s))]
    args = [a, dy]
    kw = {}
    if l is None:
        out_spec = pl.BlockSpec((None, K, n), lambda s, t: (s, 0, 0))
        out_shape = _sds((N_CHIP, K, n), F32)
    else:
        out_spec = pl.BlockSpec((None, None, K, n), lambda s, t: (l, s, 0, 0))
        out_shape = _sds((2, N_CHIP, K, n), F32)
        if prev is not None:
            in_specs.append(ANY)
            args.append(prev)
            kw["input_output_aliases"] = {2: 0}
    return pl.pallas_call(
        body, name=f"mm_tn_in_{tag}", grid=(N_CHIP, T // tt),
        in_specs=in_specs, out_specs=out_spec, out_shape=out_shape,
        compiler_params=_params(), **kw)(*args)


def _mm_tn_out(act, dh, tag, l=None, prev=None):
    T, R4 = act.shape
    D = dh.shape[1]
    r = R4 // N_CHIP
    g = 1 if r % LANES == 0 else 2
    tt = _tile(T, TN_TILE)

    def body(a_ref, d_ref, *rest):
        o_ref = rest[-1]
        part = lax.dot_general(a_ref[...], d_ref[...], TN, preferred_element_type=F32)
        first = pl.program_id(1) == 0
        for q in range(g):
            _accumulate(o_ref.at[q], part[q * r:(q + 1) * r], first)

    in_specs = [pl.BlockSpec((tt, g * r), lambda s, t: (t, s)), pl.BlockSpec((tt, D), lambda s, t: (t, 0))]
    args = [act, dh]
    kw = {}
    if l is None:
        out_spec = pl.BlockSpec((g, r, D), lambda s, t: (s, 0, 0))
        out_shape = _sds((N_CHIP, r, D), F32)
    else:
        out_spec = pl.BlockSpec((None, g, r, D), lambda s, t: (l, s, 0, 0))
        out_shape = _sds((2, N_CHIP, r, D), F32)
        if prev is not None:
            in_specs.append(ANY)
            args.append(prev)
            kw["input_output_aliases"] = {2: 0}
    return pl.pallas_call(
        body, name=f"mm_tn_out_{tag}", grid=(N_CHIP // g, T // tt),
        in_specs=in_specs, out_specs=out_spec, out_shape=out_shape,
        compiler_params=_params(), **kw)(*args)


def _mm_tn(a, b, tag):
    T, K = a.shape
    N = b.shape[1]
    tt = _tile(T, TN_TILE)

    def body(a_ref, b_ref, o_ref):
        part = lax.dot_general(a_ref[...], b_ref[...], TN, preferred_element_type=F32)
        _accumulate(o_ref, part, pl.program_id(0) == 0)

    return pl.pallas_call(
        body, name=f"mm_tn_{tag}", grid=(T // tt,),
        in_specs=[_rows(tt, K), _rows(tt, N)], out_specs=_whole((K, N)),
        out_shape=_sds((K, N), F32), compiler_params=_params())(a, b)


def _norm_mm_in(x, g, wg, tag, swiglu=False):
    T, D = x.shape
    n = wg.shape[-1]
    tm = _tile(T, FUSED_TILE)
    half = N_CHIP // 2

    def body(x_ref, g_ref, w_ref, xn_ref, y_ref, *rest):
        xv = x_ref[...]
        r = lax.rsqrt(jnp.mean(xv * xv, axis=-1, keepdims=True) + RMS_EPS)
        xn = (xv * r * g_ref[...]).astype(xn_ref.dtype)
        xn_ref[...] = xn

        def product(s):
            p = jnp.dot(xn, w_ref[s], preferred_element_type=F32)
            y_ref[:, s * n:(s + 1) * n] = p.astype(y_ref.dtype)
            return p

        if swiglu:
            for q in range(half):
                gate, up = product(q), product(half + q)
                rest[0][:, q * n:(q + 1) * n] = (gate * jax.nn.sigmoid(gate) * up).astype(rest[0].dtype)
        else:
            for s in range(N_CHIP):
                product(s)

    out_specs = [_rows(tm, D), _rows(tm, N_CHIP * n)]
    out_shape = [_sds((T, D), MM_DTYPE), _sds((T, N_CHIP * n), MM_DTYPE)]
    if swiglu:
        out_specs.append(_rows(tm, half * n))
        out_shape.append(_sds((T, half * n), MM_DTYPE))
    return pl.pallas_call(
        body, name=f"norm_mm_in_{tag}", grid=(T // tm,),
        in_specs=[_rows(tm, D), _whole((1, D)), _resident((N_CHIP, D, n))], out_specs=out_specs,
        out_shape=out_shape, compiler_params=_params())(x, g.reshape(1, D), wg)


def _mm_out_post(a, b, x, g, alpha, tag):
    T, K = a.shape
    D = b.shape[1]
    tm = _tile(T, FUSED_TILE)

    def body(a_ref, b_ref, x_ref, g_ref, h_ref, o_ref):
        hv = jnp.dot(a_ref[...], b_ref[...], preferred_element_type=F32)
        h_ref[...] = hv
        r = lax.rsqrt(jnp.mean(hv * hv, axis=-1, keepdims=True) + RMS_EPS)
        o_ref[...] = x_ref[...] + alpha * (hv * r * g_ref[...])

    return pl.pallas_call(
        body, name=f"mm_out_post_{tag}", grid=(T // tm,),
        in_specs=[_rows(tm, K), _resident((K, D)), _rows(tm, D), _whole((1, D))],
        out_specs=[_rows(tm, D), _rows(tm, D)], out_shape=[_sds((T, D), F32)] * 2,
        compiler_params=_params())(a, b, x, g.reshape(1, D))


def _post_bwd_mm(dx, h, g, alpha, b, tag, hgu=None):
    T, D = dx.shape
    K = b.shape[0]
    tm = _tile(T, FUSED_TILE)

    def body(dx_ref, h_ref, g_ref, b_ref, *rest):
        dh_ref, dg_ref, out_ref = rest[-3:]
        hv = h_ref[...]
        r = lax.rsqrt(jnp.mean(hv * hv, axis=-1, keepdims=True) + RMS_EPS)
        hh = hv * r
        dyn = alpha * dx_ref[...]
        _accumulate(dg_ref, jnp.sum(dyn * hh, axis=0, keepdims=True), pl.program_id(0) == 0)
        dhh = dyn * g_ref[...]
        dh = (r * (dhh - hh * jnp.mean(dhh * hh, axis=-1, keepdims=True))).astype(dh_ref.dtype)
        dh_ref[...] = dh
        da = lax.dot_general(dh, b_ref[...], NT, preferred_element_type=F32)
        if hgu is None:
            out_ref[...] = da.astype(out_ref.dtype)
        else:
            gate = rest[0][:, :K].astype(F32)
            up = rest[0][:, K:].astype(F32)
            sg = jax.nn.sigmoid(gate)
            out_ref[:, :K] = (da * up * sg * (1.0 + gate * (1.0 - sg))).astype(out_ref.dtype)
            out_ref[:, K:] = (da * gate * sg).astype(out_ref.dtype)

    in_specs = [_rows(tm, D), _rows(tm, D), _whole((1, D)), _resident((K, D))]
    args = [dx, h, g.reshape(1, D), b]
    wide = K
    if hgu is not None:
        wide = 2 * K
        in_specs.append(_rows(tm, wide))
        args.append(hgu)
    return pl.pallas_call(
        body, name=f"post_bwd_mm_{tag}", grid=(T // tm,), in_specs=in_specs,
        out_specs=[_rows(tm, D), _whole((1, D)), _rows(tm, wide)],
        out_shape=[_sds((T, D), MM_DTYPE), _sds((1, D), F32), _sds((T, wide), MM_DTYPE)],
        compiler_params=_params())(*args)


def _mm_nt_pre(dy, w, dres, x, g, tag):
    T, C = dy.shape
    D = x.shape[1]
    tm = _tile(T, FUSED_TILE)
    n = w.shape[-1]

    def body(dy_ref, w_ref, dres_ref, x_ref, g_ref, dx_ref, dg_ref):
        if w.ndim == 2:
            dn = lax.dot_general(dy_ref[...], w_ref[...], NT, preferred_element_type=F32)
        else:
            dn = None
            for s in range(N_CHIP):
                part = lax.dot_general(dy_ref[:, s * n:(s + 1) * n], w_ref[s], NT, preferred_element_type=F32)
                dn = part if dn is None else dn + part
        xv = x_ref[...]
        r = lax.rsqrt(jnp.mean(xv * xv, axis=-1, keepdims=True) + RMS_EPS)
        xh = xv * r
        _accumulate(dg_ref, jnp.sum(dn * xh, axis=0, keepdims=True), pl.program_id(0) == 0)
        dxh = dn * g_ref[...]
        dx_ref[...] = dres_ref[...] + r * (dxh - xh * jnp.mean(dxh * xh, axis=-1, keepdims=True))

    return pl.pallas_call(
        body, name=f"mm_nt_pre_{tag}", grid=(T // tm,),
        in_specs=[_rows(tm, C), _resident(w.shape), _rows(tm, D), _rows(tm, D), _whole((1, D))],
        out_specs=[_rows(tm, D), _whole((1, D))], out_shape=[_sds((T, D), F32), _sds((1, D), F32)],
        compiler_params=_params())(dy, w, dres, x, g.reshape(1, D))


def _adamw(w, g, m, v, tag, after=None):
    R, C = w.shape
    tr = _tile(R, ROW_TILE)
    extra = [] if after is None else [after]

    def body(w_ref, g_ref, m_ref, v_ref, *rest):
        go_ref, d_ref, mo_ref, vo_ref = rest[-4:]
        gv = g_ref[...]
        go_ref[...] = gv
        mn = ADAM_B1 * m_ref[...] + (1.0 - ADAM_B1) * gv
        vn = ADAM_B2 * v_ref[...] + (1.0 - ADAM_B2) * (gv * gv)
        m_hat = mn / (1.0 - ADAM_B1 ** ADAM_STEP)
        v_hat = vn / (1.0 - ADAM_B2 ** ADAM_STEP)
        d_ref[...] = -ADAM_LR * (m_hat / (jnp.sqrt(v_hat) + ADAM_EPS) + ADAM_WD * w_ref[...])
        mo_ref[...] = mn
        vo_ref[...] = vn

    return pl.pallas_call(
        body, name=f"adamw_{tag}", grid=(R // tr,),
        in_specs=[_rows(tr, C)] * 4 + [ANY] * len(extra), out_specs=[_rows(tr, C)] * 4,
        out_shape=[_sds((R, C), F32)] * 4, compiler_params=_params())(w, g, m, v, *extra)


def _sum_devices(gall, own, place):
    _, R, C = gall.shape

    def body(place_ref, g_ref, s_ref, o_ref):
        me = 2 * place_ref[1] + place_ref[0]
        acc = None
        for d in range(N_DEV):
            term = jnp.where(me == d, s_ref[...], g_ref[d])
            acc = term if acc is None else acc + term
        o_ref[...] = acc

    grid_spec = pltpu.PrefetchScalarGridSpec(
        num_scalar_prefetch=1, grid=(1,),
        in_specs=[pl.BlockSpec((N_DEV, R, C), lambda i, p: (0, 0, 0)), pl.BlockSpec((R, C), lambda i, p: (0, 0))],
        out_specs=pl.BlockSpec((R, C), lambda i, p: (0, 0)))
    return pl.pallas_call(
        body, name="sum_devices", grid_spec=grid_spec, out_shape=_sds((R, C), F32),
        compiler_params=_params())(place, gall, own)


HBM = pl.BlockSpec(memory_space=pltpu.HBM)
SEM = pl.BlockSpec(memory_space=pltpu.SEMAPHORE)
EFFECT = pltpu.SideEffectType.DATAFLOW_SIDE_EFFECTING


def _place():
    x, y, c = lax.axis_index("x"), lax.axis_index("y"), lax.axis_index("c")
    chips = ((1 - x, y), (x, 1 - y), (1 - x, 1 - y))
    return x, y, c, chips


def _remote(src, dst, send_sem, recv_sem, dev):
    return pltpu.make_async_remote_copy(src_ref=src, dst_ref=dst, send_sem=send_sem, recv_sem=recv_sem,
                                        device_id=dev, device_id_type=MESH)


def _in_hbm(a):
    return pltpu.with_memory_space_constraint(a, pltpu.HBM)


def _own_slot(w4, l, dtype, place, tag):
    _, _, r, col = w4.shape
    tr = _tile(r, 2 * ROW_TILE)

    def body(place_ref, x_ref, o_ref):
        o_ref[...] = x_ref[...].astype(o_ref.dtype)

    grid_spec = pltpu.PrefetchScalarGridSpec(
        num_scalar_prefetch=1, grid=(2, r // tr),
        in_specs=[pl.BlockSpec((None, None, tr, col), lambda h, i, p: (l, h, i, 0))],
        out_specs=pl.BlockSpec((None, None, tr, col), lambda h, i, p: (p[1], h, i, 0)))
    return pl.pallas_call(
        body, name=f"own_slot_{tag}", grid_spec=grid_spec, out_shape=_sds((N_CHIP, 2, r, col), dtype),
        compiler_params=_params())(place, w4)


def _gather_start(bufs, after, tag):
    n = len(bufs)

    def body(*refs):
        ins = refs[:n]
        s_sem, r_sem, token = refs[n + 1], refs[n + 2], refs[2 * n + 3]
        x, y, c, chips = _place()
        me = 2 * x + y
        for i in range(n):
            mine = ins[i].at[me, c]
            for j, (px, py) in enumerate(chips):
                _remote(mine, mine, s_sem.at[3 * i + j], r_sem.at[3 * i + j], (px, py, c)).start()
        token[...] = jnp.zeros_like(token)

    dma = pltpu.SemaphoreType.DMA
    res = pl.pallas_call(
        body, name=f"gather_start_{tag}", in_specs=[HBM] * n + [ANY],
        out_specs=[SEM, SEM] + [HBM] * n + [pl.BlockSpec(memory_space=pltpu.VMEM)],
        out_shape=[dma((3 * n,)), dma((3 * n,))] + [pltpu.HBM(b.shape, b.dtype) for b in bufs] + [_sds((8, LANES), F32)],
        input_output_aliases={i: i + 2 for i in range(n)},
        compiler_params=pltpu.CompilerParams(has_side_effects=EFFECT),
        )(*[_in_hbm(b) for b in bufs], after)
    return res[0], res[1], list(res[2:2 + n]), res[-1]


def _gather_pass(s_sem, r_sem, bufs, first, after, tag):
    n = len(bufs)

    def body(*refs):
        ins = refs[:n]
        a_s, a_r, b_s, b_r = refs[n], refs[n + 1], refs[n + 3], refs[n + 4]
        x, y, c, chips = _place()
        me = 2 * x + y
        sib = (x, y, 1 - c)
        for i in range(n):
            mine = ins[i].at[me, c]
            for j, (px, py) in enumerate(chips):
                k = 3 * (first + i) + j
                _remote(mine, mine, a_s.at[k], a_r.at[k], (px, py, c)).wait_send()
        for j, (px, py) in enumerate(chips):
            for i in range(n):
                k = 3 * (first + i) + j
                blk = ins[i].at[2 * px + py, c]
                _remote(blk, blk, a_s.at[k], a_r.at[k], (px, py, c)).wait_recv()
                _remote(blk, blk, b_s.at[3 * i + j], b_r.at[3 * i + j], sib).start()

    dma = pltpu.SemaphoreType.DMA
    res = pl.pallas_call(
        body, name=f"gather_pass_{tag}", in_specs=[HBM] * n + [SEM, SEM, ANY],
        out_specs=[SEM, SEM] + [HBM] * n,
        out_shape=[dma((3 * n,)), dma((3 * n,))] + [pltpu.HBM(b.shape, b.dtype) for b in bufs],
        input_output_aliases={i: i + 2 for i in range(n)},
        compiler_params=pltpu.CompilerParams(has_side_effects=EFFECT),
        )(*bufs, s_sem, r_sem, after)
    return res[0], res[1], list(res[2:])


def _gather_land(s_sem, r_sem, bufs, tag):
    n = len(bufs)

    def body(*refs):
        ins = refs[:n]
        b_s, b_r = refs[n], refs[n + 1]
        x, y, c, chips = _place()
        sib = (x, y, 1 - c)
        for j, (px, py) in enumerate(chips):
            for i in range(n):
                sent = ins[i].at[2 * px + py, c]
                got = ins[i].at[2 * px + py, 1 - c]
                _remote(sent, sent, b_s.at[3 * i + j], b_r.at[3 * i + j], sib).wait_send()
                _remote(got, got, b_s.at[3 * i + j], b_r.at[3 * i + j], sib).wait_recv()

    return list(pl.pallas_call(
        body, name=f"gather_land_{tag}", in_specs=[HBM] * n + [SEM, SEM], out_specs=[HBM] * n,
        out_shape=[pltpu.HBM(b.shape, b.dtype) for b in bufs],
        input_output_aliases={i: i for i in range(n)},
        compiler_params=pltpu.CompilerParams(has_side_effects=EFFECT),
        )(*bufs, s_sem, r_sem))


def _rs_pair_send(grads):
    n = len(grads)

    def body(*refs):
        ins, outs = refs[:n], refs[n:2 * n]
        s_sem, r_sem = refs[2 * n:]
        x, y, c, _ = _place()
        sib = (x, y, 1 - c)
        sends = []
        for i in range(n):
            cp = _remote(ins[i].at[:, 1 - c], outs[i], s_sem.at[i], r_sem.at[i], sib)
            cp.start()
            sends.append(cp)
        for cp in sends:
            cp.wait()

    out_shape = [_sds((N_CHIP,) + g.shape[2:], g.dtype) for g in grads]
    dma = pltpu.SemaphoreType.DMA
    return pl.pallas_call(
        body, name=f"rs_pair_send_{n}", in_specs=[ANY] * n, out_specs=[ANY] * n, out_shape=out_shape,
        scratch_shapes=[dma((n,)), dma((n,))],
        )(*grads)


def _rs_pair_add(g, recv, place, tag):
    r, col = g.shape[-2:]
    tr = _tile(r, ROW_TILE)

    def body(place_ref, g_ref, r_ref, wire_ref, own_ref):
        tot = g_ref[...] + r_ref[...]
        wire_ref[...] = tot.astype(wire_ref.dtype)

        @pl.when(pl.program_id(1) == place_ref[1])
        def _():
            own_ref[...] = tot

    grid_spec = pltpu.PrefetchScalarGridSpec(
        num_scalar_prefetch=1, grid=(r // tr, N_CHIP),
        in_specs=[pl.BlockSpec((None, None, tr, col), lambda i, s, p: (s, p[0], i, 0)),
                  pl.BlockSpec((None, tr, col), lambda i, s, p: (s, i, 0))],
        out_specs=[pl.BlockSpec((None, tr, col), lambda i, s, p: (s, i, 0)),
                   pl.BlockSpec((tr, col), lambda i, s, p: (i, 0))])
    return pl.pallas_call(
        body, name=f"rs_pair_add_{tag}", grid_spec=grid_spec,
        out_shape=[_sds((N_CHIP, r, col), WIRE_DTYPE), _sds((r, col), F32)],
        compiler_params=_params())(place, g, recv)


def _pair_plan(srcs, lands):
    x, y, c, _ = _place()
    return [(s.at[:, 1 - c], l, (x, y, 1 - c)) for s, l in zip(srcs, lands)]


def _chip_plan(srcs, lands):
    x, y, c, chips = _place()
    plan = []
    for s, l in zip(srcs, lands):
        if len(s.shape) == 2:
            me = 4 * x + 2 * y + c
            plan += [(s, l.at[me], (x ^ (k >> 2), y ^ ((k >> 1) & 1), c ^ (k & 1))) for k in range(1, N_DEV)]
        else:
            plan += [(s.at[2 * px + py], l.at[j], (px, py, c)) for j, (px, py) in enumerate(chips)]
    return plan


def _exchange_start(srcs, lands, plan, count, tag):
    n = len(srcs)
    both = list(srcs) + list(lands)

    def body(*refs):
        s_sem, r_sem, token = refs[2 * n], refs[2 * n + 1], refs[4 * n + 2]
        for k, (src, dst, dev) in enumerate(plan(refs[:n], refs[n:2 * n])):
            _remote(src, dst, s_sem.at[k], r_sem.at[k], dev).start()
        token[...] = jnp.zeros_like(token)

    dma = pltpu.SemaphoreType.DMA
    res = pl.pallas_call(
        body, name=f"exchange_start_{tag}", in_specs=[HBM] * (2 * n),
        out_specs=[SEM, SEM] + [HBM] * (2 * n) + [pl.BlockSpec(memory_space=pltpu.VMEM)],
        out_shape=[dma((count,)), dma((count,))] + [pltpu.HBM(b.shape, b.dtype) for b in both] + [_sds((8, LANES), F32)],
        input_output_aliases={i: i + 2 for i in range(2 * n)},
        compiler_params=pltpu.CompilerParams(has_side_effects=EFFECT),
        )(*[_in_hbm(b) for b in both])
    return res[0], res[1], list(res[2:2 + n]), list(res[2 + n:2 + 2 * n]), res[-1]


def _exchange_wait(s_sem, r_sem, srcs, lands, plan, after, tag):
    n = len(srcs)

    def body(*refs):
        s_ref, r_ref = refs[2 * n], refs[2 * n + 1]
        for k, (src, dst, dev) in enumerate(plan(refs[:n], refs[n:2 * n])):
            cp = _remote(src, dst, s_ref.at[k], r_ref.at[k], dev)
            cp.wait_send()
            cp.wait_recv()

    both = list(srcs) + list(lands)
    res = pl.pallas_call(
        body, name=f"exchange_wait_{tag}", in_specs=[HBM] * (2 * n) + [SEM, SEM, ANY], out_specs=[HBM] * (2 * n),
        out_shape=[pltpu.HBM(b.shape, b.dtype) for b in both],
        input_output_aliases={i: i for i in range(2 * n)},
        compiler_params=pltpu.CompilerParams(has_side_effects=EFFECT),
        )(*both, s_sem, r_sem, after)
    return list(res[:n]), list(res[n:])


def _rs_chip_add(own, recv, place, l, L, prev, tag):
    r, col = own.shape
    tr = _tile(r, ROW_TILE)

    def body(place_ref, o_ref, r_ref, *rest):
        acc = o_ref[...]
        for j in range(3):
            acc = acc + r_ref[j].astype(F32)
        rest[-1][...] = acc

    in_specs = [pl.BlockSpec((tr, col), lambda i, p: (i, 0)), pl.BlockSpec((3, tr, col), lambda i, p: (0, i, 0))]
    args = [place, own, recv]
    kw = {}
    if prev is not None:
        in_specs.append(ANY)
        args.append(prev)
        kw["input_output_aliases"] = {3: 0}
    grid_spec = pltpu.PrefetchScalarGridSpec(
        num_scalar_prefetch=1, grid=(r // tr,), in_specs=in_specs,
        out_specs=pl.BlockSpec((None, None, tr, col), lambda i, p: (l, p[0], i, 0)))
    return pl.pallas_call(
        body, name=f"rs_chip_add_{tag}", grid_spec=grid_spec, out_shape=_sds((L, 2, r, col), F32),
        compiler_params=_params(), **kw)(*args)


def _rs_pair_share(fulls, tag):
    n = len(fulls)

    def body(*refs):
        outs = refs[n:2 * n]
        s_sem, r_sem = refs[2 * n:]
        x, y, c, _ = _place()
        sib = (x, y, 1 - c)
        started = []
        for i in range(n):
            cp = _remote(outs[i].at[:, c], outs[i].at[:, c], s_sem.at[i], r_sem.at[i], sib)
            cp.start()
            started.append(cp)
        for i, cp in enumerate(started):
            cp.wait_send()
            _remote(outs[i].at[:, 1 - c], outs[i].at[:, 1 - c], s_sem.at[i], r_sem.at[i], sib).wait_recv()

    dma = pltpu.SemaphoreType.DMA
    return pl.pallas_call(
        body, name=f"rs_pair_share_{tag}", in_specs=[ANY] * n, out_specs=[ANY] * n,
        out_shape=[_sds(f.shape, f.dtype) for f in fulls],
        input_output_aliases={i: i for i in range(n)},
        scratch_shapes=[dma((n,)), dma((n,))],
        )(*fulls)


def _ffn_fwd(x, g_pre, g_post, w_in, w_out, tag):
    xn, hgu, act = _norm_mm_in(x, g_pre, w_in, tag, swiglu=True)
    if callable(w_out):
        w_out = w_out(act)
    h, x_out = _mm_out_post(act, w_out.reshape(-1, w_out.shape[-1]), x, g_post, 0.5, tag)
    return x_out, (x, xn, hgu, act, h)


def _ffn_bwd(dx, saved, g_pre, g_post, w_in, w_out, tag, between=None):
    x, xn, hgu, act, h = saved
    dh, dg_post, dhgu = _post_bwd_mm(dx, h, g_post, 0.5, w_out.reshape(-1, w_out.shape[-1]), tag, hgu=hgu)
    if between is not None:
        g_pre = g_pre + between(dhgu)[0, :1]
    dw_out = _mm_tn_out(act, dh, tag)
    dw_in = _mm_tn_in(xn, dhgu, tag)
    dx_in, dg_pre = _mm_nt_pre(dhgu, w_in, dx, x, g_pre, tag)
    return dx_in, dg_pre, dg_post, dw_in, dw_out


def kernel(x, ffn1_pre_g, ffn1_post_g, ffn1_w_in, ffn1_w_out, mix_pre_g, mix_post_g, ffn2_pre_g, ffn2_post_g, ffn2_w_in, ffn2_w_out, conv_w_in, conv_k, conv_w_out, kv_g, kv_w, forget_b, attn_w_qg, attn_w_o, loss_target, m_ffn1_pre_g, m_ffn1_post_g, m_ffn1_w_in, m_ffn1_w_out, m_mix_pre_g, m_mix_post_g, m_ffn2_pre_g, m_ffn2_post_g, m_ffn2_w_in, m_ffn2_w_out, m_conv_w_in, m_conv_k, m_conv_w_out, m_kv_g, m_kv_w, m_forget_b, m_attn_w_qg, m_attn_w_o, v_ffn1_pre_g, v_ffn1_post_g, v_ffn1_w_in, v_ffn1_w_out, v_mix_pre_g, v_mix_post_g, v_ffn2_pre_g, v_ffn2_post_g, v_ffn2_w_in, v_ffn2_w_out, v_conv_w_in, v_conv_k, v_conv_w_out, v_kv_g, v_kv_w, v_forget_b, v_attn_w_qg, v_attn_w_o):
    Bl, S, D = x.shape
    T = Bl * S
    H = forget_b.shape[0]
    assert D == H * HEAD_DIM and D % LANES == 0
    kvc = kv_w.shape[1]
    kvp = -(-kvc // LANES) * LANES
    kv_all = 2 * D + LANES
    dk_cols = conv_k.shape[2]
    chip = 2 * lax.axis_index("x") + lax.axis_index("y")
    core = lax.axis_index("c")

    given = dict(ffn1_w_in=(ffn1_w_in, m_ffn1_w_in, v_ffn1_w_in), ffn1_w_out=(ffn1_w_out, m_ffn1_w_out, v_ffn1_w_out),
                 ffn2_w_in=(ffn2_w_in, m_ffn2_w_in, v_ffn2_w_in), ffn2_w_out=(ffn2_w_out, m_ffn2_w_out, v_ffn2_w_out),
                 conv_w_in=(conv_w_in, m_conv_w_in, v_conv_w_in), conv_w_out=(conv_w_out, m_conv_w_out, v_conv_w_out),
                 kv_w=(kv_w, m_kv_w, v_kv_w), attn_w_qg=(attn_w_qg, m_attn_w_qg, v_attn_w_qg),
                 attn_w_o=(attn_w_o, m_attn_w_o, v_attn_w_o))
    shards = {k: w for k, (w, _, _) in given.items()}
    shards["kv_w"] = jnp.pad(kv_w, ((0, 0), (0, kvp - kvc)))[None]
    groups = [[("ffn1_w_in", 0), ("ffn1_w_out", 0)], [("conv_w_in", 0), ("conv_w_out", 0)],
              [("ffn2_w_in", 0), ("ffn2_w_out", 0)], [("kv_w", 0), ("ffn1_w_in", 1), ("ffn1_w_out", 1)],
              [("attn_w_qg", 0), ("attn_w_o", 0), ("ffn2_w_in", 1), ("ffn2_w_out", 1)]]
    first = groups[0] + groups[1] + groups[2]
    second = groups[3] + groups[4]
    place = jnp.stack([core, chip]).astype(jnp.int32)

    def slot(key, where):
        w = shards[key[0]]
        L, r, col = w.shape
        return _own_slot(w.reshape(L, 2, r // 2, col), key[1], MM_DTYPE, where, f"{key[0]}{key[1]}")

    def whole(g):
        return g.reshape(N_CHIP, -1, g.shape[-1])

    taps_slot = _own_slot(jnp.pad(conv_k[0], ((0, 13), (0, 0))).reshape(1, 2, 8, dk_cols), 0, F32, place, "conv_k")
    fb = jnp.pad(forget_b, (0, LANES - H)).reshape(1, LANES)
    w_in0, w_out0 = groups[0]
    s_0, r_0, fly_0, token = _gather_start([slot(w_in0, place), taps_slot, slot(w_out0, place)], fb, "first")
    later = groups[1] + groups[2] + groups[3] + groups[4]
    s_1, r_1, fly_1, token = _gather_start([slot(key, place) for key in later], token, "rest")
    W = {}

    def land(sems, bufs, lo, after, tag):
        return _gather_land(*_gather_pass(*sems, bufs, lo, after, tag), tag)

    def arrive(g, after):
        lo = sum(len(groups[k]) for k in range(1, g))
        got = land((s_1, r_1), fly_1[lo:lo + len(groups[g])], lo, after, f"g{g}")
        W.update({key: whole(b) for key, b in zip(groups[g], got)})

    w_first, taps = land((s_0, r_0), fly_0[:2], 0, token, "g0")
    k_taps = taps.reshape(N_CHIP, 16, dk_cols).transpose(1, 0, 2).reshape(16, D)[:8]

    x0 = x.reshape(T, D)
    W[w_in0] = whole(w_first)

    def first_w_out(act):
        W[w_out0] = whole(land((s_0, r_0), fly_0[2:], 2, act, "g0_out")[0])
        return W[w_out0]

    x1, s_f1a = _ffn_fwd(x0, ffn1_pre_g[0], ffn1_post_g[0], W[w_in0], first_w_out, "l0f1")
    arrive(1, x1)
    w_o_conv = W["conv_w_out", 0].reshape(D, D)
    xn_c, bch = _norm_mm_in(x1, mix_pre_g[0], W["conv_w_in", 0], "conv")
    z_c = _conv_fwd(bch, k_taps, Bl, S)
    m_c, x2 = _mm_out_post(z_c, w_o_conv, x1, mix_post_g[0], 1.0, "conv_out")
    arrive(2, x2)
    x3, s_f2a = _ffn_fwd(x2, ffn2_pre_g[0], ffn2_post_g[0], W["ffn2_w_in", 0], W["ffn2_w_out", 0], "l0f2")

    arrive(3, x3)
    kv_full = jnp.concatenate([W["kv_w", 0][s, :, :kvc] for s in range(N_CHIP)], axis=1)
    kv_full = jnp.pad(kv_full, ((0, 0), (0, kv_all - kv_full.shape[1])))
    xn_kv = _rms_fwd(x3, kv_g, "kv")
    kvact = _mm_nn(xn_kv, kv_full[:, :2 * D], MM_DTYPE, "kv")
    pf = _mm_nn(xn_kv, kv_full[:, 2 * D:], F32, "forget")
    cum = _forget_fwd(pf, fb, Bl, S)
    bq = min(S, ATT_BLOCK)
    c3 = cum.reshape(Bl, S, LANES)[:, :, :H].transpose(0, 2, 1)
    c_col = jnp.broadcast_to(c3[..., None], (Bl, H, S, LANES))
    c_row = c3.reshape(Bl, H, S // bq, 1, bq)

    x4, s_f1b = _ffn_fwd(x3, ffn1_pre_g[1], ffn1_post_g[1], W["ffn1_w_in", 1], W["ffn1_w_out", 1], "l1f1")
    arrive(4, x4)
    w_o_attn = W["attn_w_o", 0].reshape(D, D)
    xn_a, qg = _norm_mm_in(x4, mix_pre_g[1], W["attn_w_qg", 0], "qg")
    o, lse, z_a = _attn_fwd(qg, kvact, c_col, c_row, Bl, S, D)
    m_a, x5 = _mm_out_post(z_a, w_o_attn, x4, mix_post_g[1], 1.0, "attn_out")
    x6, s_f2b = _ffn_fwd(x5, ffn2_pre_g[1], ffn2_post_g[1], W["ffn2_w_in", 1], W["ffn2_w_out", 1], "l1f2")

    dy, loss_local = _loss_grad(x6, loss_target.reshape(T, D))

    G = {}
    dx5, dg_f2pre_1, dg_f2post_1, G["ffn2_w_in", 1], G["ffn2_w_out", 1] = _ffn_bwd(
        dy, s_f2b, ffn2_pre_g[1], ffn2_post_g[1], W["ffn2_w_in", 1], W["ffn2_w_out", 1], "l1f2")
    dm_a, dg_mixpost_1, dz_a = _post_bwd_mm(dx5, m_a, mix_post_g[1], 1.0, w_o_attn, "attn_out")
    G["attn_w_o", 0] = _mm_tn_out(z_a, dm_a, "attn_out")
    dq, dk, dv, dcr = _attn_bwd(qg, kvact, dz_a, lse, c_col, c_row, Bl, S, D)
    dqg = _gate_bwd(dz_a, qg, o, dq)
    G["attn_w_qg", 0] = _mm_tn_in(xn_a, dqg, "qg")
    dx4, dg_mixpre_1 = _mm_nt_pre(dqg, W["attn_w_qg", 0], dx5, x4, mix_pre_g[1], "qg")
    dx3, dg_f1pre_1, dg_f1post_1, G["ffn1_w_in", 1], G["ffn1_w_out", 1] = _ffn_bwd(
        dx4, s_f1b, ffn1_pre_g[1], ffn1_post_g[1], W["ffn1_w_in", 1], W["ffn1_w_out", 1], "l1f1")

    dcum = jnp.pad(dcr.reshape(Bl, H, S).transpose(0, 2, 1), ((0, 0), (0, 0), (0, LANES - H))).reshape(T, LANES)
    dpf, dfb = _forget_bwd(dcum, pf, fb, Bl, S)
    dp = jnp.concatenate([dk.astype(MM_DTYPE), dv.astype(MM_DTYPE), dpf], axis=1)
    G_kv_full = _mm_tn(xn_kv, dp, "kv")
    G["kv_w", 0] = jnp.stack([jnp.pad(G_kv_full[:, s * kvc:(s + 1) * kvc], ((0, 0), (0, kvp - kvc))) for s in range(N_CHIP)])
    dx3, dg_kv = _mm_nt_pre(dp, kv_full, dx3, x3, kv_g, "kv")

    def halves_of(keys):
        return [G[k].reshape(N_CHIP, 2, G[k].shape[1] // 2, G[k].shape[2]) for k in keys]

    def pair_adds(keys, grads, recvs):
        wires, owns = [], []
        for k, g, r in zip(keys, grads, recvs):
            w, own = _rs_pair_add(g, r, place, f"{k[0]}{k[1]}")
            wires.append(w)
            owns.append(own)
        return wires, owns

    def chip_start(wires, tag, extra=()):
        lands = [lax.empty((3,) + w.shape[1:], w.dtype) for w in wires]
        lands += [jnp.zeros((N_DEV,) + e.shape, e.dtype) for e in extra]
        return _exchange_start(list(wires) + list(extra), lands, _chip_plan, 3 * len(wires) + (N_DEV - 1) * len(extra), tag)

    late = groups[2] + groups[1]
    last = groups[0]
    grads_2 = halves_of(second)
    p_sems, p_semr, grads_2, sib_2, token = _exchange_start(
        grads_2, [lax.empty((N_CHIP,) + g.shape[2:], g.dtype) for g in grads_2], _pair_plan, len(grads_2), "pair_second")

    dx2, dg_f2pre_0, dg_f2post_0, G["ffn2_w_in", 0], G["ffn2_w_out", 0] = _ffn_bwd(
        dx3, s_f2a, ffn2_pre_g[0], ffn2_post_g[0] + token[0, :1], W["ffn2_w_in", 0], W["ffn2_w_out", 0], "l0f2")
    grads_2, sib_2 = _exchange_wait(p_sems, p_semr, grads_2, sib_2, _pair_plan, dx2, "pair_second")
    wires_2, owns_2 = pair_adds(second, grads_2, sib_2)
    c_2 = chip_start(wires_2, "chip_second")
    dm_c, dg_mixpost_0, dz_c = _post_bwd_mm(dx2, m_c, mix_post_g[0] + c_2[4][0, :1], 1.0, w_o_conv, "conv_out")
    G["conv_w_out", 0] = _mm_tn_out(z_c, dm_c, "conv_out")
    db, dcg, dhh, dk_taps = _conv_bwd(bch, dz_c, k_taps, Bl, S)
    dbch = jnp.concatenate([db, dcg, dhh], axis=1)
    G["conv_w_in", 0] = _mm_tn_in(xn_c, dbch, "conv")
    dx1, dg_mixpre_0 = _mm_nt_pre(dbch, W["conv_w_in", 0], dx2, x1, mix_pre_g[0], "conv")
    def pair_start(keys, tag):
        grads = halves_of(keys)
        lands = [lax.empty((N_CHIP,) + g.shape[2:], g.dtype) for g in grads]
        return _exchange_start(grads, lands, _pair_plan, len(grads), tag)

    p_l = pair_start(late, "pair_late")
    late_done = {}

    def late_leg(dhgu):
        grads_l, sib_l = _exchange_wait(*p_l[:4], _pair_plan, dhgu, "pair_late")
        late_done["wires"], late_done["owns"] = pair_adds(late, grads_l, sib_l)
        late_done["chip"] = chip_start(late_done["wires"], "chip_late")
        return late_done["chip"][4]

    dx0, dg_f1pre_0, dg_f1post_0, G["ffn1_w_in", 0], G["ffn1_w_out", 0] = _ffn_bwd(
        dx1, s_f1a, ffn1_pre_g[0], ffn1_post_g[0] + p_l[4][0, :1], W["ffn1_w_in", 0], W["ffn1_w_out", 0], "l0f1",
        between=late_leg)
    grad_x = dx0.reshape(Bl, S, D)
    owns_l, c_l = late_done["owns"], late_done["chip"]

    p_1 = pair_start(last, "pair_last")
    _, recvs_2 = _exchange_wait(*c_2[:4], _chip_plan, p_1[4], "chip_second")
    _, recvs_l = _exchange_wait(*c_l[:4], _chip_plan, p_1[4], "chip_late")
    partial = {}

    def chip_adds(keys, owns, recvs):
        for (name, l), own, rcv in zip(keys, owns, recvs):
            partial[name] = _rs_chip_add(own, rcv, place, l, shards[name].shape[0], partial.get(name), f"{name}{l}")
        return partial[name]

    added = chip_adds(late + second, owns_l + owns_2, recvs_l + recvs_2)
    grads_1, sib_1 = _exchange_wait(*p_1[:4], _pair_plan, added, "pair_last")
    wires_1, owns_1 = pair_adds(last, grads_1, sib_1)

    def row(v):
        return jnp.pad(v.reshape(-1), (0, D - v.size)).reshape(1, D)

    small_parts = [dg_f1pre_0, dg_f1pre_1, dg_f1post_0, dg_f1post_1, dg_mixpre_0, dg_mixpre_1, dg_mixpost_0, dg_mixpost_1,
                   dg_f2pre_0, dg_f2pre_1, dg_f2post_0, dg_f2post_1, dg_kv, row(dfb[0, :H]), dk_taps[:3],
                   jnp.full((1, D), loss_local)]
    small = jnp.concatenate(small_parts, axis=0)
    small = jnp.pad(small, ((0, SMALL_ROWS - small.shape[0]), (0, 0)))
    c_1 = chip_start(wires_1, "chip_last", extra=[small])
    res = {}

    def adamw(names, reduced, after):
        for k, red in zip(names, reduced):
            w, m, v = given[k]
            g2 = red.reshape(-1, red.shape[-1])
            if k == "kv_w":
                g2 = g2[:, :kvc]
            flat = lambda a: a.reshape(-1, a.shape[-1])
            go, d, mn, vn = _adamw(flat(w), g2, flat(m), flat(v), k, after=after)
            res[k] = tuple(a.reshape(w.shape) for a in (go, d, mn, vn))
        return d

    early = [k for k in partial if (k, 0) not in last]
    done = adamw(early, _rs_pair_share([partial[k] for k in early], "early"), c_1[4])
    _, recvs_1 = _exchange_wait(*c_1[:4], _chip_plan, done, "chip_last")
    chip_adds(last, owns_1, recvs_1[:-1])
    rest = [k for k, _ in last]
    adamw(rest, _rs_pair_share([partial[k] for k in rest], "last"), None)
    gsum = _sum_devices(recvs_1[-1], small, place)
    loss = gsum[17, 0]

    small_names = ["ffn1_pre_g", "ffn1_post_g", "mix_pre_g", "mix_post_g", "ffn2_pre_g", "ffn2_post_g"]
    small_given = dict(ffn1_pre_g=(ffn1_pre_g, m_ffn1_pre_g, v_ffn1_pre_g), ffn1_post_g=(ffn1_post_g, m_ffn1_post_g, v_ffn1_post_g),
                       mix_pre_g=(mix_pre_g, m_mix_pre_g, v_mix_pre_g), mix_post_g=(mix_post_g, m_mix_post_g, v_mix_post_g),
                       ffn2_pre_g=(ffn2_pre_g, m_ffn2_pre_g, v_ffn2_pre_g), ffn2_post_g=(ffn2_post_g, m_ffn2_post_g, v_ffn2_post_g))

    def pack(idx):
        rows_ = [small_given[k][idx] for k in small_names]
        rows_ += [row((kv_g, m_kv_g, v_kv_g)[idx]), row((forget_b, m_forget_b, v_forget_b)[idx])]
        rows_.append(jnp.pad((conv_k, m_conv_k, v_conv_k)[idx][0], ((0, 0), (0, D - dk_cols))))
        a = jnp.concatenate(rows_, axis=0)
        return jnp.pad(a, ((0, SMALL_ROWS - a.shape[0]), (0, 0)))

    g_taps = lax.dynamic_slice_in_dim(gsum[14:17], chip * dk_cols, dk_cols, axis=1)
    g_small = jnp.concatenate([gsum[:14], jnp.pad(g_taps, ((0, 0), (0, D - dk_cols))), gsum[17:]], axis=0)
    g_small, d_s, m_s, v_s = _adamw(pack(0), g_small, pack(1), pack(2), "small")
    for i, k in enumerate(small_names):
        res[k] = tuple(a[2 * i:2 * i + 2] for a in (g_small, d_s, m_s, v_s))
    res["kv_g"] = tuple(a[12] for a in (g_small, d_s, m_s, v_s))
    res["forget_b"] = tuple(a[13, :H] for a in (g_small, d_s, m_s, v_s))
    res["conv_k"] = tuple(a[14:17, :dk_cols][None] for a in (g_small, d_s, m_s, v_s))

    order = ["ffn1_pre_g", "ffn1_post_g", "ffn1_w_in", "ffn1_w_out", "mix_pre_g", "mix_post_g", "ffn2_pre_g", "ffn2_post_g",
             "ffn2_w_in", "ffn2_w_out", "conv_w_in", "conv_k", "conv_w_out", "kv_g", "kv_w", "forget_b", "attn_w_qg", "attn_w_o"]
    out = [loss, grad_x]
    for idx in range(4):
        out += [res[k][idx] for k in order]
    return tuple(out)
```

```python
import functools
import math

import jax
import jax.numpy as jnp
from jax import lax
from jax.experimental import pallas as pl
from jax.experimental.pallas import tpu as pltpu

F32 = jnp.float32
MM_DTYPE = jnp.bfloat16
WIRE_DTYPE = jnp.bfloat16

RMS_EPS = 1e-6
ADAM_LR = 0.001
ADAM_B1 = 0.9
ADAM_B2 = 0.999
ADAM_EPS = 1e-08
ADAM_WD = 0.01
ADAM_STEP = 10

HEAD_DIM = 64
LANES = 128
N_CHIP = 4
N_DEV = 8
ROW_TILE = 256
MM_TILE = 512
FUSED_TILE = 512
TN_TILE = 2048
ATT_BLOCK = 512
SMALL_ROWS = 24
VMEM_LIMIT = 56 * 1024 * 1024
MESH = pl.DeviceIdType.MESH
ANY = pl.BlockSpec(memory_space=pl.ANY)

NT = (((1,), (1,)), ((), ()))
TN = (((0,), (0,)), ((), ()))


def _tile(n, pref):
    if n <= pref:
        return n
    t = pref - pref % 16
    while n % t:
        t -= 16
    return t


def _params():
    return pltpu.CompilerParams(vmem_limit_bytes=VMEM_LIMIT)


def _sds(shape, dtype):
    return jax.ShapeDtypeStruct(shape, dtype)


def _rows(tm, c):
    return pl.BlockSpec((tm, c), lambda i: (i, 0))


def _whole(shape):
    return pl.BlockSpec(shape, lambda *_: (0,) * len(shape))


def _resident(shape):
    return pl.BlockSpec(shape, lambda *_: (0,) * len(shape), pipeline_mode=pl.Buffered(1))


def _rms_fwd(x, g, tag):
    T, D = x.shape
    tm = _tile(T, ROW_TILE)

    def body(x_ref, g_ref, o_ref):
        xv = x_ref[...]
        r = lax.rsqrt(jnp.mean(xv * xv, axis=-1, keepdims=True) + RMS_EPS)
        o_ref[...] = (xv * r * g_ref[...]).astype(o_ref.dtype)

    return pl.pallas_call(
        body, name=f"rms_fwd_{tag}", grid=(T // tm,),
        in_specs=[_rows(tm, D), _whole((1, D))], out_specs=_rows(tm, D),
        out_shape=_sds((T, D), MM_DTYPE), compiler_params=_params())(x, g.reshape(1, D))


def _post_fwd(x, h, g, alpha, tag):
    T, D = x.shape
    tm = _tile(T, ROW_TILE)

    def body(x_ref, h_ref, g_ref, o_ref):
        hv = h_ref[...]
        r = lax.rsqrt(jnp.mean(hv * hv, axis=-1, keepdims=True) + RMS_EPS)
        o_ref[...] = x_ref[...] + alpha * (hv * r * g_ref[...])

    return pl.pallas_call(
        body, name=f"post_fwd_{tag}", grid=(T // tm,),
        in_specs=[_rows(tm, D), _rows(tm, D), _whole((1, D))], out_specs=_rows(tm, D),
        out_shape=_sds((T, D), F32), compiler_params=_params())(x, h, g.reshape(1, D))


def _accumulate(ref, part, first):
    @pl.when(first)
    def _():
        ref[...] = part

    @pl.when(jnp.logical_not(first))
    def _():
        ref[...] += part


def _post_bwd(dx, h, g, alpha, tag):
    T, D = dx.shape
    tm = _tile(T, ROW_TILE)

    def body(dx_ref, h_ref, g_ref, dh_ref, dg_ref):
        hv = h_ref[...]
        r = lax.rsqrt(jnp.mean(hv * hv, axis=-1, keepdims=True) + RMS_EPS)
        hh = hv * r
        dyn = alpha * dx_ref[...]
        _accumulate(dg_ref, jnp.sum(dyn * hh, axis=0, keepdims=True), pl.program_id(0) == 0)
        dhh = dyn * g_ref[...]
        dh = r * (dhh - hh * jnp.mean(dhh * hh, axis=-1, keepdims=True))
        dh_ref[...] = dh.astype(dh_ref.dtype)

    return pl.pallas_call(
        body, name=f"post_bwd_{tag}", grid=(T // tm,),
        in_specs=[_rows(tm, D), _rows(tm, D), _whole((1, D))],
        out_specs=[_rows(tm, D), _whole((1, D))],
        out_shape=[_sds((T, D), MM_DTYPE), _sds((1, D), F32)],
        compiler_params=_params())(dx, h, g.reshape(1, D))


def _pre_bwd(dres, dxn, x, g, tag):
    T, D = x.shape
    tm = _tile(T, ROW_TILE)

    def body(dres_ref, dxn_ref, x_ref, g_ref, dx_ref, dg_ref):
        xv = x_ref[...]
        r = lax.rsqrt(jnp.mean(xv * xv, axis=-1, keepdims=True) + RMS_EPS)
        xh = xv * r
        dn = dxn_ref[...]
        _accumulate(dg_ref, jnp.sum(dn * xh, axis=0, keepdims=True), pl.program_id(0) == 0)
        dxh = dn * g_ref[...]
        dx_ref[...] = dres_ref[...] + r * (dxh - xh * jnp.mean(dxh * xh, axis=-1, keepdims=True))

    return pl.pallas_call(
        body, name=f"pre_bwd_{tag}", grid=(T // tm,),
        in_specs=[_rows(tm, D), _rows(tm, D), _rows(tm, D), _whole((1, D))],
        out_specs=[_rows(tm, D), _whole((1, D))],
        out_shape=[_sds((T, D), F32), _sds((1, D), F32)],
        compiler_params=_params())(dres, dxn, x, g.reshape(1, D))


def _swiglu_fwd(hgu, tag):
    T, F2 = hgu.shape
    F = F2 // 2
    tm = _tile(T, ROW_TILE)

    def body(g_ref, u_ref, o_ref):
        g = g_ref[...].astype(F32)
        o_ref[...] = (g * jax.nn.sigmoid(g) * u_ref[...].astype(F32)).astype(o_ref.dtype)

    return pl.pallas_call(
        body, name=f"swiglu_fwd_{tag}", grid=(T // tm,),
        in_specs=[pl.BlockSpec((tm, F), lambda i: (i, 0)), pl.BlockSpec((tm, F), lambda i: (i, 1))],
        out_specs=_rows(tm, F), out_shape=_sds((T, F), MM_DTYPE), compiler_params=_params())(hgu, hgu)


def _swiglu_bwd(hgu, da, tag):
    T, F2 = hgu.shape
    F = F2 // 2
    tm = _tile(T, ROW_TILE)

    def body(h_ref, da_ref, o_ref):
        g = h_ref[:, :F].astype(F32)
        u = h_ref[:, F:].astype(F32)
        d = da_ref[...].astype(F32)
        sg = jax.nn.sigmoid(g)
        o_ref[:, :F] = (d * u * sg * (1.0 + g * (1.0 - sg))).astype(o_ref.dtype)
        o_ref[:, F:] = (d * g * sg).astype(o_ref.dtype)

    return pl.pallas_call(
        body, name=f"swiglu_bwd_{tag}", grid=(T // tm,),
        in_specs=[_rows(tm, F2), _rows(tm, F)], out_specs=_rows(tm, F2),
        out_shape=_sds((T, F2), MM_DTYPE), compiler_params=_params())(hgu, da)


def _loss_grad(y, tgt):
    T, D = y.shape
    tm = _tile(T, ROW_TILE)

    def body(y_ref, t_ref, dy_ref, l_ref):
        e = y_ref[...] - t_ref[...]
        row = jnp.mean(e * e, axis=-1, keepdims=True)
        part = jnp.broadcast_to(jnp.sum(row, axis=0, keepdims=True), (8, LANES))
        _accumulate(l_ref, part, pl.program_id(0) == 0)
        dy_ref[...] = e * (1.0 / D)

    dy, lsum = pl.pallas_call(
        body, name="loss_grad", grid=(T // tm,),
        in_specs=[_rows(tm, D), _rows(tm, D)], out_specs=[_rows(tm, D), _whole((8, LANES))],
        out_shape=[_sds((T, D), F32), _sds((8, LANES), F32)], compiler_params=_params())(y, tgt)
    return dy, 0.5 * lsum[0, 0]


def _shift_down(u, d, rows):
    return jnp.where(rows >= d, pltpu.roll(u, d, 0), 0.0)


def _shift_up(u, d, rows, S):
    return jnp.where(rows < S - d, pltpu.roll(u, S - d, 0), 0.0)


def _conv_fwd(bch, k8, Bl, S):
    T, D3 = bch.shape
    D = D3 // 3
    dc = min(D, 2 * LANES)
    nd = D // dc

    def body(b_ref, c_ref, h_ref, k_ref, z_ref):
        rows = lax.broadcasted_iota(jnp.int32, (S, 1), 0)
        u = c_ref[...].astype(F32) * h_ref[...].astype(F32)
        y = k_ref[2:3, :] * u + k_ref[1:2, :] * _shift_down(u, 1, rows) + k_ref[0:1, :] * _shift_down(u, 2, rows)
        z_ref[...] = (b_ref[...].astype(F32) * y).astype(z_ref.dtype)

    return pl.pallas_call(
        body, name="conv_fwd", grid=(Bl, nd),
        in_specs=[pl.BlockSpec((S, dc), lambda b, j: (b, j)),
                  pl.BlockSpec((S, dc), lambda b, j: (b, nd + j)),
                  pl.BlockSpec((S, dc), lambda b, j: (b, 2 * nd + j)),
                  pl.BlockSpec((8, dc), lambda b, j: (0, j))],
        out_specs=pl.BlockSpec((S, dc), lambda b, j: (b, j)),
        out_shape=_sds((T, D), MM_DTYPE), compiler_params=_params())(bch, bch, bch, k8)


def _conv_bwd(bch, dz, k8, Bl, S):
    T, D3 = bch.shape
    D = D3 // 3
    dc = min(D, 2 * LANES)
    nd = D // dc

    def body(b_ref, c_ref, h_ref, dz_ref, k_ref, db_ref, dc_ref, dh_ref, dk_ref):
        rows = lax.broadcasted_iota(jnp.int32, (S, 1), 0)
        bv = b_ref[...].astype(F32)
        cv = c_ref[...].astype(F32)
        hv = h_ref[...].astype(F32)
        dzv = dz_ref[...].astype(F32)
        u = cv * hv
        u1 = _shift_down(u, 1, rows)
        u2 = _shift_down(u, 2, rows)
        y = k_ref[2:3, :] * u + k_ref[1:2, :] * u1 + k_ref[0:1, :] * u2
        db_ref[...] = (dzv * y).astype(db_ref.dtype)
        dy = dzv * bv
        du = k_ref[2:3, :] * dy + k_ref[1:2, :] * _shift_up(dy, 1, rows, S) + k_ref[0:1, :] * _shift_up(dy, 2, rows, S)
        dc_ref[...] = (du * hv).astype(dc_ref.dtype)
        dh_ref[...] = (du * cv).astype(dh_ref.dtype)

        @pl.when(pl.program_id(1) == 0)
        def _():
            dk_ref[...] = jnp.zeros_like(dk_ref)

        dk_ref[0:1, :] += jnp.sum(dy * u2, axis=0, keepdims=True)
        dk_ref[1:2, :] += jnp.sum(dy * u1, axis=0, keepdims=True)
        dk_ref[2:3, :] += jnp.sum(dy * u, axis=0, keepdims=True)

    seq = lambda off: pl.BlockSpec((S, dc), lambda j, b: (b, off + j))
    return pl.pallas_call(
        body, name="conv_bwd", grid=(nd, Bl),
        in_specs=[seq(0), seq(nd), seq(2 * nd), seq(0), pl.BlockSpec((8, dc), lambda j, b: (0, j))],
        out_specs=[seq(0), seq(0), seq(0), pl.BlockSpec((8, dc), lambda j, b: (0, j))],
        out_shape=[_sds((T, D), MM_DTYPE)] * 3 + [_sds((8, D), F32)],
        compiler_params=_params())(bch, bch, bch, dz, k8)


def _forget_fwd(pf, fb, Bl, S):
    T = pf.shape[0]

    def body(p_ref, fb_ref, c_ref):
        rows = lax.broadcasted_iota(jnp.int32, (S, 1), 0)
        z = p_ref[...] + fb_ref[...]
        acc = jnp.minimum(z, 0.0) - jnp.log1p(jnp.exp(-jnp.abs(z)))
        d = 1
        while d < S:
            acc = acc + _shift_down(acc, d, rows)
            d *= 2
        c_ref[...] = acc

    return pl.pallas_call(
        body, name="forget_fwd", grid=(Bl,),
        in_specs=[_rows(S, LANES), _whole((1, LANES))], out_specs=_rows(S, LANES),
        out_shape=_sds((T, LANES), F32), compiler_params=_params())(pf, fb)


def _forget_bwd(dc, pf, fb, Bl, S):
    T = pf.shape[0]

    def body(dc_ref, p_ref, fb_ref, df_ref, dfb_ref):
        rows = lax.broadcasted_iota(jnp.int32, (S, 1), 0)
        acc = dc_ref[...]
        d = 1
        while d < S:
            acc = acc + _shift_up(acc, d, rows, S)
            d *= 2
        df = acc * jax.nn.sigmoid(-(p_ref[...] + fb_ref[...]))
        df_ref[...] = df.astype(df_ref.dtype)
        _accumulate(dfb_ref, jnp.sum(df, axis=0, keepdims=True), pl.program_id(0) == 0)

    return pl.pallas_call(
        body, name="forget_bwd", grid=(Bl,),
        in_specs=[_rows(S, LANES), _rows(S, LANES), _whole((1, LANES))],
        out_specs=[_rows(S, LANES), _whole((1, LANES))],
        out_shape=[_sds((T, LANES), MM_DTYPE), _sds((1, LANES), F32)],
        compiler_params=_params())(dc, pf, fb)


def _head_mask(h):
    lane = lax.broadcasted_iota(jnp.int32, (1, LANES), 1)
    return (lane >= h * HEAD_DIM) & (lane < (h + 1) * HEAD_DIM)


def _attn_fwd(qg, kv, c_col, c_row, Bl, S, D):
    T = Bl * S
    H = D // HEAD_DIM
    HP = D // LANES
    bq = min(S, ATT_BLOCK)
    nq = S // bq
    scale = 1.0 / math.sqrt(HEAD_DIM)

    def body(q_ref, g_ref, k_ref, v_ref, cc_ref, cr_ref, o_ref, lse_ref, z_ref):
        i = pl.program_id(2)
        q2 = q_ref[...]
        qh = [q2 * (_head_mask(h).astype(F32) * scale).astype(q2.dtype) for h in range(2)]
        cc = [cc_ref[h][:, :1] for h in range(2)]
        diag = lax.broadcasted_iota(jnp.int32, (1, bq), 1) <= lax.broadcasted_iota(jnp.int32, (bq, 1), 0)

        def block(j, carry, on_diagonal):
            off = pl.multiple_of(j * bq, bq)
            kj = k_ref[pl.ds(off, bq), :]
            vj = v_ref[pl.ds(off, bq), :]
            new = []
            for h in range(2):
                m, l, acc = carry[h]
                s = lax.dot_general(qh[h], kj, NT, preferred_element_type=F32) + cc[h] - cr_ref[h, j]
                if on_diagonal:
                    s = jnp.where(diag, s, -jnp.inf)
                m_new = jnp.maximum(m, jnp.max(s, axis=1, keepdims=True))
                p = jnp.exp(s - m_new)
                a = jnp.exp(m - m_new)
                l = a * l + jnp.sum(p, axis=1, keepdims=True)
                acc = a * acc + jnp.dot(p.astype(MM_DTYPE), vj, preferred_element_type=F32)
                new.append((m_new, l, acc))
            return tuple(new)

        one = (jnp.full((bq, 1), -jnp.inf, F32), jnp.zeros((bq, 1), F32), jnp.zeros((bq, LANES), F32))
        carry = lax.fori_loop(0, i, lambda j, c: block(j, c, False), (one, one))
        carry = block(i, carry, True)
        outs = []
        for h in range(2):
            m, l, acc = carry[h]
            outs.append(acc / l)
            lse_ref[h] = jnp.broadcast_to(m + jnp.log(l), (bq, LANES))
        o2 = jnp.where(_head_mask(0), outs[0], outs[1])
        o_ref[...] = o2
        z_ref[...] = (jax.nn.sigmoid(g_ref[...].astype(F32)) * o2).astype(z_ref.dtype)

    return pl.pallas_call(
        body, name="attn_fwd", grid=(Bl, HP, nq),
        in_specs=[pl.BlockSpec((bq, LANES), lambda b, hp, i: (b * nq + i, hp)),
                  pl.BlockSpec((bq, LANES), lambda b, hp, i: (b * nq + i, HP + hp)),
                  pl.BlockSpec((S, LANES), lambda b, hp, i: (b, hp)),
                  pl.BlockSpec((S, LANES), lambda b, hp, i: (b, HP + hp)),
                  pl.BlockSpec((None, 2, bq, LANES), lambda b, hp, i: (b, hp, i, 0)),
                  pl.BlockSpec((None, 2, nq, 1, bq), lambda b, hp, i: (b, hp, 0, 0, 0))],
        out_specs=[pl.BlockSpec((bq, LANES), lambda b, hp, i: (b * nq + i, hp)),
                   pl.BlockSpec((None, 2, bq, LANES), lambda b, hp, i: (b, hp, i, 0)),
                   pl.BlockSpec((bq, LANES), lambda b, hp, i: (b * nq + i, hp))],
        out_shape=[_sds((T, D), F32), _sds((Bl, H, S, LANES), F32), _sds((T, D), MM_DTYPE)],
        compiler_params=_params())(qg, qg, kv, kv, c_col, c_row)


def _attn_bwd(qg, kv, dz, lse, c_col, c_row, Bl, S, D):
    T = Bl * S
    H = D // HEAD_DIM
    HP = D // LANES
    bq = min(S, ATT_BLOCK)
    nq = S // bq
    scale = 1.0 / math.sqrt(HEAD_DIM)

    def body(q_ref, g_ref, k_ref, v_ref, dz_ref, lse_ref, cc_ref, cr_ref, dq_ref, dk_ref, dv_ref, dcr_ref, p_sc, dp_sc):
        i = pl.program_id(2)

        @pl.when(i == 0)
        def _():
            dk_ref[...] = jnp.zeros_like(dk_ref)
            dv_ref[...] = jnp.zeros_like(dv_ref)
            dcr_ref[...] = jnp.zeros_like(dcr_ref)

        q2 = q_ref[...]
        do2 = (dz_ref[...].astype(F32) * jax.nn.sigmoid(g_ref[...].astype(F32))).astype(MM_DTYPE)
        masks = [_head_mask(h).astype(F32) for h in range(2)]
        qh = [q2 * (masks[h] * scale).astype(q2.dtype) for h in range(2)]
        doh = [do2 * masks[h].astype(do2.dtype) for h in range(2)]
        cc = [cc_ref[h][:, :1] for h in range(2)]
        lse = [lse_ref[h][:, :1] for h in range(2)]
        diag = lax.broadcasted_iota(jnp.int32, (1, bq), 1) <= lax.broadcasted_iota(jnp.int32, (bq, 1), 0)

        def sweep1(j, delta, on_diagonal):
            off = pl.multiple_of(j * bq, bq)
            kj = k_ref[pl.ds(off, bq), :]
            vj = v_ref[pl.ds(off, bq), :]
            new = []
            dv = None
            for h in range(2):
                s = lax.dot_general(qh[h], kj, NT, preferred_element_type=F32) + cc[h] - cr_ref[h, j]
                if on_diagonal:
                    s = jnp.where(diag, s, -jnp.inf)
                p = jnp.exp(s - lse[h])
                dp = lax.dot_general(doh[h], vj, NT, preferred_element_type=F32)
                p_sc[h, j] = p
                dp_sc[h, j] = dp
                part = lax.dot_general(p.astype(MM_DTYPE), doh[h], TN, preferred_element_type=F32)
                dv = part if dv is None else dv + part
                new.append(delta[h] + jnp.sum(p * dp, axis=1, keepdims=True))
            dv_ref[pl.ds(off, bq), :] += dv
            return tuple(new)

        zero = jnp.zeros((bq, 1), F32)
        delta = lax.fori_loop(0, i, lambda j, d: sweep1(j, d, False), (zero, zero))
        delta = sweep1(i, delta, True)

        def sweep2(j, dq):
            off = pl.multiple_of(j * bq, bq)
            kj = k_ref[pl.ds(off, bq), :]
            dk = None
            for h in range(2):
                ds = p_sc[h, j] * (dp_sc[h, j] - delta[h])
                dcr_ref[h, j] -= jnp.sum(ds, axis=0, keepdims=True)
                dsb = ds.astype(MM_DTYPE)
                dq = dq + jnp.dot(dsb, kj * (masks[h] * scale).astype(kj.dtype), preferred_element_type=F32)
                part = lax.dot_general(dsb, qh[h], TN, preferred_element_type=F32)
                dk = part if dk is None else dk + part
            dk_ref[pl.ds(off, bq), :] += dk
            return dq

        dq_ref[...] = lax.fori_loop(0, i + 1, sweep2, jnp.zeros((bq, LANES), F32))

    blk = lambda col: pl.BlockSpec((bq, LANES), lambda b, hp, i: (b * nq + i, col(hp)))
    seq = lambda col: pl.BlockSpec((S, LANES), lambda b, hp, i: (b, col(hp)))
    per_head = pl.BlockSpec((None, 2, bq, LANES), lambda b, hp, i: (b, hp, i, 0))
    rows = pl.BlockSpec((None, 2, nq, 1, bq), lambda b, hp, i: (b, hp, 0, 0, 0))
    return pl.pallas_call(
        body, name="attn_bwd", grid=(Bl, HP, nq),
        in_specs=[blk(lambda hp: hp), blk(lambda hp: HP + hp), seq(lambda hp: hp), seq(lambda hp: HP + hp),
                  blk(lambda hp: hp), per_head, per_head, rows],
        out_specs=[blk(lambda hp: hp), seq(lambda hp: hp), seq(lambda hp: hp), rows],
        out_shape=[_sds((T, D), F32), _sds((T, D), F32), _sds((T, D), F32), _sds((Bl, H, nq, 1, bq), F32)],
        scratch_shapes=[pltpu.VMEM((2, nq, bq, bq), F32), pltpu.VMEM((2, nq, bq, bq), F32)],
        compiler_params=_params())(qg, qg, kv, kv, dz, lse, c_col, c_row)


def _gate_fwd(qg, o):
    T, D = o.shape
    tm = _tile(T, ROW_TILE)

    def body(g_ref, o_ref, z_ref):
        z_ref[...] = (jax.nn.sigmoid(g_ref[...].astype(F32)) * o_ref[...]).astype(z_ref.dtype)

    return pl.pallas_call(
        body, name="gate_fwd", grid=(T // tm,),
        in_specs=[pl.BlockSpec((tm, D), lambda i: (i, 1)), _rows(tm, D)], out_specs=_rows(tm, D),
        out_shape=_sds((T, D), MM_DTYPE), compiler_params=_params())(qg, o)


def _gate_do(dz, qg):
    T, D = dz.shape
    tm = _tile(T, ROW_TILE)

    def body(dz_ref, g_ref, do_ref):
        do_ref[...] = (dz_ref[...].astype(F32) * jax.nn.sigmoid(g_ref[...].astype(F32))).astype(do_ref.dtype)

    return pl.pallas_call(
        body, name="gate_do", grid=(T // tm,),
        in_specs=[_rows(tm, D), pl.BlockSpec((tm, D), lambda i: (i, 1))], out_specs=_rows(tm, D),
        out_shape=_sds((T, D), MM_DTYPE), compiler_params=_params())(dz, qg)


def _gate_bwd(dz, qg, o, dq):
    T, D = dz.shape
    tm = _tile(T, ROW_TILE)

    def body(dz_ref, g_ref, o_ref, dq_ref, out_ref):
        g = g_ref[...].astype(F32)
        sg = jax.nn.sigmoid(g)
        out_ref[:, :D] = dq_ref[...].astype(out_ref.dtype)
        out_ref[:, D:] = (dz_ref[...].astype(F32) * o_ref[...] * sg * (1.0 - sg)).astype(out_ref.dtype)

    return pl.pallas_call(
        body, name="gate_bwd", grid=(T // tm,),
        in_specs=[_rows(tm, D), pl.BlockSpec((tm, D), lambda i: (i, 1)), _rows(tm, D), _rows(tm, D)],
        out_specs=_rows(tm, 2 * D), out_shape=_sds((T, 2 * D), MM_DTYPE),
        compiler_params=_params())(dz, qg, o, dq)


def _mm_in(a, wg, out_dtype, tag, l=None):
    T, K = a.shape
    n = wg.shape[-1]
    tm = _tile(T, MM_TILE)
    if l is None:
        w_spec = _whole((N_CHIP, K, n))
    else:
        w_spec = pl.BlockSpec((None, N_CHIP, K, n), lambda i: (l, 0, 0, 0))

    def body(a_ref, w_ref, o_ref):
        av = a_ref[...]
        for s in range(N_CHIP):
            o_ref[:, s * n:(s + 1) * n] = jnp.dot(av, w_ref[s], preferred_element_type=F32).astype(o_ref.dtype)

    return pl.pallas_call(
        body, name=f"mm_in_{tag}", grid=(T // tm,),
        in_specs=[_rows(tm, K), w_spec], out_specs=_rows(tm, N_CHIP * n),
        out_shape=_sds((T, N_CHIP * n), out_dtype), compiler_params=_params())(a, wg)


def _mm_nt_in(dy, wg, tag, l=None):
    T = dy.shape[0]
    K, n = wg.shape[-2:]
    tm = _tile(T, MM_TILE)
    if l is None:
        w_spec = _whole((N_CHIP, K, n))
    else:
        w_spec = pl.BlockSpec((None, N_CHIP, K, n), lambda i: (l, 0, 0, 0))

    def body(d_ref, w_ref, o_ref):
        acc = None
        for s in range(N_CHIP):
            part = lax.dot_general(d_ref[:, s * n:(s + 1) * n], w_ref[s], NT, preferred_element_type=F32)
            acc = part if acc is None else acc + part
        o_ref[...] = acc

    return pl.pallas_call(
        body, name=f"mm_nt_in_{tag}", grid=(T // tm,),
        in_specs=[_rows(tm, N_CHIP * n), w_spec], out_specs=_rows(tm, K),
        out_shape=_sds((T, K), F32), compiler_params=_params())(dy, wg)


def _mm_nn(a, b, out_dtype, tag):
    T, K = a.shape
    N = b.shape[1]
    tm = _tile(T, MM_TILE)

    def body(a_ref, b_ref, o_ref):
        o_ref[...] = jnp.dot(a_ref[...], b_ref[...], preferred_element_type=F32).astype(o_ref.dtype)

    return pl.pallas_call(
        body, name=f"mm_nn_{tag}", grid=(T // tm,),
        in_specs=[_rows(tm, K), _whole((K, N))], out_specs=_rows(tm, N),
        out_shape=_sds((T, N), out_dtype), compiler_params=_params())(a, b)


def _mm_nt(a, b, out_dtype, tag):
    T, C = a.shape
    N = b.shape[0]
    tm = _tile(T, MM_TILE)
    nb = N
    for cand in (1408, 1024):
        if N > cand and N % cand == 0:
            nb = cand
            break

    def body(a_ref, b_ref, o_ref):
        o_ref[...] = lax.dot_general(a_ref[...], b_ref[...], NT, preferred_element_type=F32).astype(o_ref.dtype)

    return pl.pallas_call(
        body, name=f"mm_nt_{tag}", grid=(N // nb, T // tm),
        in_specs=[pl.BlockSpec((tm, C), lambda j, i: (i, 0)), pl.BlockSpec((nb, C), lambda j, i: (j, 0))],
        out_specs=pl.BlockSpec((tm, nb), lambda j, i: (i, j)),
        out_shape=_sds((T, N), out_dtype), compiler_params=_params())(a, b)


def _mm_tn_in(a, dy, tag, after=None):
    T, K = a.shape
    n = dy.shape[1] // N_CHIP
    tt = _tile(T, TN_TILE)
    extra = [] if after is None else [after]

    def body(a_ref, d_ref, *rest):
        part = lax.dot_general(a_ref[...], d_ref[...], TN, preferred_element_type=F32)
        _accumulate(rest[-1], part, pl.program_id(1) == 0)

    return pl.pallas_call(
        body, name=f"mm_tn_in_{tag}", grid=(N_CHIP, T // tt),
        in_specs=[pl.BlockSpec((tt, K), lambda s, t: (t, 0)), pl.BlockSpec((tt, n), lambda s, t: (t, s))]
        + [ANY] * len(extra),
        out_specs=pl.BlockSpec((None, K, n), lambda s, t: (s, 0, 0)), out_shape=_sds((N_CHIP, K, n), F32),
        compiler_params=_params())(a, dy, *extra)


def _mm_tn_out(act, dh, tag, after=None):
    T, R4 = act.shape
    D = dh.shape[1]
    r = R4 // N_CHIP
    g = 1 if r % LANES == 0 else 2
    tt = _tile(T, TN_TILE)
    extra = [] if after is None else [after]

    def body(a_ref, d_ref, *rest):
        o_ref = rest[-1]
        part = lax.dot_general(a_ref[...], d_ref[...], TN, preferred_element_type=F32)
        first = pl.program_id(1) == 0
        for q in range(g):
            _accumulate(o_ref.at[q], part[q * r:(q + 1) * r], first)

    return pl.pallas_call(
        body, name=f"mm_tn_out_{tag}", grid=(N_CHIP // g, T // tt),
        in_specs=[pl.BlockSpec((tt, g * r), lambda s, t: (t, s)), pl.BlockSpec((tt, D), lambda s, t: (t, 0))]
        + [ANY] * len(extra),
        out_specs=pl.BlockSpec((g, r, D), lambda s, t: (s, 0, 0)), out_shape=_sds((N_CHIP, r, D), F32),
        compiler_params=_params())(act, dh, *extra)


def _mm_tn(a, b, tag):
    T, K = a.shape
    N = b.shape[1]
    tt = _tile(T, TN_TILE)

    def body(a_ref, b_ref, o_ref):
        part = lax.dot_general(a_ref[...], b_ref[...], TN, preferred_element_type=F32)
        _accumulate(o_ref, part, pl.program_id(0) == 0)

    return pl.pallas_call(
        body, name=f"mm_tn_{tag}", grid=(T // tt,),
        in_specs=[_rows(tt, K), _rows(tt, N)], out_specs=_whole((K, N)),
        out_shape=_sds((K, N), F32), compiler_params=_params())(a, b)


def _norm_mm_in(x, g, wg, tag, swiglu=False):
    T, D = x.shape
    n = wg.shape[-1]
    tm = _tile(T, FUSED_TILE)
    half = N_CHIP // 2

    def body(x_ref, g_ref, w_ref, xn_ref, y_ref, *rest):
        xv = x_ref[...]
        r = lax.rsqrt(jnp.mean(xv * xv, axis=-1, keepdims=True) + RMS_EPS)
        xn = (xv * r * g_ref[...]).astype(xn_ref.dtype)
        xn_ref[...] = xn

        def product(s):
            p = jnp.dot(xn, w_ref[s], preferred_element_type=F32)
            y_ref[:, s * n:(s + 1) * n] = p.astype(y_ref.dtype)
            return p

        if swiglu:
            for q in range(half):
                gate, up = product(q), product(half + q)
                rest[0][:, q * n:(q + 1) * n] = (gate * jax.nn.sigmoid(gate) * up).astype(rest[0].dtype)
        else:
            for s in range(N_CHIP):
                product(s)

    out_specs = [_rows(tm, D), _rows(tm, N_CHIP * n)]
    out_shape = [_sds((T, D), MM_DTYPE), _sds((T, N_CHIP * n), MM_DTYPE)]
    if swiglu:
        out_specs.append(_rows(tm, half * n))
        out_shape.append(_sds((T, half * n), MM_DTYPE))
    return pl.pallas_call(
        body, name=f"norm_mm_in_{tag}", grid=(T // tm,),
        in_specs=[_rows(tm, D), _whole((1, D)), _resident((N_CHIP, D, n))], out_specs=out_specs,
        out_shape=out_shape, compiler_params=_params())(x, g.reshape(1, D), wg)


def _mm_out_post(a, b, x, g, alpha, tag):
    T, K = a.shape
    D = b.shape[1]
    tm = _tile(T, FUSED_TILE)

    def body(a_ref, b_ref, x_ref, g_ref, h_ref, o_ref):
        hv = jnp.dot(a_ref[...], b_ref[...], preferred_element_type=F32)
        h_ref[...] = hv
        r = lax.rsqrt(jnp.mean(hv * hv, axis=-1, keepdims=True) + RMS_EPS)
        o_ref[...] = x_ref[...] + alpha * (hv * r * g_ref[...])

    return pl.pallas_call(
        body, name=f"mm_out_post_{tag}", grid=(T // tm,),
        in_specs=[_rows(tm, K), _resident((K, D)), _rows(tm, D), _whole((1, D))],
        out_specs=[_rows(tm, D), _rows(tm, D)], out_shape=[_sds((T, D), F32)] * 2,
        compiler_params=_params())(a, b, x, g.reshape(1, D))


def _post_bwd_mm(dx, h, g, alpha, b, tag, hgu=None):
    T, D = dx.shape
    K = b.shape[0]
    tm = _tile(T, FUSED_TILE)

    def body(dx_ref, h_ref, g_ref, b_ref, *rest):
        dh_ref, dg_ref, out_ref = rest[-3:]
        hv = h_ref[...]
        r = lax.rsqrt(jnp.mean(hv * hv, axis=-1, keepdims=True) + RMS_EPS)
        hh = hv * r
        dyn = alpha * dx_ref[...]
        _accumulate(dg_ref, jnp.sum(dyn * hh, axis=0, keepdims=True), pl.program_id(0) == 0)
        dhh = dyn * g_ref[...]
        dh = (r * (dhh - hh * jnp.mean(dhh * hh, axis=-1, keepdims=True))).astype(dh_ref.dtype)
        dh_ref[...] = dh
        da = lax.dot_general(dh, b_ref[...], NT, preferred_element_type=F32)
        if hgu is None:
            out_ref[...] = da.astype(out_ref.dtype)
        else:
            gate = rest[0][:, :K].astype(F32)
            up = rest[0][:, K:].astype(F32)
            sg = jax.nn.sigmoid(gate)
            out_ref[:, :K] = (da * up * sg * (1.0 + gate * (1.0 - sg))).astype(out_ref.dtype)
            out_ref[:, K:] = (da * gate * sg).astype(out_ref.dtype)

    in_specs = [_rows(tm, D), _rows(tm, D), _whole((1, D)), _resident((K, D))]
    args = [dx, h, g.reshape(1, D), b]
    wide = K
    if hgu is not None:
        wide = 2 * K
        in_specs.append(_rows(tm, wide))
        args.append(hgu)
    return pl.pallas_call(
        body, name=f"post_bwd_mm_{tag}", grid=(T // tm,), in_specs=in_specs,
        out_specs=[_rows(tm, D), _whole((1, D)), _rows(tm, wide)],
        out_shape=[_sds((T, D), MM_DTYPE), _sds((1, D), F32), _sds((T, wide), MM_DTYPE)],
        compiler_params=_params())(*args)


def _mm_nt_pre(dy, w, dres, x, g, tag):
    T, C = dy.shape
    D = x.shape[1]
    tm = _tile(T, FUSED_TILE)
    n = w.shape[-1]

    def body(dy_ref, w_ref, dres_ref, x_ref, g_ref, dx_ref, dg_ref):
        if w.ndim == 2:
            dn = lax.dot_general(dy_ref[...], w_ref[...], NT, preferred_element_type=F32)
        else:
            dn = None
            for s in range(N_CHIP):
                part = lax.dot_general(dy_ref[:, s * n:(s + 1) * n], w_ref[s], NT, preferred_element_type=F32)
                dn = part if dn is None else dn + part
        xv = x_ref[...]
        r = lax.rsqrt(jnp.mean(xv * xv, axis=-1, keepdims=True) + RMS_EPS)
        xh = xv * r
        _accumulate(dg_ref, jnp.sum(dn * xh, axis=0, keepdims=True), pl.program_id(0) == 0)
        dxh = dn * g_ref[...]
        dx_ref[...] = dres_ref[...] + r * (dxh - xh * jnp.mean(dxh * xh, axis=-1, keepdims=True))

    return pl.pallas_call(
        body, name=f"mm_nt_pre_{tag}", grid=(T // tm,),
        in_specs=[_rows(tm, C), _resident(w.shape), _rows(tm, D), _rows(tm, D), _whole((1, D))],
        out_specs=[_rows(tm, D), _whole((1, D))], out_shape=[_sds((T, D), F32), _sds((1, D), F32)],
        compiler_params=_params())(dy, w, dres, x, g.reshape(1, D))


def _adamw(w, g, m, v, tag, after=None):
    R, C = w.shape
    tr = _tile(R, ROW_TILE)
    extra = [] if after is None else [after]

    def body(w_ref, g_ref, m_ref, v_ref, *rest):
        go_ref, d_ref, mo_ref, vo_ref = rest[-4:]
        gv = g_ref[...]
        go_ref[...] = gv
        mn = ADAM_B1 * m_ref[...] + (1.0 - ADAM_B1) * gv
        vn = ADAM_B2 * v_ref[...] + (1.0 - ADAM_B2) * (gv * gv)
        m_hat = mn / (1.0 - ADAM_B1 ** ADAM_STEP)
        v_hat = vn / (1.0 - ADAM_B2 ** ADAM_STEP)
        d_ref[...] = -ADAM_LR * (m_hat / (jnp.sqrt(v_hat) + ADAM_EPS) + ADAM_WD * w_ref[...])
        mo_ref[...] = mn
        vo_ref[...] = vn

    return pl.pallas_call(
        body, name=f"adamw_{tag}", grid=(R // tr,),
        in_specs=[_rows(tr, C)] * 4 + [ANY] * len(extra), out_specs=[_rows(tr, C)] * 4,
        out_shape=[_sds((R, C), F32)] * 4, compiler_params=_params())(w, g, m, v, *extra)


def _sum_devices(gall, own, place):
    _, R, C = gall.shape

    def body(place_ref, g_ref, s_ref, o_ref):
        me = 2 * place_ref[1] + place_ref[0]
        acc = None
        for d in range(N_DEV):
            term = jnp.where(me == d, s_ref[...], g_ref[d])
            acc = term if acc is None else acc + term
        o_ref[...] = acc

    grid_spec = pltpu.PrefetchScalarGridSpec(
        num_scalar_prefetch=1, grid=(1,),
        in_specs=[pl.BlockSpec((N_DEV, R, C), lambda i, p: (0, 0, 0)), pl.BlockSpec((R, C), lambda i, p: (0, 0))],
        out_specs=pl.BlockSpec((R, C), lambda i, p: (0, 0)))
    return pl.pallas_call(
        body, name="sum_devices", grid_spec=grid_spec, out_shape=_sds((R, C), F32),
        compiler_params=_params())(place, gall, own)


HBM = pl.BlockSpec(memory_space=pltpu.HBM)
SEM = pl.BlockSpec(memory_space=pltpu.SEMAPHORE)
EFFECT = pltpu.SideEffectType.DATAFLOW_SIDE_EFFECTING


def _place():
    x, y, c = lax.axis_index("x"), lax.axis_index("y"), lax.axis_index("c")
    chips = ((1 - x, y), (x, 1 - y), (1 - x, 1 - y))
    return x, y, c, chips


def _remote(src, dst, send_sem, recv_sem, dev):
    return pltpu.make_async_remote_copy(src_ref=src, dst_ref=dst, send_sem=send_sem, recv_sem=recv_sem,
                                        device_id=dev, device_id_type=MESH)


def _in_hbm(a):
    return pltpu.with_memory_space_constraint(a, pltpu.HBM)


def _own_slot(w4, l, dtype, place, tag):
    _, _, r, col = w4.shape
    tr = _tile(r, 2 * ROW_TILE)

    def body(place_ref, x_ref, o_ref):
        o_ref[...] = x_ref[...].astype(o_ref.dtype)

    grid_spec = pltpu.PrefetchScalarGridSpec(
        num_scalar_prefetch=1, grid=(2, r // tr),
        in_specs=[pl.BlockSpec((None, None, tr, col), lambda h, i, p: (l, h, i, 0))],
        out_specs=pl.BlockSpec((None, None, tr, col), lambda h, i, p: (p[1], h, i, 0)))
    return pl.pallas_call(
        body, name=f"own_slot_{tag}", grid_spec=grid_spec, out_shape=_sds((N_CHIP, 2, r, col), dtype),
        compiler_params=_params())(place, w4)


def _gather_start(bufs, after, tag):
    n = len(bufs)

    def body(*refs):
        ins = refs[:n]
        s_sem, r_sem, token = refs[n + 1], refs[n + 2], refs[2 * n + 3]
        x, y, c, chips = _place()
        me = 2 * x + y
        for i in range(n):
            mine = ins[i].at[me, c]
            for j, (px, py) in enumerate(chips):
                _remote(mine, mine, s_sem.at[3 * i + j], r_sem.at[3 * i + j], (px, py, c)).start()
        token[...] = jnp.zeros_like(token)

    dma = pltpu.SemaphoreType.DMA
    res = pl.pallas_call(
        body, name=f"gather_start_{tag}", in_specs=[HBM] * n + [ANY],
        out_specs=[SEM, SEM] + [HBM] * n + [pl.BlockSpec(memory_space=pltpu.VMEM)],
        out_shape=[dma((3 * n,)), dma((3 * n,))] + [pltpu.HBM(b.shape, b.dtype) for b in bufs] + [_sds((8, LANES), F32)],
        input_output_aliases={i: i + 2 for i in range(n)},
        compiler_params=pltpu.CompilerParams(has_side_effects=EFFECT),
        )(*[_in_hbm(b) for b in bufs], after)
    return res[0], res[1], list(res[2:2 + n]), res[-1]


def _gather_pass(s_sem, r_sem, bufs, first, after, tag):
    n = len(bufs)

    def body(*refs):
        ins = refs[:n]
        a_s, a_r, b_s, b_r = refs[n], refs[n + 1], refs[n + 3], refs[n + 4]
        x, y, c, chips = _place()
        me = 2 * x + y
        sib = (x, y, 1 - c)
        for i in range(n):
            mine = ins[i].at[me, c]
            for j, (px, py) in enumerate(chips):
                k = 3 * (first + i) + j
                _remote(mine, mine, a_s.at[k], a_r.at[k], (px, py, c)).wait_send()
        for j, (px, py) in enumerate(chips):
            for i in range(n):
                k = 3 * (first + i) + j
                blk = ins[i].at[2 * px + py, c]
                _remote(blk, blk, a_s.at[k], a_r.at[k], (px, py, c)).wait_recv()
                _remote(blk, blk, b_s.at[3 * i + j], b_r.at[3 * i + j], sib).start()

    dma = pltpu.SemaphoreType.DMA
    res = pl.pallas_call(
        body, name=f"gather_pass_{tag}", in_specs=[HBM] * n + [SEM, SEM, ANY],
        out_specs=[SEM, SEM] + [HBM] * n,
        out_shape=[dma((3 * n,)), dma((3 * n,))] + [pltpu.HBM(b.shape, b.dtype) for b in bufs],
        input_output_aliases={i: i + 2 for i in range(n)},
        compiler_params=pltpu.CompilerParams(has_side_effects=EFFECT),
        )(*bufs, s_sem, r_sem, after)
    return res[0], res[1], list(res[2:])


def _gather_land(s_sem, r_sem, bufs, tag):
    n = len(bufs)

    def body(*refs):
        ins = refs[:n]
        b_s, b_r = refs[n], refs[n + 1]
        x, y, c, chips = _place()
        sib = (x, y, 1 - c)
        for j, (px, py) in enumerate(chips):
            for i in range(n):
                sent = ins[i].at[2 * px + py, c]
                got = ins[i].at[2 * px + py, 1 - c]
                _remote(sent, sent, b_s.at[3 * i + j], b_r.at[3 * i + j], sib).wait_send()
                _remote(got, got, b_s.at[3 * i + j], b_r.at[3 * i + j], sib).wait_recv()

    return list(pl.pallas_call(
        body, name=f"gather_land_{tag}", in_specs=[HBM] * n + [SEM, SEM], out_specs=[HBM] * n,
        out_shape=[pltpu.HBM(b.shape, b.dtype) for b in bufs],
        input_output_aliases={i: i for i in range(n)},
        compiler_params=pltpu.CompilerParams(has_side_effects=EFFECT),
        )(*bufs, s_sem, r_sem))


def _rs_pair_send(grads):
    n = len(grads)

    def body(*refs):
        ins, outs = refs[:n], refs[n:2 * n]
        s_sem, r_sem = refs[2 * n:]
        x, y, c, _ = _place()
        sib = (x, y, 1 - c)
        sends = []
        for i in range(n):
            cp = _remote(ins[i].at[:, 1 - c], outs[i], s_sem.at[i], r_sem.at[i], sib)
            cp.start()
            sends.append(cp)
        for cp in sends:
            cp.wait()

    out_shape = [_sds((N_CHIP,) + g.shape[2:], g.dtype) for g in grads]
    dma = pltpu.SemaphoreType.DMA
    return pl.pallas_call(
        body, name=f"rs_pair_send_{n}", in_specs=[ANY] * n, out_specs=[ANY] * n, out_shape=out_shape,
        scratch_shapes=[dma((n,)), dma((n,))],
        )(*grads)


def _rs_pair_add(g, recv, place, tag):
    r, col = g.shape[-2:]
    tr = _tile(r, ROW_TILE)

    def body(place_ref, g_ref, r_ref, wire_ref, own_ref):
        tot = g_ref[...] + r_ref[...]
        wire_ref[...] = tot.astype(wire_ref.dtype)

        @pl.when(pl.program_id(1) == place_ref[1])
        def _():
            own_ref[...] = tot

    grid_spec = pltpu.PrefetchScalarGridSpec(
        num_scalar_prefetch=1, grid=(r // tr, N_CHIP),
        in_specs=[pl.BlockSpec((None, None, tr, col), lambda i, s, p: (s, p[0], i, 0)),
                  pl.BlockSpec((None, tr, col), lambda i, s, p: (s, i, 0))],
        out_specs=[pl.BlockSpec((None, tr, col), lambda i, s, p: (s, i, 0)),
                   pl.BlockSpec((tr, col), lambda i, s, p: (i, 0))])
    return pl.pallas_call(
        body, name=f"rs_pair_add_{tag}", grid_spec=grid_spec,
        out_shape=[_sds((N_CHIP, r, col), WIRE_DTYPE), _sds((r, col), F32)],
        compiler_params=_params())(place, g, recv)


def _pair_plan(srcs, lands):
    x, y, c, _ = _place()
    return [(s.at[:, 1 - c], l, (x, y, 1 - c)) for s, l in zip(srcs, lands)]


def _chip_plan(srcs, lands):
    x, y, c, chips = _place()
    plan = []
    for s, l in zip(srcs, lands):
        if len(s.shape) == 2:
            me = 4 * x + 2 * y + c
            plan += [(s, l.at[me], (x ^ (k >> 2), y ^ ((k >> 1) & 1), c ^ (k & 1))) for k in range(1, N_DEV)]
        else:
            plan += [(s.at[2 * px + py], l.at[j], (px, py, c)) for j, (px, py) in enumerate(chips)]
    return plan


def _exchange_start(srcs, lands, plan, count, tag):
    n = len(srcs)
    both = list(srcs) + list(lands)

    def body(*refs):
        s_sem, r_sem, token = refs[2 * n], refs[2 * n + 1], refs[4 * n + 2]
        for k, (src, dst, dev) in enumerate(plan(refs[:n], refs[n:2 * n])):
            _remote(src, dst, s_sem.at[k], r_sem.at[k], dev).start()
        token[...] = jnp.zeros_like(token)

    dma = pltpu.SemaphoreType.DMA
    res = pl.pallas_call(
        body, name=f"exchange_start_{tag}", in_specs=[HBM] * (2 * n),
        out_specs=[SEM, SEM] + [HBM] * (2 * n) + [pl.BlockSpec(memory_space=pltpu.VMEM)],
        out_shape=[dma((count,)), dma((count,))] + [pltpu.HBM(b.shape, b.dtype) for b in both] + [_sds((8, LANES), F32)],
        input_output_aliases={i: i + 2 for i in range(2 * n)},
        compiler_params=pltpu.CompilerParams(has_side_effects=EFFECT),
        )(*[_in_hbm(b) for b in both])
    return res[0], res[1], list(res[2:2 + n]), list(res[2 + n:2 + 2 * n]), res[-1]


def _exchange_wait(s_sem, r_sem, srcs, lands, plan, after, tag):
    n = len(srcs)

    def body(*refs):
        s_ref, r_ref = refs[2 * n], refs[2 * n + 1]
        for k, (src, dst, dev) in enumerate(plan(refs[:n], refs[n:2 * n])):
            cp = _remote(src, dst, s_ref.at[k], r_ref.at[k], dev)
            cp.wait_send()
            cp.wait_recv()

    both = list(srcs) + list(lands)
    res = pl.pallas_call(
        body, name=f"exchange_wait_{tag}", in_specs=[HBM] * (2 * n) + [SEM, SEM, ANY], out_specs=[HBM] * (2 * n),
        out_shape=[pltpu.HBM(b.shape, b.dtype) for b in both],
        input_output_aliases={i: i for i in range(2 * n)},
        compiler_params=pltpu.CompilerParams(has_side_effects=EFFECT),
        )(*both, s_sem, r_sem, after)
    return list(res[:n]), list(res[n:])


def _rs_chip_add(own, recv, place, l, L, prev, tag):
    r, col = own.shape
    tr = _tile(r, ROW_TILE)

    def body(place_ref, o_ref, r_ref, *rest):
        acc = o_ref[...]
        for j in range(3):
            acc = acc + r_ref[j].astype(F32)
        rest[-1][...] = acc

    in_specs = [pl.BlockSpec((tr, col), lambda i, p: (i, 0)), pl.BlockSpec((3, tr, col), lambda i, p: (0, i, 0))]
    args = [place, own, recv]
    kw = {}
    if prev is not None:
        in_specs.append(ANY)
        args.append(prev)
        kw["input_output_aliases"] = {3: 0}
    grid_spec = pltpu.PrefetchScalarGridSpec(
        num_scalar_prefetch=1, grid=(r // tr,), in_specs=in_specs,
        out_specs=pl.BlockSpec((None, None, tr, col), lambda i, p: (l, p[0], i, 0)))
    return pl.pallas_call(
        body, name=f"rs_chip_add_{tag}", grid_spec=grid_spec, out_shape=_sds((L, 2, r, col), F32),
        compiler_params=_params(), **kw)(*args)


def _rs_pair_share(fulls, tag):
    n = len(fulls)

    def body(*refs):
        outs = refs[n:2 * n]
        s_sem, r_sem = refs[2 * n:]
        x, y, c, _ = _place()
        sib = (x, y, 1 - c)
        started = []
        for i in range(n):
            cp = _remote(outs[i].at[:, c], outs[i].at[:, c], s_sem.at[i], r_sem.at[i], sib)
            cp.start()
            started.append(cp)
        for i, cp in enumerate(started):
            cp.wait_send()
            _remote(outs[i].at[:, 1 - c], outs[i].at[:, 1 - c], s_sem.at[i], r_sem.at[i], sib).wait_recv()

    dma = pltpu.SemaphoreType.DMA
    return pl.pallas_call(
        body, name=f"rs_pair_share_{tag}", in_specs=[ANY] * n, out_specs=[ANY] * n,
        out_shape=[_sds(f.shape, f.dtype) for f in fulls],
        input_output_aliases={i: i for i in range(n)},
        scratch_shapes=[dma((n,)), dma((n,))],
        )(*fulls)


def _ffn_fwd(x, g_pre, g_post, w_in, w_out, tag):
    xn, hgu, act = _norm_mm_in(x, g_pre, w_in, tag, swiglu=True)
    if callable(w_out):
        w_out = w_out(act)
    h, x_out = _mm_out_post(act, w_out.reshape(-1, w_out.shape[-1]), x, g_post, 0.5, tag)
    return x_out, (x, xn, hgu, act, h)


def _ffn_bwd(dx, saved, g_pre, g_post, w_in, w_out, tag, between=None):
    x, xn, hgu, act, h = saved
    dh, dg_post, dhgu = _post_bwd_mm(dx, h, g_post, 0.5, w_out.reshape(-1, w_out.shape[-1]), tag, hgu=hgu)
    token = None
    if between is not None:
        token = between(dhgu)
        g_pre = g_pre + token[0, :1]
    dw_out = _mm_tn_out(act, dh, tag, after=token)
    dw_in = _mm_tn_in(xn, dhgu, tag, after=token)
    dx_in, dg_pre = _mm_nt_pre(dhgu, w_in, dx, x, g_pre, tag)
    return dx_in, dg_pre, dg_post, dw_in, dw_out


def kernel(x, ffn1_pre_g, ffn1_post_g, ffn1_w_in, ffn1_w_out, mix_pre_g, mix_post_g, ffn2_pre_g, ffn2_post_g, ffn2_w_in, ffn2_w_out, conv_w_in, conv_k, conv_w_out, kv_g, kv_w, forget_b, attn_w_qg, attn_w_o, loss_target, m_ffn1_pre_g, m_ffn1_post_g, m_ffn1_w_in, m_ffn1_w_out, m_mix_pre_g, m_mix_post_g, m_ffn2_pre_g, m_ffn2_post_g, m_ffn2_w_in, m_ffn2_w_out, m_conv_w_in, m_conv_k, m_conv_w_out, m_kv_g, m_kv_w, m_forget_b, m_attn_w_qg, m_attn_w_o, v_ffn1_pre_g, v_ffn1_post_g, v_ffn1_w_in, v_ffn1_w_out, v_mix_pre_g, v_mix_post_g, v_ffn2_pre_g, v_ffn2_post_g, v_ffn2_w_in, v_ffn2_w_out, v_conv_w_in, v_conv_k, v_conv_w_out, v_kv_g, v_kv_w, v_forget_b, v_attn_w_qg, v_attn_w_o):
    Bl, S, D = x.shape
    T = Bl * S
    H = forget_b.shape[0]
    assert D == H * HEAD_DIM and D % LANES == 0
    kvc = kv_w.shape[1]
    kvp = -(-kvc // LANES) * LANES
    kv_all = 2 * D + LANES
    dk_cols = conv_k.shape[2]
    chip = 2 * lax.axis_index("x") + lax.axis_index("y")
    core = lax.axis_index("c")

    given = dict(ffn1_w_in=(ffn1_w_in, m_ffn1_w_in, v_ffn1_w_in), ffn1_w_out=(ffn1_w_out, m_ffn1_w_out, v_ffn1_w_out),
                 ffn2_w_in=(ffn2_w_in, m_ffn2_w_in, v_ffn2_w_in), ffn2_w_out=(ffn2_w_out, m_ffn2_w_out, v_ffn2_w_out),
                 conv_w_in=(conv_w_in, m_conv_w_in, v_conv_w_in), conv_w_out=(conv_w_out, m_conv_w_out, v_conv_w_out),
                 kv_w=(kv_w, m_kv_w, v_kv_w), attn_w_qg=(attn_w_qg, m_attn_w_qg, v_attn_w_qg),
                 attn_w_o=(attn_w_o, m_attn_w_o, v_attn_w_o))
    shards = {k: w for k, (w, _, _) in given.items()}
    shards["kv_w"] = jnp.pad(kv_w, ((0, 0), (0, kvp - kvc)))[None]
    groups = [[("ffn1_w_in", 0), ("ffn1_w_out", 0)], [("conv_w_in", 0), ("conv_w_out", 0)],
              [("ffn2_w_in", 0), ("ffn2_w_out", 0)], [("kv_w", 0), ("ffn1_w_in", 1), ("ffn1_w_out", 1)],
              [("attn_w_qg", 0), ("attn_w_o", 0), ("ffn2_w_in", 1), ("ffn2_w_out", 1)]]
    first = groups[0] + groups[1] + groups[2]
    second = groups[3] + groups[4]
    place = jnp.stack([core, chip]).astype(jnp.int32)

    def slot(key, where):
        w = shards[key[0]]
        L, r, col = w.shape
        return _own_slot(w.reshape(L, 2, r // 2, col), key[1], MM_DTYPE, where, f"{key[0]}{key[1]}")

    def whole(g):
        return g.reshape(N_CHIP, -1, g.shape[-1])

    taps_slot = _own_slot(jnp.pad(conv_k[0], ((0, 13), (0, 0))).reshape(1, 2, 8, dk_cols), 0, F32, place, "conv_k")
    fb = jnp.pad(forget_b, (0, LANES - H)).reshape(1, LANES)
    w_in0, w_out0 = groups[0]
    s_0, r_0, fly_0, token = _gather_start([slot(w_in0, place), taps_slot, slot(w_out0, place)], fb, "first")
    later = groups[1] + groups[2] + groups[3] + groups[4]
    s_1, r_1, fly_1, token = _gather_start([slot(key, place) for key in later], token, "rest")
    W = {}

    def land(sems, bufs, lo, after, tag):
        return _gather_land(*_gather_pass(*sems, bufs, lo, after, tag), tag)

    def arrive(g, after):
        lo = sum(len(groups[k]) for k in range(1, g))
        got = land((s_1, r_1), fly_1[lo:lo + len(groups[g])], lo, after, f"g{g}")
        W.update({key: whole(b) for key, b in zip(groups[g], got)})

    w_first, taps = land((s_0, r_0), fly_0[:2], 0, token, "g0")
    k_taps = taps.reshape(N_CHIP, 16, dk_cols).transpose(1, 0, 2).reshape(16, D)[:8]

    x0 = x.reshape(T, D)
    W[w_in0] = whole(w_first)

    def first_w_out(act):
        W[w_out0] = whole(land((s_0, r_0), fly_0[2:], 2, act, "g0_out")[0])
        return W[w_out0]

    x1, s_f1a = _ffn_fwd(x0, ffn1_pre_g[0], ffn1_post_g[0], W[w_in0], first_w_out, "l0f1")
    arrive(1, x1)
    w_o_conv = W["conv_w_out", 0].reshape(D, D)
    xn_c, bch = _norm_mm_in(x1, mix_pre_g[0], W["conv_w_in", 0], "conv")
    z_c = _conv_fwd(bch, k_taps, Bl, S)
    m_c, x2 = _mm_out_post(z_c, w_o_conv, x1, mix_post_g[0], 1.0, "conv_out")
    arrive(2, x2)
    x3, s_f2a = _ffn_fwd(x2, ffn2_pre_g[0], ffn2_post_g[0], W["ffn2_w_in", 0], W["ffn2_w_out", 0], "l0f2")

    arrive(3, x3)
    kv_full = jnp.concatenate([W["kv_w", 0][s, :, :kvc] for s in range(N_CHIP)], axis=1)
    kv_full = jnp.pad(kv_full, ((0, 0), (0, kv_all - kv_full.shape[1])))
    xn_kv = _rms_fwd(x3, kv_g, "kv")
    kvact = _mm_nn(xn_kv, kv_full[:, :2 * D], MM_DTYPE, "kv")
    pf = _mm_nn(xn_kv, kv_full[:, 2 * D:], F32, "forget")
    cum = _forget_fwd(pf, fb, Bl, S)
    bq = min(S, ATT_BLOCK)
    c3 = cum.reshape(Bl, S, LANES)[:, :, :H].transpose(0, 2, 1)
    c_col = jnp.broadcast_to(c3[..., None], (Bl, H, S, LANES))
    c_row = c3.reshape(Bl, H, S // bq, 1, bq)

    x4, s_f1b = _ffn_fwd(x3, ffn1_pre_g[1], ffn1_post_g[1], W["ffn1_w_in", 1], W["ffn1_w_out", 1], "l1f1")
    arrive(4, x4)
    w_o_attn = W["attn_w_o", 0].reshape(D, D)
    xn_a, qg = _norm_mm_in(x4, mix_pre_g[1], W["attn_w_qg", 0], "qg")
    o, lse, z_a = _attn_fwd(qg, kvact, c_col, c_row, Bl, S, D)
    m_a, x5 = _mm_out_post(z_a, w_o_attn, x4, mix_post_g[1], 1.0, "attn_out")
    x6, s_f2b = _ffn_fwd(x5, ffn2_pre_g[1], ffn2_post_g[1], W["ffn2_w_in", 1], W["ffn2_w_out", 1], "l1f2")

    dy, loss_local = _loss_grad(x6, loss_target.reshape(T, D))

    G = {}
    dx5, dg_f2pre_1, dg_f2post_1, G["ffn2_w_in", 1], G["ffn2_w_out", 1] = _ffn_bwd(
        dy, s_f2b, ffn2_pre_g[1], ffn2_post_g[1], W["ffn2_w_in", 1], W["ffn2_w_out", 1], "l1f2")
    dm_a, dg_mixpost_1, dz_a = _post_bwd_mm(dx5, m_a, mix_post_g[1], 1.0, w_o_attn, "attn_out")
    G["attn_w_o", 0] = _mm_tn_out(z_a, dm_a, "attn_out")
    dq, dk, dv, dcr = _attn_bwd(qg, kvact, dz_a, lse, c_col, c_row, Bl, S, D)
    dqg = _gate_bwd(dz_a, qg, o, dq)
    G["attn_w_qg", 0] = _mm_tn_in(xn_a, dqg, "qg")
    dx4, dg_mixpre_1 = _mm_nt_pre(dqg, W["attn_w_qg", 0], dx5, x4, mix_pre_g[1], "qg")
    dx3, dg_f1pre_1, dg_f1post_1, G["ffn1_w_in", 1], G["ffn1_w_out", 1] = _ffn_bwd(
        dx4, s_f1b, ffn1_pre_g[1], ffn1_post_g[1], W["ffn1_w_in", 1], W["ffn1_w_out", 1], "l1f1")

    dcum = jnp.pad(dcr.reshape(Bl, H, S).transpose(0, 2, 1), ((0, 0), (0, 0), (0, LANES - H))).reshape(T, LANES)
    dpf, dfb = _forget_bwd(dcum, pf, fb, Bl, S)
    dp = jnp.concatenate([dk.astype(MM_DTYPE), dv.astype(MM_DTYPE), dpf], axis=1)
    G_kv_full = _mm_tn(xn_kv, dp, "kv")
    G["kv_w", 0] = jnp.stack([jnp.pad(G_kv_full[:, s * kvc:(s + 1) * kvc], ((0, 0), (0, kvp - kvc))) for s in range(N_CHIP)])
    dx3, dg_kv = _mm_nt_pre(dp, kv_full, dx3, x3, kv_g, "kv")

    def halves_of(keys):
        return [G[k].reshape(N_CHIP, 2, G[k].shape[1] // 2, G[k].shape[2]) for k in keys]

    def pair_adds(keys, grads, recvs):
        wires, owns = [], []
        for k, g, r in zip(keys, grads, recvs):
            w, own = _rs_pair_add(g, r, place, f"{k[0]}{k[1]}")
            wires.append(w)
            owns.append(own)
        return wires, owns

    def chip_start(wires, tag, extra=()):
        lands = [lax.empty((3,) + w.shape[1:], w.dtype) for w in wires]
        lands += [jnp.zeros((N_DEV,) + e.shape, e.dtype) for e in extra]
        return _exchange_start(list(wires) + list(extra), lands, _chip_plan, 3 * len(wires) + (N_DEV - 1) * len(extra), tag)

    late = groups[2] + groups[1]
    last = groups[0]
    grads_2 = halves_of(second)
    p_sems, p_semr, grads_2, sib_2, token = _exchange_start(
        grads_2, [lax.empty((N_CHIP,) + g.shape[2:], g.dtype) for g in grads_2], _pair_plan, len(grads_2), "pair_second")

    dx2, dg_f2pre_0, dg_f2post_0, G["ffn2_w_in", 0], G["ffn2_w_out", 0] = _ffn_bwd(
        dx3, s_f2a, ffn2_pre_g[0], ffn2_post_g[0] + token[0, :1], W["ffn2_w_in", 0], W["ffn2_w_out", 0], "l0f2")
    grads_2, sib_2 = _exchange_wait(p_sems, p_semr, grads_2, sib_2, _pair_plan, dx2, "pair_second")
    wires_2, owns_2 = pair_adds(second, grads_2, sib_2)
    c_2 = chip_start(wires_2, "chip_second")
    dm_c, dg_mixpost_0, dz_c = _post_bwd_mm(dx2, m_c, mix_post_g[0] + c_2[4][0, :1], 1.0, w_o_conv, "conv_out")
    G["conv_w_out", 0] = _mm_tn_out(z_c, dm_c, "conv_out")
    db, dcg, dhh, dk_taps = _conv_bwd(bch, dz_c, k_taps, Bl, S)
    dbch = jnp.concatenate([db, dcg, dhh], axis=1)
    G["conv_w_in", 0] = _mm_tn_in(xn_c, dbch, "conv")
    dx1, dg_mixpre_0 = _mm_nt_pre(dbch, W["conv_w_in", 0], dx2, x1, mix_pre_g[0], "conv")
    def pair_start(keys, tag):
        grads = halves_of(keys)
        lands = [lax.empty((N_CHIP,) + g.shape[2:], g.dtype) for g in grads]
        return _exchange_start(grads, lands, _pair_plan, len(grads), tag)

    p_l = pair_start(late, "pair_late")
    late_done = {}

    def late_leg(dhgu):
        grads_l, sib_l = _exchange_wait(*p_l[:4], _pair_plan, dhgu, "pair_late")
        late_done["wires"], late_done["owns"] = pair_adds(late, grads_l, sib_l)
        late_done["chip"] = chip_start(late_done["wires"], "chip_late")
        return late_done["chip"][4]

    dx0, dg_f1pre_0, dg_f1post_0, G["ffn1_w_in", 0], G["ffn1_w_out", 0] = _ffn_bwd(
        dx1, s_f1a, ffn1_pre_g[0], ffn1_post_g[0] + p_l[4][0, :1], W["ffn1_w_in", 0], W["ffn1_w_out", 0], "l0f1",
        between=late_leg)
    grad_x = dx0.reshape(Bl, S, D)
    owns_l, c_l = late_done["owns"], late_done["chip"]

    p_1 = pair_start(last, "pair_last")
    _, recvs_2 = _exchange_wait(*c_2[:4], _chip_plan, p_1[4], "chip_second")
    _, recvs_l = _exchange_wait(*c_l[:4], _chip_plan, p_1[4], "chip_late")
    partial = {}

    def chip_adds(keys, owns, recvs):
        for (name, l), own, rcv in zip(keys, owns, recvs):
            partial[name] = _rs_chip_add(own, rcv, place, l, shards[name].shape[0], partial.get(name), f"{name}{l}")
        return partial[name]

    added = chip_adds(late + second, owns_l + owns_2, recvs_l + recvs_2)
    grads_1, sib_1 = _exchange_wait(*p_1[:4], _pair_plan, added, "pair_last")
    wires_1, owns_1 = pair_adds(last, grads_1, sib_1)

    def row(v):
        return jnp.pad(v.reshape(-1), (0, D - v.size)).reshape(1, D)

    small_parts = [dg_f1pre_0, dg_f1pre_1, dg_f1post_0, dg_f1post_1, dg_mixpre_0, dg_mixpre_1, dg_mixpost_0, dg_mixpost_1,
                   dg_f2pre_0, dg_f2pre_1, dg_f2post_0, dg_f2post_1, dg_kv, row(dfb[0, :H]), dk_taps[:3],
                   jnp.full((1, D), loss_local)]
    small = jnp.concatenate(small_parts, axis=0)
    small = jnp.pad(small, ((0, SMALL_ROWS - small.shape[0]), (0, 0)))
    c_1 = chip_start(wires_1, "chip_last", extra=[small])
    res = {}

    def adamw(names, reduced, after):
        for k, red in zip(names, reduced):
            w, m, v = given[k]
            g2 = red.reshape(-1, red.shape[-1])
            if k == "kv_w":
                g2 = g2[:, :kvc]
            flat = lambda a: a.reshape(-1, a.shape[-1])
            go, d, mn, vn = _adamw(flat(w), g2, flat(m), flat(v), k, after=after)
            res[k] = tuple(a.reshape(w.shape) for a in (go, d, mn, vn))
        return d

    early = [k for k in partial if (k, 0) not in last]
    done = adamw(early, _rs_pair_share([partial[k] for k in early], "early"), c_1[4])
    _, recvs_1 = _exchange_wait(*c_1[:4], _chip_plan, done, "chip_last")
    chip_adds(last, owns_1, recvs_1[:-1])
    rest = [k for k, _ in last]
    adamw(rest, _rs_pair_share([partial[k] for k in rest], "last"), None)
    gsum = _sum_devices(recvs_1[-1], small, place)
    loss = gsum[17, 0]

    small_names = ["ffn1_pre_g", "ffn1_post_g", "mix_pre_g", "mix_post_g", "ffn2_pre_g", "ffn2_post_g"]
    small_given = dict(ffn1_pre_g=(ffn1_pre_g, m_ffn1_pre_g, v_ffn1_pre_g), ffn1_post_g=(ffn1_post_g, m_ffn1_post_g, v_ffn1_post_g),
                       mix_pre_g=(mix_pre_g, m_mix_pre_g, v_mix_pre_g), mix_post_g=(mix_post_g, m_mix_post_g, v_mix_post_g),
                       ffn2_pre_g=(ffn2_pre_g, m_ffn2_pre_g, v_ffn2_pre_g), ffn2_post_g=(ffn2_post_g, m_ffn2_post_g, v_ffn2_post_g))

    def pack(idx):
        rows_ = [small_given[k][idx] for k in small_names]
        rows_ += [row((kv_g, m_kv_g, v_kv_g)[idx]), row((forget_b, m_forget_b, v_forget_b)[idx])]
        rows_.append(jnp.pad((conv_k, m_conv_k, v_conv_k)[idx][0], ((0, 0), (0, D - dk_cols))))
        a = jnp.concatenate(rows_, axis=0)
        return jnp.pad(a, ((0, SMALL_ROWS - a.shape[0]), (0, 0)))

    g_taps = lax.dynamic_slice_in_dim(gsum[14:17], chip * dk_cols, dk_cols, axis=1)
    g_small = jnp.concatenate([gsum[:14], jnp.pad(g_taps, ((0, 0), (0, D - dk_cols))), gsum[17:]], axis=0)
    g_small, d_s, m_s, v_s = _adamw(pack(0), g_small, pack(1), pack(2), "small")
    for i, k in enumerate(small_names):
        res[k] = tuple(a[2 * i:2 * i + 2] for a in (g_small, d_s, m_s, v_s))
    res["kv_g"] = tuple(a[12] for a in (g_small, d_s, m_s, v_s))
    res["forget_b"] = tuple(a[13, :H] for a in (g_small, d_s, m_s, v_s))
    res["conv_k"] = tuple(a[14:17, :dk_cols][None] for a in (g_small, d_s, m_s, v_s))

    order = ["ffn1_pre_g", "ffn1_post_g", "ffn1_w_in", "ffn1_w_out", "mix_pre_g", "mix_post_g", "ffn2_pre_g", "ffn2_post_g",
             "ffn2_w_in", "ffn2_w_out", "conv_w_in", "conv_k", "conv_w_out", "kv_g", "kv_w", "forget_b", "attn_w_qg", "attn_w_o"]
    out = [loss, grad_x]
    for idx in range(4):
        out += [res[k][idx] for k in order]
    return tuple(out)
```

```python
import math

import jax
import jax.numpy as jnp
from jax import lax
from jax.experimental import pallas as pl
from jax.experimental.pallas import tpu as pltpu

F32 = jnp.float32
MM_DTYPE = jnp.bfloat16
WIRE_DTYPE = jnp.bfloat16

RMS_EPS = 1e-6
ADAM_LR = 0.001
ADAM_B1 = 0.9
ADAM_B2 = 0.999
ADAM_EPS = 1e-08
ADAM_WD = 0.01
ADAM_STEP = 10

HEAD_DIM = 64
LANES = 128
N_CHIP = 4
N_DEV = 8
ROW_TILE = 256
MM_TILE = 512
FUSED_TILE = 512
TN_TILE = 2048
ATT_BLOCK = 512
SMALL_ROWS = 24
VMEM_LIMIT = 56 * 1024 * 1024
MESH = pl.DeviceIdType.MESH
ANY = pl.BlockSpec(memory_space=pl.ANY)

NT = (((1,), (1,)), ((), ()))
TN = (((0,), (0,)), ((), ()))


def _tile(n, pref):
    if n <= pref:
        return n
    t = pref - pref % 16
    while n % t:
        t -= 16
    return t


def _params():
    return pltpu.CompilerParams(vmem_limit_bytes=VMEM_LIMIT)


def _sds(shape, dtype):
    return jax.ShapeDtypeStruct(shape, dtype)


def _rows(tm, c):
    return pl.BlockSpec((tm, c), lambda i: (i, 0))


def _whole(shape):
    return pl.BlockSpec(shape, lambda *_: (0,) * len(shape))


def _resident(shape):
    return pl.BlockSpec(shape, lambda *_: (0,) * len(shape), pipeline_mode=pl.Buffered(1))


def _rms_fwd(x, g, tag):
    T, D = x.shape
    tm = _tile(T, ROW_TILE)

    def body(x_ref, g_ref, o_ref):
        xv = x_ref[...]
        r = lax.rsqrt(jnp.mean(xv * xv, axis=-1, keepdims=True) + RMS_EPS)
        o_ref[...] = (xv * r * g_ref[...]).astype(o_ref.dtype)

    return pl.pallas_call(
        body, name=f"rms_fwd_{tag}", grid=(T // tm,),
        in_specs=[_rows(tm, D), _whole((1, D))], out_specs=_rows(tm, D),
        out_shape=_sds((T, D), MM_DTYPE), compiler_params=_params())(x, g.reshape(1, D))


def _accumulate(ref, part, first):
    @pl.when(first)
    def _():
        ref[...] = part

    @pl.when(jnp.logical_not(first))
    def _():
        ref[...] += part


def _loss_grad(y, tgt):
    T, D = y.shape
    tm = _tile(T, ROW_TILE)

    def body(y_ref, t_ref, dy_ref, l_ref):
        e = y_ref[...] - t_ref[...]
        row = jnp.mean(e * e, axis=-1, keepdims=True)
        part = jnp.broadcast_to(jnp.sum(row, axis=0, keepdims=True), (8, LANES))
        _accumulate(l_ref, part, pl.program_id(0) == 0)
        dy_ref[...] = e * (1.0 / D)

    dy, lsum = pl.pallas_call(
        body, name="loss_grad", grid=(T // tm,),
        in_specs=[_rows(tm, D), _rows(tm, D)], out_specs=[_rows(tm, D), _whole((8, LANES))],
        out_shape=[_sds((T, D), F32), _sds((8, LANES), F32)], compiler_params=_params())(y, tgt)
    return dy, 0.5 * lsum[0, 0]


def _shift_down(u, d, rows):
    return jnp.where(rows >= d, pltpu.roll(u, d, 0), 0.0)


def _shift_up(u, d, rows, S):
    return jnp.where(rows < S - d, pltpu.roll(u, S - d, 0), 0.0)


def _conv_fwd(bch, k8, Bl, S):
    T, D3 = bch.shape
    D = D3 // 3
    dc = min(D, 2 * LANES)
    nd = D // dc

    def body(b_ref, c_ref, h_ref, k_ref, z_ref):
        rows = lax.broadcasted_iota(jnp.int32, (S, 1), 0)
        u = c_ref[...].astype(F32) * h_ref[...].astype(F32)
        y = k_ref[2:3, :] * u + k_ref[1:2, :] * _shift_down(u, 1, rows) + k_ref[0:1, :] * _shift_down(u, 2, rows)
        z_ref[...] = (b_ref[...].astype(F32) * y).astype(z_ref.dtype)

    return pl.pallas_call(
        body, name="conv_fwd", grid=(Bl, nd),
        in_specs=[pl.BlockSpec((S, dc), lambda b, j: (b, j)),
                  pl.BlockSpec((S, dc), lambda b, j: (b, nd + j)),
                  pl.BlockSpec((S, dc), lambda b, j: (b, 2 * nd + j)),
                  pl.BlockSpec((8, dc), lambda b, j: (0, j))],
        out_specs=pl.BlockSpec((S, dc), lambda b, j: (b, j)),
        out_shape=_sds((T, D), MM_DTYPE), compiler_params=_params())(bch, bch, bch, k8)


def _conv_bwd(bch, dz, k8, Bl, S):
    T, D3 = bch.shape
    D = D3 // 3
    dc = min(D, 2 * LANES)
    nd = D // dc

    def body(b_ref, c_ref, h_ref, dz_ref, k_ref, db_ref, dc_ref, dh_ref, dk_ref):
        rows = lax.broadcasted_iota(jnp.int32, (S, 1), 0)
        bv = b_ref[...].astype(F32)
        cv = c_ref[...].astype(F32)
        hv = h_ref[...].astype(F32)
        dzv = dz_ref[...].astype(F32)
        u = cv * hv
        u1 = _shift_down(u, 1, rows)
        u2 = _shift_down(u, 2, rows)
        y = k_ref[2:3, :] * u + k_ref[1:2, :] * u1 + k_ref[0:1, :] * u2
        db_ref[...] = (dzv * y).astype(db_ref.dtype)
        dy = dzv * bv
        du = k_ref[2:3, :] * dy + k_ref[1:2, :] * _shift_up(dy, 1, rows, S) + k_ref[0:1, :] * _shift_up(dy, 2, rows, S)
        dc_ref[...] = (du * hv).astype(dc_ref.dtype)
        dh_ref[...] = (du * cv).astype(dh_ref.dtype)

        @pl.when(pl.program_id(1) == 0)
        def _():
            dk_ref[...] = jnp.zeros_like(dk_ref)

        dk_ref[0:1, :] += jnp.sum(dy * u2, axis=0, keepdims=True)
        dk_ref[1:2, :] += jnp.sum(dy * u1, axis=0, keepdims=True)
        dk_ref[2:3, :] += jnp.sum(dy * u, axis=0, keepdims=True)

    seq = lambda off: pl.BlockSpec((S, dc), lambda j, b: (b, off + j))
    return pl.pallas_call(
        body, name="conv_bwd", grid=(nd, Bl),
        in_specs=[seq(0), seq(nd), seq(2 * nd), seq(0), pl.BlockSpec((8, dc), lambda j, b: (0, j))],
        out_specs=[seq(0), seq(0), seq(0), pl.BlockSpec((8, dc), lambda j, b: (0, j))],
        out_shape=[_sds((T, D), MM_DTYPE)] * 3 + [_sds((8, D), F32)],
        compiler_params=_params())(bch, bch, bch, dz, k8)


def _forget_fwd(pf, fb, Bl, S):
    T = pf.shape[0]

    def body(p_ref, fb_ref, c_ref):
        rows = lax.broadcasted_iota(jnp.int32, (S, 1), 0)
        z = p_ref[...] + fb_ref[...]
        acc = jnp.minimum(z, 0.0) - jnp.log1p(jnp.exp(-jnp.abs(z)))
        d = 1
        while d < S:
            acc = acc + _shift_down(acc, d, rows)
            d *= 2
        c_ref[...] = acc

    return pl.pallas_call(
        body, name="forget_fwd", grid=(Bl,),
        in_specs=[_rows(S, LANES), _whole((1, LANES))], out_specs=_rows(S, LANES),
        out_shape=_sds((T, LANES), F32), compiler_params=_params())(pf, fb)


def _forget_bwd(dc, pf, fb, Bl, S):
    T = pf.shape[0]

    def body(dc_ref, p_ref, fb_ref, df_ref, dfb_ref):
        rows = lax.broadcasted_iota(jnp.int32, (S, 1), 0)
        acc = dc_ref[...]
        d = 1
        while d < S:
            acc = acc + _shift_up(acc, d, rows, S)
            d *= 2
        df = acc * jax.nn.sigmoid(-(p_ref[...] + fb_ref[...]))
        df_ref[...] = df.astype(df_ref.dtype)
        _accumulate(dfb_ref, jnp.sum(df, axis=0, keepdims=True), pl.program_id(0) == 0)

    return pl.pallas_call(
        body, name="forget_bwd", grid=(Bl,),
        in_specs=[_rows(S, LANES), _rows(S, LANES), _whole((1, LANES))],
        out_specs=[_rows(S, LANES), _whole((1, LANES))],
        out_shape=[_sds((T, LANES), MM_DTYPE), _sds((1, LANES), F32)],
        compiler_params=_params())(dc, pf, fb)


def _head_mask(h):
    lane = lax.broadcasted_iota(jnp.int32, (1, LANES), 1)
    return (lane >= h * HEAD_DIM) & (lane < (h + 1) * HEAD_DIM)


def _attn_fwd(qg, kv, c_col, c_row, Bl, S, D):
    T = Bl * S
    H = D // HEAD_DIM
    HP = D // LANES
    bq = min(S, ATT_BLOCK)
    nq = S // bq
    scale = 1.0 / math.sqrt(HEAD_DIM)

    def body(q_ref, g_ref, k_ref, v_ref, cc_ref, cr_ref, o_ref, lse_ref, z_ref):
        i = pl.program_id(2)
        q2 = q_ref[...]
        qh = [q2 * (_head_mask(h).astype(F32) * scale).astype(q2.dtype) for h in range(2)]
        cc = [cc_ref[h][:, :1] for h in range(2)]
        diag = lax.broadcasted_iota(jnp.int32, (1, bq), 1) <= lax.broadcasted_iota(jnp.int32, (bq, 1), 0)

        def block(j, carry, on_diagonal):
            off = pl.multiple_of(j * bq, bq)
            kj = k_ref[pl.ds(off, bq), :]
            vj = v_ref[pl.ds(off, bq), :]
            new = []
            for h in range(2):
                m, l, acc = carry[h]
                s = lax.dot_general(qh[h], kj, NT, preferred_element_type=F32) + cc[h] - cr_ref[h, j]
                if on_diagonal:
                    s = jnp.where(diag, s, -jnp.inf)
                m_new = jnp.maximum(m, jnp.max(s, axis=1, keepdims=True))
                p = jnp.exp(s - m_new)
                a = jnp.exp(m - m_new)
                l = a * l + jnp.sum(p, axis=1, keepdims=True)
                acc = a * acc + jnp.dot(p.astype(MM_DTYPE), vj, preferred_element_type=F32)
                new.append((m_new, l, acc))
            return tuple(new)

        one = (jnp.full((bq, 1), -jnp.inf, F32), jnp.zeros((bq, 1), F32), jnp.zeros((bq, LANES), F32))
        carry = lax.fori_loop(0, i, lambda j, c: block(j, c, False), (one, one))
        carry = block(i, carry, True)
        outs = []
        for h in range(2):
            m, l, acc = carry[h]
            outs.append(acc / l)
            lse_ref[h] = jnp.broadcast_to(m + jnp.log(l), (bq, LANES))
        o2 = jnp.where(_head_mask(0), outs[0], outs[1])
        o_ref[...] = o2
        z_ref[...] = (jax.nn.sigmoid(g_ref[...].astype(F32)) * o2).astype(z_ref.dtype)

    return pl.pallas_call(
        body, name="attn_fwd", grid=(Bl, HP, nq),
        in_specs=[pl.BlockSpec((bq, LANES), lambda b, hp, i: (b * nq + i, hp)),
                  pl.BlockSpec((bq, LANES), lambda b, hp, i: (b * nq + i, HP + hp)),
                  pl.BlockSpec((S, LANES), lambda b, hp, i: (b, hp)),
                  pl.BlockSpec((S, LANES), lambda b, hp, i: (b, HP + hp)),
                  pl.BlockSpec((None, 2, bq, LANES), lambda b, hp, i: (b, hp, i, 0)),
                  pl.BlockSpec((None, 2, nq, 1, bq), lambda b, hp, i: (b, hp, 0, 0, 0))],
        out_specs=[pl.BlockSpec((bq, LANES), lambda b, hp, i: (b * nq + i, hp)),
                   pl.BlockSpec((None, 2, bq, LANES), lambda b, hp, i: (b, hp, i, 0)),
                   pl.BlockSpec((bq, LANES), lambda b, hp, i: (b * nq + i, hp))],
        out_shape=[_sds((T, D), F32), _sds((Bl, H, S, LANES), F32), _sds((T, D), MM_DTYPE)],
        compiler_params=_params())(qg, qg, kv, kv, c_col, c_row)


def _attn_bwd(qg, kv, dz, lse, c_col, c_row, Bl, S, D):
    T = Bl * S
    H = D // HEAD_DIM
    HP = D // LANES
    bq = min(S, ATT_BLOCK)
    nq = S // bq
    scale = 1.0 / math.sqrt(HEAD_DIM)

    def body(q_ref, g_ref, k_ref, v_ref, dz_ref, lse_ref, cc_ref, cr_ref, dq_ref, dk_ref, dv_ref, dcr_ref, p_sc, dp_sc):
        i = pl.program_id(2)

        @pl.when(i == 0)
        def _():
            dk_ref[...] = jnp.zeros_like(dk_ref)
            dv_ref[...] = jnp.zeros_like(dv_ref)
            dcr_ref[...] = jnp.zeros_like(dcr_ref)

        q2 = q_ref[...]
        do2 = (dz_ref[...].astype(F32) * jax.nn.sigmoid(g_ref[...].astype(F32))).astype(MM_DTYPE)
        masks = [_head_mask(h).astype(F32) for h in range(2)]
        qh = [q2 * (masks[h] * scale).astype(q2.dtype) for h in range(2)]
        doh = [do2 * masks[h].astype(do2.dtype) for h in range(2)]
        cc = [cc_ref[h][:, :1] for h in range(2)]
        lse = [lse_ref[h][:, :1] for h in range(2)]
        diag = lax.broadcasted_iota(jnp.int32, (1, bq), 1) <= lax.broadcasted_iota(jnp.int32, (bq, 1), 0)

        def sweep1(j, delta, on_diagonal):
            off = pl.multiple_of(j * bq, bq)
            kj = k_ref[pl.ds(off, bq), :]
            vj = v_ref[pl.ds(off, bq), :]
            new = []
            dv = None
            for h in range(2):
                s = lax.dot_general(qh[h], kj, NT, preferred_element_type=F32) + cc[h] - cr_ref[h, j]
                if on_diagonal:
                    s = jnp.where(diag, s, -jnp.inf)
                p = jnp.exp(s - lse[h])
                dp = lax.dot_general(doh[h], vj, NT, preferred_element_type=F32)
                p_sc[h, j] = p
                dp_sc[h, j] = dp
                part = lax.dot_general(p.astype(MM_DTYPE), doh[h], TN, preferred_element_type=F32)
                dv = part if dv is None else dv + part
                new.append(delta[h] + jnp.sum(p * dp, axis=1, keepdims=True))
            dv_ref[pl.ds(off, bq), :] += dv
            return tuple(new)

        zero = jnp.zeros((bq, 1), F32)
        delta = lax.fori_loop(0, i, lambda j, d: sweep1(j, d, False), (zero, zero))
        delta = sweep1(i, delta, True)

        def sweep2(j, dq):
            off = pl.multiple_of(j * bq, bq)
            kj = k_ref[pl.ds(off, bq), :]
            dk = None
            for h in range(2):
                ds = p_sc[h, j] * (dp_sc[h, j] - delta[h])
                dcr_ref[h, j] -= jnp.sum(ds, axis=0, keepdims=True)
                dsb = ds.astype(MM_DTYPE)
                dq = dq + jnp.dot(dsb, kj * (masks[h] * scale).astype(kj.dtype), preferred_element_type=F32)
                part = lax.dot_general(dsb, qh[h], TN, preferred_element_type=F32)
                dk = part if dk is None else dk + part
            dk_ref[pl.ds(off, bq), :] += dk
            return dq

        dq_ref[...] = lax.fori_loop(0, i + 1, sweep2, jnp.zeros((bq, LANES), F32))

    blk = lambda col: pl.BlockSpec((bq, LANES), lambda b, hp, i: (b * nq + i, col(hp)))
    seq = lambda col: pl.BlockSpec((S, LANES), lambda b, hp, i: (b, col(hp)))
    per_head = pl.BlockSpec((None, 2, bq, LANES), lambda b, hp, i: (b, hp, i, 0))
    rows = pl.BlockSpec((None, 2, nq, 1, bq), lambda b, hp, i: (b, hp, 0, 0, 0))
    return pl.pallas_call(
        body, name="attn_bwd", grid=(Bl, HP, nq),
        in_specs=[blk(lambda hp: hp), blk(lambda hp: HP + hp), seq(lambda hp: hp), seq(lambda hp: HP + hp),
                  blk(lambda hp: hp), per_head, per_head, rows],
        out_specs=[blk(lambda hp: hp), seq(lambda hp: hp), seq(lambda hp: hp), rows],
        out_shape=[_sds((T, D), F32), _sds((T, D), F32), _sds((T, D), F32), _sds((Bl, H, nq, 1, bq), F32)],
        scratch_shapes=[pltpu.VMEM((2, nq, bq, bq), F32), pltpu.VMEM((2, nq, bq, bq), F32)],
        compiler_params=_params())(qg, qg, kv, kv, dz, lse, c_col, c_row)


def _gate_bwd(dz, qg, o, dq):
    T, D = dz.shape
    tm = _tile(T, ROW_TILE)

    def body(dz_ref, g_ref, o_ref, dq_ref, out_ref):
        g = g_ref[...].astype(F32)
        sg = jax.nn.sigmoid(g)
        out_ref[:, :D] = dq_ref[...].astype(out_ref.dtype)
        out_ref[:, D:] = (dz_ref[...].astype(F32) * o_ref[...] * sg * (1.0 - sg)).astype(out_ref.dtype)

    return pl.pallas_call(
        body, name="gate_bwd", grid=(T // tm,),
        in_specs=[_rows(tm, D), pl.BlockSpec((tm, D), lambda i: (i, 1)), _rows(tm, D), _rows(tm, D)],
        out_specs=_rows(tm, 2 * D), out_shape=_sds((T, 2 * D), MM_DTYPE),
        compiler_params=_params())(dz, qg, o, dq)


def _mm_nn(a, b, out_dtype, tag):
    T, K = a.shape
    N = b.shape[1]
    tm = _tile(T, MM_TILE)

    def body(a_ref, b_ref, o_ref):
        o_ref[...] = jnp.dot(a_ref[...], b_ref[...], preferred_element_type=F32).astype(o_ref.dtype)

    return pl.pallas_call(
        body, name=f"mm_nn_{tag}", grid=(T // tm,),
        in_specs=[_rows(tm, K), _whole((K, N))], out_specs=_rows(tm, N),
        out_shape=_sds((T, N), out_dtype), compiler_params=_params())(a, b)


def _mm_tn_in(a, dy, tag, after=None):
    T, K = a.shape
    n = dy.shape[1] // N_CHIP
    tt = _tile(T, TN_TILE)
    extra = [] if after is None else [after]

    def body(a_ref, d_ref, *rest):
        part = lax.dot_general(a_ref[...], d_ref[...], TN, preferred_element_type=F32)
        _accumulate(rest[-1], part, pl.program_id(1) == 0)

    return pl.pallas_call(
        body, name=f"mm_tn_in_{tag}", grid=(N_CHIP, T // tt),
        in_specs=[pl.BlockSpec((tt, K), lambda s, t: (t, 0)), pl.BlockSpec((tt, n), lambda s, t: (t, s))]
        + [ANY] * len(extra),
        out_specs=pl.BlockSpec((None, K, n), lambda s, t: (s, 0, 0)), out_shape=_sds((N_CHIP, K, n), F32),
        compiler_params=_params())(a, dy, *extra)


def _mm_tn_out(act, dh, tag, after=None):
    T, R4 = act.shape
    D = dh.shape[1]
    r = R4 // N_CHIP
    g = 1 if r % LANES == 0 else 2
    tt = _tile(T, TN_TILE)
    extra = [] if after is None else [after]

    def body(a_ref, d_ref, *rest):
        o_ref = rest[-1]
        part = lax.dot_general(a_ref[...], d_ref[...], TN, preferred_element_type=F32)
        first = pl.program_id(1) == 0
        for q in range(g):
            _accumulate(o_ref.at[q], part[q * r:(q + 1) * r], first)

    return pl.pallas_call(
        body, name=f"mm_tn_out_{tag}", grid=(N_CHIP // g, T // tt),
        in_specs=[pl.BlockSpec((tt, g * r), lambda s, t: (t, s)), pl.BlockSpec((tt, D), lambda s, t: (t, 0))]
        + [ANY] * len(extra),
        out_specs=pl.BlockSpec((g, r, D), lambda s, t: (s, 0, 0)), out_shape=_sds((N_CHIP, r, D), F32),
        compiler_params=_params())(act, dh, *extra)


def _mm_tn(a, b, tag):
    T, K = a.shape
    N = b.shape[1]
    tt = _tile(T, TN_TILE)

    def body(a_ref, b_ref, o_ref):
        part = lax.dot_general(a_ref[...], b_ref[...], TN, preferred_element_type=F32)
        _accumulate(o_ref, part, pl.program_id(0) == 0)

    return pl.pallas_call(
        body, name=f"mm_tn_{tag}", grid=(T // tt,),
        in_specs=[_rows(tt, K), _rows(tt, N)], out_specs=_whole((K, N)),
        out_shape=_sds((K, N), F32), compiler_params=_params())(a, b)


def _norm_mm_in(x, g, wg, tag, swiglu=False):
    T, D = x.shape
    n = wg.shape[-1]
    tm = _tile(T, FUSED_TILE)
    half = N_CHIP // 2

    def body(x_ref, g_ref, w_ref, xn_ref, y_ref, *rest):
        xv = x_ref[...]
        r = lax.rsqrt(jnp.mean(xv * xv, axis=-1, keepdims=True) + RMS_EPS)
        xn = (xv * r * g_ref[...]).astype(xn_ref.dtype)
        xn_ref[...] = xn

        def product(s):
            p = jnp.dot(xn, w_ref[s], preferred_element_type=F32)
            y_ref[:, s * n:(s + 1) * n] = p.astype(y_ref.dtype)
            return p

        if swiglu:
            for q in range(half):
                gate, up = product(q), product(half + q)
                rest[0][:, q * n:(q + 1) * n] = (gate * jax.nn.sigmoid(gate) * up).astype(rest[0].dtype)
        else:
            for s in range(N_CHIP):
                product(s)

    out_specs = [_rows(tm, D), _rows(tm, N_CHIP * n)]
    out_shape = [_sds((T, D), MM_DTYPE), _sds((T, N_CHIP * n), MM_DTYPE)]
    if swiglu:
        out_specs.append(_rows(tm, half * n))
        out_shape.append(_sds((T, half * n), MM_DTYPE))
    return pl.pallas_call(
        body, name=f"norm_mm_in_{tag}", grid=(T // tm,),
        in_specs=[_rows(tm, D), _whole((1, D)), _resident((N_CHIP, D, n))], out_specs=out_specs,
        out_shape=out_shape, compiler_params=_params())(x, g.reshape(1, D), wg)


def _mm_out_post(a, b, x, g, alpha, tag):
    T, K = a.shape
    D = b.shape[1]
    tm = _tile(T, FUSED_TILE)

    def body(a_ref, b_ref, x_ref, g_ref, h_ref, o_ref):
        hv = jnp.dot(a_ref[...], b_ref[...], preferred_element_type=F32)
        h_ref[...] = hv
        r = lax.rsqrt(jnp.mean(hv * hv, axis=-1, keepdims=True) + RMS_EPS)
        o_ref[...] = x_ref[...] + alpha * (hv * r * g_ref[...])

    return pl.pallas_call(
        body, name=f"mm_out_post_{tag}", grid=(T // tm,),
        in_specs=[_rows(tm, K), _resident((K, D)), _rows(tm, D), _whole((1, D))],
        out_specs=[_rows(tm, D), _rows(tm, D)], out_shape=[_sds((T, D), F32)] * 2,
        compiler_params=_params())(a, b, x, g.reshape(1, D))


def _post_bwd_mm(dx, h, g, alpha, b, tag, hgu=None):
    T, D = dx.shape
    K = b.shape[0]
    tm = _tile(T, FUSED_TILE)

    def body(dx_ref, h_ref, g_ref, b_ref, *rest):
        dh_ref, dg_ref, out_ref = rest[-3:]
        hv = h_ref[...]
        r = lax.rsqrt(jnp.mean(hv * hv, axis=-1, keepdims=True) + RMS_EPS)
        hh = hv * r
        dyn = alpha * dx_ref[...]
        _accumulate(dg_ref, jnp.sum(dyn * hh, axis=0, keepdims=True), pl.program_id(0) == 0)
        dhh = dyn * g_ref[...]
        dh = (r * (dhh - hh * jnp.mean(dhh * hh, axis=-1, keepdims=True))).astype(dh_ref.dtype)
        dh_ref[...] = dh
        da = lax.dot_general(dh, b_ref[...], NT, preferred_element_type=F32)
        if hgu is None:
            out_ref[...] = da.astype(out_ref.dtype)
        else:
            gate = rest[0][:, :K].astype(F32)
            up = rest[0][:, K:].astype(F32)
            sg = jax.nn.sigmoid(gate)
            out_ref[:, :K] = (da * up * sg * (1.0 + gate * (1.0 - sg))).astype(out_ref.dtype)
            out_ref[:, K:] = (da * gate * sg).astype(out_ref.dtype)

    in_specs = [_rows(tm, D), _rows(tm, D), _whole((1, D)), _resident((K, D))]
    args = [dx, h, g.reshape(1, D), b]
    wide = K
    if hgu is not None:
        wide = 2 * K
        in_specs.append(_rows(tm, wide))
        args.append(hgu)
    return pl.pallas_call(
        body, name=f"post_bwd_mm_{tag}", grid=(T // tm,), in_specs=in_specs,
        out_specs=[_rows(tm, D), _whole((1, D)), _rows(tm, wide)],
        out_shape=[_sds((T, D), MM_DTYPE), _sds((1, D), F32), _sds((T, wide), MM_DTYPE)],
        compiler_params=_params())(*args)


def _mm_nt_pre(dy, w, dres, x, g, tag):
    T, C = dy.shape
    D = x.shape[1]
    tm = _tile(T, FUSED_TILE)
    n = w.shape[-1]

    def body(dy_ref, w_ref, dres_ref, x_ref, g_ref, dx_ref, dg_ref):
        if w.ndim == 2:
            dn = lax.dot_general(dy_ref[...], w_ref[...], NT, preferred_element_type=F32)
        else:
            dn = None
            for s in range(N_CHIP):
                part = lax.dot_general(dy_ref[:, s * n:(s + 1) * n], w_ref[s], NT, preferred_element_type=F32)
                dn = part if dn is None else dn + part
        xv = x_ref[...]
        r = lax.rsqrt(jnp.mean(xv * xv, axis=-1, keepdims=True) + RMS_EPS)
        xh = xv * r
        _accumulate(dg_ref, jnp.sum(dn * xh, axis=0, keepdims=True), pl.program_id(0) == 0)
        dxh = dn * g_ref[...]
        dx_ref[...] = dres_ref[...] + r * (dxh - xh * jnp.mean(dxh * xh, axis=-1, keepdims=True))

    return pl.pallas_call(
        body, name=f"mm_nt_pre_{tag}", grid=(T // tm,),
        in_specs=[_rows(tm, C), _resident(w.shape), _rows(tm, D), _rows(tm, D), _whole((1, D))],
        out_specs=[_rows(tm, D), _whole((1, D))], out_shape=[_sds((T, D), F32), _sds((1, D), F32)],
        compiler_params=_params())(dy, w, dres, x, g.reshape(1, D))


def _adamw(w, g, m, v, tag, after=None):
    R, C = w.shape
    tr = _tile(R, ROW_TILE)
    extra = [] if after is None else [after]

    def body(w_ref, g_ref, m_ref, v_ref, *rest):
        go_ref, d_ref, mo_ref, vo_ref = rest[-4:]
        gv = g_ref[...]
        go_ref[...] = gv
        mn = ADAM_B1 * m_ref[...] + (1.0 - ADAM_B1) * gv
        vn = ADAM_B2 * v_ref[...] + (1.0 - ADAM_B2) * (gv * gv)
        m_hat = mn / (1.0 - ADAM_B1 ** ADAM_STEP)
        v_hat = vn / (1.0 - ADAM_B2 ** ADAM_STEP)
        d_ref[...] = -ADAM_LR * (m_hat / (jnp.sqrt(v_hat) + ADAM_EPS) + ADAM_WD * w_ref[...])
        mo_ref[...] = mn
        vo_ref[...] = vn

    return pl.pallas_call(
        body, name=f"adamw_{tag}", grid=(R // tr,),
        in_specs=[_rows(tr, C)] * 4 + [ANY] * len(extra), out_specs=[_rows(tr, C)] * 4,
        out_shape=[_sds((R, C), F32)] * 4, compiler_params=_params())(w, g, m, v, *extra)


def _sum_devices(gall, own, place):
    _, R, C = gall.shape

    def body(place_ref, g_ref, s_ref, o_ref):
        me = 2 * place_ref[1] + place_ref[0]
        acc = None
        for d in range(N_DEV):
            term = jnp.where(me == d, s_ref[...], g_ref[d])
            acc = term if acc is None else acc + term
        o_ref[...] = acc

    grid_spec = pltpu.PrefetchScalarGridSpec(
        num_scalar_prefetch=1, grid=(1,),
        in_specs=[pl.BlockSpec((N_DEV, R, C), lambda i, p: (0, 0, 0)), pl.BlockSpec((R, C), lambda i, p: (0, 0))],
        out_specs=pl.BlockSpec((R, C), lambda i, p: (0, 0)))
    return pl.pallas_call(
        body, name="sum_devices", grid_spec=grid_spec, out_shape=_sds((R, C), F32),
        compiler_params=_params())(place, gall, own)


HBM = pl.BlockSpec(memory_space=pltpu.HBM)
SEM = pl.BlockSpec(memory_space=pltpu.SEMAPHORE)
EFFECT = pltpu.SideEffectType.DATAFLOW_SIDE_EFFECTING


def _place():
    x, y, c = lax.axis_index("x"), lax.axis_index("y"), lax.axis_index("c")
    chips = ((1 - x, y), (x, 1 - y), (1 - x, 1 - y))
    return x, y, c, chips


def _remote(src, dst, send_sem, recv_sem, dev):
    return pltpu.make_async_remote_copy(src_ref=src, dst_ref=dst, send_sem=send_sem, recv_sem=recv_sem,
                                        device_id=dev, device_id_type=MESH)


def _in_hbm(a):
    return pltpu.with_memory_space_constraint(a, pltpu.HBM)


def _own_slot(w4, l, dtype, place, tag):
    _, _, r, col = w4.shape
    tr = _tile(r, 2 * ROW_TILE)

    def body(place_ref, x_ref, o_ref):
        o_ref[...] = x_ref[...].astype(o_ref.dtype)

    grid_spec = pltpu.PrefetchScalarGridSpec(
        num_scalar_prefetch=1, grid=(2, r // tr),
        in_specs=[pl.BlockSpec((None, None, tr, col), lambda h, i, p: (l, h, i, 0))],
        out_specs=pl.BlockSpec((None, None, tr, col), lambda h, i, p: (p[1], h, i, 0)))
    return pl.pallas_call(
        body, name=f"own_slot_{tag}", grid_spec=grid_spec, out_shape=_sds((N_CHIP, 2, r, col), dtype),
        compiler_params=_params())(place, w4)


def _gather_start(bufs, after, tag):
    n = len(bufs)

    def body(*refs):
        ins = refs[:n]
        s_sem, r_sem, token = refs[n + 1], refs[n + 2], refs[2 * n + 3]
        x, y, c, chips = _place()
        me = 2 * x + y
        for i in range(n):
            mine = ins[i].at[me, c]
            for j, (px, py) in enumerate(chips):
                _remote(mine, mine, s_sem.at[3 * i + j], r_sem.at[3 * i + j], (px, py, c)).start()
        token[...] = jnp.zeros_like(token)

    dma = pltpu.SemaphoreType.DMA
    res = pl.pallas_call(
        body, name=f"gather_start_{tag}", in_specs=[HBM] * n + [ANY],
        out_specs=[SEM, SEM] + [HBM] * n + [pl.BlockSpec(memory_space=pltpu.VMEM)],
        out_shape=[dma((3 * n,)), dma((3 * n,))] + [pltpu.HBM(b.shape, b.dtype) for b in bufs] + [_sds((8, LANES), F32)],
        input_output_aliases={i: i + 2 for i in range(n)},
        compiler_params=pltpu.CompilerParams(has_side_effects=EFFECT),
        )(*[_in_hbm(b) for b in bufs], after)
    return res[0], res[1], list(res[2:2 + n]), res[-1]


def _gather_pass(s_sem, r_sem, bufs, first, after, tag):
    n = len(bufs)

    def body(*refs):
        ins = refs[:n]
        a_s, a_r, b_s, b_r = refs[n], refs[n + 1], refs[n + 3], refs[n + 4]
        x, y, c, chips = _place()
        me = 2 * x + y
        sib = (x, y, 1 - c)
        for i in range(n):
            mine = ins[i].at[me, c]
            for j, (px, py) in enumerate(chips):
                k = 3 * (first + i) + j
                _remote(mine, mine, a_s.at[k], a_r.at[k], (px, py, c)).wait_send()
        for j, (px, py) in enumerate(chips):
            for i in range(n):
                k = 3 * (first + i) + j
                blk = ins[i].at[2 * px + py, c]
                _remote(blk, blk, a_s.at[k], a_r.at[k], (px, py, c)).wait_recv()
                _remote(blk, blk, b_s.at[3 * i + j], b_r.at[3 * i + j], sib).start()

    dma = pltpu.SemaphoreType.DMA
    res = pl.pallas_call(
        body, name=f"gather_pass_{tag}", in_specs=[HBM] * n + [SEM, SEM, ANY],
        out_specs=[SEM, SEM] + [HBM] * n,
        out_shape=[dma((3 * n,)), dma((3 * n,))] + [pltpu.HBM(b.shape, b.dtype) for b in bufs],
        input_output_aliases={i: i + 2 for i in range(n)},
        compiler_params=pltpu.CompilerParams(has_side_effects=EFFECT),
        )(*bufs, s_sem, r_sem, after)
    return res[0], res[1], list(res[2:])


def _gather_land(s_sem, r_sem, bufs, tag):
    n = len(bufs)

    def body(*refs):
        ins = refs[:n]
        b_s, b_r = refs[n], refs[n + 1]
        x, y, c, chips = _place()
        sib = (x, y, 1 - c)
        for j, (px, py) in enumerate(chips):
            for i in range(n):
                sent = ins[i].at[2 * px + py, c]
                got = ins[i].at[2 * px + py, 1 - c]
                _remote(sent, sent, b_s.at[3 * i + j], b_r.at[3 * i + j], sib).wait_send()
                _remote(got, got, b_s.at[3 * i + j], b_r.at[3 * i + j], sib).wait_recv()

    return list(pl.pallas_call(
        body, name=f"gather_land_{tag}", in_specs=[HBM] * n + [SEM, SEM], out_specs=[HBM] * n,
        out_shape=[pltpu.HBM(b.shape, b.dtype) for b in bufs],
        input_output_aliases={i: i for i in range(n)},
        compiler_params=pltpu.CompilerParams(has_side_effects=EFFECT),
        )(*bufs, s_sem, r_sem))


def _rs_pair_add(g, recv, place, tag):
    r, col = g.shape[-2:]
    tr = _tile(r, ROW_TILE)

    def body(place_ref, g_ref, r_ref, wire_ref, own_ref):
        tot = g_ref[...] + r_ref[...]
        wire_ref[...] = tot.astype(wire_ref.dtype)

        @pl.when(pl.program_id(1) == place_ref[1])
        def _():
            own_ref[...] = tot

    grid_spec = pltpu.PrefetchScalarGridSpec(
        num_scalar_prefetch=1, grid=(r // tr, N_CHIP),
        in_specs=[pl.BlockSpec((None, None, tr, col), lambda i, s, p: (s, p[0], i, 0)),
                  pl.BlockSpec((None, tr, col), lambda i, s, p: (s, i, 0))],
        out_specs=[pl.BlockSpec((None, tr, col), lambda i, s, p: (s, i, 0)),
                   pl.BlockSpec((tr, col), lambda i, s, p: (i, 0))])
    return pl.pallas_call(
        body, name=f"rs_pair_add_{tag}", grid_spec=grid_spec,
        out_shape=[_sds((N_CHIP, r, col), WIRE_DTYPE), _sds((r, col), F32)],
        compiler_params=_params())(place, g, recv)


def _pair_plan(srcs, lands):
    x, y, c, _ = _place()
    return [(s.at[:, 1 - c], l, (x, y, 1 - c)) for s, l in zip(srcs, lands)]


def _chip_plan(srcs, lands):
    x, y, c, chips = _place()
    plan = []
    for s, l in zip(srcs, lands):
        if len(s.shape) == 2:
            me = 4 * x + 2 * y + c
            plan += [(s, l.at[me], (x ^ (k >> 2), y ^ ((k >> 1) & 1), c ^ (k & 1))) for k in range(1, N_DEV)]
        else:
            plan += [(s.at[2 * px + py], l.at[j], (px, py, c)) for j, (px, py) in enumerate(chips)]
    return plan


def _exchange_start(srcs, lands, plan, count, tag):
    n = len(srcs)
    both = list(srcs) + list(lands)

    def body(*refs):
        s_sem, r_sem, token = refs[2 * n], refs[2 * n + 1], refs[4 * n + 2]
        for k, (src, dst, dev) in enumerate(plan(refs[:n], refs[n:2 * n])):
            _remote(src, dst, s_sem.at[k], r_sem.at[k], dev).start()
        token[...] = jnp.zeros_like(token)

    dma = pltpu.SemaphoreType.DMA
    res = pl.pallas_call(
        body, name=f"exchange_start_{tag}", in_specs=[HBM] * (2 * n),
        out_specs=[SEM, SEM] + [HBM] * (2 * n) + [pl.BlockSpec(memory_space=pltpu.VMEM)],
        out_shape=[dma((count,)), dma((count,))] + [pltpu.HBM(b.shape, b.dtype) for b in both] + [_sds((8, LANES), F32)],
        input_output_aliases={i: i + 2 for i in range(2 * n)},
        compiler_params=pltpu.CompilerParams(has_side_effects=EFFECT),
        )(*[_in_hbm(b) for b in both])
    return res[0], res[1], list(res[2:2 + n]), list(res[2 + n:2 + 2 * n]), res[-1]


def _exchange_wait(s_sem, r_sem, srcs, lands, plan, after, tag):
    n = len(srcs)

    def body(*refs):
        s_ref, r_ref = refs[2 * n], refs[2 * n + 1]
        for k, (src, dst, dev) in enumerate(plan(refs[:n], refs[n:2 * n])):
            cp = _remote(src, dst, s_ref.at[k], r_ref.at[k], dev)
            cp.wait_send()
            cp.wait_recv()

    both = list(srcs) + list(lands)
    res = pl.pallas_call(
        body, name=f"exchange_wait_{tag}", in_specs=[HBM] * (2 * n) + [SEM, SEM, ANY], out_specs=[HBM] * (2 * n),
        out_shape=[pltpu.HBM(b.shape, b.dtype) for b in both],
        input_output_aliases={i: i for i in range(2 * n)},
        compiler_params=pltpu.CompilerParams(has_side_effects=EFFECT),
        )(*both, s_sem, r_sem, after)
    return list(res[:n]), list(res[n:])


def _rs_chip_add(own, recv, place, l, L, prev, tag):
    r, col = own.shape
    tr = _tile(r, ROW_TILE)

    def body(place_ref, o_ref, r_ref, *rest):
        acc = o_ref[...]
        for j in range(3):
            acc = acc + r_ref[j].astype(F32)
        rest[-1][...] = acc

    in_specs = [pl.BlockSpec((tr, col), lambda i, p: (i, 0)), pl.BlockSpec((3, tr, col), lambda i, p: (0, i, 0))]
    args = [place, own, recv]
    kw = {}
    if prev is not None:
        in_specs.append(ANY)
        args.append(prev)
        kw["input_output_aliases"] = {3: 0}
    grid_spec = pltpu.PrefetchScalarGridSpec(
        num_scalar_prefetch=1, grid=(r // tr,), in_specs=in_specs,
        out_specs=pl.BlockSpec((None, None, tr, col), lambda i, p: (l, p[0], i, 0)))
    return pl.pallas_call(
        body, name=f"rs_chip_add_{tag}", grid_spec=grid_spec, out_shape=_sds((L, 2, r, col), F32),
        compiler_params=_params(), **kw)(*args)


def _rs_pair_share(fulls, tag):
    n = len(fulls)

    def body(*refs):
        outs = refs[n:2 * n]
        s_sem, r_sem = refs[2 * n:]
        x, y, c, _ = _place()
        sib = (x, y, 1 - c)
        started = []
        for i in range(n):
            cp = _remote(outs[i].at[:, c], outs[i].at[:, c], s_sem.at[i], r_sem.at[i], sib)
            cp.start()
            started.append(cp)
        for i, cp in enumerate(started):
            cp.wait_send()
            _remote(outs[i].at[:, 1 - c], outs[i].at[:, 1 - c], s_sem.at[i], r_sem.at[i], sib).wait_recv()

    dma = pltpu.SemaphoreType.DMA
    return pl.pallas_call(
        body, name=f"rs_pair_share_{tag}", in_specs=[ANY] * n, out_specs=[ANY] * n,
        out_shape=[_sds(f.shape, f.dtype) for f in fulls],
        input_output_aliases={i: i for i in range(n)},
        scratch_shapes=[dma((n,)), dma((n,))],
        )(*fulls)


def _ffn_fwd(x, g_pre, g_post, w_in, w_out, tag):
    xn, hgu, act = _norm_mm_in(x, g_pre, w_in, tag, swiglu=True)
    if callable(w_out):
        w_out = w_out(act)
    h, x_out = _mm_out_post(act, w_out.reshape(-1, w_out.shape[-1]), x, g_post, 0.5, tag)
    return x_out, (x, xn, hgu, act, h)


def _ffn_bwd(dx, saved, g_pre, g_post, w_in, w_out, tag, between=None, finish=None):
    x, xn, hgu, act, h = saved
    dh, dg_post, dhgu = _post_bwd_mm(dx, h, g_post, 0.5, w_out.reshape(-1, w_out.shape[-1]), tag, hgu=hgu)
    token = None
    if between is not None:
        token = between(dhgu)
    dw_out = _mm_tn_out(act, dh, tag, after=token)
    dw_in = _mm_tn_in(xn, dhgu, tag, after=token)
    if finish is not None:
        token = finish(dw_in, dw_out)
    if token is not None:
        g_pre = g_pre + token[0, :1]
    dx_in, dg_pre = _mm_nt_pre(dhgu, w_in, dx, x, g_pre, tag)
    return dx_in, dg_pre, dg_post, dw_in, dw_out


def kernel(x, ffn1_pre_g, ffn1_post_g, ffn1_w_in, ffn1_w_out, mix_pre_g, mix_post_g, ffn2_pre_g, ffn2_post_g, ffn2_w_in, ffn2_w_out, conv_w_in, conv_k, conv_w_out, kv_g, kv_w, forget_b, attn_w_qg, attn_w_o, loss_target, m_ffn1_pre_g, m_ffn1_post_g, m_ffn1_w_in, m_ffn1_w_out, m_mix_pre_g, m_mix_post_g, m_ffn2_pre_g, m_ffn2_post_g, m_ffn2_w_in, m_ffn2_w_out, m_conv_w_in, m_conv_k, m_conv_w_out, m_kv_g, m_kv_w, m_forget_b, m_attn_w_qg, m_attn_w_o, v_ffn1_pre_g, v_ffn1_post_g, v_ffn1_w_in, v_ffn1_w_out, v_mix_pre_g, v_mix_post_g, v_ffn2_pre_g, v_ffn2_post_g, v_ffn2_w_in, v_ffn2_w_out, v_conv_w_in, v_conv_k, v_conv_w_out, v_kv_g, v_kv_w, v_forget_b, v_attn_w_qg, v_attn_w_o):
    Bl, S, D = x.shape
    T = Bl * S
    H = forget_b.shape[0]
    assert D == H * HEAD_DIM and D % LANES == 0
    kvc = kv_w.shape[1]
    kvp = -(-kvc // LANES) * LANES
    kv_all = 2 * D + LANES
    dk_cols = conv_k.shape[2]
    chip = 2 * lax.axis_index("x") + lax.axis_index("y")
    core = lax.axis_index("c")

    given = dict(ffn1_w_in=(ffn1_w_in, m_ffn1_w_in, v_ffn1_w_in), ffn1_w_out=(ffn1_w_out, m_ffn1_w_out, v_ffn1_w_out),
                 ffn2_w_in=(ffn2_w_in, m_ffn2_w_in, v_ffn2_w_in), ffn2_w_out=(ffn2_w_out, m_ffn2_w_out, v_ffn2_w_out),
                 conv_w_in=(conv_w_in, m_conv_w_in, v_conv_w_in), conv_w_out=(conv_w_out, m_conv_w_out, v_conv_w_out),
                 kv_w=(kv_w, m_kv_w, v_kv_w), attn_w_qg=(attn_w_qg, m_attn_w_qg, v_attn_w_qg),
                 attn_w_o=(attn_w_o, m_attn_w_o, v_attn_w_o))
    shards = {k: w for k, (w, _, _) in given.items()}
    shards["kv_w"] = jnp.pad(kv_w, ((0, 0), (0, kvp - kvc)))[None]
    groups = [[("ffn1_w_in", 0), ("ffn1_w_out", 0)], [("conv_w_in", 0), ("conv_w_out", 0)],
              [("ffn2_w_in", 0), ("ffn2_w_out", 0)], [("kv_w", 0), ("ffn1_w_in", 1), ("ffn1_w_out", 1)],
              [("attn_w_qg", 0), ("attn_w_o", 0), ("ffn2_w_in", 1), ("ffn2_w_out", 1)]]
    second = groups[3] + groups[4]
    place = jnp.stack([core, chip]).astype(jnp.int32)

    def slot(key, where):
        w = shards[key[0]]
        L, r, col = w.shape
        return _own_slot(w.reshape(L, 2, r // 2, col), key[1], MM_DTYPE, where, f"{key[0]}{key[1]}")

    def whole(g):
        return g.reshape(N_CHIP, -1, g.shape[-1])

    taps_slot = _own_slot(jnp.pad(conv_k[0], ((0, 13), (0, 0))).reshape(1, 2, 8, dk_cols), 0, F32, place, "conv_k")
    fb = jnp.pad(forget_b, (0, LANES - H)).reshape(1, LANES)
    w_in0, w_out0 = groups[0]
    s_0, r_0, fly_0, token = _gather_start([slot(w_in0, place), taps_slot, slot(w_out0, place)], fb, "first")
    later = groups[1] + groups[2] + groups[3] + groups[4]
    s_1, r_1, fly_1, token = _gather_start([slot(key, place) for key in later], token, "rest")
    W = {}

    def land(sems, bufs, lo, after, tag):
        return _gather_land(*_gather_pass(*sems, bufs, lo, after, tag), tag)

    def arrive(g, after):
        lo = sum(len(groups[k]) for k in range(1, g))
        got = land((s_1, r_1), fly_1[lo:lo + len(groups[g])], lo, after, f"g{g}")
        W.update({key: whole(b) for key, b in zip(groups[g], got)})

    w_first, taps = land((s_0, r_0), fly_0[:2], 0, token, "g0")
    k_taps = taps.reshape(N_CHIP, 16, dk_cols).transpose(1, 0, 2).reshape(16, D)[:8]

    x0 = x.reshape(T, D)
    W[w_in0] = whole(w_first)

    def first_w_out(act):
        W[w_out0] = whole(land((s_0, r_0), fly_0[2:], 2, act, "g0_out")[0])
        return W[w_out0]

    x1, s_f1a = _ffn_fwd(x0, ffn1_pre_g[0], ffn1_post_g[0], W[w_in0], first_w_out, "l0f1")
    arrive(1, x1)
    w_o_conv = W["conv_w_out", 0].reshape(D, D)
    xn_c, bch = _norm_mm_in(x1, mix_pre_g[0], W["conv_w_in", 0], "conv")
    z_c = _conv_fwd(bch, k_taps, Bl, S)
    m_c, x2 = _mm_out_post(z_c, w_o_conv, x1, mix_post_g[0], 1.0, "conv_out")
    arrive(2, x2)
    x3, s_f2a = _ffn_fwd(x2, ffn2_pre_g[0], ffn2_post_g[0], W["ffn2_w_in", 0], W["ffn2_w_out", 0], "l0f2")

    arrive(3, x3)
    kv_full = jnp.concatenate([W["kv_w", 0][s, :, :kvc] for s in range(N_CHIP)], axis=1)
    kv_full = jnp.pad(kv_full, ((0, 0), (0, kv_all - kv_full.shape[1])))
    xn_kv = _rms_fwd(x3, kv_g, "kv")
    kvact = _mm_nn(xn_kv, kv_full[:, :2 * D], MM_DTYPE, "kv")
    pf = _mm_nn(xn_kv, kv_full[:, 2 * D:], F32, "forget")
    cum = _forget_fwd(pf, fb, Bl, S)
    bq = min(S, ATT_BLOCK)
    c3 = cum.reshape(Bl, S, LANES)[:, :, :H].transpose(0, 2, 1)
    c_col = jnp.broadcast_to(c3[..., None], (Bl, H, S, LANES))
    c_row = c3.reshape(Bl, H, S // bq, 1, bq)

    x4, s_f1b = _ffn_fwd(x3, ffn1_pre_g[1], ffn1_post_g[1], W["ffn1_w_in", 1], W["ffn1_w_out", 1], "l1f1")
    arrive(4, x4)
    w_o_attn = W["attn_w_o", 0].reshape(D, D)
    xn_a, qg = _norm_mm_in(x4, mix_pre_g[1], W["attn_w_qg", 0], "qg")
    o, lse, z_a = _attn_fwd(qg, kvact, c_col, c_row, Bl, S, D)
    m_a, x5 = _mm_out_post(z_a, w_o_attn, x4, mix_post_g[1], 1.0, "attn_out")
    x6, s_f2b = _ffn_fwd(x5, ffn2_pre_g[1], ffn2_post_g[1], W["ffn2_w_in", 1], W["ffn2_w_out", 1], "l1f2")

    dy, loss_local = _loss_grad(x6, loss_target.reshape(T, D))

    G = {}
    dx5, dg_f2pre_1, dg_f2post_1, G["ffn2_w_in", 1], G["ffn2_w_out", 1] = _ffn_bwd(
        dy, s_f2b, ffn2_pre_g[1], ffn2_post_g[1], W["ffn2_w_in", 1], W["ffn2_w_out", 1], "l1f2")
    dm_a, dg_mixpost_1, dz_a = _post_bwd_mm(dx5, m_a, mix_post_g[1], 1.0, w_o_attn, "attn_out")
    G["attn_w_o", 0] = _mm_tn_out(z_a, dm_a, "attn_out")
    dq, dk, dv, dcr = _attn_bwd(qg, kvact, dz_a, lse, c_col, c_row, Bl, S, D)
    dqg = _gate_bwd(dz_a, qg, o, dq)
    G["attn_w_qg", 0] = _mm_tn_in(xn_a, dqg, "qg")
    dx4, dg_mixpre_1 = _mm_nt_pre(dqg, W["attn_w_qg", 0], dx5, x4, mix_pre_g[1], "qg")
    dx3, dg_f1pre_1, dg_f1post_1, G["ffn1_w_in", 1], G["ffn1_w_out", 1] = _ffn_bwd(
        dx4, s_f1b, ffn1_pre_g[1], ffn1_post_g[1], W["ffn1_w_in", 1], W["ffn1_w_out", 1], "l1f1")

    dcum = jnp.pad(dcr.reshape(Bl, H, S).transpose(0, 2, 1), ((0, 0), (0, 0), (0, LANES - H))).reshape(T, LANES)
    dpf, dfb = _forget_bwd(dcum, pf, fb, Bl, S)
    dp = jnp.concatenate([dk.astype(MM_DTYPE), dv.astype(MM_DTYPE), dpf], axis=1)
    G_kv_full = _mm_tn(xn_kv, dp, "kv")
    G["kv_w", 0] = jnp.stack([jnp.pad(G_kv_full[:, s * kvc:(s + 1) * kvc], ((0, 0), (0, kvp - kvc))) for s in range(N_CHIP)])
    dx3, dg_kv = _mm_nt_pre(dp, kv_full, dx3, x3, kv_g, "kv")

    def halves_of(keys):
        return [G[k].reshape(N_CHIP, 2, G[k].shape[1] // 2, G[k].shape[2]) for k in keys]

    def pair_adds(keys, grads, recvs):
        wires, owns = [], []
        for k, g, r in zip(keys, grads, recvs):
            w, own = _rs_pair_add(g, r, place, f"{k[0]}{k[1]}")
            wires.append(w)
            owns.append(own)
        return wires, owns

    def chip_start(wires, tag, extra=()):
        lands = [lax.empty((3,) + w.shape[1:], w.dtype) for w in wires]
        lands += [jnp.zeros((N_DEV,) + e.shape, e.dtype) for e in extra]
        return _exchange_start(list(wires) + list(extra), lands, _chip_plan, 3 * len(wires) + (N_DEV - 1) * len(extra), tag)

    late = groups[2] + groups[1]
    last = groups[0]
    grads_2 = halves_of(second)
    p_sems, p_semr, grads_2, sib_2, token = _exchange_start(
        grads_2, [lax.empty((N_CHIP,) + g.shape[2:], g.dtype) for g in grads_2], _pair_plan, len(grads_2), "pair_second")

    dx2, dg_f2pre_0, dg_f2post_0, G["ffn2_w_in", 0], G["ffn2_w_out", 0] = _ffn_bwd(
        dx3, s_f2a, ffn2_pre_g[0], ffn2_post_g[0] + token[0, :1], W["ffn2_w_in", 0], W["ffn2_w_out", 0], "l0f2")
    grads_2, sib_2 = _exchange_wait(p_sems, p_semr, grads_2, sib_2, _pair_plan, dx2, "pair_second")
    wires_2, owns_2 = pair_adds(second, grads_2, sib_2)
    c_2 = chip_start(wires_2, "chip_second")
    dm_c, dg_mixpost_0, dz_c = _post_bwd_mm(dx2, m_c, mix_post_g[0] + c_2[4][0, :1], 1.0, w_o_conv, "conv_out")
    G["conv_w_out", 0] = _mm_tn_out(z_c, dm_c, "conv_out")
    db, dcg, dhh, dk_taps = _conv_bwd(bch, dz_c, k_taps, Bl, S)
    dbch = jnp.concatenate([db, dcg, dhh], axis=1)
    G["conv_w_in", 0] = _mm_tn_in(xn_c, dbch, "conv")
    dx1, dg_mixpre_0 = _mm_nt_pre(dbch, W["conv_w_in", 0], dx2, x1, mix_pre_g[0], "conv")
    def pair_start(keys, tag):
        grads = halves_of(keys)
        lands = [lax.empty((N_CHIP,) + g.shape[2:], g.dtype) for g in grads]
        return _exchange_start(grads, lands, _pair_plan, len(grads), tag)

    p_l = pair_start(late, "pair_late")
    late_done = {}

    def late_leg(dhgu):
        grads_l, sib_l = _exchange_wait(*p_l[:4], _pair_plan, dhgu, "pair_late")
        late_done["wires"], late_done["owns"] = pair_adds(late, grads_l, sib_l)
        late_done["chip"] = chip_start(late_done["wires"], "chip_late")
        return late_done["chip"][4]

    def last_pair(dw_in, dw_out):
        G["ffn1_w_in", 0], G["ffn1_w_out", 0] = dw_in, dw_out
        late_done["pair"] = pair_start(last, "pair_last")
        return late_done["pair"][4]

    dx0, dg_f1pre_0, dg_f1post_0, _, _ = _ffn_bwd(
        dx1, s_f1a, ffn1_pre_g[0], ffn1_post_g[0] + p_l[4][0, :1], W["ffn1_w_in", 0], W["ffn1_w_out", 0], "l0f1",
        between=late_leg, finish=last_pair)
    grad_x = dx0.reshape(Bl, S, D)
    owns_l, c_l, p_1 = late_done["owns"], late_done["chip"], late_done["pair"]

    grads_1, sib_1 = _exchange_wait(*p_1[:4], _pair_plan, dx0, "pair_last")
    wires_1, owns_1 = pair_adds(last, grads_1, sib_1)

    def row(v):
        return jnp.pad(v.reshape(-1), (0, D - v.size)).reshape(1, D)

    small_parts = [dg_f1pre_0, dg_f1pre_1, dg_f1post_0, dg_f1post_1, dg_mixpre_0, dg_mixpre_1, dg_mixpost_0, dg_mixpost_1,
                   dg_f2pre_0, dg_f2pre_1, dg_f2post_0, dg_f2post_1, dg_kv, row(dfb[0, :H]), dk_taps[:3],
                   jnp.full((1, D), loss_local)]
    small = jnp.concatenate(small_parts, axis=0)
    small = jnp.pad(small, ((0, SMALL_ROWS - small.shape[0]), (0, 0)))
    c_1 = chip_start(wires_1, "chip_last", extra=[small])
    _, recvs_2 = _exchange_wait(*c_2[:4], _chip_plan, c_1[4], "chip_second")
    _, recvs_l = _exchange_wait(*c_l[:4], _chip_plan, c_1[4], "chip_late")
    partial = {}

    def chip_adds(keys, owns, recvs):
        for (name, l), own, rcv in zip(keys, owns, recvs):
            partial[name] = _rs_chip_add(own, rcv, place, l, shards[name].shape[0], partial.get(name), f"{name}{l}")

    chip_adds(late + second, owns_l + owns_2, recvs_l + recvs_2)
    res = {}

    def adamw(names, reduced, after):
        for k, red in zip(names, reduced):
            w, m, v = given[k]
            g2 = red.reshape(-1, red.shape[-1])
            if k == "kv_w":
                g2 = g2[:, :kvc]
            flat = lambda a: a.reshape(-1, a.shape[-1])
            go, d, mn, vn = _adamw(flat(w), g2, flat(m), flat(v), k, after=after)
            res[k] = tuple(a.reshape(w.shape) for a in (go, d, mn, vn))
        return d

    early = [k for k in partial if (k, 0) not in last]
    done = adamw(early, _rs_pair_share([partial[k] for k in early], "early"), c_1[4])
    _, recvs_1 = _exchange_wait(*c_1[:4], _chip_plan, done, "chip_last")
    chip_adds(last, owns_1, recvs_1[:-1])
    rest = [k for k, _ in last]
    adamw(rest, _rs_pair_share([partial[k] for k in rest], "last"), None)
    gsum = _sum_devices(recvs_1[-1], small, place)
    loss = gsum[17, 0]

    small_names = ["ffn1_pre_g", "ffn1_post_g", "mix_pre_g", "mix_post_g", "ffn2_pre_g", "ffn2_post_g"]
    small_given = dict(ffn1_pre_g=(ffn1_pre_g, m_ffn1_pre_g, v_ffn1_pre_g), ffn1_post_g=(ffn1_post_g, m_ffn1_post_g, v_ffn1_post_g),
                       mix_pre_g=(mix_pre_g, m_mix_pre_g, v_mix_pre_g), mix_post_g=(mix_post_g, m_mix_post_g, v_mix_post_g),
                       ffn2_pre_g=(ffn2_pre_g, m_ffn2_pre_g, v_ffn2_pre_g), ffn2_post_g=(ffn2_post_g, m_ffn2_post_g, v_ffn2_post_g))

    def pack(idx):
        rows_ = [small_given[k][idx] for k in small_names]
        rows_ += [row((kv_g, m_kv_g, v_kv_g)[idx]), row((forget_b, m_forget_b, v_forget_b)[idx])]
        rows_.append(jnp.pad((conv_k, m_conv_k, v_conv_k)[idx][0], ((0, 0), (0, D - dk_cols))))
        a = jnp.concatenate(rows_, axis=0)
        return jnp.pad(a, ((0, SMALL_ROWS - a.shape[0]), (0, 0)))

    g_taps = lax.dynamic_slice_in_dim(gsum[14:17], chip * dk_cols, dk_cols, axis=1)
    g_small = jnp.concatenate([gsum[:14], jnp.pad(g_taps, ((0, 0), (0, D - dk_cols))), gsum[17:]], axis=0)
    g_small, d_s, m_s, v_s = _adamw(pack(0), g_small, pack(1), pack(2), "small")
    for i, k in enumerate(small_names):
        res[k] = tuple(a[2 * i:2 * i + 2] for a in (g_small, d_s, m_s, v_s))
    res["kv_g"] = tuple(a[12] for a in (g_small, d_s, m_s, v_s))
    res["forget_b"] = tuple(a[13, :H] for a in (g_small, d_s, m_s, v_s))
    res["conv_k"] = tuple(a[14:17, :dk_cols][None] for a in (g_small, d_s, m_s, v_s))

    order = ["ffn1_pre_g", "ffn1_post_g", "ffn1_w_in", "ffn1_w_out", "mix_pre_g", "mix_post_g", "ffn2_pre_g", "ffn2_post_g",
             "ffn2_w_in", "ffn2_w_out", "conv_w_in", "conv_k", "conv_w_out", "kv_g", "kv_w", "forget_b", "attn_w_qg", "attn_w_o"]
    out = [loss, grad_x]
    for idx in range(4):
        out += [res[k][idx] for k in order]
    return tuple(out)
```

```python
import math

import jax
import jax.numpy as jnp
from jax import lax
from jax.experimental import pallas as pl
from jax.experimental.pallas import tpu as pltpu

F32 = jnp.float32
MM_DTYPE = jnp.bfloat16
WIRE_DTYPE = jnp.bfloat16

RMS_EPS = 1e-6
ADAM_LR = 0.001
ADAM_B1 = 0.9
ADAM_B2 = 0.999
ADAM_EPS = 1e-08
ADAM_WD = 0.01
ADAM_STEP = 10

HEAD_DIM = 64
LANES = 128
N_CHIP = 4
N_DEV = 8
ROW_TILE = 512
MM_TILE = 512
FUSED_TILE = 512
TN_TILE = 2048
ATT_BLOCK = 512
SMALL_ROWS = 24
V7X_VMEM_BYTES = 64 * 1024 * 1024
VMEM_LIMIT = V7X_VMEM_BYTES - 8 * 1024 * 1024
MESH = pl.DeviceIdType.MESH
ANY = pl.BlockSpec(memory_space=pl.ANY)

NT = (((1,), (1,)), ((), ()))
TN = (((0,), (0,)), ((), ()))


def _tile(n, pref):
    if n <= pref:
        return n
    t = pref - pref % 16
    while n % t:
        t -= 16
    return t


def _params():
    return pltpu.CompilerParams(vmem_limit_bytes=VMEM_LIMIT)


def _sds(shape, dtype):
    return jax.ShapeDtypeStruct(shape, dtype)


def _rows(tm, c):
    return pl.BlockSpec((tm, c), lambda i: (i, 0))


def _whole(shape):
    return pl.BlockSpec(shape, lambda *_: (0,) * len(shape))


def _resident(shape):
    return pl.BlockSpec(shape, lambda *_: (0,) * len(shape), pipeline_mode=pl.Buffered(1))


def _rms_fwd(x, g, tag):
    T, D = x.shape
    tm = _tile(T, ROW_TILE)

    def body(x_ref, g_ref, o_ref):
        xv = x_ref[...]
        r = lax.rsqrt(jnp.mean(xv * xv, axis=-1, keepdims=True) + RMS_EPS)
        o_ref[...] = (xv * r * g_ref[...]).astype(o_ref.dtype)

    return pl.pallas_call(
        body, name=f"rms_fwd_{tag}", grid=(T // tm,),
        in_specs=[_rows(tm, D), _whole((1, D))], out_specs=_rows(tm, D),
        out_shape=_sds((T, D), MM_DTYPE), compiler_params=_params())(x, g.reshape(1, D))


def _accumulate(ref, part, first):
    @pl.when(first)
    def _():
        ref[...] = part

    @pl.when(jnp.logical_not(first))
    def _():
        ref[...] += part


def _loss_grad(y, tgt):
    T, D = y.shape
    tm = _tile(T, ROW_TILE)

    def body(y_ref, t_ref, dy_ref, l_ref):
        e = y_ref[...] - t_ref[...]
        row = jnp.mean(e * e, axis=-1, keepdims=True)
        part = jnp.broadcast_to(jnp.sum(row, axis=0, keepdims=True), (8, LANES))
        _accumulate(l_ref, part, pl.program_id(0) == 0)
        dy_ref[...] = e * (1.0 / D)

    dy, lsum = pl.pallas_call(
        body, name="loss_grad", grid=(T // tm,),
        in_specs=[_rows(tm, D), _rows(tm, D)], out_specs=[_rows(tm, D), _whole((8, LANES))],
        out_shape=[_sds((T, D), F32), _sds((8, LANES), F32)], compiler_params=_params())(y, tgt)
    return dy, 0.5 * lsum[0, 0]


def _shift_down(u, d, rows):
    return jnp.where(rows >= d, pltpu.roll(u, d, 0), 0.0)


def _shift_up(u, d, rows, S):
    return jnp.where(rows < S - d, pltpu.roll(u, S - d, 0), 0.0)


def _conv_fwd(bch, k8, Bl, S):
    T, D3 = bch.shape
    D = D3 // 3
    dc = min(D, 2 * LANES)
    nd = D // dc

    def body(b_ref, c_ref, h_ref, k_ref, z_ref):
        rows = lax.broadcasted_iota(jnp.int32, (S, 1), 0)
        u = c_ref[...].astype(F32) * h_ref[...].astype(F32)
        y = k_ref[2:3, :] * u + k_ref[1:2, :] * _shift_down(u, 1, rows) + k_ref[0:1, :] * _shift_down(u, 2, rows)
        z_ref[...] = (b_ref[...].astype(F32) * y).astype(z_ref.dtype)

    return pl.pallas_call(
        body, name="conv_fwd", grid=(Bl, nd),
        in_specs=[pl.BlockSpec((S, dc), lambda b, j: (b, j)),
                  pl.BlockSpec((S, dc), lambda b, j: (b, nd + j)),
                  pl.BlockSpec((S, dc), lambda b, j: (b, 2 * nd + j)),
                  pl.BlockSpec((8, dc), lambda b, j: (0, j))],
        out_specs=pl.BlockSpec((S, dc), lambda b, j: (b, j)),
        out_shape=_sds((T, D), MM_DTYPE), compiler_params=_params())(bch, bch, bch, k8)


def _conv_bwd(bch, dz, k8, Bl, S):
    T, D3 = bch.shape
    D = D3 // 3
    dc = min(D, 2 * LANES)
    nd = D // dc

    def body(b_ref, c_ref, h_ref, dz_ref, k_ref, db_ref, dc_ref, dh_ref, dk_ref):
        rows = lax.broadcasted_iota(jnp.int32, (S, 1), 0)
        bv = b_ref[...].astype(F32)
        cv = c_ref[...].astype(F32)
        hv = h_ref[...].astype(F32)
        dzv = dz_ref[...].astype(F32)
        u = cv * hv
        u1 = _shift_down(u, 1, rows)
        u2 = _shift_down(u, 2, rows)
        y = k_ref[2:3, :] * u + k_ref[1:2, :] * u1 + k_ref[0:1, :] * u2
        db_ref[...] = (dzv * y).astype(db_ref.dtype)
        dy = dzv * bv
        du = k_ref[2:3, :] * dy + k_ref[1:2, :] * _shift_up(dy, 1, rows, S) + k_ref[0:1, :] * _shift_up(dy, 2, rows, S)
        dc_ref[...] = (du * hv).astype(dc_ref.dtype)
        dh_ref[...] = (du * cv).astype(dh_ref.dtype)

        @pl.when(pl.program_id(1) == 0)
        def _():
            dk_ref[...] = jnp.zeros_like(dk_ref)

        dk_ref[0:1, :] += jnp.sum(dy * u2, axis=0, keepdims=True)
        dk_ref[1:2, :] += jnp.sum(dy * u1, axis=0, keepdims=True)
        dk_ref[2:3, :] += jnp.sum(dy * u, axis=0, keepdims=True)

    seq = lambda off: pl.BlockSpec((S, dc), lambda j, b: (b, off + j))
    return pl.pallas_call(
        body, name="conv_bwd", grid=(nd, Bl),
        in_specs=[seq(0), seq(nd), seq(2 * nd), seq(0), pl.BlockSpec((8, dc), lambda j, b: (0, j))],
        out_specs=[seq(0), seq(0), seq(0), pl.BlockSpec((8, dc), lambda j, b: (0, j))],
        out_shape=[_sds((T, D), MM_DTYPE)] * 3 + [_sds((8, D), F32)],
        compiler_params=_params())(bch, bch, bch, dz, k8)


def _forget_fwd(pf, fb, Bl, S):
    T = pf.shape[0]

    def body(p_ref, fb_ref, c_ref):
        rows = lax.broadcasted_iota(jnp.int32, (S, 1), 0)
        z = p_ref[...] + fb_ref[...]
        acc = jnp.minimum(z, 0.0) - jnp.log1p(jnp.exp(-jnp.abs(z)))
        d = 1
        while d < S:
            acc = acc + _shift_down(acc, d, rows)
            d *= 2
        c_ref[...] = acc

    return pl.pallas_call(
        body, name="forget_fwd", grid=(Bl,),
        in_specs=[_rows(S, LANES), _whole((1, LANES))], out_specs=_rows(S, LANES),
        out_shape=_sds((T, LANES), F32), compiler_params=_params())(pf, fb)


def _forget_bwd(dc, pf, fb, Bl, S):
    T = pf.shape[0]

    def body(dc_ref, p_ref, fb_ref, df_ref, dfb_ref):
        rows = lax.broadcasted_iota(jnp.int32, (S, 1), 0)
        acc = dc_ref[...]
        d = 1
        while d < S:
            acc = acc + _shift_up(acc, d, rows, S)
            d *= 2
        df = acc * jax.nn.sigmoid(-(p_ref[...] + fb_ref[...]))
        df_ref[...] = df.astype(df_ref.dtype)
        _accumulate(dfb_ref, jnp.sum(df, axis=0, keepdims=True), pl.program_id(0) == 0)

    return pl.pallas_call(
        body, name="forget_bwd", grid=(Bl,),
        in_specs=[_rows(S, LANES), _rows(S, LANES), _whole((1, LANES))],
        out_specs=[_rows(S, LANES), _whole((1, LANES))],
        out_shape=[_sds((T, LANES), MM_DTYPE), _sds((1, LANES), F32)],
        compiler_params=_params())(dc, pf, fb)


def _head_mask(h):
    lane = lax.broadcasted_iota(jnp.int32, (1, LANES), 1)
    return (lane >= h * HEAD_DIM) & (lane < (h + 1) * HEAD_DIM)


def _attn_fwd(qg, kv, c_col, c_row, Bl, S, D):
    T = Bl * S
    H = D // HEAD_DIM
    HP = D // LANES
    bq = min(S, ATT_BLOCK)
    nq = S // bq
    scale = 1.0 / math.sqrt(HEAD_DIM)

    def body(q_ref, g_ref, k_ref, v_ref, cc_ref, cr_ref, o_ref, lse_ref, z_ref):
        i = pl.program_id(2)
        q2 = q_ref[...]
        qh = [q2 * (_head_mask(h).astype(F32) * scale).astype(q2.dtype) for h in range(2)]
        cc = [cc_ref[h][:, :1] for h in range(2)]
        diag = lax.broadcasted_iota(jnp.int32, (1, bq), 1) <= lax.broadcasted_iota(jnp.int32, (bq, 1), 0)

        def block(j, carry, on_diagonal):
            off = pl.multiple_of(j * bq, bq)
            kj = k_ref[pl.ds(off, bq), :]
            vj = v_ref[pl.ds(off, bq), :]
            new = []
            for h in range(2):
                m, l, acc = carry[h]
                s = lax.dot_general(qh[h], kj, NT, preferred_element_type=F32) + cc[h] - cr_ref[h, j]
                if on_diagonal:
                    s = jnp.where(diag, s, -jnp.inf)
                m_new = jnp.maximum(m, jnp.max(s, axis=1, keepdims=True))
                p = jnp.exp(s - m_new)
                a = jnp.exp(m - m_new)
                l = a * l + jnp.sum(p, axis=1, keepdims=True)
                acc = a * acc + jnp.dot(p.astype(MM_DTYPE), vj, preferred_element_type=F32)
                new.append((m_new, l, acc))
            return tuple(new)

        one = (jnp.full((bq, 1), -jnp.inf, F32), jnp.zeros((bq, 1), F32), jnp.zeros((bq, LANES), F32))
        carry = lax.fori_loop(0, i, lambda j, c: block(j, c, False), (one, one))
        carry = block(i, carry, True)
        outs = []
        for h in range(2):
            m, l, acc = carry[h]
            outs.append(acc / l)
            lse_ref[h] = jnp.broadcast_to(m + jnp.log(l), (bq, LANES))
        o2 = jnp.where(_head_mask(0), outs[0], outs[1])
        o_ref[...] = o2
        z_ref[...] = (jax.nn.sigmoid(g_ref[...].astype(F32)) * o2).astype(z_ref.dtype)

    return pl.pallas_call(
        body, name="attn_fwd", grid=(Bl, HP, nq),
        in_specs=[pl.BlockSpec((bq, LANES), lambda b, hp, i: (b * nq + i, hp)),
                  pl.BlockSpec((bq, LANES), lambda b, hp, i: (b * nq + i, HP + hp)),
                  pl.BlockSpec((S, LANES), lambda b, hp, i: (b, hp)),
                  pl.BlockSpec((S, LANES), lambda b, hp, i: (b, HP + hp)),
                  pl.BlockSpec((None, 2, bq, LANES), lambda b, hp, i: (b, hp, i, 0)),
                  pl.BlockSpec((None, 2, nq, 1, bq), lambda b, hp, i: (b, hp, 0, 0, 0))],
        out_specs=[pl.BlockSpec((bq, LANES), lambda b, hp, i: (b * nq + i, hp)),
                   pl.BlockSpec((None, 2, bq, LANES), lambda b, hp, i: (b, hp, i, 0)),
                   pl.BlockSpec((bq, LANES), lambda b, hp, i: (b * nq + i, hp))],
        out_shape=[_sds((T, D), F32), _sds((Bl, H, S, LANES), F32), _sds((T, D), MM_DTYPE)],
        compiler_params=_params())(qg, qg, kv, kv, c_col, c_row)


def _attn_bwd(qg, kv, dz, lse, c_col, c_row, Bl, S, D):
    T = Bl * S
    H = D // HEAD_DIM
    HP = D // LANES
    bq = min(S, ATT_BLOCK)
    nq = S // bq
    scale = 1.0 / math.sqrt(HEAD_DIM)

    def body(q_ref, g_ref, k_ref, v_ref, dz_ref, lse_ref, cc_ref, cr_ref, dq_ref, dk_ref, dv_ref, dcr_ref, p_sc, dp_sc):
        i = pl.program_id(2)

        @pl.when(i == 0)
        def _():
            dk_ref[...] = jnp.zeros_like(dk_ref)
            dv_ref[...] = jnp.zeros_like(dv_ref)
            dcr_ref[...] = jnp.zeros_like(dcr_ref)

        q2 = q_ref[...]
        do2 = (dz_ref[...].astype(F32) * jax.nn.sigmoid(g_ref[...].astype(F32))).astype(MM_DTYPE)
        masks = [_head_mask(h).astype(F32) for h in range(2)]
        qh = [q2 * (masks[h] * scale).astype(q2.dtype) for h in range(2)]
        doh = [do2 * masks[h].astype(do2.dtype) for h in range(2)]
        cc = [cc_ref[h][:, :1] for h in range(2)]
        lse = [lse_ref[h][:, :1] for h in range(2)]
        diag = lax.broadcasted_iota(jnp.int32, (1, bq), 1) <= lax.broadcasted_iota(jnp.int32, (bq, 1), 0)

        def sweep1(j, delta, on_diagonal):
            off = pl.multiple_of(j * bq, bq)
            kj = k_ref[pl.ds(off, bq), :]
            vj = v_ref[pl.ds(off, bq), :]
            new = []
            dv = None
            for h in range(2):
                s = lax.dot_general(qh[h], kj, NT, preferred_element_type=F32) + cc[h] - cr_ref[h, j]
                if on_diagonal:
                    s = jnp.where(diag, s, -jnp.inf)
                p = jnp.exp(s - lse[h])
                dp = lax.dot_general(doh[h], vj, NT, preferred_element_type=F32)
                p_sc[h, j] = p
                dp_sc[h, j] = dp
                part = lax.dot_general(p.astype(MM_DTYPE), doh[h], TN, preferred_element_type=F32)
                dv = part if dv is None else dv + part
                new.append(delta[h] + jnp.sum(p * dp, axis=1, keepdims=True))
            dv_ref[pl.ds(off, bq), :] += dv
            return tuple(new)

        zero = jnp.zeros((bq, 1), F32)
        delta = lax.fori_loop(0, i, lambda j, d: sweep1(j, d, False), (zero, zero))
        delta = sweep1(i, delta, True)

        def sweep2(j, dq):
            off = pl.multiple_of(j * bq, bq)
            kj = k_ref[pl.ds(off, bq), :]
            dk = None
            for h in range(2):
                ds = p_sc[h, j] * (dp_sc[h, j] - delta[h])
                dcr_ref[h, j] -= jnp.sum(ds, axis=0, keepdims=True)
                dsb = ds.astype(MM_DTYPE)
                dq = dq + jnp.dot(dsb, kj * (masks[h] * scale).astype(kj.dtype), preferred_element_type=F32)
                part = lax.dot_general(dsb, qh[h], TN, preferred_element_type=F32)
                dk = part if dk is None else dk + part
            dk_ref[pl.ds(off, bq), :] += dk
            return dq

        dq_ref[...] = lax.fori_loop(0, i + 1, sweep2, jnp.zeros((bq, LANES), F32))

    blk = lambda col: pl.BlockSpec((bq, LANES), lambda b, hp, i: (b * nq + i, col(hp)))
    seq = lambda col: pl.BlockSpec((S, LANES), lambda b, hp, i: (b, col(hp)))
    per_head = pl.BlockSpec((None, 2, bq, LANES), lambda b, hp, i: (b, hp, i, 0))
    rows = pl.BlockSpec((None, 2, nq, 1, bq), lambda b, hp, i: (b, hp, 0, 0, 0))
    return pl.pallas_call(
        body, name="attn_bwd", grid=(Bl, HP, nq),
        in_specs=[blk(lambda hp: hp), blk(lambda hp: HP + hp), seq(lambda hp: hp), seq(lambda hp: HP + hp),
                  blk(lambda hp: hp), per_head, per_head, rows],
        out_specs=[blk(lambda hp: hp), seq(lambda hp: hp), seq(lambda hp: hp), rows],
        out_shape=[_sds((T, D), F32), _sds((T, D), F32), _sds((T, D), F32), _sds((Bl, H, nq, 1, bq), F32)],
        scratch_shapes=[pltpu.VMEM((2, nq, bq, bq), F32), pltpu.VMEM((2, nq, bq, bq), F32)],
        compiler_params=_params())(qg, qg, kv, kv, dz, lse, c_col, c_row)


def _gate_bwd(dz, qg, o, dq):
    T, D = dz.shape
    tm = _tile(T, ROW_TILE)

    def body(dz_ref, g_ref, o_ref, dq_ref, out_ref):
        g = g_ref[...].astype(F32)
        sg = jax.nn.sigmoid(g)
        out_ref[:, :D] = dq_ref[...].astype(out_ref.dtype)
        out_ref[:, D:] = (dz_ref[...].astype(F32) * o_ref[...] * sg * (1.0 - sg)).astype(out_ref.dtype)

    return pl.pallas_call(
        body, name="gate_bwd", grid=(T // tm,),
        in_specs=[_rows(tm, D), pl.BlockSpec((tm, D), lambda i: (i, 1)), _rows(tm, D), _rows(tm, D)],
        out_specs=_rows(tm, 2 * D), out_shape=_sds((T, 2 * D), MM_DTYPE),
        compiler_params=_params())(dz, qg, o, dq)


def _mm_nn(a, b, out_dtype, tag):
    T, K = a.shape
    N = b.shape[1]
    tm = _tile(T, MM_TILE)

    def body(a_ref, b_ref, o_ref):
        o_ref[...] = jnp.dot(a_ref[...], b_ref[...], preferred_element_type=F32).astype(o_ref.dtype)

    return pl.pallas_call(
        body, name=f"mm_nn_{tag}", grid=(T // tm,),
        in_specs=[_rows(tm, K), _whole((K, N))], out_specs=_rows(tm, N),
        out_shape=_sds((T, N), out_dtype), compiler_params=_params())(a, b)


def _mm_tn_in(a, dy, tag, after=None):
    T, K = a.shape
    n = dy.shape[1] // N_CHIP
    tt = _tile(T, TN_TILE)
    extra = [] if after is None else [after]

    def body(a_ref, d_ref, *rest):
        part = lax.dot_general(a_ref[...], d_ref[...], TN, preferred_element_type=F32)
        _accumulate(rest[-1], part, pl.program_id(1) == 0)

    return pl.pallas_call(
        body, name=f"mm_tn_in_{tag}", grid=(N_CHIP, T // tt),
        in_specs=[pl.BlockSpec((tt, K), lambda s, t: (t, 0)), pl.BlockSpec((tt, n), lambda s, t: (t, s))]
        + [ANY] * len(extra),
        out_specs=pl.BlockSpec((None, K, n), lambda s, t: (s, 0, 0)), out_shape=_sds((N_CHIP, K, n), F32),
        compiler_params=_params())(a, dy, *extra)


def _mm_tn_out(act, dh, tag, after=None):
    T, R4 = act.shape
    D = dh.shape[1]
    r = R4 // N_CHIP
    g = 1 if r % LANES == 0 else 2
    tt = _tile(T, TN_TILE)
    extra = [] if after is None else [after]

    def body(a_ref, d_ref, *rest):
        o_ref = rest[-1]
        part = lax.dot_general(a_ref[...], d_ref[...], TN, preferred_element_type=F32)
        first = pl.program_id(1) == 0
        for q in range(g):
            _accumulate(o_ref.at[q], part[q * r:(q + 1) * r], first)

    return pl.pallas_call(
        body, name=f"mm_tn_out_{tag}", grid=(N_CHIP // g, T // tt),
        in_specs=[pl.BlockSpec((tt, g * r), lambda s, t: (t, s)), pl.BlockSpec((tt, D), lambda s, t: (t, 0))]
        + [ANY] * len(extra),
        out_specs=pl.BlockSpec((g, r, D), lambda s, t: (s, 0, 0)), out_shape=_sds((N_CHIP, r, D), F32),
        compiler_params=_params())(act, dh, *extra)


def _mm_tn(a, b, tag):
    T, K = a.shape
    N = b.shape[1]
    tt = _tile(T, TN_TILE)

    def body(a_ref, b_ref, o_ref):
        part = lax.dot_general(a_ref[...], b_ref[...], TN, preferred_element_type=F32)
        _accumulate(o_ref, part, pl.program_id(0) == 0)

    return pl.pallas_call(
        body, name=f"mm_tn_{tag}", grid=(T // tt,),
        in_specs=[_rows(tt, K), _rows(tt, N)], out_specs=_whole((K, N)),
        out_shape=_sds((K, N), F32), compiler_params=_params())(a, b)


def _norm_mm_in(x, g, wg, tag, swiglu=False):
    T, D = x.shape
    n = wg.shape[-1]
    tm = _tile(T, FUSED_TILE)
    half = N_CHIP // 2

    def body(x_ref, g_ref, w_ref, xn_ref, y_ref, *rest):
        xv = x_ref[...]
        r = lax.rsqrt(jnp.mean(xv * xv, axis=-1, keepdims=True) + RMS_EPS)
        xn = (xv * r * g_ref[...]).astype(xn_ref.dtype)
        xn_ref[...] = xn

        def product(s):
            p = jnp.dot(xn, w_ref[s], preferred_element_type=F32)
            y_ref[:, s * n:(s + 1) * n] = p.astype(y_ref.dtype)
            return p

        if swiglu:
            for q in range(half):
                gate, up = product(q), product(half + q)
                rest[0][:, q * n:(q + 1) * n] = (gate * jax.nn.sigmoid(gate) * up).astype(rest[0].dtype)
        else:
            for s in range(N_CHIP):
                product(s)

    out_specs = [_rows(tm, D), _rows(tm, N_CHIP * n)]
    out_shape = [_sds((T, D), MM_DTYPE), _sds((T, N_CHIP * n), MM_DTYPE)]
    if swiglu:
        out_specs.append(_rows(tm, half * n))
        out_shape.append(_sds((T, half * n), MM_DTYPE))
    return pl.pallas_call(
        body, name=f"norm_mm_in_{tag}", grid=(T // tm,),
        in_specs=[_rows(tm, D), _whole((1, D)), _resident((N_CHIP, D, n))], out_specs=out_specs,
        out_shape=out_shape, compiler_params=_params())(x, g.reshape(1, D), wg)


def _mm_out_post(a, b, x, g, alpha, tag):
    T, K = a.shape
    D = b.shape[1]
    tm = _tile(T, FUSED_TILE)

    def body(a_ref, b_ref, x_ref, g_ref, h_ref, o_ref):
        hv = jnp.dot(a_ref[...], b_ref[...], preferred_element_type=F32)
        h_ref[...] = hv
        r = lax.rsqrt(jnp.mean(hv * hv, axis=-1, keepdims=True) + RMS_EPS)
        o_ref[...] = x_ref[...] + alpha * (hv * r * g_ref[...])

    return pl.pallas_call(
        body, name=f"mm_out_post_{tag}", grid=(T // tm,),
        in_specs=[_rows(tm, K), _resident((K, D)), _rows(tm, D), _whole((1, D))],
        out_specs=[_rows(tm, D), _rows(tm, D)], out_shape=[_sds((T, D), F32)] * 2,
        compiler_params=_params())(a, b, x, g.reshape(1, D))


def _post_bwd_mm(dx, h, g, alpha, b, tag, hgu=None):
    T, D = dx.shape
    K = b.shape[0]
    tm = _tile(T, FUSED_TILE)

    def body(dx_ref, h_ref, g_ref, b_ref, *rest):
        dh_ref, dg_ref, out_ref = rest[-3:]
        hv = h_ref[...]
        r = lax.rsqrt(jnp.mean(hv * hv, axis=-1, keepdims=True) + RMS_EPS)
        hh = hv * r
        dyn = alpha * dx_ref[...]
        _accumulate(dg_ref, jnp.sum(dyn * hh, axis=0, keepdims=True), pl.program_id(0) == 0)
        dhh = dyn * g_ref[...]
        dh = (r * (dhh - hh * jnp.mean(dhh * hh, axis=-1, keepdims=True))).astype(dh_ref.dtype)
        dh_ref[...] = dh
        da = lax.dot_general(dh, b_ref[...], NT, preferred_element_type=F32)
        if hgu is None:
            out_ref[...] = da.astype(out_ref.dtype)
        else:
            gate = rest[0][:, :K].astype(F32)
            up = rest[0][:, K:].astype(F32)
            sg = jax.nn.sigmoid(gate)
            out_ref[:, :K] = (da * up * sg * (1.0 + gate * (1.0 - sg))).astype(out_ref.dtype)
            out_ref[:, K:] = (da * gate * sg).astype(out_ref.dtype)

    in_specs = [_rows(tm, D), _rows(tm, D), _whole((1, D)), _resident((K, D))]
    args = [dx, h, g.reshape(1, D), b]
    wide = K
    if hgu is not None:
        wide = 2 * K
        in_specs.append(_rows(tm, wide))
        args.append(hgu)
    return pl.pallas_call(
        body, name=f"post_bwd_mm_{tag}", grid=(T // tm,), in_specs=in_specs,
        out_specs=[_rows(tm, D), _whole((1, D)), _rows(tm, wide)],
        out_shape=[_sds((T, D), MM_DTYPE), _sds((1, D), F32), _sds((T, wide), MM_DTYPE)],
        compiler_params=_params())(*args)


def _mm_nt_pre(dy, w, dres, x, g, tag):
    T, C = dy.shape
    D = x.shape[1]
    tm = _tile(T, FUSED_TILE)
    n = w.shape[-1]

    def body(dy_ref, w_ref, dres_ref, x_ref, g_ref, dx_ref, dg_ref):
        if w.ndim == 2:
            dn = lax.dot_general(dy_ref[...], w_ref[...], NT, preferred_element_type=F32)
        else:
            dn = None
            for s in range(N_CHIP):
                part = lax.dot_general(dy_ref[:, s * n:(s + 1) * n], w_ref[s], NT, preferred_element_type=F32)
                dn = part if dn is None else dn + part
        xv = x_ref[...]
        r = lax.rsqrt(jnp.mean(xv * xv, axis=-1, keepdims=True) + RMS_EPS)
        xh = xv * r
        _accumulate(dg_ref, jnp.sum(dn * xh, axis=0, keepdims=True), pl.program_id(0) == 0)
        dxh = dn * g_ref[...]
        dx_ref[...] = dres_ref[...] + r * (dxh - xh * jnp.mean(dxh * xh, axis=-1, keepdims=True))

    return pl.pallas_call(
        body, name=f"mm_nt_pre_{tag}", grid=(T // tm,),
        in_specs=[_rows(tm, C), _resident(w.shape), _rows(tm, D), _rows(tm, D), _whole((1, D))],
        out_specs=[_rows(tm, D), _whole((1, D))], out_shape=[_sds((T, D), F32), _sds((1, D), F32)],
        compiler_params=_params())(dy, w, dres, x, g.reshape(1, D))


def _adamw(w, g, m, v, tag, after=None):
    R, C = w.shape
    tr = _tile(R, ROW_TILE)
    extra = [] if after is None else [after]

    def body(w_ref, g_ref, m_ref, v_ref, *rest):
        go_ref, d_ref, mo_ref, vo_ref = rest[-4:]
        gv = g_ref[...]
        go_ref[...] = gv
        mn = ADAM_B1 * m_ref[...] + (1.0 - ADAM_B1) * gv
        vn = ADAM_B2 * v_ref[...] + (1.0 - ADAM_B2) * (gv * gv)
        m_hat = mn / (1.0 - ADAM_B1 ** ADAM_STEP)
        v_hat = vn / (1.0 - ADAM_B2 ** ADAM_STEP)
        d_ref[...] = -ADAM_LR * (m_hat / (jnp.sqrt(v_hat) + ADAM_EPS) + ADAM_WD * w_ref[...])
        mo_ref[...] = mn
        vo_ref[...] = vn

    return pl.pallas_call(
        body, name=f"adamw_{tag}", grid=(R // tr,),
        in_specs=[_rows(tr, C)] * 4 + [ANY] * len(extra), out_specs=[_rows(tr, C)] * 4,
        out_shape=[_sds((R, C), F32)] * 4, compiler_params=_params())(w, g, m, v, *extra)


def _sum_devices(gall, own, place):
    _, R, C = gall.shape

    def body(place_ref, g_ref, s_ref, o_ref):
        me = 2 * place_ref[1] + place_ref[0]
        acc = None
        for d in range(N_DEV):
            term = jnp.where(me == d, s_ref[...], g_ref[d])
            acc = term if acc is None else acc + term
        o_ref[...] = acc

    grid_spec = pltpu.PrefetchScalarGridSpec(
        num_scalar_prefetch=1, grid=(1,),
        in_specs=[pl.BlockSpec((N_DEV, R, C), lambda i, p: (0, 0, 0)), pl.BlockSpec((R, C), lambda i, p: (0, 0))],
        out_specs=pl.BlockSpec((R, C), lambda i, p: (0, 0)))
    return pl.pallas_call(
        body, name="sum_devices", grid_spec=grid_spec, out_shape=_sds((R, C), F32),
        compiler_params=_params())(place, gall, own)


HBM = pl.BlockSpec(memory_space=pltpu.HBM)
SEM = pl.BlockSpec(memory_space=pltpu.SEMAPHORE)
EFFECT = pltpu.SideEffectType.DATAFLOW_SIDE_EFFECTING


def _place():
    x, y, c = lax.axis_index("x"), lax.axis_index("y"), lax.axis_index("c")
    chips = ((1 - x, y), (x, 1 - y), (1 - x, 1 - y))
    return x, y, c, chips


def _remote(src, dst, send_sem, recv_sem, dev):
    return pltpu.make_async_remote_copy(src_ref=src, dst_ref=dst, send_sem=send_sem, recv_sem=recv_sem,
                                        device_id=dev, device_id_type=MESH)


def _in_hbm(a):
    return pltpu.with_memory_space_constraint(a, pltpu.HBM)


def _own_slot(w4, l, dtype, place, tag):
    _, _, r, col = w4.shape
    tr = _tile(r, 2 * ROW_TILE)

    def body(place_ref, x_ref, o_ref):
        o_ref[...] = x_ref[...].astype(o_ref.dtype)

    grid_spec = pltpu.PrefetchScalarGridSpec(
        num_scalar_prefetch=1, grid=(2, r // tr),
        in_specs=[pl.BlockSpec((None, None, tr, col), lambda h, i, p: (l, h, i, 0))],
        out_specs=pl.BlockSpec((None, None, tr, col), lambda h, i, p: (p[1], h, i, 0)))
    return pl.pallas_call(
        body, name=f"own_slot_{tag}", grid_spec=grid_spec, out_shape=_sds((N_CHIP, 2, r, col), dtype),
        compiler_params=_params())(place, w4)


def _gather_start(bufs, after, tag):
    n = len(bufs)

    def body(*refs):
        ins = refs[:n]
        s_sem, r_sem, token = refs[n + 1], refs[n + 2], refs[2 * n + 3]
        x, y, c, chips = _place()
        me = 2 * x + y
        for i in range(n):
            mine = ins[i].at[me, c]
            for j, (px, py) in enumerate(chips):
                _remote(mine, mine, s_sem.at[3 * i + j], r_sem.at[3 * i + j], (px, py, c)).start()
        token[...] = jnp.zeros_like(token)

    dma = pltpu.SemaphoreType.DMA
    res = pl.pallas_call(
        body, name=f"gather_start_{tag}", in_specs=[HBM] * n + [ANY],
        out_specs=[SEM, SEM] + [HBM] * n + [pl.BlockSpec(memory_space=pltpu.VMEM)],
        out_shape=[dma((3 * n,)), dma((3 * n,))] + [pltpu.HBM(b.shape, b.dtype) for b in bufs] + [_sds((8, LANES), F32)],
        input_output_aliases={i: i + 2 for i in range(n)},
        compiler_params=pltpu.CompilerParams(has_side_effects=EFFECT),
        )(*[_in_hbm(b) for b in bufs], after)
    return res[0], res[1], list(res[2:2 + n]), res[-1]


def _gather_pass(s_sem, r_sem, bufs, first, after, tag):
    n = len(bufs)

    def body(*refs):
        ins = refs[:n]
        a_s, a_r, b_s, b_r = refs[n], refs[n + 1], refs[n + 3], refs[n + 4]
        x, y, c, chips = _place()
        me = 2 * x + y
        sib = (x, y, 1 - c)
        for i in range(n):
            mine = ins[i].at[me, c]
            for j, (px, py) in enumerate(chips):
                k = 3 * (first + i) + j
                _remote(mine, mine, a_s.at[k], a_r.at[k], (px, py, c)).wait_send()
        for j, (px, py) in enumerate(chips):
            for i in range(n):
                k = 3 * (first + i) + j
                blk = ins[i].at[2 * px + py, c]
                _remote(blk, blk, a_s.at[k], a_r.at[k], (px, py, c)).wait_recv()
                _remote(blk, blk, b_s.at[3 * i + j], b_r.at[3 * i + j], sib).start()

    dma = pltpu.SemaphoreType.DMA
    res = pl.pallas_call(
        body, name=f"gather_pass_{tag}", in_specs=[HBM] * n + [SEM, SEM, ANY],
        out_specs=[SEM, SEM] + [HBM] * n,
        out_shape=[dma((3 * n,)), dma((3 * n,))] + [pltpu.HBM(b.shape, b.dtype) for b in bufs],
        input_output_aliases={i: i + 2 for i in range(n)},
        compiler_params=pltpu.CompilerParams(has_side_effects=EFFECT),
        )(*bufs, s_sem, r_sem, after)
    return res[0], res[1], list(res[2:])


def _gather_land(s_sem, r_sem, bufs, tag):
    n = len(bufs)

    def body(*refs):
        ins = refs[:n]
        b_s, b_r = refs[n], refs[n + 1]
        x, y, c, chips = _place()
        sib = (x, y, 1 - c)
        for j, (px, py) in enumerate(chips):
            for i in range(n):
                sent = ins[i].at[2 * px + py, c]
                got = ins[i].at[2 * px + py, 1 - c]
                _remote(sent, sent, b_s.at[3 * i + j], b_r.at[3 * i + j], sib).wait_send()
                _remote(got, got, b_s.at[3 * i + j], b_r.at[3 * i + j], sib).wait_recv()

    return list(pl.pallas_call(
        body, name=f"gather_land_{tag}", in_specs=[HBM] * n + [SEM, SEM], out_specs=[HBM] * n,
        out_shape=[pltpu.HBM(b.shape, b.dtype) for b in bufs],
        input_output_aliases={i: i for i in range(n)},
        compiler_params=pltpu.CompilerParams(has_side_effects=EFFECT),
        )(*bufs, s_sem, r_sem))


def _rs_pair_add(g, recv, place, tag):
    r, col = g.shape[-2:]
    tr = _tile(r, ROW_TILE)

    def body(place_ref, g_ref, r_ref, wire_ref, own_ref):
        tot = g_ref[...] + r_ref[...]
        wire_ref[...] = tot.astype(wire_ref.dtype)

        @pl.when(pl.program_id(1) == place_ref[1])
        def _():
            own_ref[...] = tot

    grid_spec = pltpu.PrefetchScalarGridSpec(
        num_scalar_prefetch=1, grid=(r // tr, N_CHIP),
        in_specs=[pl.BlockSpec((None, None, tr, col), lambda i, s, p: (s, p[0], i, 0)),
                  pl.BlockSpec((None, tr, col), lambda i, s, p: (s, i, 0))],
        out_specs=[pl.BlockSpec((None, tr, col), lambda i, s, p: (s, i, 0)),
                   pl.BlockSpec((tr, col), lambda i, s, p: (i, 0))])
    return pl.pallas_call(
        body, name=f"rs_pair_add_{tag}", grid_spec=grid_spec,
        out_shape=[_sds((N_CHIP, r, col), WIRE_DTYPE), _sds((r, col), F32)],
        compiler_params=_params())(place, g, recv)


def _pair_plan(srcs, lands):
    x, y, c, _ = _place()
    return [(s.at[:, 1 - c], l, (x, y, 1 - c)) for s, l in zip(srcs, lands)]


def _chip_plan(srcs, lands):
    x, y, c, chips = _place()
    plan = []
    for s, l in zip(srcs, lands):
        if len(s.shape) == 2:
            me = 4 * x + 2 * y + c
            plan += [(s, l.at[me], (x ^ (k >> 2), y ^ ((k >> 1) & 1), c ^ (k & 1))) for k in range(1, N_DEV)]
        else:
            plan += [(s.at[2 * px + py], l.at[j], (px, py, c)) for j, (px, py) in enumerate(chips)]
    return plan


def _exchange_start(srcs, lands, plan, count, tag):
    n = len(srcs)
    both = list(srcs) + list(lands)

    def body(*refs):
        s_sem, r_sem, token = refs[2 * n], refs[2 * n + 1], refs[4 * n + 2]
        for k, (src, dst, dev) in enumerate(plan(refs[:n], refs[n:2 * n])):
            _remote(src, dst, s_sem.at[k], r_sem.at[k], dev).start()
        token[...] = jnp.zeros_like(token)

    dma = pltpu.SemaphoreType.DMA
    res = pl.pallas_call(
        body, name=f"exchange_start_{tag}", in_specs=[HBM] * (2 * n),
        out_specs=[SEM, SEM] + [HBM] * (2 * n) + [pl.BlockSpec(memory_space=pltpu.VMEM)],
        out_shape=[dma((count,)), dma((count,))] + [pltpu.HBM(b.shape, b.dtype) for b in both] + [_sds((8, LANES), F32)],
        input_output_aliases={i: i + 2 for i in range(2 * n)},
        compiler_params=pltpu.CompilerParams(has_side_effects=EFFECT),
        )(*[_in_hbm(b) for b in both])
    return res[0], res[1], list(res[2:2 + n]), list(res[2 + n:2 + 2 * n]), res[-1]


def _exchange_wait(s_sem, r_sem, srcs, lands, plan, after, tag):
    n = len(srcs)

    def body(*refs):
        s_ref, r_ref = refs[2 * n], refs[2 * n + 1]
        for k, (src, dst, dev) in enumerate(plan(refs[:n], refs[n:2 * n])):
            cp = _remote(src, dst, s_ref.at[k], r_ref.at[k], dev)
            cp.wait_send()
            cp.wait_recv()

    both = list(srcs) + list(lands)
    res = pl.pallas_call(
        body, name=f"exchange_wait_{tag}", in_specs=[HBM] * (2 * n) + [SEM, SEM, ANY], out_specs=[HBM] * (2 * n),
        out_shape=[pltpu.HBM(b.shape, b.dtype) for b in both],
        input_output_aliases={i: i for i in range(2 * n)},
        compiler_params=pltpu.CompilerParams(has_side_effects=EFFECT),
        )(*both, s_sem, r_sem, after)
    return list(res[:n]), list(res[n:])


def _rs_chip_add(own, recv, place, l, L, prev, tag):
    r, col = own.shape
    tr = _tile(r, ROW_TILE)

    def body(place_ref, o_ref, r_ref, *rest):
        acc = o_ref[...]
        for j in range(3):
            acc = acc + r_ref[j].astype(F32)
        rest[-1][...] = acc

    in_specs = [pl.BlockSpec((tr, col), lambda i, p: (i, 0)), pl.BlockSpec((3, tr, col), lambda i, p: (0, i, 0))]
    args = [place, own, recv]
    kw = {}
    if prev is not None:
        in_specs.append(ANY)
        args.append(prev)
        kw["input_output_aliases"] = {3: 0}
    grid_spec = pltpu.PrefetchScalarGridSpec(
        num_scalar_prefetch=1, grid=(r // tr,), in_specs=in_specs,
        out_specs=pl.BlockSpec((None, None, tr, col), lambda i, p: (l, p[0], i, 0)))
    return pl.pallas_call(
        body, name=f"rs_chip_add_{tag}", grid_spec=grid_spec, out_shape=_sds((L, 2, r, col), F32),
        compiler_params=_params(), **kw)(*args)


def _rs_pair_share(fulls, tag):
    n = len(fulls)

    def body(*refs):
        outs = refs[n:2 * n]
        s_sem, r_sem = refs[2 * n:]
        x, y, c, _ = _place()
        sib = (x, y, 1 - c)
        started = []
        for i in range(n):
            cp = _remote(outs[i].at[:, c], outs[i].at[:, c], s_sem.at[i], r_sem.at[i], sib)
            cp.start()
            started.append(cp)
        for i, cp in enumerate(started):
            cp.wait_send()
            _remote(outs[i].at[:, 1 - c], outs[i].at[:, 1 - c], s_sem.at[i], r_sem.at[i], sib).wait_recv()

    dma = pltpu.SemaphoreType.DMA
    return pl.pallas_call(
        body, name=f"rs_pair_share_{tag}", in_specs=[ANY] * n, out_specs=[ANY] * n,
        out_shape=[_sds(f.shape, f.dtype) for f in fulls],
        input_output_aliases={i: i for i in range(n)},
        scratch_shapes=[dma((n,)), dma((n,))],
        )(*fulls)


def _ffn_fwd(x, g_pre, g_post, w_in, w_out, tag):
    xn, hgu, act = _norm_mm_in(x, g_pre, w_in, tag, swiglu=True)
    if callable(w_out):
        w_out = w_out(act)
    h, x_out = _mm_out_post(act, w_out.reshape(-1, w_out.shape[-1]), x, g_post, 0.5, tag)
    return x_out, (x, xn, hgu, act, h)


def _ffn_bwd(dx, saved, g_pre, g_post, w_in, w_out, tag, between=None, finish=None):
    x, xn, hgu, act, h = saved
    dh, dg_post, dhgu = _post_bwd_mm(dx, h, g_post, 0.5, w_out.reshape(-1, w_out.shape[-1]), tag, hgu=hgu)
    token = None
    if between is not None:
        token = between(dhgu)
    dw_out = _mm_tn_out(act, dh, tag, after=token)
    dw_in = _mm_tn_in(xn, dhgu, tag, after=token)
    if finish is not None:
        token = finish(dw_in, dw_out)
    if token is not None:
        g_pre = g_pre + token[0, :1]
    dx_in, dg_pre = _mm_nt_pre(dhgu, w_in, dx, x, g_pre, tag)
    return dx_in, dg_pre, dg_post, dw_in, dw_out


def kernel(x, ffn1_pre_g, ffn1_post_g, ffn1_w_in, ffn1_w_out, mix_pre_g, mix_post_g, ffn2_pre_g, ffn2_post_g, ffn2_w_in, ffn2_w_out, conv_w_in, conv_k, conv_w_out, kv_g, kv_w, forget_b, attn_w_qg, attn_w_o, loss_target, m_ffn1_pre_g, m_ffn1_post_g, m_ffn1_w_in, m_ffn1_w_out, m_mix_pre_g, m_mix_post_g, m_ffn2_pre_g, m_ffn2_post_g, m_ffn2_w_in, m_ffn2_w_out, m_conv_w_in, m_conv_k, m_conv_w_out, m_kv_g, m_kv_w, m_forget_b, m_attn_w_qg, m_attn_w_o, v_ffn1_pre_g, v_ffn1_post_g, v_ffn1_w_in, v_ffn1_w_out, v_mix_pre_g, v_mix_post_g, v_ffn2_pre_g, v_ffn2_post_g, v_ffn2_w_in, v_ffn2_w_out, v_conv_w_in, v_conv_k, v_conv_w_out, v_kv_g, v_kv_w, v_forget_b, v_attn_w_qg, v_attn_w_o):
    Bl, S, D = x.shape
    T = Bl * S
    H = forget_b.shape[0]
    assert D == H * HEAD_DIM and D % LANES == 0
    kvc = kv_w.shape[1]
    kvp = -(-kvc // LANES) * LANES
    kv_all = 2 * D + LANES
    dk_cols = conv_k.shape[2]
    chip = 2 * lax.axis_index("x") + lax.axis_index("y")
    core = lax.axis_index("c")

    given = dict(ffn1_w_in=(ffn1_w_in, m_ffn1_w_in, v_ffn1_w_in), ffn1_w_out=(ffn1_w_out, m_ffn1_w_out, v_ffn1_w_out),
                 ffn2_w_in=(ffn2_w_in, m_ffn2_w_in, v_ffn2_w_in), ffn2_w_out=(ffn2_w_out, m_ffn2_w_out, v_ffn2_w_out),
                 conv_w_in=(conv_w_in, m_conv_w_in, v_conv_w_in), conv_w_out=(conv_w_out, m_conv_w_out, v_conv_w_out),
                 kv_w=(kv_w, m_kv_w, v_kv_w), attn_w_qg=(attn_w_qg, m_attn_w_qg, v_attn_w_qg),
                 attn_w_o=(attn_w_o, m_attn_w_o, v_attn_w_o))
    shards = {k: w for k, (w, _, _) in given.items()}
    shards["kv_w"] = jnp.pad(kv_w, ((0, 0), (0, kvp - kvc)))[None]
    groups = [[("ffn1_w_in", 0), ("ffn1_w_out", 0)], [("conv_w_in", 0), ("conv_w_out", 0)],
              [("ffn2_w_in", 0), ("ffn2_w_out", 0)], [("kv_w", 0), ("ffn1_w_in", 1), ("ffn1_w_out", 1)],
              [("attn_w_qg", 0), ("attn_w_o", 0), ("ffn2_w_in", 1), ("ffn2_w_out", 1)]]
    second = groups[3] + groups[4]
    place = jnp.stack([core, chip]).astype(jnp.int32)

    def slot(key, where):
        w = shards[key[0]]
        L, r, col = w.shape
        return _own_slot(w.reshape(L, 2, r // 2, col), key[1], MM_DTYPE, where, f"{key[0]}{key[1]}")

    def whole(g):
        return g.reshape(N_CHIP, -1, g.shape[-1])

    taps_slot = _own_slot(jnp.pad(conv_k[0], ((0, 13), (0, 0))).reshape(1, 2, 8, dk_cols), 0, F32, place, "conv_k")
    fb = jnp.pad(forget_b, (0, LANES - H)).reshape(1, LANES)
    w_in0, w_out0 = groups[0]
    s_0, r_0, fly_0, token = _gather_start([slot(w_in0, place), taps_slot, slot(w_out0, place)], fb, "first")
    later = groups[1] + groups[2] + groups[3] + groups[4]
    s_1, r_1, fly_1, token = _gather_start([slot(key, place) for key in later], token, "rest")
    W = {}

    def land(sems, bufs, lo, after, tag):
        return _gather_land(*_gather_pass(*sems, bufs, lo, after, tag), tag)

    def arrive(g, after):
        lo = sum(len(groups[k]) for k in range(1, g))
        got = land((s_1, r_1), fly_1[lo:lo + len(groups[g])], lo, after, f"g{g}")
        W.update({key: whole(b) for key, b in zip(groups[g], got)})

    w_first, taps = land((s_0, r_0), fly_0[:2], 0, token, "g0")
    k_taps = taps.reshape(N_CHIP, 16, dk_cols).transpose(1, 0, 2).reshape(16, D)[:8]

    x0 = x.reshape(T, D)
    W[w_in0] = whole(w_first)

    def first_w_out(act):
        W[w_out0] = whole(land((s_0, r_0), fly_0[2:], 2, act, "g0_out")[0])
        return W[w_out0]

    x1, s_f1a = _ffn_fwd(x0, ffn1_pre_g[0], ffn1_post_g[0], W[w_in0], first_w_out, "l0f1")
    arrive(1, x1)
    w_o_conv = W["conv_w_out", 0].reshape(D, D)
    xn_c, bch = _norm_mm_in(x1, mix_pre_g[0], W["conv_w_in", 0], "conv")
    z_c = _conv_fwd(bch, k_taps, Bl, S)
    m_c, x2 = _mm_out_post(z_c, w_o_conv, x1, mix_post_g[0], 1.0, "conv_out")
    arrive(2, x2)
    x3, s_f2a = _ffn_fwd(x2, ffn2_pre_g[0], ffn2_post_g[0], W["ffn2_w_in", 0], W["ffn2_w_out", 0], "l0f2")

    arrive(3, x3)
    kv_full = jnp.concatenate([W["kv_w", 0][s, :, :kvc] for s in range(N_CHIP)], axis=1)
    kv_full = jnp.pad(kv_full, ((0, 0), (0, kv_all - kv_full.shape[1])))
    xn_kv = _rms_fwd(x3, kv_g, "kv")
    kvact = _mm_nn(xn_kv, kv_full[:, :2 * D], MM_DTYPE, "kv")
    pf = _mm_nn(xn_kv, kv_full[:, 2 * D:], F32, "forget")
    cum = _forget_fwd(pf, fb, Bl, S)
    bq = min(S, ATT_BLOCK)
    c3 = cum.reshape(Bl, S, LANES)[:, :, :H].transpose(0, 2, 1)
    c_col = jnp.broadcast_to(c3[..., None], (Bl, H, S, LANES))
    c_row = c3.reshape(Bl, H, S // bq, 1, bq)

    x4, s_f1b = _ffn_fwd(x3, ffn1_pre_g[1], ffn1_post_g[1], W["ffn1_w_in", 1], W["ffn1_w_out", 1], "l1f1")
    arrive(4, x4)
    w_o_attn = W["attn_w_o", 0].reshape(D, D)
    xn_a, qg = _norm_mm_in(x4, mix_pre_g[1], W["attn_w_qg", 0], "qg")
    o, lse, z_a = _attn_fwd(qg, kvact, c_col, c_row, Bl, S, D)
    m_a, x5 = _mm_out_post(z_a, w_o_attn, x4, mix_post_g[1], 1.0, "attn_out")
    x6, s_f2b = _ffn_fwd(x5, ffn2_pre_g[1], ffn2_post_g[1], W["ffn2_w_in", 1], W["ffn2_w_out", 1], "l1f2")

    dy, loss_local = _loss_grad(x6, loss_target.reshape(T, D))

    G = {}
    dx5, dg_f2pre_1, dg_f2post_1, G["ffn2_w_in", 1], G["ffn2_w_out", 1] = _ffn_bwd(
        dy, s_f2b, ffn2_pre_g[1], ffn2_post_g[1], W["ffn2_w_in", 1], W["ffn2_w_out", 1], "l1f2")
    dm_a, dg_mixpost_1, dz_a = _post_bwd_mm(dx5, m_a, mix_post_g[1], 1.0, w_o_attn, "attn_out")
    G["attn_w_o", 0] = _mm_tn_out(z_a, dm_a, "attn_out")
    dq, dk, dv, dcr = _attn_bwd(qg, kvact, dz_a, lse, c_col, c_row, Bl, S, D)
    dqg = _gate_bwd(dz_a, qg, o, dq)
    G["attn_w_qg", 0] = _mm_tn_in(xn_a, dqg, "qg")
    dx4, dg_mixpre_1 = _mm_nt_pre(dqg, W["attn_w_qg", 0], dx5, x4, mix_pre_g[1], "qg")
    dx3, dg_f1pre_1, dg_f1post_1, G["ffn1_w_in", 1], G["ffn1_w_out", 1] = _ffn_bwd(
        dx4, s_f1b, ffn1_pre_g[1], ffn1_post_g[1], W["ffn1_w_in", 1], W["ffn1_w_out", 1], "l1f1")

    dcum = jnp.pad(dcr.reshape(Bl, H, S).transpose(0, 2, 1), ((0, 0), (0, 0), (0, LANES - H))).reshape(T, LANES)
    dpf, dfb = _forget_bwd(dcum, pf, fb, Bl, S)
    dp = jnp.concatenate([dk.astype(MM_DTYPE), dv.astype(MM_DTYPE), dpf], axis=1)
    G_kv_full = _mm_tn(xn_kv, dp, "kv")
    G["kv_w", 0] = jnp.stack([jnp.pad(G_kv_full[:, s * kvc:(s + 1) * kvc], ((0, 0), (0, kvp - kvc))) for s in range(N_CHIP)])
    dx3, dg_kv = _mm_nt_pre(dp, kv_full, dx3, x3, kv_g, "kv")

    def halves_of(keys):
        return [G[k].reshape(N_CHIP, 2, G[k].shape[1] // 2, G[k].shape[2]) for k in keys]

    def pair_adds(keys, grads, recvs):
        wires, owns = [], []
        for k, g, r in zip(keys, grads, recvs):
            w, own = _rs_pair_add(g, r, place, f"{k[0]}{k[1]}")
            wires.append(w)
            owns.append(own)
        return wires, owns

    def chip_start(wires, tag, extra=()):
        lands = [lax.empty((3,) + w.shape[1:], w.dtype) for w in wires]
        lands += [jnp.zeros((N_DEV,) + e.shape, e.dtype) for e in extra]
        return _exchange_start(list(wires) + list(extra), lands, _chip_plan, 3 * len(wires) + (N_DEV - 1) * len(extra), tag)

    late = groups[2] + groups[1]
    last = groups[0]
    grads_2 = halves_of(second)
    p_sems, p_semr, grads_2, sib_2, token = _exchange_start(
        grads_2, [lax.empty((N_CHIP,) + g.shape[2:], g.dtype) for g in grads_2], _pair_plan, len(grads_2), "pair_second")

    dx2, dg_f2pre_0, dg_f2post_0, G["ffn2_w_in", 0], G["ffn2_w_out", 0] = _ffn_bwd(
        dx3, s_f2a, ffn2_pre_g[0], ffn2_post_g[0] + token[0, :1], W["ffn2_w_in", 0], W["ffn2_w_out", 0], "l0f2")
    grads_2, sib_2 = _exchange_wait(p_sems, p_semr, grads_2, sib_2, _pair_plan, dx2, "pair_second")
    wires_2, owns_2 = pair_adds(second, grads_2, sib_2)
    c_2 = chip_start(wires_2, "chip_second")
    dm_c, dg_mixpost_0, dz_c = _post_bwd_mm(dx2, m_c, mix_post_g[0] + c_2[4][0, :1], 1.0, w_o_conv, "conv_out")
    G["conv_w_out", 0] = _mm_tn_out(z_c, dm_c, "conv_out")
    db, dcg, dhh, dk_taps = _conv_bwd(bch, dz_c, k_taps, Bl, S)
    dbch = jnp.concatenate([db, dcg, dhh], axis=1)
    G["conv_w_in", 0] = _mm_tn_in(xn_c, dbch, "conv")
    dx1, dg_mixpre_0 = _mm_nt_pre(dbch, W["conv_w_in", 0], dx2, x1, mix_pre_g[0], "conv")
    def pair_start(keys, tag):
        grads = halves_of(keys)
        lands = [lax.empty((N_CHIP,) + g.shape[2:], g.dtype) for g in grads]
        return _exchange_start(grads, lands, _pair_plan, len(grads), tag)

    p_l = pair_start(late, "pair_late")
    late_done = {}

    def late_leg(dhgu):
        grads_l, sib_l = _exchange_wait(*p_l[:4], _pair_plan, dhgu, "pair_late")
        late_done["wires"], late_done["owns"] = pair_adds(late, grads_l, sib_l)
        late_done["chip"] = chip_start(late_done["wires"], "chip_late")
        return late_done["chip"][4]

    def last_pair(dw_in, dw_out):
        G["ffn1_w_in", 0], G["ffn1_w_out", 0] = dw_in, dw_out
        late_done["pair"] = pair_start(last, "pair_last")
        return late_done["pair"][4]

    dx0, dg_f1pre_0, dg_f1post_0, _, _ = _ffn_bwd(
        dx1, s_f1a, ffn1_pre_g[0], ffn1_post_g[0] + p_l[4][0, :1], W["ffn1_w_in", 0], W["ffn1_w_out", 0], "l0f1",
        between=late_leg, finish=last_pair)
    grad_x = dx0.reshape(Bl, S, D)
    owns_l, c_l, p_1 = late_done["owns"], late_done["chip"], late_done["pair"]

    grads_1, sib_1 = _exchange_wait(*p_1[:4], _pair_plan, dx0, "pair_last")
    wires_1, owns_1 = pair_adds(last, grads_1, sib_1)

    def row(v):
        return jnp.pad(v.reshape(-1), (0, D - v.size)).reshape(1, D)

    small_parts = [dg_f1pre_0, dg_f1pre_1, dg_f1post_0, dg_f1post_1, dg_mixpre_0, dg_mixpre_1, dg_mixpost_0, dg_mixpost_1,
                   dg_f2pre_0, dg_f2pre_1, dg_f2post_0, dg_f2post_1, dg_kv, row(dfb[0, :H]), dk_taps[:3],
                   jnp.full((1, D), loss_local)]
    small = jnp.concatenate(small_parts, axis=0)
    small = jnp.pad(small, ((0, SMALL_ROWS - small.shape[0]), (0, 0)))
    c_1 = chip_start(wires_1, "chip_last", extra=[small])
    _, recvs_2 = _exchange_wait(*c_2[:4], _chip_plan, c_1[4], "chip_second")
    _, recvs_l = _exchange_wait(*c_l[:4], _chip_plan, c_1[4], "chip_late")
    partial = {}

    def chip_adds(keys, owns, recvs):
        for (name, l), own, rcv in zip(keys, owns, recvs):
            partial[name] = _rs_chip_add(own, rcv, place, l, shards[name].shape[0], partial.get(name), f"{name}{l}")

    chip_adds(late + second, owns_l + owns_2, recvs_l + recvs_2)
    res = {}

    def adamw(names, reduced, after):
        for k, red in zip(names, reduced):
            w, m, v = given[k]
            g2 = red.reshape(-1, red.shape[-1])
            if k == "kv_w":
                g2 = g2[:, :kvc]
            flat = lambda a: a.reshape(-1, a.shape[-1])
            go, d, mn, vn = _adamw(flat(w), g2, flat(m), flat(v), k, after=after)
            res[k] = tuple(a.reshape(w.shape) for a in (go, d, mn, vn))
        return d

    early = [k for k in partial if (k, 0) not in last]
    done = adamw(early, _rs_pair_share([partial[k] for k in early], "early"), c_1[4])
    _, recvs_1 = _exchange_wait(*c_1[:4], _chip_plan, done, "chip_last")
    chip_adds(last, owns_1, recvs_1[:-1])
    rest = [k for k, _ in last]
    adamw(rest, _rs_pair_share([partial[k] for k in rest], "last"), None)
    gsum = _sum_devices(recvs_1[-1], small, place)
    loss = gsum[17, 0]

    small_names = ["ffn1_pre_g", "ffn1_post_g", "mix_pre_g", "mix_post_g", "ffn2_pre_g", "ffn2_post_g"]
    small_given = dict(ffn1_pre_g=(ffn1_pre_g, m_ffn1_pre_g, v_ffn1_pre_g), ffn1_post_g=(ffn1_post_g, m_ffn1_post_g, v_ffn1_post_g),
                       mix_pre_g=(mix_pre_g, m_mix_pre_g, v_mix_pre_g), mix_post_g=(mix_post_g, m_mix_post_g, v_mix_post_g),
                       ffn2_pre_g=(ffn2_pre_g, m_ffn2_pre_g, v_ffn2_pre_g), ffn2_post_g=(ffn2_post_g, m_ffn2_post_g, v_ffn2_post_g))

    def pack(idx):
        rows_ = [small_given[k][idx] for k in small_names]
        rows_ += [row((kv_g, m_kv_g, v_kv_g)[idx]), row((forget_b, m_forget_b, v_forget_b)[idx])]
        rows_.append(jnp.pad((conv_k, m_conv_k, v_conv_k)[idx][0], ((0, 0), (0, D - dk_cols))))
        a = jnp.concatenate(rows_, axis=0)
        return jnp.pad(a, ((0, SMALL_ROWS - a.shape[0]), (0, 0)))

    g_taps = lax.dynamic_slice_in_dim(gsum[14:17], chip * dk_cols, dk_cols, axis=1)
    g_small = jnp.concatenate([gsum[:14], jnp.pad(g_taps, ((0, 0), (0, D - dk_cols))), gsum[17:]], axis=0)
    g_small, d_s, m_s, v_s = _adamw(pack(0), g_small, pack(1), pack(2), "small")
    for i, k in enumerate(small_names):
        res[k] = tuple(a[2 * i:2 * i + 2] for a in (g_small, d_s, m_s, v_s))
    res["kv_g"] = tuple(a[12] for a in (g_small, d_s, m_s, v_s))
    res["forget_b"] = tuple(a[13, :H] for a in (g_small, d_s, m_s, v_s))
    res["conv_k"] = tuple(a[14:17, :dk_cols][None] for a in (g_small, d_s, m_s, v_s))

    order = ["ffn1_pre_g", "ffn1_post_g", "ffn1_w_in", "ffn1_w_out", "mix_pre_g", "mix_post_g", "ffn2_pre_g", "ffn2_post_g",
             "ffn2_w_in", "ffn2_w_out", "conv_w_in", "conv_k", "conv_w_out", "kv_g", "kv_w", "forget_b", "attn_w_qg", "attn_w_o"]
    out = [loss, grad_x]
    for idx in range(4):
        out += [res[k][idx] for k in order]
    return tuple(out)
```

```python
import math

import jax
import jax.numpy as jnp
from jax import lax
from jax.experimental import pallas as pl
from jax.experimental.pallas import tpu as pltpu

F32 = jnp.float32
MM_DTYPE = jnp.bfloat16
WIRE_DTYPE = jnp.bfloat16

RMS_EPS = 1e-6
ADAM_LR = 0.001
ADAM_B1 = 0.9
ADAM_B2 = 0.999
ADAM_EPS = 1e-08
ADAM_WD = 0.01
ADAM_STEP = 10

HEAD_DIM = 64
LANES = 128
N_CHIP = 4
N_DEV = 8
ROW_TILE = 512
MM_TILE = 512
FUSED_TILE = 512
TN_TILE = 2048
ATT_BLOCK = 512
SMALL_ROWS = 24
V7X_VMEM_BYTES = 64 * 1024 * 1024
VMEM_LIMIT = V7X_VMEM_BYTES - 8 * 1024 * 1024
MESH = pl.DeviceIdType.MESH
ANY = pl.BlockSpec(memory_space=pl.ANY)

NT = (((1,), (1,)), ((), ()))
TN = (((0,), (0,)), ((), ()))


def _tile(n, pref):
    if n <= pref:
        return n
    t = pref - pref % 16
    while n % t:
        t -= 16
    return t


def _params():
    return pltpu.CompilerParams(vmem_limit_bytes=VMEM_LIMIT)


def _sds(shape, dtype):
    return jax.ShapeDtypeStruct(shape, dtype)


def _rows(tm, c):
    return pl.BlockSpec((tm, c), lambda i: (i, 0))


def _whole(shape):
    return pl.BlockSpec(shape, lambda *_: (0,) * len(shape))


def _resident(shape):
    return pl.BlockSpec(shape, lambda *_: (0,) * len(shape), pipeline_mode=pl.Buffered(1))


def _rms_fwd(x, g, tag):
    T, D = x.shape
    tm = _tile(T, ROW_TILE)

    def body(x_ref, g_ref, o_ref):
        xv = x_ref[...]
        r = lax.rsqrt(jnp.mean(xv * xv, axis=-1, keepdims=True) + RMS_EPS)
        o_ref[...] = (xv * r * g_ref[...]).astype(o_ref.dtype)

    return pl.pallas_call(
        body, name=f"rms_fwd_{tag}", grid=(T // tm,),
        in_specs=[_rows(tm, D), _whole((1, D))], out_specs=_rows(tm, D),
        out_shape=_sds((T, D), MM_DTYPE), compiler_params=_params())(x, g.reshape(1, D))


def _accumulate(ref, part, first):
    @pl.when(first)
    def _():
        ref[...] = part

    @pl.when(jnp.logical_not(first))
    def _():
        ref[...] += part


def _loss_grad(y, tgt):
    T, D = y.shape
    tm = _tile(T, ROW_TILE)

    def body(y_ref, t_ref, dy_ref, l_ref):
        e = y_ref[...] - t_ref[...]
        row = jnp.mean(e * e, axis=-1, keepdims=True)
        part = jnp.broadcast_to(jnp.sum(row, axis=0, keepdims=True), (8, LANES))
        _accumulate(l_ref, part, pl.program_id(0) == 0)
        dy_ref[...] = e * (1.0 / D)

    dy, lsum = pl.pallas_call(
        body, name="loss_grad", grid=(T // tm,),
        in_specs=[_rows(tm, D), _rows(tm, D)], out_specs=[_rows(tm, D), _whole((8, LANES))],
        out_shape=[_sds((T, D), F32), _sds((8, LANES), F32)], compiler_params=_params())(y, tgt)
    return dy, 0.5 * lsum[0, 0]


def _shift_down(u, d, rows):
    return jnp.where(rows >= d, pltpu.roll(u, d, 0), 0.0)


def _shift_up(u, d, rows, S):
    return jnp.where(rows < S - d, pltpu.roll(u, S - d, 0), 0.0)


def _conv_fwd(bch, k8, Bl, S):
    T, D3 = bch.shape
    D = D3 // 3
    dc = min(D, 2 * LANES)
    nd = D // dc

    def body(b_ref, c_ref, h_ref, k_ref, z_ref):
        rows = lax.broadcasted_iota(jnp.int32, (S, 1), 0)
        u = c_ref[...].astype(F32) * h_ref[...].astype(F32)
        y = k_ref[2:3, :] * u + k_ref[1:2, :] * _shift_down(u, 1, rows) + k_ref[0:1, :] * _shift_down(u, 2, rows)
        z_ref[...] = (b_ref[...].astype(F32) * y).astype(z_ref.dtype)

    return pl.pallas_call(
        body, name="conv_fwd", grid=(Bl, nd),
        in_specs=[pl.BlockSpec((S, dc), lambda b, j: (b, j)),
                  pl.BlockSpec((S, dc), lambda b, j: (b, nd + j)),
                  pl.BlockSpec((S, dc), lambda b, j: (b, 2 * nd + j)),
                  pl.BlockSpec((8, dc), lambda b, j: (0, j))],
        out_specs=pl.BlockSpec((S, dc), lambda b, j: (b, j)),
        out_shape=_sds((T, D), MM_DTYPE), compiler_params=_params())(bch, bch, bch, k8)


def _conv_bwd(bch, dz, k8, Bl, S):
    T, D3 = bch.shape
    D = D3 // 3
    dc = min(D, 2 * LANES)
    nd = D // dc

    def body(b_ref, c_ref, h_ref, dz_ref, k_ref, db_ref, dc_ref, dh_ref, dk_ref):
        rows = lax.broadcasted_iota(jnp.int32, (S, 1), 0)
        bv = b_ref[...].astype(F32)
        cv = c_ref[...].astype(F32)
        hv = h_ref[...].astype(F32)
        dzv = dz_ref[...].astype(F32)
        u = cv * hv
        u1 = _shift_down(u, 1, rows)
        u2 = _shift_down(u, 2, rows)
        y = k_ref[2:3, :] * u + k_ref[1:2, :] * u1 + k_ref[0:1, :] * u2
        db_ref[...] = (dzv * y).astype(db_ref.dtype)
        dy = dzv * bv
        du = k_ref[2:3, :] * dy + k_ref[1:2, :] * _shift_up(dy, 1, rows, S) + k_ref[0:1, :] * _shift_up(dy, 2, rows, S)
        dc_ref[...] = (du * hv).astype(dc_ref.dtype)
        dh_ref[...] = (du * cv).astype(dh_ref.dtype)

        @pl.when(pl.program_id(1) == 0)
        def _():
            dk_ref[...] = jnp.zeros_like(dk_ref)

        dk_ref[0:1, :] += jnp.sum(dy * u2, axis=0, keepdims=True)
        dk_ref[1:2, :] += jnp.sum(dy * u1, axis=0, keepdims=True)
        dk_ref[2:3, :] += jnp.sum(dy * u, axis=0, keepdims=True)

    seq = lambda off: pl.BlockSpec((S, dc), lambda j, b: (b, off + j))
    return pl.pallas_call(
        body, name="conv_bwd", grid=(nd, Bl),
        in_specs=[seq(0), seq(nd), seq(2 * nd), seq(0), pl.BlockSpec((8, dc), lambda j, b: (0, j))],
        out_specs=[seq(0), seq(0), seq(0), pl.BlockSpec((8, dc), lambda j, b: (0, j))],
        out_shape=[_sds((T, D), MM_DTYPE)] * 3 + [_sds((8, D), F32)],
        compiler_params=_params())(bch, bch, bch, dz, k8)


def _forget_fwd(pf, fb, Bl, S):
    T = pf.shape[0]

    def body(p_ref, fb_ref, c_ref):
        rows = lax.broadcasted_iota(jnp.int32, (S, 1), 0)
        z = p_ref[...] + fb_ref[...]
        acc = jnp.minimum(z, 0.0) - jnp.log1p(jnp.exp(-jnp.abs(z)))
        d = 1
        while d < S:
            acc = acc + _shift_down(acc, d, rows)
            d *= 2
        c_ref[...] = acc

    return pl.pallas_call(
        body, name="forget_fwd", grid=(Bl,),
        in_specs=[_rows(S, LANES), _whole((1, LANES))], out_specs=_rows(S, LANES),
        out_shape=_sds((T, LANES), F32), compiler_params=_params())(pf, fb)


def _forget_bwd(dc, pf, fb, Bl, S):
    T = pf.shape[0]

    def body(dc_ref, p_ref, fb_ref, df_ref, dfb_ref):
        rows = lax.broadcasted_iota(jnp.int32, (S, 1), 0)
        acc = dc_ref[...]
        d = 1
        while d < S:
            acc = acc + _shift_up(acc, d, rows, S)
            d *= 2
        df = acc * jax.nn.sigmoid(-(p_ref[...] + fb_ref[...]))
        df_ref[...] = df.astype(df_ref.dtype)
        _accumulate(dfb_ref, jnp.sum(df, axis=0, keepdims=True), pl.program_id(0) == 0)

    return pl.pallas_call(
        body, name="forget_bwd", grid=(Bl,),
        in_specs=[_rows(S, LANES), _rows(S, LANES), _whole((1, LANES))],
        out_specs=[_rows(S, LANES), _whole((1, LANES))],
        out_shape=[_sds((T, LANES), MM_DTYPE), _sds((1, LANES), F32)],
        compiler_params=_params())(dc, pf, fb)


def _head_mask(h):
    lane = lax.broadcasted_iota(jnp.int32, (1, LANES), 1)
    return (lane >= h * HEAD_DIM) & (lane < (h + 1) * HEAD_DIM)


def _attn_fwd(qg, kv, c_col, c_row, Bl, S, D):
    T = Bl * S
    H = D // HEAD_DIM
    HP = D // LANES
    bq = min(S, ATT_BLOCK)
    nq = S // bq
    scale = 1.0 / math.sqrt(HEAD_DIM)

    def body(q_ref, g_ref, k_ref, v_ref, cc_ref, cr_ref, o_ref, lse_ref, z_ref):
        i = pl.program_id(2)
        q2 = q_ref[...]
        qh = [q2 * (_head_mask(h).astype(F32) * scale).astype(q2.dtype) for h in range(2)]
        cc = [cc_ref[h][:, :1] for h in range(2)]
        diag = lax.broadcasted_iota(jnp.int32, (1, bq), 1) <= lax.broadcasted_iota(jnp.int32, (bq, 1), 0)

        def block(j, carry, on_diagonal):
            off = pl.multiple_of(j * bq, bq)
            kj = k_ref[pl.ds(off, bq), :]
            vj = v_ref[pl.ds(off, bq), :]
            new = []
            for h in range(2):
                m, l, acc = carry[h]
                s = lax.dot_general(qh[h], kj, NT, preferred_element_type=F32) + cc[h] - cr_ref[h, j]
                if on_diagonal:
                    s = jnp.where(diag, s, -jnp.inf)
                m_new = jnp.maximum(m, jnp.max(s, axis=1, keepdims=True))
                p = jnp.exp(s - m_new)
                a = jnp.exp(m - m_new)
                l = a * l + jnp.sum(p, axis=1, keepdims=True)
                acc = a * acc + jnp.dot(p.astype(MM_DTYPE), vj, preferred_element_type=F32)
                new.append((m_new, l, acc))
            return tuple(new)

        one = (jnp.full((bq, 1), -jnp.inf, F32), jnp.zeros((bq, 1), F32), jnp.zeros((bq, LANES), F32))
        carry = lax.fori_loop(0, i, lambda j, c: block(j, c, False), (one, one))
        carry = block(i, carry, True)
        outs = []
        for h in range(2):
            m, l, acc = carry[h]
            outs.append(acc / l)
            lse_ref[h] = jnp.broadcast_to(m + jnp.log(l), (bq, LANES))
        o2 = jnp.where(_head_mask(0), outs[0], outs[1])
        o_ref[...] = o2
        z_ref[...] = (jax.nn.sigmoid(g_ref[...].astype(F32)) * o2).astype(z_ref.dtype)

    return pl.pallas_call(
        body, name="attn_fwd", grid=(Bl, HP, nq),
        in_specs=[pl.BlockSpec((bq, LANES), lambda b, hp, i: (b * nq + i, hp)),
                  pl.BlockSpec((bq, LANES), lambda b, hp, i: (b * nq + i, HP + hp)),
                  pl.BlockSpec((S, LANES), lambda b, hp, i: (b, hp)),
                  pl.BlockSpec((S, LANES), lambda b, hp, i: (b, HP + hp)),
                  pl.BlockSpec((None, 2, bq, LANES), lambda b, hp, i: (b, hp, i, 0)),
                  pl.BlockSpec((None, 2, nq, 1, bq), lambda b, hp, i: (b, hp, 0, 0, 0))],
        out_specs=[pl.BlockSpec((bq, LANES), lambda b, hp, i: (b * nq + i, hp)),
                   pl.BlockSpec((None, 2, bq, LANES), lambda b, hp, i: (b, hp, i, 0)),
                   pl.BlockSpec((bq, LANES), lambda b, hp, i: (b * nq + i, hp))],
        out_shape=[_sds((T, D), F32), _sds((Bl, H, S, LANES), F32), _sds((T, D), MM_DTYPE)],
        compiler_params=_params())(qg, qg, kv, kv, c_col, c_row)


def _attn_bwd(qg, kv, dz, lse, c_col, c_row, Bl, S, D):
    T = Bl * S
    H = D // HEAD_DIM
    HP = D // LANES
    bq = min(S, ATT_BLOCK)
    nq = S // bq
    scale = 1.0 / math.sqrt(HEAD_DIM)

    def body(q_ref, g_ref, k_ref, v_ref, dz_ref, lse_ref, cc_ref, cr_ref, dq_ref, dk_ref, dv_ref, dcr_ref, p_sc, dp_sc):
        i = pl.program_id(2)

        @pl.when(i == 0)
        def _():
            dk_ref[...] = jnp.zeros_like(dk_ref)
            dv_ref[...] = jnp.zeros_like(dv_ref)
            dcr_ref[...] = jnp.zeros_like(dcr_ref)

        q2 = q_ref[...]
        do2 = (dz_ref[...].astype(F32) * jax.nn.sigmoid(g_ref[...].astype(F32))).astype(MM_DTYPE)
        masks = [_head_mask(h).astype(F32) for h in range(2)]
        qh = [q2 * (masks[h] * scale).astype(q2.dtype) for h in range(2)]
        doh = [do2 * masks[h].astype(do2.dtype) for h in range(2)]
        cc = [cc_ref[h][:, :1] for h in range(2)]
        lse = [lse_ref[h][:, :1] for h in range(2)]
        diag = lax.broadcasted_iota(jnp.int32, (1, bq), 1) <= lax.broadcasted_iota(jnp.int32, (bq, 1), 0)

        def sweep1(j, delta, on_diagonal):
            off = pl.multiple_of(j * bq, bq)
            kj = k_ref[pl.ds(off, bq), :]
            vj = v_ref[pl.ds(off, bq), :]
            new = []
            dv = None
            for h in range(2):
                s = lax.dot_general(qh[h], kj, NT, preferred_element_type=F32) + cc[h] - cr_ref[h, j]
                if on_diagonal:
                    s = jnp.where(diag, s, -jnp.inf)
                p = jnp.exp(s - lse[h])
                dp = lax.dot_general(doh[h], vj, NT, preferred_element_type=F32)
                p_sc[h, j] = p
                dp_sc[h, j] = dp
                part = lax.dot_general(p.astype(MM_DTYPE), doh[h], TN, preferred_element_type=F32)
                dv = part if dv is None else dv + part
                new.append(delta[h] + jnp.sum(p * dp, axis=1, keepdims=True))
            dv_ref[pl.ds(off, bq), :] += dv
            return tuple(new)

        zero = jnp.zeros((bq, 1), F32)
        delta = lax.fori_loop(0, i, lambda j, d: sweep1(j, d, False), (zero, zero))
        delta = sweep1(i, delta, True)

        def sweep2(j, dq):
            off = pl.multiple_of(j * bq, bq)
            kj = k_ref[pl.ds(off, bq), :]
            dk = None
            for h in range(2):
                ds = p_sc[h, j] * (dp_sc[h, j] - delta[h])
                dcr_ref[h, j] -= jnp.sum(ds, axis=0, keepdims=True)
                dsb = ds.astype(MM_DTYPE)
                dq = dq + jnp.dot(dsb, kj * (masks[h] * scale).astype(kj.dtype), preferred_element_type=F32)
                part = lax.dot_general(dsb, qh[h], TN, preferred_element_type=F32)
                dk = part if dk is None else dk + part
            dk_ref[pl.ds(off, bq), :] += dk
            return dq

        dq_ref[...] = lax.fori_loop(0, i + 1, sweep2, jnp.zeros((bq, LANES), F32))

    blk = lambda col: pl.BlockSpec((bq, LANES), lambda b, hp, i: (b * nq + i, col(hp)))
    seq = lambda col: pl.BlockSpec((S, LANES), lambda b, hp, i: (b, col(hp)))
    per_head = pl.BlockSpec((None, 2, bq, LANES), lambda b, hp, i: (b, hp, i, 0))
    rows = pl.BlockSpec((None, 2, nq, 1, bq), lambda b, hp, i: (b, hp, 0, 0, 0))
    return pl.pallas_call(
        body, name="attn_bwd", grid=(Bl, HP, nq),
        in_specs=[blk(lambda hp: hp), blk(lambda hp: HP + hp), seq(lambda hp: hp), seq(lambda hp: HP + hp),
                  blk(lambda hp: hp), per_head, per_head, rows],
        out_specs=[blk(lambda hp: hp), seq(lambda hp: hp), seq(lambda hp: hp), rows],
        out_shape=[_sds((T, D), F32), _sds((T, D), F32), _sds((T, D), F32), _sds((Bl, H, nq, 1, bq), F32)],
        scratch_shapes=[pltpu.VMEM((2, nq, bq, bq), F32), pltpu.VMEM((2, nq, bq, bq), F32)],
        compiler_params=_params())(qg, qg, kv, kv, dz, lse, c_col, c_row)


def _gate_bwd(dz, qg, o, dq):
    T, D = dz.shape
    tm = _tile(T, ROW_TILE)

    def body(dz_ref, g_ref, o_ref, dq_ref, out_ref):
        g = g_ref[...].astype(F32)
        sg = jax.nn.sigmoid(g)
        out_ref[:, :D] = dq_ref[...].astype(out_ref.dtype)
        out_ref[:, D:] = (dz_ref[...].astype(F32) * o_ref[...] * sg * (1.0 - sg)).astype(out_ref.dtype)

    return pl.pallas_call(
        body, name="gate_bwd", grid=(T // tm,),
        in_specs=[_rows(tm, D), pl.BlockSpec((tm, D), lambda i: (i, 1)), _rows(tm, D), _rows(tm, D)],
        out_specs=_rows(tm, 2 * D), out_shape=_sds((T, 2 * D), MM_DTYPE),
        compiler_params=_params())(dz, qg, o, dq)


def _mm_nn(a, b, out_dtype, tag):
    T, K = a.shape
    N = b.shape[1]
    tm = _tile(T, MM_TILE)

    def body(a_ref, b_ref, o_ref):
        o_ref[...] = jnp.dot(a_ref[...], b_ref[...], preferred_element_type=F32).astype(o_ref.dtype)

    return pl.pallas_call(
        body, name=f"mm_nn_{tag}", grid=(T // tm,),
        in_specs=[_rows(tm, K), _whole((K, N))], out_specs=_rows(tm, N),
        out_shape=_sds((T, N), out_dtype), compiler_params=_params())(a, b)


def _mm_tn_in(a, dy, tag, after=None):
    T, K = a.shape
    n = dy.shape[1] // N_CHIP
    tt = _tile(T, TN_TILE)
    extra = [] if after is None else [after]

    def body(a_ref, d_ref, *rest):
        part = lax.dot_general(a_ref[...], d_ref[...], TN, preferred_element_type=F32)
        _accumulate(rest[-1], part, pl.program_id(1) == 0)

    return pl.pallas_call(
        body, name=f"mm_tn_in_{tag}", grid=(N_CHIP, T // tt),
        in_specs=[pl.BlockSpec((tt, K), lambda s, t: (t, 0)), pl.BlockSpec((tt, n), lambda s, t: (t, s))]
        + [ANY] * len(extra),
        out_specs=pl.BlockSpec((None, K, n), lambda s, t: (s, 0, 0)), out_shape=_sds((N_CHIP, K, n), F32),
        compiler_params=_params())(a, dy, *extra)


def _mm_tn_out(act, dh, tag, after=None):
    T, R4 = act.shape
    D = dh.shape[1]
    r = R4 // N_CHIP
    g = 1 if r % LANES == 0 else 2
    tt = _tile(T, TN_TILE)
    extra = [] if after is None else [after]

    def body(a_ref, d_ref, *rest):
        o_ref = rest[-1]
        part = lax.dot_general(a_ref[...], d_ref[...], TN, preferred_element_type=F32)
        first = pl.program_id(1) == 0
        for q in range(g):
            _accumulate(o_ref.at[q], part[q * r:(q + 1) * r], first)

    return pl.pallas_call(
        body, name=f"mm_tn_out_{tag}", grid=(N_CHIP // g, T // tt),
        in_specs=[pl.BlockSpec((tt, g * r), lambda s, t: (t, s)), pl.BlockSpec((tt, D), lambda s, t: (t, 0))]
        + [ANY] * len(extra),
        out_specs=pl.BlockSpec((g, r, D), lambda s, t: (s, 0, 0)), out_shape=_sds((N_CHIP, r, D), F32),
        compiler_params=_params())(act, dh, *extra)


def _mm_tn(a, b, tag):
    T, K = a.shape
    N = b.shape[1]
    tt = _tile(T, TN_TILE)

    def body(a_ref, b_ref, o_ref):
        part = lax.dot_general(a_ref[...], b_ref[...], TN, preferred_element_type=F32)
        _accumulate(o_ref, part, pl.program_id(0) == 0)

    return pl.pallas_call(
        body, name=f"mm_tn_{tag}", grid=(T // tt,),
        in_specs=[_rows(tt, K), _rows(tt, N)], out_specs=_whole((K, N)),
        out_shape=_sds((K, N), F32), compiler_params=_params())(a, b)


def _norm_mm_in(x, g, wg, tag, swiglu=False):
    T, D = x.shape
    n = wg.shape[-1]
    tm = _tile(T, FUSED_TILE)
    half = N_CHIP // 2

    def body(x_ref, g_ref, w_ref, xn_ref, y_ref, *rest):
        xv = x_ref[...]
        r = lax.rsqrt(jnp.mean(xv * xv, axis=-1, keepdims=True) + RMS_EPS)
        xn = (xv * r * g_ref[...]).astype(xn_ref.dtype)
        xn_ref[...] = xn

        def product(s):
            p = jnp.dot(xn, w_ref[s], preferred_element_type=F32)
            y_ref[:, s * n:(s + 1) * n] = p.astype(y_ref.dtype)
            return p

        if swiglu:
            for q in range(half):
                gate, up = product(q), product(half + q)
                rest[0][:, q * n:(q + 1) * n] = (gate * jax.nn.sigmoid(gate) * up).astype(rest[0].dtype)
        else:
            for s in range(N_CHIP):
                product(s)

    out_specs = [_rows(tm, D), _rows(tm, N_CHIP * n)]
    out_shape = [_sds((T, D), MM_DTYPE), _sds((T, N_CHIP * n), MM_DTYPE)]
    if swiglu:
        out_specs.append(_rows(tm, half * n))
        out_shape.append(_sds((T, half * n), MM_DTYPE))
    return pl.pallas_call(
        body, name=f"norm_mm_in_{tag}", grid=(T // tm,),
        in_specs=[_rows(tm, D), _whole((1, D)), _resident((N_CHIP, D, n))], out_specs=out_specs,
        out_shape=out_shape, compiler_params=_params())(x, g.reshape(1, D), wg)


def _mm_out_post(a, b, x, g, alpha, tag):
    T, K = a.shape
    D = b.shape[1]
    tm = _tile(T, FUSED_TILE)

    def body(a_ref, b_ref, x_ref, g_ref, h_ref, o_ref):
        hv = jnp.dot(a_ref[...], b_ref[...], preferred_element_type=F32)
        h_ref[...] = hv
        r = lax.rsqrt(jnp.mean(hv * hv, axis=-1, keepdims=True) + RMS_EPS)
        o_ref[...] = x_ref[...] + alpha * (hv * r * g_ref[...])

    return pl.pallas_call(
        body, name=f"mm_out_post_{tag}", grid=(T // tm,),
        in_specs=[_rows(tm, K), _resident((K, D)), _rows(tm, D), _whole((1, D))],
        out_specs=[_rows(tm, D), _rows(tm, D)], out_shape=[_sds((T, D), F32)] * 2,
        compiler_params=_params())(a, b, x, g.reshape(1, D))


def _post_bwd_mm(dx, h, g, alpha, b, tag, hgu=None):
    T, D = dx.shape
    K = b.shape[0]
    tm = _tile(T, FUSED_TILE)

    def body(dx_ref, h_ref, g_ref, b_ref, *rest):
        dh_ref, dg_ref, out_ref = rest[-3:]
        hv = h_ref[...]
        r = lax.rsqrt(jnp.mean(hv * hv, axis=-1, keepdims=True) + RMS_EPS)
        hh = hv * r
        dyn = alpha * dx_ref[...]
        _accumulate(dg_ref, jnp.sum(dyn * hh, axis=0, keepdims=True), pl.program_id(0) == 0)
        dhh = dyn * g_ref[...]
        dh = (r * (dhh - hh * jnp.mean(dhh * hh, axis=-1, keepdims=True))).astype(dh_ref.dtype)
        dh_ref[...] = dh
        da = lax.dot_general(dh, b_ref[...], NT, preferred_element_type=F32)
        if hgu is None:
            out_ref[...] = da.astype(out_ref.dtype)
        else:
            gate = rest[0][:, :K]
            up = rest[0][:, K:]
            dab = da.astype(gate.dtype)
            sg = jax.nn.sigmoid(gate)
            out_ref[:, :K] = (dab * up * sg * (1.0 + gate * (1.0 - sg))).astype(out_ref.dtype)
            out_ref[:, K:] = (dab * gate * sg).astype(out_ref.dtype)

    in_specs = [_rows(tm, D), _rows(tm, D), _whole((1, D)), _resident((K, D))]
    args = [dx, h, g.reshape(1, D), b]
    wide = K
    if hgu is not None:
        wide = 2 * K
        in_specs.append(_rows(tm, wide))
        args.append(hgu)
    return pl.pallas_call(
        body, name=f"post_bwd_mm_{tag}", grid=(T // tm,), in_specs=in_specs,
        out_specs=[_rows(tm, D), _whole((1, D)), _rows(tm, wide)],
        out_shape=[_sds((T, D), MM_DTYPE), _sds((1, D), F32), _sds((T, wide), MM_DTYPE)],
        compiler_params=_params())(*args)


def _mm_nt_pre(dy, w, dres, x, g, tag):
    T, C = dy.shape
    D = x.shape[1]
    tm = _tile(T, FUSED_TILE)
    n = w.shape[-1]

    def body(dy_ref, w_ref, dres_ref, x_ref, g_ref, dx_ref, dg_ref):
        if w.ndim == 2:
            dn = lax.dot_general(dy_ref[...], w_ref[...], NT, preferred_element_type=F32)
        else:
            dn = None
            for s in range(N_CHIP):
                part = lax.dot_general(dy_ref[:, s * n:(s + 1) * n], w_ref[s], NT, preferred_element_type=F32)
                dn = part if dn is None else dn + part
        xv = x_ref[...]
        r = lax.rsqrt(jnp.mean(xv * xv, axis=-1, keepdims=True) + RMS_EPS)
        xh = xv * r
        _accumulate(dg_ref, jnp.sum(dn * xh, axis=0, keepdims=True), pl.program_id(0) == 0)
        dxh = dn * g_ref[...]
        dx_ref[...] = dres_ref[...] + r * (dxh - xh * jnp.mean(dxh * xh, axis=-1, keepdims=True))

    return pl.pallas_call(
        body, name=f"mm_nt_pre_{tag}", grid=(T // tm,),
        in_specs=[_rows(tm, C), _resident(w.shape), _rows(tm, D), _rows(tm, D), _whole((1, D))],
        out_specs=[_rows(tm, D), _whole((1, D))], out_shape=[_sds((T, D), F32), _sds((1, D), F32)],
        compiler_params=_params())(dy, w, dres, x, g.reshape(1, D))


def _adamw(w, g, m, v, tag, after=None):
    R, C = w.shape
    tr = _tile(R, ROW_TILE)
    extra = [] if after is None else [after]

    def body(w_ref, g_ref, m_ref, v_ref, *rest):
        go_ref, d_ref, mo_ref, vo_ref = rest[-4:]
        gv = g_ref[...]
        go_ref[...] = gv
        mn = ADAM_B1 * m_ref[...] + (1.0 - ADAM_B1) * gv
        vn = ADAM_B2 * v_ref[...] + (1.0 - ADAM_B2) * (gv * gv)
        m_hat = mn / (1.0 - ADAM_B1 ** ADAM_STEP)
        v_hat = vn / (1.0 - ADAM_B2 ** ADAM_STEP)
        d_ref[...] = -ADAM_LR * (m_hat / (jnp.sqrt(v_hat) + ADAM_EPS) + ADAM_WD * w_ref[...])
        mo_ref[...] = mn
        vo_ref[...] = vn

    return pl.pallas_call(
        body, name=f"adamw_{tag}", grid=(R // tr,),
        in_specs=[_rows(tr, C)] * 4 + [ANY] * len(extra), out_specs=[_rows(tr, C)] * 4,
        out_shape=[_sds((R, C), F32)] * 4, compiler_params=_params())(w, g, m, v, *extra)


def _sum_devices(gall, own, place):
    _, R, C = gall.shape

    def body(place_ref, g_ref, s_ref, o_ref):
        me = 2 * place_ref[1] + place_ref[0]
        acc = None
        for d in range(N_DEV):
            term = jnp.where(me == d, s_ref[...], g_ref[d])
            acc = term if acc is None else acc + term
        o_ref[...] = acc

    grid_spec = pltpu.PrefetchScalarGridSpec(
        num_scalar_prefetch=1, grid=(1,),
        in_specs=[pl.BlockSpec((N_DEV, R, C), lambda i, p: (0, 0, 0)), pl.BlockSpec((R, C), lambda i, p: (0, 0))],
        out_specs=pl.BlockSpec((R, C), lambda i, p: (0, 0)))
    return pl.pallas_call(
        body, name="sum_devices", grid_spec=grid_spec, out_shape=_sds((R, C), F32),
        compiler_params=_params())(place, gall, own)


HBM = pl.BlockSpec(memory_space=pltpu.HBM)
SEM = pl.BlockSpec(memory_space=pltpu.SEMAPHORE)
EFFECT = pltpu.SideEffectType.DATAFLOW_SIDE_EFFECTING


def _place():
    x, y, c = lax.axis_index("x"), lax.axis_index("y"), lax.axis_index("c")
    chips = ((1 - x, y), (x, 1 - y), (1 - x, 1 - y))
    return x, y, c, chips


def _remote(src, dst, send_sem, recv_sem, dev):
    return pltpu.make_async_remote_copy(src_ref=src, dst_ref=dst, send_sem=send_sem, recv_sem=recv_sem,
                                        device_id=dev, device_id_type=MESH)


def _in_hbm(a):
    return pltpu.with_memory_space_constraint(a, pltpu.HBM)


def _own_slot(w4, l, dtype, place, tag):
    _, _, r, col = w4.shape
    tr = _tile(r, 2 * ROW_TILE)

    def body(place_ref, x_ref, o_ref):
        o_ref[...] = x_ref[...].astype(o_ref.dtype)

    grid_spec = pltpu.PrefetchScalarGridSpec(
        num_scalar_prefetch=1, grid=(2, r // tr),
        in_specs=[pl.BlockSpec((None, None, tr, col), lambda h, i, p: (l, h, i, 0))],
        out_specs=pl.BlockSpec((None, None, tr, col), lambda h, i, p: (p[1], h, i, 0)))
    return pl.pallas_call(
        body, name=f"own_slot_{tag}", grid_spec=grid_spec, out_shape=_sds((N_CHIP, 2, r, col), dtype),
        compiler_params=_params())(place, w4)


def _gather_start(bufs, after, tag):
    n = len(bufs)

    def body(*refs):
        ins = refs[:n]
        s_sem, r_sem, token = refs[n + 1], refs[n + 2], refs[2 * n + 3]
        x, y, c, chips = _place()
        me = 2 * x + y
        for i in range(n):
            mine = ins[i].at[me, c]
            for j, (px, py) in enumerate(chips):
                _remote(mine, mine, s_sem.at[3 * i + j], r_sem.at[3 * i + j], (px, py, c)).start()
        token[...] = jnp.zeros_like(token)

    dma = pltpu.SemaphoreType.DMA
    res = pl.pallas_call(
        body, name=f"gather_start_{tag}", in_specs=[HBM] * n + [ANY],
        out_specs=[SEM, SEM] + [HBM] * n + [pl.BlockSpec(memory_space=pltpu.VMEM)],
        out_shape=[dma((3 * n,)), dma((3 * n,))] + [pltpu.HBM(b.shape, b.dtype) for b in bufs] + [_sds((8, LANES), F32)],
        input_output_aliases={i: i + 2 for i in range(n)},
        compiler_params=pltpu.CompilerParams(has_side_effects=EFFECT),
        )(*[_in_hbm(b) for b in bufs], after)
    return res[0], res[1], list(res[2:2 + n]), res[-1]


def _gather_pass(s_sem, r_sem, bufs, first, after, tag):
    n = len(bufs)

    def body(*refs):
        ins = refs[:n]
        a_s, a_r, b_s, b_r = refs[n], refs[n + 1], refs[n + 3], refs[n + 4]
        x, y, c, chips = _place()
        me = 2 * x + y
        sib = (x, y, 1 - c)
        for i in range(n):
            mine = ins[i].at[me, c]
            for j, (px, py) in enumerate(chips):
                k = 3 * (first + i) + j
                _remote(mine, mine, a_s.at[k], a_r.at[k], (px, py, c)).wait_send()
        for j, (px, py) in enumerate(chips):
            for i in range(n):
                k = 3 * (first + i) + j
                blk = ins[i].at[2 * px + py, c]
                _remote(blk, blk, a_s.at[k], a_r.at[k], (px, py, c)).wait_recv()
                _remote(blk, blk, b_s.at[3 * i + j], b_r.at[3 * i + j], sib).start()

    dma = pltpu.SemaphoreType.DMA
    res = pl.pallas_call(
        body, name=f"gather_pass_{tag}", in_specs=[HBM] * n + [SEM, SEM, ANY],
        out_specs=[SEM, SEM] + [HBM] * n,
        out_shape=[dma((3 * n,)), dma((3 * n,))] + [pltpu.HBM(b.shape, b.dtype) for b in bufs],
        input_output_aliases={i: i + 2 for i in range(n)},
        compiler_params=pltpu.CompilerParams(has_side_effects=EFFECT),
        )(*bufs, s_sem, r_sem, after)
    return res[0], res[1], list(res[2:])


def _gather_land(s_sem, r_sem, bufs, tag):
    n = len(bufs)

    def body(*refs):
        ins = refs[:n]
        b_s, b_r = refs[n], refs[n + 1]
        x, y, c, chips = _place()
        sib = (x, y, 1 - c)
        for j, (px, py) in enumerate(chips):
            for i in range(n):
                sent = ins[i].at[2 * px + py, c]
                got = ins[i].at[2 * px + py, 1 - c]
                _remote(sent, sent, b_s.at[3 * i + j], b_r.at[3 * i + j], sib).wait_send()
                _remote(got, got, b_s.at[3 * i + j], b_r.at[3 * i + j], sib).wait_recv()

    return list(pl.pallas_call(
        body, name=f"gather_land_{tag}", in_specs=[HBM] * n + [SEM, SEM], out_specs=[HBM] * n,
        out_shape=[pltpu.HBM(b.shape, b.dtype) for b in bufs],
        input_output_aliases={i: i for i in range(n)},
        compiler_params=pltpu.CompilerParams(has_side_effects=EFFECT),
        )(*bufs, s_sem, r_sem))


def _rs_pair_add(g, recv, place, tag):
    r, col = g.shape[-2:]
    tr = _tile(r, ROW_TILE)

    def body(place_ref, g_ref, r_ref, wire_ref, own_ref):
        tot = g_ref[...] + r_ref[...]
        wire_ref[...] = tot.astype(wire_ref.dtype)

        @pl.when(pl.program_id(1) == place_ref[1])
        def _():
            own_ref[...] = tot

    grid_spec = pltpu.PrefetchScalarGridSpec(
        num_scalar_prefetch=1, grid=(r // tr, N_CHIP),
        in_specs=[pl.BlockSpec((None, None, tr, col), lambda i, s, p: (s, p[0], i, 0)),
                  pl.BlockSpec((None, tr, col), lambda i, s, p: (s, i, 0))],
        out_specs=[pl.BlockSpec((None, tr, col), lambda i, s, p: (s, i, 0)),
                   pl.BlockSpec((tr, col), lambda i, s, p: (i, 0))])
    return pl.pallas_call(
        body, name=f"rs_pair_add_{tag}", grid_spec=grid_spec,
        out_shape=[_sds((N_CHIP, r, col), WIRE_DTYPE), _sds((r, col), F32)],
        compiler_params=_params())(place, g, recv)


def _pair_plan(srcs, lands):
    x, y, c, _ = _place()
    return [(s.at[:, 1 - c], l, (x, y, 1 - c)) for s, l in zip(srcs, lands)]


def _chip_plan(srcs, lands):
    x, y, c, chips = _place()
    plan = []
    for s, l in zip(srcs, lands):
        if len(s.shape) == 2:
            me = 4 * x + 2 * y + c
            plan += [(s, l.at[me], (x ^ (k >> 2), y ^ ((k >> 1) & 1), c ^ (k & 1))) for k in range(1, N_DEV)]
        else:
            plan += [(s.at[2 * px + py], l.at[j], (px, py, c)) for j, (px, py) in enumerate(chips)]
    return plan


def _exchange_start(srcs, lands, plan, count, tag):
    n = len(srcs)
    both = list(srcs) + list(lands)

    def body(*refs):
        s_sem, r_sem, token = refs[2 * n], refs[2 * n + 1], refs[4 * n + 2]
        for k, (src, dst, dev) in enumerate(plan(refs[:n], refs[n:2 * n])):
            _remote(src, dst, s_sem.at[k], r_sem.at[k], dev).start()
        token[...] = jnp.zeros_like(token)

    dma = pltpu.SemaphoreType.DMA
    res = pl.pallas_call(
        body, name=f"exchange_start_{tag}", in_specs=[HBM] * (2 * n),
        out_specs=[SEM, SEM] + [HBM] * (2 * n) + [pl.BlockSpec(memory_space=pltpu.VMEM)],
        out_shape=[dma((count,)), dma((count,))] + [pltpu.HBM(b.shape, b.dtype) for b in both] + [_sds((8, LANES), F32)],
        input_output_aliases={i: i + 2 for i in range(2 * n)},
        compiler_params=pltpu.CompilerParams(has_side_effects=EFFECT),
        )(*[_in_hbm(b) for b in both])
    return res[0], res[1], list(res[2:2 + n]), list(res[2 + n:2 + 2 * n]), res[-1]


def _exchange_wait(s_sem, r_sem, srcs, lands, plan, after, tag):
    n = len(srcs)

    def body(*refs):
        s_ref, r_ref = refs[2 * n], refs[2 * n + 1]
        for k, (src, dst, dev) in enumerate(plan(refs[:n], refs[n:2 * n])):
            cp = _remote(src, dst, s_ref.at[k], r_ref.at[k], dev)
            cp.wait_send()
            cp.wait_recv()

    both = list(srcs) + list(lands)
    res = pl.pallas_call(
        body, name=f"exchange_wait_{tag}", in_specs=[HBM] * (2 * n) + [SEM, SEM, ANY], out_specs=[HBM] * (2 * n),
        out_shape=[pltpu.HBM(b.shape, b.dtype) for b in both],
        input_output_aliases={i: i for i in range(2 * n)},
        compiler_params=pltpu.CompilerParams(has_side_effects=EFFECT),
        )(*both, s_sem, r_sem, after)
    return list(res[:n]), list(res[n:])


def _rs_chip_add(own, recv, place, l, L, prev, tag):
    r, col = own.shape
    tr = _tile(r, ROW_TILE)

    def body(place_ref, o_ref, r_ref, *rest):
        acc = o_ref[...]
        for j in range(3):
            acc = acc + r_ref[j].astype(F32)
        rest[-1][...] = acc

    in_specs = [pl.BlockSpec((tr, col), lambda i, p: (i, 0)), pl.BlockSpec((3, tr, col), lambda i, p: (0, i, 0))]
    args = [place, own, recv]
    kw = {}
    if prev is not None:
        in_specs.append(ANY)
        args.append(prev)
        kw["input_output_aliases"] = {3: 0}
    grid_spec = pltpu.PrefetchScalarGridSpec(
        num_scalar_prefetch=1, grid=(r // tr,), in_specs=in_specs,
        out_specs=pl.BlockSpec((None, None, tr, col), lambda i, p: (l, p[0], i, 0)))
    return pl.pallas_call(
        body, name=f"rs_chip_add_{tag}", grid_spec=grid_spec, out_shape=_sds((L, 2, r, col), F32),
        compiler_params=_params(), **kw)(*args)


def _rs_pair_share(fulls, tag):
    n = len(fulls)

    def body(*refs):
        outs = refs[n:2 * n]
        s_sem, r_sem = refs[2 * n:]
        x, y, c, _ = _place()
        sib = (x, y, 1 - c)
        started = []
        for i in range(n):
            cp = _remote(outs[i].at[:, c], outs[i].at[:, c], s_sem.at[i], r_sem.at[i], sib)
            cp.start()
            started.append(cp)
        for i, cp in enumerate(started):
            cp.wait_send()
            _remote(outs[i].at[:, 1 - c], outs[i].at[:, 1 - c], s_sem.at[i], r_sem.at[i], sib).wait_recv()

    dma = pltpu.SemaphoreType.DMA
    return pl.pallas_call(
        body, name=f"rs_pair_share_{tag}", in_specs=[ANY] * n, out_specs=[ANY] * n,
        out_shape=[_sds(f.shape, f.dtype) for f in fulls],
        input_output_aliases={i: i for i in range(n)},
        scratch_shapes=[dma((n,)), dma((n,))],
        )(*fulls)


def _ffn_fwd(x, g_pre, g_post, w_in, w_out, tag):
    xn, hgu, act = _norm_mm_in(x, g_pre, w_in, tag, swiglu=True)
    if callable(w_out):
        w_out = w_out(act)
    h, x_out = _mm_out_post(act, w_out.reshape(-1, w_out.shape[-1]), x, g_post, 0.5, tag)
    return x_out, (x, xn, hgu, act, h)


def _ffn_bwd(dx, saved, g_pre, g_post, w_in, w_out, tag, between=None, finish=None):
    x, xn, hgu, act, h = saved
    dh, dg_post, dhgu = _post_bwd_mm(dx, h, g_post, 0.5, w_out.reshape(-1, w_out.shape[-1]), tag, hgu=hgu)
    token = None
    if between is not None:
        token = between(dhgu)
    dw_out = _mm_tn_out(act, dh, tag, after=token)
    dw_in = _mm_tn_in(xn, dhgu, tag, after=token)
    if finish is not None:
        token = finish(dw_in, dw_out)
    if token is not None:
        g_pre = g_pre + token[0, :1]
    dx_in, dg_pre = _mm_nt_pre(dhgu, w_in, dx, x, g_pre, tag)
    return dx_in, dg_pre, dg_post, dw_in, dw_out


def kernel(x, ffn1_pre_g, ffn1_post_g, ffn1_w_in, ffn1_w_out, mix_pre_g, mix_post_g, ffn2_pre_g, ffn2_post_g, ffn2_w_in, ffn2_w_out, conv_w_in, conv_k, conv_w_out, kv_g, kv_w, forget_b, attn_w_qg, attn_w_o, loss_target, m_ffn1_pre_g, m_ffn1_post_g, m_ffn1_w_in, m_ffn1_w_out, m_mix_pre_g, m_mix_post_g, m_ffn2_pre_g, m_ffn2_post_g, m_ffn2_w_in, m_ffn2_w_out, m_conv_w_in, m_conv_k, m_conv_w_out, m_kv_g, m_kv_w, m_forget_b, m_attn_w_qg, m_attn_w_o, v_ffn1_pre_g, v_ffn1_post_g, v_ffn1_w_in, v_ffn1_w_out, v_mix_pre_g, v_mix_post_g, v_ffn2_pre_g, v_ffn2_post_g, v_ffn2_w_in, v_ffn2_w_out, v_conv_w_in, v_conv_k, v_conv_w_out, v_kv_g, v_kv_w, v_forget_b, v_attn_w_qg, v_attn_w_o):
    Bl, S, D = x.shape
    T = Bl * S
    H = forget_b.shape[0]
    assert D == H * HEAD_DIM and D % LANES == 0
    kvc = kv_w.shape[1]
    kvp = -(-kvc // LANES) * LANES
    kv_all = 2 * D + LANES
    dk_cols = conv_k.shape[2]
    chip = 2 * lax.axis_index("x") + lax.axis_index("y")
    core = lax.axis_index("c")

    given = dict(ffn1_w_in=(ffn1_w_in, m_ffn1_w_in, v_ffn1_w_in), ffn1_w_out=(ffn1_w_out, m_ffn1_w_out, v_ffn1_w_out),
                 ffn2_w_in=(ffn2_w_in, m_ffn2_w_in, v_ffn2_w_in), ffn2_w_out=(ffn2_w_out, m_ffn2_w_out, v_ffn2_w_out),
                 conv_w_in=(conv_w_in, m_conv_w_in, v_conv_w_in), conv_w_out=(conv_w_out, m_conv_w_out, v_conv_w_out),
                 kv_w=(kv_w, m_kv_w, v_kv_w), attn_w_qg=(attn_w_qg, m_attn_w_qg, v_attn_w_qg),
                 attn_w_o=(attn_w_o, m_attn_w_o, v_attn_w_o))
    shards = {k: w for k, (w, _, _) in given.items()}
    shards["kv_w"] = jnp.pad(kv_w, ((0, 0), (0, kvp - kvc)))[None]
    groups = [[("ffn1_w_in", 0), ("ffn1_w_out", 0)], [("conv_w_in", 0), ("conv_w_out", 0)],
              [("ffn2_w_in", 0), ("ffn2_w_out", 0)], [("kv_w", 0), ("ffn1_w_in", 1), ("ffn1_w_out", 1)],
              [("attn_w_qg", 0), ("attn_w_o", 0), ("ffn2_w_in", 1), ("ffn2_w_out", 1)]]
    second = groups[3] + groups[4]
    place = jnp.stack([core, chip]).astype(jnp.int32)

    def slot(key, where):
        w = shards[key[0]]
        L, r, col = w.shape
        return _own_slot(w.reshape(L, 2, r // 2, col), key[1], MM_DTYPE, where, f"{key[0]}{key[1]}")

    def whole(g):
        return g.reshape(N_CHIP, -1, g.shape[-1])

    taps_slot = _own_slot(jnp.pad(conv_k[0], ((0, 13), (0, 0))).reshape(1, 2, 8, dk_cols), 0, F32, place, "conv_k")
    fb = jnp.pad(forget_b, (0, LANES - H)).reshape(1, LANES)
    w_in0, w_out0 = groups[0]
    s_0, r_0, fly_0, token = _gather_start([slot(w_in0, place), taps_slot, slot(w_out0, place)], fb, "first")
    later = groups[1] + groups[2] + groups[3] + groups[4]
    s_1, r_1, fly_1, token = _gather_start([slot(key, place) for key in later], token, "rest")
    W = {}

    def land(sems, bufs, lo, after, tag):
        return _gather_land(*_gather_pass(*sems, bufs, lo, after, tag), tag)

    def arrive(g, after):
        lo = sum(len(groups[k]) for k in range(1, g))
        got = land((s_1, r_1), fly_1[lo:lo + len(groups[g])], lo, after, f"g{g}")
        W.update({key: whole(b) for key, b in zip(groups[g], got)})

    w_first, taps = land((s_0, r_0), fly_0[:2], 0, token, "g0")
    k_taps = taps.reshape(N_CHIP, 16, dk_cols).transpose(1, 0, 2).reshape(16, D)[:8]

    x0 = x.reshape(T, D)
    W[w_in0] = whole(w_first)

    def first_w_out(act):
        W[w_out0] = whole(land((s_0, r_0), fly_0[2:], 2, act, "g0_out")[0])
        return W[w_out0]

    x1, s_f1a = _ffn_fwd(x0, ffn1_pre_g[0], ffn1_post_g[0], W[w_in0], first_w_out, "l0f1")
    arrive(1, x1)
    w_o_conv = W["conv_w_out", 0].reshape(D, D)
    xn_c, bch = _norm_mm_in(x1, mix_pre_g[0], W["conv_w_in", 0], "conv")
    z_c = _conv_fwd(bch, k_taps, Bl, S)
    m_c, x2 = _mm_out_post(z_c, w_o_conv, x1, mix_post_g[0], 1.0, "conv_out")
    arrive(2, x2)
    x3, s_f2a = _ffn_fwd(x2, ffn2_pre_g[0], ffn2_post_g[0], W["ffn2_w_in", 0], W["ffn2_w_out", 0], "l0f2")

    arrive(3, x3)
    kv_full = jnp.concatenate([W["kv_w", 0][s, :, :kvc] for s in range(N_CHIP)], axis=1)
    kv_full = jnp.pad(kv_full, ((0, 0), (0, kv_all - kv_full.shape[1])))
    xn_kv = _rms_fwd(x3, kv_g, "kv")
    kvact = _mm_nn(xn_kv, kv_full[:, :2 * D], MM_DTYPE, "kv")
    pf = _mm_nn(xn_kv, kv_full[:, 2 * D:], F32, "forget")
    cum = _forget_fwd(pf, fb, Bl, S)
    bq = min(S, ATT_BLOCK)
    c3 = cum.reshape(Bl, S, LANES)[:, :, :H].transpose(0, 2, 1)
    c_col = jnp.broadcast_to(c3[..., None], (Bl, H, S, LANES))
    c_row = c3.reshape(Bl, H, S // bq, 1, bq)

    x4, s_f1b = _ffn_fwd(x3, ffn1_pre_g[1], ffn1_post_g[1], W["ffn1_w_in", 1], W["ffn1_w_out", 1], "l1f1")
    arrive(4, x4)
    w_o_attn = W["attn_w_o", 0].reshape(D, D)
    xn_a, qg = _norm_mm_in(x4, mix_pre_g[1], W["attn_w_qg", 0], "qg")
    o, lse, z_a = _attn_fwd(qg, kvact, c_col, c_row, Bl, S, D)
    m_a, x5 = _mm_out_post(z_a, w_o_attn, x4, mix_post_g[1], 1.0, "attn_out")
    x6, s_f2b = _ffn_fwd(x5, ffn2_pre_g[1], ffn2_post_g[1], W["ffn2_w_in", 1], W["ffn2_w_out", 1], "l1f2")

    dy, loss_local = _loss_grad(x6, loss_target.reshape(T, D))

    G = {}
    dx5, dg_f2pre_1, dg_f2post_1, G["ffn2_w_in", 1], G["ffn2_w_out", 1] = _ffn_bwd(
        dy, s_f2b, ffn2_pre_g[1], ffn2_post_g[1], W["ffn2_w_in", 1], W["ffn2_w_out", 1], "l1f2")
    dm_a, dg_mixpost_1, dz_a = _post_bwd_mm(dx5, m_a, mix_post_g[1], 1.0, w_o_attn, "attn_out")
    G["attn_w_o", 0] = _mm_tn_out(z_a, dm_a, "attn_out")
    dq, dk, dv, dcr = _attn_bwd(qg, kvact, dz_a, lse, c_col, c_row, Bl, S, D)
    dqg = _gate_bwd(dz_a, qg, o, dq)
    G["attn_w_qg", 0] = _mm_tn_in(xn_a, dqg, "qg")
    dx4, dg_mixpre_1 = _mm_nt_pre(dqg, W["attn_w_qg", 0], dx5, x4, mix_pre_g[1], "qg")
    dx3, dg_f1pre_1, dg_f1post_1, G["ffn1_w_in", 1], G["ffn1_w_out", 1] = _ffn_bwd(
        dx4, s_f1b, ffn1_pre_g[1], ffn1_post_g[1], W["ffn1_w_in", 1], W["ffn1_w_out", 1], "l1f1")

    dcum = jnp.pad(dcr.reshape(Bl, H, S).transpose(0, 2, 1), ((0, 0), (0, 0), (0, LANES - H))).reshape(T, LANES)
    dpf, dfb = _forget_bwd(dcum, pf, fb, Bl, S)
    dp = jnp.concatenate([dk.astype(MM_DTYPE), dv.astype(MM_DTYPE), dpf], axis=1)
    G_kv_full = _mm_tn(xn_kv, dp, "kv")
    G["kv_w", 0] = jnp.stack([jnp.pad(G_kv_full[:, s * kvc:(s + 1) * kvc], ((0, 0), (0, kvp - kvc))) for s in range(N_CHIP)])
    dx3, dg_kv = _mm_nt_pre(dp, kv_full, dx3, x3, kv_g, "kv")

    def halves_of(keys):
        return [G[k].reshape(N_CHIP, 2, G[k].shape[1] // 2, G[k].shape[2]) for k in keys]

    def pair_adds(keys, grads, recvs):
        wires, owns = [], []
        for k, g, r in zip(keys, grads, recvs):
            w, own = _rs_pair_add(g, r, place, f"{k[0]}{k[1]}")
            wires.append(w)
            owns.append(own)
        return wires, owns

    def chip_start(wires, tag, extra=()):
        lands = [lax.empty((3,) + w.shape[1:], w.dtype) for w in wires]
        lands += [jnp.zeros((N_DEV,) + e.shape, e.dtype) for e in extra]
        return _exchange_start(list(wires) + list(extra), lands, _chip_plan, 3 * len(wires) + (N_DEV - 1) * len(extra), tag)

    late = groups[2] + groups[1]
    last = groups[0]
    grads_2 = halves_of(second)
    p_sems, p_semr, grads_2, sib_2, token = _exchange_start(
        grads_2, [lax.empty((N_CHIP,) + g.shape[2:], g.dtype) for g in grads_2], _pair_plan, len(grads_2), "pair_second")

    dx2, dg_f2pre_0, dg_f2post_0, G["ffn2_w_in", 0], G["ffn2_w_out", 0] = _ffn_bwd(
        dx3, s_f2a, ffn2_pre_g[0], ffn2_post_g[0] + token[0, :1], W["ffn2_w_in", 0], W["ffn2_w_out", 0], "l0f2")
    grads_2, sib_2 = _exchange_wait(p_sems, p_semr, grads_2, sib_2, _pair_plan, dx2, "pair_second")
    wires_2, owns_2 = pair_adds(second, grads_2, sib_2)
    c_2 = chip_start(wires_2, "chip_second")
    dm_c, dg_mixpost_0, dz_c = _post_bwd_mm(dx2, m_c, mix_post_g[0] + c_2[4][0, :1], 1.0, w_o_conv, "conv_out")
    G["conv_w_out", 0] = _mm_tn_out(z_c, dm_c, "conv_out")
    db, dcg, dhh, dk_taps = _conv_bwd(bch, dz_c, k_taps, Bl, S)
    dbch = jnp.concatenate([db, dcg, dhh], axis=1)
    G["conv_w_in", 0] = _mm_tn_in(xn_c, dbch, "conv")
    dx1, dg_mixpre_0 = _mm_nt_pre(dbch, W["conv_w_in", 0], dx2, x1, mix_pre_g[0], "conv")
    def pair_start(keys, tag):
        grads = halves_of(keys)
        lands = [lax.empty((N_CHIP,) + g.shape[2:], g.dtype) for g in grads]
        return _exchange_start(grads, lands, _pair_plan, len(grads), tag)

    p_l = pair_start(late, "pair_late")
    late_done = {}

    def late_leg(dhgu):
        grads_l, sib_l = _exchange_wait(*p_l[:4], _pair_plan, dhgu, "pair_late")
        late_done["wires"], late_done["owns"] = pair_adds(late, grads_l, sib_l)
        late_done["chip"] = chip_start(late_done["wires"], "chip_late")
        return late_done["chip"][4]

    def last_pair(dw_in, dw_out):
        G["ffn1_w_in", 0], G["ffn1_w_out", 0] = dw_in, dw_out
        late_done["pair"] = pair_start(last, "pair_last")
        return late_done["pair"][4]

    dx0, dg_f1pre_0, dg_f1post_0, _, _ = _ffn_bwd(
        dx1, s_f1a, ffn1_pre_g[0], ffn1_post_g[0] + p_l[4][0, :1], W["ffn1_w_in", 0], W["ffn1_w_out", 0], "l0f1",
        between=late_leg, finish=last_pair)
    grad_x = dx0.reshape(Bl, S, D)
    owns_l, c_l, p_1 = late_done["owns"], late_done["chip"], late_done["pair"]

    grads_1, sib_1 = _exchange_wait(*p_1[:4], _pair_plan, dx0, "pair_last")
    wires_1, owns_1 = pair_adds(last, grads_1, sib_1)

    def row(v):
        return jnp.pad(v.reshape(-1), (0, D - v.size)).reshape(1, D)

    small_parts = [dg_f1pre_0, dg_f1pre_1, dg_f1post_0, dg_f1post_1, dg_mixpre_0, dg_mixpre_1, dg_mixpost_0, dg_mixpost_1,
                   dg_f2pre_0, dg_f2pre_1, dg_f2post_0, dg_f2post_1, dg_kv, row(dfb[0, :H]), dk_taps[:3],
                   jnp.full((1, D), loss_local)]
    small = jnp.concatenate(small_parts, axis=0)
    small = jnp.pad(small, ((0, SMALL_ROWS - small.shape[0]), (0, 0)))
    c_1 = chip_start(wires_1, "chip_last", extra=[small])
    _, recvs_2 = _exchange_wait(*c_2[:4], _chip_plan, c_1[4], "chip_second")
    _, recvs_l = _exchange_wait(*c_l[:4], _chip_plan, c_1[4], "chip_late")
    partial = {}

    def chip_adds(keys, owns, recvs):
        for (name, l), own, rcv in zip(keys, owns, recvs):
            partial[name] = _rs_chip_add(own, rcv, place, l, shards[name].shape[0], partial.get(name), f"{name}{l}")

    chip_adds(late + second, owns_l + owns_2, recvs_l + recvs_2)
    res = {}

    def adamw(names, reduced, after):
        for k, red in zip(names, reduced):
            w, m, v = given[k]
            g2 = red.reshape(-1, red.shape[-1])
            if k == "kv_w":
                g2 = g2[:, :kvc]
            flat = lambda a: a.reshape(-1, a.shape[-1])
            go, d, mn, vn = _adamw(flat(w), g2, flat(m), flat(v), k, after=after)
            res[k] = tuple(a.reshape(w.shape) for a in (go, d, mn, vn))
        return d

    early = [k for k in partial if (k, 0) not in last]
    done = adamw(early, _rs_pair_share([partial[k] for k in early], "early"), c_1[4])
    _, recvs_1 = _exchange_wait(*c_1[:4], _chip_plan, done, "chip_last")
    chip_adds(last, owns_1, recvs_1[:-1])
    rest = [k for k, _ in last]
    adamw(rest, _rs_pair_share([partial[k] for k in rest], "last"), None)
    gsum = _sum_devices(recvs_1[-1], small, place)
    loss = gsum[17, 0]

    small_names = ["ffn1_pre_g", "ffn1_post_g", "mix_pre_g", "mix_post_g", "ffn2_pre_g", "ffn2_post_g"]
    small_given = dict(ffn1_pre_g=(ffn1_pre_g, m_ffn1_pre_g, v_ffn1_pre_g), ffn1_post_g=(ffn1_post_g, m_ffn1_post_g, v_ffn1_post_g),
                       mix_pre_g=(mix_pre_g, m_mix_pre_g, v_mix_pre_g), mix_post_g=(mix_post_g, m_mix_post_g, v_mix_post_g),
                       ffn2_pre_g=(ffn2_pre_g, m_ffn2_pre_g, v_ffn2_pre_g), ffn2_post_g=(ffn2_post_g, m_ffn2_post_g, v_ffn2_post_g))

    def pack(idx):
        rows_ = [small_given[k][idx] for k in small_names]
        rows_ += [row((kv_g, m_kv_g, v_kv_g)[idx]), row((forget_b, m_forget_b, v_forget_b)[idx])]
        rows_.append(jnp.pad((conv_k, m_conv_k, v_conv_k)[idx][0], ((0, 0), (0, D - dk_cols))))
        a = jnp.concatenate(rows_, axis=0)
        return jnp.pad(a, ((0, SMALL_ROWS - a.shape[0]), (0, 0)))

    g_taps = lax.dynamic_slice_in_dim(gsum[14:17], chip * dk_cols, dk_cols, axis=1)
    g_small = jnp.concatenate([gsum[:14], jnp.pad(g_taps, ((0, 0), (0, D - dk_cols))), gsum[17:]], axis=0)
    g_small, d_s, m_s, v_s = _adamw(pack(0), g_small, pack(1), pack(2), "small")
    for i, k in enumerate(small_names):
        res[k] = tuple(a[2 * i:2 * i + 2] for a in (g_small, d_s, m_s, v_s))
    res["kv_g"] = tuple(a[12] for a in (g_small, d_s, m_s, v_s))
    res["forget_b"] = tuple(a[13, :H] for a in (g_small, d_s, m_s, v_s))
    res["conv_k"] = tuple(a[14:17, :dk_cols][None] for a in (g_small, d_s, m_s, v_s))

    order = ["ffn1_pre_g", "ffn1_post_g", "ffn1_w_in", "ffn1_w_out", "mix_pre_g", "mix_post_g", "ffn2_pre_g", "ffn2_post_g",
             "ffn2_w_in", "ffn2_w_out", "conv_w_in", "conv_k", "conv_w_out", "kv_g", "kv_w", "forget_b", "attn_w_qg", "attn_w_o"]
    out = [loss, grad_x]
    for idx in range(4):
        out += [res[k][idx] for k in order]
    return tuple(out)
```

```python
import math

import jax
import jax.numpy as jnp
from jax import lax
from jax.experimental import pallas as pl
from jax.experimental.pallas import tpu as pltpu

F32 = jnp.float32
MM_DTYPE = jnp.bfloat16
WIRE_DTYPE = jnp.bfloat16

RMS_EPS = 1e-6
ADAM_LR = 0.001
ADAM_B1 = 0.9
ADAM_B2 = 0.999
ADAM_EPS = 1e-08
ADAM_WD = 0.01
ADAM_STEP = 10

HEAD_DIM = 64
LANES = 128
N_CHIP = 4
N_DEV = 8
ROW_TILE = 512
MM_TILE = 512
FUSED_TILE = 512
TN_TILE = 2048
ATT_BLOCK = 512
SMALL_ROWS = 24
V7X_VMEM_BYTES = 64 * 1024 * 1024
VMEM_LIMIT = V7X_VMEM_BYTES - 8 * 1024 * 1024
MESH = pl.DeviceIdType.MESH
ANY = pl.BlockSpec(memory_space=pl.ANY)

NT = (((1,), (1,)), ((), ()))
TN = (((0,), (0,)), ((), ()))


def _tile(n, pref):
    if n <= pref:
        return n
    t = pref - pref % 16
    while n % t:
        t -= 16
    return t


def _params():
    return pltpu.CompilerParams(vmem_limit_bytes=VMEM_LIMIT)


def _sds(shape, dtype):
    return jax.ShapeDtypeStruct(shape, dtype)


def _rows(tm, c):
    return pl.BlockSpec((tm, c), lambda i: (i, 0))


def _whole(shape):
    return pl.BlockSpec(shape, lambda *_: (0,) * len(shape))


def _resident(shape):
    return pl.BlockSpec(shape, lambda *_: (0,) * len(shape), pipeline_mode=pl.Buffered(1))


def _rms_fwd(x, g, tag):
    T, D = x.shape
    tm = _tile(T, ROW_TILE)

    def body(x_ref, g_ref, o_ref):
        xv = x_ref[...]
        r = lax.rsqrt(jnp.mean(xv * xv, axis=-1, keepdims=True) + RMS_EPS)
        o_ref[...] = (xv * r * g_ref[...]).astype(o_ref.dtype)

    return pl.pallas_call(
        body, name=f"rms_fwd_{tag}", grid=(T // tm,),
        in_specs=[_rows(tm, D), _whole((1, D))], out_specs=_rows(tm, D),
        out_shape=_sds((T, D), MM_DTYPE), compiler_params=_params())(x, g.reshape(1, D))


def _accumulate(ref, part, first):
    @pl.when(first)
    def _():
        ref[...] = part

    @pl.when(jnp.logical_not(first))
    def _():
        ref[...] += part


def _loss_grad(y, tgt):
    T, D = y.shape
    tm = _tile(T, ROW_TILE)

    def body(y_ref, t_ref, dy_ref, l_ref):
        e = y_ref[...] - t_ref[...]
        row = jnp.mean(e * e, axis=-1, keepdims=True)
        part = jnp.broadcast_to(jnp.sum(row, axis=0, keepdims=True), (8, LANES))
        _accumulate(l_ref, part, pl.program_id(0) == 0)
        dy_ref[...] = e * (1.0 / D)

    dy, lsum = pl.pallas_call(
        body, name="loss_grad", grid=(T // tm,),
        in_specs=[_rows(tm, D), _rows(tm, D)], out_specs=[_rows(tm, D), _whole((8, LANES))],
        out_shape=[_sds((T, D), F32), _sds((8, LANES), F32)], compiler_params=_params())(y, tgt)
    return dy, 0.5 * lsum[0, 0]


def _shift_down(u, d, rows):
    return jnp.where(rows >= d, pltpu.roll(u, d, 0), 0.0)


def _shift_up(u, d, rows, S):
    return jnp.where(rows < S - d, pltpu.roll(u, S - d, 0), 0.0)


def _conv_fwd(bch, k8, Bl, S):
    T, D3 = bch.shape
    D = D3 // 3
    dc = min(D, 2 * LANES)
    nd = D // dc

    def body(b_ref, c_ref, h_ref, k_ref, z_ref):
        rows = lax.broadcasted_iota(jnp.int32, (S, 1), 0)
        u = c_ref[...].astype(F32) * h_ref[...].astype(F32)
        y = k_ref[2:3, :] * u + k_ref[1:2, :] * _shift_down(u, 1, rows) + k_ref[0:1, :] * _shift_down(u, 2, rows)
        z_ref[...] = (b_ref[...].astype(F32) * y).astype(z_ref.dtype)

    return pl.pallas_call(
        body, name="conv_fwd", grid=(Bl, nd),
        in_specs=[pl.BlockSpec((S, dc), lambda b, j: (b, j)),
                  pl.BlockSpec((S, dc), lambda b, j: (b, nd + j)),
                  pl.BlockSpec((S, dc), lambda b, j: (b, 2 * nd + j)),
                  pl.BlockSpec((8, dc), lambda b, j: (0, j))],
        out_specs=pl.BlockSpec((S, dc), lambda b, j: (b, j)),
        out_shape=_sds((T, D), MM_DTYPE), compiler_params=_params())(bch, bch, bch, k8)


def _conv_bwd(bch, dz, k8, Bl, S):
    T, D3 = bch.shape
    D = D3 // 3
    dc = min(D, 2 * LANES)
    nd = D // dc

    def body(b_ref, c_ref, h_ref, dz_ref, k_ref, db_ref, dc_ref, dh_ref, dk_ref):
        rows = lax.broadcasted_iota(jnp.int32, (S, 1), 0)
        bv = b_ref[...].astype(F32)
        cv = c_ref[...].astype(F32)
        hv = h_ref[...].astype(F32)
        dzv = dz_ref[...].astype(F32)
        u = cv * hv
        u1 = _shift_down(u, 1, rows)
        u2 = _shift_down(u, 2, rows)
        y = k_ref[2:3, :] * u + k_ref[1:2, :] * u1 + k_ref[0:1, :] * u2
        db_ref[...] = (dzv * y).astype(db_ref.dtype)
        dy = dzv * bv
        du = k_ref[2:3, :] * dy + k_ref[1:2, :] * _shift_up(dy, 1, rows, S) + k_ref[0:1, :] * _shift_up(dy, 2, rows, S)
        dc_ref[...] = (du * hv).astype(dc_ref.dtype)
        dh_ref[...] = (du * cv).astype(dh_ref.dtype)

        @pl.when(pl.program_id(1) == 0)
        def _():
            dk_ref[...] = jnp.zeros_like(dk_ref)

        dk_ref[0:1, :] += jnp.sum(dy * u2, axis=0, keepdims=True)
        dk_ref[1:2, :] += jnp.sum(dy * u1, axis=0, keepdims=True)
        dk_ref[2:3, :] += jnp.sum(dy * u, axis=0, keepdims=True)

    seq = lambda off: pl.BlockSpec((S, dc), lambda j, b: (b, off + j))
    return pl.pallas_call(
        body, name="conv_bwd", grid=(nd, Bl),
        in_specs=[seq(0), seq(nd), seq(2 * nd), seq(0), pl.BlockSpec((8, dc), lambda j, b: (0, j))],
        out_specs=[seq(0), seq(0), seq(0), pl.BlockSpec((8, dc), lambda j, b: (0, j))],
        out_shape=[_sds((T, D), MM_DTYPE)] * 3 + [_sds((8, D), F32)],
        compiler_params=_params())(bch, bch, bch, dz, k8)


def _forget_fwd(pf, fb, Bl, S):
    T = pf.shape[0]

    def body(p_ref, fb_ref, c_ref):
        rows = lax.broadcasted_iota(jnp.int32, (S, 1), 0)
        z = p_ref[...] + fb_ref[...]
        acc = jnp.minimum(z, 0.0) - jnp.log1p(jnp.exp(-jnp.abs(z)))
        d = 1
        while d < S:
            acc = acc + _shift_down(acc, d, rows)
            d *= 2
        c_ref[...] = acc

    return pl.pallas_call(
        body, name="forget_fwd", grid=(Bl,),
        in_specs=[_rows(S, LANES), _whole((1, LANES))], out_specs=_rows(S, LANES),
        out_shape=_sds((T, LANES), F32), compiler_params=_params())(pf, fb)


def _forget_bwd(dc, pf, fb, Bl, S):
    T = pf.shape[0]

    def body(dc_ref, p_ref, fb_ref, df_ref, dfb_ref):
        rows = lax.broadcasted_iota(jnp.int32, (S, 1), 0)
        acc = dc_ref[...]
        d = 1
        while d < S:
            acc = acc + _shift_up(acc, d, rows, S)
            d *= 2
        df = acc * jax.nn.sigmoid(-(p_ref[...] + fb_ref[...]))
        df_ref[...] = df.astype(df_ref.dtype)
        _accumulate(dfb_ref, jnp.sum(df, axis=0, keepdims=True), pl.program_id(0) == 0)

    return pl.pallas_call(
        body, name="forget_bwd", grid=(Bl,),
        in_specs=[_rows(S, LANES), _rows(S, LANES), _whole((1, LANES))],
        out_specs=[_rows(S, LANES), _whole((1, LANES))],
        out_shape=[_sds((T, LANES), MM_DTYPE), _sds((1, LANES), F32)],
        compiler_params=_params())(dc, pf, fb)


def _head_mask(h):
    lane = lax.broadcasted_iota(jnp.int32, (1, LANES), 1)
    return (lane >= h * HEAD_DIM) & (lane < (h + 1) * HEAD_DIM)


def _attn_fwd(qg, kv, c_col, c_row, Bl, S, D):
    T = Bl * S
    H = D // HEAD_DIM
    HP = D // LANES
    bq = min(S, ATT_BLOCK)
    nq = S // bq
    scale = 1.0 / math.sqrt(HEAD_DIM)

    def body(q_ref, g_ref, k_ref, v_ref, cc_ref, cr_ref, o_ref, lse_ref, z_ref):
        i = pl.program_id(2)
        q2 = q_ref[...]
        qh = [q2 * (_head_mask(h).astype(F32) * scale).astype(q2.dtype) for h in range(2)]
        cc = [cc_ref[h][:, :1] for h in range(2)]
        diag = lax.broadcasted_iota(jnp.int32, (1, bq), 1) <= lax.broadcasted_iota(jnp.int32, (bq, 1), 0)

        def block(j, carry, on_diagonal):
            off = pl.multiple_of(j * bq, bq)
            kj = k_ref[pl.ds(off, bq), :]
            vj = v_ref[pl.ds(off, bq), :]
            new = []
            for h in range(2):
                m, l, acc = carry[h]
                s = lax.dot_general(qh[h], kj, NT, preferred_element_type=F32) + cc[h] - cr_ref[h, j]
                if on_diagonal:
                    s = jnp.where(diag, s, -jnp.inf)
                m_new = jnp.maximum(m, jnp.max(s, axis=1, keepdims=True))
                p = jnp.exp(s - m_new)
                a = jnp.exp(m - m_new)
                l = a * l + jnp.sum(p, axis=1, keepdims=True)
                acc = a * acc + jnp.dot(p.astype(MM_DTYPE), vj, preferred_element_type=F32)
                new.append((m_new, l, acc))
            return tuple(new)

        one = (jnp.full((bq, 1), -jnp.inf, F32), jnp.zeros((bq, 1), F32), jnp.zeros((bq, LANES), F32))
        carry = lax.fori_loop(0, i, lambda j, c: block(j, c, False), (one, one))
        carry = block(i, carry, True)
        outs = []
        for h in range(2):
            m, l, acc = carry[h]
            outs.append(acc / l)
            lse_ref[h] = jnp.broadcast_to(m + jnp.log(l), (bq, LANES))
        o2 = jnp.where(_head_mask(0), outs[0], outs[1])
        o_ref[...] = o2
        z_ref[...] = (jax.nn.sigmoid(g_ref[...].astype(F32)) * o2).astype(z_ref.dtype)

    return pl.pallas_call(
        body, name="attn_fwd", grid=(Bl, HP, nq),
        in_specs=[pl.BlockSpec((bq, LANES), lambda b, hp, i: (b * nq + i, hp)),
                  pl.BlockSpec((bq, LANES), lambda b, hp, i: (b * nq + i, HP + hp)),
                  pl.BlockSpec((S, LANES), lambda b, hp, i: (b, hp)),
                  pl.BlockSpec((S, LANES), lambda b, hp, i: (b, HP + hp)),
                  pl.BlockSpec((None, 2, bq, LANES), lambda b, hp, i: (b, hp, i, 0)),
                  pl.BlockSpec((None, 2, nq, 1, bq), lambda b, hp, i: (b, hp, 0, 0, 0))],
        out_specs=[pl.BlockSpec((bq, LANES), lambda b, hp, i: (b * nq + i, hp)),
                   pl.BlockSpec((None, 2, bq, LANES), lambda b, hp, i: (b, hp, i, 0)),
                   pl.BlockSpec((bq, LANES), lambda b, hp, i: (b * nq + i, hp))],
        out_shape=[_sds((T, D), F32), _sds((Bl, H, S, LANES), F32), _sds((T, D), MM_DTYPE)],
        compiler_params=_params())(qg, qg, kv, kv, c_col, c_row)


def _attn_bwd(qg, kv, dz, lse, c_col, c_row, Bl, S, D):
    T = Bl * S
    H = D // HEAD_DIM
    HP = D // LANES
    bq = min(S, ATT_BLOCK)
    nq = S // bq
    scale = 1.0 / math.sqrt(HEAD_DIM)

    def body(q_ref, g_ref, k_ref, v_ref, dz_ref, lse_ref, cc_ref, cr_ref, dq_ref, dk_ref, dv_ref, dcr_ref, p_sc, dp_sc):
        i = pl.program_id(2)

        @pl.when(i == 0)
        def _():
            dk_ref[...] = jnp.zeros_like(dk_ref)
            dv_ref[...] = jnp.zeros_like(dv_ref)
            dcr_ref[...] = jnp.zeros_like(dcr_ref)

        q2 = q_ref[...]
        do2 = (dz_ref[...].astype(F32) * jax.nn.sigmoid(g_ref[...].astype(F32))).astype(MM_DTYPE)
        masks = [_head_mask(h).astype(F32) for h in range(2)]
        qh = [q2 * (masks[h] * scale).astype(q2.dtype) for h in range(2)]
        doh = [do2 * masks[h].astype(do2.dtype) for h in range(2)]
        cc = [cc_ref[h][:, :1] for h in range(2)]
        lse = [lse_ref[h][:, :1] for h in range(2)]
        diag = lax.broadcasted_iota(jnp.int32, (1, bq), 1) <= lax.broadcasted_iota(jnp.int32, (bq, 1), 0)

        def sweep1(j, delta, on_diagonal):
            off = pl.multiple_of(j * bq, bq)
            kj = k_ref[pl.ds(off, bq), :]
            vj = v_ref[pl.ds(off, bq), :]
            new = []
            dv = None
            for h in range(2):
                s = lax.dot_general(qh[h], kj, NT, preferred_element_type=F32) + cc[h] - cr_ref[h, j]
                if on_diagonal:
                    s = jnp.where(diag, s, -jnp.inf)
                p = jnp.exp(s - lse[h])
                dp = lax.dot_general(doh[h], vj, NT, preferred_element_type=F32)
                p_sc[h, j] = p
                dp_sc[h, j] = dp
                part = lax.dot_general(p.astype(MM_DTYPE), doh[h], TN, preferred_element_type=F32)
                dv = part if dv is None else dv + part
                new.append(delta[h] + jnp.sum(p * dp, axis=1, keepdims=True))
            dv_ref[pl.ds(off, bq), :] += dv
            return tuple(new)

        zero = jnp.zeros((bq, 1), F32)
        delta = lax.fori_loop(0, i, lambda j, d: sweep1(j, d, False), (zero, zero))
        delta = sweep1(i, delta, True)

        def sweep2(j, dq):
            off = pl.multiple_of(j * bq, bq)
            kj = k_ref[pl.ds(off, bq), :]
            dk = None
            for h in range(2):
                ds = p_sc[h, j] * (dp_sc[h, j] - delta[h])
                dcr_ref[h, j] -= jnp.sum(ds, axis=0, keepdims=True)
                dsb = ds.astype(MM_DTYPE)
                dq = dq + jnp.dot(dsb, kj * (masks[h] * scale).astype(kj.dtype), preferred_element_type=F32)
                part = lax.dot_general(dsb, qh[h], TN, preferred_element_type=F32)
                dk = part if dk is None else dk + part
            dk_ref[pl.ds(off, bq), :] += dk
            return dq

        dq_ref[...] = lax.fori_loop(0, i + 1, sweep2, jnp.zeros((bq, LANES), F32))

    blk = lambda col: pl.BlockSpec((bq, LANES), lambda b, hp, i: (b * nq + i, col(hp)))
    seq = lambda col: pl.BlockSpec((S, LANES), lambda b, hp, i: (b, col(hp)))
    per_head = pl.BlockSpec((None, 2, bq, LANES), lambda b, hp, i: (b, hp, i, 0))
    rows = pl.BlockSpec((None, 2, nq, 1, bq), lambda b, hp, i: (b, hp, 0, 0, 0))
    return pl.pallas_call(
        body, name="attn_bwd", grid=(Bl, HP, nq),
        in_specs=[blk(lambda hp: hp), blk(lambda hp: HP + hp), seq(lambda hp: hp), seq(lambda hp: HP + hp),
                  blk(lambda hp: hp), per_head, per_head, rows],
        out_specs=[blk(lambda hp: hp), seq(lambda hp: hp), seq(lambda hp: hp), rows],
        out_shape=[_sds((T, D), F32), _sds((T, D), F32), _sds((T, D), F32), _sds((Bl, H, nq, 1, bq), F32)],
        scratch_shapes=[pltpu.VMEM((2, nq, bq, bq), F32), pltpu.VMEM((2, nq, bq, bq), F32)],
        compiler_params=_params())(qg, qg, kv, kv, dz, lse, c_col, c_row)


def _pack_dkv(dk, dv, dpf):
    T, D = dk.shape
    tm = _tile(T, ROW_TILE)

    def body(dk_ref, dv_ref, df_ref, o_ref):
        o_ref[:, :D] = dk_ref[...].astype(o_ref.dtype)
        o_ref[:, D:2 * D] = dv_ref[...].astype(o_ref.dtype)
        o_ref[:, 2 * D:] = df_ref[...]

    return pl.pallas_call(
        body, name="pack_dkv", grid=(T // tm,),
        in_specs=[_rows(tm, D), _rows(tm, D), _rows(tm, LANES)], out_specs=_rows(tm, 2 * D + LANES),
        out_shape=_sds((T, 2 * D + LANES), MM_DTYPE), compiler_params=_params())(dk, dv, dpf)


def _gate_bwd(dz, qg, o, dq):
    T, D = dz.shape
    tm = _tile(T, ROW_TILE)

    def body(dz_ref, g_ref, o_ref, dq_ref, out_ref):
        g = g_ref[...].astype(F32)
        sg = jax.nn.sigmoid(g)
        out_ref[:, :D] = dq_ref[...].astype(out_ref.dtype)
        out_ref[:, D:] = (dz_ref[...].astype(F32) * o_ref[...] * sg * (1.0 - sg)).astype(out_ref.dtype)

    return pl.pallas_call(
        body, name="gate_bwd", grid=(T // tm,),
        in_specs=[_rows(tm, D), pl.BlockSpec((tm, D), lambda i: (i, 1)), _rows(tm, D), _rows(tm, D)],
        out_specs=_rows(tm, 2 * D), out_shape=_sds((T, 2 * D), MM_DTYPE),
        compiler_params=_params())(dz, qg, o, dq)


def _mm_nn(a, b, out_dtype, tag):
    T, K = a.shape
    N = b.shape[1]
    tm = _tile(T, MM_TILE)

    def body(a_ref, b_ref, o_ref):
        o_ref[...] = jnp.dot(a_ref[...], b_ref[...], preferred_element_type=F32).astype(o_ref.dtype)

    return pl.pallas_call(
        body, name=f"mm_nn_{tag}", grid=(T // tm,),
        in_specs=[_rows(tm, K), _whole((K, N))], out_specs=_rows(tm, N),
        out_shape=_sds((T, N), out_dtype), compiler_params=_params())(a, b)


def _mm_tn_in(a, dy, tag, after=None):
    T, K = a.shape
    n = dy.shape[1] // N_CHIP
    tt = _tile(T, TN_TILE)
    extra = [] if after is None else [after]

    def body(a_ref, d_ref, *rest):
        part = lax.dot_general(a_ref[...], d_ref[...], TN, preferred_element_type=F32)
        _accumulate(rest[-1], part, pl.program_id(1) == 0)

    return pl.pallas_call(
        body, name=f"mm_tn_in_{tag}", grid=(N_CHIP, T // tt),
        in_specs=[pl.BlockSpec((tt, K), lambda s, t: (t, 0)), pl.BlockSpec((tt, n), lambda s, t: (t, s))]
        + [ANY] * len(extra),
        out_specs=pl.BlockSpec((None, K, n), lambda s, t: (s, 0, 0)), out_shape=_sds((N_CHIP, K, n), F32),
        compiler_params=_params())(a, dy, *extra)


def _mm_tn_out(act, dh, tag, after=None):
    T, R4 = act.shape
    D = dh.shape[1]
    r = R4 // N_CHIP
    g = 1 if r % LANES == 0 else 2
    tt = _tile(T, TN_TILE)
    extra = [] if after is None else [after]

    def body(a_ref, d_ref, *rest):
        o_ref = rest[-1]
        part = lax.dot_general(a_ref[...], d_ref[...], TN, preferred_element_type=F32)
        first = pl.program_id(1) == 0
        for q in range(g):
            _accumulate(o_ref.at[q], part[q * r:(q + 1) * r], first)

    return pl.pallas_call(
        body, name=f"mm_tn_out_{tag}", grid=(N_CHIP // g, T // tt),
        in_specs=[pl.BlockSpec((tt, g * r), lambda s, t: (t, s)), pl.BlockSpec((tt, D), lambda s, t: (t, 0))]
        + [ANY] * len(extra),
        out_specs=pl.BlockSpec((g, r, D), lambda s, t: (s, 0, 0)), out_shape=_sds((N_CHIP, r, D), F32),
        compiler_params=_params())(act, dh, *extra)


def _mm_tn(a, b, tag):
    T, K = a.shape
    N = b.shape[1]
    tt = _tile(T, TN_TILE)

    def body(a_ref, b_ref, o_ref):
        part = lax.dot_general(a_ref[...], b_ref[...], TN, preferred_element_type=F32)
        _accumulate(o_ref, part, pl.program_id(0) == 0)

    return pl.pallas_call(
        body, name=f"mm_tn_{tag}", grid=(T // tt,),
        in_specs=[_rows(tt, K), _rows(tt, N)], out_specs=_whole((K, N)),
        out_shape=_sds((K, N), F32), compiler_params=_params())(a, b)


def _norm_mm_in(x, g, wg, tag, swiglu=False):
    T, D = x.shape
    n = wg.shape[-1]
    tm = _tile(T, FUSED_TILE)
    half = N_CHIP // 2

    def body(x_ref, g_ref, w_ref, xn_ref, y_ref, *rest):
        xv = x_ref[...]
        r = lax.rsqrt(jnp.mean(xv * xv, axis=-1, keepdims=True) + RMS_EPS)
        xn = (xv * r * g_ref[...]).astype(xn_ref.dtype)
        xn_ref[...] = xn

        def product(s):
            p = jnp.dot(xn, w_ref[s], preferred_element_type=F32)
            y_ref[:, s * n:(s + 1) * n] = p.astype(y_ref.dtype)
            return p

        if swiglu:
            for q in range(half):
                gate, up = product(q), product(half + q)
                rest[0][:, q * n:(q + 1) * n] = (gate * jax.nn.sigmoid(gate) * up).astype(rest[0].dtype)
        else:
            for s in range(N_CHIP):
                product(s)

    out_specs = [_rows(tm, D), _rows(tm, N_CHIP * n)]
    out_shape = [_sds((T, D), MM_DTYPE), _sds((T, N_CHIP * n), MM_DTYPE)]
    if swiglu:
        out_specs.append(_rows(tm, half * n))
        out_shape.append(_sds((T, half * n), MM_DTYPE))
    return pl.pallas_call(
        body, name=f"norm_mm_in_{tag}", grid=(T // tm,),
        in_specs=[_rows(tm, D), _whole((1, D)), _resident((N_CHIP, D, n))], out_specs=out_specs,
        out_shape=out_shape, compiler_params=_params())(x, g.reshape(1, D), wg)


def _mm_out_post(a, b, x, g, alpha, tag):
    T, K = a.shape
    D = b.shape[1]
    tm = _tile(T, FUSED_TILE)

    def body(a_ref, b_ref, x_ref, g_ref, h_ref, o_ref):
        hv = jnp.dot(a_ref[...], b_ref[...], preferred_element_type=F32)
        h_ref[...] = hv
        r = lax.rsqrt(jnp.mean(hv * hv, axis=-1, keepdims=True) + RMS_EPS)
        o_ref[...] = x_ref[...] + alpha * (hv * r * g_ref[...])

    return pl.pallas_call(
        body, name=f"mm_out_post_{tag}", grid=(T // tm,),
        in_specs=[_rows(tm, K), _resident((K, D)), _rows(tm, D), _whole((1, D))],
        out_specs=[_rows(tm, D), _rows(tm, D)], out_shape=[_sds((T, D), F32)] * 2,
        compiler_params=_params())(a, b, x, g.reshape(1, D))


def _post_bwd_mm(dx, h, g, alpha, b, tag, hgu=None):
    T, D = dx.shape
    K = b.shape[0]
    tm = _tile(T, FUSED_TILE)

    def body(dx_ref, h_ref, g_ref, b_ref, *rest):
        dh_ref, dg_ref, out_ref = rest[-3:]
        hv = h_ref[...]
        r = lax.rsqrt(jnp.mean(hv * hv, axis=-1, keepdims=True) + RMS_EPS)
        hh = hv * r
        dyn = alpha * dx_ref[...]
        _accumulate(dg_ref, jnp.sum(dyn * hh, axis=0, keepdims=True), pl.program_id(0) == 0)
        dhh = dyn * g_ref[...]
        dh = (r * (dhh - hh * jnp.mean(dhh * hh, axis=-1, keepdims=True))).astype(dh_ref.dtype)
        dh_ref[...] = dh
        da = lax.dot_general(dh, b_ref[...], NT, preferred_element_type=F32)
        if hgu is None:
            out_ref[...] = da.astype(out_ref.dtype)
        else:
            gate = rest[0][:, :K]
            up = rest[0][:, K:]
            dab = da.astype(gate.dtype)
            sg = jax.nn.sigmoid(gate)
            out_ref[:, :K] = (dab * up * sg * (1.0 + gate * (1.0 - sg))).astype(out_ref.dtype)
            out_ref[:, K:] = (dab * gate * sg).astype(out_ref.dtype)

    in_specs = [_rows(tm, D), _rows(tm, D), _whole((1, D)), _resident((K, D))]
    args = [dx, h, g.reshape(1, D), b]
    wide = K
    if hgu is not None:
        wide = 2 * K
        in_specs.append(_rows(tm, wide))
        args.append(hgu)
    return pl.pallas_call(
        body, name=f"post_bwd_mm_{tag}", grid=(T // tm,), in_specs=in_specs,
        out_specs=[_rows(tm, D), _whole((1, D)), _rows(tm, wide)],
        out_shape=[_sds((T, D), MM_DTYPE), _sds((1, D), F32), _sds((T, wide), MM_DTYPE)],
        compiler_params=_params())(*args)


def _mm_nt_pre(dy, w, dres, x, g, tag):
    T, C = dy.shape
    D = x.shape[1]
    tm = _tile(T, FUSED_TILE)
    n = w.shape[-1]

    def body(dy_ref, w_ref, dres_ref, x_ref, g_ref, dx_ref, dg_ref):
        if w.ndim == 2:
            dn = lax.dot_general(dy_ref[...], w_ref[...], NT, preferred_element_type=F32)
        else:
            dn = None
            for s in range(N_CHIP):
                part = lax.dot_general(dy_ref[:, s * n:(s + 1) * n], w_ref[s], NT, preferred_element_type=F32)
                dn = part if dn is None else dn + part
        xv = x_ref[...]
        r = lax.rsqrt(jnp.mean(xv * xv, axis=-1, keepdims=True) + RMS_EPS)
        xh = xv * r
        _accumulate(dg_ref, jnp.sum(dn * xh, axis=0, keepdims=True), pl.program_id(0) == 0)
        dxh = dn * g_ref[...]
        dx_ref[...] = dres_ref[...] + r * (dxh - xh * jnp.mean(dxh * xh, axis=-1, keepdims=True))

    return pl.pallas_call(
        body, name=f"mm_nt_pre_{tag}", grid=(T // tm,),
        in_specs=[_rows(tm, C), _resident(w.shape), _rows(tm, D), _rows(tm, D), _whole((1, D))],
        out_specs=[_rows(tm, D), _whole((1, D))], out_shape=[_sds((T, D), F32), _sds((1, D), F32)],
        compiler_params=_params())(dy, w, dres, x, g.reshape(1, D))


def _adamw(w, g, m, v, tag, after=None):
    R, C = w.shape
    tr = _tile(R, ROW_TILE)
    extra = [] if after is None else [after]

    def body(w_ref, g_ref, m_ref, v_ref, *rest):
        go_ref, d_ref, mo_ref, vo_ref = rest[-4:]
        gv = g_ref[...]
        go_ref[...] = gv
        mn = ADAM_B1 * m_ref[...] + (1.0 - ADAM_B1) * gv
        vn = ADAM_B2 * v_ref[...] + (1.0 - ADAM_B2) * (gv * gv)
        m_hat = mn / (1.0 - ADAM_B1 ** ADAM_STEP)
        v_hat = vn / (1.0 - ADAM_B2 ** ADAM_STEP)
        d_ref[...] = -ADAM_LR * (m_hat / (jnp.sqrt(v_hat) + ADAM_EPS) + ADAM_WD * w_ref[...])
        mo_ref[...] = mn
        vo_ref[...] = vn

    return pl.pallas_call(
        body, name=f"adamw_{tag}", grid=(R // tr,),
        in_specs=[_rows(tr, C)] * 4 + [ANY] * len(extra), out_specs=[_rows(tr, C)] * 4,
        out_shape=[_sds((R, C), F32)] * 4, compiler_params=_params())(w, g, m, v, *extra)


def _sum_devices(gall, own, place):
    _, R, C = gall.shape

    def body(place_ref, g_ref, s_ref, o_ref):
        me = 2 * place_ref[1] + place_ref[0]
        acc = None
        for d in range(N_DEV):
            term = jnp.where(me == d, s_ref[...], g_ref[d])
            acc = term if acc is None else acc + term
        o_ref[...] = acc

    grid_spec = pltpu.PrefetchScalarGridSpec(
        num_scalar_prefetch=1, grid=(1,),
        in_specs=[pl.BlockSpec((N_DEV, R, C), lambda i, p: (0, 0, 0)), pl.BlockSpec((R, C), lambda i, p: (0, 0))],
        out_specs=pl.BlockSpec((R, C), lambda i, p: (0, 0)))
    return pl.pallas_call(
        body, name="sum_devices", grid_spec=grid_spec, out_shape=_sds((R, C), F32),
        compiler_params=_params())(place, gall, own)


HBM = pl.BlockSpec(memory_space=pltpu.HBM)
SEM = pl.BlockSpec(memory_space=pltpu.SEMAPHORE)
EFFECT = pltpu.SideEffectType.DATAFLOW_SIDE_EFFECTING


def _place():
    x, y, c = lax.axis_index("x"), lax.axis_index("y"), lax.axis_index("c")
    chips = ((1 - x, y), (x, 1 - y), (1 - x, 1 - y))
    return x, y, c, chips


def _remote(src, dst, send_sem, recv_sem, dev):
    return pltpu.make_async_remote_copy(src_ref=src, dst_ref=dst, send_sem=send_sem, recv_sem=recv_sem,
                                        device_id=dev, device_id_type=MESH)


def _in_hbm(a):
    return pltpu.with_memory_space_constraint(a, pltpu.HBM)


def _own_slot(w4, l, dtype, place, tag):
    _, _, r, col = w4.shape
    tr = _tile(r, 2 * ROW_TILE)

    def body(place_ref, x_ref, o_ref):
        o_ref[...] = x_ref[...].astype(o_ref.dtype)

    grid_spec = pltpu.PrefetchScalarGridSpec(
        num_scalar_prefetch=1, grid=(2, r // tr),
        in_specs=[pl.BlockSpec((None, None, tr, col), lambda h, i, p: (l, h, i, 0))],
        out_specs=pl.BlockSpec((None, None, tr, col), lambda h, i, p: (p[1], h, i, 0)))
    return pl.pallas_call(
        body, name=f"own_slot_{tag}", grid_spec=grid_spec, out_shape=_sds((N_CHIP, 2, r, col), dtype),
        compiler_params=_params())(place, w4)


def _gather_start(bufs, after, tag):
    n = len(bufs)

    def body(*refs):
        ins = refs[:n]
        s_sem, r_sem, token = refs[n + 1], refs[n + 2], refs[2 * n + 3]
        x, y, c, chips = _place()
        me = 2 * x + y
        for i in range(n):
            mine = ins[i].at[me, c]
            for j, (px, py) in enumerate(chips):
                _remote(mine, mine, s_sem.at[3 * i + j], r_sem.at[3 * i + j], (px, py, c)).start()
        token[...] = jnp.zeros_like(token)

    dma = pltpu.SemaphoreType.DMA
    res = pl.pallas_call(
        body, name=f"gather_start_{tag}", in_specs=[HBM] * n + [ANY],
        out_specs=[SEM, SEM] + [HBM] * n + [pl.BlockSpec(memory_space=pltpu.VMEM)],
        out_shape=[dma((3 * n,)), dma((3 * n,))] + [pltpu.HBM(b.shape, b.dtype) for b in bufs] + [_sds((8, LANES), F32)],
        input_output_aliases={i: i + 2 for i in range(n)},
        compiler_params=pltpu.CompilerParams(has_side_effects=EFFECT),
        )(*[_in_hbm(b) for b in bufs], after)
    return res[0], res[1], list(res[2:2 + n]), res[-1]


def _gather_pass(s_sem, r_sem, bufs, first, after, tag):
    n = len(bufs)

    def body(*refs):
        ins = refs[:n]
        a_s, a_r, b_s, b_r = refs[n], refs[n + 1], refs[n + 3], refs[n + 4]
        x, y, c, chips = _place()
        me = 2 * x + y
        sib = (x, y, 1 - c)
        for i in range(n):
            mine = ins[i].at[me, c]
            for j, (px, py) in enumerate(chips):
                k = 3 * (first + i) + j
                _remote(mine, mine, a_s.at[k], a_r.at[k], (px, py, c)).wait_send()
        for j, (px, py) in enumerate(chips):
            for i in range(n):
                k = 3 * (first + i) + j
                blk = ins[i].at[2 * px + py, c]
                _remote(blk, blk, a_s.at[k], a_r.at[k], (px, py, c)).wait_recv()
                _remote(blk, blk, b_s.at[3 * i + j], b_r.at[3 * i + j], sib).start()

    dma = pltpu.SemaphoreType.DMA
    res = pl.pallas_call(
        body, name=f"gather_pass_{tag}", in_specs=[HBM] * n + [SEM, SEM, ANY],
        out_specs=[SEM, SEM] + [HBM] * n,
        out_shape=[dma((3 * n,)), dma((3 * n,))] + [pltpu.HBM(b.shape, b.dtype) for b in bufs],
        input_output_aliases={i: i + 2 for i in range(n)},
        compiler_params=pltpu.CompilerParams(has_side_effects=EFFECT),
        )(*bufs, s_sem, r_sem, after)
    return res[0], res[1], list(res[2:])


def _gather_land(s_sem, r_sem, bufs, tag):
    n = len(bufs)

    def body(*refs):
        ins = refs[:n]
        b_s, b_r = refs[n], refs[n + 1]
        x, y, c, chips = _place()
        sib = (x, y, 1 - c)
        for j, (px, py) in enumerate(chips):
            for i in range(n):
                sent = ins[i].at[2 * px + py, c]
                got = ins[i].at[2 * px + py, 1 - c]
                _remote(sent, sent, b_s.at[3 * i + j], b_r.at[3 * i + j], sib).wait_send()
                _remote(got, got, b_s.at[3 * i + j], b_r.at[3 * i + j], sib).wait_recv()

    return list(pl.pallas_call(
        body, name=f"gather_land_{tag}", in_specs=[HBM] * n + [SEM, SEM], out_specs=[HBM] * n,
        out_shape=[pltpu.HBM(b.shape, b.dtype) for b in bufs],
        input_output_aliases={i: i for i in range(n)},
        compiler_params=pltpu.CompilerParams(has_side_effects=EFFECT),
        )(*bufs, s_sem, r_sem))


def _rs_pair_add(g, recv, place, tag):
    r, col = g.shape[-2:]
    tr = _tile(r, ROW_TILE)

    def body(place_ref, g_ref, r_ref, wire_ref, own_ref):
        tot = g_ref[...] + r_ref[...]
        wire_ref[...] = tot.astype(wire_ref.dtype)

        @pl.when(pl.program_id(1) == place_ref[1])
        def _():
            own_ref[...] = tot

    grid_spec = pltpu.PrefetchScalarGridSpec(
        num_scalar_prefetch=1, grid=(r // tr, N_CHIP),
        in_specs=[pl.BlockSpec((None, None, tr, col), lambda i, s, p: (s, p[0], i, 0)),
                  pl.BlockSpec((None, tr, col), lambda i, s, p: (s, i, 0))],
        out_specs=[pl.BlockSpec((None, tr, col), lambda i, s, p: (s, i, 0)),
                   pl.BlockSpec((tr, col), lambda i, s, p: (i, 0))])
    return pl.pallas_call(
        body, name=f"rs_pair_add_{tag}", grid_spec=grid_spec,
        out_shape=[_sds((N_CHIP, r, col), WIRE_DTYPE), _sds((r, col), F32)],
        compiler_params=_params())(place, g, recv)


def _pair_plan(srcs, lands):
    x, y, c, _ = _place()
    return [(s.at[:, 1 - c], l, (x, y, 1 - c)) for s, l in zip(srcs, lands)]


def _chip_plan(srcs, lands):
    x, y, c, chips = _place()
    plan = []
    for s, l in zip(srcs, lands):
        if len(s.shape) == 2:
            me = 4 * x + 2 * y + c
            plan += [(s, l.at[me], (x ^ (k >> 2), y ^ ((k >> 1) & 1), c ^ (k & 1))) for k in range(1, N_DEV)]
        else:
            plan += [(s.at[2 * px + py], l.at[j], (px, py, c)) for j, (px, py) in enumerate(chips)]
    return plan


def _exchange_start(srcs, lands, plan, count, tag):
    n = len(srcs)
    both = list(srcs) + list(lands)

    def body(*refs):
        s_sem, r_sem, token = refs[2 * n], refs[2 * n + 1], refs[4 * n + 2]
        for k, (src, dst, dev) in enumerate(plan(refs[:n], refs[n:2 * n])):
            _remote(src, dst, s_sem.at[k], r_sem.at[k], dev).start()
        token[...] = jnp.zeros_like(token)

    dma = pltpu.SemaphoreType.DMA
    res = pl.pallas_call(
        body, name=f"exchange_start_{tag}", in_specs=[HBM] * (2 * n),
        out_specs=[SEM, SEM] + [HBM] * (2 * n) + [pl.BlockSpec(memory_space=pltpu.VMEM)],
        out_shape=[dma((count,)), dma((count,))] + [pltpu.HBM(b.shape, b.dtype) for b in both] + [_sds((8, LANES), F32)],
        input_output_aliases={i: i + 2 for i in range(2 * n)},
        compiler_params=pltpu.CompilerParams(has_side_effects=EFFECT),
        )(*[_in_hbm(b) for b in both])
    return res[0], res[1], list(res[2:2 + n]), list(res[2 + n:2 + 2 * n]), res[-1]


def _exchange_wait(s_sem, r_sem, srcs, lands, plan, after, tag):
    n = len(srcs)

    def body(*refs):
        s_ref, r_ref = refs[2 * n], refs[2 * n + 1]
        for k, (src, dst, dev) in enumerate(plan(refs[:n], refs[n:2 * n])):
            cp = _remote(src, dst, s_ref.at[k], r_ref.at[k], dev)
            cp.wait_send()
            cp.wait_recv()

    both = list(srcs) + list(lands)
    res = pl.pallas_call(
        body, name=f"exchange_wait_{tag}", in_specs=[HBM] * (2 * n) + [SEM, SEM, ANY], out_specs=[HBM] * (2 * n),
        out_shape=[pltpu.HBM(b.shape, b.dtype) for b in both],
        input_output_aliases={i: i for i in range(2 * n)},
        compiler_params=pltpu.CompilerParams(has_side_effects=EFFECT),
        )(*both, s_sem, r_sem, after)
    return list(res[:n]), list(res[n:])


def _rs_chip_add(own, recv, place, l, L, prev, tag):
    r, col = own.shape
    tr = _tile(r, ROW_TILE)

    def body(place_ref, o_ref, r_ref, *rest):
        acc = o_ref[...]
        for j in range(3):
            acc = acc + r_ref[j].astype(F32)
        rest[-1][...] = acc

    in_specs = [pl.BlockSpec((tr, col), lambda i, p: (i, 0)), pl.BlockSpec((3, tr, col), lambda i, p: (0, i, 0))]
    args = [place, own, recv]
    kw = {}
    if prev is not None:
        in_specs.append(ANY)
        args.append(prev)
        kw["input_output_aliases"] = {3: 0}
    grid_spec = pltpu.PrefetchScalarGridSpec(
        num_scalar_prefetch=1, grid=(r // tr,), in_specs=in_specs,
        out_specs=pl.BlockSpec((None, None, tr, col), lambda i, p: (l, p[0], i, 0)))
    return pl.pallas_call(
        body, name=f"rs_chip_add_{tag}", grid_spec=grid_spec, out_shape=_sds((L, 2, r, col), F32),
        compiler_params=_params(), **kw)(*args)


def _rs_pair_share(fulls, tag):
    n = len(fulls)

    def body(*refs):
        outs = refs[n:2 * n]
        s_sem, r_sem = refs[2 * n:]
        x, y, c, _ = _place()
        sib = (x, y, 1 - c)
        started = []
        for i in range(n):
            cp = _remote(outs[i].at[:, c], outs[i].at[:, c], s_sem.at[i], r_sem.at[i], sib)
            cp.start()
            started.append(cp)
        for i, cp in enumerate(started):
            cp.wait_send()
            _remote(outs[i].at[:, 1 - c], outs[i].at[:, 1 - c], s_sem.at[i], r_sem.at[i], sib).wait_recv()

    dma = pltpu.SemaphoreType.DMA
    return pl.pallas_call(
        body, name=f"rs_pair_share_{tag}", in_specs=[ANY] * n, out_specs=[ANY] * n,
        out_shape=[_sds(f.shape, f.dtype) for f in fulls],
        input_output_aliases={i: i for i in range(n)},
        scratch_shapes=[dma((n,)), dma((n,))],
        )(*fulls)


def _ffn_fwd(x, g_pre, g_post, w_in, w_out, tag):
    xn, hgu, act = _norm_mm_in(x, g_pre, w_in, tag, swiglu=True)
    if callable(w_out):
        w_out = w_out(act)
    h, x_out = _mm_out_post(act, w_out.reshape(-1, w_out.shape[-1]), x, g_post, 0.5, tag)
    return x_out, (x, xn, hgu, act, h)


def _ffn_bwd(dx, saved, g_pre, g_post, w_in, w_out, tag, between=None, finish=None):
    x, xn, hgu, act, h = saved
    dh, dg_post, dhgu = _post_bwd_mm(dx, h, g_post, 0.5, w_out.reshape(-1, w_out.shape[-1]), tag, hgu=hgu)
    token = None
    if between is not None:
        token = between(dhgu)
    dw_out = _mm_tn_out(act, dh, tag, after=token)
    dw_in = _mm_tn_in(xn, dhgu, tag, after=token)
    if finish is not None:
        token = finish(dw_in, dw_out)
    if token is not None:
        g_pre = g_pre + token[0, :1]
    dx_in, dg_pre = _mm_nt_pre(dhgu, w_in, dx, x, g_pre, tag)
    return dx_in, dg_pre, dg_post, dw_in, dw_out


def kernel(x, ffn1_pre_g, ffn1_post_g, ffn1_w_in, ffn1_w_out, mix_pre_g, mix_post_g, ffn2_pre_g, ffn2_post_g, ffn2_w_in, ffn2_w_out, conv_w_in, conv_k, conv_w_out, kv_g, kv_w, forget_b, attn_w_qg, attn_w_o, loss_target, m_ffn1_pre_g, m_ffn1_post_g, m_ffn1_w_in, m_ffn1_w_out, m_mix_pre_g, m_mix_post_g, m_ffn2_pre_g, m_ffn2_post_g, m_ffn2_w_in, m_ffn2_w_out, m_conv_w_in, m_conv_k, m_conv_w_out, m_kv_g, m_kv_w, m_forget_b, m_attn_w_qg, m_attn_w_o, v_ffn1_pre_g, v_ffn1_post_g, v_ffn1_w_in, v_ffn1_w_out, v_mix_pre_g, v_mix_post_g, v_ffn2_pre_g, v_ffn2_post_g, v_ffn2_w_in, v_ffn2_w_out, v_conv_w_in, v_conv_k, v_conv_w_out, v_kv_g, v_kv_w, v_forget_b, v_attn_w_qg, v_attn_w_o):
    Bl, S, D = x.shape
    T = Bl * S
    H = forget_b.shape[0]
    assert D == H * HEAD_DIM and D % LANES == 0
    kvc = kv_w.shape[1]
    kvp = -(-kvc // LANES) * LANES
    kv_all = 2 * D + LANES
    dk_cols = conv_k.shape[2]
    chip = 2 * lax.axis_index("x") + lax.axis_index("y")
    core = lax.axis_index("c")

    given = dict(ffn1_w_in=(ffn1_w_in, m_ffn1_w_in, v_ffn1_w_in), ffn1_w_out=(ffn1_w_out, m_ffn1_w_out, v_ffn1_w_out),
                 ffn2_w_in=(ffn2_w_in, m_ffn2_w_in, v_ffn2_w_in), ffn2_w_out=(ffn2_w_out, m_ffn2_w_out, v_ffn2_w_out),
                 conv_w_in=(conv_w_in, m_conv_w_in, v_conv_w_in), conv_w_out=(conv_w_out, m_conv_w_out, v_conv_w_out),
                 kv_w=(kv_w, m_kv_w, v_kv_w), attn_w_qg=(attn_w_qg, m_attn_w_qg, v_attn_w_qg),
                 attn_w_o=(attn_w_o, m_attn_w_o, v_attn_w_o))
    shards = {k: w for k, (w, _, _) in given.items()}
    shards["kv_w"] = jnp.pad(kv_w, ((0, 0), (0, kvp - kvc)))[None]
    groups = [[("ffn1_w_in", 0), ("ffn1_w_out", 0)], [("conv_w_in", 0), ("conv_w_out", 0)],
              [("ffn2_w_in", 0), ("ffn2_w_out", 0)], [("kv_w", 0), ("ffn1_w_in", 1), ("ffn1_w_out", 1)],
              [("attn_w_qg", 0), ("attn_w_o", 0), ("ffn2_w_in", 1), ("ffn2_w_out", 1)]]
    second = groups[3] + groups[4]
    place = jnp.stack([core, chip]).astype(jnp.int32)

    def slot(key, where):
        w = shards[key[0]]
        L, r, col = w.shape
        return _own_slot(w.reshape(L, 2, r // 2, col), key[1], MM_DTYPE, where, f"{key[0]}{key[1]}")

    def whole(g):
        return g.reshape(N_CHIP, -1, g.shape[-1])

    taps_slot = _own_slot(jnp.pad(conv_k[0], ((0, 13), (0, 0))).reshape(1, 2, 8, dk_cols), 0, F32, place, "conv_k")
    fb = jnp.pad(forget_b, (0, LANES - H)).reshape(1, LANES)
    w_in0, w_out0 = groups[0]
    s_0, r_0, fly_0, token = _gather_start([slot(w_in0, place), taps_slot, slot(w_out0, place)], fb, "first")
    later = groups[1] + groups[2] + groups[3] + groups[4]
    s_1, r_1, fly_1, token = _gather_start([slot(key, place) for key in later], token, "rest")
    W = {}

    def land(sems, bufs, lo, after, tag):
        return _gather_land(*_gather_pass(*sems, bufs, lo, after, tag), tag)

    def arrive(g, after):
        lo = sum(len(groups[k]) for k in range(1, g))
        got = land((s_1, r_1), fly_1[lo:lo + len(groups[g])], lo, after, f"g{g}")
        W.update({key: whole(b) for key, b in zip(groups[g], got)})

    w_first, taps = land((s_0, r_0), fly_0[:2], 0, token, "g0")
    k_taps = taps.reshape(N_CHIP, 16, dk_cols).transpose(1, 0, 2).reshape(16, D)[:8]

    x0 = x.reshape(T, D)
    W[w_in0] = whole(w_first)

    def first_w_out(act):
        W[w_out0] = whole(land((s_0, r_0), fly_0[2:], 2, act, "g0_out")[0])
        return W[w_out0]

    x1, s_f1a = _ffn_fwd(x0, ffn1_pre_g[0], ffn1_post_g[0], W[w_in0], first_w_out, "l0f1")
    arrive(1, x1)
    w_o_conv = W["conv_w_out", 0].reshape(D, D)
    xn_c, bch = _norm_mm_in(x1, mix_pre_g[0], W["conv_w_in", 0], "conv")
    z_c = _conv_fwd(bch, k_taps, Bl, S)
    m_c, x2 = _mm_out_post(z_c, w_o_conv, x1, mix_post_g[0], 1.0, "conv_out")
    arrive(2, x2)
    x3, s_f2a = _ffn_fwd(x2, ffn2_pre_g[0], ffn2_post_g[0], W["ffn2_w_in", 0], W["ffn2_w_out", 0], "l0f2")

    arrive(3, x3)
    kv_full = jnp.concatenate([W["kv_w", 0][s, :, :kvc] for s in range(N_CHIP)], axis=1)
    kv_full = jnp.pad(kv_full, ((0, 0), (0, kv_all - kv_full.shape[1])))
    xn_kv = _rms_fwd(x3, kv_g, "kv")
    kvact = _mm_nn(xn_kv, kv_full[:, :2 * D], MM_DTYPE, "kv")
    pf = _mm_nn(xn_kv, kv_full[:, 2 * D:], F32, "forget")
    cum = _forget_fwd(pf, fb, Bl, S)
    bq = min(S, ATT_BLOCK)
    c3 = cum.reshape(Bl, S, LANES)[:, :, :H].transpose(0, 2, 1)
    c_col = jnp.broadcast_to(c3[..., None], (Bl, H, S, LANES))
    c_row = c3.reshape(Bl, H, S // bq, 1, bq)

    x4, s_f1b = _ffn_fwd(x3, ffn1_pre_g[1], ffn1_post_g[1], W["ffn1_w_in", 1], W["ffn1_w_out", 1], "l1f1")
    arrive(4, x4)
    w_o_attn = W["attn_w_o", 0].reshape(D, D)
    xn_a, qg = _norm_mm_in(x4, mix_pre_g[1], W["attn_w_qg", 0], "qg")
    o, lse, z_a = _attn_fwd(qg, kvact, c_col, c_row, Bl, S, D)
    m_a, x5 = _mm_out_post(z_a, w_o_attn, x4, mix_post_g[1], 1.0, "attn_out")
    x6, s_f2b = _ffn_fwd(x5, ffn2_pre_g[1], ffn2_post_g[1], W["ffn2_w_in", 1], W["ffn2_w_out", 1], "l1f2")

    dy, loss_local = _loss_grad(x6, loss_target.reshape(T, D))

    G = {}
    dx5, dg_f2pre_1, dg_f2post_1, G["ffn2_w_in", 1], G["ffn2_w_out", 1] = _ffn_bwd(
        dy, s_f2b, ffn2_pre_g[1], ffn2_post_g[1], W["ffn2_w_in", 1], W["ffn2_w_out", 1], "l1f2")
    dm_a, dg_mixpost_1, dz_a = _post_bwd_mm(dx5, m_a, mix_post_g[1], 1.0, w_o_attn, "attn_out")
    G["attn_w_o", 0] = _mm_tn_out(z_a, dm_a, "attn_out")
    dq, dk, dv, dcr = _attn_bwd(qg, kvact, dz_a, lse, c_col, c_row, Bl, S, D)
    dqg = _gate_bwd(dz_a, qg, o, dq)
    G["attn_w_qg", 0] = _mm_tn_in(xn_a, dqg, "qg")
    dx4, dg_mixpre_1 = _mm_nt_pre(dqg, W["attn_w_qg", 0], dx5, x4, mix_pre_g[1], "qg")
    dx3, dg_f1pre_1, dg_f1post_1, G["ffn1_w_in", 1], G["ffn1_w_out", 1] = _ffn_bwd(
        dx4, s_f1b, ffn1_pre_g[1], ffn1_post_g[1], W["ffn1_w_in", 1], W["ffn1_w_out", 1], "l1f1")

    dcum = jnp.pad(dcr.reshape(Bl, H, S).transpose(0, 2, 1), ((0, 0), (0, 0), (0, LANES - H))).reshape(T, LANES)
    dpf, dfb = _forget_bwd(dcum, pf, fb, Bl, S)
    dp = _pack_dkv(dk, dv, dpf)
    G_kv_full = _mm_tn(xn_kv, dp, "kv")
    G["kv_w", 0] = jnp.stack([jnp.pad(G_kv_full[:, s * kvc:(s + 1) * kvc], ((0, 0), (0, kvp - kvc))) for s in range(N_CHIP)])
    dx3, dg_kv = _mm_nt_pre(dp, kv_full, dx3, x3, kv_g, "kv")

    def halves_of(keys):
        return [G[k].reshape(N_CHIP, 2, G[k].shape[1] // 2, G[k].shape[2]) for k in keys]

    def pair_adds(keys, grads, recvs):
        wires, owns = [], []
        for k, g, r in zip(keys, grads, recvs):
            w, own = _rs_pair_add(g, r, place, f"{k[0]}{k[1]}")
            wires.append(w)
            owns.append(own)
        return wires, owns

    def chip_start(wires, tag, extra=()):
        lands = [lax.empty((3,) + w.shape[1:], w.dtype) for w in wires]
        lands += [jnp.zeros((N_DEV,) + e.shape, e.dtype) for e in extra]
        return _exchange_start(list(wires) + list(extra), lands, _chip_plan, 3 * len(wires) + (N_DEV - 1) * len(extra), tag)

    late = groups[2] + groups[1]
    last = groups[0]
    grads_2 = halves_of(second)
    p_sems, p_semr, grads_2, sib_2, token = _exchange_start(
        grads_2, [lax.empty((N_CHIP,) + g.shape[2:], g.dtype) for g in grads_2], _pair_plan, len(grads_2), "pair_second")

    dx2, dg_f2pre_0, dg_f2post_0, G["ffn2_w_in", 0], G["ffn2_w_out", 0] = _ffn_bwd(
        dx3, s_f2a, ffn2_pre_g[0], ffn2_post_g[0] + token[0, :1], W["ffn2_w_in", 0], W["ffn2_w_out", 0], "l0f2")
    grads_2, sib_2 = _exchange_wait(p_sems, p_semr, grads_2, sib_2, _pair_plan, dx2, "pair_second")
    wires_2, owns_2 = pair_adds(second, grads_2, sib_2)
    c_2 = chip_start(wires_2, "chip_second")
    dm_c, dg_mixpost_0, dz_c = _post_bwd_mm(dx2, m_c, mix_post_g[0] + c_2[4][0, :1], 1.0, w_o_conv, "conv_out")
    G["conv_w_out", 0] = _mm_tn_out(z_c, dm_c, "conv_out")
    db, dcg, dhh, dk_taps = _conv_bwd(bch, dz_c, k_taps, Bl, S)
    dbch = jnp.concatenate([db, dcg, dhh], axis=1)
    G["conv_w_in", 0] = _mm_tn_in(xn_c, dbch, "conv")
    dx1, dg_mixpre_0 = _mm_nt_pre(dbch, W["conv_w_in", 0], dx2, x1, mix_pre_g[0], "conv")
    def pair_start(keys, tag):
        grads = halves_of(keys)
        lands = [lax.empty((N_CHIP,) + g.shape[2:], g.dtype) for g in grads]
        return _exchange_start(grads, lands, _pair_plan, len(grads), tag)

    p_l = pair_start(late, "pair_late")
    late_done = {}

    def late_leg(dhgu):
        grads_l, sib_l = _exchange_wait(*p_l[:4], _pair_plan, dhgu, "pair_late")
        late_done["wires"], late_done["owns"] = pair_adds(late, grads_l, sib_l)
        late_done["chip"] = chip_start(late_done["wires"], "chip_late")
        return late_done["chip"][4]

    def last_pair(dw_in, dw_out):
        G["ffn1_w_in", 0], G["ffn1_w_out", 0] = dw_in, dw_out
        late_done["pair"] = pair_start(last, "pair_last")
        return late_done["pair"][4]

    dx0, dg_f1pre_0, dg_f1post_0, _, _ = _ffn_bwd(
        dx1, s_f1a, ffn1_pre_g[0], ffn1_post_g[0] + p_l[4][0, :1], W["ffn1_w_in", 0], W["ffn1_w_out", 0], "l0f1",
        between=late_leg, finish=last_pair)
    grad_x = dx0.reshape(Bl, S, D)
    owns_l, c_l, p_1 = late_done["owns"], late_done["chip"], late_done["pair"]

    grads_1, sib_1 = _exchange_wait(*p_1[:4], _pair_plan, dx0, "pair_last")
    wires_1, owns_1 = pair_adds(last, grads_1, sib_1)

    def row(v):
        return jnp.pad(v.reshape(-1), (0, D - v.size)).reshape(1, D)

    small_parts = [dg_f1pre_0, dg_f1pre_1, dg_f1post_0, dg_f1post_1, dg_mixpre_0, dg_mixpre_1, dg_mixpost_0, dg_mixpost_1,
                   dg_f2pre_0, dg_f2pre_1, dg_f2post_0, dg_f2post_1, dg_kv, row(dfb[0, :H]), dk_taps[:3],
                   jnp.full((1, D), loss_local)]
    small = jnp.concatenate(small_parts, axis=0)
    small = jnp.pad(small, ((0, SMALL_ROWS - small.shape[0]), (0, 0)))
    c_1 = chip_start(wires_1, "chip_last", extra=[small])
    _, recvs_2 = _exchange_wait(*c_2[:4], _chip_plan, c_1[4], "chip_second")
    _, recvs_l = _exchange_wait(*c_l[:4], _chip_plan, c_1[4], "chip_late")
    partial = {}

    def chip_adds(keys, owns, recvs):
        for (name, l), own, rcv in zip(keys, owns, recvs):
            partial[name] = _rs_chip_add(own, rcv, place, l, shards[name].shape[0], partial.get(name), f"{name}{l}")

    chip_adds(late + second, owns_l + owns_2, recvs_l + recvs_2)
    res = {}

    def adamw(names, reduced, after):
        for k, red in zip(names, reduced):
            w, m, v = given[k]
            g2 = red.reshape(-1, red.shape[-1])
            if k == "kv_w":
                g2 = g2[:, :kvc]
            flat = lambda a: a.reshape(-1, a.shape[-1])
            go, d, mn, vn = _adamw(flat(w), g2, flat(m), flat(v), k, after=after)
            res[k] = tuple(a.reshape(w.shape) for a in (go, d, mn, vn))
        return d

    early = [k for k in partial if (k, 0) not in last]
    done = adamw(early, _rs_pair_share([partial[k] for k in early], "early"), c_1[4])
    _, recvs_1 = _exchange_wait(*c_1[:4], _chip_plan, done, "chip_last")
    chip_adds(last, owns_1, recvs_1[:-1])
    rest = [k for k, _ in last]
    adamw(rest, _rs_pair_share([partial[k] for k in rest], "last"), None)
    gsum = _sum_devices(recvs_1[-1], small, place)
    loss = gsum[17, 0]

    small_names = ["ffn1_pre_g", "ffn1_post_g", "mix_pre_g", "mix_post_g", "ffn2_pre_g", "ffn2_post_g"]
    small_given = dict(ffn1_pre_g=(ffn1_pre_g, m_ffn1_pre_g, v_ffn1_pre_g), ffn1_post_g=(ffn1_post_g, m_ffn1_post_g, v_ffn1_post_g),
                       mix_pre_g=(mix_pre_g, m_mix_pre_g, v_mix_pre_g), mix_post_g=(mix_post_g, m_mix_post_g, v_mix_post_g),
                       ffn2_pre_g=(ffn2_pre_g, m_ffn2_pre_g, v_ffn2_pre_g), ffn2_post_g=(ffn2_post_g, m_ffn2_post_g, v_ffn2_post_g))

    def pack(idx):
        rows_ = [small_given[k][idx] for k in small_names]
        rows_ += [row((kv_g, m_kv_g, v_kv_g)[idx]), row((forget_b, m_forget_b, v_forget_b)[idx])]
        rows_.append(jnp.pad((conv_k, m_conv_k, v_conv_k)[idx][0], ((0, 0), (0, D - dk_cols))))
        a = jnp.concatenate(rows_, axis=0)
        return jnp.pad(a, ((0, SMALL_ROWS - a.shape[0]), (0, 0)))

    g_taps = lax.dynamic_slice_in_dim(gsum[14:17], chip * dk_cols, dk_cols, axis=1)
    g_small = jnp.concatenate([gsum[:14], jnp.pad(g_taps, ((0, 0), (0, D - dk_cols))), gsum[17:]], axis=0)
    g_small, d_s, m_s, v_s = _adamw(pack(0), g_small, pack(1), pack(2), "small")
    for i, k in enumerate(small_names):
        res[k] = tuple(a[2 * i:2 * i + 2] for a in (g_small, d_s, m_s, v_s))
    res["kv_g"] = tuple(a[12] for a in (g_small, d_s, m_s, v_s))
    res["forget_b"] = tuple(a[13, :H] for a in (g_small, d_s, m_s, v_s))
    res["conv_k"] = tuple(a[14:17, :dk_cols][None] for a in (g_small, d_s, m_s, v_s))

    order = ["ffn1_pre_g", "ffn1_post_g", "ffn1_w_in", "ffn1_w_out", "mix_pre_g", "mix_post_g", "ffn2_pre_g", "ffn2_post_g",
             "ffn2_w_in", "ffn2_w_out", "conv_w_in", "conv_k", "conv_w_out", "kv_g", "kv_w", "forget_b", "attn_w_qg", "attn_w_o"]
    out = [loss, grad_x]
    for idx in range(4):
        out += [res[k][idx] for k in order]
    return tuple(out)
```

```python
import math

import jax
import jax.numpy as jnp
from jax import lax
from jax.experimental import pallas as pl
from jax.experimental.pallas import tpu as pltpu

F32 = jnp.float32
MM_DTYPE = jnp.bfloat16
WIRE_DTYPE = jnp.bfloat16

RMS_EPS = 1e-6
ADAM_LR = 0.001
ADAM_B1 = 0.9
ADAM_B2 = 0.999
ADAM_EPS = 1e-08
ADAM_WD = 0.01
ADAM_STEP = 10

HEAD_DIM = 64
LANES = 128
N_CHIP = 4
N_DEV = 8
ROW_TILE = 512
MM_TILE = 512
FUSED_TILE = 512
TN_TILE = 2048
ATT_BLOCK = 512
SMALL_ROWS = 24
V7X_VMEM_BYTES = 64 * 1024 * 1024
VMEM_LIMIT = V7X_VMEM_BYTES - 8 * 1024 * 1024
MESH = pl.DeviceIdType.MESH
ANY = pl.BlockSpec(memory_space=pl.ANY)

NT = (((1,), (1,)), ((), ()))
TN = (((0,), (0,)), ((), ()))


def _tile(n, pref):
    if n <= pref:
        return n
    t = pref - pref % 16
    while n % t:
        t -= 16
    return t


def _params():
    return pltpu.CompilerParams(vmem_limit_bytes=VMEM_LIMIT)


def _sds(shape, dtype):
    return jax.ShapeDtypeStruct(shape, dtype)


def _rows(tm, c):
    return pl.BlockSpec((tm, c), lambda i: (i, 0))


def _whole(shape):
    return pl.BlockSpec(shape, lambda *_: (0,) * len(shape))


def _resident(shape):
    return pl.BlockSpec(shape, lambda *_: (0,) * len(shape), pipeline_mode=pl.Buffered(1))


def _rms_fwd(x, g, tag):
    T, D = x.shape
    tm = _tile(T, ROW_TILE)

    def body(x_ref, g_ref, o_ref):
        xv = x_ref[...]
        r = lax.rsqrt(jnp.mean(xv * xv, axis=-1, keepdims=True) + RMS_EPS)
        o_ref[...] = (xv * r * g_ref[...]).astype(o_ref.dtype)

    return pl.pallas_call(
        body, name=f"rms_fwd_{tag}", grid=(T // tm,),
        in_specs=[_rows(tm, D), _whole((1, D))], out_specs=_rows(tm, D),
        out_shape=_sds((T, D), MM_DTYPE), compiler_params=_params())(x, g.reshape(1, D))


def _accumulate(ref, part, first):
    @pl.when(first)
    def _():
        ref[...] = part

    @pl.when(jnp.logical_not(first))
    def _():
        ref[...] += part


def _loss_grad(y, tgt):
    T, D = y.shape
    tm = _tile(T, ROW_TILE)

    def body(y_ref, t_ref, dy_ref, l_ref):
        e = y_ref[...] - t_ref[...]
        row = jnp.mean(e * e, axis=-1, keepdims=True)
        part = jnp.broadcast_to(jnp.sum(row, axis=0, keepdims=True), (8, LANES))
        _accumulate(l_ref, part, pl.program_id(0) == 0)
        dy_ref[...] = e * (1.0 / D)

    dy, lsum = pl.pallas_call(
        body, name="loss_grad", grid=(T // tm,),
        in_specs=[_rows(tm, D), _rows(tm, D)], out_specs=[_rows(tm, D), _whole((8, LANES))],
        out_shape=[_sds((T, D), F32), _sds((8, LANES), F32)], compiler_params=_params())(y, tgt)
    return dy, 0.5 * lsum[0, 0]


def _shift_down(u, d, rows):
    return jnp.where(rows >= d, pltpu.roll(u, d, 0), 0.0)


def _shift_up(u, d, rows, S):
    return jnp.where(rows < S - d, pltpu.roll(u, S - d, 0), 0.0)


def _conv_fwd(bch, k8, Bl, S):
    T, D3 = bch.shape
    D = D3 // 3
    dc = min(D, 2 * LANES)
    nd = D // dc

    def body(b_ref, c_ref, h_ref, k_ref, z_ref):
        rows = lax.broadcasted_iota(jnp.int32, (S, 1), 0)
        u = c_ref[...].astype(F32) * h_ref[...].astype(F32)
        y = k_ref[2:3, :] * u + k_ref[1:2, :] * _shift_down(u, 1, rows) + k_ref[0:1, :] * _shift_down(u, 2, rows)
        z_ref[...] = (b_ref[...].astype(F32) * y).astype(z_ref.dtype)

    return pl.pallas_call(
        body, name="conv_fwd", grid=(Bl, nd),
        in_specs=[pl.BlockSpec((S, dc), lambda b, j: (b, j)),
                  pl.BlockSpec((S, dc), lambda b, j: (b, nd + j)),
                  pl.BlockSpec((S, dc), lambda b, j: (b, 2 * nd + j)),
                  pl.BlockSpec((8, dc), lambda b, j: (0, j))],
        out_specs=pl.BlockSpec((S, dc), lambda b, j: (b, j)),
        out_shape=_sds((T, D), MM_DTYPE), compiler_params=_params())(bch, bch, bch, k8)


def _conv_bwd(bch, dz, k8, Bl, S):
    T, D3 = bch.shape
    D = D3 // 3
    dc = min(D, 2 * LANES)
    nd = D // dc

    def body(b_ref, c_ref, h_ref, dz_ref, k_ref, db_ref, dc_ref, dh_ref, dk_ref):
        rows = lax.broadcasted_iota(jnp.int32, (S, 1), 0)
        bv = b_ref[...].astype(F32)
        cv = c_ref[...].astype(F32)
        hv = h_ref[...].astype(F32)
        dzv = dz_ref[...].astype(F32)
        u = cv * hv
        u1 = _shift_down(u, 1, rows)
        u2 = _shift_down(u, 2, rows)
        y = k_ref[2:3, :] * u + k_ref[1:2, :] * u1 + k_ref[0:1, :] * u2
        db_ref[...] = (dzv * y).astype(db_ref.dtype)
        dy = dzv * bv
        du = k_ref[2:3, :] * dy + k_ref[1:2, :] * _shift_up(dy, 1, rows, S) + k_ref[0:1, :] * _shift_up(dy, 2, rows, S)
        dc_ref[...] = (du * hv).astype(dc_ref.dtype)
        dh_ref[...] = (du * cv).astype(dh_ref.dtype)

        @pl.when(pl.program_id(1) == 0)
        def _():
            dk_ref[...] = jnp.zeros_like(dk_ref)

        dk_ref[0:1, :] += jnp.sum(dy * u2, axis=0, keepdims=True)
        dk_ref[1:2, :] += jnp.sum(dy * u1, axis=0, keepdims=True)
        dk_ref[2:3, :] += jnp.sum(dy * u, axis=0, keepdims=True)

    seq = lambda off: pl.BlockSpec((S, dc), lambda j, b: (b, off + j))
    return pl.pallas_call(
        body, name="conv_bwd", grid=(nd, Bl),
        in_specs=[seq(0), seq(nd), seq(2 * nd), seq(0), pl.BlockSpec((8, dc), lambda j, b: (0, j))],
        out_specs=[seq(0), seq(0), seq(0), pl.BlockSpec((8, dc), lambda j, b: (0, j))],
        out_shape=[_sds((T, D), MM_DTYPE)] * 3 + [_sds((8, D), F32)],
        compiler_params=_params())(bch, bch, bch, dz, k8)


def _forget_fwd(pf, fb, Bl, S):
    T = pf.shape[0]

    def body(p_ref, fb_ref, c_ref):
        rows = lax.broadcasted_iota(jnp.int32, (S, 1), 0)
        z = p_ref[...] + fb_ref[...]
        acc = jnp.minimum(z, 0.0) - jnp.log1p(jnp.exp(-jnp.abs(z)))
        d = 1
        while d < S:
            acc = acc + _shift_down(acc, d, rows)
            d *= 2
        c_ref[...] = acc

    return pl.pallas_call(
        body, name="forget_fwd", grid=(Bl,),
        in_specs=[_rows(S, LANES), _whole((1, LANES))], out_specs=_rows(S, LANES),
        out_shape=_sds((T, LANES), F32), compiler_params=_params())(pf, fb)


def _forget_bwd(dc, pf, fb, Bl, S):
    T = pf.shape[0]

    def body(dc_ref, p_ref, fb_ref, df_ref, dfb_ref):
        rows = lax.broadcasted_iota(jnp.int32, (S, 1), 0)
        acc = dc_ref[...]
        d = 1
        while d < S:
            acc = acc + _shift_up(acc, d, rows, S)
            d *= 2
        df = acc * jax.nn.sigmoid(-(p_ref[...] + fb_ref[...]))
        df_ref[...] = df.astype(df_ref.dtype)
        _accumulate(dfb_ref, jnp.sum(df, axis=0, keepdims=True), pl.program_id(0) == 0)

    return pl.pallas_call(
        body, name="forget_bwd", grid=(Bl,),
        in_specs=[_rows(S, LANES), _rows(S, LANES), _whole((1, LANES))],
        out_specs=[_rows(S, LANES), _whole((1, LANES))],
        out_shape=[_sds((T, LANES), MM_DTYPE), _sds((1, LANES), F32)],
        compiler_params=_params())(dc, pf, fb)


def _head_mask(h):
    lane = lax.broadcasted_iota(jnp.int32, (1, LANES), 1)
    return (lane >= h * HEAD_DIM) & (lane < (h + 1) * HEAD_DIM)


def _attn_fwd(qg, kv, c_col, c_row, Bl, S, D):
    T = Bl * S
    H = D // HEAD_DIM
    HP = D // LANES
    bq = min(S, ATT_BLOCK)
    nq = S // bq
    scale = 1.0 / math.sqrt(HEAD_DIM)

    def body(q_ref, g_ref, k_ref, v_ref, cc_ref, cr_ref, o_ref, lse_ref, z_ref):
        i = pl.program_id(2)
        q2 = q_ref[...]
        qh = [q2 * (_head_mask(h).astype(F32) * scale).astype(q2.dtype) for h in range(2)]
        cc = [cc_ref[h][:, :1] for h in range(2)]
        diag = lax.broadcasted_iota(jnp.int32, (1, bq), 1) <= lax.broadcasted_iota(jnp.int32, (bq, 1), 0)

        def block(j, carry, on_diagonal):
            off = pl.multiple_of(j * bq, bq)
            kj = k_ref[pl.ds(off, bq), :]
            vj = v_ref[pl.ds(off, bq), :]
            new = []
            for h in range(2):
                m, l, acc = carry[h]
                s = lax.dot_general(qh[h], kj, NT, preferred_element_type=F32) + cc[h] - cr_ref[h, j]
                if on_diagonal:
                    s = jnp.where(diag, s, -jnp.inf)
                m_new = jnp.maximum(m, jnp.max(s, axis=1, keepdims=True))
                p = jnp.exp(s - m_new)
                a = jnp.exp(m - m_new)
                l = a * l + jnp.sum(p, axis=1, keepdims=True)
                acc = a * acc + jnp.dot(p.astype(MM_DTYPE), vj, preferred_element_type=F32)
                new.append((m_new, l, acc))
            return tuple(new)

        one = (jnp.full((bq, 1), -jnp.inf, F32), jnp.zeros((bq, 1), F32), jnp.zeros((bq, LANES), F32))
        carry = lax.fori_loop(0, i, lambda j, c: block(j, c, False), (one, one))
        carry = block(i, carry, True)
        outs = []
        for h in range(2):
            m, l, acc = carry[h]
            outs.append(acc / l)
            lse_ref[h] = jnp.broadcast_to(m + jnp.log(l), (bq, LANES))
        o2 = jnp.where(_head_mask(0), outs[0], outs[1])
        o_ref[...] = o2
        z_ref[...] = (jax.nn.sigmoid(g_ref[...].astype(F32)) * o2).astype(z_ref.dtype)

    return pl.pallas_call(
        body, name="attn_fwd", grid=(Bl, HP, nq),
        in_specs=[pl.BlockSpec((bq, LANES), lambda b, hp, i: (b * nq + i, hp)),
                  pl.BlockSpec((bq, LANES), lambda b, hp, i: (b * nq + i, HP + hp)),
                  pl.BlockSpec((S, LANES), lambda b, hp, i: (b, hp)),
                  pl.BlockSpec((S, LANES), lambda b, hp, i: (b, HP + hp)),
                  pl.BlockSpec((None, 2, bq, LANES), lambda b, hp, i: (b, hp, i, 0)),
                  pl.BlockSpec((None, 2, nq, 1, bq), lambda b, hp, i: (b, hp, 0, 0, 0))],
        out_specs=[pl.BlockSpec((bq, LANES), lambda b, hp, i: (b * nq + i, hp)),
                   pl.BlockSpec((None, 2, bq, LANES), lambda b, hp, i: (b, hp, i, 0)),
                   pl.BlockSpec((bq, LANES), lambda b, hp, i: (b * nq + i, hp))],
        out_shape=[_sds((T, D), F32), _sds((Bl, H, S, LANES), F32), _sds((T, D), MM_DTYPE)],
        compiler_params=_params())(qg, qg, kv, kv, c_col, c_row)


def _attn_bwd(qg, kv, dz, lse, c_col, c_row, Bl, S, D):
    T = Bl * S
    H = D // HEAD_DIM
    HP = D // LANES
    bq = min(S, ATT_BLOCK)
    nq = S // bq
    scale = 1.0 / math.sqrt(HEAD_DIM)

    def body(q_ref, g_ref, k_ref, v_ref, dz_ref, lse_ref, cc_ref, cr_ref, dq_ref, dk_ref, dv_ref, dcr_ref, p_sc, dp_sc):
        i = pl.program_id(2)

        @pl.when(i == 0)
        def _():
            dk_ref[...] = jnp.zeros_like(dk_ref)
            dv_ref[...] = jnp.zeros_like(dv_ref)
            dcr_ref[...] = jnp.zeros_like(dcr_ref)

        q2 = q_ref[...]
        do2 = (dz_ref[...].astype(F32) * jax.nn.sigmoid(g_ref[...].astype(F32))).astype(MM_DTYPE)
        masks = [_head_mask(h).astype(F32) for h in range(2)]
        qh = [q2 * (masks[h] * scale).astype(q2.dtype) for h in range(2)]
        doh = [do2 * masks[h].astype(do2.dtype) for h in range(2)]
        cc = [cc_ref[h][:, :1] for h in range(2)]
        lse = [lse_ref[h][:, :1] for h in range(2)]
        diag = lax.broadcasted_iota(jnp.int32, (1, bq), 1) <= lax.broadcasted_iota(jnp.int32, (bq, 1), 0)

        def sweep1(j, delta, on_diagonal):
            off = pl.multiple_of(j * bq, bq)
            kj = k_ref[pl.ds(off, bq), :]
            vj = v_ref[pl.ds(off, bq), :]
            new = []
            dv = None
            for h in range(2):
                s = lax.dot_general(qh[h], kj, NT, preferred_element_type=F32) + cc[h] - cr_ref[h, j]
                if on_diagonal:
                    s = jnp.where(diag, s, -jnp.inf)
                p = jnp.exp(s - lse[h])
                dp = lax.dot_general(doh[h], vj, NT, preferred_element_type=F32)
                p_sc[h, j] = p
                dp_sc[h, j] = dp
                part = lax.dot_general(p.astype(MM_DTYPE), doh[h], TN, preferred_element_type=F32)
                dv = part if dv is None else dv + part
                new.append(delta[h] + jnp.sum(p * dp, axis=1, keepdims=True))
            dv_ref[pl.ds(off, bq), :] += dv
            return tuple(new)

        zero = jnp.zeros((bq, 1), F32)
        delta = lax.fori_loop(0, i, lambda j, d: sweep1(j, d, False), (zero, zero))
        delta = sweep1(i, delta, True)

        def sweep2(j, dq):
            off = pl.multiple_of(j * bq, bq)
            kj = k_ref[pl.ds(off, bq), :]
            dk = None
            for h in range(2):
                ds = p_sc[h, j] * (dp_sc[h, j] - delta[h])
                dcr_ref[h, j] -= jnp.sum(ds, axis=0, keepdims=True)
                dsb = ds.astype(MM_DTYPE)
                dq = dq + jnp.dot(dsb, kj * (masks[h] * scale).astype(kj.dtype), preferred_element_type=F32)
                part = lax.dot_general(dsb, qh[h], TN, preferred_element_type=F32)
                dk = part if dk is None else dk + part
            dk_ref[pl.ds(off, bq), :] += dk
            return dq

        dq_ref[...] = lax.fori_loop(0, i + 1, sweep2, jnp.zeros((bq, LANES), F32))

    blk = lambda col: pl.BlockSpec((bq, LANES), lambda b, hp, i: (b * nq + i, col(hp)))
    seq = lambda col: pl.BlockSpec((S, LANES), lambda b, hp, i: (b, col(hp)))
    per_head = pl.BlockSpec((None, 2, bq, LANES), lambda b, hp, i: (b, hp, i, 0))
    rows = pl.BlockSpec((None, 2, nq, 1, bq), lambda b, hp, i: (b, hp, 0, 0, 0))
    return pl.pallas_call(
        body, name="attn_bwd", grid=(Bl, HP, nq),
        in_specs=[blk(lambda hp: hp), blk(lambda hp: HP + hp), seq(lambda hp: hp), seq(lambda hp: HP + hp),
                  blk(lambda hp: hp), per_head, per_head, rows],
        out_specs=[blk(lambda hp: hp), seq(lambda hp: hp), seq(lambda hp: hp), rows],
        out_shape=[_sds((T, D), F32), _sds((T, D), F32), _sds((T, D), F32), _sds((Bl, H, nq, 1, bq), F32)],
        scratch_shapes=[pltpu.VMEM((2, nq, bq, bq), F32), pltpu.VMEM((2, nq, bq, bq), F32)],
        compiler_params=_params())(qg, qg, kv, kv, dz, lse, c_col, c_row)


def _pack_dkv(dk, dv, dpf):
    T, D = dk.shape
    tm = _tile(T, ROW_TILE)

    def body(dk_ref, dv_ref, df_ref, o_ref):
        o_ref[:, :D] = dk_ref[...].astype(o_ref.dtype)
        o_ref[:, D:2 * D] = dv_ref[...].astype(o_ref.dtype)
        o_ref[:, 2 * D:] = df_ref[...]

    return pl.pallas_call(
        body, name="pack_dkv", grid=(T // tm,),
        in_specs=[_rows(tm, D), _rows(tm, D), _rows(tm, LANES)], out_specs=_rows(tm, 2 * D + LANES),
        out_shape=_sds((T, 2 * D + LANES), MM_DTYPE), compiler_params=_params())(dk, dv, dpf)


def _gate_bwd(dz, qg, o, dq):
    T, D = dz.shape
    tm = _tile(T, ROW_TILE)

    def body(dz_ref, g_ref, o_ref, dq_ref, out_ref):
        g = g_ref[...].astype(F32)
        sg = jax.nn.sigmoid(g)
        out_ref[:, :D] = dq_ref[...].astype(out_ref.dtype)
        out_ref[:, D:] = (dz_ref[...].astype(F32) * o_ref[...] * sg * (1.0 - sg)).astype(out_ref.dtype)

    return pl.pallas_call(
        body, name="gate_bwd", grid=(T // tm,),
        in_specs=[_rows(tm, D), pl.BlockSpec((tm, D), lambda i: (i, 1)), _rows(tm, D), _rows(tm, D)],
        out_specs=_rows(tm, 2 * D), out_shape=_sds((T, 2 * D), MM_DTYPE),
        compiler_params=_params())(dz, qg, o, dq)


def _mm_nn(a, b, out_dtype, tag):
    T, K = a.shape
    N = b.shape[1]
    tm = _tile(T, MM_TILE)

    def body(a_ref, b_ref, o_ref):
        o_ref[...] = jnp.dot(a_ref[...], b_ref[...], preferred_element_type=F32).astype(o_ref.dtype)

    return pl.pallas_call(
        body, name=f"mm_nn_{tag}", grid=(T // tm,),
        in_specs=[_rows(tm, K), _whole((K, N))], out_specs=_rows(tm, N),
        out_shape=_sds((T, N), out_dtype), compiler_params=_params())(a, b)


def _mm_tn_in(a, dy, tag, after=None):
    T, K = a.shape
    n = dy.shape[1] // N_CHIP
    tt = _tile(T, TN_TILE)
    extra = [] if after is None else [after]

    def body(a_ref, d_ref, *rest):
        part = lax.dot_general(a_ref[...], d_ref[...], TN, preferred_element_type=F32)
        _accumulate(rest[-1], part, pl.program_id(1) == 0)

    return pl.pallas_call(
        body, name=f"mm_tn_in_{tag}", grid=(N_CHIP, T // tt),
        in_specs=[pl.BlockSpec((tt, K), lambda s, t: (t, 0)), pl.BlockSpec((tt, n), lambda s, t: (t, s))]
        + [ANY] * len(extra),
        out_specs=pl.BlockSpec((None, K, n), lambda s, t: (s, 0, 0)), out_shape=_sds((N_CHIP, K, n), F32),
        compiler_params=_params())(a, dy, *extra)


def _mm_tn_out(act, dh, tag, after=None):
    T, R4 = act.shape
    D = dh.shape[1]
    r = R4 // N_CHIP
    g = 1 if r % LANES == 0 else 2
    tt = _tile(T, TN_TILE)
    extra = [] if after is None else [after]

    def body(a_ref, d_ref, *rest):
        o_ref = rest[-1]
        part = lax.dot_general(a_ref[...], d_ref[...], TN, preferred_element_type=F32)
        first = pl.program_id(1) == 0
        for q in range(g):
            _accumulate(o_ref.at[q], part[q * r:(q + 1) * r], first)

    return pl.pallas_call(
        body, name=f"mm_tn_out_{tag}", grid=(N_CHIP // g, T // tt),
        in_specs=[pl.BlockSpec((tt, g * r), lambda s, t: (t, s)), pl.BlockSpec((tt, D), lambda s, t: (t, 0))]
        + [ANY] * len(extra),
        out_specs=pl.BlockSpec((g, r, D), lambda s, t: (s, 0, 0)), out_shape=_sds((N_CHIP, r, D), F32),
        compiler_params=_params())(act, dh, *extra)


def _mm_tn(a, b, tag):
    T, K = a.shape
    N = b.shape[1]
    tt = _tile(T, TN_TILE)

    def body(a_ref, b_ref, o_ref):
        part = lax.dot_general(a_ref[...], b_ref[...], TN, preferred_element_type=F32)
        _accumulate(o_ref, part, pl.program_id(0) == 0)

    return pl.pallas_call(
        body, name=f"mm_tn_{tag}", grid=(T // tt,),
        in_specs=[_rows(tt, K), _rows(tt, N)], out_specs=_whole((K, N)),
        out_shape=_sds((K, N), F32), compiler_params=_params())(a, b)


def _norm_mm_in(x, g, wg, tag, swiglu=False):
    T, D = x.shape
    n = wg.shape[-1]
    tm = _tile(T, FUSED_TILE)
    half = N_CHIP // 2

    order = [s for q in range(half) for s in (q, half + q)] if swiglu else list(range(N_CHIP))

    def body(x_ref, g_ref, w_hbm, xn_ref, y_ref, *rest):
        w_ref, sem = rest[-2:]
        first = pl.program_id(0) == 0
        fetch = {s: pltpu.make_async_copy(w_hbm.at[s], w_ref.at[s], sem.at[s]) for s in order}

        @pl.when(first)
        def _():
            for s in order:
                fetch[s].start()

        xv = x_ref[...]
        r = lax.rsqrt(jnp.mean(xv * xv, axis=-1, keepdims=True) + RMS_EPS)
        xn = (xv * r * g_ref[...]).astype(xn_ref.dtype)
        xn_ref[...] = xn

        def product(s):
            @pl.when(first)
            def _():
                fetch[s].wait()

            p = jnp.dot(xn, w_ref[s], preferred_element_type=F32)
            y_ref[:, s * n:(s + 1) * n] = p.astype(y_ref.dtype)
            return p

        if swiglu:
            for q in range(half):
                gate, up = product(q), product(half + q)
                rest[0][:, q * n:(q + 1) * n] = (gate * jax.nn.sigmoid(gate) * up).astype(rest[0].dtype)
        else:
            for s in range(N_CHIP):
                product(s)

    out_specs = [_rows(tm, D), _rows(tm, N_CHIP * n)]
    out_shape = [_sds((T, D), MM_DTYPE), _sds((T, N_CHIP * n), MM_DTYPE)]
    if swiglu:
        out_specs.append(_rows(tm, half * n))
        out_shape.append(_sds((T, half * n), MM_DTYPE))
    return pl.pallas_call(
        body, name=f"norm_mm_in_{tag}", grid=(T // tm,),
        in_specs=[_rows(tm, D), _whole((1, D)), ANY], out_specs=out_specs, out_shape=out_shape,
        scratch_shapes=[pltpu.VMEM((N_CHIP, D, n), wg.dtype), pltpu.SemaphoreType.DMA((N_CHIP,))],
        compiler_params=_params())(x, g.reshape(1, D), wg)


def _mm_out_post(a, b, x, g, alpha, tag):
    T, K = a.shape
    D = b.shape[1]
    tm = _tile(T, FUSED_TILE)

    def body(a_ref, b_ref, x_ref, g_ref, h_ref, o_ref):
        hv = jnp.dot(a_ref[...], b_ref[...], preferred_element_type=F32)
        h_ref[...] = hv
        r = lax.rsqrt(jnp.mean(hv * hv, axis=-1, keepdims=True) + RMS_EPS)
        o_ref[...] = x_ref[...] + alpha * (hv * r * g_ref[...])

    return pl.pallas_call(
        body, name=f"mm_out_post_{tag}", grid=(T // tm,),
        in_specs=[_rows(tm, K), _resident((K, D)), _rows(tm, D), _whole((1, D))],
        out_specs=[_rows(tm, D), _rows(tm, D)], out_shape=[_sds((T, D), F32)] * 2,
        compiler_params=_params())(a, b, x, g.reshape(1, D))


def _post_bwd_mm(dx, h, g, alpha, b, tag, hgu=None):
    T, D = dx.shape
    K = b.shape[0]
    tm = _tile(T, FUSED_TILE)

    def body(dx_ref, h_ref, g_ref, b_ref, *rest):
        dh_ref, dg_ref, out_ref = rest[-3:]
        hv = h_ref[...]
        r = lax.rsqrt(jnp.mean(hv * hv, axis=-1, keepdims=True) + RMS_EPS)
        hh = hv * r
        dyn = alpha * dx_ref[...]
        _accumulate(dg_ref, jnp.sum(dyn * hh, axis=0, keepdims=True), pl.program_id(0) == 0)
        dhh = dyn * g_ref[...]
        dh = (r * (dhh - hh * jnp.mean(dhh * hh, axis=-1, keepdims=True))).astype(dh_ref.dtype)
        dh_ref[...] = dh
        da = lax.dot_general(dh, b_ref[...], NT, preferred_element_type=F32)
        if hgu is None:
            out_ref[...] = da.astype(out_ref.dtype)
        else:
            gate = rest[0][:, :K]
            up = rest[0][:, K:]
            dab = da.astype(gate.dtype)
            sg = jax.nn.sigmoid(gate)
            out_ref[:, :K] = (dab * up * sg * (1.0 + gate * (1.0 - sg))).astype(out_ref.dtype)
            out_ref[:, K:] = (dab * gate * sg).astype(out_ref.dtype)

    in_specs = [_rows(tm, D), _rows(tm, D), _whole((1, D)), _resident((K, D))]
    args = [dx, h, g.reshape(1, D), b]
    wide = K
    if hgu is not None:
        wide = 2 * K
        in_specs.append(_rows(tm, wide))
        args.append(hgu)
    return pl.pallas_call(
        body, name=f"post_bwd_mm_{tag}", grid=(T // tm,), in_specs=in_specs,
        out_specs=[_rows(tm, D), _whole((1, D)), _rows(tm, wide)],
        out_shape=[_sds((T, D), MM_DTYPE), _sds((1, D), F32), _sds((T, wide), MM_DTYPE)],
        compiler_params=_params())(*args)


def _mm_nt_pre(dy, w, dres, x, g, tag):
    T, C = dy.shape
    D = x.shape[1]
    tm = _tile(T, FUSED_TILE)
    n = w.shape[-1]

    sharded = w.ndim == 3

    def body(dy_ref, w_in, dres_ref, x_ref, g_ref, dx_ref, dg_ref, *scratch):
        if not sharded:
            dn = lax.dot_general(dy_ref[...], w_in[...], NT, preferred_element_type=F32)
        else:
            w_ref, sem = scratch
            first = pl.program_id(0) == 0
            fetch = [pltpu.make_async_copy(w_in.at[s], w_ref.at[s], sem.at[s]) for s in range(N_CHIP)]

            @pl.when(first)
            def _():
                for cp in fetch:
                    cp.start()

            dn = None
            for s in range(N_CHIP):
                @pl.when(first)
                def _(s=s):
                    fetch[s].wait()

                part = lax.dot_general(dy_ref[:, s * n:(s + 1) * n], w_ref[s], NT, preferred_element_type=F32)
                dn = part if dn is None else dn + part
        xv = x_ref[...]
        r = lax.rsqrt(jnp.mean(xv * xv, axis=-1, keepdims=True) + RMS_EPS)
        xh = xv * r
        _accumulate(dg_ref, jnp.sum(dn * xh, axis=0, keepdims=True), pl.program_id(0) == 0)
        dxh = dn * g_ref[...]
        dx_ref[...] = dres_ref[...] + r * (dxh - xh * jnp.mean(dxh * xh, axis=-1, keepdims=True))

    return pl.pallas_call(
        body, name=f"mm_nt_pre_{tag}", grid=(T // tm,),
        in_specs=[_rows(tm, C), ANY if sharded else _resident(w.shape), _rows(tm, D), _rows(tm, D), _whole((1, D))],
        out_specs=[_rows(tm, D), _whole((1, D))], out_shape=[_sds((T, D), F32), _sds((1, D), F32)],
        scratch_shapes=[pltpu.VMEM(w.shape, w.dtype), pltpu.SemaphoreType.DMA((N_CHIP,))] if sharded else [],
        compiler_params=_params())(dy, w, dres, x, g.reshape(1, D))


def _adamw(w, g, m, v, tag, after=None):
    R, C = w.shape
    tr = _tile(R, ROW_TILE)
    extra = [] if after is None else [after]

    def body(w_ref, g_ref, m_ref, v_ref, *rest):
        go_ref, d_ref, mo_ref, vo_ref = rest[-4:]
        gv = g_ref[...]
        go_ref[...] = gv
        mn = ADAM_B1 * m_ref[...] + (1.0 - ADAM_B1) * gv
        vn = ADAM_B2 * v_ref[...] + (1.0 - ADAM_B2) * (gv * gv)
        m_hat = mn / (1.0 - ADAM_B1 ** ADAM_STEP)
        v_hat = vn / (1.0 - ADAM_B2 ** ADAM_STEP)
        d_ref[...] = -ADAM_LR * (m_hat / (jnp.sqrt(v_hat) + ADAM_EPS) + ADAM_WD * w_ref[...])
        mo_ref[...] = mn
        vo_ref[...] = vn

    return pl.pallas_call(
        body, name=f"adamw_{tag}", grid=(R // tr,),
        in_specs=[_rows(tr, C)] * 4 + [ANY] * len(extra), out_specs=[_rows(tr, C)] * 4,
        out_shape=[_sds((R, C), F32)] * 4, compiler_params=_params())(w, g, m, v, *extra)


def _sum_devices(gall, own, place):
    _, R, C = gall.shape

    def body(place_ref, g_ref, s_ref, o_ref):
        me = 2 * place_ref[1] + place_ref[0]
        acc = None
        for d in range(N_DEV):
            term = jnp.where(me == d, s_ref[...], g_ref[d])
            acc = term if acc is None else acc + term
        o_ref[...] = acc

    grid_spec = pltpu.PrefetchScalarGridSpec(
        num_scalar_prefetch=1, grid=(1,),
        in_specs=[pl.BlockSpec((N_DEV, R, C), lambda i, p: (0, 0, 0)), pl.BlockSpec((R, C), lambda i, p: (0, 0))],
        out_specs=pl.BlockSpec((R, C), lambda i, p: (0, 0)))
    return pl.pallas_call(
        body, name="sum_devices", grid_spec=grid_spec, out_shape=_sds((R, C), F32),
        compiler_params=_params())(place, gall, own)


HBM = pl.BlockSpec(memory_space=pltpu.HBM)
SEM = pl.BlockSpec(memory_space=pltpu.SEMAPHORE)
EFFECT = pltpu.SideEffectType.DATAFLOW_SIDE_EFFECTING


def _place():
    x, y, c = lax.axis_index("x"), lax.axis_index("y"), lax.axis_index("c")
    chips = ((1 - x, y), (x, 1 - y), (1 - x, 1 - y))
    return x, y, c, chips


def _remote(src, dst, send_sem, recv_sem, dev):
    return pltpu.make_async_remote_copy(src_ref=src, dst_ref=dst, send_sem=send_sem, recv_sem=recv_sem,
                                        device_id=dev, device_id_type=MESH)


def _in_hbm(a):
    return pltpu.with_memory_space_constraint(a, pltpu.HBM)


def _own_slot(w4, l, dtype, place, tag):
    _, _, r, col = w4.shape
    tr = _tile(r, 2 * ROW_TILE)

    def body(place_ref, x_ref, o_ref):
        o_ref[...] = x_ref[...].astype(o_ref.dtype)

    grid_spec = pltpu.PrefetchScalarGridSpec(
        num_scalar_prefetch=1, grid=(2, r // tr),
        in_specs=[pl.BlockSpec((None, None, tr, col), lambda h, i, p: (l, h, i, 0))],
        out_specs=pl.BlockSpec((None, None, tr, col), lambda h, i, p: (p[1], h, i, 0)))
    return pl.pallas_call(
        body, name=f"own_slot_{tag}", grid_spec=grid_spec, out_shape=_sds((N_CHIP, 2, r, col), dtype),
        compiler_params=_params())(place, w4)


def _gather_start(bufs, after, tag):
    n = len(bufs)

    def body(*refs):
        ins = refs[:n]
        s_sem, r_sem, token = refs[n + 1], refs[n + 2], refs[2 * n + 3]
        x, y, c, chips = _place()
        me = 2 * x + y
        for i in range(n):
            mine = ins[i].at[me, c]
            for j, (px, py) in enumerate(chips):
                _remote(mine, mine, s_sem.at[3 * i + j], r_sem.at[3 * i + j], (px, py, c)).start()
        token[...] = jnp.zeros_like(token)

    dma = pltpu.SemaphoreType.DMA
    res = pl.pallas_call(
        body, name=f"gather_start_{tag}", in_specs=[HBM] * n + [ANY],
        out_specs=[SEM, SEM] + [HBM] * n + [pl.BlockSpec(memory_space=pltpu.VMEM)],
        out_shape=[dma((3 * n,)), dma((3 * n,))] + [pltpu.HBM(b.shape, b.dtype) for b in bufs] + [_sds((8, LANES), F32)],
        input_output_aliases={i: i + 2 for i in range(n)},
        compiler_params=pltpu.CompilerParams(has_side_effects=EFFECT),
        )(*[_in_hbm(b) for b in bufs], after)
    return res[0], res[1], list(res[2:2 + n]), res[-1]


def _gather_pass(s_sem, r_sem, bufs, first, after, tag):
    n = len(bufs)

    def body(*refs):
        ins = refs[:n]
        a_s, a_r, b_s, b_r = refs[n], refs[n + 1], refs[n + 3], refs[n + 4]
        x, y, c, chips = _place()
        me = 2 * x + y
        sib = (x, y, 1 - c)
        for i in range(n):
            mine = ins[i].at[me, c]
            for j, (px, py) in enumerate(chips):
                k = 3 * (first + i) + j
                _remote(mine, mine, a_s.at[k], a_r.at[k], (px, py, c)).wait_send()
        for j, (px, py) in enumerate(chips):
            for i in range(n):
                k = 3 * (first + i) + j
                blk = ins[i].at[2 * px + py, c]
                _remote(blk, blk, a_s.at[k], a_r.at[k], (px, py, c)).wait_recv()
                _remote(blk, blk, b_s.at[3 * i + j], b_r.at[3 * i + j], sib).start()

    dma = pltpu.SemaphoreType.DMA
    res = pl.pallas_call(
        body, name=f"gather_pass_{tag}", in_specs=[HBM] * n + [SEM, SEM, ANY],
        out_specs=[SEM, SEM] + [HBM] * n,
        out_shape=[dma((3 * n,)), dma((3 * n,))] + [pltpu.HBM(b.shape, b.dtype) for b in bufs],
        input_output_aliases={i: i + 2 for i in range(n)},
        compiler_params=pltpu.CompilerParams(has_side_effects=EFFECT),
        )(*bufs, s_sem, r_sem, after)
    return res[0], res[1], list(res[2:])


def _gather_land(s_sem, r_sem, bufs, tag):
    n = len(bufs)

    def body(*refs):
        ins = refs[:n]
        b_s, b_r = refs[n], refs[n + 1]
        x, y, c, chips = _place()
        sib = (x, y, 1 - c)
        for j, (px, py) in enumerate(chips):
            for i in range(n):
                sent = ins[i].at[2 * px + py, c]
                got = ins[i].at[2 * px + py, 1 - c]
                _remote(sent, sent, b_s.at[3 * i + j], b_r.at[3 * i + j], sib).wait_send()
                _remote(got, got, b_s.at[3 * i + j], b_r.at[3 * i + j], sib).wait_recv()

    return list(pl.pallas_call(
        body, name=f"gather_land_{tag}", in_specs=[HBM] * n + [SEM, SEM], out_specs=[HBM] * n,
        out_shape=[pltpu.HBM(b.shape, b.dtype) for b in bufs],
        input_output_aliases={i: i for i in range(n)},
        compiler_params=pltpu.CompilerParams(has_side_effects=EFFECT),
        )(*bufs, s_sem, r_sem))


def _rs_pair_add(g, recv, place, tag):
    r, col = g.shape[-2:]
    tr = _tile(r, ROW_TILE)

    def body(place_ref, g_ref, r_ref, wire_ref, own_ref):
        tot = g_ref[...] + r_ref[...]
        wire_ref[...] = tot.astype(wire_ref.dtype)

        @pl.when(pl.program_id(1) == place_ref[1])
        def _():
            own_ref[...] = tot

    grid_spec = pltpu.PrefetchScalarGridSpec(
        num_scalar_prefetch=1, grid=(r // tr, N_CHIP),
        in_specs=[pl.BlockSpec((None, None, tr, col), lambda i, s, p: (s, p[0], i, 0)),
                  pl.BlockSpec((None, tr, col), lambda i, s, p: (s, i, 0))],
        out_specs=[pl.BlockSpec((None, tr, col), lambda i, s, p: (s, i, 0)),
                   pl.BlockSpec((tr, col), lambda i, s, p: (i, 0))])
    return pl.pallas_call(
        body, name=f"rs_pair_add_{tag}", grid_spec=grid_spec,
        out_shape=[_sds((N_CHIP, r, col), WIRE_DTYPE), _sds((r, col), F32)],
        compiler_params=_params())(place, g, recv)


def _pair_plan(srcs, lands):
    x, y, c, _ = _place()
    return [(s.at[:, 1 - c], l, (x, y, 1 - c)) for s, l in zip(srcs, lands)]


def _chip_plan(srcs, lands):
    x, y, c, chips = _place()
    plan = []
    for s, l in zip(srcs, lands):
        if len(s.shape) == 2:
            me = 4 * x + 2 * y + c
            plan += [(s, l.at[me], (x ^ (k >> 2), y ^ ((k >> 1) & 1), c ^ (k & 1))) for k in range(1, N_DEV)]
        else:
            plan += [(s.at[2 * px + py], l.at[j], (px, py, c)) for j, (px, py) in enumerate(chips)]
    return plan


def _exchange_start(srcs, lands, plan, count, tag):
    n = len(srcs)
    both = list(srcs) + list(lands)

    def body(*refs):
        s_sem, r_sem, token = refs[2 * n], refs[2 * n + 1], refs[4 * n + 2]
        for k, (src, dst, dev) in enumerate(plan(refs[:n], refs[n:2 * n])):
            _remote(src, dst, s_sem.at[k], r_sem.at[k], dev).start()
        token[...] = jnp.zeros_like(token)

    dma = pltpu.SemaphoreType.DMA
    res = pl.pallas_call(
        body, name=f"exchange_start_{tag}", in_specs=[HBM] * (2 * n),
        out_specs=[SEM, SEM] + [HBM] * (2 * n) + [pl.BlockSpec(memory_space=pltpu.VMEM)],
        out_shape=[dma((count,)), dma((count,))] + [pltpu.HBM(b.shape, b.dtype) for b in both] + [_sds((8, LANES), F32)],
        input_output_aliases={i: i + 2 for i in range(2 * n)},
        compiler_params=pltpu.CompilerParams(has_side_effects=EFFECT),
        )(*[_in_hbm(b) for b in both])
    return res[0], res[1], list(res[2:2 + n]), list(res[2 + n:2 + 2 * n]), res[-1]


def _exchange_wait(s_sem, r_sem, srcs, lands, plan, after, tag):
    n = len(srcs)

    def body(*refs):
        s_ref, r_ref = refs[2 * n], refs[2 * n + 1]
        for k, (src, dst, dev) in enumerate(plan(refs[:n], refs[n:2 * n])):
            cp = _remote(src, dst, s_ref.at[k], r_ref.at[k], dev)
            cp.wait_send()
            cp.wait_recv()

    both = list(srcs) + list(lands)
    res = pl.pallas_call(
        body, name=f"exchange_wait_{tag}", in_specs=[HBM] * (2 * n) + [SEM, SEM, ANY], out_specs=[HBM] * (2 * n),
        out_shape=[pltpu.HBM(b.shape, b.dtype) for b in both],
        input_output_aliases={i: i for i in range(2 * n)},
        compiler_params=pltpu.CompilerParams(has_side_effects=EFFECT),
        )(*both, s_sem, r_sem, after)
    return list(res[:n]), list(res[n:])


def _rs_chip_add(own, recv, place, l, L, prev, tag):
    r, col = own.shape
    tr = _tile(r, ROW_TILE)

    def body(place_ref, o_ref, r_ref, *rest):
        acc = o_ref[...]
        for j in range(3):
            acc = acc + r_ref[j].astype(F32)
        rest[-1][...] = acc

    in_specs = [pl.BlockSpec((tr, col), lambda i, p: (i, 0)), pl.BlockSpec((3, tr, col), lambda i, p: (0, i, 0))]
    args = [place, own, recv]
    kw = {}
    if prev is not None:
        in_specs.append(ANY)
        args.append(prev)
        kw["input_output_aliases"] = {3: 0}
    grid_spec = pltpu.PrefetchScalarGridSpec(
        num_scalar_prefetch=1, grid=(r // tr,), in_specs=in_specs,
        out_specs=pl.BlockSpec((None, None, tr, col), lambda i, p: (l, p[0], i, 0)))
    return pl.pallas_call(
        body, name=f"rs_chip_add_{tag}", grid_spec=grid_spec, out_shape=_sds((L, 2, r, col), F32),
        compiler_params=_params(), **kw)(*args)


def _rs_pair_share(fulls, tag):
    n = len(fulls)

    def body(*refs):
        outs = refs[n:2 * n]
        s_sem, r_sem = refs[2 * n:]
        x, y, c, _ = _place()
        sib = (x, y, 1 - c)
        started = []
        for i in range(n):
            cp = _remote(outs[i].at[:, c], outs[i].at[:, c], s_sem.at[i], r_sem.at[i], sib)
            cp.start()
            started.append(cp)
        for i, cp in enumerate(started):
            cp.wait_send()
            _remote(outs[i].at[:, 1 - c], outs[i].at[:, 1 - c], s_sem.at[i], r_sem.at[i], sib).wait_recv()

    dma = pltpu.SemaphoreType.DMA
    return pl.pallas_call(
        body, name=f"rs_pair_share_{tag}", in_specs=[ANY] * n, out_specs=[ANY] * n,
        out_shape=[_sds(f.shape, f.dtype) for f in fulls],
        input_output_aliases={i: i for i in range(n)},
        scratch_shapes=[dma((n,)), dma((n,))],
        )(*fulls)


def _ffn_fwd(x, g_pre, g_post, w_in, w_out, tag):
    xn, hgu, act = _norm_mm_in(x, g_pre, w_in, tag, swiglu=True)
    if callable(w_out):
        w_out = w_out(act)
    h, x_out = _mm_out_post(act, w_out.reshape(-1, w_out.shape[-1]), x, g_post, 0.5, tag)
    return x_out, (x, xn, hgu, act, h)


def _ffn_bwd(dx, saved, g_pre, g_post, w_in, w_out, tag, between=None, finish=None):
    x, xn, hgu, act, h = saved
    dh, dg_post, dhgu = _post_bwd_mm(dx, h, g_post, 0.5, w_out.reshape(-1, w_out.shape[-1]), tag, hgu=hgu)
    token = None
    if between is not None:
        token = between(dhgu)
    dw_out = _mm_tn_out(act, dh, tag, after=token)
    dw_in = _mm_tn_in(xn, dhgu, tag, after=token)
    if finish is not None:
        token = finish(dw_in, dw_out)
    if token is not None:
        g_pre = g_pre + token[0, :1]
    dx_in, dg_pre = _mm_nt_pre(dhgu, w_in, dx, x, g_pre, tag)
    return dx_in, dg_pre, dg_post, dw_in, dw_out


def kernel(x, ffn1_pre_g, ffn1_post_g, ffn1_w_in, ffn1_w_out, mix_pre_g, mix_post_g, ffn2_pre_g, ffn2_post_g, ffn2_w_in, ffn2_w_out, conv_w_in, conv_k, conv_w_out, kv_g, kv_w, forget_b, attn_w_qg, attn_w_o, loss_target, m_ffn1_pre_g, m_ffn1_post_g, m_ffn1_w_in, m_ffn1_w_out, m_mix_pre_g, m_mix_post_g, m_ffn2_pre_g, m_ffn2_post_g, m_ffn2_w_in, m_ffn2_w_out, m_conv_w_in, m_conv_k, m_conv_w_out, m_kv_g, m_kv_w, m_forget_b, m_attn_w_qg, m_attn_w_o, v_ffn1_pre_g, v_ffn1_post_g, v_ffn1_w_in, v_ffn1_w_out, v_mix_pre_g, v_mix_post_g, v_ffn2_pre_g, v_ffn2_post_g, v_ffn2_w_in, v_ffn2_w_out, v_conv_w_in, v_conv_k, v_conv_w_out, v_kv_g, v_kv_w, v_forget_b, v_attn_w_qg, v_attn_w_o):
    Bl, S, D = x.shape
    T = Bl * S
    H = forget_b.shape[0]
    assert D == H * HEAD_DIM and D % LANES == 0
    kvc = kv_w.shape[1]
    kvp = -(-kvc // LANES) * LANES
    kv_all = 2 * D + LANES
    dk_cols = conv_k.shape[2]
    chip = 2 * lax.axis_index("x") + lax.axis_index("y")
    core = lax.axis_index("c")

    given = dict(ffn1_w_in=(ffn1_w_in, m_ffn1_w_in, v_ffn1_w_in), ffn1_w_out=(ffn1_w_out, m_ffn1_w_out, v_ffn1_w_out),
                 ffn2_w_in=(ffn2_w_in, m_ffn2_w_in, v_ffn2_w_in), ffn2_w_out=(ffn2_w_out, m_ffn2_w_out, v_ffn2_w_out),
                 conv_w_in=(conv_w_in, m_conv_w_in, v_conv_w_in), conv_w_out=(conv_w_out, m_conv_w_out, v_conv_w_out),
                 kv_w=(kv_w, m_kv_w, v_kv_w), attn_w_qg=(attn_w_qg, m_attn_w_qg, v_attn_w_qg),
                 attn_w_o=(attn_w_o, m_attn_w_o, v_attn_w_o))
    shards = {k: w for k, (w, _, _) in given.items()}
    shards["kv_w"] = jnp.pad(kv_w, ((0, 0), (0, kvp - kvc)))[None]
    groups = [[("ffn1_w_in", 0), ("ffn1_w_out", 0)], [("conv_w_in", 0), ("conv_w_out", 0)],
              [("ffn2_w_in", 0), ("ffn2_w_out", 0)], [("kv_w", 0), ("ffn1_w_in", 1), ("ffn1_w_out", 1)],
              [("attn_w_qg", 0), ("attn_w_o", 0), ("ffn2_w_in", 1), ("ffn2_w_out", 1)]]
    second = groups[3] + groups[4]
    place = jnp.stack([core, chip]).astype(jnp.int32)

    def slot(key, where):
        w = shards[key[0]]
        L, r, col = w.shape
        return _own_slot(w.reshape(L, 2, r // 2, col), key[1], MM_DTYPE, where, f"{key[0]}{key[1]}")

    def whole(g):
        return g.reshape(N_CHIP, -1, g.shape[-1])

    taps_slot = _own_slot(jnp.pad(conv_k[0], ((0, 13), (0, 0))).reshape(1, 2, 8, dk_cols), 0, F32, place, "conv_k")
    fb = jnp.pad(forget_b, (0, LANES - H)).reshape(1, LANES)
    w_in0, w_out0 = groups[0]
    s_0, r_0, fly_0, token = _gather_start([slot(w_in0, place), taps_slot, slot(w_out0, place)], fb, "first")
    later = groups[1] + groups[2] + groups[3] + groups[4]
    s_1, r_1, fly_1, token = _gather_start([slot(key, place) for key in later], token, "rest")
    W = {}

    def land(sems, bufs, lo, after, tag):
        return _gather_land(*_gather_pass(*sems, bufs, lo, after, tag), tag)

    def arrive(g, after):
        lo = sum(len(groups[k]) for k in range(1, g))
        got = land((s_1, r_1), fly_1[lo:lo + len(groups[g])], lo, after, f"g{g}")
        W.update({key: whole(b) for key, b in zip(groups[g], got)})

    w_first, taps = land((s_0, r_0), fly_0[:2], 0, token, "g0")
    k_taps = taps.reshape(N_CHIP, 16, dk_cols).transpose(1, 0, 2).reshape(16, D)[:8]

    x0 = x.reshape(T, D)
    W[w_in0] = whole(w_first)

    def first_w_out(act):
        W[w_out0] = whole(land((s_0, r_0), fly_0[2:], 2, act, "g0_out")[0])
        return W[w_out0]

    x1, s_f1a = _ffn_fwd(x0, ffn1_pre_g[0], ffn1_post_g[0], W[w_in0], first_w_out, "l0f1")
    arrive(1, x1)
    w_o_conv = W["conv_w_out", 0].reshape(D, D)
    xn_c, bch = _norm_mm_in(x1, mix_pre_g[0], W["conv_w_in", 0], "conv")
    z_c = _conv_fwd(bch, k_taps, Bl, S)
    m_c, x2 = _mm_out_post(z_c, w_o_conv, x1, mix_post_g[0], 1.0, "conv_out")
    arrive(2, x2)
    x3, s_f2a = _ffn_fwd(x2, ffn2_pre_g[0], ffn2_post_g[0], W["ffn2_w_in", 0], W["ffn2_w_out", 0], "l0f2")

    arrive(3, x3)
    kv_full = jnp.concatenate([W["kv_w", 0][s, :, :kvc] for s in range(N_CHIP)], axis=1)
    kv_full = jnp.pad(kv_full, ((0, 0), (0, kv_all - kv_full.shape[1])))
    xn_kv = _rms_fwd(x3, kv_g, "kv")
    kvact = _mm_nn(xn_kv, kv_full[:, :2 * D], MM_DTYPE, "kv")
    pf = _mm_nn(xn_kv, kv_full[:, 2 * D:], F32, "forget")
    cum = _forget_fwd(pf, fb, Bl, S)
    bq = min(S, ATT_BLOCK)
    c3 = cum.reshape(Bl, S, LANES)[:, :, :H].transpose(0, 2, 1)
    c_col = jnp.broadcast_to(c3[..., None], (Bl, H, S, LANES))
    c_row = c3.reshape(Bl, H, S // bq, 1, bq)

    x4, s_f1b = _ffn_fwd(x3, ffn1_pre_g[1], ffn1_post_g[1], W["ffn1_w_in", 1], W["ffn1_w_out", 1], "l1f1")
    arrive(4, x4)
    w_o_attn = W["attn_w_o", 0].reshape(D, D)
    xn_a, qg = _norm_mm_in(x4, mix_pre_g[1], W["attn_w_qg", 0], "qg")
    o, lse, z_a = _attn_fwd(qg, kvact, c_col, c_row, Bl, S, D)
    m_a, x5 = _mm_out_post(z_a, w_o_attn, x4, mix_post_g[1], 1.0, "attn_out")
    x6, s_f2b = _ffn_fwd(x5, ffn2_pre_g[1], ffn2_post_g[1], W["ffn2_w_in", 1], W["ffn2_w_out", 1], "l1f2")

    dy, loss_local = _loss_grad(x6, loss_target.reshape(T, D))

    G = {}
    dx5, dg_f2pre_1, dg_f2post_1, G["ffn2_w_in", 1], G["ffn2_w_out", 1] = _ffn_bwd(
        dy, s_f2b, ffn2_pre_g[1], ffn2_post_g[1], W["ffn2_w_in", 1], W["ffn2_w_out", 1], "l1f2")
    dm_a, dg_mixpost_1, dz_a = _post_bwd_mm(dx5, m_a, mix_post_g[1], 1.0, w_o_attn, "attn_out")
    G["attn_w_o", 0] = _mm_tn_out(z_a, dm_a, "attn_out")
    dq, dk, dv, dcr = _attn_bwd(qg, kvact, dz_a, lse, c_col, c_row, Bl, S, D)
    dqg = _gate_bwd(dz_a, qg, o, dq)
    G["attn_w_qg", 0] = _mm_tn_in(xn_a, dqg, "qg")
    dx4, dg_mixpre_1 = _mm_nt_pre(dqg, W["attn_w_qg", 0], dx5, x4, mix_pre_g[1], "qg")
    dx3, dg_f1pre_1, dg_f1post_1, G["ffn1_w_in", 1], G["ffn1_w_out", 1] = _ffn_bwd(
        dx4, s_f1b, ffn1_pre_g[1], ffn1_post_g[1], W["ffn1_w_in", 1], W["ffn1_w_out", 1], "l1f1")

    dcum = jnp.pad(dcr.reshape(Bl, H, S).transpose(0, 2, 1), ((0, 0), (0, 0), (0, LANES - H))).reshape(T, LANES)
    dpf, dfb = _forget_bwd(dcum, pf, fb, Bl, S)
    dp = _pack_dkv(dk, dv, dpf)
    G_kv_full = _mm_tn(xn_kv, dp, "kv")
    G["kv_w", 0] = jnp.stack([jnp.pad(G_kv_full[:, s * kvc:(s + 1) * kvc], ((0, 0), (0, kvp - kvc))) for s in range(N_CHIP)])
    dx3, dg_kv = _mm_nt_pre(dp, kv_full, dx3, x3, kv_g, "kv")

    def halves_of(keys):
        return [G[k].reshape(N_CHIP, 2, G[k].shape[1] // 2, G[k].shape[2]) for k in keys]

    def pair_adds(keys, grads, recvs):
        wires, owns = [], []
        for k, g, r in zip(keys, grads, recvs):
            w, own = _rs_pair_add(g, r, place, f"{k[0]}{k[1]}")
            wires.append(w)
            owns.append(own)
        return wires, owns

    def chip_start(wires, tag, extra=()):
        lands = [lax.empty((3,) + w.shape[1:], w.dtype) for w in wires]
        lands += [jnp.zeros((N_DEV,) + e.shape, e.dtype) for e in extra]
        return _exchange_start(list(wires) + list(extra), lands, _chip_plan, 3 * len(wires) + (N_DEV - 1) * len(extra), tag)

    late = groups[2] + groups[1]
    last = groups[0]
    grads_2 = halves_of(second)
    p_sems, p_semr, grads_2, sib_2, token = _exchange_start(
        grads_2, [lax.empty((N_CHIP,) + g.shape[2:], g.dtype) for g in grads_2], _pair_plan, len(grads_2), "pair_second")

    dx2, dg_f2pre_0, dg_f2post_0, G["ffn2_w_in", 0], G["ffn2_w_out", 0] = _ffn_bwd(
        dx3, s_f2a, ffn2_pre_g[0], ffn2_post_g[0] + token[0, :1], W["ffn2_w_in", 0], W["ffn2_w_out", 0], "l0f2")
    grads_2, sib_2 = _exchange_wait(p_sems, p_semr, grads_2, sib_2, _pair_plan, dx2, "pair_second")
    wires_2, owns_2 = pair_adds(second, grads_2, sib_2)
    c_2 = chip_start(wires_2, "chip_second")
    dm_c, dg_mixpost_0, dz_c = _post_bwd_mm(dx2, m_c, mix_post_g[0] + c_2[4][0, :1], 1.0, w_o_conv, "conv_out")
    G["conv_w_out", 0] = _mm_tn_out(z_c, dm_c, "conv_out")
    db, dcg, dhh, dk_taps = _conv_bwd(bch, dz_c, k_taps, Bl, S)
    dbch = jnp.concatenate([db, dcg, dhh], axis=1)
    G["conv_w_in", 0] = _mm_tn_in(xn_c, dbch, "conv")
    dx1, dg_mixpre_0 = _mm_nt_pre(dbch, W["conv_w_in", 0], dx2, x1, mix_pre_g[0], "conv")
    def pair_start(keys, tag):
        grads = halves_of(keys)
        lands = [lax.empty((N_CHIP,) + g.shape[2:], g.dtype) for g in grads]
        return _exchange_start(grads, lands, _pair_plan, len(grads), tag)

    p_l = pair_start(late, "pair_late")
    late_done = {}

    def late_leg(dhgu):
        grads_l, sib_l = _exchange_wait(*p_l[:4], _pair_plan, dhgu, "pair_late")
        late_done["wires"], late_done["owns"] = pair_adds(late, grads_l, sib_l)
        late_done["chip"] = chip_start(late_done["wires"], "chip_late")
        return late_done["chip"][4]

    def last_pair(dw_in, dw_out):
        G["ffn1_w_in", 0], G["ffn1_w_out", 0] = dw_in, dw_out
        late_done["pair"] = pair_start(last, "pair_last")
        return late_done["pair"][4]

    dx0, dg_f1pre_0, dg_f1post_0, _, _ = _ffn_bwd(
        dx1, s_f1a, ffn1_pre_g[0], ffn1_post_g[0] + p_l[4][0, :1], W["ffn1_w_in", 0], W["ffn1_w_out", 0], "l0f1",
        between=late_leg, finish=last_pair)
    grad_x = dx0.reshape(Bl, S, D)
    owns_l, c_l, p_1 = late_done["owns"], late_done["chip"], late_done["pair"]

    grads_1, sib_1 = _exchange_wait(*p_1[:4], _pair_plan, dx0, "pair_last")
    wires_1, owns_1 = pair_adds(last, grads_1, sib_1)

    def row(v):
        return jnp.pad(v.reshape(-1), (0, D - v.size)).reshape(1, D)

    small_parts = [dg_f1pre_0, dg_f1pre_1, dg_f1post_0, dg_f1post_1, dg_mixpre_0, dg_mixpre_1, dg_mixpost_0, dg_mixpost_1,
                   dg_f2pre_0, dg_f2pre_1, dg_f2post_0, dg_f2post_1, dg_kv, row(dfb[0, :H]), dk_taps[:3],
                   jnp.full((1, D), loss_local)]
    small = jnp.concatenate(small_parts, axis=0)
    small = jnp.pad(small, ((0, SMALL_ROWS - small.shape[0]), (0, 0)))
    c_1 = chip_start(wires_1, "chip_last", extra=[small])
    _, recvs_2 = _exchange_wait(*c_2[:4], _chip_plan, c_1[4], "chip_second")
    _, recvs_l = _exchange_wait(*c_l[:4], _chip_plan, c_1[4], "chip_late")
    partial = {}

    def chip_adds(keys, owns, recvs):
        for (name, l), own, rcv in zip(keys, owns, recvs):
            partial[name] = _rs_chip_add(own, rcv, place, l, shards[name].shape[0], partial.get(name), f"{name}{l}")

    chip_adds(late + second, owns_l + owns_2, recvs_l + recvs_2)
    res = {}

    def adamw(names, reduced, after):
        for k, red in zip(names, reduced):
            w, m, v = given[k]
            g2 = red.reshape(-1, red.shape[-1])
            if k == "kv_w":
                g2 = g2[:, :kvc]
            flat = lambda a: a.reshape(-1, a.shape[-1])
            go, d, mn, vn = _adamw(flat(w), g2, flat(m), flat(v), k, after=after)
            res[k] = tuple(a.reshape(w.shape) for a in (go, d, mn, vn))
        return d

    early = [k for k in partial if (k, 0) not in last]
    done = adamw(early, _rs_pair_share([partial[k] for k in early], "early"), c_1[4])
    _, recvs_1 = _exchange_wait(*c_1[:4], _chip_plan, done, "chip_last")
    chip_adds(last, owns_1, recvs_1[:-1])
    rest = [k for k, _ in last]
    adamw(rest, _rs_pair_share([partial[k] for k in rest], "last"), None)
    gsum = _sum_devices(recvs_1[-1], small, place)
    loss = gsum[17, 0]

    small_names = ["ffn1_pre_g", "ffn1_post_g", "mix_pre_g", "mix_post_g", "ffn2_pre_g", "ffn2_post_g"]
    small_given = dict(ffn1_pre_g=(ffn1_pre_g, m_ffn1_pre_g, v_ffn1_pre_g), ffn1_post_g=(ffn1_post_g, m_ffn1_post_g, v_ffn1_post_g),
                       mix_pre_g=(mix_pre_g, m_mix_pre_g, v_mix_pre_g), mix_post_g=(mix_post_g, m_mix_post_g, v_mix_post_g),
                       ffn2_pre_g=(ffn2_pre_g, m_ffn2_pre_g, v_ffn2_pre_g), ffn2_post_g=(ffn2_post_g, m_ffn2_post_g, v_ffn2_post_g))

    def pack(idx):
        rows_ = [small_given[k][idx] for k in small_names]
        rows_ += [row((kv_g, m_kv_g, v_kv_g)[idx]), row((forget_b, m_forget_b, v_forget_b)[idx])]
        rows_.append(jnp.pad((conv_k, m_conv_k, v_conv_k)[idx][0], ((0, 0), (0, D - dk_cols))))
        a = jnp.concatenate(rows_, axis=0)
        return jnp.pad(a, ((0, SMALL_ROWS - a.shape[0]), (0, 0)))

    g_taps = lax.dynamic_slice_in_dim(gsum[14:17], chip * dk_cols, dk_cols, axis=1)
    g_small = jnp.concatenate([gsum[:14], jnp.pad(g_taps, ((0, 0), (0, D - dk_cols))), gsum[17:]], axis=0)
    g_small, d_s, m_s, v_s = _adamw(pack(0), g_small, pack(1), pack(2), "small")
    for i, k in enumerate(small_names):
        res[k] = tuple(a[2 * i:2 * i + 2] for a in (g_small, d_s, m_s, v_s))
    res["kv_g"] = tuple(a[12] for a in (g_small, d_s, m_s, v_s))
    res["forget_b"] = tuple(a[13, :H] for a in (g_small, d_s, m_s, v_s))
    res["conv_k"] = tuple(a[14:17, :dk_cols][None] for a in (g_small, d_s, m_s, v_s))

    order = ["ffn1_pre_g", "ffn1_post_g", "ffn1_w_in", "ffn1_w_out", "mix_pre_g", "mix_post_g", "ffn2_pre_g", "ffn2_post_g",
             "ffn2_w_in", "ffn2_w_out", "conv_w_in", "conv_k", "conv_w_out", "kv_g", "kv_w", "forget_b", "attn_w_qg", "attn_w_o"]
    out = [loss, grad_x]
    for idx in range(4):
        out += [res[k][idx] for k in order]
    return tuple(out)
```

```python
import math

import jax
import jax.numpy as jnp
from jax import lax
from jax.experimental import pallas as pl
from jax.experimental.pallas import tpu as pltpu

F32 = jnp.float32
MM_DTYPE = jnp.bfloat16
WIRE_DTYPE = jnp.bfloat16

RMS_EPS = 1e-6
ADAM_LR = 0.001
ADAM_B1 = 0.9
ADAM_B2 = 0.999
ADAM_EPS = 1e-08
ADAM_WD = 0.01
ADAM_STEP = 10

HEAD_DIM = 64
LANES = 128
N_CHIP = 4
N_DEV = 8
ROW_TILE = 512
MM_TILE = 512
FUSED_TILE = 512
TN_TILE = 2048
ATT_BLOCK = 512
SMALL_ROWS = 24
V7X_VMEM_BYTES = 64 * 1024 * 1024
VMEM_LIMIT = V7X_VMEM_BYTES - 8 * 1024 * 1024
MESH = pl.DeviceIdType.MESH
ANY = pl.BlockSpec(memory_space=pl.ANY)

NT = (((1,), (1,)), ((), ()))
TN = (((0,), (0,)), ((), ()))


def _tile(n, pref):
    if n <= pref:
        return n
    t = pref - pref % 16
    while n % t:
        t -= 16
    return t


def _params():
    return pltpu.CompilerParams(vmem_limit_bytes=VMEM_LIMIT)


def _sds(shape, dtype):
    return jax.ShapeDtypeStruct(shape, dtype)


def _rows(tm, c):
    return pl.BlockSpec((tm, c), lambda i: (i, 0))


def _whole(shape):
    return pl.BlockSpec(shape, lambda *_: (0,) * len(shape))


def _resident(shape):
    return pl.BlockSpec(shape, lambda *_: (0,) * len(shape), pipeline_mode=pl.Buffered(1))


def _rms_fwd(x, g, tag):
    T, D = x.shape
    tm = _tile(T, ROW_TILE)

    def body(x_ref, g_ref, o_ref):
        xv = x_ref[...]
        r = lax.rsqrt(jnp.mean(xv * xv, axis=-1, keepdims=True) + RMS_EPS)
        o_ref[...] = (xv * r * g_ref[...]).astype(o_ref.dtype)

    return pl.pallas_call(
        body, name=f"rms_fwd_{tag}", grid=(T // tm,),
        in_specs=[_rows(tm, D), _whole((1, D))], out_specs=_rows(tm, D),
        out_shape=_sds((T, D), MM_DTYPE), compiler_params=_params())(x, g.reshape(1, D))


def _accumulate(ref, part, first):
    @pl.when(first)
    def _():
        ref[...] = part

    @pl.when(jnp.logical_not(first))
    def _():
        ref[...] += part


def _loss_grad(y, tgt):
    T, D = y.shape
    tm = _tile(T, ROW_TILE)

    def body(y_ref, t_ref, dy_ref, l_ref):
        e = y_ref[...] - t_ref[...]
        row = jnp.mean(e * e, axis=-1, keepdims=True)
        part = jnp.broadcast_to(jnp.sum(row, axis=0, keepdims=True), (8, LANES))
        _accumulate(l_ref, part, pl.program_id(0) == 0)
        dy_ref[...] = e * (1.0 / D)

    dy, lsum = pl.pallas_call(
        body, name="loss_grad", grid=(T // tm,),
        in_specs=[_rows(tm, D), _rows(tm, D)], out_specs=[_rows(tm, D), _whole((8, LANES))],
        out_shape=[_sds((T, D), F32), _sds((8, LANES), F32)], compiler_params=_params())(y, tgt)
    return dy, 0.5 * lsum[0, 0]


def _shift_down(u, d, rows):
    return jnp.where(rows >= d, pltpu.roll(u, d, 0), 0.0)


def _shift_up(u, d, rows, S):
    return jnp.where(rows < S - d, pltpu.roll(u, S - d, 0), 0.0)


def _conv_fwd(bch, k8, Bl, S):
    T, D3 = bch.shape
    D = D3 // 3
    dc = min(D, 2 * LANES)
    nd = D // dc

    def body(b_ref, c_ref, h_ref, k_ref, z_ref):
        rows = lax.broadcasted_iota(jnp.int32, (S, 1), 0)
        u = c_ref[...].astype(F32) * h_ref[...].astype(F32)
        y = k_ref[2:3, :] * u + k_ref[1:2, :] * _shift_down(u, 1, rows) + k_ref[0:1, :] * _shift_down(u, 2, rows)
        z_ref[...] = (b_ref[...].astype(F32) * y).astype(z_ref.dtype)

    return pl.pallas_call(
        body, name="conv_fwd", grid=(Bl, nd),
        in_specs=[pl.BlockSpec((S, dc), lambda b, j: (b, j)),
                  pl.BlockSpec((S, dc), lambda b, j: (b, nd + j)),
                  pl.BlockSpec((S, dc), lambda b, j: (b, 2 * nd + j)),
                  pl.BlockSpec((8, dc), lambda b, j: (0, j))],
        out_specs=pl.BlockSpec((S, dc), lambda b, j: (b, j)),
        out_shape=_sds((T, D), MM_DTYPE), compiler_params=_params())(bch, bch, bch, k8)


def _conv_bwd(bch, dz, k8, Bl, S):
    T, D3 = bch.shape
    D = D3 // 3
    dc = min(D, 2 * LANES)
    nd = D // dc

    def body(b_ref, c_ref, h_ref, dz_ref, k_ref, db_ref, dc_ref, dh_ref, dk_ref):
        rows = lax.broadcasted_iota(jnp.int32, (S, 1), 0)
        bv = b_ref[...].astype(F32)
        cv = c_ref[...].astype(F32)
        hv = h_ref[...].astype(F32)
        dzv = dz_ref[...].astype(F32)
        u = cv * hv
        u1 = _shift_down(u, 1, rows)
        u2 = _shift_down(u, 2, rows)
        y = k_ref[2:3, :] * u + k_ref[1:2, :] * u1 + k_ref[0:1, :] * u2
        db_ref[...] = (dzv * y).astype(db_ref.dtype)
        dy = dzv * bv
        du = k_ref[2:3, :] * dy + k_ref[1:2, :] * _shift_up(dy, 1, rows, S) + k_ref[0:1, :] * _shift_up(dy, 2, rows, S)
        dc_ref[...] = (du * hv).astype(dc_ref.dtype)
        dh_ref[...] = (du * cv).astype(dh_ref.dtype)

        @pl.when(pl.program_id(1) == 0)
        def _():
            dk_ref[...] = jnp.zeros_like(dk_ref)

        dk_ref[0:1, :] += jnp.sum(dy * u2, axis=0, keepdims=True)
        dk_ref[1:2, :] += jnp.sum(dy * u1, axis=0, keepdims=True)
        dk_ref[2:3, :] += jnp.sum(dy * u, axis=0, keepdims=True)

    seq = lambda off: pl.BlockSpec((S, dc), lambda j, b: (b, off + j))
    return pl.pallas_call(
        body, name="conv_bwd", grid=(nd, Bl),
        in_specs=[seq(0), seq(nd), seq(2 * nd), seq(0), pl.BlockSpec((8, dc), lambda j, b: (0, j))],
        out_specs=[seq(0), seq(0), seq(0), pl.BlockSpec((8, dc), lambda j, b: (0, j))],
        out_shape=[_sds((T, D), MM_DTYPE)] * 3 + [_sds((8, D), F32)],
        compiler_params=_params())(bch, bch, bch, dz, k8)


def _forget_fwd(pf, fb, Bl, S):
    T = pf.shape[0]

    def body(p_ref, fb_ref, c_ref):
        rows = lax.broadcasted_iota(jnp.int32, (S, 1), 0)
        z = p_ref[...] + fb_ref[...]
        acc = jnp.minimum(z, 0.0) - jnp.log1p(jnp.exp(-jnp.abs(z)))
        d = 1
        while d < S:
            acc = acc + _shift_down(acc, d, rows)
            d *= 2
        c_ref[...] = acc

    return pl.pallas_call(
        body, name="forget_fwd", grid=(Bl,),
        in_specs=[_rows(S, LANES), _whole((1, LANES))], out_specs=_rows(S, LANES),
        out_shape=_sds((T, LANES), F32), compiler_params=_params())(pf, fb)


def _forget_bwd(dc, pf, fb, Bl, S):
    T = pf.shape[0]

    def body(dc_ref, p_ref, fb_ref, df_ref, dfb_ref):
        rows = lax.broadcasted_iota(jnp.int32, (S, 1), 0)
        acc = dc_ref[...]
        d = 1
        while d < S:
            acc = acc + _shift_up(acc, d, rows, S)
            d *= 2
        df = acc * jax.nn.sigmoid(-(p_ref[...] + fb_ref[...]))
        df_ref[...] = df.astype(df_ref.dtype)
        _accumulate(dfb_ref, jnp.sum(df, axis=0, keepdims=True), pl.program_id(0) == 0)

    return pl.pallas_call(
        body, name="forget_bwd", grid=(Bl,),
        in_specs=[_rows(S, LANES), _rows(S, LANES), _whole((1, LANES))],
        out_specs=[_rows(S, LANES), _whole((1, LANES))],
        out_shape=[_sds((T, LANES), MM_DTYPE), _sds((1, LANES), F32)],
        compiler_params=_params())(dc, pf, fb)


def _head_mask(h):
    lane = lax.broadcasted_iota(jnp.int32, (1, LANES), 1)
    return (lane >= h * HEAD_DIM) & (lane < (h + 1) * HEAD_DIM)


def _attn_fwd(qg, kv, c_col, c_row, Bl, S, D):
    T = Bl * S
    H = D // HEAD_DIM
    HP = D // LANES
    bq = min(S, ATT_BLOCK)
    nq = S // bq
    scale = 1.0 / math.sqrt(HEAD_DIM)

    def body(q_ref, g_ref, k_ref, v_ref, cc_ref, cr_ref, o_ref, lse_ref, z_ref):
        i = pl.program_id(2)
        q2 = q_ref[...]
        qh = [q2 * (_head_mask(h).astype(F32) * scale).astype(q2.dtype) for h in range(2)]
        cc = [cc_ref[h][:, :1] for h in range(2)]
        diag = lax.broadcasted_iota(jnp.int32, (1, bq), 1) <= lax.broadcasted_iota(jnp.int32, (bq, 1), 0)

        def block(j, carry, on_diagonal):
            off = pl.multiple_of(j * bq, bq)
            kj = k_ref[pl.ds(off, bq), :]
            vj = v_ref[pl.ds(off, bq), :]
            new = []
            for h in range(2):
                m, l, acc = carry[h]
                s = lax.dot_general(qh[h], kj, NT, preferred_element_type=F32) + cc[h] - cr_ref[h, j]
                if on_diagonal:
                    s = jnp.where(diag, s, -jnp.inf)
                m_new = jnp.maximum(m, jnp.max(s, axis=1, keepdims=True))
                p = jnp.exp(s - m_new)
                a = jnp.exp(m - m_new)
                l = a * l + jnp.sum(p, axis=1, keepdims=True)
                acc = a * acc + jnp.dot(p.astype(MM_DTYPE), vj, preferred_element_type=F32)
                new.append((m_new, l, acc))
            return tuple(new)

        one = (jnp.full((bq, 1), -jnp.inf, F32), jnp.zeros((bq, 1), F32), jnp.zeros((bq, LANES), F32))
        carry = lax.fori_loop(0, i, lambda j, c: block(j, c, False), (one, one))
        carry = block(i, carry, True)
        outs = []
        for h in range(2):
            m, l, acc = carry[h]
            outs.append(acc / l)
            lse_ref[h] = jnp.broadcast_to(m + jnp.log(l), (bq, LANES))
        o2 = jnp.where(_head_mask(0), outs[0], outs[1])
        o_ref[...] = o2
        z_ref[...] = (jax.nn.sigmoid(g_ref[...].astype(F32)) * o2).astype(z_ref.dtype)

    return pl.pallas_call(
        body, name="attn_fwd", grid=(Bl, HP, nq),
        in_specs=[pl.BlockSpec((bq, LANES), lambda b, hp, i: (b * nq + i, hp)),
                  pl.BlockSpec((bq, LANES), lambda b, hp, i: (b * nq + i, HP + hp)),
                  pl.BlockSpec((S, LANES), lambda b, hp, i: (b, hp)),
                  pl.BlockSpec((S, LANES), lambda b, hp, i: (b, HP + hp)),
                  pl.BlockSpec((None, 2, bq, LANES), lambda b, hp, i: (b, hp, i, 0)),
                  pl.BlockSpec((None, 2, nq, 1, bq), lambda b, hp, i: (b, hp, 0, 0, 0))],
        out_specs=[pl.BlockSpec((bq, LANES), lambda b, hp, i: (b * nq + i, hp)),
                   pl.BlockSpec((None, 2, bq, LANES), lambda b, hp, i: (b, hp, i, 0)),
                   pl.BlockSpec((bq, LANES), lambda b, hp, i: (b * nq + i, hp))],
        out_shape=[_sds((T, D), F32), _sds((Bl, H, S, LANES), F32), _sds((T, D), MM_DTYPE)],
        compiler_params=_params())(qg, qg, kv, kv, c_col, c_row)


def _attn_bwd(qg, kv, dz, lse, c_col, c_row, Bl, S, D):
    T = Bl * S
    H = D // HEAD_DIM
    HP = D // LANES
    bq = min(S, ATT_BLOCK)
    nq = S // bq
    scale = 1.0 / math.sqrt(HEAD_DIM)

    def body(q_ref, g_ref, k_ref, v_ref, dz_ref, lse_ref, cc_ref, cr_ref, dq_ref, dk_ref, dv_ref, dcr_ref, p_sc, dp_sc):
        i = pl.program_id(2)

        @pl.when(i == 0)
        def _():
            dk_ref[...] = jnp.zeros_like(dk_ref)
            dv_ref[...] = jnp.zeros_like(dv_ref)
            dcr_ref[...] = jnp.zeros_like(dcr_ref)

        q2 = q_ref[...]
        do2 = (dz_ref[...].astype(F32) * jax.nn.sigmoid(g_ref[...].astype(F32))).astype(MM_DTYPE)
        masks = [_head_mask(h).astype(F32) for h in range(2)]
        qh = [q2 * (masks[h] * scale).astype(q2.dtype) for h in range(2)]
        doh = [do2 * masks[h].astype(do2.dtype) for h in range(2)]
        cc = [cc_ref[h][:, :1] for h in range(2)]
        lse = [lse_ref[h][:, :1] for h in range(2)]
        diag = lax.broadcasted_iota(jnp.int32, (1, bq), 1) <= lax.broadcasted_iota(jnp.int32, (bq, 1), 0)

        def sweep1(j, delta, on_diagonal):
            off = pl.multiple_of(j * bq, bq)
            kj = k_ref[pl.ds(off, bq), :]
            vj = v_ref[pl.ds(off, bq), :]
            new = []
            dv = None
            for h in range(2):
                s = lax.dot_general(qh[h], kj, NT, preferred_element_type=F32) + cc[h] - cr_ref[h, j]
                if on_diagonal:
                    s = jnp.where(diag, s, -jnp.inf)
                p = jnp.exp(s - lse[h])
                dp = lax.dot_general(doh[h], vj, NT, preferred_element_type=F32)
                p_sc[h, j] = p
                dp_sc[h, j] = dp
                part = lax.dot_general(p.astype(MM_DTYPE), doh[h], TN, preferred_element_type=F32)
                dv = part if dv is None else dv + part
                new.append(delta[h] + jnp.sum(p * dp, axis=1, keepdims=True))
            dv_ref[pl.ds(off, bq), :] += dv
            return tuple(new)

        zero = jnp.zeros((bq, 1), F32)
        delta = lax.fori_loop(0, i, lambda j, d: sweep1(j, d, False), (zero, zero))
        delta = sweep1(i, delta, True)

        def sweep2(j, dq):
            off = pl.multiple_of(j * bq, bq)
            kj = k_ref[pl.ds(off, bq), :]
            dk = None
            for h in range(2):
                ds = p_sc[h, j] * (dp_sc[h, j] - delta[h])
                dcr_ref[h, j] -= jnp.sum(ds, axis=0, keepdims=True)
                dsb = ds.astype(MM_DTYPE)
                dq = dq + jnp.dot(dsb, kj * (masks[h] * scale).astype(kj.dtype), preferred_element_type=F32)
                part = lax.dot_general(dsb, qh[h], TN, preferred_element_type=F32)
                dk = part if dk is None else dk + part
            dk_ref[pl.ds(off, bq), :] += dk
            return dq

        dq_ref[...] = lax.fori_loop(0, i + 1, sweep2, jnp.zeros((bq, LANES), F32))

    blk = lambda col: pl.BlockSpec((bq, LANES), lambda b, hp, i: (b * nq + i, col(hp)))
    seq = lambda col: pl.BlockSpec((S, LANES), lambda b, hp, i: (b, col(hp)))
    per_head = pl.BlockSpec((None, 2, bq, LANES), lambda b, hp, i: (b, hp, i, 0))
    rows = pl.BlockSpec((None, 2, nq, 1, bq), lambda b, hp, i: (b, hp, 0, 0, 0))
    return pl.pallas_call(
        body, name="attn_bwd", grid=(Bl, HP, nq),
        in_specs=[blk(lambda hp: hp), blk(lambda hp: HP + hp), seq(lambda hp: hp), seq(lambda hp: HP + hp),
                  blk(lambda hp: hp), per_head, per_head, rows],
        out_specs=[blk(lambda hp: hp), seq(lambda hp: hp), seq(lambda hp: hp), rows],
        out_shape=[_sds((T, D), F32), _sds((T, D), F32), _sds((T, D), F32), _sds((Bl, H, nq, 1, bq), F32)],
        scratch_shapes=[pltpu.VMEM((2, nq, bq, bq), F32), pltpu.VMEM((2, nq, bq, bq), F32)],
        compiler_params=_params())(qg, qg, kv, kv, dz, lse, c_col, c_row)


def _pack_dkv(dk, dv, dpf):
    T, D = dk.shape
    tm = _tile(T, ROW_TILE)

    def body(dk_ref, dv_ref, df_ref, o_ref):
        o_ref[:, :D] = dk_ref[...].astype(o_ref.dtype)
        o_ref[:, D:2 * D] = dv_ref[...].astype(o_ref.dtype)
        o_ref[:, 2 * D:] = df_ref[...]

    return pl.pallas_call(
        body, name="pack_dkv", grid=(T // tm,),
        in_specs=[_rows(tm, D), _rows(tm, D), _rows(tm, LANES)], out_specs=_rows(tm, 2 * D + LANES),
        out_shape=_sds((T, 2 * D + LANES), MM_DTYPE), compiler_params=_params())(dk, dv, dpf)


def _gate_bwd(dz, qg, o, dq):
    T, D = dz.shape
    tm = _tile(T, ROW_TILE)

    def body(dz_ref, g_ref, o_ref, dq_ref, out_ref):
        g = g_ref[...].astype(F32)
        sg = jax.nn.sigmoid(g)
        out_ref[:, :D] = dq_ref[...].astype(out_ref.dtype)
        out_ref[:, D:] = (dz_ref[...].astype(F32) * o_ref[...] * sg * (1.0 - sg)).astype(out_ref.dtype)

    return pl.pallas_call(
        body, name="gate_bwd", grid=(T // tm,),
        in_specs=[_rows(tm, D), pl.BlockSpec((tm, D), lambda i: (i, 1)), _rows(tm, D), _rows(tm, D)],
        out_specs=_rows(tm, 2 * D), out_shape=_sds((T, 2 * D), MM_DTYPE),
        compiler_params=_params())(dz, qg, o, dq)


def _mm_nn(a, b, out_dtype, tag):
    T, K = a.shape
    N = b.shape[1]
    tm = _tile(T, MM_TILE)

    def body(a_ref, b_ref, o_ref):
        o_ref[...] = jnp.dot(a_ref[...], b_ref[...], preferred_element_type=F32).astype(o_ref.dtype)

    return pl.pallas_call(
        body, name=f"mm_nn_{tag}", grid=(T // tm,),
        in_specs=[_rows(tm, K), _whole((K, N))], out_specs=_rows(tm, N),
        out_shape=_sds((T, N), out_dtype), compiler_params=_params())(a, b)


def _mm_tn_in(a, dy, tag, after=None):
    T, K = a.shape
    n = dy.shape[1] // N_CHIP
    tt = _tile(T, TN_TILE)
    extra = [] if after is None else [after]

    def body(a_ref, d_ref, *rest):
        part = lax.dot_general(a_ref[...], d_ref[...], TN, preferred_element_type=F32)
        _accumulate(rest[-1], part, pl.program_id(1) == 0)

    return pl.pallas_call(
        body, name=f"mm_tn_in_{tag}", grid=(N_CHIP, T // tt),
        in_specs=[pl.BlockSpec((tt, K), lambda s, t: (t, 0)), pl.BlockSpec((tt, n), lambda s, t: (t, s))]
        + [ANY] * len(extra),
        out_specs=pl.BlockSpec((None, K, n), lambda s, t: (s, 0, 0)), out_shape=_sds((N_CHIP, K, n), F32),
        compiler_params=_params())(a, dy, *extra)


def _mm_tn_out(act, dh, tag, after=None):
    T, R4 = act.shape
    D = dh.shape[1]
    r = R4 // N_CHIP
    g = 1 if r % LANES == 0 else 2
    tt = _tile(T, TN_TILE)
    extra = [] if after is None else [after]

    def body(a_ref, d_ref, *rest):
        o_ref = rest[-1]
        part = lax.dot_general(a_ref[...], d_ref[...], TN, preferred_element_type=F32)
        first = pl.program_id(1) == 0
        for q in range(g):
            _accumulate(o_ref.at[q], part[q * r:(q + 1) * r], first)

    return pl.pallas_call(
        body, name=f"mm_tn_out_{tag}", grid=(N_CHIP // g, T // tt),
        in_specs=[pl.BlockSpec((tt, g * r), lambda s, t: (t, s)), pl.BlockSpec((tt, D), lambda s, t: (t, 0))]
        + [ANY] * len(extra),
        out_specs=pl.BlockSpec((g, r, D), lambda s, t: (s, 0, 0)), out_shape=_sds((N_CHIP, r, D), F32),
        compiler_params=_params())(act, dh, *extra)


def _mm_tn(a, b, tag):
    T, K = a.shape
    N = b.shape[1]
    tt = _tile(T, TN_TILE)

    def body(a_ref, b_ref, o_ref):
        part = lax.dot_general(a_ref[...], b_ref[...], TN, preferred_element_type=F32)
        _accumulate(o_ref, part, pl.program_id(0) == 0)

    return pl.pallas_call(
        body, name=f"mm_tn_{tag}", grid=(T // tt,),
        in_specs=[_rows(tt, K), _rows(tt, N)], out_specs=_whole((K, N)),
        out_shape=_sds((K, N), F32), compiler_params=_params())(a, b)


def _step_partial(ref, part):
    ref[...] = jnp.where(lax.broadcasted_iota(jnp.int32, ref.shape, 0) == 0, part, 0.0)


def _pack_small(partials, tail, at):
    n = len(partials)
    D = tail.shape[1]

    def body(*refs):
        o_ref = refs[-1]
        o_ref[...] = jnp.zeros_like(o_ref)
        for i in range(n):
            o_ref[i:i + 1, :] = jnp.sum(refs[i][...], axis=0, keepdims=True)
        o_ref[at:at + tail.shape[0], :] = refs[n][...]

    return pl.pallas_call(
        body, name="pack_small", grid=(1,),
        in_specs=[_whole(p.shape) for p in partials] + [_whole(tail.shape)], out_specs=_whole((SMALL_ROWS, D)),
        out_shape=_sds((SMALL_ROWS, D), F32), compiler_params=_params())(*partials, tail)


def _norm_mm_in(x, g, wg, tag, swiglu=False):
    T, D = x.shape
    n = wg.shape[-1]
    tm = _tile(T, FUSED_TILE)
    half = N_CHIP // 2

    def body(x_ref, g_ref, w_ref, xn_ref, y_ref, *rest):
        xv = x_ref[...]
        r = lax.rsqrt(jnp.mean(xv * xv, axis=-1, keepdims=True) + RMS_EPS)
        xn = (xv * r * g_ref[...]).astype(xn_ref.dtype)
        xn_ref[...] = xn

        def product(s):
            p = jnp.dot(xn, w_ref[s], preferred_element_type=F32)
            y_ref[:, s * n:(s + 1) * n] = p.astype(y_ref.dtype)
            return p

        if swiglu:
            for q in range(half):
                gate, up = product(q), product(half + q)
                rest[0][:, q * n:(q + 1) * n] = (gate * jax.nn.sigmoid(gate) * up).astype(rest[0].dtype)
        else:
            for s in range(N_CHIP):
                product(s)

    out_specs = [_rows(tm, D), _rows(tm, N_CHIP * n)]
    out_shape = [_sds((T, D), MM_DTYPE), _sds((T, N_CHIP * n), MM_DTYPE)]
    if swiglu:
        out_specs.append(_rows(tm, half * n))
        out_shape.append(_sds((T, half * n), MM_DTYPE))
    return pl.pallas_call(
        body, name=f"norm_mm_in_{tag}", grid=(T // tm,),
        in_specs=[_rows(tm, D), _whole((1, D)), _resident((N_CHIP, D, n))], out_specs=out_specs,
        out_shape=out_shape, compiler_params=_params())(x, g.reshape(1, D), wg)


def _mm_out_post(a, b, x, g, alpha, tag):
    T, K = a.shape
    D = b.shape[1]
    tm = _tile(T, FUSED_TILE)

    def body(a_ref, b_ref, x_ref, g_ref, h_ref, o_ref):
        hv = jnp.dot(a_ref[...], b_ref[...], preferred_element_type=F32)
        h_ref[...] = hv
        r = lax.rsqrt(jnp.mean(hv * hv, axis=-1, keepdims=True) + RMS_EPS)
        o_ref[...] = x_ref[...] + alpha * (hv * r * g_ref[...])

    return pl.pallas_call(
        body, name=f"mm_out_post_{tag}", grid=(T // tm,),
        in_specs=[_rows(tm, K), _resident((K, D)), _rows(tm, D), _whole((1, D))],
        out_specs=[_rows(tm, D), _rows(tm, D)], out_shape=[_sds((T, D), F32)] * 2,
        compiler_params=_params())(a, b, x, g.reshape(1, D))


def _post_bwd_mm(dx, h, g, alpha, b, tag, hgu=None):
    T, D = dx.shape
    K = b.shape[0]
    tm = _tile(T, FUSED_TILE)

    def body(dx_ref, h_ref, g_ref, b_ref, *rest):
        dh_ref, dg_ref, out_ref = rest[-3:]
        hv = h_ref[...]
        r = lax.rsqrt(jnp.mean(hv * hv, axis=-1, keepdims=True) + RMS_EPS)
        hh = hv * r
        dyn = alpha * dx_ref[...]
        _step_partial(dg_ref, jnp.sum(dyn * hh, axis=0, keepdims=True))
        dhh = dyn * g_ref[...]
        dh = (r * (dhh - hh * jnp.mean(dhh * hh, axis=-1, keepdims=True))).astype(dh_ref.dtype)
        dh_ref[...] = dh
        da = lax.dot_general(dh, b_ref[...], NT, preferred_element_type=F32)
        if hgu is None:
            out_ref[...] = da.astype(out_ref.dtype)
        else:
            gate = rest[0][:, :K]
            up = rest[0][:, K:]
            dab = da.astype(gate.dtype)
            sg = jax.nn.sigmoid(gate)
            out_ref[:, :K] = (dab * up * sg * (1.0 + gate * (1.0 - sg))).astype(out_ref.dtype)
            out_ref[:, K:] = (dab * gate * sg).astype(out_ref.dtype)

    in_specs = [_rows(tm, D), _rows(tm, D), _whole((1, D)), _resident((K, D))]
    args = [dx, h, g.reshape(1, D), b]
    wide = K
    if hgu is not None:
        wide = 2 * K
        in_specs.append(_rows(tm, wide))
        args.append(hgu)
    dh, dg, da = pl.pallas_call(
        body, name=f"post_bwd_mm_{tag}", grid=(T // tm,), in_specs=in_specs,
        out_specs=[_rows(tm, D), _rows(8, D), _rows(tm, wide)],
        out_shape=[_sds((T, D), MM_DTYPE), _sds((8 * (T // tm), D), F32), _sds((T, wide), MM_DTYPE)],
        compiler_params=_params())(*args)
    return dh, dg, da


def _mm_nt_pre(dy, w, dres, x, g, tag):
    T, C = dy.shape
    D = x.shape[1]
    tm = _tile(T, FUSED_TILE)
    n = w.shape[-1]

    def body(dy_ref, w_ref, dres_ref, x_ref, g_ref, dx_ref, dg_ref):
        if w.ndim == 2:
            dn = lax.dot_general(dy_ref[...], w_ref[...], NT, preferred_element_type=F32)
        else:
            dn = None
            for s in range(N_CHIP):
                part = lax.dot_general(dy_ref[:, s * n:(s + 1) * n], w_ref[s], NT, preferred_element_type=F32)
                dn = part if dn is None else dn + part
        xv = x_ref[...]
        r = lax.rsqrt(jnp.mean(xv * xv, axis=-1, keepdims=True) + RMS_EPS)
        xh = xv * r
        _step_partial(dg_ref, jnp.sum(dn * xh, axis=0, keepdims=True))
        dxh = dn * g_ref[...]
        dx_ref[...] = dres_ref[...] + r * (dxh - xh * jnp.mean(dxh * xh, axis=-1, keepdims=True))

    dx, dg = pl.pallas_call(
        body, name=f"mm_nt_pre_{tag}", grid=(T // tm,),
        in_specs=[_rows(tm, C), _resident(w.shape), _rows(tm, D), _rows(tm, D), _whole((1, D))],
        out_specs=[_rows(tm, D), _rows(8, D)], out_shape=[_sds((T, D), F32), _sds((8 * (T // tm), D), F32)],
        compiler_params=_params())(dy, w, dres, x, g.reshape(1, D))
    return dx, dg


def _adamw(w, g, m, v, tag, after=None):
    R, C = w.shape
    tr = _tile(R, ROW_TILE)
    extra = [] if after is None else [after]

    def body(w_ref, g_ref, m_ref, v_ref, *rest):
        go_ref, d_ref, mo_ref, vo_ref = rest[-4:]
        gv = g_ref[...]
        go_ref[...] = gv
        mn = ADAM_B1 * m_ref[...] + (1.0 - ADAM_B1) * gv
        vn = ADAM_B2 * v_ref[...] + (1.0 - ADAM_B2) * (gv * gv)
        m_hat = mn / (1.0 - ADAM_B1 ** ADAM_STEP)
        v_hat = vn / (1.0 - ADAM_B2 ** ADAM_STEP)
        d_ref[...] = -ADAM_LR * (m_hat / (jnp.sqrt(v_hat) + ADAM_EPS) + ADAM_WD * w_ref[...])
        mo_ref[...] = mn
        vo_ref[...] = vn

    return pl.pallas_call(
        body, name=f"adamw_{tag}", grid=(R // tr,),
        in_specs=[_rows(tr, C)] * 4 + [ANY] * len(extra), out_specs=[_rows(tr, C)] * 4,
        out_shape=[_sds((R, C), F32)] * 4, compiler_params=_params())(w, g, m, v, *extra)


def _sum_devices(gall, own, place):
    _, R, C = gall.shape

    def body(place_ref, g_ref, s_ref, o_ref):
        me = 2 * place_ref[1] + place_ref[0]
        acc = None
        for d in range(N_DEV):
            term = jnp.where(me == d, s_ref[...], g_ref[d])
            acc = term if acc is None else acc + term
        o_ref[...] = acc

    grid_spec = pltpu.PrefetchScalarGridSpec(
        num_scalar_prefetch=1, grid=(1,),
        in_specs=[pl.BlockSpec((N_DEV, R, C), lambda i, p: (0, 0, 0)), pl.BlockSpec((R, C), lambda i, p: (0, 0))],
        out_specs=pl.BlockSpec((R, C), lambda i, p: (0, 0)))
    return pl.pallas_call(
        body, name="sum_devices", grid_spec=grid_spec, out_shape=_sds((R, C), F32),
        compiler_params=_params())(place, gall, own)


HBM = pl.BlockSpec(memory_space=pltpu.HBM)
SEM = pl.BlockSpec(memory_space=pltpu.SEMAPHORE)
EFFECT = pltpu.SideEffectType.DATAFLOW_SIDE_EFFECTING


def _place():
    x, y, c = lax.axis_index("x"), lax.axis_index("y"), lax.axis_index("c")
    chips = ((1 - x, y), (x, 1 - y), (1 - x, 1 - y))
    return x, y, c, chips


def _remote(src, dst, send_sem, recv_sem, dev):
    return pltpu.make_async_remote_copy(src_ref=src, dst_ref=dst, send_sem=send_sem, recv_sem=recv_sem,
                                        device_id=dev, device_id_type=MESH)


def _in_hbm(a):
    return pltpu.with_memory_space_constraint(a, pltpu.HBM)


def _own_slot(w4, l, dtype, place, tag):
    _, _, r, col = w4.shape
    tr = _tile(r, 2 * ROW_TILE)

    def body(place_ref, x_ref, o_ref):
        o_ref[...] = x_ref[...].astype(o_ref.dtype)

    grid_spec = pltpu.PrefetchScalarGridSpec(
        num_scalar_prefetch=1, grid=(2, r // tr),
        in_specs=[pl.BlockSpec((None, None, tr, col), lambda h, i, p: (l, h, i, 0))],
        out_specs=pl.BlockSpec((None, None, tr, col), lambda h, i, p: (p[1], h, i, 0)))
    return pl.pallas_call(
        body, name=f"own_slot_{tag}", grid_spec=grid_spec, out_shape=_sds((N_CHIP, 2, r, col), dtype),
        compiler_params=_params())(place, w4)


def _gather_start(bufs, after, tag):
    n = len(bufs)

    def body(*refs):
        ins = refs[:n]
        s_sem, r_sem, token = refs[n + 1], refs[n + 2], refs[2 * n + 3]
        x, y, c, chips = _place()
        me = 2 * x + y
        for i in range(n):
            mine = ins[i].at[me, c]
            for j, (px, py) in enumerate(chips):
                _remote(mine, mine, s_sem.at[3 * i + j], r_sem.at[3 * i + j], (px, py, c)).start()
        token[...] = jnp.zeros_like(token)

    dma = pltpu.SemaphoreType.DMA
    res = pl.pallas_call(
        body, name=f"gather_start_{tag}", in_specs=[HBM] * n + [ANY],
        out_specs=[SEM, SEM] + [HBM] * n + [pl.BlockSpec(memory_space=pltpu.VMEM)],
        out_shape=[dma((3 * n,)), dma((3 * n,))] + [pltpu.HBM(b.shape, b.dtype) for b in bufs] + [_sds((8, LANES), F32)],
        input_output_aliases={i: i + 2 for i in range(n)},
        compiler_params=pltpu.CompilerParams(has_side_effects=EFFECT),
        )(*[_in_hbm(b) for b in bufs], after)
    return res[0], res[1], list(res[2:2 + n]), res[-1]


def _gather_pass(s_sem, r_sem, bufs, first, after, tag):
    n = len(bufs)

    def body(*refs):
        ins = refs[:n]
        a_s, a_r, b_s, b_r = refs[n], refs[n + 1], refs[n + 3], refs[n + 4]
        x, y, c, chips = _place()
        me = 2 * x + y
        sib = (x, y, 1 - c)
        for i in range(n):
            mine = ins[i].at[me, c]
            for j, (px, py) in enumerate(chips):
                k = 3 * (first + i) + j
                _remote(mine, mine, a_s.at[k], a_r.at[k], (px, py, c)).wait_send()
        for j, (px, py) in enumerate(chips):
            for i in range(n):
                k = 3 * (first + i) + j
                blk = ins[i].at[2 * px + py, c]
                _remote(blk, blk, a_s.at[k], a_r.at[k], (px, py, c)).wait_recv()
                _remote(blk, blk, b_s.at[3 * i + j], b_r.at[3 * i + j], sib).start()

    dma = pltpu.SemaphoreType.DMA
    res = pl.pallas_call(
        body, name=f"gather_pass_{tag}", in_specs=[HBM] * n + [SEM, SEM, ANY],
        out_specs=[SEM, SEM] + [HBM] * n,
        out_shape=[dma((3 * n,)), dma((3 * n,))] + [pltpu.HBM(b.shape, b.dtype) for b in bufs],
        input_output_aliases={i: i + 2 for i in range(n)},
        compiler_params=pltpu.CompilerParams(has_side_effects=EFFECT),
        )(*bufs, s_sem, r_sem, after)
    return res[0], res[1], list(res[2:])


def _gather_land(s_sem, r_sem, bufs, tag):
    n = len(bufs)

    def body(*refs):
        ins = refs[:n]
        b_s, b_r = refs[n], refs[n + 1]
        x, y, c, chips = _place()
        sib = (x, y, 1 - c)
        for j, (px, py) in enumerate(chips):
            for i in range(n):
                sent = ins[i].at[2 * px + py, c]
                got = ins[i].at[2 * px + py, 1 - c]
                _remote(sent, sent, b_s.at[3 * i + j], b_r.at[3 * i + j], sib).wait_send()
                _remote(got, got, b_s.at[3 * i + j], b_r.at[3 * i + j], sib).wait_recv()

    return list(pl.pallas_call(
        body, name=f"gather_land_{tag}", in_specs=[HBM] * n + [SEM, SEM], out_specs=[HBM] * n,
        out_shape=[pltpu.HBM(b.shape, b.dtype) for b in bufs],
        input_output_aliases={i: i for i in range(n)},
        compiler_params=pltpu.CompilerParams(has_side_effects=EFFECT),
        )(*bufs, s_sem, r_sem))


def _rs_pair_add(g, recv, place, tag):
    r, col = g.shape[-2:]
    tr = _tile(r, ROW_TILE)

    def body(place_ref, g_ref, r_ref, wire_ref, own_ref):
        tot = g_ref[...] + r_ref[...]
        wire_ref[...] = tot.astype(wire_ref.dtype)

        @pl.when(pl.program_id(1) == place_ref[1])
        def _():
            own_ref[...] = tot

    grid_spec = pltpu.PrefetchScalarGridSpec(
        num_scalar_prefetch=1, grid=(r // tr, N_CHIP),
        in_specs=[pl.BlockSpec((None, None, tr, col), lambda i, s, p: (s, p[0], i, 0)),
                  pl.BlockSpec((None, tr, col), lambda i, s, p: (s, i, 0))],
        out_specs=[pl.BlockSpec((None, tr, col), lambda i, s, p: (s, i, 0)),
                   pl.BlockSpec((tr, col), lambda i, s, p: (i, 0))])
    return pl.pallas_call(
        body, name=f"rs_pair_add_{tag}", grid_spec=grid_spec,
        out_shape=[_sds((N_CHIP, r, col), WIRE_DTYPE), _sds((r, col), F32)],
        compiler_params=_params())(place, g, recv)


def _pair_plan(srcs, lands):
    x, y, c, _ = _place()
    return [(s.at[:, 1 - c], l, (x, y, 1 - c)) for s, l in zip(srcs, lands)]


def _chip_plan(srcs, lands):
    x, y, c, chips = _place()
    plan = []
    for s, l in zip(srcs, lands):
        if len(s.shape) == 2:
            me = 4 * x + 2 * y + c
            plan += [(s, l.at[me], (x ^ (k >> 2), y ^ ((k >> 1) & 1), c ^ (k & 1))) for k in range(1, N_DEV)]
        else:
            plan += [(s.at[2 * px + py], l.at[j], (px, py, c)) for j, (px, py) in enumerate(chips)]
    return plan


def _exchange_start(srcs, lands, plan, count, tag):
    n = len(srcs)
    both = list(srcs) + list(lands)

    def body(*refs):
        s_sem, r_sem, token = refs[2 * n], refs[2 * n + 1], refs[4 * n + 2]
        for k, (src, dst, dev) in enumerate(plan(refs[:n], refs[n:2 * n])):
            _remote(src, dst, s_sem.at[k], r_sem.at[k], dev).start()
        token[...] = jnp.zeros_like(token)

    dma = pltpu.SemaphoreType.DMA
    res = pl.pallas_call(
        body, name=f"exchange_start_{tag}", in_specs=[HBM] * (2 * n),
        out_specs=[SEM, SEM] + [HBM] * (2 * n) + [pl.BlockSpec(memory_space=pltpu.VMEM)],
        out_shape=[dma((count,)), dma((count,))] + [pltpu.HBM(b.shape, b.dtype) for b in both] + [_sds((8, LANES), F32)],
        input_output_aliases={i: i + 2 for i in range(2 * n)},
        compiler_params=pltpu.CompilerParams(has_side_effects=EFFECT),
        )(*[_in_hbm(b) for b in both])
    return res[0], res[1], list(res[2:2 + n]), list(res[2 + n:2 + 2 * n]), res[-1]


def _exchange_wait(s_sem, r_sem, srcs, lands, plan, after, tag):
    n = len(srcs)

    def body(*refs):
        s_ref, r_ref = refs[2 * n], refs[2 * n + 1]
        for k, (src, dst, dev) in enumerate(plan(refs[:n], refs[n:2 * n])):
            cp = _remote(src, dst, s_ref.at[k], r_ref.at[k], dev)
            cp.wait_send()
            cp.wait_recv()

    both = list(srcs) + list(lands)
    res = pl.pallas_call(
        body, name=f"exchange_wait_{tag}", in_specs=[HBM] * (2 * n) + [SEM, SEM, ANY], out_specs=[HBM] * (2 * n),
        out_shape=[pltpu.HBM(b.shape, b.dtype) for b in both],
        input_output_aliases={i: i for i in range(2 * n)},
        compiler_params=pltpu.CompilerParams(has_side_effects=EFFECT),
        )(*both, s_sem, r_sem, after)
    return list(res[:n]), list(res[n:])


def _rs_chip_add(own, recv, place, l, L, prev, tag):
    r, col = own.shape
    tr = _tile(r, ROW_TILE)

    def body(place_ref, o_ref, r_ref, *rest):
        acc = o_ref[...]
        for j in range(3):
            acc = acc + r_ref[j].astype(F32)
        rest[-1][...] = acc

    in_specs = [pl.BlockSpec((tr, col), lambda i, p: (i, 0)), pl.BlockSpec((3, tr, col), lambda i, p: (0, i, 0))]
    args = [place, own, recv]
    kw = {}
    if prev is not None:
        in_specs.append(ANY)
        args.append(prev)
        kw["input_output_aliases"] = {3: 0}
    grid_spec = pltpu.PrefetchScalarGridSpec(
        num_scalar_prefetch=1, grid=(r // tr,), in_specs=in_specs,
        out_specs=pl.BlockSpec((None, None, tr, col), lambda i, p: (l, p[0], i, 0)))
    return pl.pallas_call(
        body, name=f"rs_chip_add_{tag}", grid_spec=grid_spec, out_shape=_sds((L, 2, r, col), F32),
        compiler_params=_params(), **kw)(*args)


def _rs_pair_share(fulls, tag):
    n = len(fulls)

    def body(*refs):
        outs = refs[n:2 * n]
        s_sem, r_sem = refs[2 * n:]
        x, y, c, _ = _place()
        sib = (x, y, 1 - c)
        started = []
        for i in range(n):
            cp = _remote(outs[i].at[:, c], outs[i].at[:, c], s_sem.at[i], r_sem.at[i], sib)
            cp.start()
            started.append(cp)
        for i, cp in enumerate(started):
            cp.wait_send()
            _remote(outs[i].at[:, 1 - c], outs[i].at[:, 1 - c], s_sem.at[i], r_sem.at[i], sib).wait_recv()

    dma = pltpu.SemaphoreType.DMA
    return pl.pallas_call(
        body, name=f"rs_pair_share_{tag}", in_specs=[ANY] * n, out_specs=[ANY] * n,
        out_shape=[_sds(f.shape, f.dtype) for f in fulls],
        input_output_aliases={i: i for i in range(n)},
        scratch_shapes=[dma((n,)), dma((n,))],
        )(*fulls)


def _ffn_fwd(x, g_pre, g_post, w_in, w_out, tag):
    xn, hgu, act = _norm_mm_in(x, g_pre, w_in, tag, swiglu=True)
    if callable(w_out):
        w_out = w_out(act)
    h, x_out = _mm_out_post(act, w_out.reshape(-1, w_out.shape[-1]), x, g_post, 0.5, tag)
    return x_out, (x, xn, hgu, act, h)


def _ffn_bwd(dx, saved, g_pre, g_post, w_in, w_out, tag, between=None, finish=None):
    x, xn, hgu, act, h = saved
    dh, dg_post, dhgu = _post_bwd_mm(dx, h, g_post, 0.5, w_out.reshape(-1, w_out.shape[-1]), tag, hgu=hgu)
    token = None
    if between is not None:
        token = between(dhgu)
    dw_out = _mm_tn_out(act, dh, tag, after=token)
    dw_in = _mm_tn_in(xn, dhgu, tag, after=token)
    if finish is not None:
        token = finish(dw_in, dw_out)
    if token is not None:
        g_pre = g_pre + token[0, :1]
    dx_in, dg_pre = _mm_nt_pre(dhgu, w_in, dx, x, g_pre, tag)
    return dx_in, dg_pre, dg_post, dw_in, dw_out


def kernel(x, ffn1_pre_g, ffn1_post_g, ffn1_w_in, ffn1_w_out, mix_pre_g, mix_post_g, ffn2_pre_g, ffn2_post_g, ffn2_w_in, ffn2_w_out, conv_w_in, conv_k, conv_w_out, kv_g, kv_w, forget_b, attn_w_qg, attn_w_o, loss_target, m_ffn1_pre_g, m_ffn1_post_g, m_ffn1_w_in, m_ffn1_w_out, m_mix_pre_g, m_mix_post_g, m_ffn2_pre_g, m_ffn2_post_g, m_ffn2_w_in, m_ffn2_w_out, m_conv_w_in, m_conv_k, m_conv_w_out, m_kv_g, m_kv_w, m_forget_b, m_attn_w_qg, m_attn_w_o, v_ffn1_pre_g, v_ffn1_post_g, v_ffn1_w_in, v_ffn1_w_out, v_mix_pre_g, v_mix_post_g, v_ffn2_pre_g, v_ffn2_post_g, v_ffn2_w_in, v_ffn2_w_out, v_conv_w_in, v_conv_k, v_conv_w_out, v_kv_g, v_kv_w, v_forget_b, v_attn_w_qg, v_attn_w_o):
    Bl, S, D = x.shape
    T = Bl * S
    H = forget_b.shape[0]
    assert D == H * HEAD_DIM and D % LANES == 0
    kvc = kv_w.shape[1]
    kvp = -(-kvc // LANES) * LANES
    kv_all = 2 * D + LANES
    dk_cols = conv_k.shape[2]
    chip = 2 * lax.axis_index("x") + lax.axis_index("y")
    core = lax.axis_index("c")

    given = dict(ffn1_w_in=(ffn1_w_in, m_ffn1_w_in, v_ffn1_w_in), ffn1_w_out=(ffn1_w_out, m_ffn1_w_out, v_ffn1_w_out),
                 ffn2_w_in=(ffn2_w_in, m_ffn2_w_in, v_ffn2_w_in), ffn2_w_out=(ffn2_w_out, m_ffn2_w_out, v_ffn2_w_out),
                 conv_w_in=(conv_w_in, m_conv_w_in, v_conv_w_in), conv_w_out=(conv_w_out, m_conv_w_out, v_conv_w_out),
                 kv_w=(kv_w, m_kv_w, v_kv_w), attn_w_qg=(attn_w_qg, m_attn_w_qg, v_attn_w_qg),
                 attn_w_o=(attn_w_o, m_attn_w_o, v_attn_w_o))
    shards = {k: w for k, (w, _, _) in given.items()}
    shards["kv_w"] = jnp.pad(kv_w, ((0, 0), (0, kvp - kvc)))[None]
    groups = [[("ffn1_w_in", 0), ("ffn1_w_out", 0)], [("conv_w_in", 0), ("conv_w_out", 0)],
              [("ffn2_w_in", 0), ("ffn2_w_out", 0)], [("kv_w", 0), ("ffn1_w_in", 1), ("ffn1_w_out", 1)],
              [("attn_w_qg", 0), ("attn_w_o", 0), ("ffn2_w_in", 1), ("ffn2_w_out", 1)]]
    second = groups[3] + groups[4]
    place = jnp.stack([core, chip]).astype(jnp.int32)

    def slot(key, where):
        w = shards[key[0]]
        L, r, col = w.shape
        return _own_slot(w.reshape(L, 2, r // 2, col), key[1], MM_DTYPE, where, f"{key[0]}{key[1]}")

    def whole(g):
        return g.reshape(N_CHIP, -1, g.shape[-1])

    taps_slot = _own_slot(jnp.pad(conv_k[0], ((0, 13), (0, 0))).reshape(1, 2, 8, dk_cols), 0, F32, place, "conv_k")
    fb = jnp.pad(forget_b, (0, LANES - H)).reshape(1, LANES)
    w_in0, w_out0 = groups[0]
    s_0, r_0, fly_0, token = _gather_start([slot(w_in0, place), taps_slot, slot(w_out0, place)], fb, "first")
    later = groups[1] + groups[2] + groups[3] + groups[4]
    s_1, r_1, fly_1, token = _gather_start([slot(key, place) for key in later], token, "rest")
    W = {}

    def land(sems, bufs, lo, after, tag):
        return _gather_land(*_gather_pass(*sems, bufs, lo, after, tag), tag)

    def arrive(g, after):
        lo = sum(len(groups[k]) for k in range(1, g))
        got = land((s_1, r_1), fly_1[lo:lo + len(groups[g])], lo, after, f"g{g}")
        W.update({key: whole(b) for key, b in zip(groups[g], got)})

    w_first, taps = land((s_0, r_0), fly_0[:2], 0, token, "g0")
    k_taps = taps.reshape(N_CHIP, 16, dk_cols).transpose(1, 0, 2).reshape(16, D)[:8]

    x0 = x.reshape(T, D)
    W[w_in0] = whole(w_first)

    def first_w_out(act):
        W[w_out0] = whole(land((s_0, r_0), fly_0[2:], 2, act, "g0_out")[0])
        return W[w_out0]

    x1, s_f1a = _ffn_fwd(x0, ffn1_pre_g[0], ffn1_post_g[0], W[w_in0], first_w_out, "l0f1")
    arrive(1, x1)
    w_o_conv = W["conv_w_out", 0].reshape(D, D)
    xn_c, bch = _norm_mm_in(x1, mix_pre_g[0], W["conv_w_in", 0], "conv")
    z_c = _conv_fwd(bch, k_taps, Bl, S)
    m_c, x2 = _mm_out_post(z_c, w_o_conv, x1, mix_post_g[0], 1.0, "conv_out")
    arrive(2, x2)
    x3, s_f2a = _ffn_fwd(x2, ffn2_pre_g[0], ffn2_post_g[0], W["ffn2_w_in", 0], W["ffn2_w_out", 0], "l0f2")

    arrive(3, x3)
    kv_full = jnp.concatenate([W["kv_w", 0][s, :, :kvc] for s in range(N_CHIP)], axis=1)
    kv_full = jnp.pad(kv_full, ((0, 0), (0, kv_all - kv_full.shape[1])))
    xn_kv = _rms_fwd(x3, kv_g, "kv")
    kvact = _mm_nn(xn_kv, kv_full[:, :2 * D], MM_DTYPE, "kv")
    pf = _mm_nn(xn_kv, kv_full[:, 2 * D:], F32, "forget")
    cum = _forget_fwd(pf, fb, Bl, S)
    bq = min(S, ATT_BLOCK)
    c3 = cum.reshape(Bl, S, LANES)[:, :, :H].transpose(0, 2, 1)
    c_col = jnp.broadcast_to(c3[..., None], (Bl, H, S, LANES))
    c_row = c3.reshape(Bl, H, S // bq, 1, bq)

    x4, s_f1b = _ffn_fwd(x3, ffn1_pre_g[1], ffn1_post_g[1], W["ffn1_w_in", 1], W["ffn1_w_out", 1], "l1f1")
    arrive(4, x4)
    w_o_attn = W["attn_w_o", 0].reshape(D, D)
    xn_a, qg = _norm_mm_in(x4, mix_pre_g[1], W["attn_w_qg", 0], "qg")
    o, lse, z_a = _attn_fwd(qg, kvact, c_col, c_row, Bl, S, D)
    m_a, x5 = _mm_out_post(z_a, w_o_attn, x4, mix_post_g[1], 1.0, "attn_out")
    x6, s_f2b = _ffn_fwd(x5, ffn2_pre_g[1], ffn2_post_g[1], W["ffn2_w_in", 1], W["ffn2_w_out", 1], "l1f2")

    dy, loss_local = _loss_grad(x6, loss_target.reshape(T, D))

    G = {}
    dx5, dg_f2pre_1, dg_f2post_1, G["ffn2_w_in", 1], G["ffn2_w_out", 1] = _ffn_bwd(
        dy, s_f2b, ffn2_pre_g[1], ffn2_post_g[1], W["ffn2_w_in", 1], W["ffn2_w_out", 1], "l1f2")
    dm_a, dg_mixpost_1, dz_a = _post_bwd_mm(dx5, m_a, mix_post_g[1], 1.0, w_o_attn, "attn_out")
    G["attn_w_o", 0] = _mm_tn_out(z_a, dm_a, "attn_out")
    dq, dk, dv, dcr = _attn_bwd(qg, kvact, dz_a, lse, c_col, c_row, Bl, S, D)
    dqg = _gate_bwd(dz_a, qg, o, dq)
    G["attn_w_qg", 0] = _mm_tn_in(xn_a, dqg, "qg")
    dx4, dg_mixpre_1 = _mm_nt_pre(dqg, W["attn_w_qg", 0], dx5, x4, mix_pre_g[1], "qg")
    dx3, dg_f1pre_1, dg_f1post_1, G["ffn1_w_in", 1], G["ffn1_w_out", 1] = _ffn_bwd(
        dx4, s_f1b, ffn1_pre_g[1], ffn1_post_g[1], W["ffn1_w_in", 1], W["ffn1_w_out", 1], "l1f1")

    dcum = jnp.pad(dcr.reshape(Bl, H, S).transpose(0, 2, 1), ((0, 0), (0, 0), (0, LANES - H))).reshape(T, LANES)
    dpf, dfb = _forget_bwd(dcum, pf, fb, Bl, S)
    dp = _pack_dkv(dk, dv, dpf)
    G_kv_full = _mm_tn(xn_kv, dp, "kv")
    G["kv_w", 0] = jnp.stack([jnp.pad(G_kv_full[:, s * kvc:(s + 1) * kvc], ((0, 0), (0, kvp - kvc))) for s in range(N_CHIP)])
    dx3, dg_kv = _mm_nt_pre(dp, kv_full, dx3, x3, kv_g, "kv")

    def halves_of(keys):
        return [G[k].reshape(N_CHIP, 2, G[k].shape[1] // 2, G[k].shape[2]) for k in keys]

    def pair_adds(keys, grads, recvs):
        wires, owns = [], []
        for k, g, r in zip(keys, grads, recvs):
            w, own = _rs_pair_add(g, r, place, f"{k[0]}{k[1]}")
            wires.append(w)
            owns.append(own)
        return wires, owns

    def chip_start(wires, tag, extra=()):
        lands = [lax.empty((3,) + w.shape[1:], w.dtype) for w in wires]
        lands += [jnp.zeros((N_DEV,) + e.shape, e.dtype) for e in extra]
        return _exchange_start(list(wires) + list(extra), lands, _chip_plan, 3 * len(wires) + (N_DEV - 1) * len(extra), tag)

    late = groups[2] + groups[1]
    last = groups[0]
    grads_2 = halves_of(second)
    p_sems, p_semr, grads_2, sib_2, token = _exchange_start(
        grads_2, [lax.empty((N_CHIP,) + g.shape[2:], g.dtype) for g in grads_2], _pair_plan, len(grads_2), "pair_second")

    dx2, dg_f2pre_0, dg_f2post_0, G["ffn2_w_in", 0], G["ffn2_w_out", 0] = _ffn_bwd(
        dx3, s_f2a, ffn2_pre_g[0], ffn2_post_g[0] + token[0, :1], W["ffn2_w_in", 0], W["ffn2_w_out", 0], "l0f2")
    grads_2, sib_2 = _exchange_wait(p_sems, p_semr, grads_2, sib_2, _pair_plan, dx2, "pair_second")
    wires_2, owns_2 = pair_adds(second, grads_2, sib_2)
    c_2 = chip_start(wires_2, "chip_second")
    dm_c, dg_mixpost_0, dz_c = _post_bwd_mm(dx2, m_c, mix_post_g[0] + c_2[4][0, :1], 1.0, w_o_conv, "conv_out")
    G["conv_w_out", 0] = _mm_tn_out(z_c, dm_c, "conv_out")
    db, dcg, dhh, dk_taps = _conv_bwd(bch, dz_c, k_taps, Bl, S)
    dbch = jnp.concatenate([db, dcg, dhh], axis=1)
    G["conv_w_in", 0] = _mm_tn_in(xn_c, dbch, "conv")
    dx1, dg_mixpre_0 = _mm_nt_pre(dbch, W["conv_w_in", 0], dx2, x1, mix_pre_g[0], "conv")
    def pair_start(keys, tag):
        grads = halves_of(keys)
        lands = [lax.empty((N_CHIP,) + g.shape[2:], g.dtype) for g in grads]
        return _exchange_start(grads, lands, _pair_plan, len(grads), tag)

    p_l = pair_start(late, "pair_late")
    late_done = {}

    def late_leg(dhgu):
        grads_l, sib_l = _exchange_wait(*p_l[:4], _pair_plan, dhgu, "pair_late")
        late_done["wires"], late_done["owns"] = pair_adds(late, grads_l, sib_l)
        late_done["chip"] = chip_start(late_done["wires"], "chip_late")
        return late_done["chip"][4]

    def last_pair(dw_in, dw_out):
        G["ffn1_w_in", 0], G["ffn1_w_out", 0] = dw_in, dw_out
        late_done["pair"] = pair_start(last, "pair_last")
        return late_done["pair"][4]

    dx0, dg_f1pre_0, dg_f1post_0, _, _ = _ffn_bwd(
        dx1, s_f1a, ffn1_pre_g[0], ffn1_post_g[0] + p_l[4][0, :1], W["ffn1_w_in", 0], W["ffn1_w_out", 0], "l0f1",
        between=late_leg, finish=last_pair)
    grad_x = dx0.reshape(Bl, S, D)
    owns_l, c_l, p_1 = late_done["owns"], late_done["chip"], late_done["pair"]

    grads_1, sib_1 = _exchange_wait(*p_1[:4], _pair_plan, dx0, "pair_last")
    wires_1, owns_1 = pair_adds(last, grads_1, sib_1)

    def row(v):
        return jnp.pad(v.reshape(-1), (0, D - v.size)).reshape(1, D)

    gain_parts = [dg_f1pre_0, dg_f1pre_1, dg_f1post_0, dg_f1post_1, dg_mixpre_0, dg_mixpre_1, dg_mixpost_0, dg_mixpost_1,
                  dg_f2pre_0, dg_f2pre_1, dg_f2post_0, dg_f2post_1, dg_kv]
    tail = jnp.concatenate([row(dfb[0, :H]), dk_taps[:3], jnp.full((1, D), loss_local), jnp.zeros((3, D), F32)], axis=0)
    small = _pack_small(gain_parts, tail, 16)
    c_1 = chip_start(wires_1, "chip_last", extra=[small])
    _, recvs_2 = _exchange_wait(*c_2[:4], _chip_plan, c_1[4], "chip_second")
    _, recvs_l = _exchange_wait(*c_l[:4], _chip_plan, c_1[4], "chip_late")
    partial = {}

    def chip_adds(keys, owns, recvs):
        for (name, l), own, rcv in zip(keys, owns, recvs):
            partial[name] = _rs_chip_add(own, rcv, place, l, shards[name].shape[0], partial.get(name), f"{name}{l}")

    chip_adds(late + second, owns_l + owns_2, recvs_l + recvs_2)
    res = {}

    def adamw(names, reduced, after):
        for k, red in zip(names, reduced):
            w, m, v = given[k]
            g2 = red.reshape(-1, red.shape[-1])
            if k == "kv_w":
                g2 = g2[:, :kvc]
            flat = lambda a: a.reshape(-1, a.shape[-1])
            go, d, mn, vn = _adamw(flat(w), g2, flat(m), flat(v), k, after=after)
            res[k] = tuple(a.reshape(w.shape) for a in (go, d, mn, vn))
        return d

    early = [k for k in partial if (k, 0) not in last]
    done = adamw(early, _rs_pair_share([partial[k] for k in early], "early"), c_1[4])
    _, recvs_1 = _exchange_wait(*c_1[:4], _chip_plan, done, "chip_last")
    chip_adds(last, owns_1, recvs_1[:-1])
    rest = [k for k, _ in last]
    adamw(rest, _rs_pair_share([partial[k] for k in rest], "last"), None)
    gsum = _sum_devices(recvs_1[-1], small, place)
    loss = gsum[20, 0]

    small_names = ["ffn1_pre_g", "ffn1_post_g", "mix_pre_g", "mix_post_g", "ffn2_pre_g", "ffn2_post_g"]
    small_given = dict(ffn1_pre_g=(ffn1_pre_g, m_ffn1_pre_g, v_ffn1_pre_g), ffn1_post_g=(ffn1_post_g, m_ffn1_post_g, v_ffn1_post_g),
                       mix_pre_g=(mix_pre_g, m_mix_pre_g, v_mix_pre_g), mix_post_g=(mix_post_g, m_mix_post_g, v_mix_post_g),
                       ffn2_pre_g=(ffn2_pre_g, m_ffn2_pre_g, v_ffn2_pre_g), ffn2_post_g=(ffn2_post_g, m_ffn2_post_g, v_ffn2_post_g))

    def pack(idx):
        rows_ = [small_given[k][idx] for k in small_names]
        rows_ += [row((kv_g, m_kv_g, v_kv_g)[idx]), jnp.zeros((3, D), F32), row((forget_b, m_forget_b, v_forget_b)[idx])]
        rows_.append(jnp.pad((conv_k, m_conv_k, v_conv_k)[idx][0], ((0, 0), (0, D - dk_cols))))
        a = jnp.concatenate(rows_, axis=0)
        return jnp.pad(a, ((0, SMALL_ROWS - a.shape[0]), (0, 0)))

    g_taps = lax.dynamic_slice_in_dim(gsum[17:20], chip * dk_cols, dk_cols, axis=1)
    g_small = jnp.concatenate([gsum[:17], jnp.pad(g_taps, ((0, 0), (0, D - dk_cols))), gsum[20:]], axis=0)
    g_small, d_s, m_s, v_s = _adamw(pack(0), g_small, pack(1), pack(2), "small")
    for i, k in enumerate(small_names):
        res[k] = tuple(a[2 * i:2 * i + 2] for a in (g_small, d_s, m_s, v_s))
    res["kv_g"] = tuple(a[12] for a in (g_small, d_s, m_s, v_s))
    res["forget_b"] = tuple(a[16, :H] for a in (g_small, d_s, m_s, v_s))
    res["conv_k"] = tuple(a[17:20, :dk_cols][None] for a in (g_small, d_s, m_s, v_s))

    order = ["ffn1_pre_g", "ffn1_post_g", "ffn1_w_in", "ffn1_w_out", "mix_pre_g", "mix_post_g", "ffn2_pre_g", "ffn2_post_g",
             "ffn2_w_in", "ffn2_w_out", "conv_w_in", "conv_k", "conv_w_out", "kv_g", "kv_w", "forget_b", "attn_w_qg", "attn_w_o"]
    out = [loss, grad_x]
    for idx in range(4):
        out += [res[k][idx] for k in order]
    return tuple(out)
```

```python
import math

import jax
import jax.numpy as jnp
from jax import lax
from jax.experimental import pallas as pl
from jax.experimental.pallas import tpu as pltpu

F32 = jnp.float32
MM_DTYPE = jnp.bfloat16
WIRE_DTYPE = jnp.bfloat16

RMS_EPS = 1e-6
ADAM_LR = 0.001
ADAM_B1 = 0.9
ADAM_B2 = 0.999
ADAM_EPS = 1e-08
ADAM_WD = 0.01
ADAM_STEP = 10

HEAD_DIM = 64
LANES = 128
N_CHIP = 4
N_DEV = 8
ROW_TILE = 512
MM_TILE = 512
FUSED_TILE = 512
TN_TILE = 2048
WG_TILE = 4096
ATT_BLOCK = 512
SMALL_ROWS = 24
V7X_VMEM_BYTES = 64 * 1024 * 1024
VMEM_LIMIT = V7X_VMEM_BYTES - 8 * 1024 * 1024
MESH = pl.DeviceIdType.MESH
ANY = pl.BlockSpec(memory_space=pl.ANY)

NT = (((1,), (1,)), ((), ()))
TN = (((0,), (0,)), ((), ()))


def _tile(n, pref):
    if n <= pref:
        return n
    t = pref - pref % 16
    while n % t:
        t -= 16
    return t


def _params():
    return pltpu.CompilerParams(vmem_limit_bytes=VMEM_LIMIT)


def _sds(shape, dtype):
    return jax.ShapeDtypeStruct(shape, dtype)


def _rows(tm, c):
    return pl.BlockSpec((tm, c), lambda i: (i, 0))


def _whole(shape):
    return pl.BlockSpec(shape, lambda *_: (0,) * len(shape))


def _resident(shape):
    return pl.BlockSpec(shape, lambda *_: (0,) * len(shape), pipeline_mode=pl.Buffered(1))


def _rms_fwd(x, g, tag):
    T, D = x.shape
    tm = _tile(T, ROW_TILE)

    def body(x_ref, g_ref, o_ref):
        xv = x_ref[...]
        r = lax.rsqrt(jnp.mean(xv * xv, axis=-1, keepdims=True) + RMS_EPS)
        o_ref[...] = (xv * r * g_ref[...]).astype(o_ref.dtype)

    return pl.pallas_call(
        body, name=f"rms_fwd_{tag}", grid=(T // tm,),
        in_specs=[_rows(tm, D), _whole((1, D))], out_specs=_rows(tm, D),
        out_shape=_sds((T, D), MM_DTYPE), compiler_params=_params())(x, g.reshape(1, D))


def _accumulate(ref, part, first):
    @pl.when(first)
    def _():
        ref[...] = part

    @pl.when(jnp.logical_not(first))
    def _():
        ref[...] += part


def _loss_grad(y, tgt):
    T, D = y.shape
    tm = _tile(T, ROW_TILE)

    def body(y_ref, t_ref, dy_ref, l_ref):
        e = y_ref[...] - t_ref[...]
        row = jnp.mean(e * e, axis=-1, keepdims=True)
        part = jnp.broadcast_to(jnp.sum(row, axis=0, keepdims=True), (8, LANES))
        _accumulate(l_ref, part, pl.program_id(0) == 0)
        dy_ref[...] = e * (1.0 / D)

    dy, lsum = pl.pallas_call(
        body, name="loss_grad", grid=(T // tm,),
        in_specs=[_rows(tm, D), _rows(tm, D)], out_specs=[_rows(tm, D), _whole((8, LANES))],
        out_shape=[_sds((T, D), F32), _sds((8, LANES), F32)], compiler_params=_params())(y, tgt)
    return dy, 0.5 * lsum[0, 0]


def _shift_down(u, d, rows):
    return jnp.where(rows >= d, pltpu.roll(u, d, 0), 0.0)


def _shift_up(u, d, rows, S):
    return jnp.where(rows < S - d, pltpu.roll(u, S - d, 0), 0.0)


def _conv_fwd(bch, k8, Bl, S):
    T, D3 = bch.shape
    D = D3 // 3
    dc = min(D, 2 * LANES)
    nd = D // dc

    def body(b_ref, c_ref, h_ref, k_ref, z_ref):
        rows = lax.broadcasted_iota(jnp.int32, (S, 1), 0)
        u = c_ref[...].astype(F32) * h_ref[...].astype(F32)
        y = k_ref[2:3, :] * u + k_ref[1:2, :] * _shift_down(u, 1, rows) + k_ref[0:1, :] * _shift_down(u, 2, rows)
        z_ref[...] = (b_ref[...].astype(F32) * y).astype(z_ref.dtype)

    return pl.pallas_call(
        body, name="conv_fwd", grid=(Bl, nd),
        in_specs=[pl.BlockSpec((S, dc), lambda b, j: (b, j)),
                  pl.BlockSpec((S, dc), lambda b, j: (b, nd + j)),
                  pl.BlockSpec((S, dc), lambda b, j: (b, 2 * nd + j)),
                  pl.BlockSpec((8, dc), lambda b, j: (0, j))],
        out_specs=pl.BlockSpec((S, dc), lambda b, j: (b, j)),
        out_shape=_sds((T, D), MM_DTYPE), compiler_params=_params())(bch, bch, bch, k8)


def _conv_bwd(bch, dz, k8, Bl, S):
    T, D3 = bch.shape
    D = D3 // 3
    dc = min(D, 2 * LANES)
    nd = D // dc

    def body(b_ref, c_ref, h_ref, dz_ref, k_ref, db_ref, dc_ref, dh_ref, dk_ref):
        rows = lax.broadcasted_iota(jnp.int32, (S, 1), 0)
        bv = b_ref[...].astype(F32)
        cv = c_ref[...].astype(F32)
        hv = h_ref[...].astype(F32)
        dzv = dz_ref[...].astype(F32)
        u = cv * hv
        u1 = _shift_down(u, 1, rows)
        u2 = _shift_down(u, 2, rows)
        y = k_ref[2:3, :] * u + k_ref[1:2, :] * u1 + k_ref[0:1, :] * u2
        db_ref[...] = (dzv * y).astype(db_ref.dtype)
        dy = dzv * bv
        du = k_ref[2:3, :] * dy + k_ref[1:2, :] * _shift_up(dy, 1, rows, S) + k_ref[0:1, :] * _shift_up(dy, 2, rows, S)
        dc_ref[...] = (du * hv).astype(dc_ref.dtype)
        dh_ref[...] = (du * cv).astype(dh_ref.dtype)

        @pl.when(pl.program_id(1) == 0)
        def _():
            dk_ref[...] = jnp.zeros_like(dk_ref)

        dk_ref[0:1, :] += jnp.sum(dy * u2, axis=0, keepdims=True)
        dk_ref[1:2, :] += jnp.sum(dy * u1, axis=0, keepdims=True)
        dk_ref[2:3, :] += jnp.sum(dy * u, axis=0, keepdims=True)

    seq = lambda off: pl.BlockSpec((S, dc), lambda j, b: (b, off + j))
    return pl.pallas_call(
        body, name="conv_bwd", grid=(nd, Bl),
        in_specs=[seq(0), seq(nd), seq(2 * nd), seq(0), pl.BlockSpec((8, dc), lambda j, b: (0, j))],
        out_specs=[seq(0), seq(0), seq(0), pl.BlockSpec((8, dc), lambda j, b: (0, j))],
        out_shape=[_sds((T, D), MM_DTYPE)] * 3 + [_sds((8, D), F32)],
        compiler_params=_params())(bch, bch, bch, dz, k8)


def _forget_fwd(pf, fb, Bl, S):
    T = pf.shape[0]

    def body(p_ref, fb_ref, c_ref):
        rows = lax.broadcasted_iota(jnp.int32, (S, 1), 0)
        z = p_ref[...] + fb_ref[...]
        acc = jnp.minimum(z, 0.0) - jnp.log1p(jnp.exp(-jnp.abs(z)))
        d = 1
        while d < S:
            acc = acc + _shift_down(acc, d, rows)
            d *= 2
        c_ref[...] = acc

    return pl.pallas_call(
        body, name="forget_fwd", grid=(Bl,),
        in_specs=[_rows(S, LANES), _whole((1, LANES))], out_specs=_rows(S, LANES),
        out_shape=_sds((T, LANES), F32), compiler_params=_params())(pf, fb)


def _forget_bwd(dc, pf, fb, Bl, S):
    T = pf.shape[0]

    def body(dc_ref, p_ref, fb_ref, df_ref, dfb_ref):
        rows = lax.broadcasted_iota(jnp.int32, (S, 1), 0)
        acc = dc_ref[...]
        d = 1
        while d < S:
            acc = acc + _shift_up(acc, d, rows, S)
            d *= 2
        df = acc * jax.nn.sigmoid(-(p_ref[...] + fb_ref[...]))
        df_ref[...] = df.astype(df_ref.dtype)
        _accumulate(dfb_ref, jnp.sum(df, axis=0, keepdims=True), pl.program_id(0) == 0)

    return pl.pallas_call(
        body, name="forget_bwd", grid=(Bl,),
        in_specs=[_rows(S, LANES), _rows(S, LANES), _whole((1, LANES))],
        out_specs=[_rows(S, LANES), _whole((1, LANES))],
        out_shape=[_sds((T, LANES), MM_DTYPE), _sds((1, LANES), F32)],
        compiler_params=_params())(dc, pf, fb)


def _head_mask(h):
    lane = lax.broadcasted_iota(jnp.int32, (1, LANES), 1)
    return (lane >= h * HEAD_DIM) & (lane < (h + 1) * HEAD_DIM)


def _attn_fwd(qg, kv, c_col, c_row, Bl, S, D):
    T = Bl * S
    H = D // HEAD_DIM
    HP = D // LANES
    bq = min(S, ATT_BLOCK)
    nq = S // bq
    scale = 1.0 / math.sqrt(HEAD_DIM)

    def body(q_ref, g_ref, k_ref, v_ref, cc_ref, cr_ref, o_ref, lse_ref, z_ref):
        i = pl.program_id(2)
        q2 = q_ref[...]
        qh = [q2 * (_head_mask(h).astype(F32) * scale).astype(q2.dtype) for h in range(2)]
        cc = [cc_ref[h][:, :1] for h in range(2)]
        diag = lax.broadcasted_iota(jnp.int32, (1, bq), 1) <= lax.broadcasted_iota(jnp.int32, (bq, 1), 0)

        def block(j, carry, on_diagonal):
            off = pl.multiple_of(j * bq, bq)
            kj = k_ref[pl.ds(off, bq), :]
            vj = v_ref[pl.ds(off, bq), :]
            new = []
            for h in range(2):
                m, l, acc = carry[h]
                s = lax.dot_general(qh[h], kj, NT, preferred_element_type=F32) + cc[h] - cr_ref[h, j]
                if on_diagonal:
                    s = jnp.where(diag, s, -jnp.inf)
                m_new = jnp.maximum(m, jnp.max(s, axis=1, keepdims=True))
                p = jnp.exp(s - m_new)
                a = jnp.exp(m - m_new)
                l = a * l + jnp.sum(p, axis=1, keepdims=True)
                acc = a * acc + jnp.dot(p.astype(MM_DTYPE), vj, preferred_element_type=F32)
                new.append((m_new, l, acc))
            return tuple(new)

        one = (jnp.full((bq, 1), -jnp.inf, F32), jnp.zeros((bq, 1), F32), jnp.zeros((bq, LANES), F32))
        carry = lax.fori_loop(0, i, lambda j, c: block(j, c, False), (one, one))
        carry = block(i, carry, True)
        outs = []
        for h in range(2):
            m, l, acc = carry[h]
            outs.append(acc / l)
            lse_ref[h] = jnp.broadcast_to(m + jnp.log(l), (bq, LANES))
        o2 = jnp.where(_head_mask(0), outs[0], outs[1])
        o_ref[...] = o2
        z_ref[...] = (jax.nn.sigmoid(g_ref[...].astype(F32)) * o2).astype(z_ref.dtype)

    return pl.pallas_call(
        body, name="attn_fwd", grid=(Bl, HP, nq),
        in_specs=[pl.BlockSpec((bq, LANES), lambda b, hp, i: (b * nq + i, hp)),
                  pl.BlockSpec((bq, LANES), lambda b, hp, i: (b * nq + i, HP + hp)),
                  pl.BlockSpec((S, LANES), lambda b, hp, i: (b, hp)),
                  pl.BlockSpec((S, LANES), lambda b, hp, i: (b, HP + hp)),
                  pl.BlockSpec((None, 2, bq, LANES), lambda b, hp, i: (b, hp, i, 0)),
                  pl.BlockSpec((None, 2, nq, 1, bq), lambda b, hp, i: (b, hp, 0, 0, 0))],
        out_specs=[pl.BlockSpec((bq, LANES), lambda b, hp, i: (b * nq + i, hp)),
                   pl.BlockSpec((None, 2, bq, LANES), lambda b, hp, i: (b, hp, i, 0)),
                   pl.BlockSpec((bq, LANES), lambda b, hp, i: (b * nq + i, hp))],
        out_shape=[_sds((T, D), F32), _sds((Bl, H, S, LANES), F32), _sds((T, D), MM_DTYPE)],
        compiler_params=_params())(qg, qg, kv, kv, c_col, c_row)


def _attn_bwd(qg, kv, dz, lse, c_col, c_row, Bl, S, D):
    T = Bl * S
    H = D // HEAD_DIM
    HP = D // LANES
    bq = min(S, ATT_BLOCK)
    nq = S // bq
    scale = 1.0 / math.sqrt(HEAD_DIM)

    def body(q_ref, g_ref, k_ref, v_ref, dz_ref, lse_ref, cc_ref, cr_ref, dq_ref, dk_ref, dv_ref, dcr_ref, p_sc, dp_sc):
        i = pl.program_id(2)

        @pl.when(i == 0)
        def _():
            dk_ref[...] = jnp.zeros_like(dk_ref)
            dv_ref[...] = jnp.zeros_like(dv_ref)
            dcr_ref[...] = jnp.zeros_like(dcr_ref)

        q2 = q_ref[...]
        do2 = (dz_ref[...].astype(F32) * jax.nn.sigmoid(g_ref[...].astype(F32))).astype(MM_DTYPE)
        masks = [_head_mask(h).astype(F32) for h in range(2)]
        qh = [q2 * (masks[h] * scale).astype(q2.dtype) for h in range(2)]
        doh = [do2 * masks[h].astype(do2.dtype) for h in range(2)]
        cc = [cc_ref[h][:, :1] for h in range(2)]
        lse = [lse_ref[h][:, :1] for h in range(2)]
        diag = lax.broadcasted_iota(jnp.int32, (1, bq), 1) <= lax.broadcasted_iota(jnp.int32, (bq, 1), 0)

        def sweep1(j, delta, on_diagonal):
            off = pl.multiple_of(j * bq, bq)
            kj = k_ref[pl.ds(off, bq), :]
            vj = v_ref[pl.ds(off, bq), :]
            new = []
            dv = None
            for h in range(2):
                s = lax.dot_general(qh[h], kj, NT, preferred_element_type=F32) + cc[h] - cr_ref[h, j]
                if on_diagonal:
                    s = jnp.where(diag, s, -jnp.inf)
                p = jnp.exp(s - lse[h])
                dp = lax.dot_general(doh[h], vj, NT, preferred_element_type=F32)
                p_sc[h, j] = p
                dp_sc[h, j] = dp
                part = lax.dot_general(p.astype(MM_DTYPE), doh[h], TN, preferred_element_type=F32)
                dv = part if dv is None else dv + part
                new.append(delta[h] + jnp.sum(p * dp, axis=1, keepdims=True))
            dv_ref[pl.ds(off, bq), :] += dv
            return tuple(new)

        zero = jnp.zeros((bq, 1), F32)
        delta = lax.fori_loop(0, i, lambda j, d: sweep1(j, d, False), (zero, zero))
        delta = sweep1(i, delta, True)

        def sweep2(j, dq):
            off = pl.multiple_of(j * bq, bq)
            kj = k_ref[pl.ds(off, bq), :]
            dk = None
            for h in range(2):
                ds = p_sc[h, j] * (dp_sc[h, j] - delta[h])
                dcr_ref[h, j] -= jnp.sum(ds, axis=0, keepdims=True)
                dsb = ds.astype(MM_DTYPE)
                dq = dq + jnp.dot(dsb, kj * (masks[h] * scale).astype(kj.dtype), preferred_element_type=F32)
                part = lax.dot_general(dsb, qh[h], TN, preferred_element_type=F32)
                dk = part if dk is None else dk + part
            dk_ref[pl.ds(off, bq), :] += dk
            return dq

        dq_ref[...] = lax.fori_loop(0, i + 1, sweep2, jnp.zeros((bq, LANES), F32))

    blk = lambda col: pl.BlockSpec((bq, LANES), lambda b, hp, i: (b * nq + i, col(hp)))
    seq = lambda col: pl.BlockSpec((S, LANES), lambda b, hp, i: (b, col(hp)))
    per_head = pl.BlockSpec((None, 2, bq, LANES), lambda b, hp, i: (b, hp, i, 0))
    rows = pl.BlockSpec((None, 2, nq, 1, bq), lambda b, hp, i: (b, hp, 0, 0, 0))
    return pl.pallas_call(
        body, name="attn_bwd", grid=(Bl, HP, nq),
        in_specs=[blk(lambda hp: hp), blk(lambda hp: HP + hp), seq(lambda hp: hp), seq(lambda hp: HP + hp),
                  blk(lambda hp: hp), per_head, per_head, rows],
        out_specs=[blk(lambda hp: hp), seq(lambda hp: hp), seq(lambda hp: hp), rows],
        out_shape=[_sds((T, D), F32), _sds((T, D), F32), _sds((T, D), F32), _sds((Bl, H, nq, 1, bq), F32)],
        scratch_shapes=[pltpu.VMEM((2, nq, bq, bq), F32), pltpu.VMEM((2, nq, bq, bq), F32)],
        compiler_params=_params())(qg, qg, kv, kv, dz, lse, c_col, c_row)


def _pack_dkv(dk, dv, dpf):
    T, D = dk.shape
    tm = _tile(T, ROW_TILE)

    def body(dk_ref, dv_ref, df_ref, o_ref):
        o_ref[:, :D] = dk_ref[...].astype(o_ref.dtype)
        o_ref[:, D:2 * D] = dv_ref[...].astype(o_ref.dtype)
        o_ref[:, 2 * D:] = df_ref[...]

    return pl.pallas_call(
        body, name="pack_dkv", grid=(T // tm,),
        in_specs=[_rows(tm, D), _rows(tm, D), _rows(tm, LANES)], out_specs=_rows(tm, 2 * D + LANES),
        out_shape=_sds((T, 2 * D + LANES), MM_DTYPE), compiler_params=_params())(dk, dv, dpf)


def _gate_bwd(dz, qg, o, dq):
    T, D = dz.shape
    tm = _tile(T, ROW_TILE)

    def body(dz_ref, g_ref, o_ref, dq_ref, out_ref):
        g = g_ref[...].astype(F32)
        sg = jax.nn.sigmoid(g)
        out_ref[:, :D] = dq_ref[...].astype(out_ref.dtype)
        out_ref[:, D:] = (dz_ref[...].astype(F32) * o_ref[...] * sg * (1.0 - sg)).astype(out_ref.dtype)

    return pl.pallas_call(
        body, name="gate_bwd", grid=(T // tm,),
        in_specs=[_rows(tm, D), pl.BlockSpec((tm, D), lambda i: (i, 1)), _rows(tm, D), _rows(tm, D)],
        out_specs=_rows(tm, 2 * D), out_shape=_sds((T, 2 * D), MM_DTYPE),
        compiler_params=_params())(dz, qg, o, dq)


def _mm_nn(a, b, out_dtype, tag):
    T, K = a.shape
    N = b.shape[1]
    tm = _tile(T, MM_TILE)

    def body(a_ref, b_ref, o_ref):
        o_ref[...] = jnp.dot(a_ref[...], b_ref[...], preferred_element_type=F32).astype(o_ref.dtype)

    return pl.pallas_call(
        body, name=f"mm_nn_{tag}", grid=(T // tm,),
        in_specs=[_rows(tm, K), _whole((K, N))], out_specs=_rows(tm, N),
        out_shape=_sds((T, N), out_dtype), compiler_params=_params())(a, b)


def _mm_tn_in(a, dy, tag, after=None):
    T, K = a.shape
    n = dy.shape[1] // N_CHIP
    tt = _tile(T, WG_TILE)
    whole_t = tt == T
    extra = [] if after is None else [after]

    def body(a_ref, d_ref, *rest):
        part = lax.dot_general(a_ref[...], d_ref[...], TN, preferred_element_type=F32)
        if whole_t:
            rest[-1][...] = part
        else:
            _accumulate(rest[-1], part, pl.program_id(1) == 0)

    a_spec = _resident((T, K)) if whole_t else pl.BlockSpec((tt, K), lambda s, t: (t, 0))
    return pl.pallas_call(
        body, name=f"mm_tn_in_{tag}", grid=(N_CHIP, T // tt),
        in_specs=[a_spec, pl.BlockSpec((tt, n), lambda s, t: (t, s))] + [ANY] * len(extra),
        out_specs=pl.BlockSpec((None, K, n), lambda s, t: (s, 0, 0)), out_shape=_sds((N_CHIP, K, n), F32),
        compiler_params=_params())(a, dy, *extra)


def _mm_tn_out(act, dh, tag, after=None):
    T, R4 = act.shape
    D = dh.shape[1]
    r = R4 // N_CHIP
    g = 1 if r % LANES == 0 else 2
    tt = _tile(T, WG_TILE)
    whole_t = tt == T
    extra = [] if after is None else [after]

    def body(a_ref, d_ref, *rest):
        o_ref = rest[-1]
        part = lax.dot_general(a_ref[...], d_ref[...], TN, preferred_element_type=F32)
        first = pl.program_id(1) == 0
        for q in range(g):
            if whole_t:
                o_ref[q] = part[q * r:(q + 1) * r]
            else:
                _accumulate(o_ref.at[q], part[q * r:(q + 1) * r], first)

    d_spec = _resident((T, D)) if whole_t else pl.BlockSpec((tt, D), lambda s, t: (t, 0))
    return pl.pallas_call(
        body, name=f"mm_tn_out_{tag}", grid=(N_CHIP // g, T // tt),
        in_specs=[pl.BlockSpec((tt, g * r), lambda s, t: (t, s)), d_spec] + [ANY] * len(extra),
        out_specs=pl.BlockSpec((g, r, D), lambda s, t: (s, 0, 0)), out_shape=_sds((N_CHIP, r, D), F32),
        compiler_params=_params())(act, dh, *extra)


def _mm_tn(a, b, tag):
    T, K = a.shape
    N = b.shape[1]
    tt = _tile(T, TN_TILE)

    def body(a_ref, b_ref, o_ref):
        part = lax.dot_general(a_ref[...], b_ref[...], TN, preferred_element_type=F32)
        _accumulate(o_ref, part, pl.program_id(0) == 0)

    return pl.pallas_call(
        body, name=f"mm_tn_{tag}", grid=(T // tt,),
        in_specs=[_rows(tt, K), _rows(tt, N)], out_specs=_whole((K, N)),
        out_shape=_sds((K, N), F32), compiler_params=_params())(a, b)


def _step_partial(ref, part):
    ref[...] = jnp.where(lax.broadcasted_iota(jnp.int32, ref.shape, 0) == 0, part, 0.0)


def _pack_small(partials, tail, at):
    n = len(partials)
    D = tail.shape[1]

    def body(*refs):
        o_ref = refs[-1]
        o_ref[...] = jnp.zeros_like(o_ref)
        for i in range(n):
            o_ref[i:i + 1, :] = jnp.sum(refs[i][...], axis=0, keepdims=True)
        o_ref[at:at + tail.shape[0], :] = refs[n][...]

    return pl.pallas_call(
        body, name="pack_small", grid=(1,),
        in_specs=[_whole(p.shape) for p in partials] + [_whole(tail.shape)], out_specs=_whole((SMALL_ROWS, D)),
        out_shape=_sds((SMALL_ROWS, D), F32), compiler_params=_params())(*partials, tail)


def _norm_mm_in(x, g, wg, tag, swiglu=False):
    T, D = x.shape
    n = wg.shape[-1]
    tm = _tile(T, FUSED_TILE)
    half = N_CHIP // 2

    def body(x_ref, g_ref, w_ref, xn_ref, y_ref, *rest):
        xv = x_ref[...]
        r = lax.rsqrt(jnp.mean(xv * xv, axis=-1, keepdims=True) + RMS_EPS)
        xn = (xv * r * g_ref[...]).astype(xn_ref.dtype)
        xn_ref[...] = xn

        def product(s):
            p = jnp.dot(xn, w_ref[s], preferred_element_type=F32)
            y_ref[:, s * n:(s + 1) * n] = p.astype(y_ref.dtype)
            return p

        if swiglu:
            for q in range(half):
                gate, up = product(q), product(half + q)
                rest[0][:, q * n:(q + 1) * n] = (gate * jax.nn.sigmoid(gate) * up).astype(rest[0].dtype)
        else:
            for s in range(N_CHIP):
                product(s)

    out_specs = [_rows(tm, D), _rows(tm, N_CHIP * n)]
    out_shape = [_sds((T, D), MM_DTYPE), _sds((T, N_CHIP * n), MM_DTYPE)]
    if swiglu:
        out_specs.append(_rows(tm, half * n))
        out_shape.append(_sds((T, half * n), MM_DTYPE))
    return pl.pallas_call(
        body, name=f"norm_mm_in_{tag}", grid=(T // tm,),
        in_specs=[_rows(tm, D), _whole((1, D)), _resident((N_CHIP, D, n))], out_specs=out_specs,
        out_shape=out_shape, compiler_params=_params())(x, g.reshape(1, D), wg)


def _mm_out_post(a, b, x, g, alpha, tag):
    T, K = a.shape
    D = b.shape[1]
    tm = _tile(T, FUSED_TILE)

    def body(a_ref, b_ref, x_ref, g_ref, h_ref, o_ref):
        hv = jnp.dot(a_ref[...], b_ref[...], preferred_element_type=F32)
        h_ref[...] = hv
        r = lax.rsqrt(jnp.mean(hv * hv, axis=-1, keepdims=True) + RMS_EPS)
        o_ref[...] = x_ref[...] + alpha * (hv * r * g_ref[...])

    return pl.pallas_call(
        body, name=f"mm_out_post_{tag}", grid=(T // tm,),
        in_specs=[_rows(tm, K), _resident((K, D)), _rows(tm, D), _whole((1, D))],
        out_specs=[_rows(tm, D), _rows(tm, D)], out_shape=[_sds((T, D), F32)] * 2,
        compiler_params=_params())(a, b, x, g.reshape(1, D))


def _post_bwd_mm(dx, h, g, alpha, b, tag, hgu=None):
    T, D = dx.shape
    K = b.shape[0]
    tm = _tile(T, FUSED_TILE)

    def body(dx_ref, h_ref, g_ref, b_ref, *rest):
        dh_ref, dg_ref, out_ref = rest[-3:]
        hv = h_ref[...]
        r = lax.rsqrt(jnp.mean(hv * hv, axis=-1, keepdims=True) + RMS_EPS)
        hh = hv * r
        dyn = alpha * dx_ref[...]
        _step_partial(dg_ref, jnp.sum(dyn * hh, axis=0, keepdims=True))
        dhh = dyn * g_ref[...]
        dh = (r * (dhh - hh * jnp.mean(dhh * hh, axis=-1, keepdims=True))).astype(dh_ref.dtype)
        dh_ref[...] = dh
        da = lax.dot_general(dh, b_ref[...], NT, preferred_element_type=F32)
        if hgu is None:
            out_ref[...] = da.astype(out_ref.dtype)
        else:
            gate = rest[0][:, :K]
            up = rest[0][:, K:]
            dab = da.astype(gate.dtype)
            sg = jax.nn.sigmoid(gate)
            out_ref[:, :K] = (dab * up * sg * (1.0 + gate * (1.0 - sg))).astype(out_ref.dtype)
            out_ref[:, K:] = (dab * gate * sg).astype(out_ref.dtype)

    in_specs = [_rows(tm, D), _rows(tm, D), _whole((1, D)), _resident((K, D))]
    args = [dx, h, g.reshape(1, D), b]
    wide = K
    if hgu is not None:
        wide = 2 * K
        in_specs.append(_rows(tm, wide))
        args.append(hgu)
    dh, dg, da = pl.pallas_call(
        body, name=f"post_bwd_mm_{tag}", grid=(T // tm,), in_specs=in_specs,
        out_specs=[_rows(tm, D), _rows(8, D), _rows(tm, wide)],
        out_shape=[_sds((T, D), MM_DTYPE), _sds((8 * (T // tm), D), F32), _sds((T, wide), MM_DTYPE)],
        compiler_params=_params())(*args)
    return dh, dg, da


def _mm_nt_pre(dy, w, dres, x, g, tag):
    T, C = dy.shape
    D = x.shape[1]
    tm = _tile(T, FUSED_TILE)
    n = w.shape[-1]

    def body(dy_ref, w_ref, dres_ref, x_ref, g_ref, dx_ref, dg_ref):
        if w.ndim == 2:
            dn = lax.dot_general(dy_ref[...], w_ref[...], NT, preferred_element_type=F32)
        else:
            dn = None
            for s in range(N_CHIP):
                part = lax.dot_general(dy_ref[:, s * n:(s + 1) * n], w_ref[s], NT, preferred_element_type=F32)
                dn = part if dn is None else dn + part
        xv = x_ref[...]
        r = lax.rsqrt(jnp.mean(xv * xv, axis=-1, keepdims=True) + RMS_EPS)
        xh = xv * r
        _step_partial(dg_ref, jnp.sum(dn * xh, axis=0, keepdims=True))
        dxh = dn * g_ref[...]
        dx_ref[...] = dres_ref[...] + r * (dxh - xh * jnp.mean(dxh * xh, axis=-1, keepdims=True))

    dx, dg = pl.pallas_call(
        body, name=f"mm_nt_pre_{tag}", grid=(T // tm,),
        in_specs=[_rows(tm, C), _resident(w.shape), _rows(tm, D), _rows(tm, D), _whole((1, D))],
        out_specs=[_rows(tm, D), _rows(8, D)], out_shape=[_sds((T, D), F32), _sds((8 * (T // tm), D), F32)],
        compiler_params=_params())(dy, w, dres, x, g.reshape(1, D))
    return dx, dg


def _adamw(w, g, m, v, tag, after=None):
    R, C = w.shape
    tr = _tile(R, ROW_TILE)
    extra = [] if after is None else [after]

    def body(w_ref, g_ref, m_ref, v_ref, *rest):
        go_ref, d_ref, mo_ref, vo_ref = rest[-4:]
        gv = g_ref[...]
        go_ref[...] = gv
        mn = ADAM_B1 * m_ref[...] + (1.0 - ADAM_B1) * gv
        vn = ADAM_B2 * v_ref[...] + (1.0 - ADAM_B2) * (gv * gv)
        m_hat = mn / (1.0 - ADAM_B1 ** ADAM_STEP)
        v_hat = vn / (1.0 - ADAM_B2 ** ADAM_STEP)
        d_ref[...] = -ADAM_LR * (m_hat / (jnp.sqrt(v_hat) + ADAM_EPS) + ADAM_WD * w_ref[...])
        mo_ref[...] = mn
        vo_ref[...] = vn

    return pl.pallas_call(
        body, name=f"adamw_{tag}", grid=(R // tr,),
        in_specs=[_rows(tr, C)] * 4 + [ANY] * len(extra), out_specs=[_rows(tr, C)] * 4,
        out_shape=[_sds((R, C), F32)] * 4, compiler_params=_params())(w, g, m, v, *extra)


def _sum_devices(gall, own, place):
    _, R, C = gall.shape

    def body(place_ref, g_ref, s_ref, o_ref):
        me = 2 * place_ref[1] + place_ref[0]
        acc = None
        for d in range(N_DEV):
            term = jnp.where(me == d, s_ref[...], g_ref[d])
            acc = term if acc is None else acc + term
        o_ref[...] = acc

    grid_spec = pltpu.PrefetchScalarGridSpec(
        num_scalar_prefetch=1, grid=(1,),
        in_specs=[pl.BlockSpec((N_DEV, R, C), lambda i, p: (0, 0, 0)), pl.BlockSpec((R, C), lambda i, p: (0, 0))],
        out_specs=pl.BlockSpec((R, C), lambda i, p: (0, 0)))
    return pl.pallas_call(
        body, name="sum_devices", grid_spec=grid_spec, out_shape=_sds((R, C), F32),
        compiler_params=_params())(place, gall, own)


HBM = pl.BlockSpec(memory_space=pltpu.HBM)
SEM = pl.BlockSpec(memory_space=pltpu.SEMAPHORE)
EFFECT = pltpu.SideEffectType.DATAFLOW_SIDE_EFFECTING


def _place():
    x, y, c = lax.axis_index("x"), lax.axis_index("y"), lax.axis_index("c")
    chips = ((1 - x, y), (x, 1 - y), (1 - x, 1 - y))
    return x, y, c, chips


def _remote(src, dst, send_sem, recv_sem, dev):
    return pltpu.make_async_remote_copy(src_ref=src, dst_ref=dst, send_sem=send_sem, recv_sem=recv_sem,
                                        device_id=dev, device_id_type=MESH)


def _in_hbm(a):
    return pltpu.with_memory_space_constraint(a, pltpu.HBM)


def _own_slot(w4, l, dtype, place, tag):
    _, _, r, col = w4.shape
    tr = _tile(r, 2 * ROW_TILE)

    def body(place_ref, x_ref, o_ref):
        o_ref[...] = x_ref[...].astype(o_ref.dtype)

    grid_spec = pltpu.PrefetchScalarGridSpec(
        num_scalar_prefetch=1, grid=(2, r // tr),
        in_specs=[pl.BlockSpec((None, None, tr, col), lambda h, i, p: (l, h, i, 0))],
        out_specs=pl.BlockSpec((None, None, tr, col), lambda h, i, p: (p[1], h, i, 0)))
    return pl.pallas_call(
        body, name=f"own_slot_{tag}", grid_spec=grid_spec, out_shape=_sds((N_CHIP, 2, r, col), dtype),
        compiler_params=_params())(place, w4)


def _gather_start(bufs, after, tag):
    n = len(bufs)

    def body(*refs):
        ins = refs[:n]
        s_sem, r_sem, token = refs[n + 1], refs[n + 2], refs[2 * n + 3]
        x, y, c, chips = _place()
        me = 2 * x + y
        for i in range(n):
            mine = ins[i].at[me, c]
            for j, (px, py) in enumerate(chips):
                _remote(mine, mine, s_sem.at[3 * i + j], r_sem.at[3 * i + j], (px, py, c)).start()
        token[...] = jnp.zeros_like(token)

    dma = pltpu.SemaphoreType.DMA
    res = pl.pallas_call(
        body, name=f"gather_start_{tag}", in_specs=[HBM] * n + [ANY],
        out_specs=[SEM, SEM] + [HBM] * n + [pl.BlockSpec(memory_space=pltpu.VMEM)],
        out_shape=[dma((3 * n,)), dma((3 * n,))] + [pltpu.HBM(b.shape, b.dtype) for b in bufs] + [_sds((8, LANES), F32)],
        input_output_aliases={i: i + 2 for i in range(n)},
        compiler_params=pltpu.CompilerParams(has_side_effects=EFFECT),
        )(*[_in_hbm(b) for b in bufs], after)
    return res[0], res[1], list(res[2:2 + n]), res[-1]


def _gather_pass(s_sem, r_sem, bufs, first, after, tag):
    n = len(bufs)

    def body(*refs):
        ins = refs[:n]
        a_s, a_r, b_s, b_r = refs[n], refs[n + 1], refs[n + 3], refs[n + 4]
        x, y, c, chips = _place()
        me = 2 * x + y
        sib = (x, y, 1 - c)
        for i in range(n):
            mine = ins[i].at[me, c]
            for j, (px, py) in enumerate(chips):
                k = 3 * (first + i) + j
                _remote(mine, mine, a_s.at[k], a_r.at[k], (px, py, c)).wait_send()
        for j, (px, py) in enumerate(chips):
            for i in range(n):
                k = 3 * (first + i) + j
                blk = ins[i].at[2 * px + py, c]
                _remote(blk, blk, a_s.at[k], a_r.at[k], (px, py, c)).wait_recv()
                _remote(blk, blk, b_s.at[3 * i + j], b_r.at[3 * i + j], sib).start()

    dma = pltpu.SemaphoreType.DMA
    res = pl.pallas_call(
        body, name=f"gather_pass_{tag}", in_specs=[HBM] * n + [SEM, SEM, ANY],
        out_specs=[SEM, SEM] + [HBM] * n,
        out_shape=[dma((3 * n,)), dma((3 * n,))] + [pltpu.HBM(b.shape, b.dtype) for b in bufs],
        input_output_aliases={i: i + 2 for i in range(n)},
        compiler_params=pltpu.CompilerParams(has_side_effects=EFFECT),
        )(*bufs, s_sem, r_sem, after)
    return res[0], res[1], list(res[2:])


def _gather_land(s_sem, r_sem, bufs, tag):
    n = len(bufs)

    def body(*refs):
        ins = refs[:n]
        b_s, b_r = refs[n], refs[n + 1]
        x, y, c, chips = _place()
        sib = (x, y, 1 - c)
        for j, (px, py) in enumerate(chips):
            for i in range(n):
                sent = ins[i].at[2 * px + py, c]
                got = ins[i].at[2 * px + py, 1 - c]
                _remote(sent, sent, b_s.at[3 * i + j], b_r.at[3 * i + j], sib).wait_send()
                _remote(got, got, b_s.at[3 * i + j], b_r.at[3 * i + j], sib).wait_recv()

    return list(pl.pallas_call(
        body, name=f"gather_land_{tag}", in_specs=[HBM] * n + [SEM, SEM], out_specs=[HBM] * n,
        out_shape=[pltpu.HBM(b.shape, b.dtype) for b in bufs],
        input_output_aliases={i: i for i in range(n)},
        compiler_params=pltpu.CompilerParams(has_side_effects=EFFECT),
        )(*bufs, s_sem, r_sem))


def _rs_pair_add(g, recv, place, tag):
    r, col = g.shape[-2:]
    tr = _tile(r, ROW_TILE)

    def body(place_ref, g_ref, r_ref, wire_ref, own_ref):
        tot = g_ref[...] + r_ref[...]
        wire_ref[...] = tot.astype(wire_ref.dtype)

        @pl.when(pl.program_id(1) == place_ref[1])
        def _():
            own_ref[...] = tot

    grid_spec = pltpu.PrefetchScalarGridSpec(
        num_scalar_prefetch=1, grid=(r // tr, N_CHIP),
        in_specs=[pl.BlockSpec((None, None, tr, col), lambda i, s, p: (s, p[0], i, 0)),
                  pl.BlockSpec((None, tr, col), lambda i, s, p: (s, i, 0))],
        out_specs=[pl.BlockSpec((None, tr, col), lambda i, s, p: (s, i, 0)),
                   pl.BlockSpec((tr, col), lambda i, s, p: (i, 0))])
    return pl.pallas_call(
        body, name=f"rs_pair_add_{tag}", grid_spec=grid_spec,
        out_shape=[_sds((N_CHIP, r, col), WIRE_DTYPE), _sds((r, col), F32)],
        compiler_params=_params())(place, g, recv)


def _pair_plan(srcs, lands):
    x, y, c, _ = _place()
    return [(s.at[:, 1 - c], l, (x, y, 1 - c)) for s, l in zip(srcs, lands)]


def _chip_plan(srcs, lands):
    x, y, c, chips = _place()
    plan = []
    for s, l in zip(srcs, lands):
        if len(s.shape) == 2:
            me = 4 * x + 2 * y + c
            plan += [(s, l.at[me], (x ^ (k >> 2), y ^ ((k >> 1) & 1), c ^ (k & 1))) for k in range(1, N_DEV)]
        else:
            plan += [(s.at[2 * px + py], l.at[j], (px, py, c)) for j, (px, py) in enumerate(chips)]
    return plan


def _exchange_start(srcs, lands, plan, count, tag):
    n = len(srcs)
    both = list(srcs) + list(lands)

    def body(*refs):
        s_sem, r_sem, token = refs[2 * n], refs[2 * n + 1], refs[4 * n + 2]
        for k, (src, dst, dev) in enumerate(plan(refs[:n], refs[n:2 * n])):
            _remote(src, dst, s_sem.at[k], r_sem.at[k], dev).start()
        token[...] = jnp.zeros_like(token)

    dma = pltpu.SemaphoreType.DMA
    res = pl.pallas_call(
        body, name=f"exchange_start_{tag}", in_specs=[HBM] * (2 * n),
        out_specs=[SEM, SEM] + [HBM] * (2 * n) + [pl.BlockSpec(memory_space=pltpu.VMEM)],
        out_shape=[dma((count,)), dma((count,))] + [pltpu.HBM(b.shape, b.dtype) for b in both] + [_sds((8, LANES), F32)],
        input_output_aliases={i: i + 2 for i in range(2 * n)},
        compiler_params=pltpu.CompilerParams(has_side_effects=EFFECT),
        )(*[_in_hbm(b) for b in both])
    return res[0], res[1], list(res[2:2 + n]), list(res[2 + n:2 + 2 * n]), res[-1]


def _exchange_wait(s_sem, r_sem, srcs, lands, plan, after, tag):
    n = len(srcs)

    def body(*refs):
        s_ref, r_ref = refs[2 * n], refs[2 * n + 1]
        for k, (src, dst, dev) in enumerate(plan(refs[:n], refs[n:2 * n])):
            cp = _remote(src, dst, s_ref.at[k], r_ref.at[k], dev)
            cp.wait_send()
            cp.wait_recv()

    both = list(srcs) + list(lands)
    res = pl.pallas_call(
        body, name=f"exchange_wait_{tag}", in_specs=[HBM] * (2 * n) + [SEM, SEM, ANY], out_specs=[HBM] * (2 * n),
        out_shape=[pltpu.HBM(b.shape, b.dtype) for b in both],
        input_output_aliases={i: i for i in range(2 * n)},
        compiler_params=pltpu.CompilerParams(has_side_effects=EFFECT),
        )(*both, s_sem, r_sem, after)
    return list(res[:n]), list(res[n:])


def _rs_chip_add(own, recv, place, l, L, prev, tag):
    r, col = own.shape
    tr = _tile(r, ROW_TILE)

    def body(place_ref, o_ref, r_ref, *rest):
        acc = o_ref[...]
        for j in range(3):
            acc = acc + r_ref[j].astype(F32)
        rest[-1][...] = acc

    in_specs = [pl.BlockSpec((tr, col), lambda i, p: (i, 0)), pl.BlockSpec((3, tr, col), lambda i, p: (0, i, 0))]
    args = [place, own, recv]
    kw = {}
    if prev is not None:
        in_specs.append(ANY)
        args.append(prev)
        kw["input_output_aliases"] = {3: 0}
    grid_spec = pltpu.PrefetchScalarGridSpec(
        num_scalar_prefetch=1, grid=(r // tr,), in_specs=in_specs,
        out_specs=pl.BlockSpec((None, None, tr, col), lambda i, p: (l, p[0], i, 0)))
    return pl.pallas_call(
        body, name=f"rs_chip_add_{tag}", grid_spec=grid_spec, out_shape=_sds((L, 2, r, col), F32),
        compiler_params=_params(), **kw)(*args)


def _rs_pair_share(fulls, tag):
    n = len(fulls)

    def body(*refs):
        outs = refs[n:2 * n]
        s_sem, r_sem = refs[2 * n:]
        x, y, c, _ = _place()
        sib = (x, y, 1 - c)
        started = []
        for i in range(n):
            cp = _remote(outs[i].at[:, c], outs[i].at[:, c], s_sem.at[i], r_sem.at[i], sib)
            cp.start()
            started.append(cp)
        for i, cp in enumerate(started):
            cp.wait_send()
            _remote(outs[i].at[:, 1 - c], outs[i].at[:, 1 - c], s_sem.at[i], r_sem.at[i], sib).wait_recv()

    dma = pltpu.SemaphoreType.DMA
    return pl.pallas_call(
        body, name=f"rs_pair_share_{tag}", in_specs=[ANY] * n, out_specs=[ANY] * n,
        out_shape=[_sds(f.shape, f.dtype) for f in fulls],
        input_output_aliases={i: i for i in range(n)},
        scratch_shapes=[dma((n,)), dma((n,))],
        )(*fulls)


def _ffn_fwd(x, g_pre, g_post, w_in, w_out, tag):
    xn, hgu, act = _norm_mm_in(x, g_pre, w_in, tag, swiglu=True)
    if callable(w_out):
        w_out = w_out(act)
    h, x_out = _mm_out_post(act, w_out.reshape(-1, w_out.shape[-1]), x, g_post, 0.5, tag)
    return x_out, (x, xn, hgu, act, h)


def _ffn_bwd(dx, saved, g_pre, g_post, w_in, w_out, tag, between=None, finish=None):
    x, xn, hgu, act, h = saved
    dh, dg_post, dhgu = _post_bwd_mm(dx, h, g_post, 0.5, w_out.reshape(-1, w_out.shape[-1]), tag, hgu=hgu)
    token = None
    if between is not None:
        token = between(dhgu)
    dw_out = _mm_tn_out(act, dh, tag, after=token)
    dw_in = _mm_tn_in(xn, dhgu, tag, after=token)
    if finish is not None:
        token = finish(dw_in, dw_out)
    if token is not None:
        g_pre = g_pre + token[0, :1]
    dx_in, dg_pre = _mm_nt_pre(dhgu, w_in, dx, x, g_pre, tag)
    return dx_in, dg_pre, dg_post, dw_in, dw_out


def kernel(x, ffn1_pre_g, ffn1_post_g, ffn1_w_in, ffn1_w_out, mix_pre_g, mix_post_g, ffn2_pre_g, ffn2_post_g, ffn2_w_in, ffn2_w_out, conv_w_in, conv_k, conv_w_out, kv_g, kv_w, forget_b, attn_w_qg, attn_w_o, loss_target, m_ffn1_pre_g, m_ffn1_post_g, m_ffn1_w_in, m_ffn1_w_out, m_mix_pre_g, m_mix_post_g, m_ffn2_pre_g, m_ffn2_post_g, m_ffn2_w_in, m_ffn2_w_out, m_conv_w_in, m_conv_k, m_conv_w_out, m_kv_g, m_kv_w, m_forget_b, m_attn_w_qg, m_attn_w_o, v_ffn1_pre_g, v_ffn1_post_g, v_ffn1_w_in, v_ffn1_w_out, v_mix_pre_g, v_mix_post_g, v_ffn2_pre_g, v_ffn2_post_g, v_ffn2_w_in, v_ffn2_w_out, v_conv_w_in, v_conv_k, v_conv_w_out, v_kv_g, v_kv_w, v_forget_b, v_attn_w_qg, v_attn_w_o):
    Bl, S, D = x.shape
    T = Bl * S
    H = forget_b.shape[0]
    assert D == H * HEAD_DIM and D % LANES == 0
    kvc = kv_w.shape[1]
    kvp = -(-kvc // LANES) * LANES
    kv_all = 2 * D + LANES
    dk_cols = conv_k.shape[2]
    chip = 2 * lax.axis_index("x") + lax.axis_index("y")
    core = lax.axis_index("c")

    given = dict(ffn1_w_in=(ffn1_w_in, m_ffn1_w_in, v_ffn1_w_in), ffn1_w_out=(ffn1_w_out, m_ffn1_w_out, v_ffn1_w_out),
                 ffn2_w_in=(ffn2_w_in, m_ffn2_w_in, v_ffn2_w_in), ffn2_w_out=(ffn2_w_out, m_ffn2_w_out, v_ffn2_w_out),
                 conv_w_in=(conv_w_in, m_conv_w_in, v_conv_w_in), conv_w_out=(conv_w_out, m_conv_w_out, v_conv_w_out),
                 kv_w=(kv_w, m_kv_w, v_kv_w), attn_w_qg=(attn_w_qg, m_attn_w_qg, v_attn_w_qg),
                 attn_w_o=(attn_w_o, m_attn_w_o, v_attn_w_o))
    shards = {k: w for k, (w, _, _) in given.items()}
    shards["kv_w"] = jnp.pad(kv_w, ((0, 0), (0, kvp - kvc)))[None]
    groups = [[("ffn1_w_in", 0), ("ffn1_w_out", 0)], [("conv_w_in", 0), ("conv_w_out", 0)],
              [("ffn2_w_in", 0), ("ffn2_w_out", 0)], [("kv_w", 0), ("ffn1_w_in", 1), ("ffn1_w_out", 1)],
              [("attn_w_qg", 0), ("attn_w_o", 0), ("ffn2_w_in", 1), ("ffn2_w_out", 1)]]
    second = groups[3] + groups[4]
    place = jnp.stack([core, chip]).astype(jnp.int32)

    def slot(key, where):
        w = shards[key[0]]
        L, r, col = w.shape
        return _own_slot(w.reshape(L, 2, r // 2, col), key[1], MM_DTYPE, where, f"{key[0]}{key[1]}")

    def whole(g):
        return g.reshape(N_CHIP, -1, g.shape[-1])

    taps_slot = _own_slot(jnp.pad(conv_k[0], ((0, 13), (0, 0))).reshape(1, 2, 8, dk_cols), 0, F32, place, "conv_k")
    fb = jnp.pad(forget_b, (0, LANES - H)).reshape(1, LANES)
    w_in0, w_out0 = groups[0]
    s_0, r_0, fly_0, token = _gather_start([slot(w_in0, place), taps_slot, slot(w_out0, place)], fb, "first")
    later = groups[1] + groups[2] + groups[3] + groups[4]
    s_1, r_1, fly_1, token = _gather_start([slot(key, place) for key in later], token, "rest")
    W = {}

    def land(sems, bufs, lo, after, tag):
        return _gather_land(*_gather_pass(*sems, bufs, lo, after, tag), tag)

    def arrive(g, after):
        lo = sum(len(groups[k]) for k in range(1, g))
        got = land((s_1, r_1), fly_1[lo:lo + len(groups[g])], lo, after, f"g{g}")
        W.update({key: whole(b) for key, b in zip(groups[g], got)})

    w_first, taps = land((s_0, r_0), fly_0[:2], 0, token, "g0")
    k_taps = taps.reshape(N_CHIP, 16, dk_cols).transpose(1, 0, 2).reshape(16, D)[:8]

    x0 = x.reshape(T, D)
    W[w_in0] = whole(w_first)

    def first_w_out(act):
        W[w_out0] = whole(land((s_0, r_0), fly_0[2:], 2, act, "g0_out")[0])
        return W[w_out0]

    x1, s_f1a = _ffn_fwd(x0, ffn1_pre_g[0], ffn1_post_g[0], W[w_in0], first_w_out, "l0f1")
    arrive(1, x1)
    w_o_conv = W["conv_w_out", 0].reshape(D, D)
    xn_c, bch = _norm_mm_in(x1, mix_pre_g[0], W["conv_w_in", 0], "conv")
    z_c = _conv_fwd(bch, k_taps, Bl, S)
    m_c, x2 = _mm_out_post(z_c, w_o_conv, x1, mix_post_g[0], 1.0, "conv_out")
    arrive(2, x2)
    x3, s_f2a = _ffn_fwd(x2, ffn2_pre_g[0], ffn2_post_g[0], W["ffn2_w_in", 0], W["ffn2_w_out", 0], "l0f2")

    arrive(3, x3)
    kv_full = jnp.concatenate([W["kv_w", 0][s, :, :kvc] for s in range(N_CHIP)], axis=1)
    kv_full = jnp.pad(kv_full, ((0, 0), (0, kv_all - kv_full.shape[1])))
    xn_kv = _rms_fwd(x3, kv_g, "kv")
    kvact = _mm_nn(xn_kv, kv_full[:, :2 * D], MM_DTYPE, "kv")
    pf = _mm_nn(xn_kv, kv_full[:, 2 * D:], F32, "forget")
    cum = _forget_fwd(pf, fb, Bl, S)
    bq = min(S, ATT_BLOCK)
    c3 = cum.reshape(Bl, S, LANES)[:, :, :H].transpose(0, 2, 1)
    c_col = jnp.broadcast_to(c3[..., None], (Bl, H, S, LANES))
    c_row = c3.reshape(Bl, H, S // bq, 1, bq)

    x4, s_f1b = _ffn_fwd(x3, ffn1_pre_g[1], ffn1_post_g[1], W["ffn1_w_in", 1], W["ffn1_w_out", 1], "l1f1")
    arrive(4, x4)
    w_o_attn = W["attn_w_o", 0].reshape(D, D)
    xn_a, qg = _norm_mm_in(x4, mix_pre_g[1], W["attn_w_qg", 0], "qg")
    o, lse, z_a = _attn_fwd(qg, kvact, c_col, c_row, Bl, S, D)
    m_a, x5 = _mm_out_post(z_a, w_o_attn, x4, mix_post_g[1], 1.0, "attn_out")
    x6, s_f2b = _ffn_fwd(x5, ffn2_pre_g[1], ffn2_post_g[1], W["ffn2_w_in", 1], W["ffn2_w_out", 1], "l1f2")

    dy, loss_local = _loss_grad(x6, loss_target.reshape(T, D))

    G = {}
    dx5, dg_f2pre_1, dg_f2post_1, G["ffn2_w_in", 1], G["ffn2_w_out", 1] = _ffn_bwd(
        dy, s_f2b, ffn2_pre_g[1], ffn2_post_g[1], W["ffn2_w_in", 1], W["ffn2_w_out", 1], "l1f2")
    dm_a, dg_mixpost_1, dz_a = _post_bwd_mm(dx5, m_a, mix_post_g[1], 1.0, w_o_attn, "attn_out")
    G["attn_w_o", 0] = _mm_tn_out(z_a, dm_a, "attn_out")
    dq, dk, dv, dcr = _attn_bwd(qg, kvact, dz_a, lse, c_col, c_row, Bl, S, D)
    dqg = _gate_bwd(dz_a, qg, o, dq)
    G["attn_w_qg", 0] = _mm_tn_in(xn_a, dqg, "qg")
    dx4, dg_mixpre_1 = _mm_nt_pre(dqg, W["attn_w_qg", 0], dx5, x4, mix_pre_g[1], "qg")
    dx3, dg_f1pre_1, dg_f1post_1, G["ffn1_w_in", 1], G["ffn1_w_out", 1] = _ffn_bwd(
        dx4, s_f1b, ffn1_pre_g[1], ffn1_post_g[1], W["ffn1_w_in", 1], W["ffn1_w_out", 1], "l1f1")

    dcum = jnp.pad(dcr.reshape(Bl, H, S).transpose(0, 2, 1), ((0, 0), (0, 0), (0, LANES - H))).reshape(T, LANES)
    dpf, dfb = _forget_bwd(dcum, pf, fb, Bl, S)
    dp = _pack_dkv(dk, dv, dpf)
    G_kv_full = _mm_tn(xn_kv, dp, "kv")
    G["kv_w", 0] = jnp.stack([jnp.pad(G_kv_full[:, s * kvc:(s + 1) * kvc], ((0, 0), (0, kvp - kvc))) for s in range(N_CHIP)])
    dx3, dg_kv = _mm_nt_pre(dp, kv_full, dx3, x3, kv_g, "kv")

    def halves_of(keys):
        return [G[k].reshape(N_CHIP, 2, G[k].shape[1] // 2, G[k].shape[2]) for k in keys]

    def pair_adds(keys, grads, recvs):
        wires, owns = [], []
        for k, g, r in zip(keys, grads, recvs):
            w, own = _rs_pair_add(g, r, place, f"{k[0]}{k[1]}")
            wires.append(w)
            owns.append(own)
        return wires, owns

    def chip_start(wires, tag, extra=()):
        lands = [lax.empty((3,) + w.shape[1:], w.dtype) for w in wires]
        lands += [jnp.zeros((N_DEV,) + e.shape, e.dtype) for e in extra]
        return _exchange_start(list(wires) + list(extra), lands, _chip_plan, 3 * len(wires) + (N_DEV - 1) * len(extra), tag)

    late = groups[2] + groups[1]
    last = groups[0]
    grads_2 = halves_of(second)
    p_sems, p_semr, grads_2, sib_2, token = _exchange_start(
        grads_2, [lax.empty((N_CHIP,) + g.shape[2:], g.dtype) for g in grads_2], _pair_plan, len(grads_2), "pair_second")

    dx2, dg_f2pre_0, dg_f2post_0, G["ffn2_w_in", 0], G["ffn2_w_out", 0] = _ffn_bwd(
        dx3, s_f2a, ffn2_pre_g[0], ffn2_post_g[0] + token[0, :1], W["ffn2_w_in", 0], W["ffn2_w_out", 0], "l0f2")
    grads_2, sib_2 = _exchange_wait(p_sems, p_semr, grads_2, sib_2, _pair_plan, dx2, "pair_second")
    wires_2, owns_2 = pair_adds(second, grads_2, sib_2)
    c_2 = chip_start(wires_2, "chip_second")
    dm_c, dg_mixpost_0, dz_c = _post_bwd_mm(dx2, m_c, mix_post_g[0] + c_2[4][0, :1], 1.0, w_o_conv, "conv_out")
    G["conv_w_out", 0] = _mm_tn_out(z_c, dm_c, "conv_out")
    db, dcg, dhh, dk_taps = _conv_bwd(bch, dz_c, k_taps, Bl, S)
    dbch = jnp.concatenate([db, dcg, dhh], axis=1)
    G["conv_w_in", 0] = _mm_tn_in(xn_c, dbch, "conv")
    dx1, dg_mixpre_0 = _mm_nt_pre(dbch, W["conv_w_in", 0], dx2, x1, mix_pre_g[0], "conv")
    def pair_start(keys, tag):
        grads = halves_of(keys)
        lands = [lax.empty((N_CHIP,) + g.shape[2:], g.dtype) for g in grads]
        return _exchange_start(grads, lands, _pair_plan, len(grads), tag)

    p_l = pair_start(late, "pair_late")
    late_done = {}

    def late_leg(dhgu):
        grads_l, sib_l = _exchange_wait(*p_l[:4], _pair_plan, dhgu, "pair_late")
        late_done["wires"], late_done["owns"] = pair_adds(late, grads_l, sib_l)
        late_done["chip"] = chip_start(late_done["wires"], "chip_late")
        return late_done["chip"][4]

    def last_pair(dw_in, dw_out):
        G["ffn1_w_in", 0], G["ffn1_w_out", 0] = dw_in, dw_out
        late_done["pair"] = pair_start(last, "pair_last")
        return late_done["pair"][4]

    dx0, dg_f1pre_0, dg_f1post_0, _, _ = _ffn_bwd(
        dx1, s_f1a, ffn1_pre_g[0], ffn1_post_g[0] + p_l[4][0, :1], W["ffn1_w_in", 0], W["ffn1_w_out", 0], "l0f1",
        between=late_leg, finish=last_pair)
    grad_x = dx0.reshape(Bl, S, D)
    owns_l, c_l, p_1 = late_done["owns"], late_done["chip"], late_done["pair"]

    grads_1, sib_1 = _exchange_wait(*p_1[:4], _pair_plan, dx0, "pair_last")
    wires_1, owns_1 = pair_adds(last, grads_1, sib_1)

    def row(v):
        return jnp.pad(v.reshape(-1), (0, D - v.size)).reshape(1, D)

    gain_parts = [dg_f1pre_0, dg_f1pre_1, dg_f1post_0, dg_f1post_1, dg_mixpre_0, dg_mixpre_1, dg_mixpost_0, dg_mixpost_1,
                  dg_f2pre_0, dg_f2pre_1, dg_f2post_0, dg_f2post_1, dg_kv]
    tail = jnp.concatenate([row(dfb[0, :H]), dk_taps[:3], jnp.full((1, D), loss_local), jnp.zeros((3, D), F32)], axis=0)
    small = _pack_small(gain_parts, tail, 16)
    c_1 = chip_start(wires_1, "chip_last", extra=[small])
    _, recvs_2 = _exchange_wait(*c_2[:4], _chip_plan, c_1[4], "chip_second")
    _, recvs_l = _exchange_wait(*c_l[:4], _chip_plan, c_1[4], "chip_late")
    partial = {}

    def chip_adds(keys, owns, recvs):
        for (name, l), own, rcv in zip(keys, owns, recvs):
            partial[name] = _rs_chip_add(own, rcv, place, l, shards[name].shape[0], partial.get(name), f"{name}{l}")

    chip_adds(late + second, owns_l + owns_2, recvs_l + recvs_2)
    res = {}

    def adamw(names, reduced, after):
        for k, red in zip(names, reduced):
            w, m, v = given[k]
            g2 = red.reshape(-1, red.shape[-1])
            if k == "kv_w":
                g2 = g2[:, :kvc]
            flat = lambda a: a.reshape(-1, a.shape[-1])
            go, d, mn, vn = _adamw(flat(w), g2, flat(m), flat(v), k, after=after)
            res[k] = tuple(a.reshape(w.shape) for a in (go, d, mn, vn))
        return d

    early = [k for k in partial if (k, 0) not in last]
    done = adamw(early, _rs_pair_share([partial[k] for k in early], "early"), c_1[4])
    _, recvs_1 = _exchange_wait(*c_1[:4], _chip_plan, done, "chip_last")
    chip_adds(last, owns_1, recvs_1[:-1])
    rest = [k for k, _ in last]
    adamw(rest, _rs_pair_share([partial[k] for k in rest], "last"), None)
    gsum = _sum_devices(recvs_1[-1], small, place)
    loss = gsum[20, 0]

    small_names = ["ffn1_pre_g", "ffn1_post_g", "mix_pre_g", "mix_post_g", "ffn2_pre_g", "ffn2_post_g"]
    small_given = dict(ffn1_pre_g=(ffn1_pre_g, m_ffn1_pre_g, v_ffn1_pre_g), ffn1_post_g=(ffn1_post_g, m_ffn1_post_g, v_ffn1_post_g),
                       mix_pre_g=(mix_pre_g, m_mix_pre_g, v_mix_pre_g), mix_post_g=(mix_post_g, m_mix_post_g, v_mix_post_g),
                       ffn2_pre_g=(ffn2_pre_g, m_ffn2_pre_g, v_ffn2_pre_g), ffn2_post_g=(ffn2_post_g, m_ffn2_post_g, v_ffn2_post_g))

    def pack(idx):
        rows_ = [small_given[k][idx] for k in small_names]
        rows_ += [row((kv_g, m_kv_g, v_kv_g)[idx]), jnp.zeros((3, D), F32), row((forget_b, m_forget_b, v_forget_b)[idx])]
        rows_.append(jnp.pad((conv_k, m_conv_k, v_conv_k)[idx][0], ((0, 0), (0, D - dk_cols))))
        a = jnp.concatenate(rows_, axis=0)
        return jnp.pad(a, ((0, SMALL_ROWS - a.shape[0]), (0, 0)))

    g_taps = lax.dynamic_slice_in_dim(gsum[17:20], chip * dk_cols, dk_cols, axis=1)
    g_small = jnp.concatenate([gsum[:17], jnp.pad(g_taps, ((0, 0), (0, D - dk_cols))), gsum[20:]], axis=0)
    g_small, d_s, m_s, v_s = _adamw(pack(0), g_small, pack(1), pack(2), "small")
    for i, k in enumerate(small_names):
        res[k] = tuple(a[2 * i:2 * i + 2] for a in (g_small, d_s, m_s, v_s))
    res["kv_g"] = tuple(a[12] for a in (g_small, d_s, m_s, v_s))
    res["forget_b"] = tuple(a[16, :H] for a in (g_small, d_s, m_s, v_s))
    res["conv_k"] = tuple(a[17:20, :dk_cols][None] for a in (g_small, d_s, m_s, v_s))

    order = ["ffn1_pre_g", "ffn1_post_g", "ffn1_w_in", "ffn1_w_out", "mix_pre_g", "mix_post_g", "ffn2_pre_g", "ffn2_post_g",
             "ffn2_w_in", "ffn2_w_out", "conv_w_in", "conv_k", "conv_w_out", "kv_g", "kv_w", "forget_b", "attn_w_qg", "attn_w_o"]
    out = [loss, grad_x]
    for idx in range(4):
        out += [res[k][idx] for k in order]
    return tuple(out)
```

```python
import math

import jax
import jax.numpy as jnp
from jax import lax
from jax.experimental import pallas as pl
from jax.experimental.pallas import tpu as pltpu

F32 = jnp.float32
MM_DTYPE = jnp.bfloat16
WIRE_DTYPE = jnp.bfloat16

RMS_EPS = 1e-6
ADAM_LR = 0.001
ADAM_B1 = 0.9
ADAM_B2 = 0.999
ADAM_EPS = 1e-08
ADAM_WD = 0.01
ADAM_STEP = 10

HEAD_DIM = 64
LANES = 128
N_CHIP = 4
N_DEV = 8
ROW_TILE = 512
MM_TILE = 512
FUSED_TILE = 512
TN_TILE = 2048
WG_TILE = 4096
ATT_BLOCK = 512
SMALL_ROWS = 24
V7X_VMEM_BYTES = 64 * 1024 * 1024
VMEM_LIMIT = V7X_VMEM_BYTES - 8 * 1024 * 1024
MESH = pl.DeviceIdType.MESH
ANY = pl.BlockSpec(memory_space=pl.ANY)

NT = (((1,), (1,)), ((), ()))
TN = (((0,), (0,)), ((), ()))


def _tile(n, pref):
    if n <= pref:
        return n
    t = pref - pref % 16
    while n % t:
        t -= 16
    return t


def _params():
    return pltpu.CompilerParams(vmem_limit_bytes=VMEM_LIMIT)


def _sds(shape, dtype):
    return jax.ShapeDtypeStruct(shape, dtype)


def _rows(tm, c):
    return pl.BlockSpec((tm, c), lambda i: (i, 0))


def _whole(shape):
    return pl.BlockSpec(shape, lambda *_: (0,) * len(shape))


def _resident(shape):
    return pl.BlockSpec(shape, lambda *_: (0,) * len(shape), pipeline_mode=pl.Buffered(1))


def _rms_fwd(x, g, tag):
    T, D = x.shape
    tm = _tile(T, ROW_TILE)

    def body(x_ref, g_ref, o_ref):
        xv = x_ref[...]
        r = lax.rsqrt(jnp.mean(xv * xv, axis=-1, keepdims=True) + RMS_EPS)
        o_ref[...] = (xv * r * g_ref[...]).astype(o_ref.dtype)

    return pl.pallas_call(
        body, name=f"rms_fwd_{tag}", grid=(T // tm,),
        in_specs=[_rows(tm, D), _whole((1, D))], out_specs=_rows(tm, D),
        out_shape=_sds((T, D), MM_DTYPE), compiler_params=_params())(x, g.reshape(1, D))


def _accumulate(ref, part, first):
    @pl.when(first)
    def _():
        ref[...] = part

    @pl.when(jnp.logical_not(first))
    def _():
        ref[...] += part


def _loss_grad(y, tgt):
    T, D = y.shape
    tm = _tile(T, ROW_TILE)

    def body(y_ref, t_ref, dy_ref, l_ref):
        e = y_ref[...] - t_ref[...]
        row = jnp.mean(e * e, axis=-1, keepdims=True)
        part = jnp.broadcast_to(jnp.sum(row, axis=0, keepdims=True), (8, LANES))
        _accumulate(l_ref, part, pl.program_id(0) == 0)
        dy_ref[...] = e * (1.0 / D)

    dy, lsum = pl.pallas_call(
        body, name="loss_grad", grid=(T // tm,),
        in_specs=[_rows(tm, D), _rows(tm, D)], out_specs=[_rows(tm, D), _whole((8, LANES))],
        out_shape=[_sds((T, D), F32), _sds((8, LANES), F32)], compiler_params=_params())(y, tgt)
    return dy, 0.5 * lsum[0, 0]


def _shift_down(u, d, rows):
    return jnp.where(rows >= d, pltpu.roll(u, d, 0), 0.0)


def _shift_up(u, d, rows, S):
    return jnp.where(rows < S - d, pltpu.roll(u, S - d, 0), 0.0)


def _conv_fwd(bch, k8, Bl, S):
    T, D3 = bch.shape
    D = D3 // 3
    dc = min(D, 2 * LANES)
    nd = D // dc

    def body(b_ref, c_ref, h_ref, k_ref, z_ref):
        rows = lax.broadcasted_iota(jnp.int32, (S, 1), 0)
        u = c_ref[...].astype(F32) * h_ref[...].astype(F32)
        y = k_ref[2:3, :] * u + k_ref[1:2, :] * _shift_down(u, 1, rows) + k_ref[0:1, :] * _shift_down(u, 2, rows)
        z_ref[...] = (b_ref[...].astype(F32) * y).astype(z_ref.dtype)

    return pl.pallas_call(
        body, name="conv_fwd", grid=(Bl, nd),
        in_specs=[pl.BlockSpec((S, dc), lambda b, j: (b, j)),
                  pl.BlockSpec((S, dc), lambda b, j: (b, nd + j)),
                  pl.BlockSpec((S, dc), lambda b, j: (b, 2 * nd + j)),
                  pl.BlockSpec((8, dc), lambda b, j: (0, j))],
        out_specs=pl.BlockSpec((S, dc), lambda b, j: (b, j)),
        out_shape=_sds((T, D), MM_DTYPE), compiler_params=_params())(bch, bch, bch, k8)


def _conv_bwd(bch, dz, k8, Bl, S):
    T, D3 = bch.shape
    D = D3 // 3
    dc = min(D, 2 * LANES)
    nd = D // dc

    def body(b_ref, c_ref, h_ref, dz_ref, k_ref, db_ref, dc_ref, dh_ref, dk_ref):
        rows = lax.broadcasted_iota(jnp.int32, (S, 1), 0)
        bv = b_ref[...].astype(F32)
        cv = c_ref[...].astype(F32)
        hv = h_ref[...].astype(F32)
        dzv = dz_ref[...].astype(F32)
        u = cv * hv
        u1 = _shift_down(u, 1, rows)
        u2 = _shift_down(u, 2, rows)
        y = k_ref[2:3, :] * u + k_ref[1:2, :] * u1 + k_ref[0:1, :] * u2
        db_ref[...] = (dzv * y).astype(db_ref.dtype)
        dy = dzv * bv
        du = k_ref[2:3, :] * dy + k_ref[1:2, :] * _shift_up(dy, 1, rows, S) + k_ref[0:1, :] * _shift_up(dy, 2, rows, S)
        dc_ref[...] = (du * hv).astype(dc_ref.dtype)
        dh_ref[...] = (du * cv).astype(dh_ref.dtype)

        @pl.when(pl.program_id(1) == 0)
        def _():
            dk_ref[...] = jnp.zeros_like(dk_ref)

        dk_ref[0:1, :] += jnp.sum(dy * u2, axis=0, keepdims=True)
        dk_ref[1:2, :] += jnp.sum(dy * u1, axis=0, keepdims=True)
        dk_ref[2:3, :] += jnp.sum(dy * u, axis=0, keepdims=True)

    seq = lambda off: pl.BlockSpec((S, dc), lambda j, b: (b, off + j))
    return pl.pallas_call(
        body, name="conv_bwd", grid=(nd, Bl),
        in_specs=[seq(0), seq(nd), seq(2 * nd), seq(0), pl.BlockSpec((8, dc), lambda j, b: (0, j))],
        out_specs=[seq(0), seq(0), seq(0), pl.BlockSpec((8, dc), lambda j, b: (0, j))],
        out_shape=[_sds((T, D), MM_DTYPE)] * 3 + [_sds((8, D), F32)],
        compiler_params=_params())(bch, bch, bch, dz, k8)


def _forget_fwd(pf, fb, Bl, S):
    T = pf.shape[0]

    def body(p_ref, fb_ref, c_ref):
        rows = lax.broadcasted_iota(jnp.int32, (S, 1), 0)
        z = p_ref[...] + fb_ref[...]
        acc = jnp.minimum(z, 0.0) - jnp.log1p(jnp.exp(-jnp.abs(z)))
        d = 1
        while d < S:
            acc = acc + _shift_down(acc, d, rows)
            d *= 2
        c_ref[...] = acc

    return pl.pallas_call(
        body, name="forget_fwd", grid=(Bl,),
        in_specs=[_rows(S, LANES), _whole((1, LANES))], out_specs=_rows(S, LANES),
        out_shape=_sds((T, LANES), F32), compiler_params=_params())(pf, fb)


def _forget_bwd(dc, pf, fb, Bl, S):
    T = pf.shape[0]

    def body(dc_ref, p_ref, fb_ref, df_ref, dfb_ref):
        rows = lax.broadcasted_iota(jnp.int32, (S, 1), 0)
        acc = dc_ref[...]
        d = 1
        while d < S:
            acc = acc + _shift_up(acc, d, rows, S)
            d *= 2
        df = acc * jax.nn.sigmoid(-(p_ref[...] + fb_ref[...]))
        df_ref[...] = df.astype(df_ref.dtype)
        _accumulate(dfb_ref, jnp.sum(df, axis=0, keepdims=True), pl.program_id(0) == 0)

    return pl.pallas_call(
        body, name="forget_bwd", grid=(Bl,),
        in_specs=[_rows(S, LANES), _rows(S, LANES), _whole((1, LANES))],
        out_specs=[_rows(S, LANES), _whole((1, LANES))],
        out_shape=[_sds((T, LANES), MM_DTYPE), _sds((1, LANES), F32)],
        compiler_params=_params())(dc, pf, fb)


def _head_mask(h):
    lane = lax.broadcasted_iota(jnp.int32, (1, LANES), 1)
    return (lane >= h * HEAD_DIM) & (lane < (h + 1) * HEAD_DIM)


def _attn_fwd(qg, kv, c_col, c_row, Bl, S, D):
    T = Bl * S
    H = D // HEAD_DIM
    HP = D // LANES
    bq = min(S, ATT_BLOCK)
    nq = S // bq
    scale = 1.0 / math.sqrt(HEAD_DIM)

    def body(q_ref, g_ref, k_ref, v_ref, cc_ref, cr_ref, o_ref, lse_ref, z_ref):
        i = pl.program_id(2)
        q2 = q_ref[...]
        qh = [q2 * (_head_mask(h).astype(F32) * scale).astype(q2.dtype) for h in range(2)]
        cc = [cc_ref[h][:, :1] for h in range(2)]
        diag = lax.broadcasted_iota(jnp.int32, (1, bq), 1) <= lax.broadcasted_iota(jnp.int32, (bq, 1), 0)

        def block(j, carry, on_diagonal):
            off = pl.multiple_of(j * bq, bq)
            kj = k_ref[pl.ds(off, bq), :]
            vj = v_ref[pl.ds(off, bq), :]
            new = []
            for h in range(2):
                m, l, acc = carry[h]
                s = lax.dot_general(qh[h], kj, NT, preferred_element_type=F32) + cc[h] - cr_ref[h, j]
                if on_diagonal:
                    s = jnp.where(diag, s, -jnp.inf)
                m_new = jnp.maximum(m, jnp.max(s, axis=1, keepdims=True))
                p = jnp.exp(s - m_new)
                a = jnp.exp(m - m_new)
                l = a * l + jnp.sum(p, axis=1, keepdims=True)
                acc = a * acc + jnp.dot(p.astype(MM_DTYPE), vj, preferred_element_type=F32)
                new.append((m_new, l, acc))
            return tuple(new)

        one = (jnp.full((bq, 1), -jnp.inf, F32), jnp.zeros((bq, 1), F32), jnp.zeros((bq, LANES), F32))
        carry = lax.fori_loop(0, i, lambda j, c: block(j, c, False), (one, one))
        carry = block(i, carry, True)
        outs = []
        for h in range(2):
            m, l, acc = carry[h]
            outs.append(acc / l)
            lse_ref[h] = jnp.broadcast_to(m + jnp.log(l), (bq, LANES))
        o2 = jnp.where(_head_mask(0), outs[0], outs[1])
        o_ref[...] = o2
        z_ref[...] = (jax.nn.sigmoid(g_ref[...].astype(F32)) * o2).astype(z_ref.dtype)

    return pl.pallas_call(
        body, name="attn_fwd", grid=(Bl, HP, nq),
        in_specs=[pl.BlockSpec((bq, LANES), lambda b, hp, i: (b * nq + i, hp)),
                  pl.BlockSpec((bq, LANES), lambda b, hp, i: (b * nq + i, HP + hp)),
                  pl.BlockSpec((S, LANES), lambda b, hp, i: (b, hp)),
                  pl.BlockSpec((S, LANES), lambda b, hp, i: (b, HP + hp)),
                  pl.BlockSpec((None, 2, bq, LANES), lambda b, hp, i: (b, hp, i, 0)),
                  pl.BlockSpec((None, 2, nq, 1, bq), lambda b, hp, i: (b, hp, 0, 0, 0))],
        out_specs=[pl.BlockSpec((bq, LANES), lambda b, hp, i: (b * nq + i, hp)),
                   pl.BlockSpec((None, 2, bq, LANES), lambda b, hp, i: (b, hp, i, 0)),
                   pl.BlockSpec((bq, LANES), lambda b, hp, i: (b * nq + i, hp))],
        out_shape=[_sds((T, D), F32), _sds((Bl, H, S, LANES), F32), _sds((T, D), MM_DTYPE)],
        compiler_params=_params())(qg, qg, kv, kv, c_col, c_row)


def _attn_bwd(qg, kv, dz, lse, c_col, c_row, Bl, S, D):
    T = Bl * S
    H = D // HEAD_DIM
    HP = D // LANES
    bq = min(S, ATT_BLOCK)
    nq = S // bq
    scale = 1.0 / math.sqrt(HEAD_DIM)

    def body(q_ref, g_ref, k_ref, v_ref, dz_ref, lse_ref, cc_ref, cr_ref, dq_ref, dk_ref, dv_ref, dcr_ref, p_sc, dp_sc):
        i = pl.program_id(2)

        @pl.when(i == 0)
        def _():
            dk_ref[...] = jnp.zeros_like(dk_ref)
            dv_ref[...] = jnp.zeros_like(dv_ref)
            dcr_ref[...] = jnp.zeros_like(dcr_ref)

        q2 = q_ref[...]
        do2 = (dz_ref[...].astype(F32) * jax.nn.sigmoid(g_ref[...].astype(F32))).astype(MM_DTYPE)
        masks = [_head_mask(h).astype(F32) for h in range(2)]
        qh = [q2 * (masks[h] * scale).astype(q2.dtype) for h in range(2)]
        doh = [do2 * masks[h].astype(do2.dtype) for h in range(2)]
        cc = [cc_ref[h][:, :1] for h in range(2)]
        lse = [lse_ref[h][:, :1] for h in range(2)]
        diag = lax.broadcasted_iota(jnp.int32, (1, bq), 1) <= lax.broadcasted_iota(jnp.int32, (bq, 1), 0)

        def sweep1(j, delta, on_diagonal):
            off = pl.multiple_of(j * bq, bq)
            kj = k_ref[pl.ds(off, bq), :]
            vj = v_ref[pl.ds(off, bq), :]
            new = []
            dv = None
            for h in range(2):
                s = lax.dot_general(qh[h], kj, NT, preferred_element_type=F32) + cc[h] - cr_ref[h, j]
                if on_diagonal:
                    s = jnp.where(diag, s, -jnp.inf)
                p = jnp.exp(s - lse[h])
                dp = lax.dot_general(doh[h], vj, NT, preferred_element_type=F32)
                p_sc[h, j] = p
                dp_sc[h, j] = dp
                part = lax.dot_general(p.astype(MM_DTYPE), doh[h], TN, preferred_element_type=F32)
                dv = part if dv is None else dv + part
                new.append(delta[h] + jnp.sum(p * dp, axis=1, keepdims=True))
            dv_ref[pl.ds(off, bq), :] += dv
            return tuple(new)

        zero = jnp.zeros((bq, 1), F32)
        delta = lax.fori_loop(0, i, lambda j, d: sweep1(j, d, False), (zero, zero))
        delta = sweep1(i, delta, True)

        def sweep2(j, dq):
            off = pl.multiple_of(j * bq, bq)
            kj = k_ref[pl.ds(off, bq), :]
            dk = None
            for h in range(2):
                ds = p_sc[h, j] * (dp_sc[h, j] - delta[h])
                dcr_ref[h, j] -= jnp.sum(ds, axis=0, keepdims=True)
                dsb = ds.astype(MM_DTYPE)
                dq = dq + jnp.dot(dsb, kj * (masks[h] * scale).astype(kj.dtype), preferred_element_type=F32)
                part = lax.dot_general(dsb, qh[h], TN, preferred_element_type=F32)
                dk = part if dk is None else dk + part
            dk_ref[pl.ds(off, bq), :] += dk
            return dq

        dq_ref[...] = lax.fori_loop(0, i + 1, sweep2, jnp.zeros((bq, LANES), F32))

    blk = lambda col: pl.BlockSpec((bq, LANES), lambda b, hp, i: (b * nq + i, col(hp)))
    seq = lambda col: pl.BlockSpec((S, LANES), lambda b, hp, i: (b, col(hp)))
    per_head = pl.BlockSpec((None, 2, bq, LANES), lambda b, hp, i: (b, hp, i, 0))
    rows = pl.BlockSpec((None, 2, nq, 1, bq), lambda b, hp, i: (b, hp, 0, 0, 0))
    return pl.pallas_call(
        body, name="attn_bwd", grid=(Bl, HP, nq),
        in_specs=[blk(lambda hp: hp), blk(lambda hp: HP + hp), seq(lambda hp: hp), seq(lambda hp: HP + hp),
                  blk(lambda hp: hp), per_head, per_head, rows],
        out_specs=[blk(lambda hp: hp), seq(lambda hp: hp), seq(lambda hp: hp), rows],
        out_shape=[_sds((T, D), F32), _sds((T, D), F32), _sds((T, D), F32), _sds((Bl, H, nq, 1, bq), F32)],
        scratch_shapes=[pltpu.VMEM((2, nq, bq, bq), F32), pltpu.VMEM((2, nq, bq, bq), F32)],
        compiler_params=_params())(qg, qg, kv, kv, dz, lse, c_col, c_row)


def _pack_dkv(dk, dv, dpf):
    T, D = dk.shape
    tm = _tile(T, ROW_TILE)

    def body(dk_ref, dv_ref, df_ref, o_ref):
        o_ref[:, :D] = dk_ref[...].astype(o_ref.dtype)
        o_ref[:, D:2 * D] = dv_ref[...].astype(o_ref.dtype)
        o_ref[:, 2 * D:] = df_ref[...]

    return pl.pallas_call(
        body, name="pack_dkv", grid=(T // tm,),
        in_specs=[_rows(tm, D), _rows(tm, D), _rows(tm, LANES)], out_specs=_rows(tm, 2 * D + LANES),
        out_shape=_sds((T, 2 * D + LANES), MM_DTYPE), compiler_params=_params())(dk, dv, dpf)


def _gate_bwd(dz, qg, o, dq):
    T, D = dz.shape
    tm = _tile(T, ROW_TILE)

    def body(dz_ref, g_ref, o_ref, dq_ref, out_ref):
        g = g_ref[...].astype(F32)
        sg = jax.nn.sigmoid(g)
        out_ref[:, :D] = dq_ref[...].astype(out_ref.dtype)
        out_ref[:, D:] = (dz_ref[...].astype(F32) * o_ref[...] * sg * (1.0 - sg)).astype(out_ref.dtype)

    return pl.pallas_call(
        body, name="gate_bwd", grid=(T // tm,),
        in_specs=[_rows(tm, D), pl.BlockSpec((tm, D), lambda i: (i, 1)), _rows(tm, D), _rows(tm, D)],
        out_specs=_rows(tm, 2 * D), out_shape=_sds((T, 2 * D), MM_DTYPE),
        compiler_params=_params())(dz, qg, o, dq)


def _mm_nn(a, b, out_dtype, tag):
    T, K = a.shape
    N = b.shape[1]
    tm = _tile(T, MM_TILE)

    def body(a_ref, b_ref, o_ref):
        o_ref[...] = jnp.dot(a_ref[...], b_ref[...], preferred_element_type=F32).astype(o_ref.dtype)

    return pl.pallas_call(
        body, name=f"mm_nn_{tag}", grid=(T // tm,),
        in_specs=[_rows(tm, K), _whole((K, N))], out_specs=_rows(tm, N),
        out_shape=_sds((T, N), out_dtype), compiler_params=_params())(a, b)


def _mm_tn_in(a, dy, tag, after=None):
    T, K = a.shape
    n = dy.shape[1] // N_CHIP
    tt = _tile(T, WG_TILE)
    whole_t = tt == T
    extra = [] if after is None else [after]

    def body(a_ref, d_ref, *rest):
        part = lax.dot_general(a_ref[...], d_ref[...], TN, preferred_element_type=F32)
        if whole_t:
            rest[-1][...] = part
        else:
            _accumulate(rest[-1], part, pl.program_id(1) == 0)

    a_spec = _resident((T, K)) if whole_t else pl.BlockSpec((tt, K), lambda s, t: (t, 0))
    return pl.pallas_call(
        body, name=f"mm_tn_in_{tag}", grid=(N_CHIP, T // tt),
        in_specs=[a_spec, pl.BlockSpec((tt, n), lambda s, t: (t, s))] + [ANY] * len(extra),
        out_specs=pl.BlockSpec((None, K, n), lambda s, t: (s, 0, 0)), out_shape=_sds((N_CHIP, K, n), F32),
        compiler_params=_params())(a, dy, *extra)


def _mm_tn_out(act, dh, tag, after=None):
    T, R4 = act.shape
    D = dh.shape[1]
    r = R4 // N_CHIP
    g = 1 if r % LANES == 0 else 2
    tt = _tile(T, WG_TILE)
    whole_t = tt == T
    extra = [] if after is None else [after]

    def body(a_ref, d_ref, *rest):
        o_ref = rest[-1]
        part = lax.dot_general(a_ref[...], d_ref[...], TN, preferred_element_type=F32)
        first = pl.program_id(1) == 0
        for q in range(g):
            if whole_t:
                o_ref[q] = part[q * r:(q + 1) * r]
            else:
                _accumulate(o_ref.at[q], part[q * r:(q + 1) * r], first)

    d_spec = _resident((T, D)) if whole_t else pl.BlockSpec((tt, D), lambda s, t: (t, 0))
    return pl.pallas_call(
        body, name=f"mm_tn_out_{tag}", grid=(N_CHIP // g, T // tt),
        in_specs=[pl.BlockSpec((tt, g * r), lambda s, t: (t, s)), d_spec] + [ANY] * len(extra),
        out_specs=pl.BlockSpec((g, r, D), lambda s, t: (s, 0, 0)), out_shape=_sds((N_CHIP, r, D), F32),
        compiler_params=_params())(act, dh, *extra)


def _mm_tn(a, b, tag):
    T, K = a.shape
    N = b.shape[1]
    tt = _tile(T, TN_TILE)

    def body(a_ref, b_ref, o_ref):
        part = lax.dot_general(a_ref[...], b_ref[...], TN, preferred_element_type=F32)
        _accumulate(o_ref, part, pl.program_id(0) == 0)

    return pl.pallas_call(
        body, name=f"mm_tn_{tag}", grid=(T // tt,),
        in_specs=[_rows(tt, K), _rows(tt, N)], out_specs=_whole((K, N)),
        out_shape=_sds((K, N), F32), compiler_params=_params())(a, b)


def _step_partial(ref, part):
    ref[...] = jnp.where(lax.broadcasted_iota(jnp.int32, ref.shape, 0) == 0, part, 0.0)


def _pack_small(partials, tail, at):
    n = len(partials)
    D = tail.shape[1]

    def body(*refs):
        o_ref = refs[-1]
        o_ref[...] = jnp.zeros_like(o_ref)
        for i in range(n):
            o_ref[i:i + 1, :] = jnp.sum(refs[i][...], axis=0, keepdims=True)
        o_ref[at:at + tail.shape[0], :] = refs[n][...]

    return pl.pallas_call(
        body, name="pack_small", grid=(1,),
        in_specs=[_whole(p.shape) for p in partials] + [_whole(tail.shape)], out_specs=_whole((SMALL_ROWS, D)),
        out_shape=_sds((SMALL_ROWS, D), F32), compiler_params=_params())(*partials, tail)


def _norm_mm_in(x, g, wg, tag, swiglu=False):
    T, D = x.shape
    n = wg.shape[-1]
    tm = _tile(T, FUSED_TILE)
    half = N_CHIP // 2

    def body(x_ref, g_ref, w_ref, xn_ref, y_ref, *rest):
        xv = x_ref[...]
        r = lax.rsqrt(jnp.mean(xv * xv, axis=-1, keepdims=True) + RMS_EPS)
        xn = (xv * r * g_ref[...]).astype(xn_ref.dtype)
        xn_ref[...] = xn

        def product(s):
            p = jnp.dot(xn, w_ref[s], preferred_element_type=F32)
            y_ref[:, s * n:(s + 1) * n] = p.astype(y_ref.dtype)
            return p

        if swiglu:
            for q in range(half):
                gate, up = product(q), product(half + q)
                rest[0][:, q * n:(q + 1) * n] = (gate * jax.nn.sigmoid(gate) * up).astype(rest[0].dtype)
        else:
            for s in range(N_CHIP):
                product(s)

    out_specs = [_rows(tm, D), _rows(tm, N_CHIP * n)]
    out_shape = [_sds((T, D), MM_DTYPE), _sds((T, N_CHIP * n), MM_DTYPE)]
    if swiglu:
        out_specs.append(_rows(tm, half * n))
        out_shape.append(_sds((T, half * n), MM_DTYPE))
    return pl.pallas_call(
        body, name=f"norm_mm_in_{tag}", grid=(T // tm,),
        in_specs=[_rows(tm, D), _whole((1, D)), _resident((N_CHIP, D, n))], out_specs=out_specs,
        out_shape=out_shape, compiler_params=_params())(x, g.reshape(1, D), wg)


def _mm_out_post(a, b, x, g, alpha, tag):
    T, K = a.shape
    D = b.shape[1]
    tm = _tile(T, FUSED_TILE)

    def body(a_ref, b_ref, x_ref, g_ref, h_ref, o_ref):
        hv = jnp.dot(a_ref[...], b_ref[...], preferred_element_type=F32)
        h_ref[...] = hv
        r = lax.rsqrt(jnp.mean(hv * hv, axis=-1, keepdims=True) + RMS_EPS)
        o_ref[...] = x_ref[...] + alpha * (hv * r * g_ref[...])

    return pl.pallas_call(
        body, name=f"mm_out_post_{tag}", grid=(T // tm,),
        in_specs=[_rows(tm, K), _resident((K, D)), _rows(tm, D), _whole((1, D))],
        out_specs=[_rows(tm, D), _rows(tm, D)], out_shape=[_sds((T, D), F32)] * 2,
        compiler_params=_params())(a, b, x, g.reshape(1, D))


def _post_bwd_mm(dx, h, g, alpha, b, tag, hgu=None):
    T, D = dx.shape
    K = b.shape[0]
    tm = _tile(T, FUSED_TILE)

    def body(dx_ref, h_ref, g_ref, b_ref, *rest):
        dh_ref, dg_ref, out_ref = rest[-3:]
        hv = h_ref[...]
        r = lax.rsqrt(jnp.mean(hv * hv, axis=-1, keepdims=True) + RMS_EPS)
        hh = hv * r
        dyn = alpha * dx_ref[...]
        _step_partial(dg_ref, jnp.sum(dyn * hh, axis=0, keepdims=True))
        dhh = dyn * g_ref[...]
        dh = (r * (dhh - hh * jnp.mean(dhh * hh, axis=-1, keepdims=True))).astype(dh_ref.dtype)
        dh_ref[...] = dh
        da = lax.dot_general(dh, b_ref[...], NT, preferred_element_type=F32)
        if hgu is None:
            out_ref[...] = da.astype(out_ref.dtype)
        else:
            gate = rest[0][:, :K]
            up = rest[0][:, K:]
            dab = da.astype(gate.dtype)
            sg = jax.nn.sigmoid(gate)
            out_ref[:, :K] = (dab * up * sg * (1.0 + gate * (1.0 - sg))).astype(out_ref.dtype)
            out_ref[:, K:] = (dab * gate * sg).astype(out_ref.dtype)

    in_specs = [_rows(tm, D), _rows(tm, D), _whole((1, D)), _resident((K, D))]
    args = [dx, h, g.reshape(1, D), b]
    wide = K
    if hgu is not None:
        wide = 2 * K
        in_specs.append(_rows(tm, wide))
        args.append(hgu)
    dh, dg, da = pl.pallas_call(
        body, name=f"post_bwd_mm_{tag}", grid=(T // tm,), in_specs=in_specs,
        out_specs=[_rows(tm, D), _rows(8, D), _rows(tm, wide)],
        out_shape=[_sds((T, D), MM_DTYPE), _sds((8 * (T // tm), D), F32), _sds((T, wide), MM_DTYPE)],
        compiler_params=_params())(*args)
    return dh, dg, da


def _mm_nt_pre(dy, w, dres, x, g, tag):
    T, C = dy.shape
    D = x.shape[1]
    tm = _tile(T, FUSED_TILE)
    n = w.shape[-1]

    def body(dy_ref, w_ref, dres_ref, x_ref, g_ref, dx_ref, dg_ref):
        if w.ndim == 2:
            dn = lax.dot_general(dy_ref[...], w_ref[...], NT, preferred_element_type=F32)
        else:
            dn = None
            for s in range(N_CHIP):
                part = lax.dot_general(dy_ref[:, s * n:(s + 1) * n], w_ref[s], NT, preferred_element_type=F32)
                dn = part if dn is None else dn + part
        xv = x_ref[...]
        r = lax.rsqrt(jnp.mean(xv * xv, axis=-1, keepdims=True) + RMS_EPS)
        xh = xv * r
        _step_partial(dg_ref, jnp.sum(dn * xh, axis=0, keepdims=True))
        dxh = dn * g_ref[...]
        dx_ref[...] = dres_ref[...] + r * (dxh - xh * jnp.mean(dxh * xh, axis=-1, keepdims=True))

    dx, dg = pl.pallas_call(
        body, name=f"mm_nt_pre_{tag}", grid=(T // tm,),
        in_specs=[_rows(tm, C), _resident(w.shape), _rows(tm, D), _rows(tm, D), _whole((1, D))],
        out_specs=[_rows(tm, D), _rows(8, D)], out_shape=[_sds((T, D), F32), _sds((8 * (T // tm), D), F32)],
        compiler_params=_params())(dy, w, dres, x, g.reshape(1, D))
    return dx, dg


def _adamw(w, g, m, v, tag, after=None):
    R, C = w.shape
    tr = _tile(R, ROW_TILE)
    extra = [] if after is None else [after]

    def body(w_ref, g_ref, m_ref, v_ref, *rest):
        go_ref, d_ref, mo_ref, vo_ref = rest[-4:]
        gv = g_ref[...]
        go_ref[...] = gv
        mn = ADAM_B1 * m_ref[...] + (1.0 - ADAM_B1) * gv
        vn = ADAM_B2 * v_ref[...] + (1.0 - ADAM_B2) * (gv * gv)
        m_hat = mn / (1.0 - ADAM_B1 ** ADAM_STEP)
        v_hat = vn / (1.0 - ADAM_B2 ** ADAM_STEP)
        d_ref[...] = -ADAM_LR * (m_hat / (jnp.sqrt(v_hat) + ADAM_EPS) + ADAM_WD * w_ref[...])
        mo_ref[...] = mn
        vo_ref[...] = vn

    return pl.pallas_call(
        body, name=f"adamw_{tag}", grid=(R // tr,),
        in_specs=[_rows(tr, C)] * 4 + [ANY] * len(extra), out_specs=[_rows(tr, C)] * 4,
        out_shape=[_sds((R, C), F32)] * 4, compiler_params=_params())(w, g, m, v, *extra)


def _sum_devices(gall, own, place):
    _, R, C = gall.shape

    def body(place_ref, g_ref, s_ref, o_ref):
        me = 2 * place_ref[1] + place_ref[0]
        acc = None
        for d in range(N_DEV):
            term = jnp.where(me == d, s_ref[...], g_ref[d])
            acc = term if acc is None else acc + term
        o_ref[...] = acc

    grid_spec = pltpu.PrefetchScalarGridSpec(
        num_scalar_prefetch=1, grid=(1,),
        in_specs=[pl.BlockSpec((N_DEV, R, C), lambda i, p: (0, 0, 0)), pl.BlockSpec((R, C), lambda i, p: (0, 0))],
        out_specs=pl.BlockSpec((R, C), lambda i, p: (0, 0)))
    return pl.pallas_call(
        body, name="sum_devices", grid_spec=grid_spec, out_shape=_sds((R, C), F32),
        compiler_params=_params())(place, gall, own)


HBM = pl.BlockSpec(memory_space=pltpu.HBM)
SEM = pl.BlockSpec(memory_space=pltpu.SEMAPHORE)
EFFECT = pltpu.SideEffectType.DATAFLOW_SIDE_EFFECTING


def _place():
    x, y, c = lax.axis_index("x"), lax.axis_index("y"), lax.axis_index("c")
    chips = ((1 - x, y), (x, 1 - y), (1 - x, 1 - y))
    return x, y, c, chips


def _remote(src, dst, send_sem, recv_sem, dev):
    return pltpu.make_async_remote_copy(src_ref=src, dst_ref=dst, send_sem=send_sem, recv_sem=recv_sem,
                                        device_id=dev, device_id_type=MESH)


def _in_hbm(a):
    return pltpu.with_memory_space_constraint(a, pltpu.HBM)


def _own_slot(w4, l, dtype, place, tag):
    _, _, r, col = w4.shape
    tr = _tile(r, 2 * ROW_TILE)

    def body(place_ref, x_ref, o_ref):
        o_ref[...] = x_ref[...].astype(o_ref.dtype)

    grid_spec = pltpu.PrefetchScalarGridSpec(
        num_scalar_prefetch=1, grid=(2, r // tr),
        in_specs=[pl.BlockSpec((None, None, tr, col), lambda h, i, p: (l, h, i, 0))],
        out_specs=pl.BlockSpec((None, None, tr, col), lambda h, i, p: (p[1], h, i, 0)))
    return pl.pallas_call(
        body, name=f"own_slot_{tag}", grid_spec=grid_spec, out_shape=_sds((N_CHIP, 2, r, col), dtype),
        compiler_params=_params())(place, w4)


def _gather_start(bufs, after, tag):
    n = len(bufs)

    def body(*refs):
        ins = refs[:n]
        s_sem, r_sem, token = refs[n + 1], refs[n + 2], refs[2 * n + 3]
        x, y, c, chips = _place()
        me = 2 * x + y
        for i in range(n):
            mine = ins[i].at[me, c]
            for j, (px, py) in enumerate(chips):
                _remote(mine, mine, s_sem.at[3 * i + j], r_sem.at[3 * i + j], (px, py, c)).start()
        token[...] = jnp.zeros_like(token)

    dma = pltpu.SemaphoreType.DMA
    res = pl.pallas_call(
        body, name=f"gather_start_{tag}", in_specs=[HBM] * n + [ANY],
        out_specs=[SEM, SEM] + [HBM] * n + [pl.BlockSpec(memory_space=pltpu.VMEM)],
        out_shape=[dma((3 * n,)), dma((3 * n,))] + [pltpu.HBM(b.shape, b.dtype) for b in bufs] + [_sds((8, LANES), F32)],
        input_output_aliases={i: i + 2 for i in range(n)},
        compiler_params=pltpu.CompilerParams(has_side_effects=EFFECT),
        )(*[_in_hbm(b) for b in bufs], after)
    return res[0], res[1], list(res[2:2 + n]), res[-1]


def _gather_pass(s_sem, r_sem, bufs, first, after, tag):
    n = len(bufs)

    def body(*refs):
        ins = refs[:n]
        a_s, a_r, b_s, b_r = refs[n], refs[n + 1], refs[n + 3], refs[n + 4]
        x, y, c, chips = _place()
        me = 2 * x + y
        sib = (x, y, 1 - c)
        for i in range(n):
            mine = ins[i].at[me, c]
            for j, (px, py) in enumerate(chips):
                k = 3 * (first + i) + j
                _remote(mine, mine, a_s.at[k], a_r.at[k], (px, py, c)).wait_send()
        for j, (px, py) in enumerate(chips):
            for i in range(n):
                k = 3 * (first + i) + j
                blk = ins[i].at[2 * px + py, c]
                _remote(blk, blk, a_s.at[k], a_r.at[k], (px, py, c)).wait_recv()
                _remote(blk, blk, b_s.at[3 * i + j], b_r.at[3 * i + j], sib).start()

    dma = pltpu.SemaphoreType.DMA
    res = pl.pallas_call(
        body, name=f"gather_pass_{tag}", in_specs=[HBM] * n + [SEM, SEM, ANY],
        out_specs=[SEM, SEM] + [HBM] * n,
        out_shape=[dma((3 * n,)), dma((3 * n,))] + [pltpu.HBM(b.shape, b.dtype) for b in bufs],
        input_output_aliases={i: i + 2 for i in range(n)},
        compiler_params=pltpu.CompilerParams(has_side_effects=EFFECT),
        )(*bufs, s_sem, r_sem, after)
    return res[0], res[1], list(res[2:])


def _gather_land(s_sem, r_sem, bufs, tag):
    n = len(bufs)

    def body(*refs):
        ins = refs[:n]
        b_s, b_r = refs[n], refs[n + 1]
        x, y, c, chips = _place()
        sib = (x, y, 1 - c)
        for j, (px, py) in enumerate(chips):
            for i in range(n):
                sent = ins[i].at[2 * px + py, c]
                got = ins[i].at[2 * px + py, 1 - c]
                _remote(sent, sent, b_s.at[3 * i + j], b_r.at[3 * i + j], sib).wait_send()
                _remote(got, got, b_s.at[3 * i + j], b_r.at[3 * i + j], sib).wait_recv()

    return list(pl.pallas_call(
        body, name=f"gather_land_{tag}", in_specs=[HBM] * n + [SEM, SEM], out_specs=[HBM] * n,
        out_shape=[pltpu.HBM(b.shape, b.dtype) for b in bufs],
        input_output_aliases={i: i for i in range(n)},
        compiler_params=pltpu.CompilerParams(has_side_effects=EFFECT),
        )(*bufs, s_sem, r_sem))


def _rs_pair_add(g, recv, place, tag):
    r, col = g.shape[-2:]
    tr = _tile(r, ROW_TILE)

    def body(place_ref, g_ref, r_ref, wire_ref, own_ref):
        tot = g_ref[...] + r_ref[...]
        wire_ref[...] = tot.astype(wire_ref.dtype)

        @pl.when(pl.program_id(1) == place_ref[1])
        def _():
            own_ref[...] = tot

    grid_spec = pltpu.PrefetchScalarGridSpec(
        num_scalar_prefetch=1, grid=(r // tr, N_CHIP),
        in_specs=[pl.BlockSpec((None, None, tr, col), lambda i, s, p: (s, p[0], i, 0)),
                  pl.BlockSpec((None, tr, col), lambda i, s, p: (s, i, 0))],
        out_specs=[pl.BlockSpec((None, tr, col), lambda i, s, p: (s, i, 0)),
                   pl.BlockSpec((tr, col), lambda i, s, p: (i, 0))])
    return pl.pallas_call(
        body, name=f"rs_pair_add_{tag}", grid_spec=grid_spec,
        out_shape=[_sds((N_CHIP, r, col), WIRE_DTYPE), _sds((r, col), F32)],
        compiler_params=_params())(place, g, recv)


def _pair_plan(srcs, lands):
    x, y, c, _ = _place()
    return [(s.at[:, 1 - c], l, (x, y, 1 - c)) for s, l in zip(srcs, lands)]


def _chip_plan(srcs, lands):
    x, y, c, chips = _place()
    plan = []
    for s, l in zip(srcs, lands):
        if len(s.shape) == 2:
            me = 4 * x + 2 * y + c
            plan += [(s, l.at[me], (x ^ (k >> 2), y ^ ((k >> 1) & 1), c ^ (k & 1))) for k in range(1, N_DEV)]
        else:
            plan += [(s.at[2 * px + py], l.at[j], (px, py, c)) for j, (px, py) in enumerate(chips)]
    return plan


def _exchange_start(srcs, lands, plan, count, tag):
    n = len(srcs)
    both = list(srcs) + list(lands)

    def body(*refs):
        s_sem, r_sem, token = refs[2 * n], refs[2 * n + 1], refs[4 * n + 2]
        for k, (src, dst, dev) in enumerate(plan(refs[:n], refs[n:2 * n])):
            _remote(src, dst, s_sem.at[k], r_sem.at[k], dev).start()
        token[...] = jnp.zeros_like(token)

    dma = pltpu.SemaphoreType.DMA
    res = pl.pallas_call(
        body, name=f"exchange_start_{tag}", in_specs=[HBM] * (2 * n),
        out_specs=[SEM, SEM] + [HBM] * (2 * n) + [pl.BlockSpec(memory_space=pltpu.VMEM)],
        out_shape=[dma((count,)), dma((count,))] + [pltpu.HBM(b.shape, b.dtype) for b in both] + [_sds((8, LANES), F32)],
        input_output_aliases={i: i + 2 for i in range(2 * n)},
        compiler_params=pltpu.CompilerParams(has_side_effects=EFFECT),
        )(*[_in_hbm(b) for b in both])
    return res[0], res[1], list(res[2:2 + n]), list(res[2 + n:2 + 2 * n]), res[-1]


def _exchange_wait(s_sem, r_sem, srcs, lands, plan, after, tag):
    n = len(srcs)

    def body(*refs):
        s_ref, r_ref = refs[2 * n], refs[2 * n + 1]
        for k, (src, dst, dev) in enumerate(plan(refs[:n], refs[n:2 * n])):
            cp = _remote(src, dst, s_ref.at[k], r_ref.at[k], dev)
            cp.wait_send()
            cp.wait_recv()

    both = list(srcs) + list(lands)
    res = pl.pallas_call(
        body, name=f"exchange_wait_{tag}", in_specs=[HBM] * (2 * n) + [SEM, SEM, ANY], out_specs=[HBM] * (2 * n),
        out_shape=[pltpu.HBM(b.shape, b.dtype) for b in both],
        input_output_aliases={i: i for i in range(2 * n)},
        compiler_params=pltpu.CompilerParams(has_side_effects=EFFECT),
        )(*both, s_sem, r_sem, after)
    return list(res[:n]), list(res[n:])


def _rs_chip_add(own, recv, place, l, L, prev, tag):
    r, col = own.shape
    tr = _tile(r, ROW_TILE)

    def body(place_ref, o_ref, r_ref, *rest):
        acc = o_ref[...]
        for j in range(3):
            acc = acc + r_ref[j].astype(F32)
        rest[-1][...] = acc

    in_specs = [pl.BlockSpec((tr, col), lambda i, p: (i, 0)), pl.BlockSpec((3, tr, col), lambda i, p: (0, i, 0))]
    args = [place, own, recv]
    kw = {}
    if prev is not None:
        in_specs.append(ANY)
        args.append(prev)
        kw["input_output_aliases"] = {3: 0}
    grid_spec = pltpu.PrefetchScalarGridSpec(
        num_scalar_prefetch=1, grid=(r // tr,), in_specs=in_specs,
        out_specs=pl.BlockSpec((None, None, tr, col), lambda i, p: (l, p[0], i, 0)))
    return pl.pallas_call(
        body, name=f"rs_chip_add_{tag}", grid_spec=grid_spec, out_shape=_sds((L, 2, r, col), F32),
        compiler_params=_params(), **kw)(*args)


def _rs_pair_share(fulls, tag):
    n = len(fulls)

    def body(*refs):
        outs = refs[n:2 * n]
        s_sem, r_sem = refs[2 * n:]
        x, y, c, _ = _place()
        sib = (x, y, 1 - c)
        started = []
        for i in range(n):
            cp = _remote(outs[i].at[:, c], outs[i].at[:, c], s_sem.at[i], r_sem.at[i], sib)
            cp.start()
            started.append(cp)
        for i, cp in enumerate(started):
            cp.wait_send()
            _remote(outs[i].at[:, 1 - c], outs[i].at[:, 1 - c], s_sem.at[i], r_sem.at[i], sib).wait_recv()

    dma = pltpu.SemaphoreType.DMA
    return pl.pallas_call(
        body, name=f"rs_pair_share_{tag}", in_specs=[ANY] * n, out_specs=[ANY] * n,
        out_shape=[_sds(f.shape, f.dtype) for f in fulls],
        input_output_aliases={i: i for i in range(n)},
        scratch_shapes=[dma((n,)), dma((n,))],
        )(*fulls)


def _ffn_fwd(x, g_pre, g_post, w_in, w_out, tag):
    xn, hgu, act = _norm_mm_in(x, g_pre, w_in, tag, swiglu=True)
    if callable(w_out):
        w_out = w_out(act)
    h, x_out = _mm_out_post(act, w_out.reshape(-1, w_out.shape[-1]), x, g_post, 0.5, tag)
    return x_out, (x, xn, hgu, act, h)


def _ffn_bwd(dx, saved, g_pre, g_post, w_in, w_out, tag, between=None, mid=None, finish=None):
    x, xn, hgu, act, h = saved
    dh, dg_post, dhgu = _post_bwd_mm(dx, h, g_post, 0.5, w_out.reshape(-1, w_out.shape[-1]), tag, hgu=hgu)
    token = None
    if between is not None:
        token = between(dhgu)
    dw_out = _mm_tn_out(act, dh, tag, after=token)
    if mid is not None:
        token = mid(dw_out)
    dw_in = _mm_tn_in(xn, dhgu, tag, after=token)
    if finish is not None:
        token = finish(dw_in)
    if token is not None:
        g_pre = g_pre + token[0, :1]
    dx_in, dg_pre = _mm_nt_pre(dhgu, w_in, dx, x, g_pre, tag)
    return dx_in, dg_pre, dg_post, dw_in, dw_out


def kernel(x, ffn1_pre_g, ffn1_post_g, ffn1_w_in, ffn1_w_out, mix_pre_g, mix_post_g, ffn2_pre_g, ffn2_post_g, ffn2_w_in, ffn2_w_out, conv_w_in, conv_k, conv_w_out, kv_g, kv_w, forget_b, attn_w_qg, attn_w_o, loss_target, m_ffn1_pre_g, m_ffn1_post_g, m_ffn1_w_in, m_ffn1_w_out, m_mix_pre_g, m_mix_post_g, m_ffn2_pre_g, m_ffn2_post_g, m_ffn2_w_in, m_ffn2_w_out, m_conv_w_in, m_conv_k, m_conv_w_out, m_kv_g, m_kv_w, m_forget_b, m_attn_w_qg, m_attn_w_o, v_ffn1_pre_g, v_ffn1_post_g, v_ffn1_w_in, v_ffn1_w_out, v_mix_pre_g, v_mix_post_g, v_ffn2_pre_g, v_ffn2_post_g, v_ffn2_w_in, v_ffn2_w_out, v_conv_w_in, v_conv_k, v_conv_w_out, v_kv_g, v_kv_w, v_forget_b, v_attn_w_qg, v_attn_w_o):
    Bl, S, D = x.shape
    T = Bl * S
    H = forget_b.shape[0]
    assert D == H * HEAD_DIM and D % LANES == 0
    kvc = kv_w.shape[1]
    kvp = -(-kvc // LANES) * LANES
    kv_all = 2 * D + LANES
    dk_cols = conv_k.shape[2]
    chip = 2 * lax.axis_index("x") + lax.axis_index("y")
    core = lax.axis_index("c")

    given = dict(ffn1_w_in=(ffn1_w_in, m_ffn1_w_in, v_ffn1_w_in), ffn1_w_out=(ffn1_w_out, m_ffn1_w_out, v_ffn1_w_out),
                 ffn2_w_in=(ffn2_w_in, m_ffn2_w_in, v_ffn2_w_in), ffn2_w_out=(ffn2_w_out, m_ffn2_w_out, v_ffn2_w_out),
                 conv_w_in=(conv_w_in, m_conv_w_in, v_conv_w_in), conv_w_out=(conv_w_out, m_conv_w_out, v_conv_w_out),
                 kv_w=(kv_w, m_kv_w, v_kv_w), attn_w_qg=(attn_w_qg, m_attn_w_qg, v_attn_w_qg),
                 attn_w_o=(attn_w_o, m_attn_w_o, v_attn_w_o))
    shards = {k: w for k, (w, _, _) in given.items()}
    shards["kv_w"] = jnp.pad(kv_w, ((0, 0), (0, kvp - kvc)))[None]
    groups = [[("ffn1_w_in", 0), ("ffn1_w_out", 0)], [("conv_w_in", 0), ("conv_w_out", 0)],
              [("ffn2_w_in", 0), ("ffn2_w_out", 0)], [("kv_w", 0), ("ffn1_w_in", 1), ("ffn1_w_out", 1)],
              [("attn_w_qg", 0), ("attn_w_o", 0), ("ffn2_w_in", 1), ("ffn2_w_out", 1)]]
    second = groups[3] + groups[4]
    place = jnp.stack([core, chip]).astype(jnp.int32)

    def slot(key, where):
        w = shards[key[0]]
        L, r, col = w.shape
        return _own_slot(w.reshape(L, 2, r // 2, col), key[1], MM_DTYPE, where, f"{key[0]}{key[1]}")

    def whole(g):
        return g.reshape(N_CHIP, -1, g.shape[-1])

    taps_slot = _own_slot(jnp.pad(conv_k[0], ((0, 13), (0, 0))).reshape(1, 2, 8, dk_cols), 0, F32, place, "conv_k")
    fb = jnp.pad(forget_b, (0, LANES - H)).reshape(1, LANES)
    w_in0, w_out0 = groups[0]
    s_0, r_0, fly_0, token = _gather_start([slot(w_in0, place), taps_slot, slot(w_out0, place)], fb, "first")
    later = groups[1] + groups[2] + groups[3] + groups[4]
    s_1, r_1, fly_1, token = _gather_start([slot(key, place) for key in later], token, "rest")
    W = {}

    def land(sems, bufs, lo, after, tag):
        return _gather_land(*_gather_pass(*sems, bufs, lo, after, tag), tag)

    def arrive(g, after):
        lo = sum(len(groups[k]) for k in range(1, g))
        got = land((s_1, r_1), fly_1[lo:lo + len(groups[g])], lo, after, f"g{g}")
        W.update({key: whole(b) for key, b in zip(groups[g], got)})

    w_first, taps = land((s_0, r_0), fly_0[:2], 0, token, "g0")
    k_taps = taps.reshape(N_CHIP, 16, dk_cols).transpose(1, 0, 2).reshape(16, D)[:8]

    x0 = x.reshape(T, D)
    W[w_in0] = whole(w_first)

    def first_w_out(act):
        W[w_out0] = whole(land((s_0, r_0), fly_0[2:], 2, act, "g0_out")[0])
        return W[w_out0]

    x1, s_f1a = _ffn_fwd(x0, ffn1_pre_g[0], ffn1_post_g[0], W[w_in0], first_w_out, "l0f1")
    arrive(1, x1)
    w_o_conv = W["conv_w_out", 0].reshape(D, D)
    xn_c, bch = _norm_mm_in(x1, mix_pre_g[0], W["conv_w_in", 0], "conv")
    z_c = _conv_fwd(bch, k_taps, Bl, S)
    m_c, x2 = _mm_out_post(z_c, w_o_conv, x1, mix_post_g[0], 1.0, "conv_out")
    arrive(2, x2)
    x3, s_f2a = _ffn_fwd(x2, ffn2_pre_g[0], ffn2_post_g[0], W["ffn2_w_in", 0], W["ffn2_w_out", 0], "l0f2")

    arrive(3, x3)
    kv_full = jnp.concatenate([W["kv_w", 0][s, :, :kvc] for s in range(N_CHIP)], axis=1)
    kv_full = jnp.pad(kv_full, ((0, 0), (0, kv_all - kv_full.shape[1])))
    xn_kv = _rms_fwd(x3, kv_g, "kv")
    kvact = _mm_nn(xn_kv, kv_full[:, :2 * D], MM_DTYPE, "kv")
    pf = _mm_nn(xn_kv, kv_full[:, 2 * D:], F32, "forget")
    cum = _forget_fwd(pf, fb, Bl, S)
    bq = min(S, ATT_BLOCK)
    c3 = cum.reshape(Bl, S, LANES)[:, :, :H].transpose(0, 2, 1)
    c_col = jnp.broadcast_to(c3[..., None], (Bl, H, S, LANES))
    c_row = c3.reshape(Bl, H, S // bq, 1, bq)

    x4, s_f1b = _ffn_fwd(x3, ffn1_pre_g[1], ffn1_post_g[1], W["ffn1_w_in", 1], W["ffn1_w_out", 1], "l1f1")
    arrive(4, x4)
    w_o_attn = W["attn_w_o", 0].reshape(D, D)
    xn_a, qg = _norm_mm_in(x4, mix_pre_g[1], W["attn_w_qg", 0], "qg")
    o, lse, z_a = _attn_fwd(qg, kvact, c_col, c_row, Bl, S, D)
    m_a, x5 = _mm_out_post(z_a, w_o_attn, x4, mix_post_g[1], 1.0, "attn_out")
    x6, s_f2b = _ffn_fwd(x5, ffn2_pre_g[1], ffn2_post_g[1], W["ffn2_w_in", 1], W["ffn2_w_out", 1], "l1f2")

    dy, loss_local = _loss_grad(x6, loss_target.reshape(T, D))

    G = {}
    dx5, dg_f2pre_1, dg_f2post_1, G["ffn2_w_in", 1], G["ffn2_w_out", 1] = _ffn_bwd(
        dy, s_f2b, ffn2_pre_g[1], ffn2_post_g[1], W["ffn2_w_in", 1], W["ffn2_w_out", 1], "l1f2")
    dm_a, dg_mixpost_1, dz_a = _post_bwd_mm(dx5, m_a, mix_post_g[1], 1.0, w_o_attn, "attn_out")
    G["attn_w_o", 0] = _mm_tn_out(z_a, dm_a, "attn_out")
    dq, dk, dv, dcr = _attn_bwd(qg, kvact, dz_a, lse, c_col, c_row, Bl, S, D)
    dqg = _gate_bwd(dz_a, qg, o, dq)
    G["attn_w_qg", 0] = _mm_tn_in(xn_a, dqg, "qg")
    dx4, dg_mixpre_1 = _mm_nt_pre(dqg, W["attn_w_qg", 0], dx5, x4, mix_pre_g[1], "qg")
    dx3, dg_f1pre_1, dg_f1post_1, G["ffn1_w_in", 1], G["ffn1_w_out", 1] = _ffn_bwd(
        dx4, s_f1b, ffn1_pre_g[1], ffn1_post_g[1], W["ffn1_w_in", 1], W["ffn1_w_out", 1], "l1f1")

    dcum = jnp.pad(dcr.reshape(Bl, H, S).transpose(0, 2, 1), ((0, 0), (0, 0), (0, LANES - H))).reshape(T, LANES)
    dpf, dfb = _forget_bwd(dcum, pf, fb, Bl, S)
    dp = _pack_dkv(dk, dv, dpf)
    G_kv_full = _mm_tn(xn_kv, dp, "kv")
    G["kv_w", 0] = jnp.stack([jnp.pad(G_kv_full[:, s * kvc:(s + 1) * kvc], ((0, 0), (0, kvp - kvc))) for s in range(N_CHIP)])
    dx3, dg_kv = _mm_nt_pre(dp, kv_full, dx3, x3, kv_g, "kv")

    def halves_of(keys):
        return [G[k].reshape(N_CHIP, 2, G[k].shape[1] // 2, G[k].shape[2]) for k in keys]

    def pair_adds(keys, grads, recvs):
        wires, owns = [], []
        for k, g, r in zip(keys, grads, recvs):
            w, own = _rs_pair_add(g, r, place, f"{k[0]}{k[1]}")
            wires.append(w)
            owns.append(own)
        return wires, owns

    def chip_start(wires, tag, extra=()):
        lands = [lax.empty((3,) + w.shape[1:], w.dtype) for w in wires]
        lands += [jnp.zeros((N_DEV,) + e.shape, e.dtype) for e in extra]
        return _exchange_start(list(wires) + list(extra), lands, _chip_plan, 3 * len(wires) + (N_DEV - 1) * len(extra), tag)

    late = groups[2] + groups[1]
    last = [w_in0]
    grads_2 = halves_of(second)
    p_sems, p_semr, grads_2, sib_2, token = _exchange_start(
        grads_2, [lax.empty((N_CHIP,) + g.shape[2:], g.dtype) for g in grads_2], _pair_plan, len(grads_2), "pair_second")

    dx2, dg_f2pre_0, dg_f2post_0, G["ffn2_w_in", 0], G["ffn2_w_out", 0] = _ffn_bwd(
        dx3, s_f2a, ffn2_pre_g[0], ffn2_post_g[0] + token[0, :1], W["ffn2_w_in", 0], W["ffn2_w_out", 0], "l0f2")
    grads_2, sib_2 = _exchange_wait(p_sems, p_semr, grads_2, sib_2, _pair_plan, dx2, "pair_second")
    wires_2, owns_2 = pair_adds(second, grads_2, sib_2)
    c_2 = chip_start(wires_2, "chip_second")
    dm_c, dg_mixpost_0, dz_c = _post_bwd_mm(dx2, m_c, mix_post_g[0] + c_2[4][0, :1], 1.0, w_o_conv, "conv_out")
    G["conv_w_out", 0] = _mm_tn_out(z_c, dm_c, "conv_out")
    db, dcg, dhh, dk_taps = _conv_bwd(bch, dz_c, k_taps, Bl, S)
    dbch = jnp.concatenate([db, dcg, dhh], axis=1)
    G["conv_w_in", 0] = _mm_tn_in(xn_c, dbch, "conv")
    dx1, dg_mixpre_0 = _mm_nt_pre(dbch, W["conv_w_in", 0], dx2, x1, mix_pre_g[0], "conv")
    def pair_start(keys, tag):
        grads = halves_of(keys)
        lands = [lax.empty((N_CHIP,) + g.shape[2:], g.dtype) for g in grads]
        return _exchange_start(grads, lands, _pair_plan, len(grads), tag)

    p_l = pair_start(late, "pair_late")
    late_done = {}

    def late_leg(dhgu):
        grads_l, sib_l = _exchange_wait(*p_l[:4], _pair_plan, dhgu, "pair_late")
        late_done["wires"], late_done["owns"] = pair_adds(late, grads_l, sib_l)
        late_done["chip"] = chip_start(late_done["wires"], "chip_late")
        return late_done["chip"][4]

    def out_pair(dw_out):
        G[w_out0] = dw_out
        late_done["pair_out"] = pair_start([w_out0], "pair_last_out")
        return late_done["pair_out"][4]

    def last_pair(dw_in):
        grads_o, sib_o = _exchange_wait(*late_done["pair_out"][:4], _pair_plan, dw_in, "pair_last_out")
        late_done["wires_out"], late_done["owns_out"] = pair_adds([w_out0], grads_o, sib_o)
        late_done["chip_out"] = chip_start(late_done["wires_out"], "chip_last_out")
        G[w_in0] = dw_in
        late_done["pair"] = pair_start(last, "pair_last")
        return late_done["pair"][4] + late_done["chip_out"][4]

    dx0, dg_f1pre_0, dg_f1post_0, _, _ = _ffn_bwd(
        dx1, s_f1a, ffn1_pre_g[0], ffn1_post_g[0] + p_l[4][0, :1], W["ffn1_w_in", 0], W["ffn1_w_out", 0], "l0f1",
        between=late_leg, mid=out_pair, finish=last_pair)
    grad_x = dx0.reshape(Bl, S, D)
    owns_l, c_l, p_1 = late_done["owns"], late_done["chip"], late_done["pair"]
    owns_o, c_o = late_done["owns_out"], late_done["chip_out"]

    grads_1, sib_1 = _exchange_wait(*p_1[:4], _pair_plan, dx0, "pair_last")
    wires_1, owns_1 = pair_adds(last, grads_1, sib_1)

    def row(v):
        return jnp.pad(v.reshape(-1), (0, D - v.size)).reshape(1, D)

    gain_parts = [dg_f1pre_0, dg_f1pre_1, dg_f1post_0, dg_f1post_1, dg_mixpre_0, dg_mixpre_1, dg_mixpost_0, dg_mixpost_1,
                  dg_f2pre_0, dg_f2pre_1, dg_f2post_0, dg_f2post_1, dg_kv]
    tail = jnp.concatenate([row(dfb[0, :H]), dk_taps[:3], jnp.full((1, D), loss_local), jnp.zeros((3, D), F32)], axis=0)
    small = _pack_small(gain_parts, tail, 16)
    c_1 = chip_start(wires_1, "chip_last", extra=[small])
    _, recvs_2 = _exchange_wait(*c_2[:4], _chip_plan, c_1[4], "chip_second")
    _, recvs_l = _exchange_wait(*c_l[:4], _chip_plan, c_1[4], "chip_late")
    _, recvs_o = _exchange_wait(*c_o[:4], _chip_plan, c_1[4], "chip_last_out")
    partial = {}

    def chip_adds(keys, owns, recvs):
        for (name, l), own, rcv in zip(keys, owns, recvs):
            partial[name] = _rs_chip_add(own, rcv, place, l, shards[name].shape[0], partial.get(name), f"{name}{l}")

    chip_adds(late + second + [w_out0], owns_l + owns_2 + owns_o, recvs_l + recvs_2 + recvs_o)
    res = {}

    def adamw(names, reduced, after):
        for k, red in zip(names, reduced):
            w, m, v = given[k]
            g2 = red.reshape(-1, red.shape[-1])
            if k == "kv_w":
                g2 = g2[:, :kvc]
            flat = lambda a: a.reshape(-1, a.shape[-1])
            go, d, mn, vn = _adamw(flat(w), g2, flat(m), flat(v), k, after=after)
            res[k] = tuple(a.reshape(w.shape) for a in (go, d, mn, vn))
        return d

    early = [k for k in partial if (k, 0) not in last]
    done = adamw(early, _rs_pair_share([partial[k] for k in early], "early"), c_1[4])
    _, recvs_1 = _exchange_wait(*c_1[:4], _chip_plan, done, "chip_last")
    chip_adds(last, owns_1, recvs_1[:-1])
    rest = [k for k, _ in last]
    adamw(rest, _rs_pair_share([partial[k] for k in rest], "last"), None)
    gsum = _sum_devices(recvs_1[-1], small, place)
    loss = gsum[20, 0]

    small_names = ["ffn1_pre_g", "ffn1_post_g", "mix_pre_g", "mix_post_g", "ffn2_pre_g", "ffn2_post_g"]
    small_given = dict(ffn1_pre_g=(ffn1_pre_g, m_ffn1_pre_g, v_ffn1_pre_g), ffn1_post_g=(ffn1_post_g, m_ffn1_post_g, v_ffn1_post_g),
                       mix_pre_g=(mix_pre_g, m_mix_pre_g, v_mix_pre_g), mix_post_g=(mix_post_g, m_mix_post_g, v_mix_post_g),
                       ffn2_pre_g=(ffn2_pre_g, m_ffn2_pre_g, v_ffn2_pre_g), ffn2_post_g=(ffn2_post_g, m_ffn2_post_g, v_ffn2_post_g))

    def pack(idx):
        rows_ = [small_given[k][idx] for k in small_names]
        rows_ += [row((kv_g, m_kv_g, v_kv_g)[idx]), jnp.zeros((3, D), F32), row((forget_b, m_forget_b, v_forget_b)[idx])]
        rows_.append(jnp.pad((conv_k, m_conv_k, v_conv_k)[idx][0], ((0, 0), (0, D - dk_cols))))
        a = jnp.concatenate(rows_, axis=0)
        return jnp.pad(a, ((0, SMALL_ROWS - a.shape[0]), (0, 0)))

    g_taps = lax.dynamic_slice_in_dim(gsum[17:20], chip * dk_cols, dk_cols, axis=1)
    g_small = jnp.concatenate([gsum[:17], jnp.pad(g_taps, ((0, 0), (0, D - dk_cols))), gsum[20:]], axis=0)
    g_small, d_s, m_s, v_s = _adamw(pack(0), g_small, pack(1), pack(2), "small")
    for i, k in enumerate(small_names):
        res[k] = tuple(a[2 * i:2 * i + 2] for a in (g_small, d_s, m_s, v_s))
    res["kv_g"] = tuple(a[12] for a in (g_small, d_s, m_s, v_s))
    res["forget_b"] = tuple(a[16, :H] for a in (g_small, d_s, m_s, v_s))
    res["conv_k"] = tuple(a[17:20, :dk_cols][None] for a in (g_small, d_s, m_s, v_s))

    order = ["ffn1_pre_g", "ffn1_post_g", "ffn1_w_in", "ffn1_w_out", "mix_pre_g", "mix_post_g", "ffn2_pre_g", "ffn2_post_g",
             "ffn2_w_in", "ffn2_w_out", "conv_w_in", "conv_k", "conv_w_out", "kv_g", "kv_w", "forget_b", "attn_w_qg", "attn_w_o"]
    out = [loss, grad_x]
    for idx in range(4):
        out += [res[k][idx] for k in order]
    return tuple(out)
```
